```python
import math
import jax, jax.numpy as jnp
from jax import lax
import numpy as np

D_MODEL = 1024
BATCH = 8
SEQ = 8192
DEPTH = 4

N_MIXERS = 2
CONV_CHANNELS = D_MODEL
CONV_WIDTH = 31
HEAD_DIM = 64
HEADS_PER_GROUP = 4
DILATION_PAIRS = ((128, 1), (512, 4), (2048, 16))
N_GROUPS = len(DILATION_PAIRS)
N_HEADS = N_GROUPS * HEADS_PER_GROUP
D_ATTN = N_HEADS * HEAD_DIM
N_BUCKETS = 32
REL_MAX_DISTANCE = 2048
D_FF = -(-8 * D_MODEL // (3 * 256)) * 256
EPS = 1e-6
N_CONV_LAYERS = (DEPTH + 1) // 2
N_ATTN_LAYERS = DEPTH // 2
NEG_INF = -1e30

kernel_name = "hybrid_conv_dilated_attn_trunk"


def rmsnorm(x, g):
    x32 = x.astype(jnp.float32)
    y = x32 * lax.rsqrt(jnp.mean(x32 * x32, axis=-1, keepdims=True) + EPS)
    return (y * g.astype(jnp.float32)).astype(x.dtype)


def layernorm(x, g, b):
    x32 = x.astype(jnp.float32)
    mu = jnp.mean(x32, axis=-1, keepdims=True)
    xc = x32 - mu
    var = jnp.mean(xc * xc, axis=-1, keepdims=True)
    y = xc * lax.rsqrt(var + EPS) * g.astype(jnp.float32) + b.astype(jnp.float32)
    return y.astype(x.dtype)


def t5_bucket(dist):
    max_exact = N_BUCKETS // 2
    n = jnp.maximum(dist, 0)
    nf = jnp.maximum(n, 1).astype(jnp.float32)
    large = max_exact + (jnp.log(nf / max_exact) / math.log(REL_MAX_DISTANCE / max_exact)
                         * (N_BUCKETS - max_exact)).astype(jnp.int32)
    large = jnp.minimum(large, N_BUCKETS - 1)
    return jnp.where(n < max_exact, n, large)


def conformer_conv(h, w_pw1, b_pw1, w_dw, b_dw, ln_g, ln_b, w_pw2, b_pw2):
    u = h @ w_pw1 + b_pw1
    a, gate = jnp.split(u, 2, axis=-1)
    u = a * jax.nn.sigmoid(gate)
    u = lax.conv_general_dilated(
        u, w_dw[:, None, :], window_strides=(1,), padding=[(CONV_WIDTH - 1, 0)],
        dimension_numbers=("NWC", "WIO", "NWC"), feature_group_count=CONV_CHANNELS) + b_dw
    u = jax.nn.silu(layernorm(u, ln_g, ln_b))
    return u @ w_pw2 + b_pw2


def dilated_group(q, k, v, bias_table, window, dilation):
    b_, s, h, hd = q.shape
    n_back = window // dilation
    seg = n_back * dilation
    s_pad = -(-s // seg) * seg
    nb = s_pad // seg
    pad = ((0, 0), (0, s_pad - s), (0, 0), (0, 0))

    def blocks(t):
        return jnp.pad(t, pad).reshape(b_, nb, n_back, dilation, h, hd)

    def with_prev(t):
        prev = jnp.pad(t[:, :-1], ((0, 0), (1, 0), (0, 0), (0, 0), (0, 0), (0, 0)))
        return jnp.concatenate([prev, t], axis=2)

    qb = blocks(q)
    kk = with_prev(blocks(k))
    vv = with_prev(blocks(v))

    i_idx = jnp.arange(n_back)[:, None]
    j_idx = jnp.arange(2 * n_back)[None, :]
    dist = i_idx + n_back - j_idx
    bias = jnp.transpose(bias_table[t5_bucket(dist * dilation)], (2, 0, 1)).astype(jnp.float32)
    valid = (dist >= 0) & (dist <= n_back)
    not_before_start = (jnp.arange(nb)[:, None, None] > 0) | (j_idx[None] >= n_back)
    mask = valid[None] & not_before_start

    logits = jnp.einsum("bnidhc,bnjdhc->bndhij", qb, kk) * (HEAD_DIM ** -0.5) + bias
    logits = jnp.where(mask[None, :, None, None], logits, NEG_INF)
    m = jnp.max(logits, axis=-1, keepdims=True)
    p = jnp.exp(logits - m)
    den = jnp.sum(p, axis=-1)
    o = jnp.einsum("bndhij,bnjdhc->bnidhc", p, vv)
    den_t = jnp.transpose(den, (0, 1, 4, 2, 3))
    lse_t = jnp.transpose(m[..., 0] + jnp.log(den), (0, 1, 4, 2, 3))
    o = (o / den_t[..., None]).reshape(b_, s_pad, h, hd)[:, :s]
    lse = lse_t.reshape(b_, s_pad, h)[:, :s]
    return o, lse


def dilated_attention(h, w_qkv, w_o, rel_bias):
    b_, s, _ = h.shape
    qkv = (h @ w_qkv).astype(jnp.float32).reshape(b_, s, 3, N_HEADS, HEAD_DIM)
    outs, lses = [], []
    for g, (window, dilation) in enumerate(DILATION_PAIRS):
        hs = slice(g * HEADS_PER_GROUP, (g + 1) * HEADS_PER_GROUP)
        o, l = dilated_group(qkv[:, :, 0, hs], qkv[:, :, 1, hs], qkv[:, :, 2, hs],
                             rel_bias[:, hs], window, dilation)
        outs.append(o)
        lses.append(l)
    alpha = jax.nn.softmax(jnp.stack(lses, axis=0), axis=0)
    o = jnp.concatenate([outs[g] * alpha[g][..., None] for g in range(N_GROUPS)], axis=2)
    return o.reshape(b_, s, D_ATTN).astype(h.dtype) @ w_o


def swiglu(h, w_gate, w_up, w_down):
    return (jax.nn.silu(h @ w_gate) * (h @ w_up)) @ w_down


def _fwd_setup_inputs(seed: int = 0) -> dict:
    key = jax.random.key(seed)
    ks = jax.random.split(key, 20)
    f32 = jnp.float32

    def nrm(k, shape, scale):
        return jax.random.normal(k, shape, f32) * scale

    return {
        "x": nrm(ks[0], (BATCH, SEQ, D_MODEL), 1.0),
        "norm_mix": 1.0 + nrm(ks[1], (DEPTH, D_MODEL), 0.05),
        "norm_ffn": 1.0 + nrm(ks[2], (DEPTH, D_MODEL), 0.05),
        "final_norm": 1.0 + nrm(ks[3], (D_MODEL,), 0.05),
        "conv_w_pw1": nrm(ks[4], (N_CONV_LAYERS, D_MODEL, 2 * CONV_CHANNELS), D_MODEL ** -0.5),
        "conv_b_pw1": nrm(ks[5], (N_CONV_LAYERS, 2 * CONV_CHANNELS), 0.01),
        "conv_w_dw": nrm(ks[6], (N_CONV_LAYERS, CONV_WIDTH, CONV_CHANNELS), CONV_WIDTH ** -0.5),
        "conv_b_dw": nrm(ks[7], (N_CONV_LAYERS, CONV_CHANNELS), 0.01),
        "conv_ln_g": 1.0 + nrm(ks[8], (N_CONV_LAYERS, CONV_CHANNELS), 0.05),
        "conv_ln_b": nrm(ks[9], (N_CONV_LAYERS, CONV_CHANNELS), 0.01),
        "conv_w_pw2": nrm(ks[10], (N_CONV_LAYERS, CONV_CHANNELS, D_MODEL), CONV_CHANNELS ** -0.5),
        "conv_b_pw2": nrm(ks[11], (N_CONV_LAYERS, D_MODEL), 0.01),
        "attn_w_qkv": nrm(ks[12], (N_ATTN_LAYERS, D_MODEL, 3 * D_ATTN), D_MODEL ** -0.5),
        "attn_w_o": nrm(ks[13], (N_ATTN_LAYERS, D_ATTN, D_MODEL), D_ATTN ** -0.5),
        "rel_bias": nrm(ks[14], (N_BUCKETS, N_HEADS), 0.5),
        "ffn_w_gate": nrm(ks[15], (DEPTH, D_MODEL, D_FF), D_MODEL ** -0.5),
        "ffn_w_up": nrm(ks[16], (DEPTH, D_MODEL, D_FF), D_MODEL ** -0.5),
        "ffn_w_down": nrm(ks[17], (DEPTH, D_FF, D_MODEL), D_FF ** -0.5),
    }


def _fwd_reference(x, norm_mix, norm_ffn, final_norm, conv_w_pw1, conv_b_pw1, conv_w_dw, conv_b_dw,
              conv_ln_g, conv_ln_b, conv_w_pw2, conv_b_pw2, attn_w_qkv, attn_w_o, rel_bias,
              ffn_w_gate, ffn_w_up, ffn_w_down):
    for i in range(DEPTH):
        h = rmsnorm(x, norm_mix[i])
        j = i // N_MIXERS
        if i % N_MIXERS == 0:
            x = x + conformer_conv(h, conv_w_pw1[j], conv_b_pw1[j], conv_w_dw[j], conv_b_dw[j],
                                   conv_ln_g[j], conv_ln_b[j], conv_w_pw2[j], conv_b_pw2[j])
        else:
            x = x + dilated_attention(h, attn_w_qkv[j], attn_w_o[j], rel_bias)
        h = rmsnorm(x, norm_ffn[i])
        x = x + swiglu(h, ffn_w_gate[i], ffn_w_up[i], ffn_w_down[i])
    return rmsnorm(x, final_norm)


import jax as _jax
import jax.numpy as _jnp

TWIN_FORMAT = 'train_step'
FWD_PARAMS = ['x', 'norm_mix', 'norm_ffn', 'final_norm', 'conv_w_pw1', 'conv_b_pw1', 'conv_w_dw', 'conv_b_dw', 'conv_ln_g', 'conv_ln_b', 'conv_w_pw2', 'conv_b_pw2', 'attn_w_qkv', 'attn_w_o', 'rel_bias', 'ffn_w_gate', 'ffn_w_up', 'ffn_w_down']
TWIN_WEIGHTS = ['norm_mix', 'norm_ffn', 'final_norm', 'conv_w_pw1', 'conv_b_pw1', 'conv_w_dw', 'conv_b_dw', 'conv_ln_g', 'conv_ln_b', 'conv_w_pw2', 'conv_b_pw2', 'attn_w_qkv', 'attn_w_o', 'rel_bias', 'ffn_w_gate', 'ffn_w_up', 'ffn_w_down']
TWIN_DIFF_INPUT = 'x'
TWIN_INPUTS = ['x', 'norm_mix', 'norm_ffn', 'final_norm', 'conv_w_pw1', 'conv_b_pw1', 'conv_w_dw', 'conv_b_dw', 'conv_ln_g', 'conv_ln_b', 'conv_w_pw2', 'conv_b_pw2', 'attn_w_qkv', 'attn_w_o', 'rel_bias', 'ffn_w_gate', 'ffn_w_up', 'ffn_w_down', 'loss_target', 'm_norm_mix', 'm_norm_ffn', 'm_final_norm', 'm_conv_w_pw1', 'm_conv_b_pw1', 'm_conv_w_dw', 'm_conv_b_dw', 'm_conv_ln_g', 'm_conv_ln_b', 'm_conv_w_pw2', 'm_conv_b_pw2', 'm_attn_w_qkv', 'm_attn_w_o', 'm_rel_bias', 'm_ffn_w_gate', 'm_ffn_w_up', 'm_ffn_w_down', 'v_norm_mix', 'v_norm_ffn', 'v_final_norm', 'v_conv_w_pw1', 'v_conv_b_pw1', 'v_conv_w_dw', 'v_conv_b_dw', 'v_conv_ln_g', 'v_conv_ln_b', 'v_conv_w_pw2', 'v_conv_b_pw2', 'v_attn_w_qkv', 'v_attn_w_o', 'v_rel_bias', 'v_ffn_w_gate', 'v_ffn_w_up', 'v_ffn_w_down']
TWIN_OUTPUTS = ['loss', 'grad_x', 'grad_norm_mix', 'grad_norm_ffn', 'grad_final_norm', 'grad_conv_w_pw1', 'grad_conv_b_pw1', 'grad_conv_w_dw', 'grad_conv_b_dw', 'grad_conv_ln_g', 'grad_conv_ln_b', 'grad_conv_w_pw2', 'grad_conv_b_pw2', 'grad_attn_w_qkv', 'grad_attn_w_o', 'grad_rel_bias', 'grad_ffn_w_gate', 'grad_ffn_w_up', 'grad_ffn_w_down', 'delta_norm_mix', 'delta_norm_ffn', 'delta_final_norm', 'delta_conv_w_pw1', 'delta_conv_b_pw1', 'delta_conv_w_dw', 'delta_conv_b_dw', 'delta_conv_ln_g', 'delta_conv_ln_b', 'delta_conv_w_pw2', 'delta_conv_b_pw2', 'delta_attn_w_qkv', 'delta_attn_w_o', 'delta_rel_bias', 'delta_ffn_w_gate', 'delta_ffn_w_up', 'delta_ffn_w_down', 'new_m_norm_mix', 'new_m_norm_ffn', 'new_m_final_norm', 'new_m_conv_w_pw1', 'new_m_conv_b_pw1', 'new_m_conv_w_dw', 'new_m_conv_b_dw', 'new_m_conv_ln_g', 'new_m_conv_ln_b', 'new_m_conv_w_pw2', 'new_m_conv_b_pw2', 'new_m_attn_w_qkv', 'new_m_attn_w_o', 'new_m_rel_bias', 'new_m_ffn_w_gate', 'new_m_ffn_w_up', 'new_m_ffn_w_down', 'new_v_norm_mix', 'new_v_norm_ffn', 'new_v_final_norm', 'new_v_conv_w_pw1', 'new_v_conv_b_pw1', 'new_v_conv_w_dw', 'new_v_conv_b_dw', 'new_v_conv_ln_g', 'new_v_conv_ln_b', 'new_v_conv_w_pw2', 'new_v_conv_b_pw2', 'new_v_attn_w_qkv', 'new_v_attn_w_o', 'new_v_rel_bias', 'new_v_ffn_w_gate', 'new_v_ffn_w_up', 'new_v_ffn_w_down']
TWIN_LEAF_KINDS = {'loss': 'loss', 'grad_x': 'grad_x', 'grad_norm_mix': 'grad_w', 'grad_norm_ffn': 'grad_w', 'grad_final_norm': 'grad_w', 'grad_conv_w_pw1': 'grad_w', 'grad_conv_b_pw1': 'grad_w', 'grad_conv_w_dw': 'grad_w', 'grad_conv_b_dw': 'grad_w', 'grad_conv_ln_g': 'grad_w', 'grad_conv_ln_b': 'grad_w', 'grad_conv_w_pw2': 'grad_w', 'grad_conv_b_pw2': 'grad_w', 'grad_attn_w_qkv': 'grad_w', 'grad_attn_w_o': 'grad_w', 'grad_rel_bias': 'grad_w', 'grad_ffn_w_gate': 'grad_w', 'grad_ffn_w_up': 'grad_w', 'grad_ffn_w_down': 'grad_w', 'delta_norm_mix': 'delta_w', 'delta_norm_ffn': 'delta_w', 'delta_final_norm': 'delta_w', 'delta_conv_w_pw1': 'delta_w', 'delta_conv_b_pw1': 'delta_w', 'delta_conv_w_dw': 'delta_w', 'delta_conv_b_dw': 'delta_w', 'delta_conv_ln_g': 'delta_w', 'delta_conv_ln_b': 'delta_w', 'delta_conv_w_pw2': 'delta_w', 'delta_conv_b_pw2': 'delta_w', 'delta_attn_w_qkv': 'delta_w', 'delta_attn_w_o': 'delta_w', 'delta_rel_bias': 'delta_w', 'delta_ffn_w_gate': 'delta_w', 'delta_ffn_w_up': 'delta_w', 'delta_ffn_w_down': 'delta_w', 'new_m_norm_mix': 'new_m', 'new_m_norm_ffn': 'new_m', 'new_m_final_norm': 'new_m', 'new_m_conv_w_pw1': 'new_m', 'new_m_conv_b_pw1': 'new_m', 'new_m_conv_w_dw': 'new_m', 'new_m_conv_b_dw': 'new_m', 'new_m_conv_ln_g': 'new_m', 'new_m_conv_ln_b': 'new_m', 'new_m_conv_w_pw2': 'new_m', 'new_m_conv_b_pw2': 'new_m', 'new_m_attn_w_qkv': 'new_m', 'new_m_attn_w_o': 'new_m', 'new_m_rel_bias': 'new_m', 'new_m_ffn_w_gate': 'new_m', 'new_m_ffn_w_up': 'new_m', 'new_m_ffn_w_down': 'new_m', 'new_v_norm_mix': 'new_v', 'new_v_norm_ffn': 'new_v', 'new_v_final_norm': 'new_v', 'new_v_conv_w_pw1': 'new_v', 'new_v_conv_b_pw1': 'new_v', 'new_v_conv_w_dw': 'new_v', 'new_v_conv_b_dw': 'new_v', 'new_v_conv_ln_g': 'new_v', 'new_v_conv_ln_b': 'new_v', 'new_v_conv_w_pw2': 'new_v', 'new_v_conv_b_pw2': 'new_v', 'new_v_attn_w_qkv': 'new_v', 'new_v_attn_w_o': 'new_v', 'new_v_rel_bias': 'new_v', 'new_v_ffn_w_gate': 'new_v', 'new_v_ffn_w_up': 'new_v', 'new_v_ffn_w_down': 'new_v'}


def _forward(args):
    return _fwd_reference(*[args[k] for k in FWD_PARAMS])


def _output_shape():
    def fwd():
        inp = _fwd_setup_inputs(0)
        return _fwd_reference(*[inp[k] for k in FWD_PARAMS])
    out = _jax.eval_shape(fwd)
    return out.shape, out.dtype

N_MICROBATCH = 1
ADAM_LR = 0.001
ADAM_B1 = 0.9
ADAM_B2 = 0.999
ADAM_EPS = 1e-08
ADAM_WD = 0.01
ADAM_STEP = 10
PER_EXAMPLE_BATCH_AXIS = {'x': 0, 'loss_target': 0}
SHARED_INPUTS = []
_WEIGHT_DTYPES = {'norm_mix': _jnp.float32, 'norm_ffn': _jnp.float32, 'final_norm': _jnp.float32, 'conv_w_pw1': _jnp.float32, 'conv_b_pw1': _jnp.float32, 'conv_w_dw': _jnp.float32, 'conv_b_dw': _jnp.float32, 'conv_ln_g': _jnp.float32, 'conv_ln_b': _jnp.float32, 'conv_w_pw2': _jnp.float32, 'conv_b_pw2': _jnp.float32, 'attn_w_qkv': _jnp.float32, 'attn_w_o': _jnp.float32, 'rel_bias': _jnp.float32, 'ffn_w_gate': _jnp.float32, 'ffn_w_up': _jnp.float32, 'ffn_w_down': _jnp.float32}
MOMENT_SCALE = {'norm_mix': 1.233065e-01, 'norm_ffn': 1.642803e-01, 'final_norm': 6.409123e+01, 'conv_w_pw1': 1.178465e-01, 'conv_b_pw1': 1.902376e-01, 'conv_w_dw': 1.578149e-01, 'conv_b_dw': 4.429983e-01, 'conv_ln_g': 2.387677e-01, 'conv_ln_b': 2.867572e-01, 'conv_w_pw2': 1.734732e-01, 'conv_b_pw2': 5.469935e-01, 'attn_w_qkv': 2.823708e-02, 'attn_w_o': 2.948600e-02, 'rel_bias': 4.265460e-02, 'ffn_w_gate': 7.081217e-02, 'ffn_w_up': 6.974525e-02, 'ffn_w_down': 1.155410e-01}


def _to_microbatches(a, axis):
    t = _jnp.moveaxis(a, axis, 0)
    t = t.reshape((N_MICROBATCH, t.shape[0] // N_MICROBATCH) + t.shape[1:])
    return _jnp.moveaxis(t, 1, axis + 1)


def setup_inputs(seed: int = 0) -> dict:
    inp = _fwd_setup_inputs(seed)
    key = _jax.random.fold_in(_jax.random.key(seed), 7919)
    shape, _ = _output_shape()
    out = dict(inp)
    out["loss_target"] = _jax.random.normal(_jax.random.fold_in(key, 0), shape, _jnp.float32)
    for i, name in enumerate(TWIN_WEIGHTS):
        w = inp[name].astype(_jnp.float32)
        if MOMENT_SCALE is None:
            s = _jnp.sqrt(_jnp.mean(_jnp.square(w)) + 1e-30)
        else:
            s = MOMENT_SCALE[name]
        km, kv = _jax.random.split(_jax.random.fold_in(key, i + 1))
        out[name] = w
        out["m_" + name] = s * _jax.random.normal(km, w.shape, _jnp.float32)
        out["v_" + name] = (s * s) * _jax.random.uniform(kv, w.shape, _jnp.float32, 0.5, 1.5)
    if N_MICROBATCH > 1:
        for name, axis in PER_EXAMPLE_BATCH_AXIS.items():
            out[name] = _to_microbatches(out[name], axis)
    return {'x': out['x'], 'norm_mix': out['norm_mix'], 'norm_ffn': out['norm_ffn'], 'final_norm': out['final_norm'], 'conv_w_pw1': out['conv_w_pw1'], 'conv_b_pw1': out['conv_b_pw1'], 'conv_w_dw': out['conv_w_dw'], 'conv_b_dw': out['conv_b_dw'], 'conv_ln_g': out['conv_ln_g'], 'conv_ln_b': out['conv_ln_b'], 'conv_w_pw2': out['conv_w_pw2'], 'conv_b_pw2': out['conv_b_pw2'], 'attn_w_qkv': out['attn_w_qkv'], 'attn_w_o': out['attn_w_o'], 'rel_bias': out['rel_bias'], 'ffn_w_gate': out['ffn_w_gate'], 'ffn_w_up': out['ffn_w_up'], 'ffn_w_down': out['ffn_w_down'], 'loss_target': out['loss_target'], 'm_norm_mix': out['m_norm_mix'], 'm_norm_ffn': out['m_norm_ffn'], 'm_final_norm': out['m_final_norm'], 'm_conv_w_pw1': out['m_conv_w_pw1'], 'm_conv_b_pw1': out['m_conv_b_pw1'], 'm_conv_w_dw': out['m_conv_w_dw'], 'm_conv_b_dw': out['m_conv_b_dw'], 'm_conv_ln_g': out['m_conv_ln_g'], 'm_conv_ln_b': out['m_conv_ln_b'], 'm_conv_w_pw2': out['m_conv_w_pw2'], 'm_conv_b_pw2': out['m_conv_b_pw2'], 'm_attn_w_qkv': out['m_attn_w_qkv'], 'm_attn_w_o': out['m_attn_w_o'], 'm_rel_bias': out['m_rel_bias'], 'm_ffn_w_gate': out['m_ffn_w_gate'], 'm_ffn_w_up': out['m_ffn_w_up'], 'm_ffn_w_down': out['m_ffn_w_down'], 'v_norm_mix': out['v_norm_mix'], 'v_norm_ffn': out['v_norm_ffn'], 'v_final_norm': out['v_final_norm'], 'v_conv_w_pw1': out['v_conv_w_pw1'], 'v_conv_b_pw1': out['v_conv_b_pw1'], 'v_conv_w_dw': out['v_conv_w_dw'], 'v_conv_b_dw': out['v_conv_b_dw'], 'v_conv_ln_g': out['v_conv_ln_g'], 'v_conv_ln_b': out['v_conv_ln_b'], 'v_conv_w_pw2': out['v_conv_w_pw2'], 'v_conv_b_pw2': out['v_conv_b_pw2'], 'v_attn_w_qkv': out['v_attn_w_qkv'], 'v_attn_w_o': out['v_attn_w_o'], 'v_rel_bias': out['v_rel_bias'], 'v_ffn_w_gate': out['v_ffn_w_gate'], 'v_ffn_w_up': out['v_ffn_w_up'], 'v_ffn_w_down': out['v_ffn_w_down']}


def _loss(weights, diff, rest, loss_target):
    with _jax.named_scope("forward"):
        args = {**rest, TWIN_DIFF_INPUT: diff, **{k: w.astype(_WEIGHT_DTYPES[k]) for k, w in weights.items()}}
        y = _forward(args)
    with _jax.named_scope("loss_head"):
        err = _jnp.square(y.astype(_jnp.float32) - loss_target)
        return 0.5 * _jnp.sum(_jnp.mean(err, axis=-1)) if err.ndim else 0.5 * err


def _adamw(w, g, m, v):
    m = ADAM_B1 * m + (1.0 - ADAM_B1) * g
    v = ADAM_B2 * v + (1.0 - ADAM_B2) * _jnp.square(g)
    m_hat = m / (1.0 - ADAM_B1 ** ADAM_STEP)
    v_hat = v / (1.0 - ADAM_B2 ** ADAM_STEP)
    delta = -ADAM_LR * (m_hat / (_jnp.sqrt(v_hat) + ADAM_EPS) + ADAM_WD * w)
    return delta, m, v


def reference(x, norm_mix, norm_ffn, final_norm, conv_w_pw1, conv_b_pw1, conv_w_dw, conv_b_dw, conv_ln_g, conv_ln_b, conv_w_pw2, conv_b_pw2, attn_w_qkv, attn_w_o, rel_bias, ffn_w_gate, ffn_w_up, ffn_w_down, loss_target, m_norm_mix, m_norm_ffn, m_final_norm, m_conv_w_pw1, m_conv_b_pw1, m_conv_w_dw, m_conv_b_dw, m_conv_ln_g, m_conv_ln_b, m_conv_w_pw2, m_conv_b_pw2, m_attn_w_qkv, m_attn_w_o, m_rel_bias, m_ffn_w_gate, m_ffn_w_up, m_ffn_w_down, v_norm_mix, v_norm_ffn, v_final_norm, v_conv_w_pw1, v_conv_b_pw1, v_conv_w_dw, v_conv_b_dw, v_conv_ln_g, v_conv_ln_b, v_conv_w_pw2, v_conv_b_pw2, v_attn_w_qkv, v_attn_w_o, v_rel_bias, v_ffn_w_gate, v_ffn_w_up, v_ffn_w_down):
    given = dict(x=x, norm_mix=norm_mix, norm_ffn=norm_ffn, final_norm=final_norm, conv_w_pw1=conv_w_pw1, conv_b_pw1=conv_b_pw1, conv_w_dw=conv_w_dw, conv_b_dw=conv_b_dw, conv_ln_g=conv_ln_g, conv_ln_b=conv_ln_b, conv_w_pw2=conv_w_pw2, conv_b_pw2=conv_b_pw2, attn_w_qkv=attn_w_qkv, attn_w_o=attn_w_o, rel_bias=rel_bias, ffn_w_gate=ffn_w_gate, ffn_w_up=ffn_w_up, ffn_w_down=ffn_w_down, loss_target=loss_target, m_norm_mix=m_norm_mix, m_norm_ffn=m_norm_ffn, m_final_norm=m_final_norm, m_conv_w_pw1=m_conv_w_pw1, m_conv_b_pw1=m_conv_b_pw1, m_conv_w_dw=m_conv_w_dw, m_conv_b_dw=m_conv_b_dw, m_conv_ln_g=m_conv_ln_g, m_conv_ln_b=m_conv_ln_b, m_conv_w_pw2=m_conv_w_pw2, m_conv_b_pw2=m_conv_b_pw2, m_attn_w_qkv=m_attn_w_qkv, m_attn_w_o=m_attn_w_o, m_rel_bias=m_rel_bias, m_ffn_w_gate=m_ffn_w_gate, m_ffn_w_up=m_ffn_w_up, m_ffn_w_down=m_ffn_w_down, v_norm_mix=v_norm_mix, v_norm_ffn=v_norm_ffn, v_final_norm=v_final_norm, v_conv_w_pw1=v_conv_w_pw1, v_conv_b_pw1=v_conv_b_pw1, v_conv_w_dw=v_conv_w_dw, v_conv_b_dw=v_conv_b_dw, v_conv_ln_g=v_conv_ln_g, v_conv_ln_b=v_conv_ln_b, v_conv_w_pw2=v_conv_w_pw2, v_conv_b_pw2=v_conv_b_pw2, v_attn_w_qkv=v_attn_w_qkv, v_attn_w_o=v_attn_w_o, v_rel_bias=v_rel_bias, v_ffn_w_gate=v_ffn_w_gate, v_ffn_w_up=v_ffn_w_up, v_ffn_w_down=v_ffn_w_down)
    weights = {n: given[n] for n in TWIN_WEIGHTS}
    shared = {n: given[n] for n in SHARED_INPUTS}
    per_example = {n: given[n] for n in ['x']}
    grad_fn = _jax.value_and_grad(_loss, argnums=(0, 1))

    def one_microbatch(ex, loss_target):
        ex = dict(ex)
        diff = ex.pop(TWIN_DIFF_INPUT)
        return grad_fn(weights, diff, {**shared, **ex}, loss_target)

    if N_MICROBATCH == 1:
        loss, (grad_w, grad_x) = one_microbatch(per_example, given["loss_target"])
    else:
        def body(carry, xs):
            loss_sum, grad_sum = carry
            l_k, (gw_k, gx_k) = one_microbatch(xs[0], xs[1])
            with _jax.named_scope("update"):
                return (loss_sum + l_k, _jax.tree.map(_jnp.add, grad_sum, gw_k)), gx_k

        init = (_jnp.zeros((), _jnp.float32), _jax.tree.map(_jnp.zeros_like, weights))
        (loss, grad_w), grad_x = _jax.lax.scan(body, init, (per_example, given["loss_target"]))
    with _jax.named_scope("update"):
        delta_w, new_m, new_v = {}, {}, {}
        for n in TWIN_WEIGHTS:
            delta_w[n], new_m[n], new_v[n] = _adamw(weights[n], grad_w[n], given["m_" + n], given["v_" + n])
    return (loss, grad_x, *[grad_w[n] for n in TWIN_WEIGHTS], *[delta_w[n] for n in TWIN_WEIGHTS],
            *[new_m[n] for n in TWIN_WEIGHTS], *[new_v[n] for n in TWIN_WEIGHTS])
```

```python
import functools
import math

import numpy as np
import jax
import jax.numpy as jnp
from jax import lax
from jax.experimental import pallas as pl
from jax.experimental.pallas import tpu as pltpu

F32 = jnp.float32
BF16 = jnp.bfloat16

N_DEV = 8
HEAD_DIM = 64
HEADS_PER_GROUP = 4
GROUP_COLS = HEADS_PER_GROUP * HEAD_DIM
DILATIONS = (1, 4, 16)
N_BACK = 128
N_GROUPS = 3
N_HEADS = 12
D_ATTN = 768
N_BUCKETS = 32
REL_MAX_DISTANCE = 2048
CONV_WIDTH = 31
CONV_HALO = 32
EPS = 1e-6
NEG_INF = -1e30
ADAM_LR, ADAM_B1, ADAM_B2, ADAM_EPS, ADAM_WD, ADAM_STEP = 0.001, 0.9, 0.999, 1e-08, 0.01, 10
V7X_VMEM_LIMIT_BYTES = 56 * 1024 * 1024
MESH = pl.DeviceIdType.MESH
ANY = pl.BlockSpec(memory_space=pl.ANY)


def _pick(n, prefs):
    for p in prefs:
        if n % p == 0:
            return p
    return n


def _params(sem, vmem=None):
    return pltpu.CompilerParams(dimension_semantics=sem, vmem_limit_bytes=vmem)


def _sigmoid(x):
    return 1.0 / (1.0 + jnp.exp(-x))


def _exchange(name, groups, scatter):
    n_arr = [len(arrs) for arrs, _ in groups]
    n_in = sum(n_arr) + len(groups)
    ng = len(groups)

    def body(*refs):
        ins, outs, (send_sems, recv_sems, local_sems) = refs[:n_in], refs[n_in:-3], refs[-3:]
        x, y, c = lax.axis_index("x"), lax.axis_index("y"), lax.axis_index("c")
        me = 4 * x + 2 * y + c
        pos_in = pos_out = 0
        plans = []
        for gi in range(ng):
            srcs = ins[pos_in:pos_in + n_arr[gi]]
            like = ins[pos_in + n_arr[gi]]
            dsts = outs[pos_out:pos_out + n_arr[gi]]
            pos_in += n_arr[gi] + 1
            pos_out += n_arr[gi]
            plans.append((gi, srcs, like, dsts))
        local = []
        for gi, srcs, like, dsts in plans:
            for s, d in zip(srcs, dsts):
                cp = pltpu.make_async_copy(s.at[me] if scatter else s, d.at[me], local_sems.at[gi])
                cp.start()
                local.append(cp)
        for delta in range(1, N_DEV):
            dx, dy, dc = (delta >> 2) & 1, (delta >> 1) & 1, delta & 1
            px, py, pc = (1 - x if dx else x), (1 - y if dy else y), (1 - c if dc else c)
            peer = 4 * px + 2 * py + pc
            for gi, srcs, like, dsts in plans:
                for s, d in zip(srcs, dsts):
                    pltpu.make_async_remote_copy(
                        src_ref=s.at[peer] if scatter else s, dst_ref=d.at[me],
                        send_sem=send_sems.at[gi, delta - 1], recv_sem=recv_sems.at[gi, delta - 1],
                        device_id=(px, py, pc), device_id_type=MESH).start()
        for delta in range(1, N_DEV):
            for gi, srcs, like, dsts in plans:
                pltpu.make_async_remote_copy(
                    src_ref=like, dst_ref=like, send_sem=send_sems.at[gi, delta - 1],
                    recv_sem=recv_sems.at[gi, delta - 1], device_id=(x, y, c), device_id_type=MESH).wait()
        for cp in local:
            cp.wait()

    operands, out_shape = [], []
    for arrs, like in groups:
        operands += list(arrs) + [like]
        for a in arrs:
            blk = a.shape[1:] if scatter else a.shape
            out_shape.append(jax.ShapeDtypeStruct((N_DEV,) + tuple(blk), a.dtype))
    outs = pl.pallas_call(
        body, name=name, out_shape=out_shape, in_specs=[ANY] * len(operands), out_specs=[ANY] * len(out_shape),
        scratch_shapes=[pltpu.SemaphoreType.DMA((ng, N_DEV - 1)), pltpu.SemaphoreType.DMA((ng, N_DEV - 1)),
                        pltpu.SemaphoreType.DMA((ng,))],
        compiler_params=pltpu.CompilerParams(has_side_effects=True),
    )(*operands)
    res, pos = [], 0
    for n in n_arr:
        res.append(list(outs[pos:pos + n]))
        pos += n
    return res


def _sum8(name, r):
    _, rows, cols = r.shape
    tr = _pick(rows, (256, 128, 64, 32, 16, 8))

    def body(r_ref, o_ref):
        acc = r_ref[0].astype(F32)
        for p in range(1, N_DEV):
            acc = acc + r_ref[p].astype(F32)
        o_ref[...] = acc

    return pl.pallas_call(
        body, name=name, out_shape=jax.ShapeDtypeStruct((rows, cols), F32), grid=(rows // tr,),
        in_specs=[pl.BlockSpec((N_DEV, tr, cols), lambda i: (0, i, 0))],
        out_specs=pl.BlockSpec((tr, cols), lambda i: (i, 0)), compiler_params=_params(("parallel",)),
    )(r)


def _adamw(name, g, w, m, v):
    rows, cols = g.shape
    tr = _pick(rows, (512, 256, 128, 64, 32, 16, 8))
    c1 = 1.0 - ADAM_B1 ** ADAM_STEP
    c2 = 1.0 - ADAM_B2 ** ADAM_STEP

    def body(g_ref, w_ref, m_ref, v_ref, d_ref, nm_ref, nv_ref):
        gv = g_ref[...]
        nm = ADAM_B1 * m_ref[...] + (1.0 - ADAM_B1) * gv
        nv = ADAM_B2 * v_ref[...] + (1.0 - ADAM_B2) * (gv * gv)
        d_ref[...] = -ADAM_LR * ((nm / c1) / (jnp.sqrt(nv / c2) + ADAM_EPS) + ADAM_WD * w_ref[...])
        nm_ref[...] = nm
        nv_ref[...] = nv

    spec = pl.BlockSpec((tr, cols), lambda i: (i, 0))
    return pl.pallas_call(
        body, name=name, out_shape=[jax.ShapeDtypeStruct((rows, cols), F32)] * 3, grid=(rows // tr,),
        in_specs=[spec] * 4, out_specs=[spec] * 3, compiler_params=_params(("parallel",)),
    )(g, w, m, v)


def _mm_nt(name, a, ws, w_offs, n, epi, out_dtypes, extras=(), rows=(), tm=None, tn=None):
    m, k = a.shape
    tm = tm or _pick(m, (512, 256, 128))
    tn = tn or _pick(n, (1408, 1152, 1024, 512, 256, 128))
    nw, ne, nr = len(ws), len(extras), len(rows)

    def body(*refs):
        a_ref, w_refs = refs[0], refs[1:1 + nw]
        e_refs, r_refs = refs[1 + nw:1 + nw + ne], refs[1 + nw + ne:1 + nw + ne + nr]
        o_refs = refs[1 + nw + ne + nr:]
        av = a_ref[...].astype(BF16)
        accs = [lax.dot_general(av, w[...], (((1,), (1,)), ((), ())), preferred_element_type=F32) for w in w_refs]
        outs = epi(accs, [e[...] for e in e_refs], [r[...] for r in r_refs])
        for o_ref, o in zip(o_refs, outs):
            o_ref[...] = o.astype(o_ref.dtype)

    in_specs = [pl.BlockSpec((tm, k), lambda j, i: (i, 0))]
    in_specs += [pl.BlockSpec((tn, k), functools.partial(lambda j, i, off: (j + off, 0), off=off)) for off in w_offs]
    in_specs += [pl.BlockSpec((tm, tn), lambda j, i: (i, j))] * ne
    in_specs += [pl.BlockSpec((1, tn), lambda j, i: (0, j))] * nr
    return pl.pallas_call(
        body, name=name, out_shape=[jax.ShapeDtypeStruct((m, n), dt) for dt in out_dtypes],
        grid=(n // tn, m // tm), in_specs=in_specs,
        out_specs=[pl.BlockSpec((tm, tn), lambda j, i: (i, j))] * len(out_dtypes),
        compiler_params=_params(("parallel", "parallel"), V7X_VMEM_LIMIT_BYTES),
    )(a, *ws, *extras, *rows)


def _mm_nn(name, as_, bs, epi, out_dtype, extras=(), rows=(), tm=None):
    m, k = as_[0].shape
    n = bs[0].shape[1]
    tm = tm or _pick(m, (512, 256, 128))
    npair, ne, nr = len(as_), len(extras), len(rows)

    def body(*refs):
        a_refs, b_refs = refs[:npair], refs[npair:2 * npair]
        e_refs, r_refs = refs[2 * npair:2 * npair + ne], refs[2 * npair + ne:2 * npair + ne + nr]
        o_ref = refs[-1]
        acc = None
        for a_ref, b_ref in zip(a_refs, b_refs):
            p = jnp.dot(a_ref[...].astype(BF16), b_ref[...], preferred_element_type=F32)
            acc = p if acc is None else acc + p
        o_ref[...] = epi(acc, [e[...] for e in e_refs], [r[...] for r in r_refs]).astype(o_ref.dtype)

    in_specs = [pl.BlockSpec((tm, k), lambda i: (i, 0))] * npair
    in_specs += [pl.BlockSpec((k, n), lambda i: (0, 0))] * npair
    in_specs += [pl.BlockSpec((tm, n), lambda i: (i, 0))] * ne
    in_specs += [pl.BlockSpec((1, n), lambda i: (0, 0))] * nr
    return pl.pallas_call(
        body, name=name, out_shape=jax.ShapeDtypeStruct((m, n), out_dtype), grid=(m // tm,), in_specs=in_specs,
        out_specs=pl.BlockSpec((tm, n), lambda i: (i, 0)),
        compiler_params=_params(("parallel",), V7X_VMEM_LIMIT_BYTES),
    )(*as_, *bs, *extras, *rows)


def _mm_tn(name, a, b, colsum_b=False, tm=None, tk=512):
    t, ma = a.shape
    nb = b.shape[1]
    tm = tm or _pick(ma, (1408, 1152, 1024, 768, 512, 256, 128))
    tk = _pick(t, (tk, 256, 128))
    nk = t // tk

    def body(*refs):
        a_ref, b_ref, o_ref = refs[0], refs[1], refs[2]
        acc_ref = refs[-1]
        kk = pl.program_id(1)
        bv = b_ref[...]

        @pl.when(kk == 0)
        def _():
            acc_ref[...] = jnp.zeros_like(acc_ref)

        acc_ref[...] += lax.dot_general(a_ref[...].astype(BF16), bv.astype(BF16), (((0,), (0,)), ((), ())),
                                        preferred_element_type=F32)
        if colsum_b:
            s_ref = refs[3]

            @pl.when((kk == 0) & (pl.program_id(0) == 0))
            def _():
                s_ref[...] = jnp.zeros_like(s_ref)

            @pl.when(pl.program_id(0) == 0)
            def _():
                s_ref[...] += jnp.sum(bv.astype(F32), axis=0, keepdims=True)

        @pl.when(kk == nk - 1)
        def _():
            o_ref[...] = acc_ref[...].astype(o_ref.dtype)

    out_shape = [jax.ShapeDtypeStruct((ma, nb), BF16)]
    out_specs = [pl.BlockSpec((tm, nb), lambda i, kk: (i, 0))]
    if colsum_b:
        out_shape.append(jax.ShapeDtypeStruct((1, nb), F32))
        out_specs.append(pl.BlockSpec((1, nb), lambda i, kk: (0, 0)))
    res = pl.pallas_call(
        body, name=name, out_shape=out_shape, grid=(ma // tm, nk),
        in_specs=[pl.BlockSpec((tk, tm), lambda i, kk: (kk, i)), pl.BlockSpec((tk, nb), lambda i, kk: (kk, 0))],
        out_specs=out_specs, scratch_shapes=[pltpu.VMEM((tm, nb), F32)],
        compiler_params=_params(("arbitrary", "arbitrary"), V7X_VMEM_LIMIT_BYTES),
    )(a, b)
    return res if colsum_b else res[0]


def _rmsnorm_fwd(name, x, g):
    t, d = x.shape
    tr = _pick(t, (512, 256, 128))

    def body(x_ref, g_ref, h_ref):
        xv = x_ref[...]
        r = lax.rsqrt(jnp.mean(xv * xv, axis=-1, keepdims=True) + EPS)
        h_ref[...] = (xv * r * g_ref[...]).astype(BF16)

    return pl.pallas_call(
        body, name=name, out_shape=jax.ShapeDtypeStruct((t, d), BF16), grid=(t // tr,),
        in_specs=[pl.BlockSpec((tr, d), lambda i: (i, 0)), pl.BlockSpec((1, d), lambda i: (0, 0))],
        out_specs=pl.BlockSpec((tr, d), lambda i: (i, 0)), compiler_params=_params(("parallel",)),
    )(x, g)


def _rmsnorm_bwd(name, dh, x, g, dx_out):
    t, d = x.shape
    tr = _pick(t, (512, 256, 128))

    def body(dh_ref, x_ref, g_ref, dxo_ref, dx_ref, dg_ref):
        xv = x_ref[...]
        r = lax.rsqrt(jnp.mean(xv * xv, axis=-1, keepdims=True) + EPS)
        yv = xv * r
        dhv = dh_ref[...].astype(F32)
        dy = dhv * g_ref[...]
        dx_ref[...] = dxo_ref[...] + r * (dy - yv * jnp.mean(dy * yv, axis=-1, keepdims=True))

        @pl.when(pl.program_id(0) == 0)
        def _():
            dg_ref[...] = jnp.zeros_like(dg_ref)

        dg_ref[...] += jnp.sum(dhv * yv, axis=0, keepdims=True)

    big = pl.BlockSpec((tr, d), lambda i: (i, 0))
    row = pl.BlockSpec((1, d), lambda i: (0, 0))
    return pl.pallas_call(
        body, name=name, out_shape=[jax.ShapeDtypeStruct((t, d), F32), jax.ShapeDtypeStruct((1, d), F32)],
        grid=(t // tr,), in_specs=[big, big, row, big], out_specs=[big, row],
        compiler_params=_params(("arbitrary",)),
    )(dh, x, g, dx_out)


def _loss_head(name, x, g, target):
    t, d = x.shape
    tr = _pick(t, (512, 256, 128))

    def body(x_ref, g_ref, t_ref, dx_ref, dg_ref, l_ref):
        xv = x_ref[...]
        r = lax.rsqrt(jnp.mean(xv * xv, axis=-1, keepdims=True) + EPS)
        yv = xv * r
        diff = yv * g_ref[...] - t_ref[...]
        dout = diff * (1.0 / d)
        dy = dout * g_ref[...]
        dx_ref[...] = r * (dy - yv * jnp.mean(dy * yv, axis=-1, keepdims=True))

        @pl.when(pl.program_id(0) == 0)
        def _():
            dg_ref[...] = jnp.zeros_like(dg_ref)
            l_ref[...] = jnp.zeros_like(l_ref)

        dg_ref[...] += jnp.sum(dout * yv, axis=0, keepdims=True)
        l_ref[...] += (0.5 / d) * jnp.sum(diff * diff, axis=0, keepdims=True)

    big = pl.BlockSpec((tr, d), lambda i: (i, 0))
    row = pl.BlockSpec((1, d), lambda i: (0, 0))
    return pl.pallas_call(
        body, name=name,
        out_shape=[jax.ShapeDtypeStruct((t, d), F32), jax.ShapeDtypeStruct((1, d), F32),
                   jax.ShapeDtypeStruct((1, d), F32)],
        grid=(t // tr,), in_specs=[big, row, big], out_specs=[big, row, row],
        compiler_params=_params(("arbitrary",)),
    )(x, g, target)


CONV_ROWS = 64


def _dwconv_fwd(name, glu, w_dw, b_dw, ln_g, ln_b):
    t, c = glu.shape
    tt = _pick(t, (256, 128))
    hb = tt // CONV_HALO

    def body(cur_ref, halo_ref, w_ref, b_ref, g_ref, be_ref, dw_ref, s_ref, win_ref):
        i = pl.program_id(0)
        halo = halo_ref[...].astype(F32)
        win_ref[0:CONV_HALO, :] = jnp.where(i > 0, halo, 0.0)
        win_ref[CONV_HALO:, :] = cur_ref[...].astype(F32)
        for r0 in range(0, tt, CONV_ROWS):
            for c0 in range(0, c, 128):
                acc = jnp.zeros((CONV_ROWS, 128), F32) + b_ref[:, c0:c0 + 128]
                for k in range(CONV_WIDTH):
                    o = r0 + k + CONV_HALO - (CONV_WIDTH - 1)
                    acc = acc + w_ref[k:k + 1, c0:c0 + 128] * win_ref[o:o + CONV_ROWS, c0:c0 + 128]
                dw_ref[r0:r0 + CONV_ROWS, c0:c0 + 128] = acc
        u = dw_ref[...]
        mu = jnp.mean(u, axis=-1, keepdims=True)
        uc = u - mu
        rstd = lax.rsqrt(jnp.mean(uc * uc, axis=-1, keepdims=True) + EPS)
        z = uc * rstd * g_ref[...] + be_ref[...]
        s_ref[...] = (z * _sigmoid(z)).astype(BF16)

    big = pl.BlockSpec((tt, c), lambda i: (i, 0))
    row = pl.BlockSpec((1, c), lambda i: (0, 0))
    return pl.pallas_call(
        body, name=name, out_shape=[jax.ShapeDtypeStruct((t, c), F32), jax.ShapeDtypeStruct((t, c), BF16)],
        grid=(t // tt,),
        in_specs=[big, pl.BlockSpec((CONV_HALO, c), lambda i: (jnp.maximum(i * hb - 1, 0), 0)),
                  pl.BlockSpec((CONV_HALO, c), lambda i: (0, 0)), row, row, row],
        out_specs=[big, big], scratch_shapes=[pltpu.VMEM((tt + CONV_HALO, c), F32)],
        compiler_params=_params(("parallel",)),
    )(glu, glu, w_dw, b_dw, ln_g, ln_b)


def _ln_silu_bwd(name, ds, dw, ln_g, ln_b):
    t, c = dw.shape
    tr = _pick(t, (256, 128))

    def body(ds_ref, dw_ref, g_ref, be_ref, o_ref, acc_ref):
        u = dw_ref[...]
        mu = jnp.mean(u, axis=-1, keepdims=True)
        uc = u - mu
        rstd = lax.rsqrt(jnp.mean(uc * uc, axis=-1, keepdims=True) + EPS)
        xh = uc * rstd
        z = xh * g_ref[...] + be_ref[...]
        sg = _sigmoid(z)
        dz = ds_ref[...].astype(F32) * (sg * (1.0 + z * (1.0 - sg)))
        dxh = dz * g_ref[...]
        du = rstd * (dxh - jnp.mean(dxh, axis=-1, keepdims=True) - xh * jnp.mean(dxh * xh, axis=-1, keepdims=True))
        o_ref[...] = du

        @pl.when(pl.program_id(0) == 0)
        def _():
            acc_ref[...] = jnp.zeros_like(acc_ref)

        acc_ref[0:1, :] += jnp.sum(dz * xh, axis=0, keepdims=True)
        acc_ref[1:2, :] += jnp.sum(dz, axis=0, keepdims=True)
        acc_ref[2:3, :] += jnp.sum(du, axis=0, keepdims=True)

    big = pl.BlockSpec((tr, c), lambda i: (i, 0))
    row = pl.BlockSpec((1, c), lambda i: (0, 0))
    return pl.pallas_call(
        body, name=name, out_shape=[jax.ShapeDtypeStruct((t, c), F32), jax.ShapeDtypeStruct((8, c), F32)],
        grid=(t // tr,), in_specs=[big, big, row, row],
        out_specs=[big, pl.BlockSpec((8, c), lambda i: (0, 0))], compiler_params=_params(("arbitrary",)),
    )(ds, dw, ln_g, ln_b)


def _dwconv_bwd(name, ddw, a, gt, w_dw):
    t, c = ddw.shape
    tt = _pick(t, (256, 128))
    hb = tt // CONV_HALO
    last = t // tt - 1
    back = CONV_WIDTH - 1

    def body(d_ref, dn_ref, a_ref, ap_ref, g_ref, gp_ref, w_ref, du_ref, dwk_ref, db_ref, wd_ref, wg_ref, dg_ref):
        i = pl.program_id(0)
        wd_ref[0:tt, :] = d_ref[...]
        wd_ref[tt:, :] = jnp.where(i < last, dn_ref[...], 0.0)
        glu_prev = ap_ref[...].astype(F32) * _sigmoid(gp_ref[...].astype(F32))
        wg_ref[0:CONV_HALO, :] = jnp.where(i > 0, glu_prev, 0.0)
        av = a_ref[...].astype(F32)
        sg = _sigmoid(g_ref[...].astype(F32))
        wg_ref[CONV_HALO:, :] = av * sg

        @pl.when(i == 0)
        def _():
            dwk_ref[...] = jnp.zeros_like(dwk_ref)
            db_ref[...] = jnp.zeros_like(db_ref)

        for r0 in range(0, tt, CONV_ROWS):
            for c0 in range(0, c, 128):
                dcur = wd_ref[r0:r0 + CONV_ROWS, c0:c0 + 128]
                acc = jnp.zeros((CONV_ROWS, 128), F32)
                for k in range(CONV_WIDTH):
                    o = r0 + back - k
                    acc = acc + w_ref[k:k + 1, c0:c0 + 128] * wd_ref[o:o + CONV_ROWS, c0:c0 + 128]
                    og = r0 + k + CONV_HALO - back
                    dwk_ref[k:k + 1, c0:c0 + 128] += jnp.sum(
                        dcur * wg_ref[og:og + CONV_ROWS, c0:c0 + 128], axis=0, keepdims=True)
                dg_ref[r0:r0 + CONV_ROWS, c0:c0 + 128] = acc
        dglu = dg_ref[...]
        da = dglu * sg
        dgate = dglu * av * sg * (1.0 - sg)
        du_ref[:, 0:c] = da.astype(BF16)
        du_ref[:, c:] = dgate.astype(BF16)
        db_ref[:, 0:c] += jnp.sum(da, axis=0, keepdims=True)
        db_ref[:, c:] += jnp.sum(dgate, axis=0, keepdims=True)

    big = pl.BlockSpec((tt, c), lambda i: (i, 0))
    prev = pl.BlockSpec((CONV_HALO, c), lambda i: (jnp.maximum(i * hb - 1, 0), 0))
    nxt = pl.BlockSpec((CONV_HALO, c), lambda i: (jnp.minimum((i + 1) * hb, t // CONV_HALO - 1), 0))
    return pl.pallas_call(
        body, name=name,
        out_shape=[jax.ShapeDtypeStruct((t, 2 * c), BF16), jax.ShapeDtypeStruct((CONV_HALO, c), F32),
                   jax.ShapeDtypeStruct((1, 2 * c), F32)],
        grid=(t // tt,),
        in_specs=[big, nxt, big, prev, big, prev, pl.BlockSpec((CONV_HALO, c), lambda i: (0, 0))],
        out_specs=[pl.BlockSpec((tt, 2 * c), lambda i: (i, 0)), pl.BlockSpec((CONV_HALO, c), lambda i: (0, 0)),
                   pl.BlockSpec((1, 2 * c), lambda i: (0, 0))],
        scratch_shapes=[pltpu.VMEM((tt + CONV_HALO, c), F32), pltpu.VMEM((tt + CONV_HALO, c), F32),
                        pltpu.VMEM((tt, c), F32)],
        compiler_params=_params(("arbitrary",)),
    )(ddw, ddw, a, a, gt, gt, w_dw)


def _bucket_tables():
    i = np.arange(N_BACK)[:, None]
    j = np.arange(2 * N_BACK)[None, :]
    dist = i + N_BACK - j
    valid = (dist >= 0) & (dist <= N_BACK)
    max_exact = N_BUCKETS // 2
    out = []
    for d in DILATIONS:
        n = np.maximum(dist * d, 0)
        nf = np.maximum(n, 1).astype(np.float32)
        large = max_exact + (np.log(nf / np.float32(max_exact)) / np.float32(math.log(REL_MAX_DISTANCE / max_exact))
                             * np.float32(N_BUCKETS - max_exact)).astype(np.int32)
        large = np.minimum(large, N_BUCKETS - 1)
        out.append(np.where(valid, np.where(n < max_exact, n, large), -1))
    return np.stack(out).astype(np.int32)


def _bias_build(name, rel_bias, buckets):
    def body(tbl_ref, bk_ref, o_ref):
        g = pl.program_id(0)
        bk = bk_ref[0]
        for h in range(HEADS_PER_GROUP):
            acc = jnp.zeros(bk.shape, F32)
            for b in range(N_BUCKETS):
                acc = jnp.where(bk == b, tbl_ref[b, g * HEADS_PER_GROUP + h], acc)
            o_ref[h] = jnp.where(bk < 0, NEG_INF, acc)

    return pl.pallas_call(
        body, name=name, out_shape=jax.ShapeDtypeStruct((N_HEADS, N_BACK, 2 * N_BACK), F32), grid=(N_GROUPS,),
        in_specs=[pl.BlockSpec(memory_space=pltpu.SMEM), pl.BlockSpec((1, N_BACK, 2 * N_BACK), lambda g: (g, 0, 0))],
        out_specs=pl.BlockSpec((HEADS_PER_GROUP, N_BACK, 2 * N_BACK), lambda g: (g, 0, 0)),
        compiler_params=_params(("arbitrary",)),
    )(rel_bias, buckets)


def _bias_grad(name, dbs, buckets):
    nd = len(dbs)

    def body(*refs):
        bk = refs[nd][0]
        o_ref = refs[nd + 1]
        lane = lax.broadcasted_iota(jnp.int32, (1, 128), 1)
        db = [sum(r[h] for r in refs[:nd]) for h in range(HEADS_PER_GROUP)]
        for b in range(N_BUCKETS):
            row = jnp.zeros((1, 128), F32)
            for h in range(HEADS_PER_GROUP):
                s = jnp.sum(jnp.where(bk == b, db[h], 0.0), axis=0, keepdims=True)
                s = jnp.sum(s, axis=1, keepdims=True)
                row = jnp.where(lane // 32 == h, s, row)
            o_ref[0, b:b + 1, :] = row

    spec = pl.BlockSpec((HEADS_PER_GROUP, N_BACK, 2 * N_BACK), lambda g: (g, 0, 0))
    return pl.pallas_call(
        body, name=name, out_shape=jax.ShapeDtypeStruct((N_GROUPS, N_BUCKETS, 128), F32), grid=(N_GROUPS,),
        in_specs=[spec] * nd + [pl.BlockSpec((1, N_BACK, 2 * N_BACK), lambda g: (g, 0, 0))],
        out_specs=pl.BlockSpec((1, N_BUCKETS, 128), lambda g: (g, 0, 0)), compiler_params=_params(("arbitrary",)),
    )(*dbs, buckets)


def _head_cols(h):
    return slice(h * HEAD_DIM, (h + 1) * HEAD_DIM)


def _attn_fwd(name, qkv, bias, g):
    t = qkv.shape[0]
    d = DILATIONS[g]
    tq = t // d
    nblk = qkv.shape[1] // GROUP_COLS
    scale = HEAD_DIM ** -0.5

    def body(q_ref, kp_ref, kc_ref, vp_ref, vc_ref, b_ref, o_ref, l_ref):
        n = pl.program_id(1)
        col = lax.broadcasted_iota(jnp.int32, (N_BACK, 2 * N_BACK), 1)
        keep = (col >= N_BACK) | (n > 0)
        lane = lax.broadcasted_iota(jnp.int32, (N_BACK, 128), 1)
        lse_tile = jnp.zeros((N_BACK, 128), F32)
        outs = []
        for h in range(HEADS_PER_GROUP):
            hc = _head_cols(h)
            kk = jnp.concatenate([kp_ref[:, hc], kc_ref[:, hc]], axis=0)
            vv = jnp.concatenate([vp_ref[:, hc], vc_ref[:, hc]], axis=0)
            s = lax.dot_general(q_ref[:, hc], kk, (((1,), (1,)), ((), ())), preferred_element_type=F32)
            s = jnp.where(keep, s * scale + b_ref[h], NEG_INF)
            m = jnp.max(s, axis=-1, keepdims=True)
            p = jnp.exp(s - m)
            den = jnp.sum(p, axis=-1, keepdims=True)
            outs.append(jnp.dot(p.astype(BF16), vv, preferred_element_type=F32) / den)
            lse_tile = jnp.where(lane // 32 == h, m + jnp.log(den), lse_tile)
        o_ref[...] = jnp.concatenate(outs, axis=1)
        l_ref[...] = lse_tile

    def blk(part, prev):
        if prev:
            return pl.BlockSpec((N_BACK, GROUP_COLS), lambda r, n: (jnp.maximum(n - 1, 0), r * nblk + 3 * part + g))
        return pl.BlockSpec((N_BACK, GROUP_COLS), lambda r, n: (n, r * nblk + 3 * part + g))

    qv = qkv.reshape(tq, d * qkv.shape[1])
    o, l = pl.pallas_call(
        body, name=name,
        out_shape=[jax.ShapeDtypeStruct((tq, d * GROUP_COLS), F32), jax.ShapeDtypeStruct((tq, d * 128), F32)],
        grid=(d, tq // N_BACK),
        in_specs=[blk(0, False), blk(1, True), blk(1, False), blk(2, True), blk(2, False),
                  pl.BlockSpec((HEADS_PER_GROUP, N_BACK, 2 * N_BACK), lambda r, n: (g, 0, 0))],
        out_specs=[pl.BlockSpec((N_BACK, GROUP_COLS), lambda r, n: (n, r)),
                   pl.BlockSpec((N_BACK, 128), lambda r, n: (n, r))],
        compiler_params=_params(("parallel", "parallel")),
    )(qv, qv, qv, qv, qv, bias)
    return o.reshape(t, GROUP_COLS), l.reshape(t, 128)


def _group_weights(l_refs, h):
    ls = [l_ref[:, 32 * h:32 * h + 1] for l_ref in l_refs]
    m = jnp.maximum(jnp.maximum(ls[0], ls[1]), ls[2])
    es = [jnp.exp(l - m) for l in ls]
    tot = es[0] + es[1] + es[2]
    return [e / tot for e in es]


def _attn_merge(name, os_, ls):
    t = os_[0].shape[0]
    tr = _pick(t, (512, 256, 128))

    def body(o0, o1, o2, l0, l1, l2, out_ref):
        o_refs = (o0, o1, o2)
        pieces = [[None] * HEADS_PER_GROUP for _ in range(N_GROUPS)]
        for h in range(HEADS_PER_GROUP):
            al = _group_weights((l0, l1, l2), h)
            for g in range(N_GROUPS):
                pieces[g][h] = o_refs[g][:, _head_cols(h)] * al[g]
        out_ref[...] = jnp.concatenate([p for row in pieces for p in row], axis=1).astype(BF16)

    so = pl.BlockSpec((tr, GROUP_COLS), lambda i: (i, 0))
    sl = pl.BlockSpec((tr, 128), lambda i: (i, 0))
    return pl.pallas_call(
        body, name=name, out_shape=jax.ShapeDtypeStruct((t, D_ATTN), BF16), grid=(t // tr,),
        in_specs=[so] * 3 + [sl] * 3, out_specs=pl.BlockSpec((tr, D_ATTN), lambda i: (i, 0)),
        compiler_params=_params(("parallel",)),
    )(*os_, *ls)


def _attn_bwd_prep(name, d_out, os_, ls):
    t = d_out.shape[0]
    tr = _pick(t, (512, 256, 128))

    def body(do_ref, o0, o1, o2, l0, l1, l2, d0, d1, d2, c0, c1, c2):
        o_refs, d_refs, c_refs = (o0, o1, o2), (d0, d1, d2), (c0, c1, c2)
        lane = lax.broadcasted_iota(jnp.int32, (tr, 128), 1)
        dos = [[None] * HEADS_PER_GROUP for _ in range(N_GROUPS)]
        cs = [jnp.zeros((tr, 128), F32) for _ in range(N_GROUPS)]
        for h in range(HEADS_PER_GROUP):
            al = _group_weights((l0, l1, l2), h)
            tot = jnp.zeros((tr, 1), F32)
            for g in range(N_GROUPS):
                dv = do_ref[:, g * GROUP_COLS + h * HEAD_DIM:g * GROUP_COLS + (h + 1) * HEAD_DIM].astype(F32)
                tot = tot + al[g] * jnp.sum(dv * o_refs[g][:, _head_cols(h)], axis=-1, keepdims=True)
                dos[g][h] = dv * al[g]
            for g in range(N_GROUPS):
                cs[g] = jnp.where(lane // 32 == h, -al[g] * tot, cs[g])
        for g in range(N_GROUPS):
            d_refs[g][...] = jnp.concatenate(dos[g], axis=1).astype(BF16)
            c_refs[g][...] = cs[g]

    so = pl.BlockSpec((tr, GROUP_COLS), lambda i: (i, 0))
    sl = pl.BlockSpec((tr, 128), lambda i: (i, 0))
    res = pl.pallas_call(
        body, name=name,
        out_shape=[jax.ShapeDtypeStruct((t, GROUP_COLS), BF16)] * 3 + [jax.ShapeDtypeStruct((t, 128), F32)] * 3,
        grid=(t // tr,), in_specs=[pl.BlockSpec((tr, D_ATTN), lambda i: (i, 0))] + [so] * 3 + [sl] * 3,
        out_specs=[so] * 3 + [sl] * 3, compiler_params=_params(("parallel",)),
    )(d_out, *os_, *ls)
    return res[:3], res[3:]


def _attn_bwd(name, qkv, do, lse, cterm, bias, g):
    t = qkv.shape[0]
    d = DILATIONS[g]
    tq = t // d
    nb = tq // N_BACK
    nblk = qkv.shape[1] // GROUP_COLS
    scale = HEAD_DIM ** -0.5
    nt = (((1,), (1,)), ((), ()))
    tn = (((0,), (0,)), ((), ()))

    def body(qn, qx, kp, kn, vp, vn, don, dox, ln, lx, cn, cx, b_ref, dqkv_ref, db_ref):
        n = pl.program_id(1)
        has_prev = n > 0
        has_next = n < nb - 1

        @pl.when((n == 0) & (pl.program_id(0) == 0))
        def _():
            db_ref[...] = jnp.zeros_like(db_ref)

        dqs, dks, dvs = [], [], []
        for h in range(HEADS_PER_GROUP):
            hc = _head_cols(h)
            st = slice(32 * h, 32 * h + 1)
            b_prev, b_same = b_ref[h, :, 0:N_BACK], b_ref[h, :, N_BACK:]

            def pair(q, k, v, dout, l, cc, bias_blk, on):
                s = lax.dot_general(q, k, nt, preferred_element_type=F32) * scale + bias_blk
                p = jnp.where(on, jnp.exp(s - l), 0.0)
                dp = lax.dot_general(dout, v, nt, preferred_element_type=F32)
                return p, p * (dp + cc)

            q0, q1, k0, k1, v0, v1 = qn[:, hc], qx[:, hc], kp[:, hc], kn[:, hc], vp[:, hc], vn[:, hc]
            d0, d1 = don[:, hc], dox[:, hc]
            p_a, ds_a = pair(q0, k1, v1, d0, ln[:, st], cn[:, st], b_same, True)
            p_b, ds_b = pair(q0, k0, v0, d0, ln[:, st], cn[:, st], b_prev, has_prev)
            p_c, ds_c = pair(q1, k1, v1, d1, lx[:, st], cx[:, st], b_prev, has_next)
            ds_a16, ds_b16, ds_c16 = ds_a.astype(BF16), ds_b.astype(BF16), ds_c.astype(BF16)
            dqs.append(scale * (jnp.dot(ds_a16, k1, preferred_element_type=F32)
                                + jnp.dot(ds_b16, k0, preferred_element_type=F32)))
            dks.append(scale * (lax.dot_general(ds_a16, q0, tn, preferred_element_type=F32)
                                + lax.dot_general(ds_c16, q1, tn, preferred_element_type=F32)))
            dvs.append(lax.dot_general(p_a.astype(BF16), d0, tn, preferred_element_type=F32)
                       + lax.dot_general(p_c.astype(BF16), d1, tn, preferred_element_type=F32))
            db_ref[h, :, 0:N_BACK] += ds_b
            db_ref[h, :, N_BACK:] += ds_a
        dqkv_ref[...] = jnp.concatenate(dqs + dks + dvs, axis=1).astype(BF16)

    def rows(which):
        if which == "prev":
            return lambda n: jnp.maximum(n - 1, 0)
        if which == "next":
            return lambda n: jnp.minimum(n + 1, nb - 1)
        return lambda n: n

    def qkv_blk(part, which):
        f = rows(which)
        return pl.BlockSpec((N_BACK, GROUP_COLS), lambda r, n: (f(n), r * nblk + 3 * part + g))

    def grp_blk(width, which):
        f = rows(which)
        return pl.BlockSpec((N_BACK, width), lambda r, n: (f(n), r))

    qv = qkv.reshape(tq, d * qkv.shape[1])
    dov = do.reshape(tq, d * GROUP_COLS)
    lv = lse.reshape(tq, d * 128)
    cv = cterm.reshape(tq, d * 128)
    dqkv_g, db = pl.pallas_call(
        body, name=name,
        out_shape=[jax.ShapeDtypeStruct((tq, d * 3 * GROUP_COLS), BF16),
                   jax.ShapeDtypeStruct((HEADS_PER_GROUP, N_BACK, 2 * N_BACK), F32)],
        grid=(d, nb),
        in_specs=[qkv_blk(0, "same"), qkv_blk(0, "next"), qkv_blk(1, "prev"), qkv_blk(1, "same"),
                  qkv_blk(2, "prev"), qkv_blk(2, "same"), grp_blk(GROUP_COLS, "same"), grp_blk(GROUP_COLS, "next"),
                  grp_blk(128, "same"), grp_blk(128, "next"), grp_blk(128, "same"), grp_blk(128, "next"),
                  pl.BlockSpec((HEADS_PER_GROUP, N_BACK, 2 * N_BACK), lambda r, n: (g, 0, 0))],
        out_specs=[pl.BlockSpec((N_BACK, 3 * GROUP_COLS), lambda r, n: (n, r)),
                   pl.BlockSpec((HEADS_PER_GROUP, N_BACK, 2 * N_BACK), lambda r, n: (0, 0, 0))],
        compiler_params=_params(("arbitrary", "arbitrary")),
    )(qv, qv, qv, qv, qv, qv, dov, dov, lv, lv, cv, cv, bias)
    return dqkv_g.reshape(t, 3 * GROUP_COLS), db


def _row(v):
    return v.reshape(1, -1)


def _plain(accs, extras, rows):
    return (accs[0],)


def _glu_epi(accs, extras, rows):
    a = (accs[0] + rows[0]).astype(BF16)
    gt = (accs[1] + rows[1]).astype(BF16)
    return a, gt, a.astype(F32) * _sigmoid(gt.astype(F32))


def _swiglu_epi(accs, extras, rows):
    gq, uq = accs[0].astype(BF16), accs[1].astype(BF16)
    gf = gq.astype(F32)
    return gq, uq, gf * _sigmoid(gf) * uq.astype(F32)


def _swiglu_bwd_epi(accs, extras, rows):
    gf, uf = extras[0].astype(F32), extras[1].astype(F32)
    sg = _sigmoid(gf)
    return accs[0] * uf * (sg * (1.0 + gf * (1.0 - sg))), accs[0] * gf * sg


def _residual(acc, extras, rows):
    out = acc + extras[0]
    return out + rows[0] if rows else out


def _identity(acc, extras, rows):
    return acc


def _group_rows(w):
    parts = [w[p * D_ATTN:(p + 1) * D_ATTN].reshape(N_GROUPS, GROUP_COLS, -1) for p in range(3)]
    return jnp.concatenate(parts, axis=1)


def _ungroup_rows(wg):
    return jnp.concatenate([wg[g][p * GROUP_COLS:(p + 1) * GROUP_COLS] for p in range(3) for g in range(N_GROUPS)],
                           axis=0)


def _local_step(x, target, sm, big):
    d_model = x.shape[1]
    depth = len(big["wgt"])
    buckets = jnp.asarray(_bucket_tables())
    bias = _bias_build("bias_build", sm["rel_bias"], buckets)
    saved = []
    for i in range(depth):
        j = i // 2
        rec = {"x_mix": x}
        h = _rmsnorm_fwd(f"rms_mix_fwd{i}", x, _row(sm["norm_mix"][i]))
        rec["h_mix"] = h
        if i % 2 == 0:
            c = big["w2"][j].shape[0]
            tn = _pick(c, (512, 256, 128))
            b1 = sm["conv_b_pw1"][j]
            a, gt, glu = _mm_nt(f"conv_pw1_fwd{j}", h, [big["w1t"][j]] * 2, [0, c // tn], c, _glu_epi, (BF16,) * 3,
                                rows=[_row(b1[:c]), _row(b1[c:])], tn=tn)
            dw, s = _dwconv_fwd(f"dwconv_fwd{j}", glu, big["wdw"][j], _row(sm["conv_b_dw"][j]),
                                _row(sm["conv_ln_g"][j]), _row(sm["conv_ln_b"][j]))
            x = _mm_nn(f"conv_pw2_fwd{j}", [s], [big["w2"][j]], _residual, F32, extras=[x],
                       rows=[_row(sm["conv_b_pw2"][j])])
            rec.update(a=a, gt=gt, dw=dw, s=s)
        else:
            qkv = _mm_nt(f"attn_qkv_fwd{j}", h, [big["wqkvt"][j]], [0], 3 * D_ATTN, _plain, (BF16,))[0]
            og = [_attn_fwd(f"attn_fwd{j}_{g}", qkv, bias, g) for g in range(N_GROUPS)]
            os_, ls = [o for o, _ in og], [l for _, l in og]
            om = _attn_merge(f"attn_merge{j}", os_, ls)
            x = _mm_nt(f"attn_out_fwd{j}", om, [big["wot"][j]], [0], d_model, lambda accs, e, r: (accs[0] + e[0],),
                       (F32,), extras=[x])[0]
            rec.update(qkv=qkv, os=os_, ls=ls, om=om)
        rec["x_ffn"] = x
        h2 = _rmsnorm_fwd(f"rms_ffn_fwd{i}", x, _row(sm["norm_ffn"][i]))
        f = big["wd"][i].shape[0]
        gq, uq, act = _mm_nt(f"ffn_up_fwd{i}", h2, [big["wgt"][i], big["wut"][i]], [0, 0], f, _swiglu_epi,
                             (BF16,) * 3)
        x = _mm_nn(f"ffn_down_fwd{i}", [act], [big["wd"][i]], _residual, F32, extras=[x])
        rec.update(h_ffn=h2, gq=gq, uq=uq, act=act)
        saved.append(rec)

    dx, g_final, loss_cols = _loss_head("loss_head", x, _row(sm["final_norm"]), target)

    gs = {k: [None] * len(v) for k, v in big.items()}
    g_mix, g_ffn = [None] * depth, [None] * depth
    nconv = len(big["w2"])
    g_b1, g_bdw, g_lng, g_lnb, g_b2 = ([None] * nconv for _ in range(5))
    dbias = []
    for i in reversed(range(depth)):
        j = i // 2
        rec = saved[i]
        f = big["wd"][i].shape[0]
        dgate, dup = _mm_nt(f"ffn_down_bwd{i}", dx, [big["wd"][i]], [0], f, _swiglu_bwd_epi, (BF16, BF16),
                            extras=[rec["gq"], rec["uq"]])
        gs["wd"][i] = _mm_tn(f"ffn_down_dw{i}", rec["act"], dx)
        gs["wgt"][i] = _mm_tn(f"ffn_gate_dw{i}", dgate, rec["h_ffn"])
        gs["wut"][i] = _mm_tn(f"ffn_up_dw{i}", dup, rec["h_ffn"])
        dh = _mm_nn(f"ffn_up_bwd{i}", [dgate, dup], [big["wgt"][i], big["wut"][i]], _identity, BF16, tm=256)
        dx, g_ffn[i] = _rmsnorm_bwd(f"rms_ffn_bwd{i}", dh, rec["x_ffn"], _row(sm["norm_ffn"][i]), dx)
        if i % 2 == 0:
            c = big["w2"][j].shape[0]
            gs["w2"][j], g_b2[j] = _mm_tn(f"conv_pw2_dw{j}", rec["s"], dx, colsum_b=True)
            ds = _mm_nt(f"conv_pw2_bwd{j}", dx, [big["w2"][j]], [0], c, _plain, (BF16,))[0]
            ddw, sums = _ln_silu_bwd(f"ln_silu_bwd{j}", ds, rec["dw"], _row(sm["conv_ln_g"][j]),
                                     _row(sm["conv_ln_b"][j]))
            g_lng[j], g_lnb[j], g_bdw[j] = sums[0], sums[1], sums[2]
            du, dwk, db1 = _dwconv_bwd(f"dwconv_bwd{j}", ddw, rec["a"], rec["gt"], big["wdw"][j])
            gs["wdw"][j] = dwk[:CONV_WIDTH]
            g_b1[j] = db1[0]
            gs["w1t"][j] = _mm_tn(f"conv_pw1_dw{j}", du, rec["h_mix"])
            dh = _mm_nn(f"conv_pw1_bwd{j}", [du], [big["w1t"][j]], _identity, BF16)
        else:
            gs["wot"][j] = _mm_tn(f"attn_out_dw{j}", dx, rec["om"])
            d_om = _mm_nn(f"attn_out_bwd{j}", [dx], [big["wot"][j]], _identity, BF16)
            dos, cs = _attn_bwd_prep(f"attn_bwd_prep{j}", d_om, rec["os"], rec["ls"])
            back = [_attn_bwd(f"attn_bwd{j}_{g}", rec["qkv"], dos[g], rec["ls"][g], cs[g], bias, g)
                    for g in range(N_GROUPS)]
            dqkv = [b[0] for b in back]
            dbias.append(jnp.concatenate([b[1] for b in back], axis=0))
            wq = _group_rows(big["wqkvt"][j])
            dh = _mm_nn(f"attn_qkv_bwd{j}", dqkv, [wq[g] for g in range(N_GROUPS)], _identity, BF16)
            gs["wqkvt"][j] = _ungroup_rows([_mm_tn(f"attn_qkv_dw{j}_{g}", dqkv[g], rec["h_mix"])
                                            for g in range(N_GROUPS)])
        dx, g_mix[i] = _rmsnorm_bwd(f"rms_mix_bwd{i}", dh, rec["x_mix"], _row(sm["norm_mix"][i]), dx)

    gb = _bias_grad("bias_grad", dbias, buckets)
    g_rel = jnp.transpose(gb[:, :, ::32], (1, 0, 2)).reshape(N_BUCKETS, N_HEADS)
    gsm = {
        "norm_mix": jnp.concatenate(g_mix, axis=0), "norm_ffn": jnp.concatenate(g_ffn, axis=0),
        "final_norm": g_final[0], "conv_b_pw1": jnp.stack(g_b1), "conv_b_dw": jnp.stack(g_bdw),
        "conv_ln_g": jnp.stack(g_lng), "conv_ln_b": jnp.stack(g_lnb),
        "conv_b_pw2": jnp.concatenate(g_b2, axis=0), "rel_bias": g_rel,
    }
    return loss_cols, dx, gsm, gs


SMALL = ("norm_mix", "norm_ffn", "final_norm", "conv_b_pw1", "conv_b_dw", "conv_ln_g", "conv_ln_b", "conv_b_pw2",
         "rel_bias")
SHARDED = (("conv_w_pw1", "w1t", True), ("conv_w_pw2", "w2", False), ("attn_w_qkv", "wqkvt", True),
           ("attn_w_o", "wot", True), ("ffn_w_gate", "wgt", True), ("ffn_w_up", "wut", True),
           ("ffn_w_down", "wd", False))
ORDER = ("norm_mix", "norm_ffn", "final_norm", "conv_w_pw1", "conv_b_pw1", "conv_w_dw", "conv_b_dw", "conv_ln_g",
         "conv_ln_b", "conv_w_pw2", "conv_b_pw2", "attn_w_qkv", "attn_w_o", "rel_bias", "ffn_w_gate", "ffn_w_up",
         "ffn_w_down")
PACK_LANES = 128
PACK_ROW_TILE = 8


def _pack_small(vals):
    flat = jnp.concatenate([vals[n].reshape(-1) for n in SMALL])
    per_tile = PACK_LANES * PACK_ROW_TILE
    return jnp.pad(flat, (0, -flat.shape[0] % per_tile)).reshape(-1, PACK_LANES)


def _unpack_small(pack, like):
    flat, out, pos = pack.reshape(-1), {}, 0
    for n in SMALL:
        out[n] = flat[pos:pos + like[n].size].reshape(like[n].shape)
        pos += like[n].size
    return out


def _dw_blocks(w):
    l, k, c = w.shape
    blk = jnp.transpose(w.reshape(l, k, N_DEV, c // N_DEV), (2, 0, 1, 3)).reshape(N_DEV, l * k, c // N_DEV)
    return jnp.pad(blk, ((0, 0), (0, -(l * k) % 8), (0, 0)))


def kernel(x, norm_mix, norm_ffn, final_norm, conv_w_pw1, conv_b_pw1, conv_w_dw, conv_b_dw, conv_ln_g, conv_ln_b, conv_w_pw2, conv_b_pw2, attn_w_qkv, attn_w_o, rel_bias, ffn_w_gate, ffn_w_up, ffn_w_down, loss_target, m_norm_mix, m_norm_ffn, m_final_norm, m_conv_w_pw1, m_conv_b_pw1, m_conv_w_dw, m_conv_b_dw, m_conv_ln_g, m_conv_ln_b, m_conv_w_pw2, m_conv_b_pw2, m_attn_w_qkv, m_attn_w_o, m_rel_bias, m_ffn_w_gate, m_ffn_w_up, m_ffn_w_down, v_norm_mix, v_norm_ffn, v_final_norm, v_conv_w_pw1, v_conv_b_pw1, v_conv_w_dw, v_conv_b_dw, v_conv_ln_g, v_conv_ln_b, v_conv_w_pw2, v_conv_b_pw2, v_attn_w_qkv, v_attn_w_o, v_rel_bias, v_ffn_w_gate, v_ffn_w_up, v_ffn_w_down):
    w = dict(norm_mix=norm_mix, norm_ffn=norm_ffn, final_norm=final_norm, conv_w_pw1=conv_w_pw1,
             conv_b_pw1=conv_b_pw1, conv_w_dw=conv_w_dw, conv_b_dw=conv_b_dw, conv_ln_g=conv_ln_g,
             conv_ln_b=conv_ln_b, conv_w_pw2=conv_w_pw2, conv_b_pw2=conv_b_pw2, attn_w_qkv=attn_w_qkv,
             attn_w_o=attn_w_o, rel_bias=rel_bias, ffn_w_gate=ffn_w_gate, ffn_w_up=ffn_w_up, ffn_w_down=ffn_w_down)
    m = dict(norm_mix=m_norm_mix, norm_ffn=m_norm_ffn, final_norm=m_final_norm, conv_w_pw1=m_conv_w_pw1,
             conv_b_pw1=m_conv_b_pw1, conv_w_dw=m_conv_w_dw, conv_b_dw=m_conv_b_dw, conv_ln_g=m_conv_ln_g,
             conv_ln_b=m_conv_ln_b, conv_w_pw2=m_conv_w_pw2, conv_b_pw2=m_conv_b_pw2, attn_w_qkv=m_attn_w_qkv,
             attn_w_o=m_attn_w_o, rel_bias=m_rel_bias, ffn_w_gate=m_ffn_w_gate, ffn_w_up=m_ffn_w_up,
             ffn_w_down=m_ffn_w_down)
    v = dict(norm_mix=v_norm_mix, norm_ffn=v_norm_ffn, final_norm=v_final_norm, conv_w_pw1=v_conv_w_pw1,
             conv_b_pw1=v_conv_b_pw1, conv_w_dw=v_conv_w_dw, conv_b_dw=v_conv_b_dw, conv_ln_g=v_conv_ln_g,
             conv_ln_b=v_conv_ln_b, conv_w_pw2=v_conv_w_pw2, conv_b_pw2=v_conv_b_pw2, attn_w_qkv=v_attn_w_qkv,
             attn_w_o=v_attn_w_o, rel_bias=v_rel_bias, ffn_w_gate=v_ffn_w_gate, ffn_w_up=v_ffn_w_up,
             ffn_w_down=v_ffn_w_down)

    shards, owner = [], []
    for name, key, cols in SHARDED:
        sw = (jnp.swapaxes(w[name], 1, 2) if cols else w[name]).astype(BF16)
        for layer in range(sw.shape[0]):
            shards.append(sw[layer])
            owner.append((key, layer))
    n_elem = sum(s.size for s in shards)
    like_big = jnp.zeros((n_elem // 1024, 1024), BF16)
    dw_shard = jnp.pad(conv_w_dw.reshape(-1, conv_w_dw.shape[2]), ((0, -(conv_w_dw.shape[0] * CONV_WIDTH) % 8), (0, 0)))
    gathered, (dw_all,) = _exchange("gather_weights", [(shards, like_big), ([dw_shard], dw_shard)], scatter=False)
    big = {key: [None] * w[name].shape[0] for name, key, _ in SHARDED}
    for (key, layer), garr in zip(owner, gathered):
        big[key][layer] = garr.reshape(garr.shape[0] * garr.shape[1], garr.shape[2])
    n_conv, _, cb = conv_w_dw.shape
    dw_full = jnp.transpose(dw_all[:, :n_conv * CONV_WIDTH].reshape(N_DEV, n_conv, CONV_WIDTH, cb), (1, 2, 0, 3))
    dw_full = jnp.pad(dw_full.reshape(n_conv, CONV_WIDTH, N_DEV * cb), ((0, 0), (0, CONV_HALO - CONV_WIDTH), (0, 0)))
    big["wdw"] = [dw_full[layer] for layer in range(n_conv)]

    sm = {n: w[n] for n in SMALL}
    loss_cols, dx, gsm, gs = _local_step(x[0], loss_target[0], sm, big)
    loss = lax.psum(jnp.sum(loss_cols), ("x", "y", "c"))

    parts = [gs[key][layer].reshape(N_DEV, -1, gs[key][layer].shape[1]) for key, layer in owner]
    dw_parts = _dw_blocks(jnp.stack(gs["wdw"]))
    recv, (dw_recv,) = _exchange("scatter_grads", [(parts, like_big), ([dw_parts], dw_parts[0])], scatter=True)
    pack = _pack_small(gsm)
    ((pack_all,),) = _exchange("gather_small_grads", [([pack], pack)], scatter=False)

    grads = {}
    summed = {}
    for (key, layer), r in zip(owner, recv):
        summed.setdefault(key, []).append(_sum8(f"sum_{key}{layer}", r))
    for name, key, cols in SHARDED:
        g = jnp.stack(summed[key])
        grads[name] = jnp.swapaxes(g, 1, 2) if cols else g
    grads["conv_w_dw"] = _sum8("sum_wdw", dw_recv)[:n_conv * CONV_WIDTH].reshape(conv_w_dw.shape)
    pack_sum = _sum8("sum_small", pack_all)
    grads.update(_unpack_small(pack_sum, sm))

    delta, new_m, new_v = {}, {}, {}
    for name in [n for n, _, _ in SHARDED] + ["conv_w_dw"]:
        shape = w[name].shape
        res = _adamw(f"adamw_{name}", *[t.reshape(-1, shape[-1]) for t in (grads[name], w[name], m[name], v[name])])
        delta[name], new_m[name], new_v[name] = (t.reshape(shape) for t in res)
    res = _adamw("adamw_small", pack_sum, _pack_small(sm), _pack_small({n: m[n] for n in SMALL}),
                 _pack_small({n: v[n] for n in SMALL}))
    for dst, t in zip((delta, new_m, new_v), res):
        dst.update(_unpack_small(t, sm))

    outs = [loss, dx[None]]
    for d in (grads, delta, new_m, new_v):
        outs += [d[n] for n in ORDER]
    return tuple(outs)
```

```python
import functools
import math

import numpy as np
import jax
import jax.numpy as jnp
from jax import lax
from jax.experimental import pallas as pl
from jax.experimental.pallas import tpu as pltpu

F32 = jnp.float32
BF16 = jnp.bfloat16

N_DEV = 8
HEAD_DIM = 64
HEADS_PER_GROUP = 4
GROUP_COLS = HEADS_PER_GROUP * HEAD_DIM
DILATIONS = (1, 4, 16)
N_BACK = 128
N_GROUPS = 3
N_HEADS = 12
D_ATTN = 768
N_BUCKETS = 32
REL_MAX_DISTANCE = 2048
CONV_WIDTH = 31
CONV_HALO = 32
EPS = 1e-6
NEG_INF = -1e30
ADAM_LR, ADAM_B1, ADAM_B2, ADAM_EPS, ADAM_WD, ADAM_STEP = 0.001, 0.9, 0.999, 1e-08, 0.01, 10
V7X_VMEM_LIMIT_BYTES = 56 * 1024 * 1024
MESH = pl.DeviceIdType.MESH
ANY = pl.BlockSpec(memory_space=pl.ANY)


def _pick(n, prefs):
    for p in prefs:
        if n % p == 0:
            return p
    return n


def _params(sem, vmem=None):
    return pltpu.CompilerParams(dimension_semantics=sem, vmem_limit_bytes=vmem)


def _sigmoid(x):
    return 1.0 / (1.0 + jnp.exp(-x))


def _exchange(name, groups, scatter):
    n_arr = [len(arrs) for arrs, _ in groups]
    n_in = sum(n_arr) + len(groups)
    ng = len(groups)

    def body(*refs):
        ins, outs, (send_sems, recv_sems, local_sems) = refs[:n_in], refs[n_in:-3], refs[-3:]
        x, y, c = lax.axis_index("x"), lax.axis_index("y"), lax.axis_index("c")
        me = 4 * x + 2 * y + c
        pos_in = pos_out = 0
        plans = []
        for gi in range(ng):
            srcs = ins[pos_in:pos_in + n_arr[gi]]
            like = ins[pos_in + n_arr[gi]]
            dsts = outs[pos_out:pos_out + n_arr[gi]]
            pos_in += n_arr[gi] + 1
            pos_out += n_arr[gi]
            plans.append((gi, srcs, like, dsts))
        local = []
        for gi, srcs, like, dsts in plans:
            for s, d in zip(srcs, dsts):
                cp = pltpu.make_async_copy(s.at[me] if scatter else s, d.at[me], local_sems.at[gi])
                cp.start()
                local.append(cp)
        for delta in range(1, N_DEV):
            dx, dy, dc = (delta >> 2) & 1, (delta >> 1) & 1, delta & 1
            px, py, pc = (1 - x if dx else x), (1 - y if dy else y), (1 - c if dc else c)
            peer = 4 * px + 2 * py + pc
            for gi, srcs, like, dsts in plans:
                for s, d in zip(srcs, dsts):
                    pltpu.make_async_remote_copy(
                        src_ref=s.at[peer] if scatter else s, dst_ref=d.at[me],
                        send_sem=send_sems.at[gi, delta - 1], recv_sem=recv_sems.at[gi, delta - 1],
                        device_id=(px, py, pc), device_id_type=MESH).start()
        for delta in range(1, N_DEV):
            for gi, srcs, like, dsts in plans:
                pltpu.make_async_remote_copy(
                    src_ref=like, dst_ref=like, send_sem=send_sems.at[gi, delta - 1],
                    recv_sem=recv_sems.at[gi, delta - 1], device_id=(x, y, c), device_id_type=MESH).wait()
        for cp in local:
            cp.wait()

    operands, out_shape = [], []
    for arrs, like in groups:
        operands += list(arrs) + [like]
        for a in arrs:
            blk = a.shape[1:] if scatter else a.shape
            out_shape.append(jax.ShapeDtypeStruct((N_DEV,) + tuple(blk), a.dtype))
    outs = pl.pallas_call(
        body, name=name, out_shape=out_shape, in_specs=[ANY] * len(operands), out_specs=[ANY] * len(out_shape),
        scratch_shapes=[pltpu.SemaphoreType.DMA((ng, N_DEV - 1)), pltpu.SemaphoreType.DMA((ng, N_DEV - 1)),
                        pltpu.SemaphoreType.DMA((ng,))],
        compiler_params=pltpu.CompilerParams(has_side_effects=True),
    )(*operands)
    res, pos = [], 0
    for n in n_arr:
        res.append(list(outs[pos:pos + n]))
        pos += n
    return res


HBM = pl.BlockSpec(memory_space=pltpu.HBM)
SEM = pl.BlockSpec(memory_space=pltpu.SEMAPHORE)
EFFECT = pltpu.SideEffectType.DATAFLOW_SIDE_EFFECTING


def _in_hbm(a):
    return pltpu.with_memory_space_constraint(a, pltpu.HBM)


def _exchange_start(name, groups, scatter):
    ns = [len(s) for s, _ in groups]
    n_in = 2 * sum(ns)

    def body(*refs):
        ins, outs = refs[:n_in], refs[n_in:]
        x, y, c = lax.axis_index("x"), lax.axis_index("y"), lax.axis_index("c")
        me = 4 * x + 2 * y + c
        pi = po = 0
        for n in ns:
            srcs, lands = ins[pi:pi + n], ins[pi + n:pi + 2 * n]
            send_sems, recv_sems = outs[po], outs[po + 1]
            pi += 2 * n
            po += 2 + 2 * n
            for delta in range(1, N_DEV):
                dx, dy, dc = (delta >> 2) & 1, (delta >> 1) & 1, delta & 1
                px, py, pc = (1 - x if dx else x), (1 - y if dy else y), (1 - c if dc else c)
                peer = 4 * px + 2 * py + pc
                for s, d in zip(srcs, lands):
                    pltpu.make_async_remote_copy(
                        src_ref=s.at[peer] if scatter else s, dst_ref=d.at[me], send_sem=send_sems.at[delta - 1],
                        recv_sem=recv_sems.at[delta - 1], device_id=(px, py, pc), device_id_type=MESH).start()
        outs[-1][...] = jnp.zeros_like(outs[-1])

    operands, out_shape, out_specs, aliases = [], [], [], {}
    for srcs, lands in groups:
        out_shape += [pltpu.SemaphoreType.DMA((N_DEV - 1,))] * 2
        out_specs += [SEM, SEM]
        for a in list(srcs) + list(lands):
            aliases[len(operands)] = len(out_shape)
            operands.append(_in_hbm(a))
            out_shape.append(pltpu.HBM(a.shape, a.dtype))
            out_specs.append(HBM)
    out_shape.append(jax.ShapeDtypeStruct((8, 128), F32))
    out_specs.append(pl.BlockSpec(memory_space=pltpu.VMEM))
    outs = pl.pallas_call(
        body, name=name, out_shape=out_shape, in_specs=[HBM] * len(operands), out_specs=out_specs,
        input_output_aliases=aliases, compiler_params=pltpu.CompilerParams(has_side_effects=EFFECT),
    )(*operands)
    handles, po = [], 0
    for n in ns:
        handles.append((outs[po], outs[po + 1], list(outs[po + 2:po + 2 + n]), list(outs[po + 2 + n:po + 2 + 2 * n])))
        po += 2 + 2 * n
    return handles, outs[-1]


def _exchange_wait(name, pieces, after):
    ns = [len(h[2]) for h, _ in pieces]

    def body(*refs):
        x, y, c = lax.axis_index("x"), lax.axis_index("y"), lax.axis_index("c")
        pi = 0
        for n in ns:
            send_sems, recv_sems, like = refs[pi + 2 * n], refs[pi + 2 * n + 1], refs[pi + 2 * n + 2]
            pi += 2 * n + 3
            for delta in range(1, N_DEV):
                cp = pltpu.make_async_remote_copy(
                    src_ref=like, dst_ref=like, send_sem=send_sems.at[delta - 1], recv_sem=recv_sems.at[delta - 1],
                    device_id=(x, y, c), device_id_type=MESH)
                cp.wait_send()
                cp.wait_recv()

    operands, in_specs, out_shape, aliases = [], [], [], {}
    for (send_sems, recv_sems, srcs, lands), like in pieces:
        for a in srcs + lands:
            aliases[len(operands)] = len(out_shape)
            operands.append(a)
            in_specs.append(HBM)
            out_shape.append(pltpu.HBM(a.shape, a.dtype))
        operands += [send_sems, recv_sems, like]
        in_specs += [SEM, SEM, ANY]
    operands.append(after)
    in_specs.append(ANY)
    outs = pl.pallas_call(
        body, name=name, out_shape=out_shape, in_specs=in_specs, out_specs=[HBM] * len(out_shape),
        input_output_aliases=aliases, compiler_params=pltpu.CompilerParams(has_side_effects=EFFECT),
    )(*operands)
    res, po = [], 0
    for n in ns:
        res.append(list(outs[po + n:po + 2 * n]))
        po += 2 * n
    return res


def _sum8(name, r):
    _, rows, cols = r.shape
    tr = _pick(rows, (256, 128, 64, 32, 16, 8))

    def body(r_ref, o_ref):
        acc = r_ref[0].astype(F32)
        for p in range(1, N_DEV):
            acc = acc + r_ref[p].astype(F32)
        o_ref[...] = acc

    return pl.pallas_call(
        body, name=name, out_shape=jax.ShapeDtypeStruct((rows, cols), F32), grid=(rows // tr,),
        in_specs=[pl.BlockSpec((N_DEV, tr, cols), lambda i: (0, i, 0))],
        out_specs=pl.BlockSpec((tr, cols), lambda i: (i, 0)), compiler_params=_params(("parallel",)),
    )(r)


def _adamw(name, g, w, m, v):
    rows, cols = g.shape
    tr = _pick(rows, (512, 256, 128, 64, 32, 16, 8))
    c1 = 1.0 - ADAM_B1 ** ADAM_STEP
    c2 = 1.0 - ADAM_B2 ** ADAM_STEP

    def body(g_ref, w_ref, m_ref, v_ref, d_ref, nm_ref, nv_ref):
        gv = g_ref[...]
        nm = ADAM_B1 * m_ref[...] + (1.0 - ADAM_B1) * gv
        nv = ADAM_B2 * v_ref[...] + (1.0 - ADAM_B2) * (gv * gv)
        d_ref[...] = -ADAM_LR * ((nm / c1) / (jnp.sqrt(nv / c2) + ADAM_EPS) + ADAM_WD * w_ref[...])
        nm_ref[...] = nm
        nv_ref[...] = nv

    spec = pl.BlockSpec((tr, cols), lambda i: (i, 0))
    return pl.pallas_call(
        body, name=name, out_shape=[jax.ShapeDtypeStruct((rows, cols), F32)] * 3, grid=(rows // tr,),
        in_specs=[spec] * 4, out_specs=[spec] * 3, compiler_params=_params(("parallel",)),
    )(g, w, m, v)


def _mm_nt(name, a, ws, w_offs, n, epi, out_dtypes, extras=(), rows=(), tm=None, tn=None):
    m, k = a.shape
    tm = tm or _pick(m, (512, 256, 128))
    tn = tn or _pick(n, (1408, 1152, 1024, 512, 256, 128))
    nw, ne, nr = len(ws), len(extras), len(rows)

    def body(*refs):
        a_ref, w_refs = refs[0], refs[1:1 + nw]
        e_refs, r_refs = refs[1 + nw:1 + nw + ne], refs[1 + nw + ne:1 + nw + ne + nr]
        o_refs = refs[1 + nw + ne + nr:]
        av = a_ref[...].astype(BF16)
        accs = [lax.dot_general(av, w[...], (((1,), (1,)), ((), ())), preferred_element_type=F32) for w in w_refs]
        outs = epi(accs, [e[...] for e in e_refs], [r[...] for r in r_refs])
        for o_ref, o in zip(o_refs, outs):
            o_ref[...] = o.astype(o_ref.dtype)

    in_specs = [pl.BlockSpec((tm, k), lambda j, i: (i, 0))]
    in_specs += [pl.BlockSpec((tn, k), functools.partial(lambda j, i, off: (j + off, 0), off=off)) for off in w_offs]
    in_specs += [pl.BlockSpec((tm, tn), lambda j, i: (i, j))] * ne
    in_specs += [pl.BlockSpec((1, tn), lambda j, i: (0, j))] * nr
    return pl.pallas_call(
        body, name=name, out_shape=[jax.ShapeDtypeStruct((m, n), dt) for dt in out_dtypes],
        grid=(n // tn, m // tm), in_specs=in_specs,
        out_specs=[pl.BlockSpec((tm, tn), lambda j, i: (i, j))] * len(out_dtypes),
        compiler_params=_params(("parallel", "parallel"), V7X_VMEM_LIMIT_BYTES),
    )(a, *ws, *extras, *rows)


def _mm_nn(name, as_, bs, epi, out_dtype, extras=(), rows=(), tm=None):
    m, k = as_[0].shape
    n = bs[0].shape[1]
    tm = tm or _pick(m, (512, 256, 128))
    npair, ne, nr = len(as_), len(extras), len(rows)

    def body(*refs):
        a_refs, b_refs = refs[:npair], refs[npair:2 * npair]
        e_refs, r_refs = refs[2 * npair:2 * npair + ne], refs[2 * npair + ne:2 * npair + ne + nr]
        o_ref = refs[-1]
        acc = None
        for a_ref, b_ref in zip(a_refs, b_refs):
            p = jnp.dot(a_ref[...].astype(BF16), b_ref[...], preferred_element_type=F32)
            acc = p if acc is None else acc + p
        o_ref[...] = epi(acc, [e[...] for e in e_refs], [r[...] for r in r_refs]).astype(o_ref.dtype)

    in_specs = [pl.BlockSpec((tm, k), lambda i: (i, 0))] * npair
    in_specs += [pl.BlockSpec((k, n), lambda i: (0, 0))] * npair
    in_specs += [pl.BlockSpec((tm, n), lambda i: (i, 0))] * ne
    in_specs += [pl.BlockSpec((1, n), lambda i: (0, 0))] * nr
    return pl.pallas_call(
        body, name=name, out_shape=jax.ShapeDtypeStruct((m, n), out_dtype), grid=(m // tm,), in_specs=in_specs,
        out_specs=pl.BlockSpec((tm, n), lambda i: (i, 0)),
        compiler_params=_params(("parallel",), V7X_VMEM_LIMIT_BYTES),
    )(*as_, *bs, *extras, *rows)


def _mm_tn(name, a, b, colsum_b=False, tm=None, tk=512):
    t, ma = a.shape
    nb = b.shape[1]
    tm = tm or _pick(ma, (1408, 1152, 1024, 768, 512, 256, 128))
    tk = _pick(t, (tk, 256, 128))
    nk = t // tk

    def body(*refs):
        a_ref, b_ref, o_ref = refs[0], refs[1], refs[2]
        acc_ref = refs[-1]
        kk = pl.program_id(1)
        bv = b_ref[...]

        @pl.when(kk == 0)
        def _():
            acc_ref[...] = jnp.zeros_like(acc_ref)

        acc_ref[...] += lax.dot_general(a_ref[...].astype(BF16), bv.astype(BF16), (((0,), (0,)), ((), ())),
                                        preferred_element_type=F32)
        if colsum_b:
            s_ref = refs[3]

            @pl.when((kk == 0) & (pl.program_id(0) == 0))
            def _():
                s_ref[...] = jnp.zeros_like(s_ref)

            @pl.when(pl.program_id(0) == 0)
            def _():
                s_ref[...] += jnp.sum(bv.astype(F32), axis=0, keepdims=True)

        @pl.when(kk == nk - 1)
        def _():
            o_ref[...] = acc_ref[...].astype(o_ref.dtype)

    out_shape = [jax.ShapeDtypeStruct((ma, nb), BF16)]
    out_specs = [pl.BlockSpec((tm, nb), lambda i, kk: (i, 0))]
    if colsum_b:
        out_shape.append(jax.ShapeDtypeStruct((1, nb), F32))
        out_specs.append(pl.BlockSpec((1, nb), lambda i, kk: (0, 0)))
    res = pl.pallas_call(
        body, name=name, out_shape=out_shape, grid=(ma // tm, nk),
        in_specs=[pl.BlockSpec((tk, tm), lambda i, kk: (kk, i)), pl.BlockSpec((tk, nb), lambda i, kk: (kk, 0))],
        out_specs=out_specs, scratch_shapes=[pltpu.VMEM((tm, nb), F32)],
        compiler_params=_params(("arbitrary", "arbitrary"), V7X_VMEM_LIMIT_BYTES),
    )(a, b)
    return res if colsum_b else res[0]


def _rmsnorm_fwd(name, x, g):
    t, d = x.shape
    tr = _pick(t, (512, 256, 128))

    def body(x_ref, g_ref, h_ref):
        xv = x_ref[...]
        r = lax.rsqrt(jnp.mean(xv * xv, axis=-1, keepdims=True) + EPS)
        h_ref[...] = (xv * r * g_ref[...]).astype(BF16)

    return pl.pallas_call(
        body, name=name, out_shape=jax.ShapeDtypeStruct((t, d), BF16), grid=(t // tr,),
        in_specs=[pl.BlockSpec((tr, d), lambda i: (i, 0)), pl.BlockSpec((1, d), lambda i: (0, 0))],
        out_specs=pl.BlockSpec((tr, d), lambda i: (i, 0)), compiler_params=_params(("parallel",)),
    )(x, g)


def _rmsnorm_bwd(name, dh, x, g, dx_out):
    t, d = x.shape
    tr = _pick(t, (512, 256, 128))

    def body(dh_ref, x_ref, g_ref, dxo_ref, dx_ref, dg_ref):
        xv = x_ref[...]
        r = lax.rsqrt(jnp.mean(xv * xv, axis=-1, keepdims=True) + EPS)
        yv = xv * r
        dhv = dh_ref[...].astype(F32)
        dy = dhv * g_ref[...]
        dx_ref[...] = dxo_ref[...] + r * (dy - yv * jnp.mean(dy * yv, axis=-1, keepdims=True))

        @pl.when(pl.program_id(0) == 0)
        def _():
            dg_ref[...] = jnp.zeros_like(dg_ref)

        dg_ref[...] += jnp.sum(dhv * yv, axis=0, keepdims=True)

    big = pl.BlockSpec((tr, d), lambda i: (i, 0))
    row = pl.BlockSpec((1, d), lambda i: (0, 0))
    return pl.pallas_call(
        body, name=name, out_shape=[jax.ShapeDtypeStruct((t, d), F32), jax.ShapeDtypeStruct((1, d), F32)],
        grid=(t // tr,), in_specs=[big, big, row, big], out_specs=[big, row],
        compiler_params=_params(("arbitrary",)),
    )(dh, x, g, dx_out)


def _loss_head(name, x, g, target):
    t, d = x.shape
    tr = _pick(t, (512, 256, 128))

    def body(x_ref, g_ref, t_ref, dx_ref, dg_ref, l_ref):
        xv = x_ref[...]
        r = lax.rsqrt(jnp.mean(xv * xv, axis=-1, keepdims=True) + EPS)
        yv = xv * r
        diff = yv * g_ref[...] - t_ref[...]
        dout = diff * (1.0 / d)
        dy = dout * g_ref[...]
        dx_ref[...] = r * (dy - yv * jnp.mean(dy * yv, axis=-1, keepdims=True))

        @pl.when(pl.program_id(0) == 0)
        def _():
            dg_ref[...] = jnp.zeros_like(dg_ref)
            l_ref[...] = jnp.zeros_like(l_ref)

        dg_ref[...] += jnp.sum(dout * yv, axis=0, keepdims=True)
        l_ref[...] += (0.5 / d) * jnp.sum(diff * diff, axis=0, keepdims=True)

    big = pl.BlockSpec((tr, d), lambda i: (i, 0))
    row = pl.BlockSpec((1, d), lambda i: (0, 0))
    return pl.pallas_call(
        body, name=name,
        out_shape=[jax.ShapeDtypeStruct((t, d), F32), jax.ShapeDtypeStruct((1, d), F32),
                   jax.ShapeDtypeStruct((1, d), F32)],
        grid=(t // tr,), in_specs=[big, row, big], out_specs=[big, row, row],
        compiler_params=_params(("arbitrary",)),
    )(x, g, target)


CONV_ROWS = 64


def _dwconv_fwd(name, glu, w_dw, b_dw, ln_g, ln_b):
    t, c = glu.shape
    tt = _pick(t, (256, 128))
    hb = tt // CONV_HALO

    def body(cur_ref, halo_ref, w_ref, b_ref, g_ref, be_ref, dw_ref, s_ref, win_ref):
        i = pl.program_id(0)
        halo = halo_ref[...].astype(F32)
        win_ref[0:CONV_HALO, :] = jnp.where(i > 0, halo, 0.0)
        win_ref[CONV_HALO:, :] = cur_ref[...].astype(F32)
        for r0 in range(0, tt, CONV_ROWS):
            for c0 in range(0, c, 128):
                acc = jnp.zeros((CONV_ROWS, 128), F32) + b_ref[:, c0:c0 + 128]
                for k in range(CONV_WIDTH):
                    o = r0 + k + CONV_HALO - (CONV_WIDTH - 1)
                    acc = acc + w_ref[k:k + 1, c0:c0 + 128] * win_ref[o:o + CONV_ROWS, c0:c0 + 128]
                dw_ref[r0:r0 + CONV_ROWS, c0:c0 + 128] = acc
        u = dw_ref[...]
        mu = jnp.mean(u, axis=-1, keepdims=True)
        uc = u - mu
        rstd = lax.rsqrt(jnp.mean(uc * uc, axis=-1, keepdims=True) + EPS)
        z = uc * rstd * g_ref[...] + be_ref[...]
        s_ref[...] = (z * _sigmoid(z)).astype(BF16)

    big = pl.BlockSpec((tt, c), lambda i: (i, 0))
    row = pl.BlockSpec((1, c), lambda i: (0, 0))
    return pl.pallas_call(
        body, name=name, out_shape=[jax.ShapeDtypeStruct((t, c), F32), jax.ShapeDtypeStruct((t, c), BF16)],
        grid=(t // tt,),
        in_specs=[big, pl.BlockSpec((CONV_HALO, c), lambda i: (jnp.maximum(i * hb - 1, 0), 0)),
                  pl.BlockSpec((CONV_HALO, c), lambda i: (0, 0)), row, row, row],
        out_specs=[big, big], scratch_shapes=[pltpu.VMEM((tt + CONV_HALO, c), F32)],
        compiler_params=_params(("parallel",)),
    )(glu, glu, w_dw, b_dw, ln_g, ln_b)


def _ln_silu_bwd(name, ds, dw, ln_g, ln_b):
    t, c = dw.shape
    tr = _pick(t, (256, 128))

    def body(ds_ref, dw_ref, g_ref, be_ref, o_ref, acc_ref):
        u = dw_ref[...]
        mu = jnp.mean(u, axis=-1, keepdims=True)
        uc = u - mu
        rstd = lax.rsqrt(jnp.mean(uc * uc, axis=-1, keepdims=True) + EPS)
        xh = uc * rstd
        z = xh * g_ref[...] + be_ref[...]
        sg = _sigmoid(z)
        dz = ds_ref[...].astype(F32) * (sg * (1.0 + z * (1.0 - sg)))
        dxh = dz * g_ref[...]
        du = rstd * (dxh - jnp.mean(dxh, axis=-1, keepdims=True) - xh * jnp.mean(dxh * xh, axis=-1, keepdims=True))
        o_ref[...] = du

        @pl.when(pl.program_id(0) == 0)
        def _():
            acc_ref[...] = jnp.zeros_like(acc_ref)

        acc_ref[0:1, :] += jnp.sum(dz * xh, axis=0, keepdims=True)
        acc_ref[1:2, :] += jnp.sum(dz, axis=0, keepdims=True)
        acc_ref[2:3, :] += jnp.sum(du, axis=0, keepdims=True)

    big = pl.BlockSpec((tr, c), lambda i: (i, 0))
    row = pl.BlockSpec((1, c), lambda i: (0, 0))
    return pl.pallas_call(
        body, name=name, out_shape=[jax.ShapeDtypeStruct((t, c), F32), jax.ShapeDtypeStruct((8, c), F32)],
        grid=(t // tr,), in_specs=[big, big, row, row],
        out_specs=[big, pl.BlockSpec((8, c), lambda i: (0, 0))], compiler_params=_params(("arbitrary",)),
    )(ds, dw, ln_g, ln_b)


def _dwconv_bwd(name, ddw, a, gt, w_dw):
    t, c = ddw.shape
    tt = _pick(t, (256, 128))
    hb = tt // CONV_HALO
    last = t // tt - 1
    back = CONV_WIDTH - 1

    def body(d_ref, dn_ref, a_ref, ap_ref, g_ref, gp_ref, w_ref, du_ref, dwk_ref, db_ref, wd_ref, wg_ref, dg_ref):
        i = pl.program_id(0)
        wd_ref[0:tt, :] = d_ref[...]
        wd_ref[tt:, :] = jnp.where(i < last, dn_ref[...], 0.0)
        glu_prev = ap_ref[...].astype(F32) * _sigmoid(gp_ref[...].astype(F32))
        wg_ref[0:CONV_HALO, :] = jnp.where(i > 0, glu_prev, 0.0)
        av = a_ref[...].astype(F32)
        sg = _sigmoid(g_ref[...].astype(F32))
        wg_ref[CONV_HALO:, :] = av * sg

        @pl.when(i == 0)
        def _():
            dwk_ref[...] = jnp.zeros_like(dwk_ref)
            db_ref[...] = jnp.zeros_like(db_ref)

        for r0 in range(0, tt, CONV_ROWS):
            for c0 in range(0, c, 128):
                dcur = wd_ref[r0:r0 + CONV_ROWS, c0:c0 + 128]
                acc = jnp.zeros((CONV_ROWS, 128), F32)
                for k in range(CONV_WIDTH):
                    o = r0 + back - k
                    acc = acc + w_ref[k:k + 1, c0:c0 + 128] * wd_ref[o:o + CONV_ROWS, c0:c0 + 128]
                    og = r0 + k + CONV_HALO - back
                    dwk_ref[k:k + 1, c0:c0 + 128] += jnp.sum(
                        dcur * wg_ref[og:og + CONV_ROWS, c0:c0 + 128], axis=0, keepdims=True)
                dg_ref[r0:r0 + CONV_ROWS, c0:c0 + 128] = acc
        dglu = dg_ref[...]
        da = dglu * sg
        dgate = dglu * av * sg * (1.0 - sg)
        du_ref[:, 0:c] = da.astype(BF16)
        du_ref[:, c:] = dgate.astype(BF16)
        db_ref[:, 0:c] += jnp.sum(da, axis=0, keepdims=True)
        db_ref[:, c:] += jnp.sum(dgate, axis=0, keepdims=True)

    big = pl.BlockSpec((tt, c), lambda i: (i, 0))
    prev = pl.BlockSpec((CONV_HALO, c), lambda i: (jnp.maximum(i * hb - 1, 0), 0))
    nxt = pl.BlockSpec((CONV_HALO, c), lambda i: (jnp.minimum((i + 1) * hb, t // CONV_HALO - 1), 0))
    return pl.pallas_call(
        body, name=name,
        out_shape=[jax.ShapeDtypeStruct((t, 2 * c), BF16), jax.ShapeDtypeStruct((CONV_HALO, c), F32),
                   jax.ShapeDtypeStruct((1, 2 * c), F32)],
        grid=(t // tt,),
        in_specs=[big, nxt, big, prev, big, prev, pl.BlockSpec((CONV_HALO, c), lambda i: (0, 0))],
        out_specs=[pl.BlockSpec((tt, 2 * c), lambda i: (i, 0)), pl.BlockSpec((CONV_HALO, c), lambda i: (0, 0)),
                   pl.BlockSpec((1, 2 * c), lambda i: (0, 0))],
        scratch_shapes=[pltpu.VMEM((tt + CONV_HALO, c), F32), pltpu.VMEM((tt + CONV_HALO, c), F32),
                        pltpu.VMEM((tt, c), F32)],
        compiler_params=_params(("arbitrary",)),
    )(ddw, ddw, a, a, gt, gt, w_dw)


def _bucket_tables():
    i = np.arange(N_BACK)[:, None]
    j = np.arange(2 * N_BACK)[None, :]
    dist = i + N_BACK - j
    valid = (dist >= 0) & (dist <= N_BACK)
    max_exact = N_BUCKETS // 2
    out = []
    for d in DILATIONS:
        n = np.maximum(dist * d, 0)
        nf = np.maximum(n, 1).astype(np.float32)
        large = max_exact + (np.log(nf / np.float32(max_exact)) / np.float32(math.log(REL_MAX_DISTANCE / max_exact))
                             * np.float32(N_BUCKETS - max_exact)).astype(np.int32)
        large = np.minimum(large, N_BUCKETS - 1)
        out.append(np.where(valid, np.where(n < max_exact, n, large), -1))
    return np.stack(out).astype(np.int32)


def _bias_build(name, rel_bias, buckets):
    def body(tbl_ref, bk_ref, o_ref):
        g = pl.program_id(0)
        bk = bk_ref[0]
        for h in range(HEADS_PER_GROUP):
            acc = jnp.zeros(bk.shape, F32)
            for b in range(N_BUCKETS):
                acc = jnp.where(bk == b, tbl_ref[b, g * HEADS_PER_GROUP + h], acc)
            o_ref[h] = jnp.where(bk < 0, NEG_INF, acc)

    return pl.pallas_call(
        body, name=name, out_shape=jax.ShapeDtypeStruct((N_HEADS, N_BACK, 2 * N_BACK), F32), grid=(N_GROUPS,),
        in_specs=[pl.BlockSpec(memory_space=pltpu.SMEM), pl.BlockSpec((1, N_BACK, 2 * N_BACK), lambda g: (g, 0, 0))],
        out_specs=pl.BlockSpec((HEADS_PER_GROUP, N_BACK, 2 * N_BACK), lambda g: (g, 0, 0)),
        compiler_params=_params(("arbitrary",)),
    )(rel_bias, buckets)


def _bias_grad(name, dbs, buckets):
    nd = len(dbs)

    def body(*refs):
        bk = refs[nd][0]
        o_ref = refs[nd + 1]
        lane = lax.broadcasted_iota(jnp.int32, (1, 128), 1)
        db = [sum(r[h] for r in refs[:nd]) for h in range(HEADS_PER_GROUP)]
        for b in range(N_BUCKETS):
            row = jnp.zeros((1, 128), F32)
            for h in range(HEADS_PER_GROUP):
                s = jnp.sum(jnp.where(bk == b, db[h], 0.0), axis=0, keepdims=True)
                s = jnp.sum(s, axis=1, keepdims=True)
                row = jnp.where(lane // 32 == h, s, row)
            o_ref[0, b:b + 1, :] = row

    spec = pl.BlockSpec((HEADS_PER_GROUP, N_BACK, 2 * N_BACK), lambda g: (g, 0, 0))
    return pl.pallas_call(
        body, name=name, out_shape=jax.ShapeDtypeStruct((N_GROUPS, N_BUCKETS, 128), F32), grid=(N_GROUPS,),
        in_specs=[spec] * nd + [pl.BlockSpec((1, N_BACK, 2 * N_BACK), lambda g: (g, 0, 0))],
        out_specs=pl.BlockSpec((1, N_BUCKETS, 128), lambda g: (g, 0, 0)), compiler_params=_params(("arbitrary",)),
    )(*dbs, buckets)


def _head_cols(h):
    return slice(h * HEAD_DIM, (h + 1) * HEAD_DIM)


def _attn_fwd(name, qkv, bias, g):
    t = qkv.shape[0]
    d = DILATIONS[g]
    tq = t // d
    nblk = qkv.shape[1] // GROUP_COLS
    scale = HEAD_DIM ** -0.5

    def body(q_ref, kp_ref, kc_ref, vp_ref, vc_ref, b_ref, o_ref, l_ref):
        n = pl.program_id(1)
        col = lax.broadcasted_iota(jnp.int32, (N_BACK, 2 * N_BACK), 1)
        keep = (col >= N_BACK) | (n > 0)
        lane = lax.broadcasted_iota(jnp.int32, (N_BACK, 128), 1)
        lse_tile = jnp.zeros((N_BACK, 128), F32)
        outs = []
        for h in range(HEADS_PER_GROUP):
            hc = _head_cols(h)
            kk = jnp.concatenate([kp_ref[:, hc], kc_ref[:, hc]], axis=0)
            vv = jnp.concatenate([vp_ref[:, hc], vc_ref[:, hc]], axis=0)
            s = lax.dot_general(q_ref[:, hc], kk, (((1,), (1,)), ((), ())), preferred_element_type=F32)
            s = jnp.where(keep, s * scale + b_ref[h], NEG_INF)
            m = jnp.max(s, axis=-1, keepdims=True)
            p = jnp.exp(s - m)
            den = jnp.sum(p, axis=-1, keepdims=True)
            outs.append(jnp.dot(p.astype(BF16), vv, preferred_element_type=F32) / den)
            lse_tile = jnp.where(lane // 32 == h, m + jnp.log(den), lse_tile)
        o_ref[...] = jnp.concatenate(outs, axis=1)
        l_ref[...] = lse_tile

    def blk(part, prev):
        if prev:
            return pl.BlockSpec((N_BACK, GROUP_COLS), lambda r, n: (jnp.maximum(n - 1, 0), r * nblk + 3 * part + g))
        return pl.BlockSpec((N_BACK, GROUP_COLS), lambda r, n: (n, r * nblk + 3 * part + g))

    qv = qkv.reshape(tq, d * qkv.shape[1])
    o, l = pl.pallas_call(
        body, name=name,
        out_shape=[jax.ShapeDtypeStruct((tq, d * GROUP_COLS), F32), jax.ShapeDtypeStruct((tq, d * 128), F32)],
        grid=(d, tq // N_BACK),
        in_specs=[blk(0, False), blk(1, True), blk(1, False), blk(2, True), blk(2, False),
                  pl.BlockSpec((HEADS_PER_GROUP, N_BACK, 2 * N_BACK), lambda r, n: (g, 0, 0))],
        out_specs=[pl.BlockSpec((N_BACK, GROUP_COLS), lambda r, n: (n, r)),
                   pl.BlockSpec((N_BACK, 128), lambda r, n: (n, r))],
        compiler_params=_params(("parallel", "parallel")),
    )(qv, qv, qv, qv, qv, bias)
    return o.reshape(t, GROUP_COLS), l.reshape(t, 128)


def _group_weights(l_refs, h):
    ls = [l_ref[:, 32 * h:32 * h + 1] for l_ref in l_refs]
    m = jnp.maximum(jnp.maximum(ls[0], ls[1]), ls[2])
    es = [jnp.exp(l - m) for l in ls]
    tot = es[0] + es[1] + es[2]
    return [e / tot for e in es]


def _attn_merge(name, os_, ls):
    t = os_[0].shape[0]
    tr = _pick(t, (512, 256, 128))

    def body(o0, o1, o2, l0, l1, l2, out_ref):
        o_refs = (o0, o1, o2)
        pieces = [[None] * HEADS_PER_GROUP for _ in range(N_GROUPS)]
        for h in range(HEADS_PER_GROUP):
            al = _group_weights((l0, l1, l2), h)
            for g in range(N_GROUPS):
                pieces[g][h] = o_refs[g][:, _head_cols(h)] * al[g]
        out_ref[...] = jnp.concatenate([p for row in pieces for p in row], axis=1).astype(BF16)

    so = pl.BlockSpec((tr, GROUP_COLS), lambda i: (i, 0))
    sl = pl.BlockSpec((tr, 128), lambda i: (i, 0))
    return pl.pallas_call(
        body, name=name, out_shape=jax.ShapeDtypeStruct((t, D_ATTN), BF16), grid=(t // tr,),
        in_specs=[so] * 3 + [sl] * 3, out_specs=pl.BlockSpec((tr, D_ATTN), lambda i: (i, 0)),
        compiler_params=_params(("parallel",)),
    )(*os_, *ls)


def _attn_bwd_prep(name, d_out, os_, ls):
    t = d_out.shape[0]
    tr = _pick(t, (512, 256, 128))

    def body(do_ref, o0, o1, o2, l0, l1, l2, d0, d1, d2, c0, c1, c2):
        o_refs, d_refs, c_refs = (o0, o1, o2), (d0, d1, d2), (c0, c1, c2)
        lane = lax.broadcasted_iota(jnp.int32, (tr, 128), 1)
        dos = [[None] * HEADS_PER_GROUP for _ in range(N_GROUPS)]
        cs = [jnp.zeros((tr, 128), F32) for _ in range(N_GROUPS)]
        for h in range(HEADS_PER_GROUP):
            al = _group_weights((l0, l1, l2), h)
            tot = jnp.zeros((tr, 1), F32)
            for g in range(N_GROUPS):
                dv = do_ref[:, g * GROUP_COLS + h * HEAD_DIM:g * GROUP_COLS + (h + 1) * HEAD_DIM].astype(F32)
                tot = tot + al[g] * jnp.sum(dv * o_refs[g][:, _head_cols(h)], axis=-1, keepdims=True)
                dos[g][h] = dv * al[g]
            for g in range(N_GROUPS):
                cs[g] = jnp.where(lane // 32 == h, -al[g] * tot, cs[g])
        for g in range(N_GROUPS):
            d_refs[g][...] = jnp.concatenate(dos[g], axis=1).astype(BF16)
            c_refs[g][...] = cs[g]

    so = pl.BlockSpec((tr, GROUP_COLS), lambda i: (i, 0))
    sl = pl.BlockSpec((tr, 128), lambda i: (i, 0))
    res = pl.pallas_call(
        body, name=name,
        out_shape=[jax.ShapeDtypeStruct((t, GROUP_COLS), BF16)] * 3 + [jax.ShapeDtypeStruct((t, 128), F32)] * 3,
        grid=(t // tr,), in_specs=[pl.BlockSpec((tr, D_ATTN), lambda i: (i, 0))] + [so] * 3 + [sl] * 3,
        out_specs=[so] * 3 + [sl] * 3, compiler_params=_params(("parallel",)),
    )(d_out, *os_, *ls)
    return res[:3], res[3:]


def _attn_bwd(name, qkv, do, lse, cterm, bias, g):
    t = qkv.shape[0]
    d = DILATIONS[g]
    tq = t // d
    nb = tq // N_BACK
    nblk = qkv.shape[1] // GROUP_COLS
    scale = HEAD_DIM ** -0.5
    nt = (((1,), (1,)), ((), ()))
    tn = (((0,), (0,)), ((), ()))

    def body(qn, qx, kp, kn, vp, vn, don, dox, ln, lx, cn, cx, b_ref, dqkv_ref, db_ref):
        n = pl.program_id(1)
        has_prev = n > 0
        has_next = n < nb - 1

        @pl.when((n == 0) & (pl.program_id(0) == 0))
        def _():
            db_ref[...] = jnp.zeros_like(db_ref)

        dqs, dks, dvs = [], [], []
        for h in range(HEADS_PER_GROUP):
            hc = _head_cols(h)
            st = slice(32 * h, 32 * h + 1)
            b_prev, b_same = b_ref[h, :, 0:N_BACK], b_ref[h, :, N_BACK:]

            def pair(q, k, v, dout, l, cc, bias_blk, on):
                s = lax.dot_general(q, k, nt, preferred_element_type=F32) * scale + bias_blk
                p = jnp.where(on, jnp.exp(s - l), 0.0)
                dp = lax.dot_general(dout, v, nt, preferred_element_type=F32)
                return p, p * (dp + cc)

            q0, q1, k0, k1, v0, v1 = qn[:, hc], qx[:, hc], kp[:, hc], kn[:, hc], vp[:, hc], vn[:, hc]
            d0, d1 = don[:, hc], dox[:, hc]
            p_a, ds_a = pair(q0, k1, v1, d0, ln[:, st], cn[:, st], b_same, True)
            p_b, ds_b = pair(q0, k0, v0, d0, ln[:, st], cn[:, st], b_prev, has_prev)
            p_c, ds_c = pair(q1, k1, v1, d1, lx[:, st], cx[:, st], b_prev, has_next)
            ds_a16, ds_b16, ds_c16 = ds_a.astype(BF16), ds_b.astype(BF16), ds_c.astype(BF16)
            dqs.append(scale * (jnp.dot(ds_a16, k1, preferred_element_type=F32)
                                + jnp.dot(ds_b16, k0, preferred_element_type=F32)))
            dks.append(scale * (lax.dot_general(ds_a16, q0, tn, preferred_element_type=F32)
                                + lax.dot_general(ds_c16, q1, tn, preferred_element_type=F32)))
            dvs.append(lax.dot_general(p_a.astype(BF16), d0, tn, preferred_element_type=F32)
                       + lax.dot_general(p_c.astype(BF16), d1, tn, preferred_element_type=F32))
            db_ref[h, :, 0:N_BACK] += ds_b
            db_ref[h, :, N_BACK:] += ds_a
        dqkv_ref[...] = jnp.concatenate(dqs + dks + dvs, axis=1).astype(BF16)

    def rows(which):
        if which == "prev":
            return lambda n: jnp.maximum(n - 1, 0)
        if which == "next":
            return lambda n: jnp.minimum(n + 1, nb - 1)
        return lambda n: n

    def qkv_blk(part, which):
        f = rows(which)
        return pl.BlockSpec((N_BACK, GROUP_COLS), lambda r, n: (f(n), r * nblk + 3 * part + g))

    def grp_blk(width, which):
        f = rows(which)
        return pl.BlockSpec((N_BACK, width), lambda r, n: (f(n), r))

    qv = qkv.reshape(tq, d * qkv.shape[1])
    dov = do.reshape(tq, d * GROUP_COLS)
    lv = lse.reshape(tq, d * 128)
    cv = cterm.reshape(tq, d * 128)
    dqkv_g, db = pl.pallas_call(
        body, name=name,
        out_shape=[jax.ShapeDtypeStruct((tq, d * 3 * GROUP_COLS), BF16),
                   jax.ShapeDtypeStruct((HEADS_PER_GROUP, N_BACK, 2 * N_BACK), F32)],
        grid=(d, nb),
        in_specs=[qkv_blk(0, "same"), qkv_blk(0, "next"), qkv_blk(1, "prev"), qkv_blk(1, "same"),
                  qkv_blk(2, "prev"), qkv_blk(2, "same"), grp_blk(GROUP_COLS, "same"), grp_blk(GROUP_COLS, "next"),
                  grp_blk(128, "same"), grp_blk(128, "next"), grp_blk(128, "same"), grp_blk(128, "next"),
                  pl.BlockSpec((HEADS_PER_GROUP, N_BACK, 2 * N_BACK), lambda r, n: (g, 0, 0))],
        out_specs=[pl.BlockSpec((N_BACK, 3 * GROUP_COLS), lambda r, n: (n, r)),
                   pl.BlockSpec((HEADS_PER_GROUP, N_BACK, 2 * N_BACK), lambda r, n: (0, 0, 0))],
        compiler_params=_params(("arbitrary", "arbitrary")),
    )(qv, qv, qv, qv, qv, qv, dov, dov, lv, lv, cv, cv, bias)
    return dqkv_g.reshape(t, 3 * GROUP_COLS), db


def _row(v):
    return v.reshape(1, -1)


def _plain(accs, extras, rows):
    return (accs[0],)


def _glu_epi(accs, extras, rows):
    a = (accs[0] + rows[0]).astype(BF16)
    gt = (accs[1] + rows[1]).astype(BF16)
    return a, gt, a.astype(F32) * _sigmoid(gt.astype(F32))


def _swiglu_epi(accs, extras, rows):
    gq, uq = accs[0].astype(BF16), accs[1].astype(BF16)
    gf = gq.astype(F32)
    return gq, uq, gf * _sigmoid(gf) * uq.astype(F32)


def _swiglu_bwd_epi(accs, extras, rows):
    gf, uf = extras[0].astype(F32), extras[1].astype(F32)
    sg = _sigmoid(gf)
    return accs[0] * uf * (sg * (1.0 + gf * (1.0 - sg))), accs[0] * gf * sg


def _residual(acc, extras, rows):
    out = acc + extras[0]
    return out + rows[0] if rows else out


def _identity(acc, extras, rows):
    return acc


def _group_rows(w):
    parts = [w[p * D_ATTN:(p + 1) * D_ATTN].reshape(N_GROUPS, GROUP_COLS, -1) for p in range(3)]
    return jnp.concatenate(parts, axis=1)


def _ungroup_rows(wg):
    return jnp.concatenate([wg[g][p * GROUP_COLS:(p + 1) * GROUP_COLS] for p in range(3) for g in range(N_GROUPS)],
                           axis=0)


def _local_step(x, target, sm, depth, fetch, emit):
    d_model = x.shape[1]
    buckets = jnp.asarray(_bucket_tables())
    bias = _bias_build("bias_build", sm["rel_bias"], buckets)
    saved = []
    for i in range(depth):
        j = i // 2
        rec = {"x_mix": x}
        wm = fetch(2 * i, x)
        h = _rmsnorm_fwd(f"rms_mix_fwd{i}", x, _row(sm["norm_mix"][i]))
        rec.update(h_mix=h, wm=wm)
        if i % 2 == 0:
            c = wm["w2"].shape[0]
            tn = _pick(c, (512, 256, 128))
            b1 = sm["conv_b_pw1"][j]
            a, gt, glu = _mm_nt(f"conv_pw1_fwd{j}", h, [wm["w1t"]] * 2, [0, c // tn], c, _glu_epi, (BF16,) * 3,
                                rows=[_row(b1[:c]), _row(b1[c:])], tn=tn)
            dw, s = _dwconv_fwd(f"dwconv_fwd{j}", glu, wm["wdw"], _row(sm["conv_b_dw"][j]),
                                _row(sm["conv_ln_g"][j]), _row(sm["conv_ln_b"][j]))
            x = _mm_nn(f"conv_pw2_fwd{j}", [s], [wm["w2"]], _residual, F32, extras=[x],
                       rows=[_row(sm["conv_b_pw2"][j])])
            rec.update(a=a, gt=gt, dw=dw, s=s)
        else:
            qkv = _mm_nt(f"attn_qkv_fwd{j}", h, [wm["wqkvt"]], [0], 3 * D_ATTN, _plain, (BF16,))[0]
            og = [_attn_fwd(f"attn_fwd{j}_{g}", qkv, bias, g) for g in range(N_GROUPS)]
            os_, ls = [o for o, _ in og], [l for _, l in og]
            om = _attn_merge(f"attn_merge{j}", os_, ls)
            x = _mm_nt(f"attn_out_fwd{j}", om, [wm["wot"]], [0], d_model, lambda accs, e, r: (accs[0] + e[0],),
                       (F32,), extras=[x])[0]
            rec.update(qkv=qkv, os=os_, ls=ls, om=om)
        rec["x_ffn"] = x
        wf = fetch(2 * i + 1, x)
        h2 = _rmsnorm_fwd(f"rms_ffn_fwd{i}", x, _row(sm["norm_ffn"][i]))
        f = wf["wd"].shape[0]
        gq, uq, act = _mm_nt(f"ffn_up_fwd{i}", h2, [wf["wgt"], wf["wut"]], [0, 0], f, _swiglu_epi, (BF16,) * 3)
        x = _mm_nn(f"ffn_down_fwd{i}", [act], [wf["wd"]], _residual, F32, extras=[x])
        rec.update(h_ffn=h2, gq=gq, uq=uq, act=act, wf=wf)
        saved.append(rec)

    dx, g_final, loss_cols = _loss_head("loss_head", x, _row(sm["final_norm"]), target)

    g_mix, g_ffn = [None] * depth, [None] * depth
    nconv = (depth + 1) // 2
    g_b1, g_bdw, g_lng, g_lnb, g_b2 = ([None] * nconv for _ in range(5))
    dbias = []
    tok = jnp.zeros((8, 128), F32)
    for i in reversed(range(depth)):
        j = i // 2
        rec = saved[i]
        wm, wf = rec["wm"], rec["wf"]
        f = wf["wd"].shape[0]
        dgate, dup = _mm_nt(f"ffn_down_bwd{i}", dx, [wf["wd"]], [0], f, _swiglu_bwd_epi, (BF16, BF16),
                            extras=[rec["gq"], rec["uq"]])
        gf = {"wd": _mm_tn(f"ffn_down_dw{i}", rec["act"], dx),
              "wgt": _mm_tn(f"ffn_gate_dw{i}", dgate, rec["h_ffn"]),
              "wut": _mm_tn(f"ffn_up_dw{i}", dup, rec["h_ffn"])}
        dh = _mm_nn(f"ffn_up_bwd{i}", [dgate, dup], [wf["wgt"], wf["wut"]], _identity, BF16, tm=256)
        dx, g_ffn[i] = _rmsnorm_bwd(f"rms_ffn_bwd{i}", dh, rec["x_ffn"], _row(sm["norm_ffn"][i]) + tok[0:1, 0:1], dx)
        tok = emit(2 * i + 1, gf)
        if i % 2 == 0:
            c = wm["w2"].shape[0]
            gm = {}
            gm["w2"], g_b2[j] = _mm_tn(f"conv_pw2_dw{j}", rec["s"], dx, colsum_b=True)
            ds = _mm_nt(f"conv_pw2_bwd{j}", dx, [wm["w2"]], [0], c, _plain, (BF16,))[0]
            ddw, sums = _ln_silu_bwd(f"ln_silu_bwd{j}", ds, rec["dw"], _row(sm["conv_ln_g"][j]),
                                     _row(sm["conv_ln_b"][j]))
            g_lng[j], g_lnb[j], g_bdw[j] = sums[0], sums[1], sums[2]
            du, dwk, db1 = _dwconv_bwd(f"dwconv_bwd{j}", ddw, rec["a"], rec["gt"], wm["wdw"])
            gm["wdw"] = dwk[:CONV_WIDTH]
            g_b1[j] = db1[0]
            gm["w1t"] = _mm_tn(f"conv_pw1_dw{j}", du, rec["h_mix"])
            dh = _mm_nn(f"conv_pw1_bwd{j}", [du], [wm["w1t"]], _identity, BF16)
        else:
            gm = {"wot": _mm_tn(f"attn_out_dw{j}", dx, rec["om"])}
            d_om = _mm_nn(f"attn_out_bwd{j}", [dx], [wm["wot"]], _identity, BF16)
            dos, cs = _attn_bwd_prep(f"attn_bwd_prep{j}", d_om, rec["os"], rec["ls"])
            back = [_attn_bwd(f"attn_bwd{j}_{g}", rec["qkv"], dos[g], rec["ls"][g], cs[g], bias, g)
                    for g in range(N_GROUPS)]
            dqkv = [b[0] for b in back]
            dbias.append(jnp.concatenate([b[1] for b in back], axis=0))
            wq = _group_rows(wm["wqkvt"])
            dh = _mm_nn(f"attn_qkv_bwd{j}", dqkv, [wq[g] for g in range(N_GROUPS)], _identity, BF16)
            gm["wqkvt"] = _ungroup_rows([_mm_tn(f"attn_qkv_dw{j}_{g}", dqkv[g], rec["h_mix"])
                                         for g in range(N_GROUPS)])
        dx, g_mix[i] = _rmsnorm_bwd(f"rms_mix_bwd{i}", dh, rec["x_mix"], _row(sm["norm_mix"][i]) + tok[0:1, 0:1], dx)
        tok = emit(2 * i, gm)

    gb = _bias_grad("bias_grad", dbias, buckets)
    g_rel = jnp.transpose(gb[:, :, ::32], (1, 0, 2)).reshape(N_BUCKETS, N_HEADS)
    gsm = {
        "norm_mix": jnp.concatenate(g_mix, axis=0), "norm_ffn": jnp.concatenate(g_ffn, axis=0),
        "final_norm": g_final[0], "conv_b_pw1": jnp.stack(g_b1), "conv_b_dw": jnp.stack(g_bdw),
        "conv_ln_g": jnp.stack(g_lng), "conv_ln_b": jnp.stack(g_lnb),
        "conv_b_pw2": jnp.concatenate(g_b2, axis=0), "rel_bias": g_rel,
    }
    return loss_cols, dx, gsm


SMALL = ("norm_mix", "norm_ffn", "final_norm", "conv_b_pw1", "conv_b_dw", "conv_ln_g", "conv_ln_b", "conv_b_pw2",
         "rel_bias")
SHARDED = (("conv_w_pw1", "w1t", True), ("conv_w_pw2", "w2", False), ("attn_w_qkv", "wqkvt", True),
           ("attn_w_o", "wot", True), ("ffn_w_gate", "wgt", True), ("ffn_w_up", "wut", True),
           ("ffn_w_down", "wd", False))
ORDER = ("norm_mix", "norm_ffn", "final_norm", "conv_w_pw1", "conv_b_pw1", "conv_w_dw", "conv_b_dw", "conv_ln_g",
         "conv_ln_b", "conv_w_pw2", "conv_b_pw2", "attn_w_qkv", "attn_w_o", "rel_bias", "ffn_w_gate", "ffn_w_up",
         "ffn_w_down")
PACK_LANES = 128
PACK_ROW_TILE = 8


def _pack_small(vals):
    flat = jnp.concatenate([vals[n].reshape(-1) for n in SMALL])
    per_tile = PACK_LANES * PACK_ROW_TILE
    return jnp.pad(flat, (0, -flat.shape[0] % per_tile)).reshape(-1, PACK_LANES)


def _unpack_small(pack, like):
    flat, out, pos = pack.reshape(-1), {}, 0
    for n in SMALL:
        out[n] = flat[pos:pos + like[n].size].reshape(like[n].shape)
        pos += like[n].size
    return out


def _dw_blocks(w):
    l, k, c = w.shape
    blk = jnp.transpose(w.reshape(l, k, N_DEV, c // N_DEV), (2, 0, 1, 3)).reshape(N_DEV, l * k, c // N_DEV)
    return jnp.pad(blk, ((0, 0), (0, -(l * k) % 8), (0, 0)))


def kernel(x, norm_mix, norm_ffn, final_norm, conv_w_pw1, conv_b_pw1, conv_w_dw, conv_b_dw, conv_ln_g, conv_ln_b, conv_w_pw2, conv_b_pw2, attn_w_qkv, attn_w_o, rel_bias, ffn_w_gate, ffn_w_up, ffn_w_down, loss_target, m_norm_mix, m_norm_ffn, m_final_norm, m_conv_w_pw1, m_conv_b_pw1, m_conv_w_dw, m_conv_b_dw, m_conv_ln_g, m_conv_ln_b, m_conv_w_pw2, m_conv_b_pw2, m_attn_w_qkv, m_attn_w_o, m_rel_bias, m_ffn_w_gate, m_ffn_w_up, m_ffn_w_down, v_norm_mix, v_norm_ffn, v_final_norm, v_conv_w_pw1, v_conv_b_pw1, v_conv_w_dw, v_conv_b_dw, v_conv_ln_g, v_conv_ln_b, v_conv_w_pw2, v_conv_b_pw2, v_attn_w_qkv, v_attn_w_o, v_rel_bias, v_ffn_w_gate, v_ffn_w_up, v_ffn_w_down):
    w = dict(norm_mix=norm_mix, norm_ffn=norm_ffn, final_norm=final_norm, conv_w_pw1=conv_w_pw1,
             conv_b_pw1=conv_b_pw1, conv_w_dw=conv_w_dw, conv_b_dw=conv_b_dw, conv_ln_g=conv_ln_g,
             conv_ln_b=conv_ln_b, conv_w_pw2=conv_w_pw2, conv_b_pw2=conv_b_pw2, attn_w_qkv=attn_w_qkv,
             attn_w_o=attn_w_o, rel_bias=rel_bias, ffn_w_gate=ffn_w_gate, ffn_w_up=ffn_w_up, ffn_w_down=ffn_w_down)
    m = dict(norm_mix=m_norm_mix, norm_ffn=m_norm_ffn, final_norm=m_final_norm, conv_w_pw1=m_conv_w_pw1,
             conv_b_pw1=m_conv_b_pw1, conv_w_dw=m_conv_w_dw, conv_b_dw=m_conv_b_dw, conv_ln_g=m_conv_ln_g,
             conv_ln_b=m_conv_ln_b, conv_w_pw2=m_conv_w_pw2, conv_b_pw2=m_conv_b_pw2, attn_w_qkv=m_attn_w_qkv,
             attn_w_o=m_attn_w_o, rel_bias=m_rel_bias, ffn_w_gate=m_ffn_w_gate, ffn_w_up=m_ffn_w_up,
             ffn_w_down=m_ffn_w_down)
    v = dict(norm_mix=v_norm_mix, norm_ffn=v_norm_ffn, final_norm=v_final_norm, conv_w_pw1=v_conv_w_pw1,
             conv_b_pw1=v_conv_b_pw1, conv_w_dw=v_conv_w_dw, conv_b_dw=v_conv_b_dw, conv_ln_g=v_conv_ln_g,
             conv_ln_b=v_conv_ln_b, conv_w_pw2=v_conv_w_pw2, conv_b_pw2=v_conv_b_pw2, attn_w_qkv=v_attn_w_qkv,
             attn_w_o=v_attn_w_o, rel_bias=v_rel_bias, ffn_w_gate=v_ffn_w_gate, ffn_w_up=v_ffn_w_up,
             ffn_w_down=v_ffn_w_down)

    me = 4 * lax.axis_index("x") + 2 * lax.axis_index("y") + lax.axis_index("c")
    depth = ffn_w_gate.shape[0]
    n_conv, _, cb = conv_w_dw.shape

    def sublayer(key, layer):
        if key in ("wgt", "wut", "wd"):
            return 2 * layer + 1
        return 4 * layer if key in ("w1t", "w2") else 4 * layer + 2

    def landing(block, own):
        land = lax.empty((N_DEV,) + block.shape, block.dtype)
        return lax.dynamic_update_slice(land, own[None], (me,) + (0,) * block.ndim)

    by_sub = {s: [] for s in range(2 * depth)}
    for name, key, cols in SHARDED:
        sw = (jnp.swapaxes(w[name], 1, 2) if cols else w[name]).astype(BF16)
        for layer in range(sw.shape[0]):
            by_sub[sublayer(key, layer)].append((key, layer, sw[layer]))
    likes = {s: jnp.zeros((sum(sh.size for _, _, sh in by_sub[s]) // 1024, 1024), BF16) for s in by_sub}
    dw_shard = jnp.pad(conv_w_dw.reshape(-1, cb), ((0, -(n_conv * CONV_WIDTH) % 8), (0, 0)))
    like_dw = jnp.zeros(dw_shard.shape, F32)
    groups = [([dw_shard], [landing(dw_shard, dw_shard)])]
    for s in range(2 * depth):
        groups.append(([sh for _, _, sh in by_sub[s]], [landing(sh, sh) for _, _, sh in by_sub[s]]))
    gather, _ = _exchange_start("gather_start", groups, scatter=False)
    dw_filters = []

    def fetch(s, after):
        pieces = [(gather[s + 1], likes[s])]
        if s == 0:
            pieces.append((gather[0], like_dw))
        landed = _exchange_wait(f"gather_wait{s}", pieces, after)
        out = {key: g.reshape(g.shape[0] * g.shape[1], g.shape[2]) for (key, _, _), g in zip(by_sub[s], landed[0])}
        if s == 0:
            dw_all = landed[1][0]
            full = jnp.transpose(dw_all[:, :n_conv * CONV_WIDTH].reshape(N_DEV, n_conv, CONV_WIDTH, cb), (1, 2, 0, 3))
            full = jnp.pad(full.reshape(n_conv, CONV_WIDTH, N_DEV * cb), ((0, 0), (0, CONV_HALO - CONV_WIDTH), (0, 0)))
            dw_filters.extend(full[layer] for layer in range(n_conv))
        if "w1t" in out:
            out["wdw"] = dw_filters[s // 4]
        return out

    scatter, dw_grads = {}, {}

    def emit(s, gd):
        parts = [gd[key].reshape(N_DEV, -1, gd[key].shape[1]) for key, _, _ in by_sub[s]]
        if "wdw" in gd:
            dw_grads[s // 4] = gd["wdw"]
        if s == 0:
            parts.append(_dw_blocks(jnp.stack([dw_grads[layer] for layer in range(n_conv)])))
        lands = [landing(p[0], lax.dynamic_index_in_dim(p, me, 0, keepdims=False)) for p in parts]
        groups = [(parts[:len(by_sub[s])], lands[:len(by_sub[s])])]
        if s == 0:
            groups.append((parts[-1:], lands[-1:]))
        scatter[s], token = _exchange_start(f"scatter_start{s}", groups, scatter=True)
        return token

    sm = {n: w[n] for n in SMALL}
    loss_cols, dx, gsm = _local_step(x[0], loss_target[0], sm, depth, fetch, emit)
    loss = lax.psum(jnp.sum(loss_cols), ("x", "y", "c"))

    order = sorted(scatter, reverse=True)
    pieces = [(scatter[s][0], likes[s]) for s in order] + [(scatter[0][1], like_dw)]
    landed = _exchange_wait("scatter_wait", pieces, dx)
    pack = _pack_small(gsm)
    ((pack_all,),) = _exchange("gather_small_grads", [([pack], pack)], scatter=False)

    grads = {}
    summed = {}
    for s, recv in zip(order, landed):
        for (key, layer, _), r in zip(by_sub[s], recv):
            summed[key, layer] = _sum8(f"sum_{key}{layer}", r)
    for name, key, cols in SHARDED:
        g = jnp.stack([summed[key, layer] for layer in range(w[name].shape[0])])
        grads[name] = jnp.swapaxes(g, 1, 2) if cols else g
    grads["conv_w_dw"] = _sum8("sum_wdw", landed[-1][0])[:n_conv * CONV_WIDTH].reshape(conv_w_dw.shape)
    pack_sum = _sum8("sum_small", pack_all)
    grads.update(_unpack_small(pack_sum, sm))

    delta, new_m, new_v = {}, {}, {}
    for name in [n for n, _, _ in SHARDED] + ["conv_w_dw"]:
        shape = w[name].shape
        res = _adamw(f"adamw_{name}", *[t.reshape(-1, shape[-1]) for t in (grads[name], w[name], m[name], v[name])])
        delta[name], new_m[name], new_v[name] = (t.reshape(shape) for t in res)
    res = _adamw("adamw_small", pack_sum, _pack_small(sm), _pack_small({n: m[n] for n in SMALL}),
                 _pack_small({n: v[n] for n in SMALL}))
    for dst, t in zip((delta, new_m, new_v), res):
        dst.update(_unpack_small(t, sm))

    outs = [loss, dx[None]]
    for d in (grads, delta, new_m, new_v):
        outs += [d[n] for n in ORDER]
    return tuple(outs)
```

```python
import functools
import math

import numpy as np
import jax
import jax.numpy as jnp
from jax import lax
from jax.experimental import pallas as pl
from jax.experimental.pallas import tpu as pltpu

F32 = jnp.float32
BF16 = jnp.bfloat16

N_DEV = 8
HEAD_DIM = 64
HEADS_PER_GROUP = 4
GROUP_COLS = HEADS_PER_GROUP * HEAD_DIM
DILATIONS = (1, 4, 16)
N_BACK = 128
N_GROUPS = 3
N_HEADS = 12
D_ATTN = 768
N_BUCKETS = 32
REL_MAX_DISTANCE = 2048
CONV_WIDTH = 31
CONV_HALO = 32
EPS = 1e-6
NEG_INF = -1e30
ADAM_LR, ADAM_B1, ADAM_B2, ADAM_EPS, ADAM_WD, ADAM_STEP = 0.001, 0.9, 0.999, 1e-08, 0.01, 10
V7X_VMEM_LIMIT_BYTES = 56 * 1024 * 1024
MESH = pl.DeviceIdType.MESH
ANY = pl.BlockSpec(memory_space=pl.ANY)


def _pick(n, prefs):
    for p in prefs:
        if n % p == 0:
            return p
    return n


def _params(sem, vmem=None):
    return pltpu.CompilerParams(dimension_semantics=sem, vmem_limit_bytes=vmem)


def _sigmoid(x):
    return 1.0 / (1.0 + jnp.exp(-x))


def _exchange(name, groups, scatter):
    n_arr = [len(arrs) for arrs, _ in groups]
    n_in = sum(n_arr) + len(groups)
    ng = len(groups)

    def body(*refs):
        ins, outs, (send_sems, recv_sems, local_sems) = refs[:n_in], refs[n_in:-3], refs[-3:]
        x, y, c = lax.axis_index("x"), lax.axis_index("y"), lax.axis_index("c")
        me = 4 * x + 2 * y + c
        pos_in = pos_out = 0
        plans = []
        for gi in range(ng):
            srcs = ins[pos_in:pos_in + n_arr[gi]]
            like = ins[pos_in + n_arr[gi]]
            dsts = outs[pos_out:pos_out + n_arr[gi]]
            pos_in += n_arr[gi] + 1
            pos_out += n_arr[gi]
            plans.append((gi, srcs, like, dsts))
        local = []
        for gi, srcs, like, dsts in plans:
            for s, d in zip(srcs, dsts):
                cp = pltpu.make_async_copy(s.at[me] if scatter else s, d.at[me], local_sems.at[gi])
                cp.start()
                local.append(cp)
        for delta in range(1, N_DEV):
            dx, dy, dc = (delta >> 2) & 1, (delta >> 1) & 1, delta & 1
            px, py, pc = (1 - x if dx else x), (1 - y if dy else y), (1 - c if dc else c)
            peer = 4 * px + 2 * py + pc
            for gi, srcs, like, dsts in plans:
                for s, d in zip(srcs, dsts):
                    pltpu.make_async_remote_copy(
                        src_ref=s.at[peer] if scatter else s, dst_ref=d.at[me],
                        send_sem=send_sems.at[gi, delta - 1], recv_sem=recv_sems.at[gi, delta - 1],
                        device_id=(px, py, pc), device_id_type=MESH).start()
        for delta in range(1, N_DEV):
            for gi, srcs, like, dsts in plans:
                pltpu.make_async_remote_copy(
                    src_ref=like, dst_ref=like, send_sem=send_sems.at[gi, delta - 1],
                    recv_sem=recv_sems.at[gi, delta - 1], device_id=(x, y, c), device_id_type=MESH).wait()
        for cp in local:
            cp.wait()

    operands, out_shape = [], []
    for arrs, like in groups:
        operands += list(arrs) + [like]
        for a in arrs:
            blk = a.shape[1:] if scatter else a.shape
            out_shape.append(jax.ShapeDtypeStruct((N_DEV,) + tuple(blk), a.dtype))
    outs = pl.pallas_call(
        body, name=name, out_shape=out_shape, in_specs=[ANY] * len(operands), out_specs=[ANY] * len(out_shape),
        scratch_shapes=[pltpu.SemaphoreType.DMA((ng, N_DEV - 1)), pltpu.SemaphoreType.DMA((ng, N_DEV - 1)),
                        pltpu.SemaphoreType.DMA((ng,))],
        compiler_params=pltpu.CompilerParams(has_side_effects=True),
    )(*operands)
    res, pos = [], 0
    for n in n_arr:
        res.append(list(outs[pos:pos + n]))
        pos += n
    return res


HBM = pl.BlockSpec(memory_space=pltpu.HBM)
SEM = pl.BlockSpec(memory_space=pltpu.SEMAPHORE)
EFFECT = pltpu.SideEffectType.DATAFLOW_SIDE_EFFECTING


def _in_hbm(a):
    return pltpu.with_memory_space_constraint(a, pltpu.HBM)


def _exchange_start(name, groups, scatter, carry):
    ns = [len(s) for s, _ in groups]
    n_in = 2 * sum(ns)

    def body(*refs):
        ins, outs = refs[:n_in], refs[n_in + 1:]
        x, y, c = lax.axis_index("x"), lax.axis_index("y"), lax.axis_index("c")
        me = 4 * x + 2 * y + c
        pi = po = 0
        for n in ns:
            srcs, lands = ins[pi:pi + n], ins[pi + n:pi + 2 * n]
            send_sems, recv_sems = outs[po], outs[po + 1]
            pi += 2 * n
            po += 2 + 2 * n
            for delta in range(1, N_DEV):
                dx, dy, dc = (delta >> 2) & 1, (delta >> 1) & 1, delta & 1
                px, py, pc = (1 - x if dx else x), (1 - y if dy else y), (1 - c if dc else c)
                peer = 4 * px + 2 * py + pc
                for s, d in zip(srcs, lands):
                    pltpu.make_async_remote_copy(
                        src_ref=s.at[peer] if scatter else s, dst_ref=d.at[me], send_sem=send_sems.at[delta - 1],
                        recv_sem=recv_sems.at[delta - 1], device_id=(px, py, pc), device_id_type=MESH).start()

    operands, out_shape, out_specs, aliases = [], [], [], {}
    for srcs, lands in groups:
        out_shape += [pltpu.SemaphoreType.DMA((N_DEV - 1,))] * 2
        out_specs += [SEM, SEM]
        for a in list(srcs) + list(lands):
            aliases[len(operands)] = len(out_shape)
            operands.append(_in_hbm(a))
            out_shape.append(pltpu.HBM(a.shape, a.dtype))
            out_specs.append(HBM)
    aliases[len(operands)] = len(out_shape)
    operands.append(_in_hbm(carry))
    out_shape.append(pltpu.HBM(carry.shape, carry.dtype))
    out_specs.append(HBM)
    outs = pl.pallas_call(
        body, name=name, out_shape=out_shape, in_specs=[HBM] * len(operands), out_specs=out_specs,
        input_output_aliases=aliases, compiler_params=pltpu.CompilerParams(has_side_effects=EFFECT),
    )(*operands)
    handles, po = [], 0
    for n in ns:
        handles.append((outs[po], outs[po + 1], list(outs[po + 2:po + 2 + n]), list(outs[po + 2 + n:po + 2 + 2 * n])))
        po += 2 + 2 * n
    return handles, outs[-1]


def _exchange_wait(name, pieces, after):
    ns = [len(h[2]) for h, _ in pieces]

    def body(*refs):
        x, y, c = lax.axis_index("x"), lax.axis_index("y"), lax.axis_index("c")
        pi = 0
        for n in ns:
            send_sems, recv_sems, like = refs[pi + 2 * n], refs[pi + 2 * n + 1], refs[pi + 2 * n + 2]
            pi += 2 * n + 3
            for delta in range(1, N_DEV):
                cp = pltpu.make_async_remote_copy(
                    src_ref=like, dst_ref=like, send_sem=send_sems.at[delta - 1], recv_sem=recv_sems.at[delta - 1],
                    device_id=(x, y, c), device_id_type=MESH)
                cp.wait_send()
                cp.wait_recv()

    operands, in_specs, out_shape, aliases = [], [], [], {}
    for (send_sems, recv_sems, srcs, lands), like in pieces:
        for a in srcs + lands:
            aliases[len(operands)] = len(out_shape)
            operands.append(a)
            in_specs.append(HBM)
            out_shape.append(pltpu.HBM(a.shape, a.dtype))
        operands += [send_sems, recv_sems, like]
        in_specs += [SEM, SEM, ANY]
    operands.append(after)
    in_specs.append(ANY)
    outs = pl.pallas_call(
        body, name=name, out_shape=out_shape, in_specs=in_specs, out_specs=[HBM] * len(out_shape),
        input_output_aliases=aliases, compiler_params=pltpu.CompilerParams(has_side_effects=EFFECT),
    )(*operands)
    res, po = [], 0
    for n in ns:
        res.append(list(outs[po + n:po + 2 * n]))
        po += 2 * n
    return res


def _sum8(name, r):
    _, rows, cols = r.shape
    tr = _pick(rows, (256, 128, 64, 32, 16, 8))

    def body(r_ref, o_ref):
        acc = r_ref[0].astype(F32)
        for p in range(1, N_DEV):
            acc = acc + r_ref[p].astype(F32)
        o_ref[...] = acc

    return pl.pallas_call(
        body, name=name, out_shape=jax.ShapeDtypeStruct((rows, cols), F32), grid=(rows // tr,),
        in_specs=[pl.BlockSpec((N_DEV, tr, cols), lambda i: (0, i, 0))],
        out_specs=pl.BlockSpec((tr, cols), lambda i: (i, 0)), compiler_params=_params(("parallel",)),
    )(r)


def _adamw(name, g, w, m, v):
    rows, cols = g.shape
    tr = _pick(rows, (512, 256, 128, 64, 32, 16, 8))
    c1 = 1.0 - ADAM_B1 ** ADAM_STEP
    c2 = 1.0 - ADAM_B2 ** ADAM_STEP

    def body(g_ref, w_ref, m_ref, v_ref, d_ref, nm_ref, nv_ref):
        gv = g_ref[...]
        nm = ADAM_B1 * m_ref[...] + (1.0 - ADAM_B1) * gv
        nv = ADAM_B2 * v_ref[...] + (1.0 - ADAM_B2) * (gv * gv)
        d_ref[...] = -ADAM_LR * ((nm / c1) / (jnp.sqrt(nv / c2) + ADAM_EPS) + ADAM_WD * w_ref[...])
        nm_ref[...] = nm
        nv_ref[...] = nv

    spec = pl.BlockSpec((tr, cols), lambda i: (i, 0))
    return pl.pallas_call(
        body, name=name, out_shape=[jax.ShapeDtypeStruct((rows, cols), F32)] * 3, grid=(rows // tr,),
        in_specs=[spec] * 4, out_specs=[spec] * 3, compiler_params=_params(("parallel",)),
    )(g, w, m, v)


def _mm_nt(name, a, ws, w_offs, n, epi, out_dtypes, extras=(), rows=(), tm=None, tn=None):
    m, k = a.shape
    tm = tm or _pick(m, (512, 256, 128))
    tn = tn or _pick(n, (1408, 1152, 1024, 512, 256, 128))
    nw, ne, nr = len(ws), len(extras), len(rows)

    def body(*refs):
        a_ref, w_refs = refs[0], refs[1:1 + nw]
        e_refs, r_refs = refs[1 + nw:1 + nw + ne], refs[1 + nw + ne:1 + nw + ne + nr]
        o_refs = refs[1 + nw + ne + nr:]
        av = a_ref[...].astype(BF16)
        accs = [lax.dot_general(av, w[...], (((1,), (1,)), ((), ())), preferred_element_type=F32) for w in w_refs]
        outs = epi(accs, [e[...] for e in e_refs], [r[...] for r in r_refs])
        for o_ref, o in zip(o_refs, outs):
            o_ref[...] = o.astype(o_ref.dtype)

    in_specs = [pl.BlockSpec((tm, k), lambda j, i: (i, 0))]
    in_specs += [pl.BlockSpec((tn, k), functools.partial(lambda j, i, off: (j + off, 0), off=off)) for off in w_offs]
    in_specs += [pl.BlockSpec((tm, tn), lambda j, i: (i, j))] * ne
    in_specs += [pl.BlockSpec((1, tn), lambda j, i: (0, j))] * nr
    return pl.pallas_call(
        body, name=name, out_shape=[jax.ShapeDtypeStruct((m, n), dt) for dt in out_dtypes],
        grid=(n // tn, m // tm), in_specs=in_specs,
        out_specs=[pl.BlockSpec((tm, tn), lambda j, i: (i, j))] * len(out_dtypes),
        compiler_params=_params(("parallel", "parallel"), V7X_VMEM_LIMIT_BYTES),
    )(a, *ws, *extras, *rows)


def _mm_nn(name, as_, bs, epi, out_dtype, extras=(), rows=(), tm=None):
    m, k = as_[0].shape
    n = bs[0].shape[1]
    tm = tm or _pick(m, (512, 256, 128))
    npair, ne, nr = len(as_), len(extras), len(rows)

    def body(*refs):
        a_refs, b_refs = refs[:npair], refs[npair:2 * npair]
        e_refs, r_refs = refs[2 * npair:2 * npair + ne], refs[2 * npair + ne:2 * npair + ne + nr]
        o_ref = refs[-1]
        acc = None
        for a_ref, b_ref in zip(a_refs, b_refs):
            p = jnp.dot(a_ref[...].astype(BF16), b_ref[...], preferred_element_type=F32)
            acc = p if acc is None else acc + p
        o_ref[...] = epi(acc, [e[...] for e in e_refs], [r[...] for r in r_refs]).astype(o_ref.dtype)

    in_specs = [pl.BlockSpec((tm, k), lambda i: (i, 0))] * npair
    in_specs += [pl.BlockSpec((k, n), lambda i: (0, 0))] * npair
    in_specs += [pl.BlockSpec((tm, n), lambda i: (i, 0))] * ne
    in_specs += [pl.BlockSpec((1, n), lambda i: (0, 0))] * nr
    return pl.pallas_call(
        body, name=name, out_shape=jax.ShapeDtypeStruct((m, n), out_dtype), grid=(m // tm,), in_specs=in_specs,
        out_specs=pl.BlockSpec((tm, n), lambda i: (i, 0)),
        compiler_params=_params(("parallel",), V7X_VMEM_LIMIT_BYTES),
    )(*as_, *bs, *extras, *rows)


def _mm_tn(name, a, b, colsum_b=False, tm=None, tk=1024):
    t, ma = a.shape
    nb = b.shape[1]
    tm = tm or _pick(ma, (1408, 1152, 1024, 768, 512, 256, 128))
    tk = _pick(t, (tk, 256, 128))
    nk = t // tk

    def body(*refs):
        a_ref, b_ref, o_ref = refs[0], refs[1], refs[2]
        acc_ref = refs[-1]
        kk = pl.program_id(1)
        bv = b_ref[...]

        @pl.when(kk == 0)
        def _():
            acc_ref[...] = jnp.zeros_like(acc_ref)

        acc_ref[...] += lax.dot_general(a_ref[...].astype(BF16), bv.astype(BF16), (((0,), (0,)), ((), ())),
                                        preferred_element_type=F32)
        if colsum_b:
            s_ref = refs[3]

            @pl.when((kk == 0) & (pl.program_id(0) == 0))
            def _():
                s_ref[...] = jnp.zeros_like(s_ref)

            @pl.when(pl.program_id(0) == 0)
            def _():
                s_ref[...] += jnp.sum(bv.astype(F32), axis=0, keepdims=True)

        @pl.when(kk == nk - 1)
        def _():
            o_ref[...] = acc_ref[...].astype(o_ref.dtype)

    out_shape = [jax.ShapeDtypeStruct((ma, nb), BF16)]
    out_specs = [pl.BlockSpec((tm, nb), lambda i, kk: (i, 0))]
    if colsum_b:
        out_shape.append(jax.ShapeDtypeStruct((1, nb), F32))
        out_specs.append(pl.BlockSpec((1, nb), lambda i, kk: (0, 0)))
    res = pl.pallas_call(
        body, name=name, out_shape=out_shape, grid=(ma // tm, nk),
        in_specs=[pl.BlockSpec((tk, tm), lambda i, kk: (kk, i)), pl.BlockSpec((tk, nb), lambda i, kk: (kk, 0))],
        out_specs=out_specs, scratch_shapes=[pltpu.VMEM((tm, nb), F32)],
        compiler_params=_params(("arbitrary", "arbitrary"), V7X_VMEM_LIMIT_BYTES),
    )(a, b)
    return res if colsum_b else res[0]


def _rmsnorm_fwd(name, x, g):
    t, d = x.shape
    tr = _pick(t, (512, 256, 128))

    def body(x_ref, g_ref, h_ref):
        xv = x_ref[...]
        r = lax.rsqrt(jnp.mean(xv * xv, axis=-1, keepdims=True) + EPS)
        h_ref[...] = (xv * r * g_ref[...]).astype(BF16)

    return pl.pallas_call(
        body, name=name, out_shape=jax.ShapeDtypeStruct((t, d), BF16), grid=(t // tr,),
        in_specs=[pl.BlockSpec((tr, d), lambda i: (i, 0)), pl.BlockSpec((1, d), lambda i: (0, 0))],
        out_specs=pl.BlockSpec((tr, d), lambda i: (i, 0)), compiler_params=_params(("parallel",)),
    )(x, g)


def _rmsnorm_bwd(name, dh, x, g, dx_out):
    t, d = x.shape
    tr = _pick(t, (512, 256, 128))

    def body(dh_ref, x_ref, g_ref, dxo_ref, dx_ref, dg_ref):
        xv = x_ref[...]
        r = lax.rsqrt(jnp.mean(xv * xv, axis=-1, keepdims=True) + EPS)
        yv = xv * r
        dhv = dh_ref[...].astype(F32)
        dy = dhv * g_ref[...]
        dx_ref[...] = dxo_ref[...] + r * (dy - yv * jnp.mean(dy * yv, axis=-1, keepdims=True))

        @pl.when(pl.program_id(0) == 0)
        def _():
            dg_ref[...] = jnp.zeros_like(dg_ref)

        dg_ref[...] += jnp.sum(dhv * yv, axis=0, keepdims=True)

    big = pl.BlockSpec((tr, d), lambda i: (i, 0))
    row = pl.BlockSpec((1, d), lambda i: (0, 0))
    return pl.pallas_call(
        body, name=name, out_shape=[jax.ShapeDtypeStruct((t, d), F32), jax.ShapeDtypeStruct((1, d), F32)],
        grid=(t // tr,), in_specs=[big, big, row, big], out_specs=[big, row],
        compiler_params=_params(("arbitrary",)),
    )(dh, x, g, dx_out)


def _loss_head(name, x, g, target):
    t, d = x.shape
    tr = _pick(t, (512, 256, 128))

    def body(x_ref, g_ref, t_ref, dx_ref, dg_ref, l_ref):
        xv = x_ref[...]
        r = lax.rsqrt(jnp.mean(xv * xv, axis=-1, keepdims=True) + EPS)
        yv = xv * r
        diff = yv * g_ref[...] - t_ref[...]
        dout = diff * (1.0 / d)
        dy = dout * g_ref[...]
        dx_ref[...] = r * (dy - yv * jnp.mean(dy * yv, axis=-1, keepdims=True))

        @pl.when(pl.program_id(0) == 0)
        def _():
            dg_ref[...] = jnp.zeros_like(dg_ref)
            l_ref[...] = jnp.zeros_like(l_ref)

        dg_ref[...] += jnp.sum(dout * yv, axis=0, keepdims=True)
        l_ref[...] += (0.5 / d) * jnp.sum(diff * diff, axis=0, keepdims=True)

    big = pl.BlockSpec((tr, d), lambda i: (i, 0))
    row = pl.BlockSpec((1, d), lambda i: (0, 0))
    return pl.pallas_call(
        body, name=name,
        out_shape=[jax.ShapeDtypeStruct((t, d), F32), jax.ShapeDtypeStruct((1, d), F32),
                   jax.ShapeDtypeStruct((1, d), F32)],
        grid=(t // tr,), in_specs=[big, row, big], out_specs=[big, row, row],
        compiler_params=_params(("arbitrary",)),
    )(x, g, target)


CONV_ROWS = 64
SUBLANES = 8


def _shift_window(win_ref, sh_ref, rows):
    win_ref[rows:rows + SUBLANES, :] = jnp.zeros((SUBLANES, win_ref.shape[1]), F32)
    for b in range(1, SUBLANES):
        sh_ref[b - 1] = win_ref[b:b + rows, :]


def _window_rows(win_ref, sh_ref, o, c0):
    b = o % SUBLANES
    if b == 0:
        return win_ref[o:o + CONV_ROWS, c0:c0 + 128]
    return sh_ref[b - 1, o - b:o - b + CONV_ROWS, c0:c0 + 128]


def _dwconv_fwd(name, glu, w_dw, b_dw, ln_g, ln_b):
    t, c = glu.shape
    tt = _pick(t, (256, 128))
    hb = tt // CONV_HALO

    def body(cur_ref, halo_ref, w_ref, b_ref, g_ref, be_ref, dw_ref, s_ref, win_ref, sh_ref):
        i = pl.program_id(0)
        halo = halo_ref[...].astype(F32)
        win_ref[0:CONV_HALO, :] = jnp.where(i > 0, halo, 0.0)
        win_ref[CONV_HALO:tt + CONV_HALO, :] = cur_ref[...].astype(F32)
        _shift_window(win_ref, sh_ref, tt + CONV_HALO)
        for r0 in range(0, tt, CONV_ROWS):
            for c0 in range(0, c, 128):
                acc = jnp.zeros((CONV_ROWS, 128), F32) + b_ref[:, c0:c0 + 128]
                for k in range(CONV_WIDTH):
                    o = r0 + k + CONV_HALO - (CONV_WIDTH - 1)
                    acc = acc + w_ref[k:k + 1, c0:c0 + 128] * _window_rows(win_ref, sh_ref, o, c0)
                dw_ref[r0:r0 + CONV_ROWS, c0:c0 + 128] = acc
        u = dw_ref[...]
        mu = jnp.mean(u, axis=-1, keepdims=True)
        uc = u - mu
        rstd = lax.rsqrt(jnp.mean(uc * uc, axis=-1, keepdims=True) + EPS)
        z = uc * rstd * g_ref[...] + be_ref[...]
        s_ref[...] = (z * _sigmoid(z)).astype(BF16)

    big = pl.BlockSpec((tt, c), lambda i: (i, 0))
    row = pl.BlockSpec((1, c), lambda i: (0, 0))
    return pl.pallas_call(
        body, name=name, out_shape=[jax.ShapeDtypeStruct((t, c), F32), jax.ShapeDtypeStruct((t, c), BF16)],
        grid=(t // tt,),
        in_specs=[big, pl.BlockSpec((CONV_HALO, c), lambda i: (jnp.maximum(i * hb - 1, 0), 0)),
                  pl.BlockSpec((CONV_HALO, c), lambda i: (0, 0)), row, row, row],
        out_specs=[big, big],
        scratch_shapes=[pltpu.VMEM((tt + CONV_HALO + SUBLANES, c), F32),
                        pltpu.VMEM((SUBLANES - 1, tt + CONV_HALO, c), F32)],
        compiler_params=_params(("parallel",), V7X_VMEM_LIMIT_BYTES),
    )(glu, glu, w_dw, b_dw, ln_g, ln_b)


def _ln_silu_bwd(name, ds, dw, ln_g, ln_b):
    t, c = dw.shape
    tr = _pick(t, (256, 128))

    def body(ds_ref, dw_ref, g_ref, be_ref, o_ref, acc_ref):
        u = dw_ref[...]
        mu = jnp.mean(u, axis=-1, keepdims=True)
        uc = u - mu
        rstd = lax.rsqrt(jnp.mean(uc * uc, axis=-1, keepdims=True) + EPS)
        xh = uc * rstd
        z = xh * g_ref[...] + be_ref[...]
        sg = _sigmoid(z)
        dz = ds_ref[...].astype(F32) * (sg * (1.0 + z * (1.0 - sg)))
        dxh = dz * g_ref[...]
        du = rstd * (dxh - jnp.mean(dxh, axis=-1, keepdims=True) - xh * jnp.mean(dxh * xh, axis=-1, keepdims=True))
        o_ref[...] = du

        @pl.when(pl.program_id(0) == 0)
        def _():
            acc_ref[...] = jnp.zeros_like(acc_ref)

        acc_ref[0:1, :] += jnp.sum(dz * xh, axis=0, keepdims=True)
        acc_ref[1:2, :] += jnp.sum(dz, axis=0, keepdims=True)
        acc_ref[2:3, :] += jnp.sum(du, axis=0, keepdims=True)

    big = pl.BlockSpec((tr, c), lambda i: (i, 0))
    row = pl.BlockSpec((1, c), lambda i: (0, 0))
    return pl.pallas_call(
        body, name=name, out_shape=[jax.ShapeDtypeStruct((t, c), F32), jax.ShapeDtypeStruct((8, c), F32)],
        grid=(t // tr,), in_specs=[big, big, row, row],
        out_specs=[big, pl.BlockSpec((8, c), lambda i: (0, 0))], compiler_params=_params(("arbitrary",)),
    )(ds, dw, ln_g, ln_b)


def _dwconv_bwd(name, ddw, a, gt, w_dw):
    t, c = ddw.shape
    tt = _pick(t, (256, 128))
    hb = tt // CONV_HALO
    last = t // tt - 1
    back = CONV_WIDTH - 1

    def body(d_ref, dn_ref, a_ref, ap_ref, g_ref, gp_ref, w_ref, du_ref, dwk_ref, db_ref,
             wd_ref, wg_ref, dg_ref, shd_ref, shg_ref):
        i = pl.program_id(0)
        wd_ref[0:tt, :] = d_ref[...]
        wd_ref[tt:tt + CONV_HALO, :] = jnp.where(i < last, dn_ref[...], 0.0)
        glu_prev = ap_ref[...].astype(F32) * _sigmoid(gp_ref[...].astype(F32))
        wg_ref[0:CONV_HALO, :] = jnp.where(i > 0, glu_prev, 0.0)
        av = a_ref[...].astype(F32)
        sg = _sigmoid(g_ref[...].astype(F32))
        wg_ref[CONV_HALO:tt + CONV_HALO, :] = av * sg
        _shift_window(wd_ref, shd_ref, tt + CONV_HALO)
        _shift_window(wg_ref, shg_ref, tt + CONV_HALO)

        @pl.when(i == 0)
        def _():
            dwk_ref[...] = jnp.zeros_like(dwk_ref)
            db_ref[...] = jnp.zeros_like(db_ref)

        for r0 in range(0, tt, CONV_ROWS):
            for c0 in range(0, c, 128):
                dcur = wd_ref[r0:r0 + CONV_ROWS, c0:c0 + 128]
                acc = jnp.zeros((CONV_ROWS, 128), F32)
                for k in range(CONV_WIDTH):
                    acc = acc + w_ref[k:k + 1, c0:c0 + 128] * _window_rows(wd_ref, shd_ref, r0 + back - k, c0)
                    glu_rows = _window_rows(wg_ref, shg_ref, r0 + k + CONV_HALO - back, c0)
                    dwk_ref[k:k + 1, c0:c0 + 128] += jnp.sum(dcur * glu_rows, axis=0, keepdims=True)
                dg_ref[r0:r0 + CONV_ROWS, c0:c0 + 128] = acc
        dglu = dg_ref[...]
        da = dglu * sg
        dgate = dglu * av * sg * (1.0 - sg)
        du_ref[:, 0:c] = da.astype(BF16)
        du_ref[:, c:] = dgate.astype(BF16)
        db_ref[:, 0:c] += jnp.sum(da, axis=0, keepdims=True)
        db_ref[:, c:] += jnp.sum(dgate, axis=0, keepdims=True)

    big = pl.BlockSpec((tt, c), lambda i: (i, 0))
    prev = pl.BlockSpec((CONV_HALO, c), lambda i: (jnp.maximum(i * hb - 1, 0), 0))
    nxt = pl.BlockSpec((CONV_HALO, c), lambda i: (jnp.minimum((i + 1) * hb, t // CONV_HALO - 1), 0))
    return pl.pallas_call(
        body, name=name,
        out_shape=[jax.ShapeDtypeStruct((t, 2 * c), BF16), jax.ShapeDtypeStruct((CONV_HALO, c), F32),
                   jax.ShapeDtypeStruct((1, 2 * c), F32)],
        grid=(t // tt,),
        in_specs=[big, nxt, big, prev, big, prev, pl.BlockSpec((CONV_HALO, c), lambda i: (0, 0))],
        out_specs=[pl.BlockSpec((tt, 2 * c), lambda i: (i, 0)), pl.BlockSpec((CONV_HALO, c), lambda i: (0, 0)),
                   pl.BlockSpec((1, 2 * c), lambda i: (0, 0))],
        scratch_shapes=[pltpu.VMEM((tt + CONV_HALO + SUBLANES, c), F32),
                        pltpu.VMEM((tt + CONV_HALO + SUBLANES, c), F32), pltpu.VMEM((tt, c), F32),
                        pltpu.VMEM((SUBLANES - 1, tt + CONV_HALO, c), F32),
                        pltpu.VMEM((SUBLANES - 1, tt + CONV_HALO, c), F32)],
        compiler_params=_params(("arbitrary",), V7X_VMEM_LIMIT_BYTES),
    )(ddw, ddw, a, a, gt, gt, w_dw)


def _bucket_tables():
    i = np.arange(N_BACK)[:, None]
    j = np.arange(2 * N_BACK)[None, :]
    dist = i + N_BACK - j
    valid = (dist >= 0) & (dist <= N_BACK)
    max_exact = N_BUCKETS // 2
    out = []
    for d in DILATIONS:
        n = np.maximum(dist * d, 0)
        nf = np.maximum(n, 1).astype(np.float32)
        large = max_exact + (np.log(nf / np.float32(max_exact)) / np.float32(math.log(REL_MAX_DISTANCE / max_exact))
                             * np.float32(N_BUCKETS - max_exact)).astype(np.int32)
        large = np.minimum(large, N_BUCKETS - 1)
        out.append(np.where(valid, np.where(n < max_exact, n, large), -1))
    return np.stack(out).astype(np.int32)


def _bias_build(name, rel_bias, buckets):
    def body(tbl_ref, bk_ref, o_ref):
        g = pl.program_id(0)
        bk = bk_ref[0]
        for h in range(HEADS_PER_GROUP):
            acc = jnp.zeros(bk.shape, F32)
            for b in range(N_BUCKETS):
                acc = jnp.where(bk == b, tbl_ref[b, g * HEADS_PER_GROUP + h], acc)
            o_ref[h] = jnp.where(bk < 0, NEG_INF, acc)

    return pl.pallas_call(
        body, name=name, out_shape=jax.ShapeDtypeStruct((N_HEADS, N_BACK, 2 * N_BACK), F32), grid=(N_GROUPS,),
        in_specs=[pl.BlockSpec(memory_space=pltpu.SMEM), pl.BlockSpec((1, N_BACK, 2 * N_BACK), lambda g: (g, 0, 0))],
        out_specs=pl.BlockSpec((HEADS_PER_GROUP, N_BACK, 2 * N_BACK), lambda g: (g, 0, 0)),
        compiler_params=_params(("arbitrary",)),
    )(rel_bias, buckets)


def _bias_grad(name, dbs, buckets):
    nd = len(dbs)

    def body(*refs):
        bk = refs[nd][0]
        o_ref = refs[nd + 1]
        lane = lax.broadcasted_iota(jnp.int32, (1, 128), 1)
        db = [sum(r[h] for r in refs[:nd]) for h in range(HEADS_PER_GROUP)]
        for b in range(N_BUCKETS):
            row = jnp.zeros((1, 128), F32)
            for h in range(HEADS_PER_GROUP):
                s = jnp.sum(jnp.where(bk == b, db[h], 0.0), axis=0, keepdims=True)
                s = jnp.sum(s, axis=1, keepdims=True)
                row = jnp.where(lane // 32 == h, s, row)
            o_ref[0, b:b + 1, :] = row

    spec = pl.BlockSpec((HEADS_PER_GROUP, N_BACK, 2 * N_BACK), lambda g: (g, 0, 0))
    return pl.pallas_call(
        body, name=name, out_shape=jax.ShapeDtypeStruct((N_GROUPS, N_BUCKETS, 128), F32), grid=(N_GROUPS,),
        in_specs=[spec] * nd + [pl.BlockSpec((1, N_BACK, 2 * N_BACK), lambda g: (g, 0, 0))],
        out_specs=pl.BlockSpec((1, N_BUCKETS, 128), lambda g: (g, 0, 0)), compiler_params=_params(("arbitrary",)),
    )(*dbs, buckets)


def _head_cols(h):
    return slice(h * HEAD_DIM, (h + 1) * HEAD_DIM)


def _attn_fwd(name, qkv, bias, g):
    t = qkv.shape[0]
    d = DILATIONS[g]
    tq = t // d
    nblk = qkv.shape[1] // GROUP_COLS
    scale = HEAD_DIM ** -0.5

    def body(q_ref, kp_ref, kc_ref, vp_ref, vc_ref, b_ref, o_ref, l_ref):
        n = pl.program_id(1)
        col = lax.broadcasted_iota(jnp.int32, (N_BACK, 2 * N_BACK), 1)
        keep = (col >= N_BACK) | (n > 0)
        lane = lax.broadcasted_iota(jnp.int32, (N_BACK, 128), 1)
        lse_tile = jnp.zeros((N_BACK, 128), F32)
        outs = []
        for h in range(HEADS_PER_GROUP):
            hc = _head_cols(h)
            kk = jnp.concatenate([kp_ref[:, hc], kc_ref[:, hc]], axis=0)
            vv = jnp.concatenate([vp_ref[:, hc], vc_ref[:, hc]], axis=0)
            s = lax.dot_general(q_ref[:, hc], kk, (((1,), (1,)), ((), ())), preferred_element_type=F32)
            s = jnp.where(keep, s * scale + b_ref[h], NEG_INF)
            m = jnp.max(s, axis=-1, keepdims=True)
            p = jnp.exp(s - m)
            den = jnp.sum(p, axis=-1, keepdims=True)
            outs.append(jnp.dot(p.astype(BF16), vv, preferred_element_type=F32) / den)
            lse_tile = jnp.where(lane // 32 == h, m + jnp.log(den), lse_tile)
        o_ref[...] = jnp.concatenate(outs, axis=1)
        l_ref[...] = lse_tile

    def blk(part, prev):
        if prev:
            return pl.BlockSpec((N_BACK, GROUP_COLS), lambda r, n: (jnp.maximum(n - 1, 0), r * nblk + 3 * part + g))
        return pl.BlockSpec((N_BACK, GROUP_COLS), lambda r, n: (n, r * nblk + 3 * part + g))

    qv = qkv.reshape(tq, d * qkv.shape[1])
    o, l = pl.pallas_call(
        body, name=name,
        out_shape=[jax.ShapeDtypeStruct((tq, d * GROUP_COLS), F32), jax.ShapeDtypeStruct((tq, d * 128), F32)],
        grid=(d, tq // N_BACK),
        in_specs=[blk(0, False), blk(1, True), blk(1, False), blk(2, True), blk(2, False),
                  pl.BlockSpec((HEADS_PER_GROUP, N_BACK, 2 * N_BACK), lambda r, n: (g, 0, 0))],
        out_specs=[pl.BlockSpec((N_BACK, GROUP_COLS), lambda r, n: (n, r)),
                   pl.BlockSpec((N_BACK, 128), lambda r, n: (n, r))],
        compiler_params=_params(("parallel", "parallel")),
    )(qv, qv, qv, qv, qv, bias)
    return o.reshape(t, GROUP_COLS), l.reshape(t, 128)


def _group_weights(l_refs, h):
    ls = [l_ref[:, 32 * h:32 * h + 1] for l_ref in l_refs]
    m = jnp.maximum(jnp.maximum(ls[0], ls[1]), ls[2])
    es = [jnp.exp(l - m) for l in ls]
    tot = es[0] + es[1] + es[2]
    return [e / tot for e in es]


def _attn_merge(name, os_, ls):
    t = os_[0].shape[0]
    tr = _pick(t, (512, 256, 128))

    def body(o0, o1, o2, l0, l1, l2, out_ref):
        o_refs = (o0, o1, o2)
        pieces = [[None] * HEADS_PER_GROUP for _ in range(N_GROUPS)]
        for h in range(HEADS_PER_GROUP):
            al = _group_weights((l0, l1, l2), h)
            for g in range(N_GROUPS):
                pieces[g][h] = o_refs[g][:, _head_cols(h)] * al[g]
        out_ref[...] = jnp.concatenate([p for row in pieces for p in row], axis=1).astype(BF16)

    so = pl.BlockSpec((tr, GROUP_COLS), lambda i: (i, 0))
    sl = pl.BlockSpec((tr, 128), lambda i: (i, 0))
    return pl.pallas_call(
        body, name=name, out_shape=jax.ShapeDtypeStruct((t, D_ATTN), BF16), grid=(t // tr,),
        in_specs=[so] * 3 + [sl] * 3, out_specs=pl.BlockSpec((tr, D_ATTN), lambda i: (i, 0)),
        compiler_params=_params(("parallel",)),
    )(*os_, *ls)


def _attn_bwd_prep(name, d_out, os_, ls):
    t = d_out.shape[0]
    tr = _pick(t, (512, 256, 128))

    def body(do_ref, o0, o1, o2, l0, l1, l2, d0, d1, d2, c0, c1, c2):
        o_refs, d_refs, c_refs = (o0, o1, o2), (d0, d1, d2), (c0, c1, c2)
        lane = lax.broadcasted_iota(jnp.int32, (tr, 128), 1)
        dos = [[None] * HEADS_PER_GROUP for _ in range(N_GROUPS)]
        cs = [jnp.zeros((tr, 128), F32) for _ in range(N_GROUPS)]
        for h in range(HEADS_PER_GROUP):
            al = _group_weights((l0, l1, l2), h)
            tot = jnp.zeros((tr, 1), F32)
            for g in range(N_GROUPS):
                dv = do_ref[:, g * GROUP_COLS + h * HEAD_DIM:g * GROUP_COLS + (h + 1) * HEAD_DIM].astype(F32)
                tot = tot + al[g] * jnp.sum(dv * o_refs[g][:, _head_cols(h)], axis=-1, keepdims=True)
                dos[g][h] = dv * al[g]
            for g in range(N_GROUPS):
                cs[g] = jnp.where(lane // 32 == h, -al[g] * tot, cs[g])
        for g in range(N_GROUPS):
            d_refs[g][...] = jnp.concatenate(dos[g], axis=1).astype(BF16)
            c_refs[g][...] = cs[g]

    so = pl.BlockSpec((tr, GROUP_COLS), lambda i: (i, 0))
    sl = pl.BlockSpec((tr, 128), lambda i: (i, 0))
    res = pl.pallas_call(
        body, name=name,
        out_shape=[jax.ShapeDtypeStruct((t, GROUP_COLS), BF16)] * 3 + [jax.ShapeDtypeStruct((t, 128), F32)] * 3,
        grid=(t // tr,), in_specs=[pl.BlockSpec((tr, D_ATTN), lambda i: (i, 0))] + [so] * 3 + [sl] * 3,
        out_specs=[so] * 3 + [sl] * 3, compiler_params=_params(("parallel",)),
    )(d_out, *os_, *ls)
    return res[:3], res[3:]


def _attn_bwd(name, qkv, do, lse, cterm, bias, g):
    t = qkv.shape[0]
    d = DILATIONS[g]
    tq = t // d
    nb = tq // N_BACK
    nblk = qkv.shape[1] // GROUP_COLS
    scale = HEAD_DIM ** -0.5
    nt = (((1,), (1,)), ((), ()))
    tn = (((0,), (0,)), ((), ()))

    def body(qn, qx, kp, kn, vp, vn, don, dox, ln, lx, cn, cx, b_ref, dqkv_ref, db_ref):
        n = pl.program_id(1)
        has_prev = n > 0
        has_next = n < nb - 1

        @pl.when((n == 0) & (pl.program_id(0) == 0))
        def _():
            db_ref[...] = jnp.zeros_like(db_ref)

        dqs, dks, dvs = [], [], []
        for h in range(HEADS_PER_GROUP):
            hc = _head_cols(h)
            st = slice(32 * h, 32 * h + 1)
            b_prev, b_same = b_ref[h, :, 0:N_BACK], b_ref[h, :, N_BACK:]

            def pair(q, k, v, dout, l, cc, bias_blk, on):
                s = lax.dot_general(q, k, nt, preferred_element_type=F32) * scale + bias_blk
                p = jnp.where(on, jnp.exp(s - l), 0.0)
                dp = lax.dot_general(dout, v, nt, preferred_element_type=F32)
                return p, p * (dp + cc)

            q0, q1, k0, k1, v0, v1 = qn[:, hc], qx[:, hc], kp[:, hc], kn[:, hc], vp[:, hc], vn[:, hc]
            d0, d1 = don[:, hc], dox[:, hc]
            p_a, ds_a = pair(q0, k1, v1, d0, ln[:, st], cn[:, st], b_same, True)
            p_b, ds_b = pair(q0, k0, v0, d0, ln[:, st], cn[:, st], b_prev, has_prev)
            p_c, ds_c = pair(q1, k1, v1, d1, lx[:, st], cx[:, st], b_prev, has_next)
            ds_a16, ds_b16, ds_c16 = ds_a.astype(BF16), ds_b.astype(BF16), ds_c.astype(BF16)
            dqs.append(scale * (jnp.dot(ds_a16, k1, preferred_element_type=F32)
                                + jnp.dot(ds_b16, k0, preferred_element_type=F32)))
            dks.append(scale * (lax.dot_general(ds_a16, q0, tn, preferred_element_type=F32)
                                + lax.dot_general(ds_c16, q1, tn, preferred_element_type=F32)))
            dvs.append(lax.dot_general(p_a.astype(BF16), d0, tn, preferred_element_type=F32)
                       + lax.dot_general(p_c.astype(BF16), d1, tn, preferred_element_type=F32))
            db_ref[h, :, 0:N_BACK] += ds_b
            db_ref[h, :, N_BACK:] += ds_a
        dqkv_ref[...] = jnp.concatenate(dqs + dks + dvs, axis=1).astype(BF16)

    def rows(which):
        if which == "prev":
            return lambda n: jnp.maximum(n - 1, 0)
        if which == "next":
            return lambda n: jnp.minimum(n + 1, nb - 1)
        return lambda n: n

    def qkv_blk(part, which):
        f = rows(which)
        return pl.BlockSpec((N_BACK, GROUP_COLS), lambda r, n: (f(n), r * nblk + 3 * part + g))

    def grp_blk(width, which):
        f = rows(which)
        return pl.BlockSpec((N_BACK, width), lambda r, n: (f(n), r))

    qv = qkv.reshape(tq, d * qkv.shape[1])
    dov = do.reshape(tq, d * GROUP_COLS)
    lv = lse.reshape(tq, d * 128)
    cv = cterm.reshape(tq, d * 128)
    dqkv_g, db = pl.pallas_call(
        body, name=name,
        out_shape=[jax.ShapeDtypeStruct((tq, d * 3 * GROUP_COLS), BF16),
                   jax.ShapeDtypeStruct((HEADS_PER_GROUP, N_BACK, 2 * N_BACK), F32)],
        grid=(d, nb),
        in_specs=[qkv_blk(0, "same"), qkv_blk(0, "next"), qkv_blk(1, "prev"), qkv_blk(1, "same"),
                  qkv_blk(2, "prev"), qkv_blk(2, "same"), grp_blk(GROUP_COLS, "same"), grp_blk(GROUP_COLS, "next"),
                  grp_blk(128, "same"), grp_blk(128, "next"), grp_blk(128, "same"), grp_blk(128, "next"),
                  pl.BlockSpec((HEADS_PER_GROUP, N_BACK, 2 * N_BACK), lambda r, n: (g, 0, 0))],
        out_specs=[pl.BlockSpec((N_BACK, 3 * GROUP_COLS), lambda r, n: (n, r)),
                   pl.BlockSpec((HEADS_PER_GROUP, N_BACK, 2 * N_BACK), lambda r, n: (0, 0, 0))],
        compiler_params=_params(("arbitrary", "arbitrary")),
    )(qv, qv, qv, qv, qv, qv, dov, dov, lv, lv, cv, cv, bias)
    return dqkv_g.reshape(t, 3 * GROUP_COLS), db


def _row(v):
    return v.reshape(1, -1)


def _plain(accs, extras, rows):
    return (accs[0],)


def _glu_epi(accs, extras, rows):
    a = (accs[0] + rows[0]).astype(BF16)
    gt = (accs[1] + rows[1]).astype(BF16)
    return a, gt, a.astype(F32) * _sigmoid(gt.astype(F32))


def _swiglu_epi(accs, extras, rows):
    gq, uq = accs[0].astype(BF16), accs[1].astype(BF16)
    gf = gq.astype(F32)
    return gq, uq, gf * _sigmoid(gf) * uq.astype(F32)


def _swiglu_bwd_epi(accs, extras, rows):
    gf, uf = extras[0].astype(F32), extras[1].astype(F32)
    sg = _sigmoid(gf)
    return accs[0] * uf * (sg * (1.0 + gf * (1.0 - sg))), accs[0] * gf * sg


def _residual(acc, extras, rows):
    out = acc + extras[0]
    return out + rows[0] if rows else out


def _identity(acc, extras, rows):
    return acc


def _group_rows(w):
    parts = [w[p * D_ATTN:(p + 1) * D_ATTN].reshape(N_GROUPS, GROUP_COLS, -1) for p in range(3)]
    return jnp.concatenate(parts, axis=1)


def _ungroup_rows(wg):
    return jnp.concatenate([wg[g][p * GROUP_COLS:(p + 1) * GROUP_COLS] for p in range(3) for g in range(N_GROUPS)],
                           axis=0)


def _local_step(x, target, sm, depth, fetch, emit):
    d_model = x.shape[1]
    buckets = jnp.asarray(_bucket_tables())
    bias = _bias_build("bias_build", sm["rel_bias"], buckets)
    saved = []
    for i in range(depth):
        j = i // 2
        rec = {"x_mix": x}
        wm = fetch(2 * i, x)
        h = _rmsnorm_fwd(f"rms_mix_fwd{i}", x, _row(sm["norm_mix"][i]))
        rec.update(h_mix=h, wm=wm)
        if i % 2 == 0:
            c = wm["w2"].shape[0]
            tn = _pick(c, (512, 256, 128))
            b1 = sm["conv_b_pw1"][j]
            a, gt, glu = _mm_nt(f"conv_pw1_fwd{j}", h, [wm["w1t"]] * 2, [0, c // tn], c, _glu_epi, (BF16,) * 3,
                                rows=[_row(b1[:c]), _row(b1[c:])], tn=tn)
            dw, s = _dwconv_fwd(f"dwconv_fwd{j}", glu, wm["wdw"], _row(sm["conv_b_dw"][j]),
                                _row(sm["conv_ln_g"][j]), _row(sm["conv_ln_b"][j]))
            x = _mm_nn(f"conv_pw2_fwd{j}", [s], [wm["w2"]], _residual, F32, extras=[x],
                       rows=[_row(sm["conv_b_pw2"][j])])
            rec.update(a=a, gt=gt, dw=dw, s=s)
        else:
            qkv = _mm_nt(f"attn_qkv_fwd{j}", h, [wm["wqkvt"]], [0], 3 * D_ATTN, _plain, (BF16,))[0]
            og = [_attn_fwd(f"attn_fwd{j}_{g}", qkv, bias, g) for g in range(N_GROUPS)]
            os_, ls = [o for o, _ in og], [l for _, l in og]
            om = _attn_merge(f"attn_merge{j}", os_, ls)
            x = _mm_nt(f"attn_out_fwd{j}", om, [wm["wot"]], [0], d_model, lambda accs, e, r: (accs[0] + e[0],),
                       (F32,), extras=[x])[0]
            rec.update(qkv=qkv, os=os_, ls=ls, om=om)
        rec["x_ffn"] = x
        wf = fetch(2 * i + 1, x)
        h2 = _rmsnorm_fwd(f"rms_ffn_fwd{i}", x, _row(sm["norm_ffn"][i]))
        f = wf["wd"].shape[0]
        gq, uq, act = _mm_nt(f"ffn_up_fwd{i}", h2, [wf["wgt"], wf["wut"]], [0, 0], f, _swiglu_epi, (BF16,) * 3)
        x = _mm_nn(f"ffn_down_fwd{i}", [act], [wf["wd"]], _residual, F32, extras=[x])
        rec.update(h_ffn=h2, gq=gq, uq=uq, act=act, wf=wf)
        saved.append(rec)

    dx, g_final, loss_cols = _loss_head("loss_head", x, _row(sm["final_norm"]), target)

    g_mix, g_ffn = [None] * depth, [None] * depth
    nconv = (depth + 1) // 2
    g_b1, g_bdw, g_lng, g_lnb, g_b2 = ([None] * nconv for _ in range(5))
    dbias = []
    for i in reversed(range(depth)):
        j = i // 2
        rec = saved[i]
        wm, wf = rec["wm"], rec["wf"]
        f = wf["wd"].shape[0]
        dgate, dup = _mm_nt(f"ffn_down_bwd{i}", dx, [wf["wd"]], [0], f, _swiglu_bwd_epi, (BF16, BF16),
                            extras=[rec["gq"], rec["uq"]])
        gf = {"wd": _mm_tn(f"ffn_down_dw{i}", rec["act"], dx),
              "wgt": _mm_tn(f"ffn_gate_dw{i}", dgate, rec["h_ffn"]),
              "wut": _mm_tn(f"ffn_up_dw{i}", dup, rec["h_ffn"])}
        dh = _mm_nn(f"ffn_up_bwd{i}", [dgate, dup], [wf["wgt"], wf["wut"]], _identity, BF16, tm=256)
        dx, g_ffn[i] = _rmsnorm_bwd(f"rms_ffn_bwd{i}", dh, rec["x_ffn"], _row(sm["norm_ffn"][i]), dx)
        dx = emit(2 * i + 1, gf, dx)
        if i % 2 == 0:
            c = wm["w2"].shape[0]
            gm = {}
            gm["w2"], g_b2[j] = _mm_tn(f"conv_pw2_dw{j}", rec["s"], dx, colsum_b=True)
            ds = _mm_nt(f"conv_pw2_bwd{j}", dx, [wm["w2"]], [0], c, _plain, (BF16,))[0]
            ddw, sums = _ln_silu_bwd(f"ln_silu_bwd{j}", ds, rec["dw"], _row(sm["conv_ln_g"][j]),
                                     _row(sm["conv_ln_b"][j]))
            g_lng[j], g_lnb[j], g_bdw[j] = sums[0], sums[1], sums[2]
            du, dwk, db1 = _dwconv_bwd(f"dwconv_bwd{j}", ddw, rec["a"], rec["gt"], wm["wdw"])
            gm["wdw"] = dwk[:CONV_WIDTH]
            g_b1[j] = db1[0]
            gm["w1t"] = _mm_tn(f"conv_pw1_dw{j}", du, rec["h_mix"])
            dh = _mm_nn(f"conv_pw1_bwd{j}", [du], [wm["w1t"]], _identity, BF16)
        else:
            gm = {"wot": _mm_tn(f"attn_out_dw{j}", dx, rec["om"])}
            d_om = _mm_nn(f"attn_out_bwd{j}", [dx], [wm["wot"]], _identity, BF16)
            dos, cs = _attn_bwd_prep(f"attn_bwd_prep{j}", d_om, rec["os"], rec["ls"])
            back = [_attn_bwd(f"attn_bwd{j}_{g}", rec["qkv"], dos[g], rec["ls"][g], cs[g], bias, g)
                    for g in range(N_GROUPS)]
            dqkv = [b[0] for b in back]
            dbias.append(jnp.concatenate([b[1] for b in back], axis=0))
            wq = _group_rows(wm["wqkvt"])
            dh = _mm_nn(f"attn_qkv_bwd{j}", dqkv, [wq[g] for g in range(N_GROUPS)], _identity, BF16)
            gm["wqkvt"] = _ungroup_rows([_mm_tn(f"attn_qkv_dw{j}_{g}", dqkv[g], rec["h_mix"])
                                         for g in range(N_GROUPS)])
        dx, g_mix[i] = _rmsnorm_bwd(f"rms_mix_bwd{i}", dh, rec["x_mix"], _row(sm["norm_mix"][i]), dx)
        dx = emit(2 * i, gm, dx)

    gb = _bias_grad("bias_grad", dbias, buckets)
    g_rel = jnp.transpose(gb[:, :, ::32], (1, 0, 2)).reshape(N_BUCKETS, N_HEADS)
    gsm = {
        "norm_mix": jnp.concatenate(g_mix, axis=0), "norm_ffn": jnp.concatenate(g_ffn, axis=0),
        "final_norm": g_final[0], "conv_b_pw1": jnp.stack(g_b1), "conv_b_dw": jnp.stack(g_bdw),
        "conv_ln_g": jnp.stack(g_lng), "conv_ln_b": jnp.stack(g_lnb),
        "conv_b_pw2": jnp.concatenate(g_b2, axis=0), "rel_bias": g_rel,
    }
    return loss_cols, dx, gsm


SMALL = ("norm_mix", "norm_ffn", "final_norm", "conv_b_pw1", "conv_b_dw", "conv_ln_g", "conv_ln_b", "conv_b_pw2",
         "rel_bias")
SHARDED = (("conv_w_pw1", "w1t", True), ("conv_w_pw2", "w2", False), ("attn_w_qkv", "wqkvt", True),
           ("attn_w_o", "wot", True), ("ffn_w_gate", "wgt", True), ("ffn_w_up", "wut", True),
           ("ffn_w_down", "wd", False))
ORDER = ("norm_mix", "norm_ffn", "final_norm", "conv_w_pw1", "conv_b_pw1", "conv_w_dw", "conv_b_dw", "conv_ln_g",
         "conv_ln_b", "conv_w_pw2", "conv_b_pw2", "attn_w_qkv", "attn_w_o", "rel_bias", "ffn_w_gate", "ffn_w_up",
         "ffn_w_down")
PACK_LANES = 128
PACK_ROW_TILE = 8


def _pack_small(vals):
    flat = jnp.concatenate([vals[n].reshape(-1) for n in SMALL])
    per_tile = PACK_LANES * PACK_ROW_TILE
    return jnp.pad(flat, (0, -flat.shape[0] % per_tile)).reshape(-1, PACK_LANES)


def _unpack_small(pack, like):
    flat, out, pos = pack.reshape(-1), {}, 0
    for n in SMALL:
        out[n] = flat[pos:pos + like[n].size].reshape(like[n].shape)
        pos += like[n].size
    return out


def _dw_blocks(w):
    l, k, c = w.shape
    blk = jnp.transpose(w.reshape(l, k, N_DEV, c // N_DEV), (2, 0, 1, 3)).reshape(N_DEV, l * k, c // N_DEV)
    return jnp.pad(blk, ((0, 0), (0, -(l * k) % 8), (0, 0)))


def kernel(x, norm_mix, norm_ffn, final_norm, conv_w_pw1, conv_b_pw1, conv_w_dw, conv_b_dw, conv_ln_g, conv_ln_b, conv_w_pw2, conv_b_pw2, attn_w_qkv, attn_w_o, rel_bias, ffn_w_gate, ffn_w_up, ffn_w_down, loss_target, m_norm_mix, m_norm_ffn, m_final_norm, m_conv_w_pw1, m_conv_b_pw1, m_conv_w_dw, m_conv_b_dw, m_conv_ln_g, m_conv_ln_b, m_conv_w_pw2, m_conv_b_pw2, m_attn_w_qkv, m_attn_w_o, m_rel_bias, m_ffn_w_gate, m_ffn_w_up, m_ffn_w_down, v_norm_mix, v_norm_ffn, v_final_norm, v_conv_w_pw1, v_conv_b_pw1, v_conv_w_dw, v_conv_b_dw, v_conv_ln_g, v_conv_ln_b, v_conv_w_pw2, v_conv_b_pw2, v_attn_w_qkv, v_attn_w_o, v_rel_bias, v_ffn_w_gate, v_ffn_w_up, v_ffn_w_down):
    w = dict(norm_mix=norm_mix, norm_ffn=norm_ffn, final_norm=final_norm, conv_w_pw1=conv_w_pw1,
             conv_b_pw1=conv_b_pw1, conv_w_dw=conv_w_dw, conv_b_dw=conv_b_dw, conv_ln_g=conv_ln_g,
             conv_ln_b=conv_ln_b, conv_w_pw2=conv_w_pw2, conv_b_pw2=conv_b_pw2, attn_w_qkv=attn_w_qkv,
             attn_w_o=attn_w_o, rel_bias=rel_bias, ffn_w_gate=ffn_w_gate, ffn_w_up=ffn_w_up, ffn_w_down=ffn_w_down)
    m = dict(norm_mix=m_norm_mix, norm_ffn=m_norm_ffn, final_norm=m_final_norm, conv_w_pw1=m_conv_w_pw1,
             conv_b_pw1=m_conv_b_pw1, conv_w_dw=m_conv_w_dw, conv_b_dw=m_conv_b_dw, conv_ln_g=m_conv_ln_g,
             conv_ln_b=m_conv_ln_b, conv_w_pw2=m_conv_w_pw2, conv_b_pw2=m_conv_b_pw2, attn_w_qkv=m_attn_w_qkv,
             attn_w_o=m_attn_w_o, rel_bias=m_rel_bias, ffn_w_gate=m_ffn_w_gate, ffn_w_up=m_ffn_w_up,
             ffn_w_down=m_ffn_w_down)
    v = dict(norm_mix=v_norm_mix, norm_ffn=v_norm_ffn, final_norm=v_final_norm, conv_w_pw1=v_conv_w_pw1,
             conv_b_pw1=v_conv_b_pw1, conv_w_dw=v_conv_w_dw, conv_b_dw=v_conv_b_dw, conv_ln_g=v_conv_ln_g,
             conv_ln_b=v_conv_ln_b, conv_w_pw2=v_conv_w_pw2, conv_b_pw2=v_conv_b_pw2, attn_w_qkv=v_attn_w_qkv,
             attn_w_o=v_attn_w_o, rel_bias=v_rel_bias, ffn_w_gate=v_ffn_w_gate, ffn_w_up=v_ffn_w_up,
             ffn_w_down=v_ffn_w_down)

    me = 4 * lax.axis_index("x") + 2 * lax.axis_index("y") + lax.axis_index("c")
    depth = ffn_w_gate.shape[0]
    n_conv, _, cb = conv_w_dw.shape

    def sublayer(key, layer):
        if key in ("wgt", "wut", "wd"):
            return 2 * layer + 1
        return 4 * layer if key in ("w1t", "w2") else 4 * layer + 2

    def landing(block, own):
        land = lax.empty((N_DEV,) + block.shape, block.dtype)
        return lax.dynamic_update_slice(land, own[None], (me,) + (0,) * block.ndim)

    by_sub = {s: [] for s in range(2 * depth)}
    for name, key, cols in SHARDED:
        sw = (jnp.swapaxes(w[name], 1, 2) if cols else w[name]).astype(BF16)
        for layer in range(sw.shape[0]):
            by_sub[sublayer(key, layer)].append((key, layer, sw[layer]))
    likes = {s: jnp.zeros((sum(sh.size for _, _, sh in by_sub[s]) // 1024, 1024), BF16) for s in by_sub}
    dw_shard = jnp.pad(conv_w_dw.reshape(-1, cb), ((0, -(n_conv * CONV_WIDTH) % 8), (0, 0)))
    like_dw = jnp.zeros(dw_shard.shape, F32)
    groups = [([dw_shard], [landing(dw_shard, dw_shard)])]
    for s in range(2 * depth):
        groups.append(([sh for _, _, sh in by_sub[s]], [landing(sh, sh) for _, _, sh in by_sub[s]]))
    gather, _ = _exchange_start("gather_start", groups, scatter=False, carry=jnp.zeros((8, 128), F32))
    dw_filters = []

    def fetch(s, after):
        pieces = [(gather[s + 1], likes[s])]
        if s == 0:
            pieces.append((gather[0], like_dw))
        landed = _exchange_wait(f"gather_wait{s}", pieces, after)
        out = {key: g.reshape(g.shape[0] * g.shape[1], g.shape[2]) for (key, _, _), g in zip(by_sub[s], landed[0])}
        if s == 0:
            dw_all = landed[1][0]
            full = jnp.transpose(dw_all[:, :n_conv * CONV_WIDTH].reshape(N_DEV, n_conv, CONV_WIDTH, cb), (1, 2, 0, 3))
            full = jnp.pad(full.reshape(n_conv, CONV_WIDTH, N_DEV * cb), ((0, 0), (0, CONV_HALO - CONV_WIDTH), (0, 0)))
            dw_filters.extend(full[layer] for layer in range(n_conv))
        if "w1t" in out:
            out["wdw"] = dw_filters[s // 4]
        return out

    scatter, dw_grads = {}, {}

    def emit(s, gd, carry):
        parts = [gd[key].reshape(N_DEV, -1, gd[key].shape[1]) for key, _, _ in by_sub[s]]
        if "wdw" in gd:
            dw_grads[s // 4] = gd["wdw"]
        if s == 0:
            parts.append(_dw_blocks(jnp.stack([dw_grads[layer] for layer in range(n_conv)])))
        lands = [landing(p[0], lax.dynamic_index_in_dim(p, me, 0, keepdims=False)) for p in parts]
        groups = [(parts[:len(by_sub[s])], lands[:len(by_sub[s])])]
        if s == 0:
            groups.append((parts[-1:], lands[-1:]))
        scatter[s], carry = _exchange_start(f"scatter_start{s}", groups, scatter=True, carry=carry)
        return carry

    sm = {n: w[n] for n in SMALL}
    loss_cols, dx, gsm = _local_step(x[0], loss_target[0], sm, depth, fetch, emit)
    loss = lax.psum(jnp.sum(loss_cols), ("x", "y", "c"))

    order = sorted(scatter, reverse=True)
    pieces = [(scatter[s][0], likes[s]) for s in order] + [(scatter[0][1], like_dw)]
    landed = _exchange_wait("scatter_wait", pieces, dx)
    pack = _pack_small(gsm)
    ((pack_all,),) = _exchange("gather_small_grads", [([pack], pack)], scatter=False)

    grads = {}
    summed = {}
    for s, recv in zip(order, landed):
        for (key, layer, _), r in zip(by_sub[s], recv):
            summed[key, layer] = _sum8(f"sum_{key}{layer}", r)
    for name, key, cols in SHARDED:
        g = jnp.stack([summed[key, layer] for layer in range(w[name].shape[0])])
        grads[name] = jnp.swapaxes(g, 1, 2) if cols else g
    grads["conv_w_dw"] = _sum8("sum_wdw", landed[-1][0])[:n_conv * CONV_WIDTH].reshape(conv_w_dw.shape)
    pack_sum = _sum8("sum_small", pack_all)
    grads.update(_unpack_small(pack_sum, sm))

    delta, new_m, new_v = {}, {}, {}
    for name in [n for n, _, _ in SHARDED] + ["conv_w_dw"]:
        shape = w[name].shape
        res = _adamw(f"adamw_{name}", *[t.reshape(-1, shape[-1]) for t in (grads[name], w[name], m[name], v[name])])
        delta[name], new_m[name], new_v[name] = (t.reshape(shape) for t in res)
    res = _adamw("adamw_small", pack_sum, _pack_small(sm), _pack_small({n: m[n] for n in SMALL}),
                 _pack_small({n: v[n] for n in SMALL}))
    for dst, t in zip((delta, new_m, new_v), res):
        dst.update(_unpack_small(t, sm))

    outs = [loss, dx[None]]
    for d in (grads, delta, new_m, new_v):
        outs += [d[n] for n in ORDER]
    return tuple(outs)
```

```python
import functools
import math

import numpy as np
import jax
import jax.numpy as jnp
from jax import lax
from jax.experimental import pallas as pl
from jax.experimental.pallas import tpu as pltpu

F32 = jnp.float32
BF16 = jnp.bfloat16

N_DEV = 8
HEAD_DIM = 64
HEADS_PER_GROUP = 4
GROUP_COLS = HEADS_PER_GROUP * HEAD_DIM
DILATIONS = (1, 4, 16)
N_BACK = 128
N_GROUPS = 3
N_HEADS = 12
D_ATTN = 768
N_BUCKETS = 32
REL_MAX_DISTANCE = 2048
CONV_WIDTH = 31
CONV_HALO = 32
EPS = 1e-6
NEG_INF = -1e30
ADAM_LR, ADAM_B1, ADAM_B2, ADAM_EPS, ADAM_WD, ADAM_STEP = 0.001, 0.9, 0.999, 1e-08, 0.01, 10
V7X_VMEM_LIMIT_BYTES = 56 * 1024 * 1024
MESH = pl.DeviceIdType.MESH
ANY = pl.BlockSpec(memory_space=pl.ANY)


def _pick(n, prefs):
    for p in prefs:
        if n % p == 0:
            return p
    return n


def _params(sem, vmem=None):
    return pltpu.CompilerParams(dimension_semantics=sem, vmem_limit_bytes=vmem)


def _sigmoid(x):
    return 1.0 / (1.0 + jnp.exp(-x))


def _exchange(name, groups, scatter):
    n_arr = [len(arrs) for arrs, _ in groups]
    n_in = sum(n_arr) + len(groups)
    ng = len(groups)

    def body(*refs):
        ins, outs, (send_sems, recv_sems, local_sems) = refs[:n_in], refs[n_in:-3], refs[-3:]
        x, y, c = lax.axis_index("x"), lax.axis_index("y"), lax.axis_index("c")
        me = 4 * x + 2 * y + c
        pos_in = pos_out = 0
        plans = []
        for gi in range(ng):
            srcs = ins[pos_in:pos_in + n_arr[gi]]
            like = ins[pos_in + n_arr[gi]]
            dsts = outs[pos_out:pos_out + n_arr[gi]]
            pos_in += n_arr[gi] + 1
            pos_out += n_arr[gi]
            plans.append((gi, srcs, like, dsts))
        local = []
        for gi, srcs, like, dsts in plans:
            for s, d in zip(srcs, dsts):
                cp = pltpu.make_async_copy(s.at[me] if scatter else s, d.at[me], local_sems.at[gi])
                cp.start()
                local.append(cp)
        for delta in range(1, N_DEV):
            dx, dy, dc = (delta >> 2) & 1, (delta >> 1) & 1, delta & 1
            px, py, pc = (1 - x if dx else x), (1 - y if dy else y), (1 - c if dc else c)
            peer = 4 * px + 2 * py + pc
            for gi, srcs, like, dsts in plans:
                for s, d in zip(srcs, dsts):
                    pltpu.make_async_remote_copy(
                        src_ref=s.at[peer] if scatter else s, dst_ref=d.at[me],
                        send_sem=send_sems.at[gi, delta - 1], recv_sem=recv_sems.at[gi, delta - 1],
                        device_id=(px, py, pc), device_id_type=MESH).start()
        for delta in range(1, N_DEV):
            for gi, srcs, like, dsts in plans:
                pltpu.make_async_remote_copy(
                    src_ref=like, dst_ref=like, send_sem=send_sems.at[gi, delta - 1],
                    recv_sem=recv_sems.at[gi, delta - 1], device_id=(x, y, c), device_id_type=MESH).wait()
        for cp in local:
            cp.wait()

    operands, out_shape = [], []
    for arrs, like in groups:
        operands += list(arrs) + [like]
        for a in arrs:
            blk = a.shape[1:] if scatter else a.shape
            out_shape.append(jax.ShapeDtypeStruct((N_DEV,) + tuple(blk), a.dtype))
    outs = pl.pallas_call(
        body, name=name, out_shape=out_shape, in_specs=[ANY] * len(operands), out_specs=[ANY] * len(out_shape),
        scratch_shapes=[pltpu.SemaphoreType.DMA((ng, N_DEV - 1)), pltpu.SemaphoreType.DMA((ng, N_DEV - 1)),
                        pltpu.SemaphoreType.DMA((ng,))],
        compiler_params=pltpu.CompilerParams(has_side_effects=True),
    )(*operands)
    res, pos = [], 0
    for n in n_arr:
        res.append(list(outs[pos:pos + n]))
        pos += n
    return res


HBM = pl.BlockSpec(memory_space=pltpu.HBM)
SEM = pl.BlockSpec(memory_space=pltpu.SEMAPHORE)
EFFECT = pltpu.SideEffectType.DATAFLOW_SIDE_EFFECTING


def _in_hbm(a):
    return pltpu.with_memory_space_constraint(a, pltpu.HBM)


def _exchange_start(name, groups, scatter, carry):
    ns = [len(s) for s, _ in groups]
    n_in = 2 * sum(ns)

    def body(*refs):
        ins, outs = refs[:n_in], refs[n_in + 1:]
        x, y, c = lax.axis_index("x"), lax.axis_index("y"), lax.axis_index("c")
        me = 4 * x + 2 * y + c
        pi = po = 0
        for n in ns:
            srcs, lands = ins[pi:pi + n], ins[pi + n:pi + 2 * n]
            send_sems, recv_sems = outs[po], outs[po + 1]
            pi += 2 * n
            po += 2 + 2 * n
            for delta in range(1, N_DEV):
                dx, dy, dc = (delta >> 2) & 1, (delta >> 1) & 1, delta & 1
                px, py, pc = (1 - x if dx else x), (1 - y if dy else y), (1 - c if dc else c)
                peer = 4 * px + 2 * py + pc
                for s, d in zip(srcs, lands):
                    pltpu.make_async_remote_copy(
                        src_ref=s.at[peer] if scatter else s, dst_ref=d.at[me], send_sem=send_sems.at[delta - 1],
                        recv_sem=recv_sems.at[delta - 1], device_id=(px, py, pc), device_id_type=MESH).start()

    operands, out_shape, out_specs, aliases = [], [], [], {}
    for srcs, lands in groups:
        out_shape += [pltpu.SemaphoreType.DMA((N_DEV - 1,))] * 2
        out_specs += [SEM, SEM]
        for a in list(srcs) + list(lands):
            aliases[len(operands)] = len(out_shape)
            operands.append(_in_hbm(a))
            out_shape.append(pltpu.HBM(a.shape, a.dtype))
            out_specs.append(HBM)
    aliases[len(operands)] = len(out_shape)
    operands.append(_in_hbm(carry))
    out_shape.append(pltpu.HBM(carry.shape, carry.dtype))
    out_specs.append(HBM)
    outs = pl.pallas_call(
        body, name=name, out_shape=out_shape, in_specs=[HBM] * len(operands), out_specs=out_specs,
        input_output_aliases=aliases, compiler_params=pltpu.CompilerParams(has_side_effects=EFFECT),
    )(*operands)
    handles, po = [], 0
    for n in ns:
        handles.append((outs[po], outs[po + 1], list(outs[po + 2:po + 2 + n]), list(outs[po + 2 + n:po + 2 + 2 * n])))
        po += 2 + 2 * n
    return handles, outs[-1]


def _exchange_wait(name, pieces, after):
    ns = [len(h[2]) for h, _ in pieces]

    def body(*refs):
        x, y, c = lax.axis_index("x"), lax.axis_index("y"), lax.axis_index("c")
        pi = 0
        for n in ns:
            send_sems, recv_sems, like = refs[pi + 2 * n], refs[pi + 2 * n + 1], refs[pi + 2 * n + 2]
            pi += 2 * n + 3
            for delta in range(1, N_DEV):
                cp = pltpu.make_async_remote_copy(
                    src_ref=like, dst_ref=like, send_sem=send_sems.at[delta - 1], recv_sem=recv_sems.at[delta - 1],
                    device_id=(x, y, c), device_id_type=MESH)
                cp.wait_send()
                cp.wait_recv()

    operands, in_specs, out_shape, aliases = [], [], [], {}
    for (send_sems, recv_sems, srcs, lands), like in pieces:
        for a in srcs + lands:
            aliases[len(operands)] = len(out_shape)
            operands.append(a)
            in_specs.append(HBM)
            out_shape.append(pltpu.HBM(a.shape, a.dtype))
        operands += [send_sems, recv_sems, like]
        in_specs += [SEM, SEM, ANY]
    operands.append(after)
    in_specs.append(ANY)
    outs = pl.pallas_call(
        body, name=name, out_shape=out_shape, in_specs=in_specs, out_specs=[HBM] * len(out_shape),
        input_output_aliases=aliases, compiler_params=pltpu.CompilerParams(has_side_effects=EFFECT),
    )(*operands)
    res, po = [], 0
    for n in ns:
        res.append(list(outs[po + n:po + 2 * n]))
        po += 2 * n
    return res


def _sum8(name, r):
    _, rows, cols = r.shape
    tr = _pick(rows, (256, 128, 64, 32, 16, 8))

    def body(r_ref, o_ref):
        acc = r_ref[0].astype(F32)
        for p in range(1, N_DEV):
            acc = acc + r_ref[p].astype(F32)
        o_ref[...] = acc

    return pl.pallas_call(
        body, name=name, out_shape=jax.ShapeDtypeStruct((rows, cols), F32), grid=(rows // tr,),
        in_specs=[pl.BlockSpec((N_DEV, tr, cols), lambda i: (0, i, 0))],
        out_specs=pl.BlockSpec((tr, cols), lambda i: (i, 0)), compiler_params=_params(("parallel",)),
    )(r)


def _adamw(name, g, w, m, v):
    rows, cols = g.shape
    tr = _pick(rows, (512, 256, 128, 64, 32, 16, 8))
    c1 = 1.0 - ADAM_B1 ** ADAM_STEP
    c2 = 1.0 - ADAM_B2 ** ADAM_STEP

    def body(g_ref, w_ref, m_ref, v_ref, d_ref, nm_ref, nv_ref):
        gv = g_ref[...]
        nm = ADAM_B1 * m_ref[...] + (1.0 - ADAM_B1) * gv
        nv = ADAM_B2 * v_ref[...] + (1.0 - ADAM_B2) * (gv * gv)
        d_ref[...] = -ADAM_LR * ((nm / c1) / (jnp.sqrt(nv / c2) + ADAM_EPS) + ADAM_WD * w_ref[...])
        nm_ref[...] = nm
        nv_ref[...] = nv

    spec = pl.BlockSpec((tr, cols), lambda i: (i, 0))
    return pl.pallas_call(
        body, name=name, out_shape=[jax.ShapeDtypeStruct((rows, cols), F32)] * 3, grid=(rows // tr,),
        in_specs=[spec] * 4, out_specs=[spec] * 3, compiler_params=_params(("parallel",)),
    )(g, w, m, v)


def _mm_nt(name, a, ws, w_offs, n, epi, out_dtypes, extras=(), rows=(), tm=None, tn=None, chunk=None):
    m, k = a.shape
    tm = tm or _pick(m, (512, 256, 128))
    tn = tn or _pick(n, (1408, 1152, 1024, 512, 256, 128))
    chunk = chunk or tn
    nw, ne, nr = len(ws), len(extras), len(rows)

    def body(*refs):
        a_ref, w_refs = refs[0], refs[1:1 + nw]
        e_refs, r_refs = refs[1 + nw:1 + nw + ne], refs[1 + nw + ne:1 + nw + ne + nr]
        o_refs = refs[1 + nw + ne + nr:]
        av = a_ref[...].astype(BF16)
        for c0 in range(0, tn, chunk):
            cs = slice(c0, c0 + chunk)
            accs = [lax.dot_general(av, w[cs, :], (((1,), (1,)), ((), ())), preferred_element_type=F32)
                    for w in w_refs]
            outs = epi(accs, [e[:, cs] for e in e_refs], [r[:, cs] for r in r_refs])
            for o_ref, o in zip(o_refs, outs):
                o_ref[:, cs] = o.astype(o_ref.dtype)

    in_specs = [pl.BlockSpec((tm, k), lambda j, i: (i, 0))]
    in_specs += [pl.BlockSpec((tn, k), functools.partial(lambda j, i, off: (j + off, 0), off=off)) for off in w_offs]
    in_specs += [pl.BlockSpec((tm, tn), lambda j, i: (i, j))] * ne
    in_specs += [pl.BlockSpec((1, tn), lambda j, i: (0, j))] * nr
    return pl.pallas_call(
        body, name=name, out_shape=[jax.ShapeDtypeStruct((m, n), dt) for dt in out_dtypes],
        grid=(n // tn, m // tm), in_specs=in_specs,
        out_specs=[pl.BlockSpec((tm, tn), lambda j, i: (i, j))] * len(out_dtypes),
        compiler_params=_params(("parallel", "parallel"), V7X_VMEM_LIMIT_BYTES),
    )(a, *ws, *extras, *rows)


def _mm_nn(name, as_, bs, epi, out_dtype, extras=(), rows=(), tm=None):
    m, k = as_[0].shape
    n = bs[0].shape[1]
    tm = tm or _pick(m, (512, 256, 128))
    npair, ne, nr = len(as_), len(extras), len(rows)

    def body(*refs):
        a_refs, b_refs = refs[:npair], refs[npair:2 * npair]
        e_refs, r_refs = refs[2 * npair:2 * npair + ne], refs[2 * npair + ne:2 * npair + ne + nr]
        o_ref = refs[-1]
        acc = None
        for a_ref, b_ref in zip(a_refs, b_refs):
            p = jnp.dot(a_ref[...].astype(BF16), b_ref[...], preferred_element_type=F32)
            acc = p if acc is None else acc + p
        o_ref[...] = epi(acc, [e[...] for e in e_refs], [r[...] for r in r_refs]).astype(o_ref.dtype)

    in_specs = [pl.BlockSpec((tm, k), lambda i: (i, 0))] * npair
    in_specs += [pl.BlockSpec((k, n), lambda i: (0, 0))] * npair
    in_specs += [pl.BlockSpec((tm, n), lambda i: (i, 0))] * ne
    in_specs += [pl.BlockSpec((1, n), lambda i: (0, 0))] * nr
    return pl.pallas_call(
        body, name=name, out_shape=jax.ShapeDtypeStruct((m, n), out_dtype), grid=(m // tm,), in_specs=in_specs,
        out_specs=pl.BlockSpec((tm, n), lambda i: (i, 0)),
        compiler_params=_params(("parallel",), V7X_VMEM_LIMIT_BYTES),
    )(*as_, *bs, *extras, *rows)


def _mm_tn(name, a, b, colsum_b=False, tm=None, tk=1024):
    t, ma = a.shape
    nb = b.shape[1]
    tm = tm or _pick(ma, (1408, 1152, 1024, 768, 512, 256, 128))
    tk = _pick(t, (tk, 256, 128))
    nk = t // tk

    def body(*refs):
        a_ref, b_ref, o_ref = refs[0], refs[1], refs[2]
        acc_ref = refs[-1]
        kk = pl.program_id(1)
        bv = b_ref[...]

        @pl.when(kk == 0)
        def _():
            acc_ref[...] = jnp.zeros_like(acc_ref)

        acc_ref[...] += lax.dot_general(a_ref[...].astype(BF16), bv.astype(BF16), (((0,), (0,)), ((), ())),
                                        preferred_element_type=F32)
        if colsum_b:
            s_ref = refs[3]

            @pl.when((kk == 0) & (pl.program_id(0) == 0))
            def _():
                s_ref[...] = jnp.zeros_like(s_ref)

            @pl.when(pl.program_id(0) == 0)
            def _():
                s_ref[...] += jnp.sum(bv.astype(F32), axis=0, keepdims=True)

        @pl.when(kk == nk - 1)
        def _():
            o_ref[...] = acc_ref[...].astype(o_ref.dtype)

    out_shape = [jax.ShapeDtypeStruct((ma, nb), BF16)]
    out_specs = [pl.BlockSpec((tm, nb), lambda i, kk: (i, 0))]
    if colsum_b:
        out_shape.append(jax.ShapeDtypeStruct((1, nb), F32))
        out_specs.append(pl.BlockSpec((1, nb), lambda i, kk: (0, 0)))
    res = pl.pallas_call(
        body, name=name, out_shape=out_shape, grid=(ma // tm, nk),
        in_specs=[pl.BlockSpec((tk, tm), lambda i, kk: (kk, i)), pl.BlockSpec((tk, nb), lambda i, kk: (kk, 0))],
        out_specs=out_specs, scratch_shapes=[pltpu.VMEM((tm, nb), F32)],
        compiler_params=_params(("arbitrary", "arbitrary"), V7X_VMEM_LIMIT_BYTES),
    )(a, b)
    return res if colsum_b else res[0]


def _rmsnorm_fwd(name, x, g):
    t, d = x.shape
    tr = _pick(t, (512, 256, 128))

    def body(x_ref, g_ref, h_ref):
        xv = x_ref[...]
        r = lax.rsqrt(jnp.mean(xv * xv, axis=-1, keepdims=True) + EPS)
        h_ref[...] = (xv * r * g_ref[...]).astype(BF16)

    return pl.pallas_call(
        body, name=name, out_shape=jax.ShapeDtypeStruct((t, d), BF16), grid=(t // tr,),
        in_specs=[pl.BlockSpec((tr, d), lambda i: (i, 0)), pl.BlockSpec((1, d), lambda i: (0, 0))],
        out_specs=pl.BlockSpec((tr, d), lambda i: (i, 0)), compiler_params=_params(("parallel",)),
    )(x, g)


def _rmsnorm_bwd(name, dh, x, g, dx_out):
    t, d = x.shape
    tr = _pick(t, (512, 256, 128))

    def body(dh_ref, x_ref, g_ref, dxo_ref, dx_ref, dg_ref):
        xv = x_ref[...]
        r = lax.rsqrt(jnp.mean(xv * xv, axis=-1, keepdims=True) + EPS)
        yv = xv * r
        dhv = dh_ref[...].astype(F32)
        dy = dhv * g_ref[...]
        dx_ref[...] = dxo_ref[...] + r * (dy - yv * jnp.mean(dy * yv, axis=-1, keepdims=True))

        @pl.when(pl.program_id(0) == 0)
        def _():
            dg_ref[...] = jnp.zeros_like(dg_ref)

        dg_ref[...] += jnp.sum(dhv * yv, axis=0, keepdims=True)

    big = pl.BlockSpec((tr, d), lambda i: (i, 0))
    row = pl.BlockSpec((1, d), lambda i: (0, 0))
    return pl.pallas_call(
        body, name=name, out_shape=[jax.ShapeDtypeStruct((t, d), F32), jax.ShapeDtypeStruct((1, d), F32)],
        grid=(t // tr,), in_specs=[big, big, row, big], out_specs=[big, row],
        compiler_params=_params(("arbitrary",)),
    )(dh, x, g, dx_out)


def _loss_head(name, x, g, target):
    t, d = x.shape
    tr = _pick(t, (512, 256, 128))

    def body(x_ref, g_ref, t_ref, dx_ref, dg_ref, l_ref):
        xv = x_ref[...]
        r = lax.rsqrt(jnp.mean(xv * xv, axis=-1, keepdims=True) + EPS)
        yv = xv * r
        diff = yv * g_ref[...] - t_ref[...]
        dout = diff * (1.0 / d)
        dy = dout * g_ref[...]
        dx_ref[...] = r * (dy - yv * jnp.mean(dy * yv, axis=-1, keepdims=True))

        @pl.when(pl.program_id(0) == 0)
        def _():
            dg_ref[...] = jnp.zeros_like(dg_ref)
            l_ref[...] = jnp.zeros_like(l_ref)

        dg_ref[...] += jnp.sum(dout * yv, axis=0, keepdims=True)
        l_ref[...] += (0.5 / d) * jnp.sum(diff * diff, axis=0, keepdims=True)

    big = pl.BlockSpec((tr, d), lambda i: (i, 0))
    row = pl.BlockSpec((1, d), lambda i: (0, 0))
    return pl.pallas_call(
        body, name=name,
        out_shape=[jax.ShapeDtypeStruct((t, d), F32), jax.ShapeDtypeStruct((1, d), F32),
                   jax.ShapeDtypeStruct((1, d), F32)],
        grid=(t // tr,), in_specs=[big, row, big], out_specs=[big, row, row],
        compiler_params=_params(("arbitrary",)),
    )(x, g, target)


CONV_ROWS = 64
SUBLANES = 8


def _fill_window(win_ref, sh_ref, parts):
    rows = sh_ref.shape[2]
    for cb in range(win_ref.shape[0]):
        for r0, val in parts:
            win_ref[cb, r0:r0 + val.shape[0], :] = val[:, 128 * cb:128 * (cb + 1)]
        win_ref[cb, rows:rows + SUBLANES, :] = jnp.zeros((SUBLANES, 128), F32)
        for b in range(1, SUBLANES):
            sh_ref[b - 1, cb] = win_ref[cb, b:b + rows, :]


def _window_rows(win_ref, sh_ref, o, cb):
    b = o % SUBLANES
    if b == 0:
        return win_ref[cb, o:o + CONV_ROWS, :]
    return sh_ref[b - 1, cb, o - b:o - b + CONV_ROWS, :]


def _window_scratch(rows, c):
    return [pltpu.VMEM((c // 128, rows + SUBLANES, 128), F32), pltpu.VMEM((SUBLANES - 1, c // 128, rows, 128), F32)]


def _dwconv_fwd(name, glu, w_dw, b_dw, ln_g, ln_b):
    t, c = glu.shape
    tt = _pick(t, (256, 128))
    hb = tt // CONV_HALO

    def body(cur_ref, halo_ref, w_ref, b_ref, g_ref, be_ref, dw_ref, s_ref, win_ref, sh_ref):
        i = pl.program_id(0)
        halo = jnp.where(i > 0, halo_ref[...].astype(F32), 0.0)
        _fill_window(win_ref, sh_ref, [(0, halo), (CONV_HALO, cur_ref[...].astype(F32))])
        for r0 in range(0, tt, CONV_ROWS):
            for cb in range(c // 128):
                c0 = 128 * cb
                acc = jnp.zeros((CONV_ROWS, 128), F32) + b_ref[:, c0:c0 + 128]
                for k in range(CONV_WIDTH):
                    o = r0 + k + CONV_HALO - (CONV_WIDTH - 1)
                    acc = acc + w_ref[k:k + 1, c0:c0 + 128] * _window_rows(win_ref, sh_ref, o, cb)
                dw_ref[r0:r0 + CONV_ROWS, c0:c0 + 128] = acc
        u = dw_ref[...]
        mu = jnp.mean(u, axis=-1, keepdims=True)
        uc = u - mu
        rstd = lax.rsqrt(jnp.mean(uc * uc, axis=-1, keepdims=True) + EPS)
        z = uc * rstd * g_ref[...] + be_ref[...]
        s_ref[...] = (z * _sigmoid(z)).astype(BF16)

    big = pl.BlockSpec((tt, c), lambda i: (i, 0))
    row = pl.BlockSpec((1, c), lambda i: (0, 0))
    return pl.pallas_call(
        body, name=name, out_shape=[jax.ShapeDtypeStruct((t, c), F32), jax.ShapeDtypeStruct((t, c), BF16)],
        grid=(t // tt,),
        in_specs=[big, pl.BlockSpec((CONV_HALO, c), lambda i: (jnp.maximum(i * hb - 1, 0), 0)),
                  pl.BlockSpec((CONV_HALO, c), lambda i: (0, 0)), row, row, row],
        out_specs=[big, big], scratch_shapes=_window_scratch(tt + CONV_HALO, c),
        compiler_params=_params(("parallel",), V7X_VMEM_LIMIT_BYTES),
    )(glu, glu, w_dw, b_dw, ln_g, ln_b)


def _ln_silu_bwd(name, ds, dw, ln_g, ln_b):
    t, c = dw.shape
    tr = _pick(t, (256, 128))

    def body(ds_ref, dw_ref, g_ref, be_ref, o_ref, acc_ref):
        u = dw_ref[...]
        mu = jnp.mean(u, axis=-1, keepdims=True)
        uc = u - mu
        rstd = lax.rsqrt(jnp.mean(uc * uc, axis=-1, keepdims=True) + EPS)
        xh = uc * rstd
        z = xh * g_ref[...] + be_ref[...]
        sg = _sigmoid(z)
        dz = ds_ref[...].astype(F32) * (sg * (1.0 + z * (1.0 - sg)))
        dxh = dz * g_ref[...]
        du = rstd * (dxh - jnp.mean(dxh, axis=-1, keepdims=True) - xh * jnp.mean(dxh * xh, axis=-1, keepdims=True))
        o_ref[...] = du

        @pl.when(pl.program_id(0) == 0)
        def _():
            acc_ref[...] = jnp.zeros_like(acc_ref)

        acc_ref[0:1, :] += jnp.sum(dz * xh, axis=0, keepdims=True)
        acc_ref[1:2, :] += jnp.sum(dz, axis=0, keepdims=True)
        acc_ref[2:3, :] += jnp.sum(du, axis=0, keepdims=True)

    big = pl.BlockSpec((tr, c), lambda i: (i, 0))
    row = pl.BlockSpec((1, c), lambda i: (0, 0))
    return pl.pallas_call(
        body, name=name, out_shape=[jax.ShapeDtypeStruct((t, c), F32), jax.ShapeDtypeStruct((8, c), F32)],
        grid=(t // tr,), in_specs=[big, big, row, row],
        out_specs=[big, pl.BlockSpec((8, c), lambda i: (0, 0))], compiler_params=_params(("arbitrary",)),
    )(ds, dw, ln_g, ln_b)


def _dwconv_bwd(name, ddw, a, gt, w_dw):
    t, c = ddw.shape
    tt = _pick(t, (256, 128))
    hb = tt // CONV_HALO
    last = t // tt - 1
    back = CONV_WIDTH - 1

    def body(d_ref, dn_ref, a_ref, ap_ref, g_ref, gp_ref, w_ref, du_ref, dwk_ref, db_ref,
             wd_ref, shd_ref, wg_ref, shg_ref, dg_ref, dwk8_ref):
        i = pl.program_id(0)
        _fill_window(wd_ref, shd_ref, [(0, d_ref[...]), (tt, jnp.where(i < last, dn_ref[...], 0.0))])
        glu_prev = ap_ref[...].astype(F32) * _sigmoid(gp_ref[...].astype(F32))
        av = a_ref[...].astype(F32)
        sg = _sigmoid(g_ref[...].astype(F32))
        _fill_window(wg_ref, shg_ref, [(0, jnp.where(i > 0, glu_prev, 0.0)), (CONV_HALO, av * sg)])

        @pl.when(i == 0)
        def _():
            dwk8_ref[...] = jnp.zeros_like(dwk8_ref)
            db_ref[...] = jnp.zeros_like(db_ref)

        for r0 in range(0, tt, CONV_ROWS):
            for cb in range(c // 128):
                c0 = 128 * cb
                dcur = wd_ref[cb, r0:r0 + CONV_ROWS, :]
                acc = jnp.zeros((CONV_ROWS, 128), F32)
                for k in range(CONV_WIDTH):
                    acc = acc + w_ref[k:k + 1, c0:c0 + 128] * _window_rows(wd_ref, shd_ref, r0 + back - k, cb)
                    p = dcur * _window_rows(wg_ref, shg_ref, r0 + k + CONV_HALO - back, cb)
                    s8 = p[0:SUBLANES]
                    for q in range(SUBLANES, CONV_ROWS, SUBLANES):
                        s8 = s8 + p[q:q + SUBLANES]
                    dwk8_ref[SUBLANES * k:SUBLANES * (k + 1), c0:c0 + 128] += s8
                dg_ref[r0:r0 + CONV_ROWS, c0:c0 + 128] = acc

        @pl.when(i == last)
        def _():
            for k in range(CONV_WIDTH):
                dwk_ref[k:k + 1, :] = jnp.sum(dwk8_ref[SUBLANES * k:SUBLANES * (k + 1), :], axis=0, keepdims=True)
            dwk_ref[CONV_WIDTH:, :] = jnp.zeros((CONV_HALO - CONV_WIDTH, c), F32)
        dglu = dg_ref[...]
        da = dglu * sg
        dgate = dglu * av * sg * (1.0 - sg)
        du_ref[:, 0:c] = da.astype(BF16)
        du_ref[:, c:] = dgate.astype(BF16)
        db_ref[:, 0:c] += jnp.sum(da, axis=0, keepdims=True)
        db_ref[:, c:] += jnp.sum(dgate, axis=0, keepdims=True)

    big = pl.BlockSpec((tt, c), lambda i: (i, 0))
    prev = pl.BlockSpec((CONV_HALO, c), lambda i: (jnp.maximum(i * hb - 1, 0), 0))
    nxt = pl.BlockSpec((CONV_HALO, c), lambda i: (jnp.minimum((i + 1) * hb, t // CONV_HALO - 1), 0))
    return pl.pallas_call(
        body, name=name,
        out_shape=[jax.ShapeDtypeStruct((t, 2 * c), BF16), jax.ShapeDtypeStruct((CONV_HALO, c), F32),
                   jax.ShapeDtypeStruct((1, 2 * c), F32)],
        grid=(t // tt,),
        in_specs=[big, nxt, big, prev, big, prev, pl.BlockSpec((CONV_HALO, c), lambda i: (0, 0))],
        out_specs=[pl.BlockSpec((tt, 2 * c), lambda i: (i, 0)), pl.BlockSpec((CONV_HALO, c), lambda i: (0, 0)),
                   pl.BlockSpec((1, 2 * c), lambda i: (0, 0))],
        scratch_shapes=_window_scratch(tt + CONV_HALO, c) + _window_scratch(tt + CONV_HALO, c)
        + [pltpu.VMEM((tt, c), F32), pltpu.VMEM((SUBLANES * CONV_HALO, c), F32)],
        compiler_params=_params(("arbitrary",), V7X_VMEM_LIMIT_BYTES),
    )(ddw, ddw, a, a, gt, gt, w_dw)


def _bucket_tables():
    i = np.arange(N_BACK)[:, None]
    j = np.arange(2 * N_BACK)[None, :]
    dist = i + N_BACK - j
    valid = (dist >= 0) & (dist <= N_BACK)
    max_exact = N_BUCKETS // 2
    out = []
    for d in DILATIONS:
        n = np.maximum(dist * d, 0)
        nf = np.maximum(n, 1).astype(np.float32)
        large = max_exact + (np.log(nf / np.float32(max_exact)) / np.float32(math.log(REL_MAX_DISTANCE / max_exact))
                             * np.float32(N_BUCKETS - max_exact)).astype(np.int32)
        large = np.minimum(large, N_BUCKETS - 1)
        out.append(np.where(valid, np.where(n < max_exact, n, large), -1))
    return np.stack(out).astype(np.int32)


def _bias_build(name, rel_bias, buckets):
    def body(tbl_ref, bk_ref, o_ref):
        g = pl.program_id(0)
        bk = bk_ref[0]
        for h in range(HEADS_PER_GROUP):
            acc = jnp.zeros(bk.shape, F32)
            for b in range(N_BUCKETS):
                acc = jnp.where(bk == b, tbl_ref[b, g * HEADS_PER_GROUP + h], acc)
            o_ref[h] = jnp.where(bk < 0, NEG_INF, acc)

    return pl.pallas_call(
        body, name=name, out_shape=jax.ShapeDtypeStruct((N_HEADS, N_BACK, 2 * N_BACK), F32), grid=(N_GROUPS,),
        in_specs=[pl.BlockSpec(memory_space=pltpu.SMEM), pl.BlockSpec((1, N_BACK, 2 * N_BACK), lambda g: (g, 0, 0))],
        out_specs=pl.BlockSpec((HEADS_PER_GROUP, N_BACK, 2 * N_BACK), lambda g: (g, 0, 0)),
        compiler_params=_params(("arbitrary",)),
    )(rel_bias, buckets)


def _bias_grad(name, dbs, buckets):
    nd = len(dbs)

    def body(*refs):
        bk = refs[nd][0]
        o_ref = refs[nd + 1]
        lane = lax.broadcasted_iota(jnp.int32, (1, 128), 1)
        db = [sum(r[h] for r in refs[:nd]) for h in range(HEADS_PER_GROUP)]
        for b in range(N_BUCKETS):
            row = jnp.zeros((1, 128), F32)
            for h in range(HEADS_PER_GROUP):
                s = jnp.sum(jnp.where(bk == b, db[h], 0.0), axis=0, keepdims=True)
                s = jnp.sum(s, axis=1, keepdims=True)
                row = jnp.where(lane // 32 == h, s, row)
            o_ref[0, b:b + 1, :] = row

    spec = pl.BlockSpec((HEADS_PER_GROUP, N_BACK, 2 * N_BACK), lambda g: (g, 0, 0))
    return pl.pallas_call(
        body, name=name, out_shape=jax.ShapeDtypeStruct((N_GROUPS, N_BUCKETS, 128), F32), grid=(N_GROUPS,),
        in_specs=[spec] * nd + [pl.BlockSpec((1, N_BACK, 2 * N_BACK), lambda g: (g, 0, 0))],
        out_specs=pl.BlockSpec((1, N_BUCKETS, 128), lambda g: (g, 0, 0)), compiler_params=_params(("arbitrary",)),
    )(*dbs, buckets)


def _head_cols(h):
    return slice(h * HEAD_DIM, (h + 1) * HEAD_DIM)


def _attn_fwd(name, qkv, bias, g):
    t = qkv.shape[0]
    d = DILATIONS[g]
    tq = t // d
    nblk = qkv.shape[1] // GROUP_COLS
    scale = HEAD_DIM ** -0.5

    def body(q_ref, kp_ref, kc_ref, vp_ref, vc_ref, b_ref, o_ref, l_ref):
        n = pl.program_id(1)
        col = lax.broadcasted_iota(jnp.int32, (N_BACK, 2 * N_BACK), 1)
        keep = (col >= N_BACK) | (n > 0)
        lane = lax.broadcasted_iota(jnp.int32, (N_BACK, 128), 1)
        lse_tile = jnp.zeros((N_BACK, 128), F32)
        outs = []
        for h in range(HEADS_PER_GROUP):
            hc = _head_cols(h)
            kk = jnp.concatenate([kp_ref[:, hc], kc_ref[:, hc]], axis=0)
            vv = jnp.concatenate([vp_ref[:, hc], vc_ref[:, hc]], axis=0)
            s = lax.dot_general(q_ref[:, hc], kk, (((1,), (1,)), ((), ())), preferred_element_type=F32)
            s = jnp.where(keep, s * scale + b_ref[h], NEG_INF)
            m = jnp.max(s, axis=-1, keepdims=True)
            p = jnp.exp(s - m)
            den = jnp.sum(p, axis=-1, keepdims=True)
            outs.append(jnp.dot(p.astype(BF16), vv, preferred_element_type=F32) / den)
            lse_tile = jnp.where(lane // 32 == h, m + jnp.log(den), lse_tile)
        o_ref[...] = jnp.concatenate(outs, axis=1)
        l_ref[...] = lse_tile

    def blk(part, prev):
        if prev:
            return pl.BlockSpec((N_BACK, GROUP_COLS), lambda r, n: (jnp.maximum(n - 1, 0), r * nblk + part))
        return pl.BlockSpec((N_BACK, GROUP_COLS), lambda r, n: (n, r * nblk + part))

    qv = qkv.reshape(tq, d * qkv.shape[1])
    o, l = pl.pallas_call(
        body, name=name,
        out_shape=[jax.ShapeDtypeStruct((tq, d * GROUP_COLS), F32), jax.ShapeDtypeStruct((tq, d * 128), F32)],
        grid=(d, tq // N_BACK),
        in_specs=[blk(0, False), blk(1, True), blk(1, False), blk(2, True), blk(2, False),
                  pl.BlockSpec((HEADS_PER_GROUP, N_BACK, 2 * N_BACK), lambda r, n: (g, 0, 0))],
        out_specs=[pl.BlockSpec((N_BACK, GROUP_COLS), lambda r, n: (n, r)),
                   pl.BlockSpec((N_BACK, 128), lambda r, n: (n, r))],
        compiler_params=_params(("parallel", "parallel")),
    )(qv, qv, qv, qv, qv, bias)
    return o.reshape(t, GROUP_COLS), l.reshape(t, 128)


def _group_weights(l_refs, h):
    ls = [l_ref[:, 32 * h:32 * h + 1] for l_ref in l_refs]
    m = jnp.maximum(jnp.maximum(ls[0], ls[1]), ls[2])
    es = [jnp.exp(l - m) for l in ls]
    tot = es[0] + es[1] + es[2]
    return [e / tot for e in es]


def _attn_merge(name, os_, ls):
    t = os_[0].shape[0]
    tr = _pick(t, (512, 256, 128))

    def body(o0, o1, o2, l0, l1, l2, out_ref):
        o_refs = (o0, o1, o2)
        pieces = [[None] * HEADS_PER_GROUP for _ in range(N_GROUPS)]
        for h in range(HEADS_PER_GROUP):
            al = _group_weights((l0, l1, l2), h)
            for g in range(N_GROUPS):
                pieces[g][h] = o_refs[g][:, _head_cols(h)] * al[g]
        out_ref[...] = jnp.concatenate([p for row in pieces for p in row], axis=1).astype(BF16)

    so = pl.BlockSpec((tr, GROUP_COLS), lambda i: (i, 0))
    sl = pl.BlockSpec((tr, 128), lambda i: (i, 0))
    return pl.pallas_call(
        body, name=name, out_shape=jax.ShapeDtypeStruct((t, D_ATTN), BF16), grid=(t // tr,),
        in_specs=[so] * 3 + [sl] * 3, out_specs=pl.BlockSpec((tr, D_ATTN), lambda i: (i, 0)),
        compiler_params=_params(("parallel",)),
    )(*os_, *ls)


def _attn_bwd_prep(name, d_out, os_, ls):
    t = d_out.shape[0]
    tr = _pick(t, (512, 256, 128))

    def body(do_ref, o0, o1, o2, l0, l1, l2, d0, d1, d2, c0, c1, c2):
        o_refs, d_refs, c_refs = (o0, o1, o2), (d0, d1, d2), (c0, c1, c2)
        lane = lax.broadcasted_iota(jnp.int32, (tr, 128), 1)
        dos = [[None] * HEADS_PER_GROUP for _ in range(N_GROUPS)]
        cs = [jnp.zeros((tr, 128), F32) for _ in range(N_GROUPS)]
        for h in range(HEADS_PER_GROUP):
            al = _group_weights((l0, l1, l2), h)
            tot = jnp.zeros((tr, 1), F32)
            for g in range(N_GROUPS):
                dv = do_ref[:, g * GROUP_COLS + h * HEAD_DIM:g * GROUP_COLS + (h + 1) * HEAD_DIM].astype(F32)
                tot = tot + al[g] * jnp.sum(dv * o_refs[g][:, _head_cols(h)], axis=-1, keepdims=True)
                dos[g][h] = dv * al[g]
            for g in range(N_GROUPS):
                cs[g] = jnp.where(lane // 32 == h, -al[g] * tot, cs[g])
        for g in range(N_GROUPS):
            d_refs[g][...] = jnp.concatenate(dos[g], axis=1).astype(BF16)
            c_refs[g][...] = cs[g]

    so = pl.BlockSpec((tr, GROUP_COLS), lambda i: (i, 0))
    sl = pl.BlockSpec((tr, 128), lambda i: (i, 0))
    res = pl.pallas_call(
        body, name=name,
        out_shape=[jax.ShapeDtypeStruct((t, GROUP_COLS), BF16)] * 3 + [jax.ShapeDtypeStruct((t, 128), F32)] * 3,
        grid=(t // tr,), in_specs=[pl.BlockSpec((tr, D_ATTN), lambda i: (i, 0))] + [so] * 3 + [sl] * 3,
        out_specs=[so] * 3 + [sl] * 3, compiler_params=_params(("parallel",)),
    )(d_out, *os_, *ls)
    return res[:3], res[3:]


def _attn_bwd(name, qkv, do, lse, cterm, bias, g):
    t = qkv.shape[0]
    d = DILATIONS[g]
    tq = t // d
    nb = tq // N_BACK
    nblk = qkv.shape[1] // GROUP_COLS
    scale = HEAD_DIM ** -0.5
    nt = (((1,), (1,)), ((), ()))
    tn = (((0,), (0,)), ((), ()))

    def body(qn, qx, kp, kn, vp, vn, don, dox, ln, lx, cn, cx, b_ref, dqkv_ref, db_ref):
        n = pl.program_id(1)
        has_prev = n > 0
        has_next = n < nb - 1

        @pl.when((n == 0) & (pl.program_id(0) == 0))
        def _():
            db_ref[...] = jnp.zeros_like(db_ref)

        dqs, dks, dvs = [], [], []
        for h in range(HEADS_PER_GROUP):
            hc = _head_cols(h)
            st = slice(32 * h, 32 * h + 1)
            b_prev, b_same = b_ref[h, :, 0:N_BACK], b_ref[h, :, N_BACK:]

            def pair(q, k, v, dout, l, cc, bias_blk, on):
                s = lax.dot_general(q, k, nt, preferred_element_type=F32) * scale + bias_blk
                p = jnp.where(on, jnp.exp(s - l), 0.0)
                dp = lax.dot_general(dout, v, nt, preferred_element_type=F32)
                return p, p * (dp + cc)

            q0, q1, k0, k1, v0, v1 = qn[:, hc], qx[:, hc], kp[:, hc], kn[:, hc], vp[:, hc], vn[:, hc]
            d0, d1 = don[:, hc], dox[:, hc]
            p_a, ds_a = pair(q0, k1, v1, d0, ln[:, st], cn[:, st], b_same, True)
            p_b, ds_b = pair(q0, k0, v0, d0, ln[:, st], cn[:, st], b_prev, has_prev)
            p_c, ds_c = pair(q1, k1, v1, d1, lx[:, st], cx[:, st], b_prev, has_next)
            ds_a16, ds_b16, ds_c16 = ds_a.astype(BF16), ds_b.astype(BF16), ds_c.astype(BF16)
            dqs.append(scale * (jnp.dot(ds_a16, k1, preferred_element_type=F32)
                                + jnp.dot(ds_b16, k0, preferred_element_type=F32)))
            dks.append(scale * (lax.dot_general(ds_a16, q0, tn, preferred_element_type=F32)
                                + lax.dot_general(ds_c16, q1, tn, preferred_element_type=F32)))
            dvs.append(lax.dot_general(p_a.astype(BF16), d0, tn, preferred_element_type=F32)
                       + lax.dot_general(p_c.astype(BF16), d1, tn, preferred_element_type=F32))
            db_ref[h, :, 0:N_BACK] += ds_b
            db_ref[h, :, N_BACK:] += ds_a
        dqkv_ref[...] = jnp.concatenate(dqs + dks + dvs, axis=1).astype(BF16)

    def rows(which):
        if which == "prev":
            return lambda n: jnp.maximum(n - 1, 0)
        if which == "next":
            return lambda n: jnp.minimum(n + 1, nb - 1)
        return lambda n: n

    def qkv_blk(part, which):
        f = rows(which)
        return pl.BlockSpec((N_BACK, GROUP_COLS), lambda r, n: (f(n), r * nblk + part))

    def grp_blk(width, which):
        f = rows(which)
        return pl.BlockSpec((N_BACK, width), lambda r, n: (f(n), r))

    qv = qkv.reshape(tq, d * qkv.shape[1])
    dov = do.reshape(tq, d * GROUP_COLS)
    lv = lse.reshape(tq, d * 128)
    cv = cterm.reshape(tq, d * 128)
    dqkv_g, db = pl.pallas_call(
        body, name=name,
        out_shape=[jax.ShapeDtypeStruct((tq, d * 3 * GROUP_COLS), BF16),
                   jax.ShapeDtypeStruct((HEADS_PER_GROUP, N_BACK, 2 * N_BACK), F32)],
        grid=(d, nb),
        in_specs=[qkv_blk(0, "same"), qkv_blk(0, "next"), qkv_blk(1, "prev"), qkv_blk(1, "same"),
                  qkv_blk(2, "prev"), qkv_blk(2, "same"), grp_blk(GROUP_COLS, "same"), grp_blk(GROUP_COLS, "next"),
                  grp_blk(128, "same"), grp_blk(128, "next"), grp_blk(128, "same"), grp_blk(128, "next"),
                  pl.BlockSpec((HEADS_PER_GROUP, N_BACK, 2 * N_BACK), lambda r, n: (g, 0, 0))],
        out_specs=[pl.BlockSpec((N_BACK, 3 * GROUP_COLS), lambda r, n: (n, r)),
                   pl.BlockSpec((HEADS_PER_GROUP, N_BACK, 2 * N_BACK), lambda r, n: (0, 0, 0))],
        compiler_params=_params(("arbitrary", "arbitrary")),
    )(qv, qv, qv, qv, qv, qv, dov, dov, lv, lv, cv, cv, bias)
    return dqkv_g.reshape(t, 3 * GROUP_COLS), db


def _row(v):
    return v.reshape(1, -1)


def _plain(accs, extras, rows):
    return (accs[0],)


def _glu_epi(accs, extras, rows):
    a = (accs[0] + rows[0]).astype(BF16)
    gt = (accs[1] + rows[1]).astype(BF16)
    return a, gt, a.astype(F32) * _sigmoid(gt.astype(F32))


def _swiglu_epi(accs, extras, rows):
    gq, uq = accs[0].astype(BF16), accs[1].astype(BF16)
    gf = gq.astype(F32)
    return gq, uq, gf * _sigmoid(gf) * uq.astype(F32)


def _swiglu_bwd_epi(accs, extras, rows):
    gf, uf = extras[0].astype(F32), extras[1].astype(F32)
    sg = _sigmoid(gf)
    return accs[0] * uf * (sg * (1.0 + gf * (1.0 - sg))), accs[0] * gf * sg


def _residual(acc, extras, rows):
    out = acc + extras[0]
    return out + rows[0] if rows else out


def _identity(acc, extras, rows):
    return acc


def _group_rows(w):
    parts = [w[p * D_ATTN:(p + 1) * D_ATTN].reshape(N_GROUPS, GROUP_COLS, -1) for p in range(3)]
    return jnp.concatenate(parts, axis=1)


def _ungroup_rows(wg):
    return jnp.concatenate([wg[g][p * GROUP_COLS:(p + 1) * GROUP_COLS] for p in range(3) for g in range(N_GROUPS)],
                           axis=0)


def _local_step(x, target, sm, depth, fetch, emit):
    d_model = x.shape[1]
    buckets = jnp.asarray(_bucket_tables())
    bias = _bias_build("bias_build", sm["rel_bias"], buckets)
    saved = []
    for i in range(depth):
        j = i // 2
        rec = {"x_mix": x}
        wm = fetch(2 * i, x)
        h = _rmsnorm_fwd(f"rms_mix_fwd{i}", x, _row(sm["norm_mix"][i]))
        rec.update(h_mix=h, wm=wm)
        if i % 2 == 0:
            c = wm["w2"].shape[0]
            tn = _pick(c, (512, 256, 128))
            b1 = sm["conv_b_pw1"][j]
            a, gt, glu = _mm_nt(f"conv_pw1_fwd{j}", h, [wm["w1t"]] * 2, [0, c // tn], c, _glu_epi, (BF16,) * 3,
                                rows=[_row(b1[:c]), _row(b1[c:])], tn=tn)
            dw, s = _dwconv_fwd(f"dwconv_fwd{j}", glu, wm["wdw"], _row(sm["conv_b_dw"][j]),
                                _row(sm["conv_ln_g"][j]), _row(sm["conv_ln_b"][j]))
            x = _mm_nn(f"conv_pw2_fwd{j}", [s], [wm["w2"]], _residual, F32, extras=[x],
                       rows=[_row(sm["conv_b_pw2"][j])])
            rec.update(a=a, gt=gt, dw=dw, s=s)
        else:
            wq = _group_rows(wm["wqkvt"])
            qkv = _mm_nt(f"attn_qkv_fwd{j}", h, [wq[g] for g in range(N_GROUPS)], [0] * N_GROUPS, D_ATTN,
                         lambda accs, e, r: tuple(accs), (BF16,) * N_GROUPS, tn=D_ATTN)
            og = [_attn_fwd(f"attn_fwd{j}_{g}", qkv[g], bias, g) for g in range(N_GROUPS)]
            os_, ls = [o for o, _ in og], [l for _, l in og]
            om = _attn_merge(f"attn_merge{j}", os_, ls)
            x = _mm_nt(f"attn_out_fwd{j}", om, [wm["wot"]], [0], d_model, lambda accs, e, r: (accs[0] + e[0],),
                       (F32,), extras=[x])[0]
            rec.update(qkv=qkv, os=os_, ls=ls, om=om, wq=wq)
        rec["x_ffn"] = x
        wf = fetch(2 * i + 1, x)
        h2 = _rmsnorm_fwd(f"rms_ffn_fwd{i}", x, _row(sm["norm_ffn"][i]))
        f = wf["wd"].shape[0]
        gq, uq, act = _mm_nt(f"ffn_up_fwd{i}", h2, [wf["wgt"], wf["wut"]], [0, 0], f, _swiglu_epi, (BF16,) * 3,
                             tn=f, chunk=_pick(f, (256, 128)))
        x = _mm_nn(f"ffn_down_fwd{i}", [act], [wf["wd"]], _residual, F32, extras=[x])
        rec.update(h_ffn=h2, gq=gq, uq=uq, act=act, wf=wf)
        saved.append(rec)

    dx, g_final, loss_cols = _loss_head("loss_head", x, _row(sm["final_norm"]), target)

    g_mix, g_ffn = [None] * depth, [None] * depth
    nconv = (depth + 1) // 2
    g_b1, g_bdw, g_lng, g_lnb, g_b2 = ([None] * nconv for _ in range(5))
    dbias = []
    for i in reversed(range(depth)):
        j = i // 2
        rec = saved[i]
        wm, wf = rec["wm"], rec["wf"]
        f = wf["wd"].shape[0]
        dgate, dup = _mm_nt(f"ffn_down_bwd{i}", dx, [wf["wd"]], [0], f, _swiglu_bwd_epi, (BF16, BF16),
                            extras=[rec["gq"], rec["uq"]], tn=f, chunk=_pick(f, (256, 128)))
        gf = {"wd": _mm_tn(f"ffn_down_dw{i}", rec["act"], dx),
              "wgt": _mm_tn(f"ffn_gate_dw{i}", dgate, rec["h_ffn"]),
              "wut": _mm_tn(f"ffn_up_dw{i}", dup, rec["h_ffn"])}
        dh = _mm_nn(f"ffn_up_bwd{i}", [dgate, dup], [wf["wgt"], wf["wut"]], _identity, BF16, tm=256)
        dx, g_ffn[i] = _rmsnorm_bwd(f"rms_ffn_bwd{i}", dh, rec["x_ffn"], _row(sm["norm_ffn"][i]), dx)
        dx = emit(2 * i + 1, gf, dx)
        if i % 2 == 0:
            c = wm["w2"].shape[0]
            gm = {}
            gm["w2"], g_b2[j] = _mm_tn(f"conv_pw2_dw{j}", rec["s"], dx, colsum_b=True)
            ds = _mm_nt(f"conv_pw2_bwd{j}", dx, [wm["w2"]], [0], c, _plain, (BF16,))[0]
            ddw, sums = _ln_silu_bwd(f"ln_silu_bwd{j}", ds, rec["dw"], _row(sm["conv_ln_g"][j]),
                                     _row(sm["conv_ln_b"][j]))
            g_lng[j], g_lnb[j], g_bdw[j] = sums[0], sums[1], sums[2]
            du, dwk, db1 = _dwconv_bwd(f"dwconv_bwd{j}", ddw, rec["a"], rec["gt"], wm["wdw"])
            gm["wdw"] = dwk[:CONV_WIDTH]
            g_b1[j] = db1[0]
            gm["w1t"] = _mm_tn(f"conv_pw1_dw{j}", du, rec["h_mix"])
            dh = _mm_nn(f"conv_pw1_bwd{j}", [du], [wm["w1t"]], _identity, BF16)
        else:
            gm = {"wot": _mm_tn(f"attn_out_dw{j}", dx, rec["om"])}
            d_om = _mm_nn(f"attn_out_bwd{j}", [dx], [wm["wot"]], _identity, BF16)
            dos, cs = _attn_bwd_prep(f"attn_bwd_prep{j}", d_om, rec["os"], rec["ls"])
            back = [_attn_bwd(f"attn_bwd{j}_{g}", rec["qkv"][g], dos[g], rec["ls"][g], cs[g], bias, g)
                    for g in range(N_GROUPS)]
            dqkv = [b[0] for b in back]
            dbias.append(jnp.concatenate([b[1] for b in back], axis=0))
            wq = rec["wq"]
            dh = _mm_nn(f"attn_qkv_bwd{j}", dqkv, [wq[g] for g in range(N_GROUPS)], _identity, BF16)
            gm["wqkvt"] = _ungroup_rows([_mm_tn(f"attn_qkv_dw{j}_{g}", dqkv[g], rec["h_mix"])
                                         for g in range(N_GROUPS)])
        dx, g_mix[i] = _rmsnorm_bwd(f"rms_mix_bwd{i}", dh, rec["x_mix"], _row(sm["norm_mix"][i]), dx)
        dx = emit(2 * i, gm, dx)

    gb = _bias_grad("bias_grad", dbias, buckets)
    g_rel = jnp.transpose(gb[:, :, ::32], (1, 0, 2)).reshape(N_BUCKETS, N_HEADS)
    gsm = {
        "norm_mix": jnp.concatenate(g_mix, axis=0), "norm_ffn": jnp.concatenate(g_ffn, axis=0),
        "final_norm": g_final[0], "conv_b_pw1": jnp.stack(g_b1), "conv_b_dw": jnp.stack(g_bdw),
        "conv_ln_g": jnp.stack(g_lng), "conv_ln_b": jnp.stack(g_lnb),
        "conv_b_pw2": jnp.concatenate(g_b2, axis=0), "rel_bias": g_rel,
    }
    return loss_cols, dx, gsm


SMALL = ("norm_mix", "norm_ffn", "final_norm", "conv_b_pw1", "conv_b_dw", "conv_ln_g", "conv_ln_b", "conv_b_pw2",
         "rel_bias")
SHARDED = (("conv_w_pw1", "w1t", True), ("conv_w_pw2", "w2", False), ("attn_w_qkv", "wqkvt", True),
           ("attn_w_o", "wot", True), ("ffn_w_gate", "wgt", True), ("ffn_w_up", "wut", True),
           ("ffn_w_down", "wd", False))
ORDER = ("norm_mix", "norm_ffn", "final_norm", "conv_w_pw1", "conv_b_pw1", "conv_w_dw", "conv_b_dw", "conv_ln_g",
         "conv_ln_b", "conv_w_pw2", "conv_b_pw2", "attn_w_qkv", "attn_w_o", "rel_bias", "ffn_w_gate", "ffn_w_up",
         "ffn_w_down")
PACK_LANES = 128
PACK_ROW_TILE = 8


def _pack_small(vals):
    flat = jnp.concatenate([vals[n].reshape(-1) for n in SMALL])
    per_tile = PACK_LANES * PACK_ROW_TILE
    return jnp.pad(flat, (0, -flat.shape[0] % per_tile)).reshape(-1, PACK_LANES)


def _unpack_small(pack, like):
    flat, out, pos = pack.reshape(-1), {}, 0
    for n in SMALL:
        out[n] = flat[pos:pos + like[n].size].reshape(like[n].shape)
        pos += like[n].size
    return out


def _dw_blocks(w):
    l, k, c = w.shape
    blk = jnp.transpose(w.reshape(l, k, N_DEV, c // N_DEV), (2, 0, 1, 3)).reshape(N_DEV, l * k, c // N_DEV)
    return jnp.pad(blk, ((0, 0), (0, -(l * k) % 8), (0, 0)))


def kernel(x, norm_mix, norm_ffn, final_norm, conv_w_pw1, conv_b_pw1, conv_w_dw, conv_b_dw, conv_ln_g, conv_ln_b, conv_w_pw2, conv_b_pw2, attn_w_qkv, attn_w_o, rel_bias, ffn_w_gate, ffn_w_up, ffn_w_down, loss_target, m_norm_mix, m_norm_ffn, m_final_norm, m_conv_w_pw1, m_conv_b_pw1, m_conv_w_dw, m_conv_b_dw, m_conv_ln_g, m_conv_ln_b, m_conv_w_pw2, m_conv_b_pw2, m_attn_w_qkv, m_attn_w_o, m_rel_bias, m_ffn_w_gate, m_ffn_w_up, m_ffn_w_down, v_norm_mix, v_norm_ffn, v_final_norm, v_conv_w_pw1, v_conv_b_pw1, v_conv_w_dw, v_conv_b_dw, v_conv_ln_g, v_conv_ln_b, v_conv_w_pw2, v_conv_b_pw2, v_attn_w_qkv, v_attn_w_o, v_rel_bias, v_ffn_w_gate, v_ffn_w_up, v_ffn_w_down):
    w = dict(norm_mix=norm_mix, norm_ffn=norm_ffn, final_norm=final_norm, conv_w_pw1=conv_w_pw1,
             conv_b_pw1=conv_b_pw1, conv_w_dw=conv_w_dw, conv_b_dw=conv_b_dw, conv_ln_g=conv_ln_g,
             conv_ln_b=conv_ln_b, conv_w_pw2=conv_w_pw2, conv_b_pw2=conv_b_pw2, attn_w_qkv=attn_w_qkv,
             attn_w_o=attn_w_o, rel_bias=rel_bias, ffn_w_gate=ffn_w_gate, ffn_w_up=ffn_w_up, ffn_w_down=ffn_w_down)
    m = dict(norm_mix=m_norm_mix, norm_ffn=m_norm_ffn, final_norm=m_final_norm, conv_w_pw1=m_conv_w_pw1,
             conv_b_pw1=m_conv_b_pw1, conv_w_dw=m_conv_w_dw, conv_b_dw=m_conv_b_dw, conv_ln_g=m_conv_ln_g,
             conv_ln_b=m_conv_ln_b, conv_w_pw2=m_conv_w_pw2, conv_b_pw2=m_conv_b_pw2, attn_w_qkv=m_attn_w_qkv,
             attn_w_o=m_attn_w_o, rel_bias=m_rel_bias, ffn_w_gate=m_ffn_w_gate, ffn_w_up=m_ffn_w_up,
             ffn_w_down=m_ffn_w_down)
    v = dict(norm_mix=v_norm_mix, norm_ffn=v_norm_ffn, final_norm=v_final_norm, conv_w_pw1=v_conv_w_pw1,
             conv_b_pw1=v_conv_b_pw1, conv_w_dw=v_conv_w_dw, conv_b_dw=v_conv_b_dw, conv_ln_g=v_conv_ln_g,
             conv_ln_b=v_conv_ln_b, conv_w_pw2=v_conv_w_pw2, conv_b_pw2=v_conv_b_pw2, attn_w_qkv=v_attn_w_qkv,
             attn_w_o=v_attn_w_o, rel_bias=v_rel_bias, ffn_w_gate=v_ffn_w_gate, ffn_w_up=v_ffn_w_up,
             ffn_w_down=v_ffn_w_down)

    me = 4 * lax.axis_index("x") + 2 * lax.axis_index("y") + lax.axis_index("c")
    depth = ffn_w_gate.shape[0]
    n_conv, _, cb = conv_w_dw.shape

    def sublayer(key, layer):
        if key in ("wgt", "wut", "wd"):
            return 2 * layer + 1
        return 4 * layer if key in ("w1t", "w2") else 4 * layer + 2

    def landing(block, own):
        land = lax.empty((N_DEV,) + block.shape, block.dtype)
        return lax.dynamic_update_slice(land, own[None], (me,) + (0,) * block.ndim)

    by_sub = {s: [] for s in range(2 * depth)}
    for name, key, cols in SHARDED:
        sw = (jnp.swapaxes(w[name], 1, 2) if cols else w[name]).astype(BF16)
        for layer in range(sw.shape[0]):
            by_sub[sublayer(key, layer)].append((key, layer, sw[layer]))
    likes = {s: jnp.zeros((sum(sh.size for _, _, sh in by_sub[s]) // 1024, 1024), BF16) for s in by_sub}
    dw_shard = jnp.pad(conv_w_dw.reshape(-1, cb), ((0, -(n_conv * CONV_WIDTH) % 8), (0, 0)))
    like_dw = jnp.zeros(dw_shard.shape, F32)
    groups = [([dw_shard], [landing(dw_shard, dw_shard)])]
    for s in range(2 * depth):
        groups.append(([sh for _, _, sh in by_sub[s]], [landing(sh, sh) for _, _, sh in by_sub[s]]))
    gather, _ = _exchange_start("gather_start", groups, scatter=False, carry=jnp.zeros((8, 128), F32))
    dw_filters = []

    def fetch(s, after):
        pieces = [(gather[s + 1], likes[s])]
        if s == 0:
            pieces.append((gather[0], like_dw))
        landed = _exchange_wait(f"gather_wait{s}", pieces, after)
        out = {key: g.reshape(g.shape[0] * g.shape[1], g.shape[2]) for (key, _, _), g in zip(by_sub[s], landed[0])}
        if s == 0:
            dw_all = landed[1][0]
            full = jnp.transpose(dw_all[:, :n_conv * CONV_WIDTH].reshape(N_DEV, n_conv, CONV_WIDTH, cb), (1, 2, 0, 3))
            full = jnp.pad(full.reshape(n_conv, CONV_WIDTH, N_DEV * cb), ((0, 0), (0, CONV_HALO - CONV_WIDTH), (0, 0)))
            dw_filters.extend(full[layer] for layer in range(n_conv))
        if "w1t" in out:
            out["wdw"] = dw_filters[s // 4]
        return out

    scatter, dw_grads = {}, {}

    def emit(s, gd, carry):
        parts = [gd[key].reshape(N_DEV, -1, gd[key].shape[1]) for key, _, _ in by_sub[s]]
        if "wdw" in gd:
            dw_grads[s // 4] = gd["wdw"]
        if s == 0:
            parts.append(_dw_blocks(jnp.stack([dw_grads[layer] for layer in range(n_conv)])))
        lands = [landing(p[0], lax.dynamic_index_in_dim(p, me, 0, keepdims=False)) for p in parts]
        groups = [(parts[:len(by_sub[s])], lands[:len(by_sub[s])])]
        if s == 0:
            groups.append((parts[-1:], lands[-1:]))
        scatter[s], carry = _exchange_start(f"scatter_start{s}", groups, scatter=True, carry=carry)
        return carry

    sm = {n: w[n] for n in SMALL}
    loss_cols, dx, gsm = _local_step(x[0], loss_target[0], sm, depth, fetch, emit)
    loss = lax.psum(jnp.sum(loss_cols), ("x", "y", "c"))

    order = sorted(scatter, reverse=True)
    pieces = [(scatter[s][0], likes[s]) for s in order] + [(scatter[0][1], like_dw)]
    landed = _exchange_wait("scatter_wait", pieces, dx)
    pack = _pack_small(gsm)
    ((pack_all,),) = _exchange("gather_small_grads", [([pack], pack)], scatter=False)

    grads = {}
    summed = {}
    for s, recv in zip(order, landed):
        for (key, layer, _), r in zip(by_sub[s], recv):
            summed[key, layer] = _sum8(f"sum_{key}{layer}", r)
    for name, key, cols in SHARDED:
        g = jnp.stack([summed[key, layer] for layer in range(w[name].shape[0])])
        grads[name] = jnp.swapaxes(g, 1, 2) if cols else g
    grads["conv_w_dw"] = _sum8("sum_wdw", landed[-1][0])[:n_conv * CONV_WIDTH].reshape(conv_w_dw.shape)
    pack_sum = _sum8("sum_small", pack_all)
    grads.update(_unpack_small(pack_sum, sm))

    delta, new_m, new_v = {}, {}, {}
    for name in [n for n, _, _ in SHARDED] + ["conv_w_dw"]:
        shape = w[name].shape
        res = _adamw(f"adamw_{name}", *[t.reshape(-1, shape[-1]) for t in (grads[name], w[name], m[name], v[name])])
        delta[name], new_m[name], new_v[name] = (t.reshape(shape) for t in res)
    res = _adamw("adamw_small", pack_sum, _pack_small(sm), _pack_small({n: m[n] for n in SMALL}),
                 _pack_small({n: v[n] for n in SMALL}))
    for dst, t in zip((delta, new_m, new_v), res):
        dst.update(_unpack_small(t, sm))

    outs = [loss, dx[None]]
    for d in (grads, delta, new_m, new_v):
        outs += [d[n] for n in ORDER]
    return tuple(outs)
```

```python
import functools
import math

import numpy as np
import jax
import jax.numpy as jnp
from jax import lax
from jax.experimental import pallas as pl
from jax.experimental.pallas import tpu as pltpu

F32 = jnp.float32
BF16 = jnp.bfloat16

N_DEV = 8
HEAD_DIM = 64
HEADS_PER_GROUP = 4
GROUP_COLS = HEADS_PER_GROUP * HEAD_DIM
DILATIONS = (1, 4, 16)
N_BACK = 128
N_GROUPS = 3
N_HEADS = 12
D_ATTN = 768
N_BUCKETS = 32
REL_MAX_DISTANCE = 2048
CONV_WIDTH = 31
CONV_HALO = 32
EPS = 1e-6
NEG_INF = -1e30
ADAM_LR, ADAM_B1, ADAM_B2, ADAM_EPS, ADAM_WD, ADAM_STEP = 0.001, 0.9, 0.999, 1e-08, 0.01, 10
V7X_VMEM_LIMIT_BYTES = 56 * 1024 * 1024
MESH = pl.DeviceIdType.MESH
ANY = pl.BlockSpec(memory_space=pl.ANY)


def _pick(n, prefs):
    for p in prefs:
        if n % p == 0:
            return p
    return n


def _params(sem, vmem=None):
    return pltpu.CompilerParams(dimension_semantics=sem, vmem_limit_bytes=vmem)


def _sigmoid(x):
    return 1.0 / (1.0 + jnp.exp(-x))


def _exchange(name, groups, scatter):
    n_arr = [len(arrs) for arrs, _ in groups]
    n_in = sum(n_arr) + len(groups)
    ng = len(groups)

    def body(*refs):
        ins, outs, (send_sems, recv_sems, local_sems) = refs[:n_in], refs[n_in:-3], refs[-3:]
        x, y, c = lax.axis_index("x"), lax.axis_index("y"), lax.axis_index("c")
        me = 4 * x + 2 * y + c
        pos_in = pos_out = 0
        plans = []
        for gi in range(ng):
            srcs = ins[pos_in:pos_in + n_arr[gi]]
            like = ins[pos_in + n_arr[gi]]
            dsts = outs[pos_out:pos_out + n_arr[gi]]
            pos_in += n_arr[gi] + 1
            pos_out += n_arr[gi]
            plans.append((gi, srcs, like, dsts))
        local = []
        for gi, srcs, like, dsts in plans:
            for s, d in zip(srcs, dsts):
                cp = pltpu.make_async_copy(s.at[me] if scatter else s, d.at[me], local_sems.at[gi])
                cp.start()
                local.append(cp)
        for delta in range(1, N_DEV):
            dx, dy, dc = (delta >> 2) & 1, (delta >> 1) & 1, delta & 1
            px, py, pc = (1 - x if dx else x), (1 - y if dy else y), (1 - c if dc else c)
            peer = 4 * px + 2 * py + pc
            for gi, srcs, like, dsts in plans:
                for s, d in zip(srcs, dsts):
                    pltpu.make_async_remote_copy(
                        src_ref=s.at[peer] if scatter else s, dst_ref=d.at[me],
                        send_sem=send_sems.at[gi, delta - 1], recv_sem=recv_sems.at[gi, delta - 1],
                        device_id=(px, py, pc), device_id_type=MESH).start()
        for delta in range(1, N_DEV):
            for gi, srcs, like, dsts in plans:
                pltpu.make_async_remote_copy(
                    src_ref=like, dst_ref=like, send_sem=send_sems.at[gi, delta - 1],
                    recv_sem=recv_sems.at[gi, delta - 1], device_id=(x, y, c), device_id_type=MESH).wait()
        for cp in local:
            cp.wait()

    operands, out_shape = [], []
    for arrs, like in groups:
        operands += list(arrs) + [like]
        for a in arrs:
            blk = a.shape[1:] if scatter else a.shape
            out_shape.append(jax.ShapeDtypeStruct((N_DEV,) + tuple(blk), a.dtype))
    outs = pl.pallas_call(
        body, name=name, out_shape=out_shape, in_specs=[ANY] * len(operands), out_specs=[ANY] * len(out_shape),
        scratch_shapes=[pltpu.SemaphoreType.DMA((ng, N_DEV - 1)), pltpu.SemaphoreType.DMA((ng, N_DEV - 1)),
                        pltpu.SemaphoreType.DMA((ng,))],
        compiler_params=pltpu.CompilerParams(has_side_effects=True),
    )(*operands)
    res, pos = [], 0
    for n in n_arr:
        res.append(list(outs[pos:pos + n]))
        pos += n
    return res


HBM = pl.BlockSpec(memory_space=pltpu.HBM)
SEM = pl.BlockSpec(memory_space=pltpu.SEMAPHORE)
EFFECT = pltpu.SideEffectType.DATAFLOW_SIDE_EFFECTING


def _in_hbm(a):
    return pltpu.with_memory_space_constraint(a, pltpu.HBM)


def _exchange_start(name, groups, scatter, carry):
    ns = [len(s) for s, _ in groups]
    n_in = 2 * sum(ns)

    def body(*refs):
        ins, outs = refs[:n_in], refs[n_in + 1:]
        x, y, c = lax.axis_index("x"), lax.axis_index("y"), lax.axis_index("c")
        me = 4 * x + 2 * y + c
        pi = po = 0
        for n in ns:
            srcs, lands = ins[pi:pi + n], ins[pi + n:pi + 2 * n]
            send_sems, recv_sems = outs[po], outs[po + 1]
            pi += 2 * n
            po += 2 + 2 * n
            for delta in range(1, N_DEV):
                dx, dy, dc = (delta >> 2) & 1, (delta >> 1) & 1, delta & 1
                px, py, pc = (1 - x if dx else x), (1 - y if dy else y), (1 - c if dc else c)
                peer = 4 * px + 2 * py + pc
                for s, d in zip(srcs, lands):
                    pltpu.make_async_remote_copy(
                        src_ref=s.at[peer] if scatter else s, dst_ref=d.at[me], send_sem=send_sems.at[delta - 1],
                        recv_sem=recv_sems.at[delta - 1], device_id=(px, py, pc), device_id_type=MESH).start()

    operands, out_shape, out_specs, aliases = [], [], [], {}
    for srcs, lands in groups:
        out_shape += [pltpu.SemaphoreType.DMA((N_DEV - 1,))] * 2
        out_specs += [SEM, SEM]
        for a in list(srcs) + list(lands):
            aliases[len(operands)] = len(out_shape)
            operands.append(_in_hbm(a))
            out_shape.append(pltpu.HBM(a.shape, a.dtype))
            out_specs.append(HBM)
    aliases[len(operands)] = len(out_shape)
    operands.append(_in_hbm(carry))
    out_shape.append(pltpu.HBM(carry.shape, carry.dtype))
    out_specs.append(HBM)
    outs = pl.pallas_call(
        body, name=name, out_shape=out_shape, in_specs=[HBM] * len(operands), out_specs=out_specs,
        input_output_aliases=aliases, compiler_params=pltpu.CompilerParams(has_side_effects=EFFECT),
    )(*operands)
    handles, po = [], 0
    for n in ns:
        handles.append((outs[po], outs[po + 1], list(outs[po + 2:po + 2 + n]), list(outs[po + 2 + n:po + 2 + 2 * n])))
        po += 2 + 2 * n
    return handles, outs[-1]


def _exchange_wait(name, pieces, after):
    ns = [len(h[2]) for h, _ in pieces]

    def body(*refs):
        x, y, c = lax.axis_index("x"), lax.axis_index("y"), lax.axis_index("c")
        pi = 0
        for n in ns:
            send_sems, recv_sems, like = refs[pi + 2 * n], refs[pi + 2 * n + 1], refs[pi + 2 * n + 2]
            pi += 2 * n + 3
            for delta in range(1, N_DEV):
                cp = pltpu.make_async_remote_copy(
                    src_ref=like, dst_ref=like, send_sem=send_sems.at[delta - 1], recv_sem=recv_sems.at[delta - 1],
                    device_id=(x, y, c), device_id_type=MESH)
                cp.wait_send()
                cp.wait_recv()

    operands, in_specs, out_shape, aliases = [], [], [], {}
    for (send_sems, recv_sems, srcs, lands), like in pieces:
        for a in srcs + lands:
            aliases[len(operands)] = len(out_shape)
            operands.append(a)
            in_specs.append(HBM)
            out_shape.append(pltpu.HBM(a.shape, a.dtype))
        operands += [send_sems, recv_sems, like]
        in_specs += [SEM, SEM, ANY]
    operands.append(after)
    in_specs.append(ANY)
    outs = pl.pallas_call(
        body, name=name, out_shape=out_shape, in_specs=in_specs, out_specs=[HBM] * len(out_shape),
        input_output_aliases=aliases, compiler_params=pltpu.CompilerParams(has_side_effects=EFFECT),
    )(*operands)
    res, po = [], 0
    for n in ns:
        res.append(list(outs[po + n:po + 2 * n]))
        po += 2 * n
    return res


def _sum8(name, r):
    _, rows, cols = r.shape
    tr = _pick(rows, (256, 128, 64, 32, 16, 8))

    def body(r_ref, o_ref):
        acc = r_ref[0].astype(F32)
        for p in range(1, N_DEV):
            acc = acc + r_ref[p].astype(F32)
        o_ref[...] = acc

    return pl.pallas_call(
        body, name=name, out_shape=jax.ShapeDtypeStruct((rows, cols), F32), grid=(rows // tr,),
        in_specs=[pl.BlockSpec((N_DEV, tr, cols), lambda i: (0, i, 0))],
        out_specs=pl.BlockSpec((tr, cols), lambda i: (i, 0)), compiler_params=_params(("parallel",)),
    )(r)


def _adamw(name, g, w, m, v):
    rows, cols = g.shape
    tr = _pick(rows, (512, 256, 128, 64, 32, 16, 8))
    c1 = 1.0 - ADAM_B1 ** ADAM_STEP
    c2 = 1.0 - ADAM_B2 ** ADAM_STEP

    def body(g_ref, w_ref, m_ref, v_ref, d_ref, nm_ref, nv_ref):
        gv = g_ref[...]
        nm = ADAM_B1 * m_ref[...] + (1.0 - ADAM_B1) * gv
        nv = ADAM_B2 * v_ref[...] + (1.0 - ADAM_B2) * (gv * gv)
        d_ref[...] = -ADAM_LR * ((nm / c1) / (jnp.sqrt(nv / c2) + ADAM_EPS) + ADAM_WD * w_ref[...])
        nm_ref[...] = nm
        nv_ref[...] = nv

    spec = pl.BlockSpec((tr, cols), lambda i: (i, 0))
    return pl.pallas_call(
        body, name=name, out_shape=[jax.ShapeDtypeStruct((rows, cols), F32)] * 3, grid=(rows // tr,),
        in_specs=[spec] * 4, out_specs=[spec] * 3, compiler_params=_params(("parallel",)),
    )(g, w, m, v)


def _mm_nt(name, a, ws, w_offs, n, epi, out_dtypes, extras=(), rows=(), tm=None, tn=None, chunk=None):
    m, k = a.shape
    tm = tm or _pick(m, (512, 256, 128))
    tn = tn or _pick(n, (1408, 1152, 1024, 512, 256, 128))
    chunk = chunk or tn
    nw, ne, nr = len(ws), len(extras), len(rows)

    def body(*refs):
        a_ref, w_refs = refs[0], refs[1:1 + nw]
        e_refs, r_refs = refs[1 + nw:1 + nw + ne], refs[1 + nw + ne:1 + nw + ne + nr]
        o_refs = refs[1 + nw + ne + nr:]
        av = a_ref[...].astype(BF16)
        for c0 in range(0, tn, chunk):
            cs = slice(c0, c0 + chunk)
            accs = [lax.dot_general(av, w[cs, :], (((1,), (1,)), ((), ())), preferred_element_type=F32)
                    for w in w_refs]
            outs = epi(accs, [e[:, cs] for e in e_refs], [r[:, cs] for r in r_refs])
            for o_ref, o in zip(o_refs, outs):
                o_ref[:, cs] = o.astype(o_ref.dtype)

    in_specs = [pl.BlockSpec((tm, k), lambda j, i: (i, 0))]
    in_specs += [pl.BlockSpec((tn, k), functools.partial(lambda j, i, off: (j + off, 0), off=off)) for off in w_offs]
    in_specs += [pl.BlockSpec((tm, tn), lambda j, i: (i, j))] * ne
    in_specs += [pl.BlockSpec((1, tn), lambda j, i: (0, j))] * nr
    return pl.pallas_call(
        body, name=name, out_shape=[jax.ShapeDtypeStruct((m, n), dt) for dt in out_dtypes],
        grid=(n // tn, m // tm), in_specs=in_specs,
        out_specs=[pl.BlockSpec((tm, tn), lambda j, i: (i, j))] * len(out_dtypes),
        compiler_params=_params(("parallel", "parallel"), V7X_VMEM_LIMIT_BYTES),
    )(a, *ws, *extras, *rows)


def _mm_nn(name, as_, bs, epi, out_dtype, extras=(), rows=(), tm=None):
    m, k = as_[0].shape
    n = bs[0].shape[1]
    tm = tm or _pick(m, (512, 256, 128))
    npair, ne, nr = len(as_), len(extras), len(rows)

    def body(*refs):
        a_refs, b_refs = refs[:npair], refs[npair:2 * npair]
        e_refs, r_refs = refs[2 * npair:2 * npair + ne], refs[2 * npair + ne:2 * npair + ne + nr]
        o_ref = refs[-1]
        acc = None
        for a_ref, b_ref in zip(a_refs, b_refs):
            p = jnp.dot(a_ref[...].astype(BF16), b_ref[...], preferred_element_type=F32)
            acc = p if acc is None else acc + p
        o_ref[...] = epi(acc, [e[...] for e in e_refs], [r[...] for r in r_refs]).astype(o_ref.dtype)

    in_specs = [pl.BlockSpec((tm, k), lambda i: (i, 0))] * npair
    in_specs += [pl.BlockSpec((k, n), lambda i: (0, 0))] * npair
    in_specs += [pl.BlockSpec((tm, n), lambda i: (i, 0))] * ne
    in_specs += [pl.BlockSpec((1, n), lambda i: (0, 0))] * nr
    return pl.pallas_call(
        body, name=name, out_shape=jax.ShapeDtypeStruct((m, n), out_dtype), grid=(m // tm,), in_specs=in_specs,
        out_specs=pl.BlockSpec((tm, n), lambda i: (i, 0)),
        compiler_params=_params(("parallel",), V7X_VMEM_LIMIT_BYTES),
    )(*as_, *bs, *extras, *rows)


def _mm_nn_rms_bwd(name, as_, bs, x, g, dx_out, tm=None):
    m, k = as_[0].shape
    n = bs[0].shape[1]
    tm = tm or _pick(m, (512, 256, 128))
    npair = len(as_)

    def body(*refs):
        a_refs, b_refs = refs[:npair], refs[npair:2 * npair]
        x_ref, g_ref, dxo_ref, dx_ref, dg_ref = refs[2 * npair:]
        dh = None
        for a_ref, b_ref in zip(a_refs, b_refs):
            p = jnp.dot(a_ref[...].astype(BF16), b_ref[...], preferred_element_type=F32)
            dh = p if dh is None else dh + p
        xv = x_ref[...]
        r = lax.rsqrt(jnp.mean(xv * xv, axis=-1, keepdims=True) + EPS)
        yv = xv * r
        dy = dh * g_ref[...]
        dx_ref[...] = dxo_ref[...] + r * (dy - yv * jnp.mean(dy * yv, axis=-1, keepdims=True))

        @pl.when(pl.program_id(0) == 0)
        def _():
            dg_ref[...] = jnp.zeros_like(dg_ref)

        dg_ref[...] += jnp.sum(dh * yv, axis=0, keepdims=True)

    big = pl.BlockSpec((tm, n), lambda i: (i, 0))
    row = pl.BlockSpec((1, n), lambda i: (0, 0))
    in_specs = [pl.BlockSpec((tm, k), lambda i: (i, 0))] * npair + [pl.BlockSpec((k, n), lambda i: (0, 0))] * npair
    return pl.pallas_call(
        body, name=name, out_shape=[jax.ShapeDtypeStruct((m, n), F32), jax.ShapeDtypeStruct((1, n), F32)],
        grid=(m // tm,), in_specs=in_specs + [big, row, big], out_specs=[big, row],
        compiler_params=_params(("arbitrary",), V7X_VMEM_LIMIT_BYTES),
    )(*as_, *bs, x, g, dx_out)


def _mm_tn(name, a, b, colsum_b=False, tm=None, tk=1024):
    t, ma = a.shape
    nb = b.shape[1]
    tm = tm or _pick(ma, (1408, 1152, 1024, 768, 512, 256, 128))
    tk = _pick(t, (tk, 256, 128))
    nk = t // tk

    def body(*refs):
        a_ref, b_ref, o_ref = refs[0], refs[1], refs[2]
        acc_ref = refs[-1]
        kk = pl.program_id(1)
        bv = b_ref[...]

        @pl.when(kk == 0)
        def _():
            acc_ref[...] = jnp.zeros_like(acc_ref)

        acc_ref[...] += lax.dot_general(a_ref[...].astype(BF16), bv.astype(BF16), (((0,), (0,)), ((), ())),
                                        preferred_element_type=F32)
        if colsum_b:
            s_ref = refs[3]

            @pl.when((kk == 0) & (pl.program_id(0) == 0))
            def _():
                s_ref[...] = jnp.zeros_like(s_ref)

            @pl.when(pl.program_id(0) == 0)
            def _():
                s_ref[...] += jnp.sum(bv.astype(F32), axis=0, keepdims=True)

        @pl.when(kk == nk - 1)
        def _():
            o_ref[...] = acc_ref[...].astype(o_ref.dtype)

    out_shape = [jax.ShapeDtypeStruct((ma, nb), BF16)]
    out_specs = [pl.BlockSpec((tm, nb), lambda i, kk: (i, 0))]
    if colsum_b:
        out_shape.append(jax.ShapeDtypeStruct((1, nb), F32))
        out_specs.append(pl.BlockSpec((1, nb), lambda i, kk: (0, 0)))
    res = pl.pallas_call(
        body, name=name, out_shape=out_shape, grid=(ma // tm, nk),
        in_specs=[pl.BlockSpec((tk, tm), lambda i, kk: (kk, i)), pl.BlockSpec((tk, nb), lambda i, kk: (kk, 0))],
        out_specs=out_specs, scratch_shapes=[pltpu.VMEM((tm, nb), F32)],
        compiler_params=_params(("arbitrary", "arbitrary"), V7X_VMEM_LIMIT_BYTES),
    )(a, b)
    return res if colsum_b else res[0]


def _rmsnorm_fwd(name, x, g):
    t, d = x.shape
    tr = _pick(t, (512, 256, 128))

    def body(x_ref, g_ref, h_ref):
        xv = x_ref[...]
        r = lax.rsqrt(jnp.mean(xv * xv, axis=-1, keepdims=True) + EPS)
        h_ref[...] = (xv * r * g_ref[...]).astype(BF16)

    return pl.pallas_call(
        body, name=name, out_shape=jax.ShapeDtypeStruct((t, d), BF16), grid=(t // tr,),
        in_specs=[pl.BlockSpec((tr, d), lambda i: (i, 0)), pl.BlockSpec((1, d), lambda i: (0, 0))],
        out_specs=pl.BlockSpec((tr, d), lambda i: (i, 0)), compiler_params=_params(("parallel",)),
    )(x, g)


def _rmsnorm_bwd(name, dh, x, g, dx_out):
    t, d = x.shape
    tr = _pick(t, (512, 256, 128))

    def body(dh_ref, x_ref, g_ref, dxo_ref, dx_ref, dg_ref):
        xv = x_ref[...]
        r = lax.rsqrt(jnp.mean(xv * xv, axis=-1, keepdims=True) + EPS)
        yv = xv * r
        dhv = dh_ref[...].astype(F32)
        dy = dhv * g_ref[...]
        dx_ref[...] = dxo_ref[...] + r * (dy - yv * jnp.mean(dy * yv, axis=-1, keepdims=True))

        @pl.when(pl.program_id(0) == 0)
        def _():
            dg_ref[...] = jnp.zeros_like(dg_ref)

        dg_ref[...] += jnp.sum(dhv * yv, axis=0, keepdims=True)

    big = pl.BlockSpec((tr, d), lambda i: (i, 0))
    row = pl.BlockSpec((1, d), lambda i: (0, 0))
    return pl.pallas_call(
        body, name=name, out_shape=[jax.ShapeDtypeStruct((t, d), F32), jax.ShapeDtypeStruct((1, d), F32)],
        grid=(t // tr,), in_specs=[big, big, row, big], out_specs=[big, row],
        compiler_params=_params(("arbitrary",)),
    )(dh, x, g, dx_out)


def _loss_head(name, x, g, target):
    t, d = x.shape
    tr = _pick(t, (512, 256, 128))

    def body(x_ref, g_ref, t_ref, dx_ref, dg_ref, l_ref):
        xv = x_ref[...]
        r = lax.rsqrt(jnp.mean(xv * xv, axis=-1, keepdims=True) + EPS)
        yv = xv * r
        diff = yv * g_ref[...] - t_ref[...]
        dout = diff * (1.0 / d)
        dy = dout * g_ref[...]
        dx_ref[...] = r * (dy - yv * jnp.mean(dy * yv, axis=-1, keepdims=True))

        @pl.when(pl.program_id(0) == 0)
        def _():
            dg_ref[...] = jnp.zeros_like(dg_ref)
            l_ref[...] = jnp.zeros_like(l_ref)

        dg_ref[...] += jnp.sum(dout * yv, axis=0, keepdims=True)
        l_ref[...] += (0.5 / d) * jnp.sum(diff * diff, axis=0, keepdims=True)

    big = pl.BlockSpec((tr, d), lambda i: (i, 0))
    row = pl.BlockSpec((1, d), lambda i: (0, 0))
    return pl.pallas_call(
        body, name=name,
        out_shape=[jax.ShapeDtypeStruct((t, d), F32), jax.ShapeDtypeStruct((1, d), F32),
                   jax.ShapeDtypeStruct((1, d), F32)],
        grid=(t // tr,), in_specs=[big, row, big], out_specs=[big, row, row],
        compiler_params=_params(("arbitrary",)),
    )(x, g, target)


CONV_ROWS = 64
SUBLANES = 8


def _fill_window(win_ref, sh_ref, parts):
    rows = sh_ref.shape[2]
    for cb in range(win_ref.shape[0]):
        for r0, val in parts:
            win_ref[cb, r0:r0 + val.shape[0], :] = val[:, 128 * cb:128 * (cb + 1)]
        win_ref[cb, rows:rows + SUBLANES, :] = jnp.zeros((SUBLANES, 128), F32)
        for b in range(1, SUBLANES):
            sh_ref[b - 1, cb] = win_ref[cb, b:b + rows, :]


def _window_rows(win_ref, sh_ref, o, cb):
    b = o % SUBLANES
    if b == 0:
        return win_ref[cb, o:o + CONV_ROWS, :]
    return sh_ref[b - 1, cb, o - b:o - b + CONV_ROWS, :]


def _window_scratch(rows, c):
    return [pltpu.VMEM((c // 128, rows + SUBLANES, 128), F32), pltpu.VMEM((SUBLANES - 1, c // 128, rows, 128), F32)]


def _dwconv_fwd(name, glu, w_dw, b_dw, ln_g, ln_b):
    t, c = glu.shape
    tt = _pick(t, (256, 128))
    hb = tt // CONV_HALO

    def body(cur_ref, halo_ref, w_ref, b_ref, g_ref, be_ref, dw_ref, s_ref, win_ref, sh_ref):
        i = pl.program_id(0)
        halo = jnp.where(i > 0, halo_ref[...].astype(F32), 0.0)
        _fill_window(win_ref, sh_ref, [(0, halo), (CONV_HALO, cur_ref[...].astype(F32))])
        for r0 in range(0, tt, CONV_ROWS):
            for cb in range(c // 128):
                c0 = 128 * cb
                acc = jnp.zeros((CONV_ROWS, 128), F32) + b_ref[:, c0:c0 + 128]
                for k in range(CONV_WIDTH):
                    o = r0 + k + CONV_HALO - (CONV_WIDTH - 1)
                    acc = acc + w_ref[k:k + 1, c0:c0 + 128] * _window_rows(win_ref, sh_ref, o, cb)
                dw_ref[r0:r0 + CONV_ROWS, c0:c0 + 128] = acc
        u = dw_ref[...]
        mu = jnp.mean(u, axis=-1, keepdims=True)
        uc = u - mu
        rstd = lax.rsqrt(jnp.mean(uc * uc, axis=-1, keepdims=True) + EPS)
        z = uc * rstd * g_ref[...] + be_ref[...]
        s_ref[...] = (z * _sigmoid(z)).astype(BF16)

    big = pl.BlockSpec((tt, c), lambda i: (i, 0))
    row = pl.BlockSpec((1, c), lambda i: (0, 0))
    return pl.pallas_call(
        body, name=name, out_shape=[jax.ShapeDtypeStruct((t, c), F32), jax.ShapeDtypeStruct((t, c), BF16)],
        grid=(t // tt,),
        in_specs=[big, pl.BlockSpec((CONV_HALO, c), lambda i: (jnp.maximum(i * hb - 1, 0), 0)),
                  pl.BlockSpec((CONV_HALO, c), lambda i: (0, 0)), row, row, row],
        out_specs=[big, big], scratch_shapes=_window_scratch(tt + CONV_HALO, c),
        compiler_params=_params(("parallel",), V7X_VMEM_LIMIT_BYTES),
    )(glu, glu, w_dw, b_dw, ln_g, ln_b)


def _ln_silu_bwd(name, ds, dw, ln_g, ln_b):
    t, c = dw.shape
    tr = _pick(t, (256, 128))

    def body(ds_ref, dw_ref, g_ref, be_ref, o_ref, acc_ref):
        u = dw_ref[...]
        mu = jnp.mean(u, axis=-1, keepdims=True)
        uc = u - mu
        rstd = lax.rsqrt(jnp.mean(uc * uc, axis=-1, keepdims=True) + EPS)
        xh = uc * rstd
        z = xh * g_ref[...] + be_ref[...]
        sg = _sigmoid(z)
        dz = ds_ref[...].astype(F32) * (sg * (1.0 + z * (1.0 - sg)))
        dxh = dz * g_ref[...]
        du = rstd * (dxh - jnp.mean(dxh, axis=-1, keepdims=True) - xh * jnp.mean(dxh * xh, axis=-1, keepdims=True))
        o_ref[...] = du

        @pl.when(pl.program_id(0) == 0)
        def _():
            acc_ref[...] = jnp.zeros_like(acc_ref)

        acc_ref[0:1, :] += jnp.sum(dz * xh, axis=0, keepdims=True)
        acc_ref[1:2, :] += jnp.sum(dz, axis=0, keepdims=True)
        acc_ref[2:3, :] += jnp.sum(du, axis=0, keepdims=True)

    big = pl.BlockSpec((tr, c), lambda i: (i, 0))
    row = pl.BlockSpec((1, c), lambda i: (0, 0))
    return pl.pallas_call(
        body, name=name, out_shape=[jax.ShapeDtypeStruct((t, c), F32), jax.ShapeDtypeStruct((8, c), F32)],
        grid=(t // tr,), in_specs=[big, big, row, row],
        out_specs=[big, pl.BlockSpec((8, c), lambda i: (0, 0))], compiler_params=_params(("arbitrary",)),
    )(ds, dw, ln_g, ln_b)


def _dwconv_bwd(name, ddw, a, gt, w_dw):
    t, c = ddw.shape
    tt = _pick(t, (256, 128))
    hb = tt // CONV_HALO
    last = t // tt - 1
    back = CONV_WIDTH - 1

    def body(d_ref, dn_ref, a_ref, ap_ref, g_ref, gp_ref, w_ref, du_ref, dwk_ref, db_ref,
             wd_ref, shd_ref, wg_ref, shg_ref, dg_ref, dwk8_ref):
        i = pl.program_id(0)
        _fill_window(wd_ref, shd_ref, [(0, d_ref[...]), (tt, jnp.where(i < last, dn_ref[...], 0.0))])
        glu_prev = ap_ref[...].astype(F32) * _sigmoid(gp_ref[...].astype(F32))
        av = a_ref[...].astype(F32)
        sg = _sigmoid(g_ref[...].astype(F32))
        _fill_window(wg_ref, shg_ref, [(0, jnp.where(i > 0, glu_prev, 0.0)), (CONV_HALO, av * sg)])

        @pl.when(i == 0)
        def _():
            dwk8_ref[...] = jnp.zeros_like(dwk8_ref)
            db_ref[...] = jnp.zeros_like(db_ref)

        for r0 in range(0, tt, CONV_ROWS):
            for cb in range(c // 128):
                c0 = 128 * cb
                dcur = wd_ref[cb, r0:r0 + CONV_ROWS, :]
                acc = jnp.zeros((CONV_ROWS, 128), F32)
                for k in range(CONV_WIDTH):
                    acc = acc + w_ref[k:k + 1, c0:c0 + 128] * _window_rows(wd_ref, shd_ref, r0 + back - k, cb)
                    p = dcur * _window_rows(wg_ref, shg_ref, r0 + k + CONV_HALO - back, cb)
                    s8 = p[0:SUBLANES]
                    for q in range(SUBLANES, CONV_ROWS, SUBLANES):
                        s8 = s8 + p[q:q + SUBLANES]
                    dwk8_ref[SUBLANES * k:SUBLANES * (k + 1), c0:c0 + 128] += s8
                dg_ref[r0:r0 + CONV_ROWS, c0:c0 + 128] = acc

        @pl.when(i == last)
        def _():
            for k in range(CONV_WIDTH):
                dwk_ref[k:k + 1, :] = jnp.sum(dwk8_ref[SUBLANES * k:SUBLANES * (k + 1), :], axis=0, keepdims=True)
            dwk_ref[CONV_WIDTH:, :] = jnp.zeros((CONV_HALO - CONV_WIDTH, c), F32)
        dglu = dg_ref[...]
        da = dglu * sg
        dgate = dglu * av * sg * (1.0 - sg)
        du_ref[:, 0:c] = da.astype(BF16)
        du_ref[:, c:] = dgate.astype(BF16)
        db_ref[:, 0:c] += jnp.sum(da, axis=0, keepdims=True)
        db_ref[:, c:] += jnp.sum(dgate, axis=0, keepdims=True)

    big = pl.BlockSpec((tt, c), lambda i: (i, 0))
    prev = pl.BlockSpec((CONV_HALO, c), lambda i: (jnp.maximum(i * hb - 1, 0), 0))
    nxt = pl.BlockSpec((CONV_HALO, c), lambda i: (jnp.minimum((i + 1) * hb, t // CONV_HALO - 1), 0))
    return pl.pallas_call(
        body, name=name,
        out_shape=[jax.ShapeDtypeStruct((t, 2 * c), BF16), jax.ShapeDtypeStruct((CONV_HALO, c), F32),
                   jax.ShapeDtypeStruct((1, 2 * c), F32)],
        grid=(t // tt,),
        in_specs=[big, nxt, big, prev, big, prev, pl.BlockSpec((CONV_HALO, c), lambda i: (0, 0))],
        out_specs=[pl.BlockSpec((tt, 2 * c), lambda i: (i, 0)), pl.BlockSpec((CONV_HALO, c), lambda i: (0, 0)),
                   pl.BlockSpec((1, 2 * c), lambda i: (0, 0))],
        scratch_shapes=_window_scratch(tt + CONV_HALO, c) + _window_scratch(tt + CONV_HALO, c)
        + [pltpu.VMEM((tt, c), F32), pltpu.VMEM((SUBLANES * CONV_HALO, c), F32)],
        compiler_params=_params(("arbitrary",), V7X_VMEM_LIMIT_BYTES),
    )(ddw, ddw, a, a, gt, gt, w_dw)


def _bucket_tables():
    i = np.arange(N_BACK)[:, None]
    j = np.arange(2 * N_BACK)[None, :]
    dist = i + N_BACK - j
    valid = (dist >= 0) & (dist <= N_BACK)
    max_exact = N_BUCKETS // 2
    out = []
    for d in DILATIONS:
        n = np.maximum(dist * d, 0)
        nf = np.maximum(n, 1).astype(np.float32)
        large = max_exact + (np.log(nf / np.float32(max_exact)) / np.float32(math.log(REL_MAX_DISTANCE / max_exact))
                             * np.float32(N_BUCKETS - max_exact)).astype(np.int32)
        large = np.minimum(large, N_BUCKETS - 1)
        out.append(np.where(valid, np.where(n < max_exact, n, large), -1))
    return np.stack(out).astype(np.int32)


def _bias_build(name, rel_bias, buckets):
    def body(tbl_ref, bk_ref, o_ref):
        g = pl.program_id(0)
        bk = bk_ref[0]
        for h in range(HEADS_PER_GROUP):
            acc = jnp.zeros(bk.shape, F32)
            for b in range(N_BUCKETS):
                acc = jnp.where(bk == b, tbl_ref[b, g * HEADS_PER_GROUP + h], acc)
            o_ref[h] = jnp.where(bk < 0, NEG_INF, acc)

    return pl.pallas_call(
        body, name=name, out_shape=jax.ShapeDtypeStruct((N_HEADS, N_BACK, 2 * N_BACK), F32), grid=(N_GROUPS,),
        in_specs=[pl.BlockSpec(memory_space=pltpu.SMEM), pl.BlockSpec((1, N_BACK, 2 * N_BACK), lambda g: (g, 0, 0))],
        out_specs=pl.BlockSpec((HEADS_PER_GROUP, N_BACK, 2 * N_BACK), lambda g: (g, 0, 0)),
        compiler_params=_params(("arbitrary",)),
    )(rel_bias, buckets)


def _bias_grad(name, dbs, buckets):
    nd = len(dbs)

    def body(*refs):
        bk = refs[nd][0]
        o_ref = refs[nd + 1]
        lane = lax.broadcasted_iota(jnp.int32, (1, 128), 1)
        db = [sum(r[h] for r in refs[:nd]) for h in range(HEADS_PER_GROUP)]
        for b in range(N_BUCKETS):
            row = jnp.zeros((1, 128), F32)
            for h in range(HEADS_PER_GROUP):
                s = jnp.sum(jnp.where(bk == b, db[h], 0.0), axis=0, keepdims=True)
                s = jnp.sum(s, axis=1, keepdims=True)
                row = jnp.where(lane // 32 == h, s, row)
            o_ref[0, b:b + 1, :] = row

    spec = pl.BlockSpec((HEADS_PER_GROUP, N_BACK, 2 * N_BACK), lambda g: (g, 0, 0))
    return pl.pallas_call(
        body, name=name, out_shape=jax.ShapeDtypeStruct((N_GROUPS, N_BUCKETS, 128), F32), grid=(N_GROUPS,),
        in_specs=[spec] * nd + [pl.BlockSpec((1, N_BACK, 2 * N_BACK), lambda g: (g, 0, 0))],
        out_specs=pl.BlockSpec((1, N_BUCKETS, 128), lambda g: (g, 0, 0)), compiler_params=_params(("arbitrary",)),
    )(*dbs, buckets)


def _head_cols(h):
    return slice(h * HEAD_DIM, (h + 1) * HEAD_DIM)


def _attn_fwd(name, qkv, bias, g):
    t = qkv.shape[0]
    d = DILATIONS[g]
    tq = t // d
    nblk = qkv.shape[1] // GROUP_COLS
    scale = HEAD_DIM ** -0.5

    def body(q_ref, kp_ref, kc_ref, vp_ref, vc_ref, b_ref, o_ref, l_ref):
        n = pl.program_id(1)
        col = lax.broadcasted_iota(jnp.int32, (N_BACK, 2 * N_BACK), 1)
        keep = (col >= N_BACK) | (n > 0)
        lane = lax.broadcasted_iota(jnp.int32, (N_BACK, 128), 1)
        lse_tile = jnp.zeros((N_BACK, 128), F32)
        outs = []
        for h in range(HEADS_PER_GROUP):
            hc = _head_cols(h)
            kk = jnp.concatenate([kp_ref[:, hc], kc_ref[:, hc]], axis=0)
            vv = jnp.concatenate([vp_ref[:, hc], vc_ref[:, hc]], axis=0)
            s = lax.dot_general(q_ref[:, hc], kk, (((1,), (1,)), ((), ())), preferred_element_type=F32)
            s = jnp.where(keep, s * scale + b_ref[h], NEG_INF)
            m = jnp.max(s, axis=-1, keepdims=True)
            p = jnp.exp(s - m)
            den = jnp.sum(p, axis=-1, keepdims=True)
            outs.append(jnp.dot(p.astype(BF16), vv, preferred_element_type=F32) / den)
            lse_tile = jnp.where(lane // 32 == h, m + jnp.log(den), lse_tile)
        o_ref[...] = jnp.concatenate(outs, axis=1)
        l_ref[...] = lse_tile

    def blk(part, prev):
        if prev:
            return pl.BlockSpec((N_BACK, GROUP_COLS), lambda r, n: (jnp.maximum(n - 1, 0), r * nblk + part))
        return pl.BlockSpec((N_BACK, GROUP_COLS), lambda r, n: (n, r * nblk + part))

    qv = qkv.reshape(tq, d * qkv.shape[1])
    o, l = pl.pallas_call(
        body, name=name,
        out_shape=[jax.ShapeDtypeStruct((tq, d * GROUP_COLS), F32), jax.ShapeDtypeStruct((tq, d * 128), F32)],
        grid=(d, tq // N_BACK),
        in_specs=[blk(0, False), blk(1, True), blk(1, False), blk(2, True), blk(2, False),
                  pl.BlockSpec((HEADS_PER_GROUP, N_BACK, 2 * N_BACK), lambda r, n: (g, 0, 0))],
        out_specs=[pl.BlockSpec((N_BACK, GROUP_COLS), lambda r, n: (n, r)),
                   pl.BlockSpec((N_BACK, 128), lambda r, n: (n, r))],
        compiler_params=_params(("parallel", "parallel")),
    )(qv, qv, qv, qv, qv, bias)
    return o.reshape(t, GROUP_COLS), l.reshape(t, 128)


def _group_weights(l_refs, h):
    ls = [l_ref[:, 32 * h:32 * h + 1] for l_ref in l_refs]
    m = jnp.maximum(jnp.maximum(ls[0], ls[1]), ls[2])
    es = [jnp.exp(l - m) for l in ls]
    tot = es[0] + es[1] + es[2]
    return [e / tot for e in es]


def _attn_merge(name, os_, ls):
    t = os_[0].shape[0]
    tr = _pick(t, (512, 256, 128))

    def body(o0, o1, o2, l0, l1, l2, out_ref):
        o_refs = (o0, o1, o2)
        pieces = [[None] * HEADS_PER_GROUP for _ in range(N_GROUPS)]
        for h in range(HEADS_PER_GROUP):
            al = _group_weights((l0, l1, l2), h)
            for g in range(N_GROUPS):
                pieces[g][h] = o_refs[g][:, _head_cols(h)] * al[g]
        out_ref[...] = jnp.concatenate([p for row in pieces for p in row], axis=1).astype(BF16)

    so = pl.BlockSpec((tr, GROUP_COLS), lambda i: (i, 0))
    sl = pl.BlockSpec((tr, 128), lambda i: (i, 0))
    return pl.pallas_call(
        body, name=name, out_shape=jax.ShapeDtypeStruct((t, D_ATTN), BF16), grid=(t // tr,),
        in_specs=[so] * 3 + [sl] * 3, out_specs=pl.BlockSpec((tr, D_ATTN), lambda i: (i, 0)),
        compiler_params=_params(("parallel",)),
    )(*os_, *ls)


def _attn_bwd_prep(name, d_out, os_, ls):
    t = d_out.shape[0]
    tr = _pick(t, (512, 256, 128))

    def body(do_ref, o0, o1, o2, l0, l1, l2, d0, d1, d2, c0, c1, c2):
        o_refs, d_refs, c_refs = (o0, o1, o2), (d0, d1, d2), (c0, c1, c2)
        lane = lax.broadcasted_iota(jnp.int32, (tr, 128), 1)
        dos = [[None] * HEADS_PER_GROUP for _ in range(N_GROUPS)]
        cs = [jnp.zeros((tr, 128), F32) for _ in range(N_GROUPS)]
        for h in range(HEADS_PER_GROUP):
            al = _group_weights((l0, l1, l2), h)
            tot = jnp.zeros((tr, 1), F32)
            for g in range(N_GROUPS):
                dv = do_ref[:, g * GROUP_COLS + h * HEAD_DIM:g * GROUP_COLS + (h + 1) * HEAD_DIM].astype(F32)
                tot = tot + al[g] * jnp.sum(dv * o_refs[g][:, _head_cols(h)], axis=-1, keepdims=True)
                dos[g][h] = dv * al[g]
            for g in range(N_GROUPS):
                cs[g] = jnp.where(lane // 32 == h, -al[g] * tot, cs[g])
        for g in range(N_GROUPS):
            d_refs[g][...] = jnp.concatenate(dos[g], axis=1).astype(BF16)
            c_refs[g][...] = cs[g]

    so = pl.BlockSpec((tr, GROUP_COLS), lambda i: (i, 0))
    sl = pl.BlockSpec((tr, 128), lambda i: (i, 0))
    res = pl.pallas_call(
        body, name=name,
        out_shape=[jax.ShapeDtypeStruct((t, GROUP_COLS), BF16)] * 3 + [jax.ShapeDtypeStruct((t, 128), F32)] * 3,
        grid=(t // tr,), in_specs=[pl.BlockSpec((tr, D_ATTN), lambda i: (i, 0))] + [so] * 3 + [sl] * 3,
        out_specs=[so] * 3 + [sl] * 3, compiler_params=_params(("parallel",)),
    )(d_out, *os_, *ls)
    return res[:3], res[3:]


def _attn_bwd(name, qkv, do, lse, cterm, bias, g):
    t = qkv.shape[0]
    d = DILATIONS[g]
    tq = t // d
    nb = tq // N_BACK
    nblk = qkv.shape[1] // GROUP_COLS
    scale = HEAD_DIM ** -0.5
    nt = (((1,), (1,)), ((), ()))
    tn = (((0,), (0,)), ((), ()))

    def body(qn, qx, kp, kn, vp, vn, don, dox, ln, lx, cn, cx, b_ref, dqkv_ref, db_ref):
        n = pl.program_id(1)
        has_prev = n > 0
        has_next = n < nb - 1

        @pl.when((n == 0) & (pl.program_id(0) == 0))
        def _():
            db_ref[...] = jnp.zeros_like(db_ref)

        dqs, dks, dvs = [], [], []
        for h in range(HEADS_PER_GROUP):
            hc = _head_cols(h)
            st = slice(32 * h, 32 * h + 1)
            b_prev, b_same = b_ref[h, :, 0:N_BACK], b_ref[h, :, N_BACK:]

            def pair(q, k, v, dout, l, cc, bias_blk, on):
                s = lax.dot_general(q, k, nt, preferred_element_type=F32) * scale + bias_blk
                p = jnp.where(on, jnp.exp(s - l), 0.0)
                dp = lax.dot_general(dout, v, nt, preferred_element_type=F32)
                return p, p * (dp + cc)

            q0, q1, k0, k1, v0, v1 = qn[:, hc], qx[:, hc], kp[:, hc], kn[:, hc], vp[:, hc], vn[:, hc]
            d0, d1 = don[:, hc], dox[:, hc]
            p_a, ds_a = pair(q0, k1, v1, d0, ln[:, st], cn[:, st], b_same, True)
            p_b, ds_b = pair(q0, k0, v0, d0, ln[:, st], cn[:, st], b_prev, has_prev)
            p_c, ds_c = pair(q1, k1, v1, d1, lx[:, st], cx[:, st], b_prev, has_next)
            ds_a16, ds_b16, ds_c16 = ds_a.astype(BF16), ds_b.astype(BF16), ds_c.astype(BF16)
            dqs.append(scale * (jnp.dot(ds_a16, k1, preferred_element_type=F32)
                                + jnp.dot(ds_b16, k0, preferred_element_type=F32)))
            dks.append(scale * (lax.dot_general(ds_a16, q0, tn, preferred_element_type=F32)
                                + lax.dot_general(ds_c16, q1, tn, preferred_element_type=F32)))
            dvs.append(lax.dot_general(p_a.astype(BF16), d0, tn, preferred_element_type=F32)
                       + lax.dot_general(p_c.astype(BF16), d1, tn, preferred_element_type=F32))
            db_ref[h, :, 0:N_BACK] += ds_b
            db_ref[h, :, N_BACK:] += ds_a
        dqkv_ref[...] = jnp.concatenate(dqs + dks + dvs, axis=1).astype(BF16)

    def rows(which):
        if which == "prev":
            return lambda n: jnp.maximum(n - 1, 0)
        if which == "next":
            return lambda n: jnp.minimum(n + 1, nb - 1)
        return lambda n: n

    def qkv_blk(part, which):
        f = rows(which)
        return pl.BlockSpec((N_BACK, GROUP_COLS), lambda r, n: (f(n), r * nblk + part))

    def grp_blk(width, which):
        f = rows(which)
        return pl.BlockSpec((N_BACK, width), lambda r, n: (f(n), r))

    qv = qkv.reshape(tq, d * qkv.shape[1])
    dov = do.reshape(tq, d * GROUP_COLS)
    lv = lse.reshape(tq, d * 128)
    cv = cterm.reshape(tq, d * 128)
    dqkv_g, db = pl.pallas_call(
        body, name=name,
        out_shape=[jax.ShapeDtypeStruct((tq, d * 3 * GROUP_COLS), BF16),
                   jax.ShapeDtypeStruct((HEADS_PER_GROUP, N_BACK, 2 * N_BACK), F32)],
        grid=(d, nb),
        in_specs=[qkv_blk(0, "same"), qkv_blk(0, "next"), qkv_blk(1, "prev"), qkv_blk(1, "same"),
                  qkv_blk(2, "prev"), qkv_blk(2, "same"), grp_blk(GROUP_COLS, "same"), grp_blk(GROUP_COLS, "next"),
                  grp_blk(128, "same"), grp_blk(128, "next"), grp_blk(128, "same"), grp_blk(128, "next"),
                  pl.BlockSpec((HEADS_PER_GROUP, N_BACK, 2 * N_BACK), lambda r, n: (g, 0, 0))],
        out_specs=[pl.BlockSpec((N_BACK, 3 * GROUP_COLS), lambda r, n: (n, r)),
                   pl.BlockSpec((HEADS_PER_GROUP, N_BACK, 2 * N_BACK), lambda r, n: (0, 0, 0))],
        compiler_params=_params(("arbitrary", "arbitrary")),
    )(qv, qv, qv, qv, qv, qv, dov, dov, lv, lv, cv, cv, bias)
    return dqkv_g.reshape(t, 3 * GROUP_COLS), db


def _row(v):
    return v.reshape(1, -1)


def _plain(accs, extras, rows):
    return (accs[0],)


def _glu_epi(accs, extras, rows):
    a = (accs[0] + rows[0]).astype(BF16)
    gt = (accs[1] + rows[1]).astype(BF16)
    return a, gt, a.astype(F32) * _sigmoid(gt.astype(F32))


def _swiglu_epi(accs, extras, rows):
    gq, uq = accs[0].astype(BF16), accs[1].astype(BF16)
    gf = gq.astype(F32)
    return gq, uq, gf * _sigmoid(gf) * uq.astype(F32)


def _swiglu_bwd_epi(accs, extras, rows):
    gf, uf = extras[0].astype(F32), extras[1].astype(F32)
    sg = _sigmoid(gf)
    return accs[0] * uf * (sg * (1.0 + gf * (1.0 - sg))), accs[0] * gf * sg


def _residual(acc, extras, rows):
    out = acc + extras[0]
    return out + rows[0] if rows else out


def _identity(acc, extras, rows):
    return acc


def _group_rows(w):
    parts = [w[p * D_ATTN:(p + 1) * D_ATTN].reshape(N_GROUPS, GROUP_COLS, -1) for p in range(3)]
    return jnp.concatenate(parts, axis=1)


def _ungroup_rows(wg):
    return jnp.concatenate([wg[g][p * GROUP_COLS:(p + 1) * GROUP_COLS] for p in range(3) for g in range(N_GROUPS)],
                           axis=0)


def _local_step(x, target, sm, depth, fetch, emit):
    d_model = x.shape[1]
    buckets = jnp.asarray(_bucket_tables())
    bias = _bias_build("bias_build", sm["rel_bias"], buckets)
    saved = []
    for i in range(depth):
        j = i // 2
        rec = {"x_mix": x}
        wm = fetch(2 * i, x)
        h = _rmsnorm_fwd(f"rms_mix_fwd{i}", x, _row(sm["norm_mix"][i]))
        rec.update(h_mix=h, wm=wm)
        if i % 2 == 0:
            c = wm["w2"].shape[0]
            tn = _pick(c, (512, 256, 128))
            b1 = sm["conv_b_pw1"][j]
            a, gt, glu = _mm_nt(f"conv_pw1_fwd{j}", h, [wm["w1t"]] * 2, [0, c // tn], c, _glu_epi, (BF16,) * 3,
                                rows=[_row(b1[:c]), _row(b1[c:])], tn=tn)
            dw, s = _dwconv_fwd(f"dwconv_fwd{j}", glu, wm["wdw"], _row(sm["conv_b_dw"][j]),
                                _row(sm["conv_ln_g"][j]), _row(sm["conv_ln_b"][j]))
            x = _mm_nn(f"conv_pw2_fwd{j}", [s], [wm["w2"]], _residual, F32, extras=[x],
                       rows=[_row(sm["conv_b_pw2"][j])])
            rec.update(a=a, gt=gt, dw=dw, s=s)
        else:
            wq = _group_rows(wm["wqkvt"])
            qkv = _mm_nt(f"attn_qkv_fwd{j}", h, [wq[g] for g in range(N_GROUPS)], [0] * N_GROUPS, D_ATTN,
                         lambda accs, e, r: tuple(accs), (BF16,) * N_GROUPS, tn=D_ATTN)
            og = [_attn_fwd(f"attn_fwd{j}_{g}", qkv[g], bias, g) for g in range(N_GROUPS)]
            os_, ls = [o for o, _ in og], [l for _, l in og]
            om = _attn_merge(f"attn_merge{j}", os_, ls)
            x = _mm_nt(f"attn_out_fwd{j}", om, [wm["wot"]], [0], d_model, lambda accs, e, r: (accs[0] + e[0],),
                       (F32,), extras=[x])[0]
            rec.update(qkv=qkv, os=os_, ls=ls, om=om, wq=wq)
        rec["x_ffn"] = x
        wf = fetch(2 * i + 1, x)
        h2 = _rmsnorm_fwd(f"rms_ffn_fwd{i}", x, _row(sm["norm_ffn"][i]))
        f = wf["wd"].shape[0]
        gq, uq, act = _mm_nt(f"ffn_up_fwd{i}", h2, [wf["wgt"], wf["wut"]], [0, 0], f, _swiglu_epi, (BF16,) * 3,
                             tn=f, chunk=_pick(f, (256, 128)))
        x = _mm_nn(f"ffn_down_fwd{i}", [act], [wf["wd"]], _residual, F32, extras=[x])
        rec.update(h_ffn=h2, gq=gq, uq=uq, act=act, wf=wf)
        saved.append(rec)

    dx, g_final, loss_cols = _loss_head("loss_head", x, _row(sm["final_norm"]), target)

    g_mix, g_ffn = [None] * depth, [None] * depth
    nconv = (depth + 1) // 2
    g_b1, g_bdw, g_lng, g_lnb, g_b2 = ([None] * nconv for _ in range(5))
    dbias = []
    for i in reversed(range(depth)):
        j = i // 2
        rec = saved[i]
        wm, wf = rec["wm"], rec["wf"]
        f = wf["wd"].shape[0]
        dgate, dup = _mm_nt(f"ffn_down_bwd{i}", dx, [wf["wd"]], [0], f, _swiglu_bwd_epi, (BF16, BF16),
                            extras=[rec["gq"], rec["uq"]], tn=f, chunk=_pick(f, (256, 128)))
        gf = {"wd": _mm_tn(f"ffn_down_dw{i}", rec["act"], dx),
              "wgt": _mm_tn(f"ffn_gate_dw{i}", dgate, rec["h_ffn"]),
              "wut": _mm_tn(f"ffn_up_dw{i}", dup, rec["h_ffn"])}
        dx, g_ffn[i] = _mm_nn_rms_bwd(f"ffn_up_bwd{i}", [dgate, dup], [wf["wgt"], wf["wut"]], rec["x_ffn"],
                                      _row(sm["norm_ffn"][i]), dx, tm=256)
        dx = emit(2 * i + 1, gf, dx)
        if i % 2 == 0:
            c = wm["w2"].shape[0]
            gm = {}
            gm["w2"], g_b2[j] = _mm_tn(f"conv_pw2_dw{j}", rec["s"], dx, colsum_b=True)
            ds = _mm_nt(f"conv_pw2_bwd{j}", dx, [wm["w2"]], [0], c, _plain, (BF16,))[0]
            ddw, sums = _ln_silu_bwd(f"ln_silu_bwd{j}", ds, rec["dw"], _row(sm["conv_ln_g"][j]),
                                     _row(sm["conv_ln_b"][j]))
            g_lng[j], g_lnb[j], g_bdw[j] = sums[0], sums[1], sums[2]
            du, dwk, db1 = _dwconv_bwd(f"dwconv_bwd{j}", ddw, rec["a"], rec["gt"], wm["wdw"])
            gm["wdw"] = dwk[:CONV_WIDTH]
            g_b1[j] = db1[0]
            gm["w1t"] = _mm_tn(f"conv_pw1_dw{j}", du, rec["h_mix"])
            dh_terms = ([du], [wm["w1t"]])
        else:
            gm = {"wot": _mm_tn(f"attn_out_dw{j}", dx, rec["om"])}
            d_om = _mm_nn(f"attn_out_bwd{j}", [dx], [wm["wot"]], _identity, BF16)
            dos, cs = _attn_bwd_prep(f"attn_bwd_prep{j}", d_om, rec["os"], rec["ls"])
            back = [_attn_bwd(f"attn_bwd{j}_{g}", rec["qkv"][g], dos[g], rec["ls"][g], cs[g], bias, g)
                    for g in range(N_GROUPS)]
            dqkv = [b[0] for b in back]
            dbias.append(jnp.concatenate([b[1] for b in back], axis=0))
            dh_terms = (dqkv, [rec["wq"][g] for g in range(N_GROUPS)])
            gm["wqkvt"] = _ungroup_rows([_mm_tn(f"attn_qkv_dw{j}_{g}", dqkv[g], rec["h_mix"])
                                         for g in range(N_GROUPS)])
        dx, g_mix[i] = _mm_nn_rms_bwd(f"mix_in_bwd{i}", dh_terms[0], dh_terms[1], rec["x_mix"],
                                      _row(sm["norm_mix"][i]), dx)
        dx = emit(2 * i, gm, dx)

    gb = _bias_grad("bias_grad", dbias, buckets)
    g_rel = jnp.transpose(gb[:, :, ::32], (1, 0, 2)).reshape(N_BUCKETS, N_HEADS)
    gsm = {
        "norm_mix": jnp.concatenate(g_mix, axis=0), "norm_ffn": jnp.concatenate(g_ffn, axis=0),
        "final_norm": g_final[0], "conv_b_pw1": jnp.stack(g_b1), "conv_b_dw": jnp.stack(g_bdw),
        "conv_ln_g": jnp.stack(g_lng), "conv_ln_b": jnp.stack(g_lnb),
        "conv_b_pw2": jnp.concatenate(g_b2, axis=0), "rel_bias": g_rel,
    }
    return loss_cols, dx, gsm


SMALL = ("norm_mix", "norm_ffn", "final_norm", "conv_b_pw1", "conv_b_dw", "conv_ln_g", "conv_ln_b", "conv_b_pw2",
         "rel_bias")
SHARDED = (("conv_w_pw1", "w1t", True), ("conv_w_pw2", "w2", False), ("attn_w_qkv", "wqkvt", True),
           ("attn_w_o", "wot", True), ("ffn_w_gate", "wgt", True), ("ffn_w_up", "wut", True),
           ("ffn_w_down", "wd", False))
ORDER = ("norm_mix", "norm_ffn", "final_norm", "conv_w_pw1", "conv_b_pw1", "conv_w_dw", "conv_b_dw", "conv_ln_g",
         "conv_ln_b", "conv_w_pw2", "conv_b_pw2", "attn_w_qkv", "attn_w_o", "rel_bias", "ffn_w_gate", "ffn_w_up",
         "ffn_w_down")
PACK_LANES = 128
PACK_ROW_TILE = 8


def _pack_small(vals):
    flat = jnp.concatenate([vals[n].reshape(-1) for n in SMALL])
    per_tile = PACK_LANES * PACK_ROW_TILE
    return jnp.pad(flat, (0, -flat.shape[0] % per_tile)).reshape(-1, PACK_LANES)


def _unpack_small(pack, like):
    flat, out, pos = pack.reshape(-1), {}, 0
    for n in SMALL:
        out[n] = flat[pos:pos + like[n].size].reshape(like[n].shape)
        pos += like[n].size
    return out


def _dw_blocks(w):
    l, k, c = w.shape
    blk = jnp.transpose(w.reshape(l, k, N_DEV, c // N_DEV), (2, 0, 1, 3)).reshape(N_DEV, l * k, c // N_DEV)
    return jnp.pad(blk, ((0, 0), (0, -(l * k) % 8), (0, 0)))


def kernel(x, norm_mix, norm_ffn, final_norm, conv_w_pw1, conv_b_pw1, conv_w_dw, conv_b_dw, conv_ln_g, conv_ln_b, conv_w_pw2, conv_b_pw2, attn_w_qkv, attn_w_o, rel_bias, ffn_w_gate, ffn_w_up, ffn_w_down, loss_target, m_norm_mix, m_norm_ffn, m_final_norm, m_conv_w_pw1, m_conv_b_pw1, m_conv_w_dw, m_conv_b_dw, m_conv_ln_g, m_conv_ln_b, m_conv_w_pw2, m_conv_b_pw2, m_attn_w_qkv, m_attn_w_o, m_rel_bias, m_ffn_w_gate, m_ffn_w_up, m_ffn_w_down, v_norm_mix, v_norm_ffn, v_final_norm, v_conv_w_pw1, v_conv_b_pw1, v_conv_w_dw, v_conv_b_dw, v_conv_ln_g, v_conv_ln_b, v_conv_w_pw2, v_conv_b_pw2, v_attn_w_qkv, v_attn_w_o, v_rel_bias, v_ffn_w_gate, v_ffn_w_up, v_ffn_w_down):
    w = dict(norm_mix=norm_mix, norm_ffn=norm_ffn, final_norm=final_norm, conv_w_pw1=conv_w_pw1,
             conv_b_pw1=conv_b_pw1, conv_w_dw=conv_w_dw, conv_b_dw=conv_b_dw, conv_ln_g=conv_ln_g,
             conv_ln_b=conv_ln_b, conv_w_pw2=conv_w_pw2, conv_b_pw2=conv_b_pw2, attn_w_qkv=attn_w_qkv,
             attn_w_o=attn_w_o, rel_bias=rel_bias, ffn_w_gate=ffn_w_gate, ffn_w_up=ffn_w_up, ffn_w_down=ffn_w_down)
    m = dict(norm_mix=m_norm_mix, norm_ffn=m_norm_ffn, final_norm=m_final_norm, conv_w_pw1=m_conv_w_pw1,
             conv_b_pw1=m_conv_b_pw1, conv_w_dw=m_conv_w_dw, conv_b_dw=m_conv_b_dw, conv_ln_g=m_conv_ln_g,
             conv_ln_b=m_conv_ln_b, conv_w_pw2=m_conv_w_pw2, conv_b_pw2=m_conv_b_pw2, attn_w_qkv=m_attn_w_qkv,
             attn_w_o=m_attn_w_o, rel_bias=m_rel_bias, ffn_w_gate=m_ffn_w_gate, ffn_w_up=m_ffn_w_up,
             ffn_w_down=m_ffn_w_down)
    v = dict(norm_mix=v_norm_mix, norm_ffn=v_norm_ffn, final_norm=v_final_norm, conv_w_pw1=v_conv_w_pw1,
             conv_b_pw1=v_conv_b_pw1, conv_w_dw=v_conv_w_dw, conv_b_dw=v_conv_b_dw, conv_ln_g=v_conv_ln_g,
             conv_ln_b=v_conv_ln_b, conv_w_pw2=v_conv_w_pw2, conv_b_pw2=v_conv_b_pw2, attn_w_qkv=v_attn_w_qkv,
             attn_w_o=v_attn_w_o, rel_bias=v_rel_bias, ffn_w_gate=v_ffn_w_gate, ffn_w_up=v_ffn_w_up,
             ffn_w_down=v_ffn_w_down)

    me = 4 * lax.axis_index("x") + 2 * lax.axis_index("y") + lax.axis_index("c")
    depth = ffn_w_gate.shape[0]
    n_conv, _, cb = conv_w_dw.shape

    def sublayer(key, layer):
        if key in ("wgt", "wut", "wd"):
            return 2 * layer + 1
        return 4 * layer if key in ("w1t", "w2") else 4 * layer + 2

    def landing(block, own):
        land = lax.empty((N_DEV,) + block.shape, block.dtype)
        return lax.dynamic_update_slice(land, own[None], (me,) + (0,) * block.ndim)

    by_sub = {s: [] for s in range(2 * depth)}
    for name, key, cols in SHARDED:
        sw = (jnp.swapaxes(w[name], 1, 2) if cols else w[name]).astype(BF16)
        for layer in range(sw.shape[0]):
            by_sub[sublayer(key, layer)].append((key, layer, sw[layer]))
    likes = {s: jnp.zeros((sum(sh.size for _, _, sh in by_sub[s]) // 1024, 1024), BF16) for s in by_sub}
    dw_shard = jnp.pad(conv_w_dw.reshape(-1, cb), ((0, -(n_conv * CONV_WIDTH) % 8), (0, 0)))
    like_dw = jnp.zeros(dw_shard.shape, F32)
    groups = [([dw_shard], [landing(dw_shard, dw_shard)])]
    for s in range(2 * depth):
        groups.append(([sh for _, _, sh in by_sub[s]], [landing(sh, sh) for _, _, sh in by_sub[s]]))
    gather, _ = _exchange_start("gather_start", groups, scatter=False, carry=jnp.zeros((8, 128), F32))
    dw_filters = []

    def fetch(s, after):
        pieces = [(gather[s + 1], likes[s])]
        if s == 0:
            pieces.append((gather[0], like_dw))
        landed = _exchange_wait(f"gather_wait{s}", pieces, after)
        out = {key: g.reshape(g.shape[0] * g.shape[1], g.shape[2]) for (key, _, _), g in zip(by_sub[s], landed[0])}
        if s == 0:
            dw_all = landed[1][0]
            full = jnp.transpose(dw_all[:, :n_conv * CONV_WIDTH].reshape(N_DEV, n_conv, CONV_WIDTH, cb), (1, 2, 0, 3))
            full = jnp.pad(full.reshape(n_conv, CONV_WIDTH, N_DEV * cb), ((0, 0), (0, CONV_HALO - CONV_WIDTH), (0, 0)))
            dw_filters.extend(full[layer] for layer in range(n_conv))
        if "w1t" in out:
            out["wdw"] = dw_filters[s // 4]
        return out

    scatter, dw_grads = {}, {}

    def emit(s, gd, carry):
        parts = [gd[key].reshape(N_DEV, -1, gd[key].shape[1]) for key, _, _ in by_sub[s]]
        if "wdw" in gd:
            dw_grads[s // 4] = gd["wdw"]
        if s == 0:
            parts.append(_dw_blocks(jnp.stack([dw_grads[layer] for layer in range(n_conv)])))
        lands = [landing(p[0], lax.dynamic_index_in_dim(p, me, 0, keepdims=False)) for p in parts]
        groups = [(parts[:len(by_sub[s])], lands[:len(by_sub[s])])]
        if s == 0:
            groups.append((parts[-1:], lands[-1:]))
        scatter[s], carry = _exchange_start(f"scatter_start{s}", groups, scatter=True, carry=carry)
        return carry

    sm = {n: w[n] for n in SMALL}
    loss_cols, dx, gsm = _local_step(x[0], loss_target[0], sm, depth, fetch, emit)
    loss = lax.psum(jnp.sum(loss_cols), ("x", "y", "c"))

    order = sorted(scatter, reverse=True)
    pieces = [(scatter[s][0], likes[s]) for s in order] + [(scatter[0][1], like_dw)]
    landed = _exchange_wait("scatter_wait", pieces, dx)
    pack = _pack_small(gsm)
    ((pack_all,),) = _exchange("gather_small_grads", [([pack], pack)], scatter=False)

    grads = {}
    summed = {}
    for s, recv in zip(order, landed):
        for (key, layer, _), r in zip(by_sub[s], recv):
            summed[key, layer] = _sum8(f"sum_{key}{layer}", r)
    for name, key, cols in SHARDED:
        g = jnp.stack([summed[key, layer] for layer in range(w[name].shape[0])])
        grads[name] = jnp.swapaxes(g, 1, 2) if cols else g
    grads["conv_w_dw"] = _sum8("sum_wdw", landed[-1][0])[:n_conv * CONV_WIDTH].reshape(conv_w_dw.shape)
    pack_sum = _sum8("sum_small", pack_all)
    grads.update(_unpack_small(pack_sum, sm))

    delta, new_m, new_v = {}, {}, {}
    for name in [n for n, _, _ in SHARDED] + ["conv_w_dw"]:
        shape = w[name].shape
        res = _adamw(f"adamw_{name}", *[t.reshape(-1, shape[-1]) for t in (grads[name], w[name], m[name], v[name])])
        delta[name], new_m[name], new_v[name] = (t.reshape(shape) for t in res)
    res = _adamw("adamw_small", pack_sum, _pack_small(sm), _pack_small({n: m[n] for n in SMALL}),
                 _pack_small({n: v[n] for n in SMALL}))
    for dst, t in zip((delta, new_m, new_v), res):
        dst.update(_unpack_small(t, sm))

    outs = [loss, dx[None]]
    for d in (grads, delta, new_m, new_v):
        outs += [d[n] for n in ORDER]
    return tuple(outs)
```

```python
import functools
import math

import numpy as np
import jax
import jax.numpy as jnp
from jax import lax
from jax.experimental import pallas as pl
from jax.experimental.pallas import tpu as pltpu

F32 = jnp.float32
BF16 = jnp.bfloat16

N_DEV = 8
HEAD_DIM = 64
HEADS_PER_GROUP = 4
GROUP_COLS = HEADS_PER_GROUP * HEAD_DIM
DILATIONS = (1, 4, 16)
N_BACK = 128
N_GROUPS = 3
N_HEADS = 12
D_ATTN = 768
N_BUCKETS = 32
REL_MAX_DISTANCE = 2048
CONV_WIDTH = 31
CONV_HALO = 32
EPS = 1e-6
NEG_INF = -1e30
ADAM_LR, ADAM_B1, ADAM_B2, ADAM_EPS, ADAM_WD, ADAM_STEP = 0.001, 0.9, 0.999, 1e-08, 0.01, 10
V7X_VMEM_LIMIT_BYTES = 56 * 1024 * 1024
MESH = pl.DeviceIdType.MESH
ANY = pl.BlockSpec(memory_space=pl.ANY)


def _pick(n, prefs):
    for p in prefs:
        if n % p == 0:
            return p
    return n


def _params(sem, vmem=None):
    return pltpu.CompilerParams(dimension_semantics=sem, vmem_limit_bytes=vmem)


def _sigmoid(x):
    return 1.0 / (1.0 + jnp.exp(-x))


def _exchange(name, groups, scatter):
    n_arr = [len(arrs) for arrs, _ in groups]
    n_in = sum(n_arr) + len(groups)
    ng = len(groups)

    def body(*refs):
        ins, outs, (send_sems, recv_sems, local_sems) = refs[:n_in], refs[n_in:-3], refs[-3:]
        x, y, c = lax.axis_index("x"), lax.axis_index("y"), lax.axis_index("c")
        me = 4 * x + 2 * y + c
        pos_in = pos_out = 0
        plans = []
        for gi in range(ng):
            srcs = ins[pos_in:pos_in + n_arr[gi]]
            like = ins[pos_in + n_arr[gi]]
            dsts = outs[pos_out:pos_out + n_arr[gi]]
            pos_in += n_arr[gi] + 1
            pos_out += n_arr[gi]
            plans.append((gi, srcs, like, dsts))
        local = []
        for gi, srcs, like, dsts in plans:
            for s, d in zip(srcs, dsts):
                cp = pltpu.make_async_copy(s.at[me] if scatter else s, d.at[me], local_sems.at[gi])
                cp.start()
                local.append(cp)
        for delta in range(1, N_DEV):
            dx, dy, dc = (delta >> 2) & 1, (delta >> 1) & 1, delta & 1
            px, py, pc = (1 - x if dx else x), (1 - y if dy else y), (1 - c if dc else c)
            peer = 4 * px + 2 * py + pc
            for gi, srcs, like, dsts in plans:
                for s, d in zip(srcs, dsts):
                    pltpu.make_async_remote_copy(
                        src_ref=s.at[peer] if scatter else s, dst_ref=d.at[me],
                        send_sem=send_sems.at[gi, delta - 1], recv_sem=recv_sems.at[gi, delta - 1],
                        device_id=(px, py, pc), device_id_type=MESH).start()
        for delta in range(1, N_DEV):
            for gi, srcs, like, dsts in plans:
                pltpu.make_async_remote_copy(
                    src_ref=like, dst_ref=like, send_sem=send_sems.at[gi, delta - 1],
                    recv_sem=recv_sems.at[gi, delta - 1], device_id=(x, y, c), device_id_type=MESH).wait()
        for cp in local:
            cp.wait()

    operands, out_shape = [], []
    for arrs, like in groups:
        operands += list(arrs) + [like]
        for a in arrs:
            blk = a.shape[1:] if scatter else a.shape
            out_shape.append(jax.ShapeDtypeStruct((N_DEV,) + tuple(blk), a.dtype))
    outs = pl.pallas_call(
        body, name=name, out_shape=out_shape, in_specs=[ANY] * len(operands), out_specs=[ANY] * len(out_shape),
        scratch_shapes=[pltpu.SemaphoreType.DMA((ng, N_DEV - 1)), pltpu.SemaphoreType.DMA((ng, N_DEV - 1)),
                        pltpu.SemaphoreType.DMA((ng,))],
        compiler_params=pltpu.CompilerParams(has_side_effects=True),
    )(*operands)
    res, pos = [], 0
    for n in n_arr:
        res.append(list(outs[pos:pos + n]))
        pos += n
    return res


HBM = pl.BlockSpec(memory_space=pltpu.HBM)
SEM = pl.BlockSpec(memory_space=pltpu.SEMAPHORE)
EFFECT = pltpu.SideEffectType.DATAFLOW_SIDE_EFFECTING


def _in_hbm(a):
    return pltpu.with_memory_space_constraint(a, pltpu.HBM)


def _exchange_start(name, groups, scatter, carry):
    ns = [len(s) for s, _ in groups]
    n_in = 2 * sum(ns)

    def body(*refs):
        ins, outs = refs[:n_in], refs[n_in + 1:]
        x, y, c = lax.axis_index("x"), lax.axis_index("y"), lax.axis_index("c")
        me = 4 * x + 2 * y + c
        pi = po = 0
        for n in ns:
            srcs, lands = ins[pi:pi + n], ins[pi + n:pi + 2 * n]
            send_sems, recv_sems = outs[po], outs[po + 1]
            pi += 2 * n
            po += 2 + 2 * n
            for delta in range(1, N_DEV):
                dx, dy, dc = (delta >> 2) & 1, (delta >> 1) & 1, delta & 1
                px, py, pc = (1 - x if dx else x), (1 - y if dy else y), (1 - c if dc else c)
                peer = 4 * px + 2 * py + pc
                for s, d in zip(srcs, lands):
                    pltpu.make_async_remote_copy(
                        src_ref=s.at[peer] if scatter else s, dst_ref=d.at[me], send_sem=send_sems.at[delta - 1],
                        recv_sem=recv_sems.at[delta - 1], device_id=(px, py, pc), device_id_type=MESH).start()

    operands, out_shape, out_specs, aliases = [], [], [], {}
    for srcs, lands in groups:
        out_shape += [pltpu.SemaphoreType.DMA((N_DEV - 1,))] * 2
        out_specs += [SEM, SEM]
        for a in list(srcs) + list(lands):
            aliases[len(operands)] = len(out_shape)
            operands.append(_in_hbm(a))
            out_shape.append(pltpu.HBM(a.shape, a.dtype))
            out_specs.append(HBM)
    aliases[len(operands)] = len(out_shape)
    operands.append(_in_hbm(carry))
    out_shape.append(pltpu.HBM(carry.shape, carry.dtype))
    out_specs.append(HBM)
    outs = pl.pallas_call(
        body, name=name, out_shape=out_shape, in_specs=[HBM] * len(operands), out_specs=out_specs,
        input_output_aliases=aliases, compiler_params=pltpu.CompilerParams(has_side_effects=EFFECT),
    )(*operands)
    handles, po = [], 0
    for n in ns:
        handles.append((outs[po], outs[po + 1], list(outs[po + 2:po + 2 + n]), list(outs[po + 2 + n:po + 2 + 2 * n])))
        po += 2 + 2 * n
    return handles, outs[-1]


def _exchange_wait(name, pieces, after):
    ns = [len(h[2]) for h, _ in pieces]

    def body(*refs):
        x, y, c = lax.axis_index("x"), lax.axis_index("y"), lax.axis_index("c")
        pi = 0
        for n in ns:
            send_sems, recv_sems, like = refs[pi + 2 * n], refs[pi + 2 * n + 1], refs[pi + 2 * n + 2]
            pi += 2 * n + 3
            for delta in range(1, N_DEV):
                cp = pltpu.make_async_remote_copy(
                    src_ref=like, dst_ref=like, send_sem=send_sems.at[delta - 1], recv_sem=recv_sems.at[delta - 1],
                    device_id=(x, y, c), device_id_type=MESH)
                cp.wait_send()
                cp.wait_recv()

    operands, in_specs, out_shape, aliases = [], [], [], {}
    for (send_sems, recv_sems, srcs, lands), like in pieces:
        for a in srcs + lands:
            aliases[len(operands)] = len(out_shape)
            operands.append(a)
            in_specs.append(HBM)
            out_shape.append(pltpu.HBM(a.shape, a.dtype))
        operands += [send_sems, recv_sems, like]
        in_specs += [SEM, SEM, ANY]
    operands.append(after)
    in_specs.append(ANY)
    outs = pl.pallas_call(
        body, name=name, out_shape=out_shape, in_specs=in_specs, out_specs=[HBM] * len(out_shape),
        input_output_aliases=aliases, compiler_params=pltpu.CompilerParams(has_side_effects=EFFECT),
    )(*operands)
    res, po = [], 0
    for n in ns:
        res.append(list(outs[po + n:po + 2 * n]))
        po += 2 * n
    return res


def _sum8(name, r):
    _, rows, cols = r.shape
    tr = _pick(rows, (256, 128, 64, 32, 16, 8))

    def body(r_ref, o_ref):
        acc = r_ref[0].astype(F32)
        for p in range(1, N_DEV):
            acc = acc + r_ref[p].astype(F32)
        o_ref[...] = acc

    return pl.pallas_call(
        body, name=name, out_shape=jax.ShapeDtypeStruct((rows, cols), F32), grid=(rows // tr,),
        in_specs=[pl.BlockSpec((N_DEV, tr, cols), lambda i: (0, i, 0))],
        out_specs=pl.BlockSpec((tr, cols), lambda i: (i, 0)), compiler_params=_params(("parallel",)),
    )(r)


def _adamw(name, g, w, m, v):
    rows, cols = g.shape
    tr = _pick(rows, (512, 256, 128, 64, 32, 16, 8))
    c1 = 1.0 - ADAM_B1 ** ADAM_STEP
    c2 = 1.0 - ADAM_B2 ** ADAM_STEP

    def body(g_ref, w_ref, m_ref, v_ref, d_ref, nm_ref, nv_ref):
        gv = g_ref[...]
        nm = ADAM_B1 * m_ref[...] + (1.0 - ADAM_B1) * gv
        nv = ADAM_B2 * v_ref[...] + (1.0 - ADAM_B2) * (gv * gv)
        d_ref[...] = -ADAM_LR * ((nm / c1) / (jnp.sqrt(nv / c2) + ADAM_EPS) + ADAM_WD * w_ref[...])
        nm_ref[...] = nm
        nv_ref[...] = nv

    spec = pl.BlockSpec((tr, cols), lambda i: (i, 0))
    return pl.pallas_call(
        body, name=name, out_shape=[jax.ShapeDtypeStruct((rows, cols), F32)] * 3, grid=(rows // tr,),
        in_specs=[spec] * 4, out_specs=[spec] * 3, compiler_params=_params(("parallel",)),
    )(g, w, m, v)


def _mm_nt(name, a, ws, w_offs, n, epi, out_dtypes, extras=(), rows=(), tm=None, tn=None, chunk=None):
    m, k = a.shape
    tm = tm or _pick(m, (512, 256, 128))
    tn = tn or _pick(n, (1408, 1152, 1024, 512, 256, 128))
    chunk = chunk or tn
    nw, ne, nr = len(ws), len(extras), len(rows)

    def body(*refs):
        a_ref, w_refs = refs[0], refs[1:1 + nw]
        e_refs, r_refs = refs[1 + nw:1 + nw + ne], refs[1 + nw + ne:1 + nw + ne + nr]
        o_refs = refs[1 + nw + ne + nr:]
        av = a_ref[...].astype(BF16)
        for c0 in range(0, tn, chunk):
            cs = slice(c0, c0 + chunk)
            accs = [lax.dot_general(av, w[cs, :], (((1,), (1,)), ((), ())), preferred_element_type=F32)
                    for w in w_refs]
            outs = epi(accs, [e[:, cs] for e in e_refs], [r[:, cs] for r in r_refs])
            for o_ref, o in zip(o_refs, outs):
                o_ref[:, cs] = o.astype(o_ref.dtype)

    in_specs = [pl.BlockSpec((tm, k), lambda j, i: (i, 0))]
    in_specs += [pl.BlockSpec((tn, k), functools.partial(lambda j, i, off: (j + off, 0), off=off)) for off in w_offs]
    in_specs += [pl.BlockSpec((tm, tn), lambda j, i: (i, j))] * ne
    in_specs += [pl.BlockSpec((1, tn), lambda j, i: (0, j))] * nr
    return pl.pallas_call(
        body, name=name, out_shape=[jax.ShapeDtypeStruct((m, n), dt) for dt in out_dtypes],
        grid=(n // tn, m // tm), in_specs=in_specs,
        out_specs=[pl.BlockSpec((tm, tn), lambda j, i: (i, j))] * len(out_dtypes),
        compiler_params=_params(("parallel", "parallel"), V7X_VMEM_LIMIT_BYTES),
    )(a, *ws, *extras, *rows)


def _mm_nn(name, as_, bs, epi, out_dtype, extras=(), rows=(), tm=None):
    m, k = as_[0].shape
    n = bs[0].shape[1]
    tm = tm or _pick(m, (512, 256, 128))
    npair, ne, nr = len(as_), len(extras), len(rows)

    def body(*refs):
        a_refs, b_refs = refs[:npair], refs[npair:2 * npair]
        e_refs, r_refs = refs[2 * npair:2 * npair + ne], refs[2 * npair + ne:2 * npair + ne + nr]
        o_ref = refs[-1]
        acc = None
        for a_ref, b_ref in zip(a_refs, b_refs):
            p = jnp.dot(a_ref[...].astype(BF16), b_ref[...], preferred_element_type=F32)
            acc = p if acc is None else acc + p
        o_ref[...] = epi(acc, [e[...] for e in e_refs], [r[...] for r in r_refs]).astype(o_ref.dtype)

    in_specs = [pl.BlockSpec((tm, k), lambda i: (i, 0))] * npair
    in_specs += [pl.BlockSpec((k, n), lambda i: (0, 0))] * npair
    in_specs += [pl.BlockSpec((tm, n), lambda i: (i, 0))] * ne
    in_specs += [pl.BlockSpec((1, n), lambda i: (0, 0))] * nr
    return pl.pallas_call(
        body, name=name, out_shape=jax.ShapeDtypeStruct((m, n), out_dtype), grid=(m // tm,), in_specs=in_specs,
        out_specs=pl.BlockSpec((tm, n), lambda i: (i, 0)),
        compiler_params=_params(("parallel",), V7X_VMEM_LIMIT_BYTES),
    )(*as_, *bs, *extras, *rows)


def _mm_nn_rms_bwd(name, as_, bs, x, g, dx_out, tm=None):
    m, k = as_[0].shape
    n = bs[0].shape[1]
    tm = tm or _pick(m, (512, 256, 128))
    npair = len(as_)

    def body(*refs):
        a_refs, b_refs = refs[:npair], refs[npair:2 * npair]
        x_ref, g_ref, dxo_ref, dx_ref, dg_ref = refs[2 * npair:]
        dh = None
        for a_ref, b_ref in zip(a_refs, b_refs):
            p = jnp.dot(a_ref[...].astype(BF16), b_ref[...], preferred_element_type=F32)
            dh = p if dh is None else dh + p
        xv = x_ref[...]
        r = lax.rsqrt(jnp.mean(xv * xv, axis=-1, keepdims=True) + EPS)
        yv = xv * r
        dy = dh * g_ref[...]
        dx_ref[...] = dxo_ref[...] + r * (dy - yv * jnp.mean(dy * yv, axis=-1, keepdims=True))

        @pl.when(pl.program_id(0) == 0)
        def _():
            dg_ref[...] = jnp.zeros_like(dg_ref)

        dg_ref[...] += jnp.sum(dh * yv, axis=0, keepdims=True)

    big = pl.BlockSpec((tm, n), lambda i: (i, 0))
    row = pl.BlockSpec((1, n), lambda i: (0, 0))
    in_specs = [pl.BlockSpec((tm, k), lambda i: (i, 0))] * npair + [pl.BlockSpec((k, n), lambda i: (0, 0))] * npair
    return pl.pallas_call(
        body, name=name, out_shape=[jax.ShapeDtypeStruct((m, n), F32), jax.ShapeDtypeStruct((1, n), F32)],
        grid=(m // tm,), in_specs=in_specs + [big, row, big], out_specs=[big, row],
        compiler_params=_params(("arbitrary",), V7X_VMEM_LIMIT_BYTES),
    )(*as_, *bs, x, g, dx_out)


def _mm_tn(name, a, b, colsum_b=False, tm=None, tk=1024):
    t, ma = a.shape
    nb = b.shape[1]
    tm = tm or _pick(ma, (1408, 1152, 1024, 768, 512, 256, 128))
    tk = _pick(t, (tk, 256, 128))
    nk = t // tk

    def body(*refs):
        a_ref, b_ref, o_ref = refs[0], refs[1], refs[2]
        acc_ref = refs[-1]
        kk = pl.program_id(1)
        bv = b_ref[...]

        @pl.when(kk == 0)
        def _():
            acc_ref[...] = jnp.zeros_like(acc_ref)

        acc_ref[...] += lax.dot_general(a_ref[...].astype(BF16), bv.astype(BF16), (((0,), (0,)), ((), ())),
                                        preferred_element_type=F32)
        if colsum_b:
            s_ref = refs[3]

            @pl.when((kk == 0) & (pl.program_id(0) == 0))
            def _():
                s_ref[...] = jnp.zeros_like(s_ref)

            @pl.when(pl.program_id(0) == 0)
            def _():
                s_ref[...] += jnp.sum(bv.astype(F32), axis=0, keepdims=True)

        @pl.when(kk == nk - 1)
        def _():
            o_ref[...] = acc_ref[...].astype(o_ref.dtype)

    out_shape = [jax.ShapeDtypeStruct((ma, nb), BF16)]
    out_specs = [pl.BlockSpec((tm, nb), lambda i, kk: (i, 0))]
    if colsum_b:
        out_shape.append(jax.ShapeDtypeStruct((1, nb), F32))
        out_specs.append(pl.BlockSpec((1, nb), lambda i, kk: (0, 0)))
    res = pl.pallas_call(
        body, name=name, out_shape=out_shape, grid=(ma // tm, nk),
        in_specs=[pl.BlockSpec((tk, tm), lambda i, kk: (kk, i)), pl.BlockSpec((tk, nb), lambda i, kk: (kk, 0))],
        out_specs=out_specs, scratch_shapes=[pltpu.VMEM((tm, nb), F32)],
        compiler_params=_params(("arbitrary", "arbitrary"), V7X_VMEM_LIMIT_BYTES),
    )(a, b)
    return res if colsum_b else res[0]


def _rmsnorm_fwd(name, x, g):
    t, d = x.shape
    tr = _pick(t, (512, 256, 128))

    def body(x_ref, g_ref, h_ref):
        xv = x_ref[...]
        r = lax.rsqrt(jnp.mean(xv * xv, axis=-1, keepdims=True) + EPS)
        h_ref[...] = (xv * r * g_ref[...]).astype(BF16)

    return pl.pallas_call(
        body, name=name, out_shape=jax.ShapeDtypeStruct((t, d), BF16), grid=(t // tr,),
        in_specs=[pl.BlockSpec((tr, d), lambda i: (i, 0)), pl.BlockSpec((1, d), lambda i: (0, 0))],
        out_specs=pl.BlockSpec((tr, d), lambda i: (i, 0)), compiler_params=_params(("parallel",)),
    )(x, g)


def _rmsnorm_bwd(name, dh, x, g, dx_out):
    t, d = x.shape
    tr = _pick(t, (512, 256, 128))

    def body(dh_ref, x_ref, g_ref, dxo_ref, dx_ref, dg_ref):
        xv = x_ref[...]
        r = lax.rsqrt(jnp.mean(xv * xv, axis=-1, keepdims=True) + EPS)
        yv = xv * r
        dhv = dh_ref[...].astype(F32)
        dy = dhv * g_ref[...]
        dx_ref[...] = dxo_ref[...] + r * (dy - yv * jnp.mean(dy * yv, axis=-1, keepdims=True))

        @pl.when(pl.program_id(0) == 0)
        def _():
            dg_ref[...] = jnp.zeros_like(dg_ref)

        dg_ref[...] += jnp.sum(dhv * yv, axis=0, keepdims=True)

    big = pl.BlockSpec((tr, d), lambda i: (i, 0))
    row = pl.BlockSpec((1, d), lambda i: (0, 0))
    return pl.pallas_call(
        body, name=name, out_shape=[jax.ShapeDtypeStruct((t, d), F32), jax.ShapeDtypeStruct((1, d), F32)],
        grid=(t // tr,), in_specs=[big, big, row, big], out_specs=[big, row],
        compiler_params=_params(("arbitrary",)),
    )(dh, x, g, dx_out)


def _loss_head(name, x, g, target):
    t, d = x.shape
    tr = _pick(t, (512, 256, 128))

    def body(x_ref, g_ref, t_ref, dx_ref, dg_ref, l_ref):
        xv = x_ref[...]
        r = lax.rsqrt(jnp.mean(xv * xv, axis=-1, keepdims=True) + EPS)
        yv = xv * r
        diff = yv * g_ref[...] - t_ref[...]
        dout = diff * (1.0 / d)
        dy = dout * g_ref[...]
        dx_ref[...] = r * (dy - yv * jnp.mean(dy * yv, axis=-1, keepdims=True))

        @pl.when(pl.program_id(0) == 0)
        def _():
            dg_ref[...] = jnp.zeros_like(dg_ref)
            l_ref[...] = jnp.zeros_like(l_ref)

        dg_ref[...] += jnp.sum(dout * yv, axis=0, keepdims=True)
        l_ref[...] += (0.5 / d) * jnp.sum(diff * diff, axis=0, keepdims=True)

    big = pl.BlockSpec((tr, d), lambda i: (i, 0))
    row = pl.BlockSpec((1, d), lambda i: (0, 0))
    return pl.pallas_call(
        body, name=name,
        out_shape=[jax.ShapeDtypeStruct((t, d), F32), jax.ShapeDtypeStruct((1, d), F32),
                   jax.ShapeDtypeStruct((1, d), F32)],
        grid=(t // tr,), in_specs=[big, row, big], out_specs=[big, row, row],
        compiler_params=_params(("arbitrary",)),
    )(x, g, target)


CONV_ROWS = 64
SUBLANES = 8


def _fill_window(win_ref, sh_ref, parts):
    rows = sh_ref.shape[2]
    for cb in range(win_ref.shape[0]):
        for r0, val in parts:
            win_ref[cb, r0:r0 + val.shape[0], :] = val[:, 128 * cb:128 * (cb + 1)]
        win_ref[cb, rows:rows + SUBLANES, :] = jnp.zeros((SUBLANES, 128), F32)
        for b in range(1, SUBLANES):
            sh_ref[b - 1, cb] = win_ref[cb, b:b + rows, :]


def _window_rows(win_ref, sh_ref, o, cb):
    b = o % SUBLANES
    if b == 0:
        return win_ref[cb, o:o + CONV_ROWS, :]
    return sh_ref[b - 1, cb, o - b:o - b + CONV_ROWS, :]


def _window_scratch(rows, c):
    return [pltpu.VMEM((c // 128, rows + SUBLANES, 128), F32), pltpu.VMEM((SUBLANES - 1, c // 128, rows, 128), F32)]


def _dwconv_fwd(name, glu, w_dw, b_dw, ln_g, ln_b):
    t, c = glu.shape
    tt = _pick(t, (256, 128))
    hb = tt // CONV_HALO

    def body(cur_ref, halo_ref, w_ref, b_ref, g_ref, be_ref, dw_ref, s_ref, win_ref, sh_ref):
        i = pl.program_id(0)
        halo = jnp.where(i > 0, halo_ref[...].astype(F32), 0.0)
        _fill_window(win_ref, sh_ref, [(0, halo), (CONV_HALO, cur_ref[...].astype(F32))])
        for r0 in range(0, tt, CONV_ROWS):
            for cb in range(c // 128):
                c0 = 128 * cb
                acc = jnp.zeros((CONV_ROWS, 128), F32) + b_ref[:, c0:c0 + 128]
                for k in range(CONV_WIDTH):
                    o = r0 + k + CONV_HALO - (CONV_WIDTH - 1)
                    acc = acc + w_ref[k:k + 1, c0:c0 + 128] * _window_rows(win_ref, sh_ref, o, cb)
                dw_ref[r0:r0 + CONV_ROWS, c0:c0 + 128] = acc
        u = dw_ref[...]
        mu = jnp.mean(u, axis=-1, keepdims=True)
        uc = u - mu
        rstd = lax.rsqrt(jnp.mean(uc * uc, axis=-1, keepdims=True) + EPS)
        z = uc * rstd * g_ref[...] + be_ref[...]
        s_ref[...] = (z * _sigmoid(z)).astype(BF16)

    big = pl.BlockSpec((tt, c), lambda i: (i, 0))
    row = pl.BlockSpec((1, c), lambda i: (0, 0))
    return pl.pallas_call(
        body, name=name, out_shape=[jax.ShapeDtypeStruct((t, c), F32), jax.ShapeDtypeStruct((t, c), BF16)],
        grid=(t // tt,),
        in_specs=[big, pl.BlockSpec((CONV_HALO, c), lambda i: (jnp.maximum(i * hb - 1, 0), 0)),
                  pl.BlockSpec((CONV_HALO, c), lambda i: (0, 0)), row, row, row],
        out_specs=[big, big], scratch_shapes=_window_scratch(tt + CONV_HALO, c),
        compiler_params=_params(("parallel",), V7X_VMEM_LIMIT_BYTES),
    )(glu, glu, w_dw, b_dw, ln_g, ln_b)


def _ln_silu_bwd(name, ds, dw, ln_g, ln_b):
    t, c = dw.shape
    tr = _pick(t, (256, 128))

    def body(ds_ref, dw_ref, g_ref, be_ref, o_ref, acc_ref):
        u = dw_ref[...]
        mu = jnp.mean(u, axis=-1, keepdims=True)
        uc = u - mu
        rstd = lax.rsqrt(jnp.mean(uc * uc, axis=-1, keepdims=True) + EPS)
        xh = uc * rstd
        z = xh * g_ref[...] + be_ref[...]
        sg = _sigmoid(z)
        dz = ds_ref[...].astype(F32) * (sg * (1.0 + z * (1.0 - sg)))
        dxh = dz * g_ref[...]
        du = rstd * (dxh - jnp.mean(dxh, axis=-1, keepdims=True) - xh * jnp.mean(dxh * xh, axis=-1, keepdims=True))
        o_ref[...] = du

        @pl.when(pl.program_id(0) == 0)
        def _():
            acc_ref[...] = jnp.zeros_like(acc_ref)

        acc_ref[0:1, :] += jnp.sum(dz * xh, axis=0, keepdims=True)
        acc_ref[1:2, :] += jnp.sum(dz, axis=0, keepdims=True)
        acc_ref[2:3, :] += jnp.sum(du, axis=0, keepdims=True)

    big = pl.BlockSpec((tr, c), lambda i: (i, 0))
    row = pl.BlockSpec((1, c), lambda i: (0, 0))
    return pl.pallas_call(
        body, name=name, out_shape=[jax.ShapeDtypeStruct((t, c), F32), jax.ShapeDtypeStruct((8, c), F32)],
        grid=(t // tr,), in_specs=[big, big, row, row],
        out_specs=[big, pl.BlockSpec((8, c), lambda i: (0, 0))], compiler_params=_params(("arbitrary",)),
    )(ds, dw, ln_g, ln_b)


def _dwconv_bwd(name, ddw, a, gt, w_dw):
    t, c = ddw.shape
    tt = _pick(t, (256, 128))
    hb = tt // CONV_HALO
    last = t // tt - 1
    back = CONV_WIDTH - 1

    def body(d_ref, dn_ref, a_ref, ap_ref, g_ref, gp_ref, w_ref, du_ref, dwk_ref, db_ref,
             wd_ref, shd_ref, wg_ref, shg_ref, dg_ref, dwk8_ref):
        i = pl.program_id(0)
        _fill_window(wd_ref, shd_ref, [(0, d_ref[...]), (tt, jnp.where(i < last, dn_ref[...], 0.0))])
        glu_prev = ap_ref[...].astype(F32) * _sigmoid(gp_ref[...].astype(F32))
        av = a_ref[...].astype(F32)
        sg = _sigmoid(g_ref[...].astype(F32))
        _fill_window(wg_ref, shg_ref, [(0, jnp.where(i > 0, glu_prev, 0.0)), (CONV_HALO, av * sg)])

        @pl.when(i == 0)
        def _():
            dwk8_ref[...] = jnp.zeros_like(dwk8_ref)
            db_ref[...] = jnp.zeros_like(db_ref)

        for r0 in range(0, tt, CONV_ROWS):
            for cb in range(c // 128):
                c0 = 128 * cb
                dcur = wd_ref[cb, r0:r0 + CONV_ROWS, :]
                acc = jnp.zeros((CONV_ROWS, 128), F32)
                for k in range(CONV_WIDTH):
                    acc = acc + w_ref[k:k + 1, c0:c0 + 128] * _window_rows(wd_ref, shd_ref, r0 + back - k, cb)
                    p = dcur * _window_rows(wg_ref, shg_ref, r0 + k + CONV_HALO - back, cb)
                    s8 = p[0:SUBLANES]
                    for q in range(SUBLANES, CONV_ROWS, SUBLANES):
                        s8 = s8 + p[q:q + SUBLANES]
                    dwk8_ref[SUBLANES * k:SUBLANES * (k + 1), c0:c0 + 128] += s8
                dg_ref[r0:r0 + CONV_ROWS, c0:c0 + 128] = acc

        @pl.when(i == last)
        def _():
            for k in range(CONV_WIDTH):
                dwk_ref[k:k + 1, :] = jnp.sum(dwk8_ref[SUBLANES * k:SUBLANES * (k + 1), :], axis=0, keepdims=True)
            dwk_ref[CONV_WIDTH:, :] = jnp.zeros((CONV_HALO - CONV_WIDTH, c), F32)
        dglu = dg_ref[...]
        da = dglu * sg
        dgate = dglu * av * sg * (1.0 - sg)
        du_ref[:, 0:c] = da.astype(BF16)
        du_ref[:, c:] = dgate.astype(BF16)
        db_ref[:, 0:c] += jnp.sum(da, axis=0, keepdims=True)
        db_ref[:, c:] += jnp.sum(dgate, axis=0, keepdims=True)

    big = pl.BlockSpec((tt, c), lambda i: (i, 0))
    prev = pl.BlockSpec((CONV_HALO, c), lambda i: (jnp.maximum(i * hb - 1, 0), 0))
    nxt = pl.BlockSpec((CONV_HALO, c), lambda i: (jnp.minimum((i + 1) * hb, t // CONV_HALO - 1), 0))
    return pl.pallas_call(
        body, name=name,
        out_shape=[jax.ShapeDtypeStruct((t, 2 * c), BF16), jax.ShapeDtypeStruct((CONV_HALO, c), F32),
                   jax.ShapeDtypeStruct((1, 2 * c), F32)],
        grid=(t // tt,),
        in_specs=[big, nxt, big, prev, big, prev, pl.BlockSpec((CONV_HALO, c), lambda i: (0, 0))],
        out_specs=[pl.BlockSpec((tt, 2 * c), lambda i: (i, 0)), pl.BlockSpec((CONV_HALO, c), lambda i: (0, 0)),
                   pl.BlockSpec((1, 2 * c), lambda i: (0, 0))],
        scratch_shapes=_window_scratch(tt + CONV_HALO, c) + _window_scratch(tt + CONV_HALO, c)
        + [pltpu.VMEM((tt, c), F32), pltpu.VMEM((SUBLANES * CONV_HALO, c), F32)],
        compiler_params=_params(("arbitrary",), V7X_VMEM_LIMIT_BYTES),
    )(ddw, ddw, a, a, gt, gt, w_dw)


def _bucket_tables():
    i = np.arange(N_BACK)[:, None]
    j = np.arange(2 * N_BACK)[None, :]
    dist = i + N_BACK - j
    valid = (dist >= 0) & (dist <= N_BACK)
    max_exact = N_BUCKETS // 2
    out = []
    for d in DILATIONS:
        n = np.maximum(dist * d, 0)
        nf = np.maximum(n, 1).astype(np.float32)
        large = max_exact + (np.log(nf / np.float32(max_exact)) / np.float32(math.log(REL_MAX_DISTANCE / max_exact))
                             * np.float32(N_BUCKETS - max_exact)).astype(np.int32)
        large = np.minimum(large, N_BUCKETS - 1)
        out.append(np.where(valid, np.where(n < max_exact, n, large), -1))
    return np.stack(out).astype(np.int32)


def _bias_build(name, rel_bias, buckets):
    def body(tbl_ref, bk_ref, o_ref):
        g = pl.program_id(0)
        bk = bk_ref[0]
        for h in range(HEADS_PER_GROUP):
            acc = jnp.zeros(bk.shape, F32)
            for b in range(N_BUCKETS):
                acc = jnp.where(bk == b, tbl_ref[b, g * HEADS_PER_GROUP + h], acc)
            o_ref[h] = jnp.where(bk < 0, NEG_INF, acc)

    return pl.pallas_call(
        body, name=name, out_shape=jax.ShapeDtypeStruct((N_HEADS, N_BACK, 2 * N_BACK), F32), grid=(N_GROUPS,),
        in_specs=[pl.BlockSpec(memory_space=pltpu.SMEM), pl.BlockSpec((1, N_BACK, 2 * N_BACK), lambda g: (g, 0, 0))],
        out_specs=pl.BlockSpec((HEADS_PER_GROUP, N_BACK, 2 * N_BACK), lambda g: (g, 0, 0)),
        compiler_params=_params(("arbitrary",)),
    )(rel_bias, buckets)


def _bias_grad(name, dbs, buckets):
    nd = len(dbs)

    def body(*refs):
        bk = refs[nd][0]
        o_ref = refs[nd + 1]
        lane = lax.broadcasted_iota(jnp.int32, (1, 128), 1)
        db = [sum(r[h] for r in refs[:nd]) for h in range(HEADS_PER_GROUP)]
        for b in range(N_BUCKETS):
            row = jnp.zeros((1, 128), F32)
            for h in range(HEADS_PER_GROUP):
                s = jnp.sum(jnp.where(bk == b, db[h], 0.0), axis=0, keepdims=True)
                s = jnp.sum(s, axis=1, keepdims=True)
                row = jnp.where(lane // 32 == h, s, row)
            o_ref[0, b:b + 1, :] = row

    spec = pl.BlockSpec((HEADS_PER_GROUP, N_BACK, 2 * N_BACK), lambda g: (g, 0, 0))
    return pl.pallas_call(
        body, name=name, out_shape=jax.ShapeDtypeStruct((N_GROUPS, N_BUCKETS, 128), F32), grid=(N_GROUPS,),
        in_specs=[spec] * nd + [pl.BlockSpec((1, N_BACK, 2 * N_BACK), lambda g: (g, 0, 0))],
        out_specs=pl.BlockSpec((1, N_BUCKETS, 128), lambda g: (g, 0, 0)), compiler_params=_params(("arbitrary",)),
    )(*dbs, buckets)


def _head_cols(h):
    return slice(h * HEAD_DIM, (h + 1) * HEAD_DIM)


def _attn_fwd(name, qkv, bias, g):
    t = qkv.shape[0]
    d = DILATIONS[g]
    tq = t // d
    nblk = qkv.shape[1] // GROUP_COLS
    scale = HEAD_DIM ** -0.5

    def body(q_ref, kp_ref, kc_ref, vp_ref, vc_ref, b_ref, o_ref, l_ref):
        m2 = pl.program_id(1)
        col = lax.broadcasted_iota(jnp.int32, (N_BACK, 2 * N_BACK), 1)
        lane = lax.broadcasted_iota(jnp.int32, (N_BACK, 128), 1)
        for sub in range(2):
            rows = slice(N_BACK * sub, N_BACK * (sub + 1))
            lse_tile = jnp.zeros((N_BACK, 128), F32)
            outs = []
            for h in range(HEADS_PER_GROUP):
                hc = _head_cols(h)
                if sub == 0:
                    kk = jnp.concatenate([kp_ref[:, hc], kc_ref[0:N_BACK, hc]], axis=0)
                    vv = jnp.concatenate([vp_ref[:, hc], vc_ref[0:N_BACK, hc]], axis=0)
                else:
                    kk, vv = kc_ref[:, hc], vc_ref[:, hc]
                s = lax.dot_general(q_ref[rows, hc], kk, (((1,), (1,)), ((), ())), preferred_element_type=F32)
                s = s * scale + b_ref[h]
                if sub == 0:
                    s = jnp.where((col >= N_BACK) | (m2 > 0), s, NEG_INF)
                m = jnp.max(s, axis=-1, keepdims=True)
                p = jnp.exp(s - m)
                den = jnp.sum(p, axis=-1, keepdims=True)
                outs.append(jnp.dot(p.astype(BF16), vv, preferred_element_type=F32) / den)
                lse_tile = jnp.where(lane // 32 == h, m + jnp.log(den), lse_tile)
            o_ref[rows, :] = jnp.concatenate(outs, axis=1)
            l_ref[rows, :] = lse_tile

    def blk(part, prev):
        if prev:
            return pl.BlockSpec((N_BACK, GROUP_COLS), lambda r, n: (jnp.maximum(2 * n - 1, 0), r * nblk + part))
        return pl.BlockSpec((2 * N_BACK, GROUP_COLS), lambda r, n: (n, r * nblk + part))

    qv = qkv.reshape(tq, d * qkv.shape[1])
    o, l = pl.pallas_call(
        body, name=name,
        out_shape=[jax.ShapeDtypeStruct((tq, d * GROUP_COLS), F32), jax.ShapeDtypeStruct((tq, d * 128), F32)],
        grid=(d, tq // (2 * N_BACK)),
        in_specs=[blk(0, False), blk(1, True), blk(1, False), blk(2, True), blk(2, False),
                  pl.BlockSpec((HEADS_PER_GROUP, N_BACK, 2 * N_BACK), lambda r, n: (g, 0, 0))],
        out_specs=[pl.BlockSpec((2 * N_BACK, GROUP_COLS), lambda r, n: (n, r)),
                   pl.BlockSpec((2 * N_BACK, 128), lambda r, n: (n, r))],
        compiler_params=_params(("parallel", "parallel")),
    )(qv, qv, qv, qv, qv, bias)
    return o.reshape(t, GROUP_COLS), l.reshape(t, 128)


def _group_weights(l_refs, h):
    ls = [l_ref[:, 32 * h:32 * h + 1] for l_ref in l_refs]
    m = jnp.maximum(jnp.maximum(ls[0], ls[1]), ls[2])
    es = [jnp.exp(l - m) for l in ls]
    tot = es[0] + es[1] + es[2]
    return [e / tot for e in es]


def _attn_merge(name, os_, ls):
    t = os_[0].shape[0]
    tr = _pick(t, (512, 256, 128))

    def body(o0, o1, o2, l0, l1, l2, out_ref):
        o_refs = (o0, o1, o2)
        pieces = [[None] * HEADS_PER_GROUP for _ in range(N_GROUPS)]
        for h in range(HEADS_PER_GROUP):
            al = _group_weights((l0, l1, l2), h)
            for g in range(N_GROUPS):
                pieces[g][h] = o_refs[g][:, _head_cols(h)] * al[g]
        out_ref[...] = jnp.concatenate([p for row in pieces for p in row], axis=1).astype(BF16)

    so = pl.BlockSpec((tr, GROUP_COLS), lambda i: (i, 0))
    sl = pl.BlockSpec((tr, 128), lambda i: (i, 0))
    return pl.pallas_call(
        body, name=name, out_shape=jax.ShapeDtypeStruct((t, D_ATTN), BF16), grid=(t // tr,),
        in_specs=[so] * 3 + [sl] * 3, out_specs=pl.BlockSpec((tr, D_ATTN), lambda i: (i, 0)),
        compiler_params=_params(("parallel",)),
    )(*os_, *ls)


def _attn_bwd_prep(name, d_out, os_, ls):
    t = d_out.shape[0]
    tr = _pick(t, (512, 256, 128))

    def body(do_ref, o0, o1, o2, l0, l1, l2, d0, d1, d2, c0, c1, c2):
        o_refs, d_refs, c_refs = (o0, o1, o2), (d0, d1, d2), (c0, c1, c2)
        lane = lax.broadcasted_iota(jnp.int32, (tr, 128), 1)
        dos = [[None] * HEADS_PER_GROUP for _ in range(N_GROUPS)]
        cs = [jnp.zeros((tr, 128), F32) for _ in range(N_GROUPS)]
        for h in range(HEADS_PER_GROUP):
            al = _group_weights((l0, l1, l2), h)
            tot = jnp.zeros((tr, 1), F32)
            for g in range(N_GROUPS):
                dv = do_ref[:, g * GROUP_COLS + h * HEAD_DIM:g * GROUP_COLS + (h + 1) * HEAD_DIM].astype(F32)
                tot = tot + al[g] * jnp.sum(dv * o_refs[g][:, _head_cols(h)], axis=-1, keepdims=True)
                dos[g][h] = dv * al[g]
            for g in range(N_GROUPS):
                cs[g] = jnp.where(lane // 32 == h, -al[g] * tot, cs[g])
        for g in range(N_GROUPS):
            d_refs[g][...] = jnp.concatenate(dos[g], axis=1).astype(BF16)
            c_refs[g][...] = cs[g]

    so = pl.BlockSpec((tr, GROUP_COLS), lambda i: (i, 0))
    sl = pl.BlockSpec((tr, 128), lambda i: (i, 0))
    res = pl.pallas_call(
        body, name=name,
        out_shape=[jax.ShapeDtypeStruct((t, GROUP_COLS), BF16)] * 3 + [jax.ShapeDtypeStruct((t, 128), F32)] * 3,
        grid=(t // tr,), in_specs=[pl.BlockSpec((tr, D_ATTN), lambda i: (i, 0))] + [so] * 3 + [sl] * 3,
        out_specs=[so] * 3 + [sl] * 3, compiler_params=_params(("parallel",)),
    )(d_out, *os_, *ls)
    return res[:3], res[3:]


def _attn_bwd(name, qkv, do, lse, cterm, bias, g):
    t = qkv.shape[0]
    d = DILATIONS[g]
    tq = t // d
    nb = tq // N_BACK
    nblk = qkv.shape[1] // GROUP_COLS
    scale = HEAD_DIM ** -0.5
    nt = (((1,), (1,)), ((), ()))
    tn = (((0,), (0,)), ((), ()))

    def body(qn, qx, kp, kn, vp, vn, don, dox, ln, lx, cn, cx, b_ref, dqkv_ref, db_ref):
        n = pl.program_id(1)
        has_prev = n > 0
        has_next = n < nb - 1

        @pl.when((n == 0) & (pl.program_id(0) == 0))
        def _():
            db_ref[...] = jnp.zeros_like(db_ref)

        row2 = lax.broadcasted_iota(jnp.int32, (2 * N_BACK, N_BACK), 0)
        on_ac = (row2 < N_BACK) | has_next
        dqs, dks, dvs = [], [], []
        for h in range(HEADS_PER_GROUP):
            hc = _head_cols(h)
            st = slice(32 * h, 32 * h + 1)
            b_prev, b_same = b_ref[h, :, 0:N_BACK], b_ref[h, :, N_BACK:]
            q0, k0, k1, v0, v1, d0 = qn[:, hc], kp[:, hc], kn[:, hc], vp[:, hc], vn[:, hc], don[:, hc]
            q_ac = jnp.concatenate([q0, qx[:, hc]], axis=0)
            d_ac = jnp.concatenate([d0, dox[:, hc]], axis=0)
            l_ac = jnp.concatenate([ln[:, st], lx[:, st]], axis=0)
            c_ac = jnp.concatenate([cn[:, st], cx[:, st]], axis=0)
            s_ac = (lax.dot_general(q_ac, k1, nt, preferred_element_type=F32) * scale
                    + jnp.concatenate([b_same, b_prev], axis=0))
            p_ac = jnp.where(on_ac, jnp.exp(s_ac - l_ac), 0.0)
            ds_ac = p_ac * (lax.dot_general(d_ac, v1, nt, preferred_element_type=F32) + c_ac)
            s_b = lax.dot_general(q0, k0, nt, preferred_element_type=F32) * scale + b_prev
            p_b = jnp.where(has_prev, jnp.exp(s_b - ln[:, st]), 0.0)
            ds_b = p_b * (lax.dot_general(d0, v0, nt, preferred_element_type=F32) + cn[:, st])
            ds_a = ds_ac[0:N_BACK]
            dqs.append(scale * jnp.dot(jnp.concatenate([ds_b, ds_a], axis=1).astype(BF16),
                                       jnp.concatenate([k0, k1], axis=0), preferred_element_type=F32))
            dks.append(scale * lax.dot_general(ds_ac.astype(BF16), q_ac, tn, preferred_element_type=F32))
            dvs.append(lax.dot_general(p_ac.astype(BF16), d_ac, tn, preferred_element_type=F32))
            db_ref[h, :, 0:N_BACK] += ds_b
            db_ref[h, :, N_BACK:] += ds_a
        dqkv_ref[...] = jnp.concatenate(dqs + dks + dvs, axis=1).astype(BF16)

    def rows(which):
        if which == "prev":
            return lambda n: jnp.maximum(n - 1, 0)
        if which == "next":
            return lambda n: jnp.minimum(n + 1, nb - 1)
        return lambda n: n

    def qkv_blk(part, which):
        f = rows(which)
        return pl.BlockSpec((N_BACK, GROUP_COLS), lambda r, n: (f(n), r * nblk + part))

    def grp_blk(width, which):
        f = rows(which)
        return pl.BlockSpec((N_BACK, width), lambda r, n: (f(n), r))

    qv = qkv.reshape(tq, d * qkv.shape[1])
    dov = do.reshape(tq, d * GROUP_COLS)
    lv = lse.reshape(tq, d * 128)
    cv = cterm.reshape(tq, d * 128)
    dqkv_g, db = pl.pallas_call(
        body, name=name,
        out_shape=[jax.ShapeDtypeStruct((tq, d * 3 * GROUP_COLS), BF16),
                   jax.ShapeDtypeStruct((HEADS_PER_GROUP, N_BACK, 2 * N_BACK), F32)],
        grid=(d, nb),
        in_specs=[qkv_blk(0, "same"), qkv_blk(0, "next"), qkv_blk(1, "prev"), qkv_blk(1, "same"),
                  qkv_blk(2, "prev"), qkv_blk(2, "same"), grp_blk(GROUP_COLS, "same"), grp_blk(GROUP_COLS, "next"),
                  grp_blk(128, "same"), grp_blk(128, "next"), grp_blk(128, "same"), grp_blk(128, "next"),
                  pl.BlockSpec((HEADS_PER_GROUP, N_BACK, 2 * N_BACK), lambda r, n: (g, 0, 0))],
        out_specs=[pl.BlockSpec((N_BACK, 3 * GROUP_COLS), lambda r, n: (n, r)),
                   pl.BlockSpec((HEADS_PER_GROUP, N_BACK, 2 * N_BACK), lambda r, n: (0, 0, 0))],
        compiler_params=_params(("arbitrary", "arbitrary")),
    )(qv, qv, qv, qv, qv, qv, dov, dov, lv, lv, cv, cv, bias)
    return dqkv_g.reshape(t, 3 * GROUP_COLS), db


def _row(v):
    return v.reshape(1, -1)


def _plain(accs, extras, rows):
    return (accs[0],)


def _glu_epi(accs, extras, rows):
    a = (accs[0] + rows[0]).astype(BF16)
    gt = (accs[1] + rows[1]).astype(BF16)
    return a, gt, a.astype(F32) * _sigmoid(gt.astype(F32))


def _swiglu_epi(accs, extras, rows):
    gq, uq = accs[0].astype(BF16), accs[1].astype(BF16)
    gf = gq.astype(F32)
    return gq, uq, gf * _sigmoid(gf) * uq.astype(F32)


def _swiglu_bwd_epi(accs, extras, rows):
    gf, uf = extras[0].astype(F32), extras[1].astype(F32)
    sg = _sigmoid(gf)
    return accs[0] * uf * (sg * (1.0 + gf * (1.0 - sg))), accs[0] * gf * sg


def _residual(acc, extras, rows):
    out = acc + extras[0]
    return out + rows[0] if rows else out


def _identity(acc, extras, rows):
    return acc


def _group_rows(w):
    parts = [w[p * D_ATTN:(p + 1) * D_ATTN].reshape(N_GROUPS, GROUP_COLS, -1) for p in range(3)]
    return jnp.concatenate(parts, axis=1)


def _ungroup_rows(wg):
    return jnp.concatenate([wg[g][p * GROUP_COLS:(p + 1) * GROUP_COLS] for p in range(3) for g in range(N_GROUPS)],
                           axis=0)


def _local_step(x, target, sm, depth, fetch, emit):
    d_model = x.shape[1]
    buckets = jnp.asarray(_bucket_tables())
    bias = _bias_build("bias_build", sm["rel_bias"], buckets)
    saved = []
    for i in range(depth):
        j = i // 2
        rec = {"x_mix": x}
        wm = fetch(2 * i, x)
        h = _rmsnorm_fwd(f"rms_mix_fwd{i}", x, _row(sm["norm_mix"][i]))
        rec.update(h_mix=h, wm=wm)
        if i % 2 == 0:
            c = wm["w2"].shape[0]
            tn = _pick(c, (512, 256, 128))
            b1 = sm["conv_b_pw1"][j]
            a, gt, glu = _mm_nt(f"conv_pw1_fwd{j}", h, [wm["w1t"]] * 2, [0, c // tn], c, _glu_epi, (BF16,) * 3,
                                rows=[_row(b1[:c]), _row(b1[c:])], tn=tn)
            dw, s = _dwconv_fwd(f"dwconv_fwd{j}", glu, wm["wdw"], _row(sm["conv_b_dw"][j]),
                                _row(sm["conv_ln_g"][j]), _row(sm["conv_ln_b"][j]))
            x = _mm_nn(f"conv_pw2_fwd{j}", [s], [wm["w2"]], _residual, F32, extras=[x],
                       rows=[_row(sm["conv_b_pw2"][j])])
            rec.update(a=a, gt=gt, dw=dw, s=s)
        else:
            wq = _group_rows(wm["wqkvt"])
            qkv = _mm_nt(f"attn_qkv_fwd{j}", h, [wq[g] for g in range(N_GROUPS)], [0] * N_GROUPS, D_ATTN,
                         lambda accs, e, r: tuple(accs), (BF16,) * N_GROUPS, tn=D_ATTN)
            og = [_attn_fwd(f"attn_fwd{j}_{g}", qkv[g], bias, g) for g in range(N_GROUPS)]
            os_, ls = [o for o, _ in og], [l for _, l in og]
            om = _attn_merge(f"attn_merge{j}", os_, ls)
            x = _mm_nt(f"attn_out_fwd{j}", om, [wm["wot"]], [0], d_model, lambda accs, e, r: (accs[0] + e[0],),
                       (F32,), extras=[x])[0]
            rec.update(qkv=qkv, os=os_, ls=ls, om=om, wq=wq)
        rec["x_ffn"] = x
        wf = fetch(2 * i + 1, x)
        h2 = _rmsnorm_fwd(f"rms_ffn_fwd{i}", x, _row(sm["norm_ffn"][i]))
        f = wf["wd"].shape[0]
        gq, uq, act = _mm_nt(f"ffn_up_fwd{i}", h2, [wf["wgt"], wf["wut"]], [0, 0], f, _swiglu_epi, (BF16,) * 3)
        x = _mm_nn(f"ffn_down_fwd{i}", [act], [wf["wd"]], _residual, F32, extras=[x])
        rec.update(h_ffn=h2, gq=gq, uq=uq, act=act, wf=wf)
        saved.append(rec)

    dx, g_final, loss_cols = _loss_head("loss_head", x, _row(sm["final_norm"]), target)

    g_mix, g_ffn = [None] * depth, [None] * depth
    nconv = (depth + 1) // 2
    g_b1, g_bdw, g_lng, g_lnb, g_b2 = ([None] * nconv for _ in range(5))
    dbias = []
    for i in reversed(range(depth)):
        j = i // 2
        rec = saved[i]
        wm, wf = rec["wm"], rec["wf"]
        f = wf["wd"].shape[0]
        dgate, dup = _mm_nt(f"ffn_down_bwd{i}", dx, [wf["wd"]], [0], f, _swiglu_bwd_epi, (BF16, BF16),
                            extras=[rec["gq"], rec["uq"]], tn=f, chunk=_pick(f, (256, 128)))
        gf = {"wd": _mm_tn(f"ffn_down_dw{i}", rec["act"], dx),
              "wgt": _mm_tn(f"ffn_gate_dw{i}", dgate, rec["h_ffn"]),
              "wut": _mm_tn(f"ffn_up_dw{i}", dup, rec["h_ffn"])}
        dx, g_ffn[i] = _mm_nn_rms_bwd(f"ffn_up_bwd{i}", [dgate, dup], [wf["wgt"], wf["wut"]], rec["x_ffn"],
                                      _row(sm["norm_ffn"][i]), dx, tm=256)
        dx = emit(2 * i + 1, gf, dx)
        if i % 2 == 0:
            c = wm["w2"].shape[0]
            gm = {}
            gm["w2"], g_b2[j] = _mm_tn(f"conv_pw2_dw{j}", rec["s"], dx, colsum_b=True)
            ds = _mm_nt(f"conv_pw2_bwd{j}", dx, [wm["w2"]], [0], c, _plain, (BF16,))[0]
            ddw, sums = _ln_silu_bwd(f"ln_silu_bwd{j}", ds, rec["dw"], _row(sm["conv_ln_g"][j]),
                                     _row(sm["conv_ln_b"][j]))
            g_lng[j], g_lnb[j], g_bdw[j] = sums[0], sums[1], sums[2]
            du, dwk, db1 = _dwconv_bwd(f"dwconv_bwd{j}", ddw, rec["a"], rec["gt"], wm["wdw"])
            gm["wdw"] = dwk[:CONV_WIDTH]
            g_b1[j] = db1[0]
            gm["w1t"] = _mm_tn(f"conv_pw1_dw{j}", du, rec["h_mix"])
            dh_terms = ([du], [wm["w1t"]])
        else:
            gm = {"wot": _mm_tn(f"attn_out_dw{j}", dx, rec["om"])}
            d_om = _mm_nn(f"attn_out_bwd{j}", [dx], [wm["wot"]], _identity, BF16)
            dos, cs = _attn_bwd_prep(f"attn_bwd_prep{j}", d_om, rec["os"], rec["ls"])
            back = [_attn_bwd(f"attn_bwd{j}_{g}", rec["qkv"][g], dos[g], rec["ls"][g], cs[g], bias, g)
                    for g in range(N_GROUPS)]
            dqkv = [b[0] for b in back]
            dbias.append(jnp.concatenate([b[1] for b in back], axis=0))
            dh_terms = (dqkv, [rec["wq"][g] for g in range(N_GROUPS)])
            gm["wqkvt"] = _ungroup_rows([_mm_tn(f"attn_qkv_dw{j}_{g}", dqkv[g], rec["h_mix"])
                                         for g in range(N_GROUPS)])
        dx, g_mix[i] = _mm_nn_rms_bwd(f"mix_in_bwd{i}", dh_terms[0], dh_terms[1], rec["x_mix"],
                                      _row(sm["norm_mix"][i]), dx)
        dx = emit(2 * i, gm, dx)

    gb = _bias_grad("bias_grad", dbias, buckets)
    g_rel = jnp.transpose(gb[:, :, ::32], (1, 0, 2)).reshape(N_BUCKETS, N_HEADS)
    gsm = {
        "norm_mix": jnp.concatenate(g_mix, axis=0), "norm_ffn": jnp.concatenate(g_ffn, axis=0),
        "final_norm": g_final[0], "conv_b_pw1": jnp.stack(g_b1), "conv_b_dw": jnp.stack(g_bdw),
        "conv_ln_g": jnp.stack(g_lng), "conv_ln_b": jnp.stack(g_lnb),
        "conv_b_pw2": jnp.concatenate(g_b2, axis=0), "rel_bias": g_rel,
    }
    return loss_cols, dx, gsm


SMALL = ("norm_mix", "norm_ffn", "final_norm", "conv_b_pw1", "conv_b_dw", "conv_ln_g", "conv_ln_b", "conv_b_pw2",
         "rel_bias")
SHARDED = (("conv_w_pw1", "w1t", True), ("conv_w_pw2", "w2", False), ("attn_w_qkv", "wqkvt", True),
           ("attn_w_o", "wot", True), ("ffn_w_gate", "wgt", True), ("ffn_w_up", "wut", True),
           ("ffn_w_down", "wd", False))
ORDER = ("norm_mix", "norm_ffn", "final_norm", "conv_w_pw1", "conv_b_pw1", "conv_w_dw", "conv_b_dw", "conv_ln_g",
         "conv_ln_b", "conv_w_pw2", "conv_b_pw2", "attn_w_qkv", "attn_w_o", "rel_bias", "ffn_w_gate", "ffn_w_up",
         "ffn_w_down")
PACK_LANES = 128
PACK_ROW_TILE = 8


def _pack_small(vals):
    flat = jnp.concatenate([vals[n].reshape(-1) for n in SMALL])
    per_tile = PACK_LANES * PACK_ROW_TILE
    return jnp.pad(flat, (0, -flat.shape[0] % per_tile)).reshape(-1, PACK_LANES)


def _unpack_small(pack, like):
    flat, out, pos = pack.reshape(-1), {}, 0
    for n in SMALL:
        out[n] = flat[pos:pos + like[n].size].reshape(like[n].shape)
        pos += like[n].size
    return out


def _dw_blocks(w):
    l, k, c = w.shape
    blk = jnp.transpose(w.reshape(l, k, N_DEV, c // N_DEV), (2, 0, 1, 3)).reshape(N_DEV, l * k, c // N_DEV)
    return jnp.pad(blk, ((0, 0), (0, -(l * k) % 8), (0, 0)))


def kernel(x, norm_mix, norm_ffn, final_norm, conv_w_pw1, conv_b_pw1, conv_w_dw, conv_b_dw, conv_ln_g, conv_ln_b, conv_w_pw2, conv_b_pw2, attn_w_qkv, attn_w_o, rel_bias, ffn_w_gate, ffn_w_up, ffn_w_down, loss_target, m_norm_mix, m_norm_ffn, m_final_norm, m_conv_w_pw1, m_conv_b_pw1, m_conv_w_dw, m_conv_b_dw, m_conv_ln_g, m_conv_ln_b, m_conv_w_pw2, m_conv_b_pw2, m_attn_w_qkv, m_attn_w_o, m_rel_bias, m_ffn_w_gate, m_ffn_w_up, m_ffn_w_down, v_norm_mix, v_norm_ffn, v_final_norm, v_conv_w_pw1, v_conv_b_pw1, v_conv_w_dw, v_conv_b_dw, v_conv_ln_g, v_conv_ln_b, v_conv_w_pw2, v_conv_b_pw2, v_attn_w_qkv, v_attn_w_o, v_rel_bias, v_ffn_w_gate, v_ffn_w_up, v_ffn_w_down):
    w = dict(norm_mix=norm_mix, norm_ffn=norm_ffn, final_norm=final_norm, conv_w_pw1=conv_w_pw1,
             conv_b_pw1=conv_b_pw1, conv_w_dw=conv_w_dw, conv_b_dw=conv_b_dw, conv_ln_g=conv_ln_g,
             conv_ln_b=conv_ln_b, conv_w_pw2=conv_w_pw2, conv_b_pw2=conv_b_pw2, attn_w_qkv=attn_w_qkv,
             attn_w_o=attn_w_o, rel_bias=rel_bias, ffn_w_gate=ffn_w_gate, ffn_w_up=ffn_w_up, ffn_w_down=ffn_w_down)
    m = dict(norm_mix=m_norm_mix, norm_ffn=m_norm_ffn, final_norm=m_final_norm, conv_w_pw1=m_conv_w_pw1,
             conv_b_pw1=m_conv_b_pw1, conv_w_dw=m_conv_w_dw, conv_b_dw=m_conv_b_dw, conv_ln_g=m_conv_ln_g,
             conv_ln_b=m_conv_ln_b, conv_w_pw2=m_conv_w_pw2, conv_b_pw2=m_conv_b_pw2, attn_w_qkv=m_attn_w_qkv,
             attn_w_o=m_attn_w_o, rel_bias=m_rel_bias, ffn_w_gate=m_ffn_w_gate, ffn_w_up=m_ffn_w_up,
             ffn_w_down=m_ffn_w_down)
    v = dict(norm_mix=v_norm_mix, norm_ffn=v_norm_ffn, final_norm=v_final_norm, conv_w_pw1=v_conv_w_pw1,
             conv_b_pw1=v_conv_b_pw1, conv_w_dw=v_conv_w_dw, conv_b_dw=v_conv_b_dw, conv_ln_g=v_conv_ln_g,
             conv_ln_b=v_conv_ln_b, conv_w_pw2=v_conv_w_pw2, conv_b_pw2=v_conv_b_pw2, attn_w_qkv=v_attn_w_qkv,
             attn_w_o=v_attn_w_o, rel_bias=v_rel_bias, ffn_w_gate=v_ffn_w_gate, ffn_w_up=v_ffn_w_up,
             ffn_w_down=v_ffn_w_down)

    me = 4 * lax.axis_index("x") + 2 * lax.axis_index("y") + lax.axis_index("c")
    depth = ffn_w_gate.shape[0]
    n_conv, _, cb = conv_w_dw.shape

    def sublayer(key, layer):
        if key in ("wgt", "wut", "wd"):
            return 2 * layer + 1
        return 4 * layer if key in ("w1t", "w2") else 4 * layer + 2

    def landing(block, own):
        land = lax.empty((N_DEV,) + block.shape, block.dtype)
        return lax.dynamic_update_slice(land, own[None], (me,) + (0,) * block.ndim)

    by_sub = {s: [] for s in range(2 * depth)}
    for name, key, cols in SHARDED:
        sw = (jnp.swapaxes(w[name], 1, 2) if cols else w[name]).astype(BF16)
        for layer in range(sw.shape[0]):
            by_sub[sublayer(key, layer)].append((key, layer, sw[layer]))
    likes = {s: jnp.zeros((sum(sh.size for _, _, sh in by_sub[s]) // 1024, 1024), BF16) for s in by_sub}
    dw_shard = jnp.pad(conv_w_dw.reshape(-1, cb), ((0, -(n_conv * CONV_WIDTH) % 8), (0, 0)))
    like_dw = jnp.zeros(dw_shard.shape, F32)
    groups = [([dw_shard], [landing(dw_shard, dw_shard)])]
    for s in range(2 * depth):
        groups.append(([sh for _, _, sh in by_sub[s]], [landing(sh, sh) for _, _, sh in by_sub[s]]))
    gather, _ = _exchange_start("gather_start", groups, scatter=False, carry=jnp.zeros((8, 128), F32))
    dw_filters = []

    def fetch(s, after):
        pieces = [(gather[s + 1], likes[s])]
        if s == 0:
            pieces.append((gather[0], like_dw))
        landed = _exchange_wait(f"gather_wait{s}", pieces, after)
        out = {key: g.reshape(g.shape[0] * g.shape[1], g.shape[2]) for (key, _, _), g in zip(by_sub[s], landed[0])}
        if s == 0:
            dw_all = landed[1][0]
            full = jnp.transpose(dw_all[:, :n_conv * CONV_WIDTH].reshape(N_DEV, n_conv, CONV_WIDTH, cb), (1, 2, 0, 3))
            full = jnp.pad(full.reshape(n_conv, CONV_WIDTH, N_DEV * cb), ((0, 0), (0, CONV_HALO - CONV_WIDTH), (0, 0)))
            dw_filters.extend(full[layer] for layer in range(n_conv))
        if "w1t" in out:
            out["wdw"] = dw_filters[s // 4]
        return out

    scatter, dw_grads = {}, {}

    def emit(s, gd, carry):
        parts = [gd[key].reshape(N_DEV, -1, gd[key].shape[1]) for key, _, _ in by_sub[s]]
        if "wdw" in gd:
            dw_grads[s // 4] = gd["wdw"]
        if s == 0:
            parts.append(_dw_blocks(jnp.stack([dw_grads[layer] for layer in range(n_conv)])))
        lands = [landing(p[0], lax.dynamic_index_in_dim(p, me, 0, keepdims=False)) for p in parts]
        groups = [(parts[:len(by_sub[s])], lands[:len(by_sub[s])])]
        if s == 0:
            groups.append((parts[-1:], lands[-1:]))
        scatter[s], carry = _exchange_start(f"scatter_start{s}", groups, scatter=True, carry=carry)
        return carry

    sm = {n: w[n] for n in SMALL}
    loss_cols, dx, gsm = _local_step(x[0], loss_target[0], sm, depth, fetch, emit)
    loss = lax.psum(jnp.sum(loss_cols), ("x", "y", "c"))

    order = sorted(scatter, reverse=True)
    pieces = [(scatter[s][0], likes[s]) for s in order] + [(scatter[0][1], like_dw)]
    landed = _exchange_wait("scatter_wait", pieces, dx)
    pack = _pack_small(gsm)
    ((pack_all,),) = _exchange("gather_small_grads", [([pack], pack)], scatter=False)

    grads = {}
    summed = {}
    for s, recv in zip(order, landed):
        for (key, layer, _), r in zip(by_sub[s], recv):
            summed[key, layer] = _sum8(f"sum_{key}{layer}", r)
    for name, key, cols in SHARDED:
        g = jnp.stack([summed[key, layer] for layer in range(w[name].shape[0])])
        grads[name] = jnp.swapaxes(g, 1, 2) if cols else g
    grads["conv_w_dw"] = _sum8("sum_wdw", landed[-1][0])[:n_conv * CONV_WIDTH].reshape(conv_w_dw.shape)
    pack_sum = _sum8("sum_small", pack_all)
    grads.update(_unpack_small(pack_sum, sm))

    delta, new_m, new_v = {}, {}, {}
    for name in [n for n, _, _ in SHARDED] + ["conv_w_dw"]:
        shape = w[name].shape
        res = _adamw(f"adamw_{name}", *[t.reshape(-1, shape[-1]) for t in (grads[name], w[name], m[name], v[name])])
        delta[name], new_m[name], new_v[name] = (t.reshape(shape) for t in res)
    res = _adamw("adamw_small", pack_sum, _pack_small(sm), _pack_small({n: m[n] for n in SMALL}),
                 _pack_small({n: v[n] for n in SMALL}))
    for dst, t in zip((delta, new_m, new_v), res):
        dst.update(_unpack_small(t, sm))

    outs = [loss, dx[None]]
    for d in (grads, delta, new_m, new_v):
        outs += [d[n] for n in ORDER]
    return tuple(outs)
```

```python
import functools
import math

import numpy as np
import jax
import jax.numpy as jnp
from jax import lax
from jax.experimental import pallas as pl
from jax.experimental.pallas import tpu as pltpu

F32 = jnp.float32
BF16 = jnp.bfloat16

N_DEV = 8
HEAD_DIM = 64
HEADS_PER_GROUP = 4
GROUP_COLS = HEADS_PER_GROUP * HEAD_DIM
DILATIONS = (1, 4, 16)
N_BACK = 128
N_GROUPS = 3
N_HEADS = 12
D_ATTN = 768
N_BUCKETS = 32
REL_MAX_DISTANCE = 2048
CONV_WIDTH = 31
CONV_HALO = 32
EPS = 1e-6
NEG_INF = -1e30
ADAM_LR, ADAM_B1, ADAM_B2, ADAM_EPS, ADAM_WD, ADAM_STEP = 0.001, 0.9, 0.999, 1e-08, 0.01, 10
V7X_VMEM_LIMIT_BYTES = 56 * 1024 * 1024
MESH = pl.DeviceIdType.MESH
ANY = pl.BlockSpec(memory_space=pl.ANY)


def _pick(n, prefs):
    for p in prefs:
        if n % p == 0:
            return p
    return n


def _params(sem, vmem=None):
    return pltpu.CompilerParams(dimension_semantics=sem, vmem_limit_bytes=vmem)


def _sigmoid(x):
    return 1.0 / (1.0 + jnp.exp(-x))


def _exchange(name, groups, scatter):
    n_arr = [len(arrs) for arrs, _ in groups]
    n_in = sum(n_arr) + len(groups)
    ng = len(groups)

    def body(*refs):
        ins, outs, (send_sems, recv_sems, local_sems) = refs[:n_in], refs[n_in:-3], refs[-3:]
        x, y, c = lax.axis_index("x"), lax.axis_index("y"), lax.axis_index("c")
        me = 4 * x + 2 * y + c
        pos_in = pos_out = 0
        plans = []
        for gi in range(ng):
            srcs = ins[pos_in:pos_in + n_arr[gi]]
            like = ins[pos_in + n_arr[gi]]
            dsts = outs[pos_out:pos_out + n_arr[gi]]
            pos_in += n_arr[gi] + 1
            pos_out += n_arr[gi]
            plans.append((gi, srcs, like, dsts))
        local = []
        for gi, srcs, like, dsts in plans:
            for s, d in zip(srcs, dsts):
                cp = pltpu.make_async_copy(s.at[me] if scatter else s, d.at[me], local_sems.at[gi])
                cp.start()
                local.append(cp)
        for delta in range(1, N_DEV):
            dx, dy, dc = (delta >> 2) & 1, (delta >> 1) & 1, delta & 1
            px, py, pc = (1 - x if dx else x), (1 - y if dy else y), (1 - c if dc else c)
            peer = 4 * px + 2 * py + pc
            for gi, srcs, like, dsts in plans:
                for s, d in zip(srcs, dsts):
                    pltpu.make_async_remote_copy(
                        src_ref=s.at[peer] if scatter else s, dst_ref=d.at[me],
                        send_sem=send_sems.at[gi, delta - 1], recv_sem=recv_sems.at[gi, delta - 1],
                        device_id=(px, py, pc), device_id_type=MESH).start()
        for delta in range(1, N_DEV):
            for gi, srcs, like, dsts in plans:
                pltpu.make_async_remote_copy(
                    src_ref=like, dst_ref=like, send_sem=send_sems.at[gi, delta - 1],
                    recv_sem=recv_sems.at[gi, delta - 1], device_id=(x, y, c), device_id_type=MESH).wait()
        for cp in local:
            cp.wait()

    operands, out_shape = [], []
    for arrs, like in groups:
        operands += list(arrs) + [like]
        for a in arrs:
            blk = a.shape[1:] if scatter else a.shape
            out_shape.append(jax.ShapeDtypeStruct((N_DEV,) + tuple(blk), a.dtype))
    outs = pl.pallas_call(
        body, name=name, out_shape=out_shape, in_specs=[ANY] * len(operands), out_specs=[ANY] * len(out_shape),
        scratch_shapes=[pltpu.SemaphoreType.DMA((ng, N_DEV - 1)), pltpu.SemaphoreType.DMA((ng, N_DEV - 1)),
                        pltpu.SemaphoreType.DMA((ng,))],
        compiler_params=pltpu.CompilerParams(has_side_effects=True),
    )(*operands)
    res, pos = [], 0
    for n in n_arr:
        res.append(list(outs[pos:pos + n]))
        pos += n
    return res


HBM = pl.BlockSpec(memory_space=pltpu.HBM)
SEM = pl.BlockSpec(memory_space=pltpu.SEMAPHORE)
EFFECT = pltpu.SideEffectType.DATAFLOW_SIDE_EFFECTING


def _in_hbm(a):
    return pltpu.with_memory_space_constraint(a, pltpu.HBM)


def _exchange_start(name, groups, scatter, carry):
    ns = [len(s) for s, _ in groups]
    n_in = 2 * sum(ns)

    def body(*refs):
        ins, outs = refs[:n_in], refs[n_in + 1:]
        x, y, c = lax.axis_index("x"), lax.axis_index("y"), lax.axis_index("c")
        me = 4 * x + 2 * y + c
        pi = po = 0
        for n in ns:
            srcs, lands = ins[pi:pi + n], ins[pi + n:pi + 2 * n]
            send_sems, recv_sems = outs[po], outs[po + 1]
            pi += 2 * n
            po += 2 + 2 * n
            for delta in range(1, N_DEV):
                dx, dy, dc = (delta >> 2) & 1, (delta >> 1) & 1, delta & 1
                px, py, pc = (1 - x if dx else x), (1 - y if dy else y), (1 - c if dc else c)
                peer = 4 * px + 2 * py + pc
                for s, d in zip(srcs, lands):
                    pltpu.make_async_remote_copy(
                        src_ref=s.at[peer] if scatter else s, dst_ref=d.at[me], send_sem=send_sems.at[delta - 1],
                        recv_sem=recv_sems.at[delta - 1], device_id=(px, py, pc), device_id_type=MESH).start()

    operands, out_shape, out_specs, aliases = [], [], [], {}
    for srcs, lands in groups:
        out_shape += [pltpu.SemaphoreType.DMA((N_DEV - 1,))] * 2
        out_specs += [SEM, SEM]
        for a in list(srcs) + list(lands):
            aliases[len(operands)] = len(out_shape)
            operands.append(_in_hbm(a))
            out_shape.append(pltpu.HBM(a.shape, a.dtype))
            out_specs.append(HBM)
    aliases[len(operands)] = len(out_shape)
    operands.append(_in_hbm(carry))
    out_shape.append(pltpu.HBM(carry.shape, carry.dtype))
    out_specs.append(HBM)
    outs = pl.pallas_call(
        body, name=name, out_shape=out_shape, in_specs=[HBM] * len(operands), out_specs=out_specs,
        input_output_aliases=aliases, compiler_params=pltpu.CompilerParams(has_side_effects=EFFECT),
    )(*operands)
    handles, po = [], 0
    for n in ns:
        handles.append((outs[po], outs[po + 1], list(outs[po + 2:po + 2 + n]), list(outs[po + 2 + n:po + 2 + 2 * n])))
        po += 2 + 2 * n
    return handles, outs[-1]


def _exchange_wait(name, pieces, after):
    ns = [len(h[2]) for h, _ in pieces]

    def body(*refs):
        x, y, c = lax.axis_index("x"), lax.axis_index("y"), lax.axis_index("c")
        pi = 0
        for n in ns:
            send_sems, recv_sems, like = refs[pi + 2 * n], refs[pi + 2 * n + 1], refs[pi + 2 * n + 2]
            pi += 2 * n + 3
            for delta in range(1, N_DEV):
                cp = pltpu.make_async_remote_copy(
                    src_ref=like, dst_ref=like, send_sem=send_sems.at[delta - 1], recv_sem=recv_sems.at[delta - 1],
                    device_id=(x, y, c), device_id_type=MESH)
                cp.wait_send()
                cp.wait_recv()

    operands, in_specs, out_shape, aliases = [], [], [], {}
    for (send_sems, recv_sems, srcs, lands), like in pieces:
        for a in srcs + lands:
            aliases[len(operands)] = len(out_shape)
            operands.append(a)
            in_specs.append(HBM)
            out_shape.append(pltpu.HBM(a.shape, a.dtype))
        operands += [send_sems, recv_sems, like]
        in_specs += [SEM, SEM, ANY]
    operands.append(after)
    in_specs.append(ANY)
    outs = pl.pallas_call(
        body, name=name, out_shape=out_shape, in_specs=in_specs, out_specs=[HBM] * len(out_shape),
        input_output_aliases=aliases, compiler_params=pltpu.CompilerParams(has_side_effects=EFFECT),
    )(*operands)
    res, po = [], 0
    for n in ns:
        res.append(list(outs[po + n:po + 2 * n]))
        po += 2 * n
    return res


def _sum8(name, r):
    _, rows, cols = r.shape
    tr = _pick(rows, (256, 128, 64, 32, 16, 8))

    def body(r_ref, o_ref):
        acc = r_ref[0].astype(F32)
        for p in range(1, N_DEV):
            acc = acc + r_ref[p].astype(F32)
        o_ref[...] = acc

    return pl.pallas_call(
        body, name=name, out_shape=jax.ShapeDtypeStruct((rows, cols), F32), grid=(rows // tr,),
        in_specs=[pl.BlockSpec((N_DEV, tr, cols), lambda i: (0, i, 0))],
        out_specs=pl.BlockSpec((tr, cols), lambda i: (i, 0)), compiler_params=_params(("parallel",)),
    )(r)


def _adamw(name, g, w, m, v):
    rows, cols = g.shape
    tr = _pick(rows, (512, 256, 128, 64, 32, 16, 8))
    c1 = 1.0 - ADAM_B1 ** ADAM_STEP
    c2 = 1.0 - ADAM_B2 ** ADAM_STEP

    def body(g_ref, w_ref, m_ref, v_ref, d_ref, nm_ref, nv_ref):
        gv = g_ref[...]
        nm = ADAM_B1 * m_ref[...] + (1.0 - ADAM_B1) * gv
        nv = ADAM_B2 * v_ref[...] + (1.0 - ADAM_B2) * (gv * gv)
        d_ref[...] = -ADAM_LR * ((nm / c1) / (jnp.sqrt(nv / c2) + ADAM_EPS) + ADAM_WD * w_ref[...])
        nm_ref[...] = nm
        nv_ref[...] = nv

    spec = pl.BlockSpec((tr, cols), lambda i: (i, 0))
    return pl.pallas_call(
        body, name=name, out_shape=[jax.ShapeDtypeStruct((rows, cols), F32)] * 3, grid=(rows // tr,),
        in_specs=[spec] * 4, out_specs=[spec] * 3, compiler_params=_params(("parallel",)),
    )(g, w, m, v)


def _mm_nt(name, a, ws, w_offs, n, epi, out_dtypes, extras=(), rows=(), tm=None, tn=None, chunk=None):
    m, k = a.shape
    tm = tm or _pick(m, (512, 256, 128))
    tn = tn or _pick(n, (1408, 1152, 1024, 512, 256, 128))
    chunk = chunk or tn
    nw, ne, nr = len(ws), len(extras), len(rows)

    def body(*refs):
        a_ref, w_refs = refs[0], refs[1:1 + nw]
        e_refs, r_refs = refs[1 + nw:1 + nw + ne], refs[1 + nw + ne:1 + nw + ne + nr]
        o_refs = refs[1 + nw + ne + nr:]
        av = a_ref[...].astype(BF16)
        for c0 in range(0, tn, chunk):
            cs = slice(c0, c0 + chunk)
            accs = [lax.dot_general(av, w[cs, :], (((1,), (1,)), ((), ())), preferred_element_type=F32)
                    for w in w_refs]
            outs = epi(accs, [e[:, cs] for e in e_refs], [r[:, cs] for r in r_refs])
            for o_ref, o in zip(o_refs, outs):
                o_ref[:, cs] = o.astype(o_ref.dtype)

    in_specs = [pl.BlockSpec((tm, k), lambda j, i: (i, 0))]
    in_specs += [pl.BlockSpec((tn, k), functools.partial(lambda j, i, off: (j + off, 0), off=off)) for off in w_offs]
    in_specs += [pl.BlockSpec((tm, tn), lambda j, i: (i, j))] * ne
    in_specs += [pl.BlockSpec((1, tn), lambda j, i: (0, j))] * nr
    return pl.pallas_call(
        body, name=name, out_shape=[jax.ShapeDtypeStruct((m, n), dt) for dt in out_dtypes],
        grid=(n // tn, m // tm), in_specs=in_specs,
        out_specs=[pl.BlockSpec((tm, tn), lambda j, i: (i, j))] * len(out_dtypes),
        compiler_params=_params(("parallel", "parallel"), V7X_VMEM_LIMIT_BYTES),
    )(a, *ws, *extras, *rows)


def _mm_nn(name, as_, bs, epi, out_dtype, extras=(), rows=(), tm=None):
    m, k = as_[0].shape
    n = bs[0].shape[1]
    tm = tm or _pick(m, (512, 256, 128))
    npair, ne, nr = len(as_), len(extras), len(rows)

    def body(*refs):
        a_refs, b_refs = refs[:npair], refs[npair:2 * npair]
        e_refs, r_refs = refs[2 * npair:2 * npair + ne], refs[2 * npair + ne:2 * npair + ne + nr]
        o_ref = refs[-1]
        acc = None
        for a_ref, b_ref in zip(a_refs, b_refs):
            p = jnp.dot(a_ref[...].astype(BF16), b_ref[...], preferred_element_type=F32)
            acc = p if acc is None else acc + p
        o_ref[...] = epi(acc, [e[...] for e in e_refs], [r[...] for r in r_refs]).astype(o_ref.dtype)

    in_specs = [pl.BlockSpec((tm, k), lambda i: (i, 0))] * npair
    in_specs += [pl.BlockSpec((k, n), lambda i: (0, 0))] * npair
    in_specs += [pl.BlockSpec((tm, n), lambda i: (i, 0))] * ne
    in_specs += [pl.BlockSpec((1, n), lambda i: (0, 0))] * nr
    return pl.pallas_call(
        body, name=name, out_shape=jax.ShapeDtypeStruct((m, n), out_dtype), grid=(m // tm,), in_specs=in_specs,
        out_specs=pl.BlockSpec((tm, n), lambda i: (i, 0)),
        compiler_params=_params(("parallel",), V7X_VMEM_LIMIT_BYTES),
    )(*as_, *bs, *extras, *rows)


def _mm_nn_rms_bwd(name, as_, bs, x, g, dx_out, tm=None):
    m, k = as_[0].shape
    n = bs[0].shape[1]
    tm = tm or _pick(m, (512, 256, 128))
    npair = len(as_)

    def body(*refs):
        a_refs, b_refs = refs[:npair], refs[npair:2 * npair]
        x_ref, g_ref, dxo_ref, dx_ref, dg_ref = refs[2 * npair:]
        dh = None
        for a_ref, b_ref in zip(a_refs, b_refs):
            p = jnp.dot(a_ref[...].astype(BF16), b_ref[...], preferred_element_type=F32)
            dh = p if dh is None else dh + p
        xv = x_ref[...]
        r = lax.rsqrt(jnp.mean(xv * xv, axis=-1, keepdims=True) + EPS)
        yv = xv * r
        dy = dh * g_ref[...]
        dx_ref[...] = dxo_ref[...] + r * (dy - yv * jnp.mean(dy * yv, axis=-1, keepdims=True))

        @pl.when(pl.program_id(0) == 0)
        def _():
            dg_ref[...] = jnp.zeros_like(dg_ref)

        dg_ref[...] += jnp.sum(dh * yv, axis=0, keepdims=True)

    big = pl.BlockSpec((tm, n), lambda i: (i, 0))
    row = pl.BlockSpec((1, n), lambda i: (0, 0))
    in_specs = [pl.BlockSpec((tm, k), lambda i: (i, 0))] * npair + [pl.BlockSpec((k, n), lambda i: (0, 0))] * npair
    return pl.pallas_call(
        body, name=name, out_shape=[jax.ShapeDtypeStruct((m, n), F32), jax.ShapeDtypeStruct((1, n), F32)],
        grid=(m // tm,), in_specs=in_specs + [big, row, big], out_specs=[big, row],
        compiler_params=_params(("arbitrary",), V7X_VMEM_LIMIT_BYTES),
    )(*as_, *bs, x, g, dx_out)


def _mm_tn(name, a, b, colsum_b=False, tm=None, tk=2048):
    t, ma = a.shape
    nb = b.shape[1]
    tm = tm or _pick(ma, (1408, 1152, 1024, 768, 512, 256, 128))
    tk = _pick(t, (tk, 256, 128))
    nk = t // tk

    def body(*refs):
        a_ref, b_ref, o_ref = refs[0], refs[1], refs[2]
        acc_ref = refs[-1]
        kk = pl.program_id(1)
        bv = b_ref[...]

        @pl.when(kk == 0)
        def _():
            acc_ref[...] = jnp.zeros_like(acc_ref)

        acc_ref[...] += lax.dot_general(a_ref[...].astype(BF16), bv.astype(BF16), (((0,), (0,)), ((), ())),
                                        preferred_element_type=F32)
        if colsum_b:
            s_ref = refs[3]

            @pl.when((kk == 0) & (pl.program_id(0) == 0))
            def _():
                s_ref[...] = jnp.zeros_like(s_ref)

            @pl.when(pl.program_id(0) == 0)
            def _():
                s_ref[...] += jnp.sum(bv.astype(F32), axis=0, keepdims=True)

        @pl.when(kk == nk - 1)
        def _():
            o_ref[...] = acc_ref[...].astype(o_ref.dtype)

    out_shape = [jax.ShapeDtypeStruct((ma, nb), BF16)]
    out_specs = [pl.BlockSpec((tm, nb), lambda i, kk: (i, 0))]
    if colsum_b:
        out_shape.append(jax.ShapeDtypeStruct((1, nb), F32))
        out_specs.append(pl.BlockSpec((1, nb), lambda i, kk: (0, 0)))
    res = pl.pallas_call(
        body, name=name, out_shape=out_shape, grid=(ma // tm, nk),
        in_specs=[pl.BlockSpec((tk, tm), lambda i, kk: (kk, i)), pl.BlockSpec((tk, nb), lambda i, kk: (kk, 0))],
        out_specs=out_specs, scratch_shapes=[pltpu.VMEM((tm, nb), F32)],
        compiler_params=_params(("arbitrary", "arbitrary"), V7X_VMEM_LIMIT_BYTES),
    )(a, b)
    return res if colsum_b else res[0]


def _rmsnorm_fwd(name, x, g):
    t, d = x.shape
    tr = _pick(t, (512, 256, 128))

    def body(x_ref, g_ref, h_ref):
        xv = x_ref[...]
        r = lax.rsqrt(jnp.mean(xv * xv, axis=-1, keepdims=True) + EPS)
        h_ref[...] = (xv * r * g_ref[...]).astype(BF16)

    return pl.pallas_call(
        body, name=name, out_shape=jax.ShapeDtypeStruct((t, d), BF16), grid=(t // tr,),
        in_specs=[pl.BlockSpec((tr, d), lambda i: (i, 0)), pl.BlockSpec((1, d), lambda i: (0, 0))],
        out_specs=pl.BlockSpec((tr, d), lambda i: (i, 0)), compiler_params=_params(("parallel",)),
    )(x, g)


def _loss_head(name, x, g, target):
    t, d = x.shape
    tr = _pick(t, (512, 256, 128))

    def body(x_ref, g_ref, t_ref, dx_ref, dg_ref, l_ref):
        xv = x_ref[...]
        r = lax.rsqrt(jnp.mean(xv * xv, axis=-1, keepdims=True) + EPS)
        yv = xv * r
        diff = yv * g_ref[...] - t_ref[...]
        dout = diff * (1.0 / d)
        dy = dout * g_ref[...]
        dx_ref[...] = r * (dy - yv * jnp.mean(dy * yv, axis=-1, keepdims=True))

        @pl.when(pl.program_id(0) == 0)
        def _():
            dg_ref[...] = jnp.zeros_like(dg_ref)
            l_ref[...] = jnp.zeros_like(l_ref)

        dg_ref[...] += jnp.sum(dout * yv, axis=0, keepdims=True)
        l_ref[...] += (0.5 / d) * jnp.sum(diff * diff, axis=0, keepdims=True)

    big = pl.BlockSpec((tr, d), lambda i: (i, 0))
    row = pl.BlockSpec((1, d), lambda i: (0, 0))
    return pl.pallas_call(
        body, name=name,
        out_shape=[jax.ShapeDtypeStruct((t, d), F32), jax.ShapeDtypeStruct((1, d), F32),
                   jax.ShapeDtypeStruct((1, d), F32)],
        grid=(t // tr,), in_specs=[big, row, big], out_specs=[big, row, row],
        compiler_params=_params(("arbitrary",)),
    )(x, g, target)


CONV_ROWS = 64
SUBLANES = 8


def _fill_window(win_ref, sh_ref, parts):
    rows = sh_ref.shape[2]
    for cb in range(win_ref.shape[0]):
        for r0, val in parts:
            win_ref[cb, r0:r0 + val.shape[0], :] = val[:, 128 * cb:128 * (cb + 1)]
        win_ref[cb, rows:rows + SUBLANES, :] = jnp.zeros((SUBLANES, 128), F32)
        for b in range(1, SUBLANES):
            sh_ref[b - 1, cb] = win_ref[cb, b:b + rows, :]


def _window_rows(win_ref, sh_ref, o, cb):
    b = o % SUBLANES
    if b == 0:
        return win_ref[cb, o:o + CONV_ROWS, :]
    return sh_ref[b - 1, cb, o - b:o - b + CONV_ROWS, :]


def _window_scratch(rows, c):
    return [pltpu.VMEM((c // 128, rows + SUBLANES, 128), F32), pltpu.VMEM((SUBLANES - 1, c // 128, rows, 128), F32)]


def _dwconv_fwd(name, glu, w_dw, b_dw, ln_g, ln_b):
    t, c = glu.shape
    tt = _pick(t, (256, 128))
    hb = tt // CONV_HALO

    def body(cur_ref, halo_ref, w_ref, b_ref, g_ref, be_ref, dw_ref, s_ref, win_ref, sh_ref):
        i = pl.program_id(0)
        halo = jnp.where(i > 0, halo_ref[...].astype(F32), 0.0)
        _fill_window(win_ref, sh_ref, [(0, halo), (CONV_HALO, cur_ref[...].astype(F32))])
        for r0 in range(0, tt, CONV_ROWS):
            for cb in range(c // 128):
                c0 = 128 * cb
                acc = jnp.zeros((CONV_ROWS, 128), F32) + b_ref[:, c0:c0 + 128]
                for k in range(CONV_WIDTH):
                    o = r0 + k + CONV_HALO - (CONV_WIDTH - 1)
                    acc = acc + w_ref[k:k + 1, c0:c0 + 128] * _window_rows(win_ref, sh_ref, o, cb)
                dw_ref[r0:r0 + CONV_ROWS, c0:c0 + 128] = acc
        u = dw_ref[...]
        mu = jnp.mean(u, axis=-1, keepdims=True)
        uc = u - mu
        rstd = lax.rsqrt(jnp.mean(uc * uc, axis=-1, keepdims=True) + EPS)
        z = uc * rstd * g_ref[...] + be_ref[...]
        s_ref[...] = (z * _sigmoid(z)).astype(BF16)

    big = pl.BlockSpec((tt, c), lambda i: (i, 0))
    row = pl.BlockSpec((1, c), lambda i: (0, 0))
    return pl.pallas_call(
        body, name=name, out_shape=[jax.ShapeDtypeStruct((t, c), F32), jax.ShapeDtypeStruct((t, c), BF16)],
        grid=(t // tt,),
        in_specs=[big, pl.BlockSpec((CONV_HALO, c), lambda i: (jnp.maximum(i * hb - 1, 0), 0)),
                  pl.BlockSpec((CONV_HALO, c), lambda i: (0, 0)), row, row, row],
        out_specs=[big, big], scratch_shapes=_window_scratch(tt + CONV_HALO, c),
        compiler_params=_params(("parallel",), V7X_VMEM_LIMIT_BYTES),
    )(glu, glu, w_dw, b_dw, ln_g, ln_b)


def _ln_silu_bwd(name, dx, w2, dw, ln_g, ln_b):
    t, c = dw.shape
    d_model = dx.shape[1]
    tr = _pick(t, (256, 128))

    def body(dx_ref, w_ref, dw_ref, g_ref, be_ref, o_ref, acc_ref):
        ds = lax.dot_general(dx_ref[...].astype(BF16), w_ref[...], (((1,), (1,)), ((), ())),
                             preferred_element_type=F32)
        u = dw_ref[...]
        mu = jnp.mean(u, axis=-1, keepdims=True)
        uc = u - mu
        rstd = lax.rsqrt(jnp.mean(uc * uc, axis=-1, keepdims=True) + EPS)
        xh = uc * rstd
        z = xh * g_ref[...] + be_ref[...]
        sg = _sigmoid(z)
        dz = ds * (sg * (1.0 + z * (1.0 - sg)))
        dxh = dz * g_ref[...]
        du = rstd * (dxh - jnp.mean(dxh, axis=-1, keepdims=True) - xh * jnp.mean(dxh * xh, axis=-1, keepdims=True))
        o_ref[...] = du

        @pl.when(pl.program_id(0) == 0)
        def _():
            acc_ref[...] = jnp.zeros_like(acc_ref)

        acc_ref[0:1, :] += jnp.sum(dz * xh, axis=0, keepdims=True)
        acc_ref[1:2, :] += jnp.sum(dz, axis=0, keepdims=True)
        acc_ref[2:3, :] += jnp.sum(du, axis=0, keepdims=True)

    big = pl.BlockSpec((tr, c), lambda i: (i, 0))
    row = pl.BlockSpec((1, c), lambda i: (0, 0))
    return pl.pallas_call(
        body, name=name, out_shape=[jax.ShapeDtypeStruct((t, c), F32), jax.ShapeDtypeStruct((8, c), F32)],
        grid=(t // tr,),
        in_specs=[pl.BlockSpec((tr, d_model), lambda i: (i, 0)), pl.BlockSpec((c, d_model), lambda i: (0, 0)),
                  big, row, row],
        out_specs=[big, pl.BlockSpec((8, c), lambda i: (0, 0))],
        compiler_params=_params(("arbitrary",), V7X_VMEM_LIMIT_BYTES),
    )(dx, w2, dw, ln_g, ln_b)


def _dwconv_bwd(name, ddw, a, gt, w_dw):
    t, c = ddw.shape
    tt = _pick(t, (256, 128))
    hb = tt // CONV_HALO
    last = t // tt - 1
    back = CONV_WIDTH - 1

    def body(d_ref, dn_ref, a_ref, ap_ref, g_ref, gp_ref, w_ref, du_ref, dwk_ref, db_ref,
             wd_ref, shd_ref, wg_ref, shg_ref, dg_ref, dwk8_ref):
        i = pl.program_id(0)
        _fill_window(wd_ref, shd_ref, [(0, d_ref[...]), (tt, jnp.where(i < last, dn_ref[...], 0.0))])
        glu_prev = ap_ref[...].astype(F32) * _sigmoid(gp_ref[...].astype(F32))
        av = a_ref[...].astype(F32)
        sg = _sigmoid(g_ref[...].astype(F32))
        _fill_window(wg_ref, shg_ref, [(0, jnp.where(i > 0, glu_prev, 0.0)), (CONV_HALO, av * sg)])

        @pl.when(i == 0)
        def _():
            dwk8_ref[...] = jnp.zeros_like(dwk8_ref)
            db_ref[...] = jnp.zeros_like(db_ref)

        for r0 in range(0, tt, CONV_ROWS):
            for cb in range(c // 128):
                c0 = 128 * cb
                dcur = wd_ref[cb, r0:r0 + CONV_ROWS, :]
                acc = jnp.zeros((CONV_ROWS, 128), F32)
                for k in range(CONV_WIDTH):
                    acc = acc + w_ref[k:k + 1, c0:c0 + 128] * _window_rows(wd_ref, shd_ref, r0 + back - k, cb)
                    p = dcur * _window_rows(wg_ref, shg_ref, r0 + k + CONV_HALO - back, cb)
                    s8 = p[0:SUBLANES]
                    for q in range(SUBLANES, CONV_ROWS, SUBLANES):
                        s8 = s8 + p[q:q + SUBLANES]
                    dwk8_ref[SUBLANES * k:SUBLANES * (k + 1), c0:c0 + 128] += s8
                dg_ref[r0:r0 + CONV_ROWS, c0:c0 + 128] = acc

        @pl.when(i == last)
        def _():
            for k in range(CONV_WIDTH):
                dwk_ref[k:k + 1, :] = jnp.sum(dwk8_ref[SUBLANES * k:SUBLANES * (k + 1), :], axis=0, keepdims=True)
            dwk_ref[CONV_WIDTH:, :] = jnp.zeros((CONV_HALO - CONV_WIDTH, c), F32)
        dglu = dg_ref[...]
        da = dglu * sg
        dgate = dglu * av * sg * (1.0 - sg)
        du_ref[:, 0:c] = da.astype(BF16)
        du_ref[:, c:] = dgate.astype(BF16)
        db_ref[:, 0:c] += jnp.sum(da, axis=0, keepdims=True)
        db_ref[:, c:] += jnp.sum(dgate, axis=0, keepdims=True)

    big = pl.BlockSpec((tt, c), lambda i: (i, 0))
    prev = pl.BlockSpec((CONV_HALO, c), lambda i: (jnp.maximum(i * hb - 1, 0), 0))
    nxt = pl.BlockSpec((CONV_HALO, c), lambda i: (jnp.minimum((i + 1) * hb, t // CONV_HALO - 1), 0))
    return pl.pallas_call(
        body, name=name,
        out_shape=[jax.ShapeDtypeStruct((t, 2 * c), BF16), jax.ShapeDtypeStruct((CONV_HALO, c), F32),
                   jax.ShapeDtypeStruct((1, 2 * c), F32)],
        grid=(t // tt,),
        in_specs=[big, nxt, big, prev, big, prev, pl.BlockSpec((CONV_HALO, c), lambda i: (0, 0))],
        out_specs=[pl.BlockSpec((tt, 2 * c), lambda i: (i, 0)), pl.BlockSpec((CONV_HALO, c), lambda i: (0, 0)),
                   pl.BlockSpec((1, 2 * c), lambda i: (0, 0))],
        scratch_shapes=_window_scratch(tt + CONV_HALO, c) + _window_scratch(tt + CONV_HALO, c)
        + [pltpu.VMEM((tt, c), F32), pltpu.VMEM((SUBLANES * CONV_HALO, c), F32)],
        compiler_params=_params(("arbitrary",), V7X_VMEM_LIMIT_BYTES),
    )(ddw, ddw, a, a, gt, gt, w_dw)


def _bucket_tables():
    i = np.arange(N_BACK)[:, None]
    j = np.arange(2 * N_BACK)[None, :]
    dist = i + N_BACK - j
    valid = (dist >= 0) & (dist <= N_BACK)
    max_exact = N_BUCKETS // 2
    out = []
    for d in DILATIONS:
        n = np.maximum(dist * d, 0)
        nf = np.maximum(n, 1).astype(np.float32)
        large = max_exact + (np.log(nf / np.float32(max_exact)) / np.float32(math.log(REL_MAX_DISTANCE / max_exact))
                             * np.float32(N_BUCKETS - max_exact)).astype(np.int32)
        large = np.minimum(large, N_BUCKETS - 1)
        out.append(np.where(valid, np.where(n < max_exact, n, large), -1))
    return np.stack(out).astype(np.int32)


def _bias_build(name, rel_bias, buckets):
    def body(tbl_ref, bk_ref, o_ref):
        g = pl.program_id(0)
        bk = bk_ref[0]
        for h in range(HEADS_PER_GROUP):
            acc = jnp.zeros(bk.shape, F32)
            for b in range(N_BUCKETS):
                acc = jnp.where(bk == b, tbl_ref[b, g * HEADS_PER_GROUP + h], acc)
            o_ref[h] = jnp.where(bk < 0, NEG_INF, acc)

    return pl.pallas_call(
        body, name=name, out_shape=jax.ShapeDtypeStruct((N_HEADS, N_BACK, 2 * N_BACK), F32), grid=(N_GROUPS,),
        in_specs=[pl.BlockSpec(memory_space=pltpu.SMEM), pl.BlockSpec((1, N_BACK, 2 * N_BACK), lambda g: (g, 0, 0))],
        out_specs=pl.BlockSpec((HEADS_PER_GROUP, N_BACK, 2 * N_BACK), lambda g: (g, 0, 0)),
        compiler_params=_params(("arbitrary",)),
    )(rel_bias, buckets)


def _bias_grad(name, dbs, buckets):
    nd = len(dbs)

    def body(*refs):
        bk = refs[nd][0]
        o_ref = refs[nd + 1]
        lane = lax.broadcasted_iota(jnp.int32, (1, 128), 1)
        db = [sum(r[h] for r in refs[:nd]) for h in range(HEADS_PER_GROUP)]
        for b in range(N_BUCKETS):
            row = jnp.zeros((1, 128), F32)
            for h in range(HEADS_PER_GROUP):
                s = jnp.sum(jnp.where(bk == b, db[h], 0.0), axis=0, keepdims=True)
                s = jnp.sum(s, axis=1, keepdims=True)
                row = jnp.where(lane // 32 == h, s, row)
            o_ref[0, b:b + 1, :] = row

    spec = pl.BlockSpec((HEADS_PER_GROUP, N_BACK, 2 * N_BACK), lambda g: (g, 0, 0))
    return pl.pallas_call(
        body, name=name, out_shape=jax.ShapeDtypeStruct((N_GROUPS, N_BUCKETS, 128), F32), grid=(N_GROUPS,),
        in_specs=[spec] * nd + [pl.BlockSpec((1, N_BACK, 2 * N_BACK), lambda g: (g, 0, 0))],
        out_specs=pl.BlockSpec((1, N_BUCKETS, 128), lambda g: (g, 0, 0)), compiler_params=_params(("arbitrary",)),
    )(*dbs, buckets)


def _head_cols(h):
    return slice(h * HEAD_DIM, (h + 1) * HEAD_DIM)


def _attn_fwd(name, qkv, bias, g):
    t = qkv.shape[0]
    d = DILATIONS[g]
    tq = t // d
    nblk = qkv.shape[1] // GROUP_COLS
    scale = HEAD_DIM ** -0.5

    def body(q_ref, kp_ref, kc_ref, vp_ref, vc_ref, b_ref, o_ref, l_ref):
        m2 = pl.program_id(1)
        col = lax.broadcasted_iota(jnp.int32, (N_BACK, 2 * N_BACK), 1)
        lane = lax.broadcasted_iota(jnp.int32, (N_BACK, 128), 1)
        for sub in range(2):
            rows = slice(N_BACK * sub, N_BACK * (sub + 1))
            lse_tile = jnp.zeros((N_BACK, 128), F32)
            outs = []
            for h in range(HEADS_PER_GROUP):
                hc = _head_cols(h)
                if sub == 0:
                    kk = jnp.concatenate([kp_ref[:, hc], kc_ref[0:N_BACK, hc]], axis=0)
                    vv = jnp.concatenate([vp_ref[:, hc], vc_ref[0:N_BACK, hc]], axis=0)
                else:
                    kk, vv = kc_ref[:, hc], vc_ref[:, hc]
                s = lax.dot_general(q_ref[rows, hc], kk, (((1,), (1,)), ((), ())), preferred_element_type=F32)
                s = s * scale + b_ref[h]
                if sub == 0:
                    s = jnp.where((col >= N_BACK) | (m2 > 0), s, NEG_INF)
                m = jnp.max(s, axis=-1, keepdims=True)
                p = jnp.exp(s - m)
                den = jnp.sum(p, axis=-1, keepdims=True)
                outs.append(jnp.dot(p.astype(BF16), vv, preferred_element_type=F32) / den)
                lse_tile = jnp.where(lane // 32 == h, m + jnp.log(den), lse_tile)
            o_ref[rows, :] = jnp.concatenate(outs, axis=1)
            l_ref[rows, :] = lse_tile

    def blk(part, prev):
        if prev:
            return pl.BlockSpec((N_BACK, GROUP_COLS), lambda r, n: (jnp.maximum(2 * n - 1, 0), r * nblk + part))
        return pl.BlockSpec((2 * N_BACK, GROUP_COLS), lambda r, n: (n, r * nblk + part))

    qv = qkv.reshape(tq, d * qkv.shape[1])
    o, l = pl.pallas_call(
        body, name=name,
        out_shape=[jax.ShapeDtypeStruct((tq, d * GROUP_COLS), F32), jax.ShapeDtypeStruct((tq, d * 128), F32)],
        grid=(d, tq // (2 * N_BACK)),
        in_specs=[blk(0, False), blk(1, True), blk(1, False), blk(2, True), blk(2, False),
                  pl.BlockSpec((HEADS_PER_GROUP, N_BACK, 2 * N_BACK), lambda r, n: (g, 0, 0))],
        out_specs=[pl.BlockSpec((2 * N_BACK, GROUP_COLS), lambda r, n: (n, r)),
                   pl.BlockSpec((2 * N_BACK, 128), lambda r, n: (n, r))],
        compiler_params=_params(("parallel", "parallel")),
    )(qv, qv, qv, qv, qv, bias)
    return o.reshape(t, GROUP_COLS), l.reshape(t, 128)


def _group_weights(l_refs, h):
    ls = [l_ref[:, 32 * h:32 * h + 1] for l_ref in l_refs]
    m = jnp.maximum(jnp.maximum(ls[0], ls[1]), ls[2])
    es = [jnp.exp(l - m) for l in ls]
    tot = es[0] + es[1] + es[2]
    return [e / tot for e in es]


def _attn_merge_out(name, os_, ls, wot, x):
    t, d_model = x.shape
    tr = _pick(t, (512, 256, 128))

    def body(o0, o1, o2, l0, l1, l2, w_ref, x_ref, om_ref, out_ref):
        o_refs = (o0, o1, o2)
        pieces = [[None] * HEADS_PER_GROUP for _ in range(N_GROUPS)]
        for h in range(HEADS_PER_GROUP):
            al = _group_weights((l0, l1, l2), h)
            for g in range(N_GROUPS):
                pieces[g][h] = o_refs[g][:, _head_cols(h)] * al[g]
        om = jnp.concatenate([p for row in pieces for p in row], axis=1).astype(BF16)
        om_ref[...] = om
        out_ref[...] = x_ref[...] + lax.dot_general(om, w_ref[...], (((1,), (1,)), ((), ())),
                                                    preferred_element_type=F32)

    so = pl.BlockSpec((tr, GROUP_COLS), lambda i: (i, 0))
    sl = pl.BlockSpec((tr, 128), lambda i: (i, 0))
    sx = pl.BlockSpec((tr, d_model), lambda i: (i, 0))
    return pl.pallas_call(
        body, name=name,
        out_shape=[jax.ShapeDtypeStruct((t, D_ATTN), BF16), jax.ShapeDtypeStruct((t, d_model), F32)],
        grid=(t // tr,), in_specs=[so] * 3 + [sl] * 3 + [pl.BlockSpec((d_model, D_ATTN), lambda i: (0, 0)), sx],
        out_specs=[pl.BlockSpec((tr, D_ATTN), lambda i: (i, 0)), sx],
        compiler_params=_params(("parallel",), V7X_VMEM_LIMIT_BYTES),
    )(*os_, *ls, wot, x)


def _attn_bwd_prep(name, dx, wot, os_, ls):
    t, d_model = dx.shape
    tr = _pick(t, (512, 256, 128))

    def body(dx_ref, w_ref, o0, o1, o2, l0, l1, l2, d0, d1, d2, c0, c1, c2):
        o_refs, d_refs, c_refs = (o0, o1, o2), (d0, d1, d2), (c0, c1, c2)
        d_out = jnp.dot(dx_ref[...].astype(BF16), w_ref[...], preferred_element_type=F32)
        lane = lax.broadcasted_iota(jnp.int32, (tr, 128), 1)
        dos = [[None] * HEADS_PER_GROUP for _ in range(N_GROUPS)]
        cs = [jnp.zeros((tr, 128), F32) for _ in range(N_GROUPS)]
        for h in range(HEADS_PER_GROUP):
            al = _group_weights((l0, l1, l2), h)
            tot = jnp.zeros((tr, 1), F32)
            for g in range(N_GROUPS):
                dv = d_out[:, g * GROUP_COLS + h * HEAD_DIM:g * GROUP_COLS + (h + 1) * HEAD_DIM]
                tot = tot + al[g] * jnp.sum(dv * o_refs[g][:, _head_cols(h)], axis=-1, keepdims=True)
                dos[g][h] = dv * al[g]
            for g in range(N_GROUPS):
                cs[g] = jnp.where(lane // 32 == h, -al[g] * tot, cs[g])
        for g in range(N_GROUPS):
            d_refs[g][...] = jnp.concatenate(dos[g], axis=1).astype(BF16)
            c_refs[g][...] = cs[g]

    so = pl.BlockSpec((tr, GROUP_COLS), lambda i: (i, 0))
    sl = pl.BlockSpec((tr, 128), lambda i: (i, 0))
    res = pl.pallas_call(
        body, name=name,
        out_shape=[jax.ShapeDtypeStruct((t, GROUP_COLS), BF16)] * 3 + [jax.ShapeDtypeStruct((t, 128), F32)] * 3,
        grid=(t // tr,),
        in_specs=[pl.BlockSpec((tr, d_model), lambda i: (i, 0)), pl.BlockSpec((d_model, D_ATTN), lambda i: (0, 0))]
        + [so] * 3 + [sl] * 3,
        out_specs=[so] * 3 + [sl] * 3, compiler_params=_params(("parallel",), V7X_VMEM_LIMIT_BYTES),
    )(dx, wot, *os_, *ls)
    return res[:3], res[3:]


def _attn_bwd(name, qkv, do, lse, cterm, bias, g):
    t = qkv.shape[0]
    d = DILATIONS[g]
    tq = t // d
    nb = tq // N_BACK
    nblk = qkv.shape[1] // GROUP_COLS
    scale = HEAD_DIM ** -0.5
    nt = (((1,), (1,)), ((), ()))
    tn = (((0,), (0,)), ((), ()))

    def body(qn, qx, kp, kn, vp, vn, don, dox, ln, lx, cn, cx, b_ref, dqkv_ref, db_ref):
        n = pl.program_id(1)
        has_prev = n > 0
        has_next = n < nb - 1

        @pl.when((n == 0) & (pl.program_id(0) == 0))
        def _():
            db_ref[...] = jnp.zeros_like(db_ref)

        row2 = lax.broadcasted_iota(jnp.int32, (2 * N_BACK, N_BACK), 0)
        on_ac = (row2 < N_BACK) | has_next
        dqs, dks, dvs = [], [], []
        for h in range(HEADS_PER_GROUP):
            hc = _head_cols(h)
            st = slice(32 * h, 32 * h + 1)
            b_prev, b_same = b_ref[h, :, 0:N_BACK], b_ref[h, :, N_BACK:]
            q0, k0, k1, v0, v1, d0 = qn[:, hc], kp[:, hc], kn[:, hc], vp[:, hc], vn[:, hc], don[:, hc]
            q_ac = jnp.concatenate([q0, qx[:, hc]], axis=0)
            d_ac = jnp.concatenate([d0, dox[:, hc]], axis=0)
            l_ac = jnp.concatenate([ln[:, st], lx[:, st]], axis=0)
            c_ac = jnp.concatenate([cn[:, st], cx[:, st]], axis=0)
            s_ac = (lax.dot_general(q_ac, k1, nt, preferred_element_type=F32) * scale
                    + jnp.concatenate([b_same, b_prev], axis=0))
            p_ac = jnp.where(on_ac, jnp.exp(s_ac - l_ac), 0.0)
            ds_ac = p_ac * (lax.dot_general(d_ac, v1, nt, preferred_element_type=F32) + c_ac)
            s_b = lax.dot_general(q0, k0, nt, preferred_element_type=F32) * scale + b_prev
            p_b = jnp.where(has_prev, jnp.exp(s_b - ln[:, st]), 0.0)
            ds_b = p_b * (lax.dot_general(d0, v0, nt, preferred_element_type=F32) + cn[:, st])
            ds_a = ds_ac[0:N_BACK]
            dqs.append(scale * jnp.dot(jnp.concatenate([ds_b, ds_a], axis=1).astype(BF16),
                                       jnp.concatenate([k0, k1], axis=0), preferred_element_type=F32))
            dks.append(scale * lax.dot_general(ds_ac.astype(BF16), q_ac, tn, preferred_element_type=F32))
            dvs.append(lax.dot_general(p_ac.astype(BF16), d_ac, tn, preferred_element_type=F32))
            db_ref[h, :, 0:N_BACK] += ds_b
            db_ref[h, :, N_BACK:] += ds_a
        dqkv_ref[...] = jnp.concatenate(dqs + dks + dvs, axis=1).astype(BF16)

    def rows(which):
        if which == "prev":
            return lambda n: jnp.maximum(n - 1, 0)
        if which == "next":
            return lambda n: jnp.minimum(n + 1, nb - 1)
        return lambda n: n

    def qkv_blk(part, which):
        f = rows(which)
        return pl.BlockSpec((N_BACK, GROUP_COLS), lambda r, n: (f(n), r * nblk + part))

    def grp_blk(width, which):
        f = rows(which)
        return pl.BlockSpec((N_BACK, width), lambda r, n: (f(n), r))

    qv = qkv.reshape(tq, d * qkv.shape[1])
    dov = do.reshape(tq, d * GROUP_COLS)
    lv = lse.reshape(tq, d * 128)
    cv = cterm.reshape(tq, d * 128)
    dqkv_g, db = pl.pallas_call(
        body, name=name,
        out_shape=[jax.ShapeDtypeStruct((tq, d * 3 * GROUP_COLS), BF16),
                   jax.ShapeDtypeStruct((HEADS_PER_GROUP, N_BACK, 2 * N_BACK), F32)],
        grid=(d, nb),
        in_specs=[qkv_blk(0, "same"), qkv_blk(0, "next"), qkv_blk(1, "prev"), qkv_blk(1, "same"),
                  qkv_blk(2, "prev"), qkv_blk(2, "same"), grp_blk(GROUP_COLS, "same"), grp_blk(GROUP_COLS, "next"),
                  grp_blk(128, "same"), grp_blk(128, "next"), grp_blk(128, "same"), grp_blk(128, "next"),
                  pl.BlockSpec((HEADS_PER_GROUP, N_BACK, 2 * N_BACK), lambda r, n: (g, 0, 0))],
        out_specs=[pl.BlockSpec((N_BACK, 3 * GROUP_COLS), lambda r, n: (n, r)),
                   pl.BlockSpec((HEADS_PER_GROUP, N_BACK, 2 * N_BACK), lambda r, n: (0, 0, 0))],
        compiler_params=_params(("arbitrary", "arbitrary")),
    )(qv, qv, qv, qv, qv, qv, dov, dov, lv, lv, cv, cv, bias)
    return dqkv_g.reshape(t, 3 * GROUP_COLS), db


def _row(v):
    return v.reshape(1, -1)


def _glu_epi(accs, extras, rows):
    a = (accs[0] + rows[0]).astype(BF16)
    gt = (accs[1] + rows[1]).astype(BF16)
    return a, gt, a.astype(F32) * _sigmoid(gt.astype(F32))


def _swiglu_epi(accs, extras, rows):
    gq, uq = accs[0].astype(BF16), accs[1].astype(BF16)
    gf = gq.astype(F32)
    return gq, uq, gf * _sigmoid(gf) * uq.astype(F32)


def _swiglu_bwd_epi(accs, extras, rows):
    gf, uf = extras[0].astype(F32), extras[1].astype(F32)
    sg = _sigmoid(gf)
    return accs[0] * uf * (sg * (1.0 + gf * (1.0 - sg))), accs[0] * gf * sg


def _residual(acc, extras, rows):
    out = acc + extras[0]
    return out + rows[0] if rows else out


def _group_rows(w):
    parts = [w[p * D_ATTN:(p + 1) * D_ATTN].reshape(N_GROUPS, GROUP_COLS, -1) for p in range(3)]
    return jnp.concatenate(parts, axis=1)


def _ungroup_rows(wg):
    return jnp.concatenate([wg[g][p * GROUP_COLS:(p + 1) * GROUP_COLS] for p in range(3) for g in range(N_GROUPS)],
                           axis=0)


def _local_step(x, target, sm, depth, fetch, emit):
    d_model = x.shape[1]
    buckets = jnp.asarray(_bucket_tables())
    bias = _bias_build("bias_build", sm["rel_bias"], buckets)
    saved = []
    for i in range(depth):
        j = i // 2
        rec = {"x_mix": x}
        wm = fetch(2 * i, x)
        h = _rmsnorm_fwd(f"rms_mix_fwd{i}", x, _row(sm["norm_mix"][i]))
        rec.update(h_mix=h, wm=wm)
        if i % 2 == 0:
            c = wm["w2"].shape[0]
            tn = _pick(c, (512, 256, 128))
            b1 = sm["conv_b_pw1"][j]
            a, gt, glu = _mm_nt(f"conv_pw1_fwd{j}", h, [wm["w1t"]] * 2, [0, c // tn], c, _glu_epi, (BF16,) * 3,
                                rows=[_row(b1[:c]), _row(b1[c:])], tn=tn)
            dw, s = _dwconv_fwd(f"dwconv_fwd{j}", glu, wm["wdw"], _row(sm["conv_b_dw"][j]),
                                _row(sm["conv_ln_g"][j]), _row(sm["conv_ln_b"][j]))
            x = _mm_nn(f"conv_pw2_fwd{j}", [s], [wm["w2"]], _residual, F32, extras=[x],
                       rows=[_row(sm["conv_b_pw2"][j])])
            rec.update(a=a, gt=gt, dw=dw, s=s)
        else:
            wq = _group_rows(wm["wqkvt"])
            qkv = _mm_nt(f"attn_qkv_fwd{j}", h, [wq[g] for g in range(N_GROUPS)], [0] * N_GROUPS, D_ATTN,
                         lambda accs, e, r: tuple(accs), (BF16,) * N_GROUPS, tn=D_ATTN)
            og = [_attn_fwd(f"attn_fwd{j}_{g}", qkv[g], bias, g) for g in range(N_GROUPS)]
            os_, ls = [o for o, _ in og], [l for _, l in og]
            om, x = _attn_merge_out(f"attn_out_fwd{j}", os_, ls, wm["wot"], x)
            rec.update(qkv=qkv, os=os_, ls=ls, om=om, wq=wq)
        rec["x_ffn"] = x
        wf = fetch(2 * i + 1, x)
        h2 = _rmsnorm_fwd(f"rms_ffn_fwd{i}", x, _row(sm["norm_ffn"][i]))
        f = wf["wd"].shape[0]
        gq, uq, act = _mm_nt(f"ffn_up_fwd{i}", h2, [wf["wgt"], wf["wut"]], [0, 0], f, _swiglu_epi, (BF16,) * 3)
        x = _mm_nn(f"ffn_down_fwd{i}", [act], [wf["wd"]], _residual, F32, extras=[x])
        rec.update(h_ffn=h2, gq=gq, uq=uq, act=act, wf=wf)
        saved.append(rec)

    dx, g_final, loss_cols = _loss_head("loss_head", x, _row(sm["final_norm"]), target)

    g_mix, g_ffn = [None] * depth, [None] * depth
    nconv = (depth + 1) // 2
    g_b1, g_bdw, g_lng, g_lnb, g_b2 = ([None] * nconv for _ in range(5))
    dbias = []
    for i in reversed(range(depth)):
        j = i // 2
        rec = saved[i]
        wm, wf = rec["wm"], rec["wf"]
        f = wf["wd"].shape[0]
        dgate, dup = _mm_nt(f"ffn_down_bwd{i}", dx, [wf["wd"]], [0], f, _swiglu_bwd_epi, (BF16, BF16),
                            extras=[rec["gq"], rec["uq"]], tn=f, chunk=_pick(f, (256, 128)))
        gf = {"wd": _mm_tn(f"ffn_down_dw{i}", rec["act"], dx),
              "wgt": _mm_tn(f"ffn_gate_dw{i}", dgate, rec["h_ffn"]),
              "wut": _mm_tn(f"ffn_up_dw{i}", dup, rec["h_ffn"])}
        dx, g_ffn[i] = _mm_nn_rms_bwd(f"ffn_up_bwd{i}", [dgate, dup], [wf["wgt"], wf["wut"]], rec["x_ffn"],
                                      _row(sm["norm_ffn"][i]), dx, tm=256)
        dx = emit(2 * i + 1, gf, dx)
        if i % 2 == 0:
            c = wm["w2"].shape[0]
            gm = {}
            gm["w2"], g_b2[j] = _mm_tn(f"conv_pw2_dw{j}", rec["s"], dx, colsum_b=True)
            ddw, sums = _ln_silu_bwd(f"conv_pw2_bwd{j}", dx, wm["w2"], rec["dw"], _row(sm["conv_ln_g"][j]),
                                     _row(sm["conv_ln_b"][j]))
            g_lng[j], g_lnb[j], g_bdw[j] = sums[0], sums[1], sums[2]
            du, dwk, db1 = _dwconv_bwd(f"dwconv_bwd{j}", ddw, rec["a"], rec["gt"], wm["wdw"])
            gm["wdw"] = dwk[:CONV_WIDTH]
            g_b1[j] = db1[0]
            gm["w1t"] = _mm_tn(f"conv_pw1_dw{j}", du, rec["h_mix"])
            dh_terms = ([du], [wm["w1t"]])
        else:
            gm = {"wot": _mm_tn(f"attn_out_dw{j}", dx, rec["om"])}
            dos, cs = _attn_bwd_prep(f"attn_out_bwd{j}", dx, wm["wot"], rec["os"], rec["ls"])
            back = [_attn_bwd(f"attn_bwd{j}_{g}", rec["qkv"][g], dos[g], rec["ls"][g], cs[g], bias, g)
                    for g in range(N_GROUPS)]
            dqkv = [b[0] for b in back]
            dbias.append(jnp.concatenate([b[1] for b in back], axis=0))
            dh_terms = (dqkv, [rec["wq"][g] for g in range(N_GROUPS)])
            gm["wqkvt"] = _ungroup_rows([_mm_tn(f"attn_qkv_dw{j}_{g}", dqkv[g], rec["h_mix"])
                                         for g in range(N_GROUPS)])
        dx, g_mix[i] = _mm_nn_rms_bwd(f"mix_in_bwd{i}", dh_terms[0], dh_terms[1], rec["x_mix"],
                                      _row(sm["norm_mix"][i]), dx)
        dx = emit(2 * i, gm, dx)

    gb = _bias_grad("bias_grad", dbias, buckets)
    g_rel = jnp.transpose(gb[:, :, ::32], (1, 0, 2)).reshape(N_BUCKETS, N_HEADS)
    gsm = {
        "norm_mix": jnp.concatenate(g_mix, axis=0), "norm_ffn": jnp.concatenate(g_ffn, axis=0),
        "final_norm": g_final[0], "conv_b_pw1": jnp.stack(g_b1), "conv_b_dw": jnp.stack(g_bdw),
        "conv_ln_g": jnp.stack(g_lng), "conv_ln_b": jnp.stack(g_lnb),
        "conv_b_pw2": jnp.concatenate(g_b2, axis=0), "rel_bias": g_rel,
    }
    return loss_cols, dx, gsm


SMALL = ("norm_mix", "norm_ffn", "final_norm", "conv_b_pw1", "conv_b_dw", "conv_ln_g", "conv_ln_b", "conv_b_pw2",
         "rel_bias")
SHARDED = (("conv_w_pw1", "w1t", True), ("conv_w_pw2", "w2", False), ("attn_w_qkv", "wqkvt", True),
           ("attn_w_o", "wot", True), ("ffn_w_gate", "wgt", True), ("ffn_w_up", "wut", True),
           ("ffn_w_down", "wd", False))
ORDER = ("norm_mix", "norm_ffn", "final_norm", "conv_w_pw1", "conv_b_pw1", "conv_w_dw", "conv_b_dw", "conv_ln_g",
         "conv_ln_b", "conv_w_pw2", "conv_b_pw2", "attn_w_qkv", "attn_w_o", "rel_bias", "ffn_w_gate", "ffn_w_up",
         "ffn_w_down")
PACK_LANES = 128
PACK_ROW_TILE = 8


def _pack_small(vals):
    flat = jnp.concatenate([vals[n].reshape(-1) for n in SMALL])
    per_tile = PACK_LANES * PACK_ROW_TILE
    return jnp.pad(flat, (0, -flat.shape[0] % per_tile)).reshape(-1, PACK_LANES)


def _unpack_small(pack, like):
    flat, out, pos = pack.reshape(-1), {}, 0
    for n in SMALL:
        out[n] = flat[pos:pos + like[n].size].reshape(like[n].shape)
        pos += like[n].size
    return out


def _dw_blocks(w):
    l, k, c = w.shape
    blk = jnp.transpose(w.reshape(l, k, N_DEV, c // N_DEV), (2, 0, 1, 3)).reshape(N_DEV, l * k, c // N_DEV)
    return jnp.pad(blk, ((0, 0), (0, -(l * k) % 8), (0, 0)))


def kernel(x, norm_mix, norm_ffn, final_norm, conv_w_pw1, conv_b_pw1, conv_w_dw, conv_b_dw, conv_ln_g, conv_ln_b, conv_w_pw2, conv_b_pw2, attn_w_qkv, attn_w_o, rel_bias, ffn_w_gate, ffn_w_up, ffn_w_down, loss_target, m_norm_mix, m_norm_ffn, m_final_norm, m_conv_w_pw1, m_conv_b_pw1, m_conv_w_dw, m_conv_b_dw, m_conv_ln_g, m_conv_ln_b, m_conv_w_pw2, m_conv_b_pw2, m_attn_w_qkv, m_attn_w_o, m_rel_bias, m_ffn_w_gate, m_ffn_w_up, m_ffn_w_down, v_norm_mix, v_norm_ffn, v_final_norm, v_conv_w_pw1, v_conv_b_pw1, v_conv_w_dw, v_conv_b_dw, v_conv_ln_g, v_conv_ln_b, v_conv_w_pw2, v_conv_b_pw2, v_attn_w_qkv, v_attn_w_o, v_rel_bias, v_ffn_w_gate, v_ffn_w_up, v_ffn_w_down):
    w = dict(norm_mix=norm_mix, norm_ffn=norm_ffn, final_norm=final_norm, conv_w_pw1=conv_w_pw1,
             conv_b_pw1=conv_b_pw1, conv_w_dw=conv_w_dw, conv_b_dw=conv_b_dw, conv_ln_g=conv_ln_g,
             conv_ln_b=conv_ln_b, conv_w_pw2=conv_w_pw2, conv_b_pw2=conv_b_pw2, attn_w_qkv=attn_w_qkv,
             attn_w_o=attn_w_o, rel_bias=rel_bias, ffn_w_gate=ffn_w_gate, ffn_w_up=ffn_w_up, ffn_w_down=ffn_w_down)
    m = dict(norm_mix=m_norm_mix, norm_ffn=m_norm_ffn, final_norm=m_final_norm, conv_w_pw1=m_conv_w_pw1,
             conv_b_pw1=m_conv_b_pw1, conv_w_dw=m_conv_w_dw, conv_b_dw=m_conv_b_dw, conv_ln_g=m_conv_ln_g,
             conv_ln_b=m_conv_ln_b, conv_w_pw2=m_conv_w_pw2, conv_b_pw2=m_conv_b_pw2, attn_w_qkv=m_attn_w_qkv,
             attn_w_o=m_attn_w_o, rel_bias=m_rel_bias, ffn_w_gate=m_ffn_w_gate, ffn_w_up=m_ffn_w_up,
             ffn_w_down=m_ffn_w_down)
    v = dict(norm_mix=v_norm_mix, norm_ffn=v_norm_ffn, final_norm=v_final_norm, conv_w_pw1=v_conv_w_pw1,
             conv_b_pw1=v_conv_b_pw1, conv_w_dw=v_conv_w_dw, conv_b_dw=v_conv_b_dw, conv_ln_g=v_conv_ln_g,
             conv_ln_b=v_conv_ln_b, conv_w_pw2=v_conv_w_pw2, conv_b_pw2=v_conv_b_pw2, attn_w_qkv=v_attn_w_qkv,
             attn_w_o=v_attn_w_o, rel_bias=v_rel_bias, ffn_w_gate=v_ffn_w_gate, ffn_w_up=v_ffn_w_up,
             ffn_w_down=v_ffn_w_down)

    me = 4 * lax.axis_index("x") + 2 * lax.axis_index("y") + lax.axis_index("c")
    depth = ffn_w_gate.shape[0]
    n_conv, _, cb = conv_w_dw.shape

    def sublayer(key, layer):
        if key in ("wgt", "wut", "wd"):
            return 2 * layer + 1
        return 4 * layer if key in ("w1t", "w2") else 4 * layer + 2

    def landing(block, own):
        land = lax.empty((N_DEV,) + block.shape, block.dtype)
        return lax.dynamic_update_slice(land, own[None], (me,) + (0,) * block.ndim)

    by_sub = {s: [] for s in range(2 * depth)}
    for name, key, cols in SHARDED:
        sw = (jnp.swapaxes(w[name], 1, 2) if cols else w[name]).astype(BF16)
        for layer in range(sw.shape[0]):
            by_sub[sublayer(key, layer)].append((key, layer, sw[layer]))
    likes = {s: jnp.zeros((sum(sh.size for _, _, sh in by_sub[s]) // 1024, 1024), BF16) for s in by_sub}
    dw_shard = jnp.pad(conv_w_dw.reshape(-1, cb), ((0, -(n_conv * CONV_WIDTH) % 8), (0, 0)))
    like_dw = jnp.zeros(dw_shard.shape, F32)
    groups = [([dw_shard], [landing(dw_shard, dw_shard)])]
    for s in range(2 * depth):
        groups.append(([sh for _, _, sh in by_sub[s]], [landing(sh, sh) for _, _, sh in by_sub[s]]))
    gather, _ = _exchange_start("gather_start", groups, scatter=False, carry=jnp.zeros((8, 128), F32))
    dw_filters = []

    def fetch(s, after):
        pieces = [(gather[s + 1], likes[s])]
        if s == 0:
            pieces.append((gather[0], like_dw))
        landed = _exchange_wait(f"gather_wait{s}", pieces, after)
        out = {key: g.reshape(g.shape[0] * g.shape[1], g.shape[2]) for (key, _, _), g in zip(by_sub[s], landed[0])}
        if s == 0:
            dw_all = landed[1][0]
            full = jnp.transpose(dw_all[:, :n_conv * CONV_WIDTH].reshape(N_DEV, n_conv, CONV_WIDTH, cb), (1, 2, 0, 3))
            full = jnp.pad(full.reshape(n_conv, CONV_WIDTH, N_DEV * cb), ((0, 0), (0, CONV_HALO - CONV_WIDTH), (0, 0)))
            dw_filters.extend(full[layer] for layer in range(n_conv))
        if "w1t" in out:
            out["wdw"] = dw_filters[s // 4]
        return out

    scatter, dw_grads = {}, {}

    def emit(s, gd, carry):
        parts = [gd[key].reshape(N_DEV, -1, gd[key].shape[1]) for key, _, _ in by_sub[s]]
        if "wdw" in gd:
            dw_grads[s // 4] = gd["wdw"]
        if s == 0:
            parts.append(_dw_blocks(jnp.stack([dw_grads[layer] for layer in range(n_conv)])))
        lands = [landing(p[0], lax.dynamic_index_in_dim(p, me, 0, keepdims=False)) for p in parts]
        groups = [(parts[:len(by_sub[s])], lands[:len(by_sub[s])])]
        if s == 0:
            groups.append((parts[-1:], lands[-1:]))
        scatter[s], carry = _exchange_start(f"scatter_start{s}", groups, scatter=True, carry=carry)
        return carry

    sm = {n: w[n] for n in SMALL}
    loss_cols, dx, gsm = _local_step(x[0], loss_target[0], sm, depth, fetch, emit)
    loss = lax.psum(jnp.sum(loss_cols), ("x", "y", "c"))

    order = sorted(scatter, reverse=True)
    pieces = [(scatter[s][0], likes[s]) for s in order] + [(scatter[0][1], like_dw)]
    landed = _exchange_wait("scatter_wait", pieces, dx)
    pack = _pack_small(gsm)
    ((pack_all,),) = _exchange("gather_small_grads", [([pack], pack)], scatter=False)

    grads = {}
    summed = {}
    for s, recv in zip(order, landed):
        for (key, layer, _), r in zip(by_sub[s], recv):
            summed[key, layer] = _sum8(f"sum_{key}{layer}", r)
    for name, key, cols in SHARDED:
        g = jnp.stack([summed[key, layer] for layer in range(w[name].shape[0])])
        grads[name] = jnp.swapaxes(g, 1, 2) if cols else g
    grads["conv_w_dw"] = _sum8("sum_wdw", landed[-1][0])[:n_conv * CONV_WIDTH].reshape(conv_w_dw.shape)
    pack_sum = _sum8("sum_small", pack_all)
    grads.update(_unpack_small(pack_sum, sm))

    delta, new_m, new_v = {}, {}, {}
    for name in [n for n, _, _ in SHARDED] + ["conv_w_dw"]:
        shape = w[name].shape
        res = _adamw(f"adamw_{name}", *[t.reshape(-1, shape[-1]) for t in (grads[name], w[name], m[name], v[name])])
        delta[name], new_m[name], new_v[name] = (t.reshape(shape) for t in res)
    res = _adamw("adamw_small", pack_sum, _pack_small(sm), _pack_small({n: m[n] for n in SMALL}),
                 _pack_small({n: v[n] for n in SMALL}))
    for dst, t in zip((delta, new_m, new_v), res):
        dst.update(_unpack_small(t, sm))

    outs = [loss, dx[None]]
    for d in (grads, delta, new_m, new_v):
        outs += [d[n] for n in ORDER]
    return tuple(outs)
```

```python
import functools
import math

import numpy as np
import jax
import jax.numpy as jnp
from jax import lax
from jax.experimental import pallas as pl
from jax.experimental.pallas import tpu as pltpu

F32 = jnp.float32
BF16 = jnp.bfloat16

N_DEV = 8
HEAD_DIM = 64
HEADS_PER_GROUP = 4
GROUP_COLS = HEADS_PER_GROUP * HEAD_DIM
DILATIONS = (1, 4, 16)
N_BACK = 128
N_GROUPS = 3
N_HEADS = 12
D_ATTN = 768
N_BUCKETS = 32
REL_MAX_DISTANCE = 2048
CONV_WIDTH = 31
CONV_HALO = 32
EPS = 1e-6
NEG_INF = -1e30
ADAM_LR, ADAM_B1, ADAM_B2, ADAM_EPS, ADAM_WD, ADAM_STEP = 0.001, 0.9, 0.999, 1e-08, 0.01, 10
V7X_VMEM_LIMIT_BYTES = 56 * 1024 * 1024
MESH = pl.DeviceIdType.MESH
ANY = pl.BlockSpec(memory_space=pl.ANY)


def _pick(n, prefs):
    for p in prefs:
        if n % p == 0:
            return p
    return n


def _params(sem, vmem=None):
    return pltpu.CompilerParams(dimension_semantics=sem, vmem_limit_bytes=vmem)


def _sigmoid(x):
    return 1.0 / (1.0 + jnp.exp(-x))


def _exchange(name, groups, scatter):
    n_arr = [len(arrs) for arrs, _ in groups]
    n_in = sum(n_arr) + len(groups)
    ng = len(groups)

    def body(*refs):
        ins, outs, (send_sems, recv_sems, local_sems) = refs[:n_in], refs[n_in:-3], refs[-3:]
        x, y, c = lax.axis_index("x"), lax.axis_index("y"), lax.axis_index("c")
        me = 4 * x + 2 * y + c
        pos_in = pos_out = 0
        plans = []
        for gi in range(ng):
            srcs = ins[pos_in:pos_in + n_arr[gi]]
            like = ins[pos_in + n_arr[gi]]
            dsts = outs[pos_out:pos_out + n_arr[gi]]
            pos_in += n_arr[gi] + 1
            pos_out += n_arr[gi]
            plans.append((gi, srcs, like, dsts))
        local = []
        for gi, srcs, like, dsts in plans:
            for s, d in zip(srcs, dsts):
                cp = pltpu.make_async_copy(s.at[me] if scatter else s, d.at[me], local_sems.at[gi])
                cp.start()
                local.append(cp)
        for delta in range(1, N_DEV):
            dx, dy, dc = (delta >> 2) & 1, (delta >> 1) & 1, delta & 1
            px, py, pc = (1 - x if dx else x), (1 - y if dy else y), (1 - c if dc else c)
            peer = 4 * px + 2 * py + pc
            for gi, srcs, like, dsts in plans:
                for s, d in zip(srcs, dsts):
                    pltpu.make_async_remote_copy(
                        src_ref=s.at[peer] if scatter else s, dst_ref=d.at[me],
                        send_sem=send_sems.at[gi, delta - 1], recv_sem=recv_sems.at[gi, delta - 1],
                        device_id=(px, py, pc), device_id_type=MESH).start()
        for delta in range(1, N_DEV):
            for gi, srcs, like, dsts in plans:
                pltpu.make_async_remote_copy(
                    src_ref=like, dst_ref=like, send_sem=send_sems.at[gi, delta - 1],
                    recv_sem=recv_sems.at[gi, delta - 1], device_id=(x, y, c), device_id_type=MESH).wait()
        for cp in local:
            cp.wait()

    operands, out_shape = [], []
    for arrs, like in groups:
        operands += list(arrs) + [like]
        for a in arrs:
            blk = a.shape[1:] if scatter else a.shape
            out_shape.append(jax.ShapeDtypeStruct((N_DEV,) + tuple(blk), a.dtype))
    outs = pl.pallas_call(
        body, name=name, out_shape=out_shape, in_specs=[ANY] * len(operands), out_specs=[ANY] * len(out_shape),
        scratch_shapes=[pltpu.SemaphoreType.DMA((ng, N_DEV - 1)), pltpu.SemaphoreType.DMA((ng, N_DEV - 1)),
                        pltpu.SemaphoreType.DMA((ng,))],
        compiler_params=pltpu.CompilerParams(has_side_effects=True),
    )(*operands)
    res, pos = [], 0
    for n in n_arr:
        res.append(list(outs[pos:pos + n]))
        pos += n
    return res


HBM = pl.BlockSpec(memory_space=pltpu.HBM)
SEM = pl.BlockSpec(memory_space=pltpu.SEMAPHORE)
EFFECT = pltpu.SideEffectType.DATAFLOW_SIDE_EFFECTING


def _in_hbm(a):
    return pltpu.with_memory_space_constraint(a, pltpu.HBM)


def _exchange_start(name, groups, scatter, carry):
    ns = [len(s) for s, _ in groups]
    n_in = 2 * sum(ns)

    def body(*refs):
        ins, outs = refs[:n_in], refs[n_in + 1:]
        x, y, c = lax.axis_index("x"), lax.axis_index("y"), lax.axis_index("c")
        me = 4 * x + 2 * y + c
        pi = po = 0
        for n in ns:
            srcs, lands = ins[pi:pi + n], ins[pi + n:pi + 2 * n]
            send_sems, recv_sems = outs[po], outs[po + 1]
            pi += 2 * n
            po += 2 + 2 * n
            for delta in range(1, N_DEV):
                dx, dy, dc = (delta >> 2) & 1, (delta >> 1) & 1, delta & 1
                px, py, pc = (1 - x if dx else x), (1 - y if dy else y), (1 - c if dc else c)
                peer = 4 * px + 2 * py + pc
                for s, d in zip(srcs, lands):
                    pltpu.make_async_remote_copy(
                        src_ref=s.at[peer] if scatter else s, dst_ref=d.at[me], send_sem=send_sems.at[delta - 1],
                        recv_sem=recv_sems.at[delta - 1], device_id=(px, py, pc), device_id_type=MESH).start()

    operands, out_shape, out_specs, aliases = [], [], [], {}
    for srcs, lands in groups:
        out_shape += [pltpu.SemaphoreType.DMA((N_DEV - 1,))] * 2
        out_specs += [SEM, SEM]
        for a in list(srcs) + list(lands):
            aliases[len(operands)] = len(out_shape)
            operands.append(_in_hbm(a))
            out_shape.append(pltpu.HBM(a.shape, a.dtype))
            out_specs.append(HBM)
    aliases[len(operands)] = len(out_shape)
    operands.append(_in_hbm(carry))
    out_shape.append(pltpu.HBM(carry.shape, carry.dtype))
    out_specs.append(HBM)
    outs = pl.pallas_call(
        body, name=name, out_shape=out_shape, in_specs=[HBM] * len(operands), out_specs=out_specs,
        input_output_aliases=aliases, compiler_params=pltpu.CompilerParams(has_side_effects=EFFECT),
    )(*operands)
    handles, po = [], 0
    for n in ns:
        handles.append((outs[po], outs[po + 1], list(outs[po + 2:po + 2 + n]), list(outs[po + 2 + n:po + 2 + 2 * n])))
        po += 2 + 2 * n
    return handles, outs[-1]


def _exchange_wait(name, pieces, after):
    ns = [len(h[2]) for h, _ in pieces]

    def body(*refs):
        x, y, c = lax.axis_index("x"), lax.axis_index("y"), lax.axis_index("c")
        pi = 0
        for n in ns:
            send_sems, recv_sems, like = refs[pi + 2 * n], refs[pi + 2 * n + 1], refs[pi + 2 * n + 2]
            pi += 2 * n + 3
            for delta in range(1, N_DEV):
                cp = pltpu.make_async_remote_copy(
                    src_ref=like, dst_ref=like, send_sem=send_sems.at[delta - 1], recv_sem=recv_sems.at[delta - 1],
                    device_id=(x, y, c), device_id_type=MESH)
                cp.wait_send()
                cp.wait_recv()

    operands, in_specs, out_shape, aliases = [], [], [], {}
    for (send_sems, recv_sems, srcs, lands), like in pieces:
        for a in srcs + lands:
            aliases[len(operands)] = len(out_shape)
            operands.append(a)
            in_specs.append(HBM)
            out_shape.append(pltpu.HBM(a.shape, a.dtype))
        operands += [send_sems, recv_sems, like]
        in_specs += [SEM, SEM, ANY]
    operands.append(after)
    in_specs.append(ANY)
    outs = pl.pallas_call(
        body, name=name, out_shape=out_shape, in_specs=in_specs, out_specs=[HBM] * len(out_shape),
        input_output_aliases=aliases, compiler_params=pltpu.CompilerParams(has_side_effects=EFFECT),
    )(*operands)
    res, po = [], 0
    for n in ns:
        res.append(list(outs[po + n:po + 2 * n]))
        po += 2 * n
    return res


def _sum8(name, r):
    _, rows, cols = r.shape
    tr = _pick(rows, (256, 128, 64, 32, 16, 8))

    def body(r_ref, o_ref):
        acc = r_ref[0].astype(F32)
        for p in range(1, N_DEV):
            acc = acc + r_ref[p].astype(F32)
        o_ref[...] = acc

    return pl.pallas_call(
        body, name=name, out_shape=jax.ShapeDtypeStruct((rows, cols), F32), grid=(rows // tr,),
        in_specs=[pl.BlockSpec((N_DEV, tr, cols), lambda i: (0, i, 0))],
        out_specs=pl.BlockSpec((tr, cols), lambda i: (i, 0)), compiler_params=_params(("parallel",)),
    )(r)


def _adamw(name, g, w, m, v):
    rows, cols = g.shape
    tr = _pick(rows, (512, 256, 128, 64, 32, 16, 8))
    c1 = 1.0 - ADAM_B1 ** ADAM_STEP
    c2 = 1.0 - ADAM_B2 ** ADAM_STEP

    def body(g_ref, w_ref, m_ref, v_ref, d_ref, nm_ref, nv_ref):
        gv = g_ref[...]
        nm = ADAM_B1 * m_ref[...] + (1.0 - ADAM_B1) * gv
        nv = ADAM_B2 * v_ref[...] + (1.0 - ADAM_B2) * (gv * gv)
        d_ref[...] = -ADAM_LR * ((nm / c1) / (jnp.sqrt(nv / c2) + ADAM_EPS) + ADAM_WD * w_ref[...])
        nm_ref[...] = nm
        nv_ref[...] = nv

    spec = pl.BlockSpec((tr, cols), lambda i: (i, 0))
    return pl.pallas_call(
        body, name=name, out_shape=[jax.ShapeDtypeStruct((rows, cols), F32)] * 3, grid=(rows // tr,),
        in_specs=[spec] * 4, out_specs=[spec] * 3, compiler_params=_params(("parallel",)),
    )(g, w, m, v)


def _mm_nt(name, a, ws, w_offs, n, epi, out_dtypes, extras=(), rows=(), tm=None, tn=None):
    m, k = a.shape
    tm = tm or _pick(m, (512, 256, 128))
    tn = tn or _pick(n, (1408, 1152, 1024, 512, 256, 128))
    nw, ne, nr = len(ws), len(extras), len(rows)

    def body(*refs):
        a_ref, w_refs = refs[0], refs[1:1 + nw]
        e_refs, r_refs = refs[1 + nw:1 + nw + ne], refs[1 + nw + ne:1 + nw + ne + nr]
        o_refs = refs[1 + nw + ne + nr:]
        av = a_ref[...].astype(BF16)
        accs = [lax.dot_general(av, w[...], (((1,), (1,)), ((), ())), preferred_element_type=F32) for w in w_refs]
        outs = epi(accs, [e[...] for e in e_refs], [r[...] for r in r_refs])
        for o_ref, o in zip(o_refs, outs):
            o_ref[...] = o.astype(o_ref.dtype)

    in_specs = [pl.BlockSpec((tm, k), lambda j, i: (i, 0))]
    in_specs += [pl.BlockSpec((tn, k), functools.partial(lambda j, i, off: (j + off, 0), off=off)) for off in w_offs]
    in_specs += [pl.BlockSpec((tm, tn), lambda j, i: (i, j))] * ne
    in_specs += [pl.BlockSpec((1, tn), lambda j, i: (0, j))] * nr
    return pl.pallas_call(
        body, name=name, out_shape=[jax.ShapeDtypeStruct((m, n), dt) for dt in out_dtypes],
        grid=(n // tn, m // tm), in_specs=in_specs,
        out_specs=[pl.BlockSpec((tm, tn), lambda j, i: (i, j))] * len(out_dtypes),
        compiler_params=_params(("parallel", "parallel"), V7X_VMEM_LIMIT_BYTES),
    )(a, *ws, *extras, *rows)


def _mm_nn(name, as_, bs, epi, out_dtype, extras=(), rows=(), tm=None):
    m, k = as_[0].shape
    n = bs[0].shape[1]
    tm = tm or _pick(m, (512, 256, 128))
    npair, ne, nr = len(as_), len(extras), len(rows)

    def body(*refs):
        a_refs, b_refs = refs[:npair], refs[npair:2 * npair]
        e_refs, r_refs = refs[2 * npair:2 * npair + ne], refs[2 * npair + ne:2 * npair + ne + nr]
        o_ref = refs[-1]
        acc = None
        for a_ref, b_ref in zip(a_refs, b_refs):
            p = jnp.dot(a_ref[...].astype(BF16), b_ref[...], preferred_element_type=F32)
            acc = p if acc is None else acc + p
        o_ref[...] = epi(acc, [e[...] for e in e_refs], [r[...] for r in r_refs]).astype(o_ref.dtype)

    in_specs = [pl.BlockSpec((tm, k), lambda i: (i, 0))] * npair
    in_specs += [pl.BlockSpec((k, n), lambda i: (0, 0))] * npair
    in_specs += [pl.BlockSpec((tm, n), lambda i: (i, 0))] * ne
    in_specs += [pl.BlockSpec((1, n), lambda i: (0, 0))] * nr
    return pl.pallas_call(
        body, name=name, out_shape=jax.ShapeDtypeStruct((m, n), out_dtype), grid=(m // tm,), in_specs=in_specs,
        out_specs=pl.BlockSpec((tm, n), lambda i: (i, 0)),
        compiler_params=_params(("parallel",), V7X_VMEM_LIMIT_BYTES),
    )(*as_, *bs, *extras, *rows)


def _mm_nn_rms_bwd(name, as_, bs, x, g, dx_out, tm=None):
    m, k = as_[0].shape
    n = bs[0].shape[1]
    tm = tm or _pick(m, (512, 256, 128))
    npair = len(as_)

    def body(*refs):
        a_refs, b_refs = refs[:npair], refs[npair:2 * npair]
        x_ref, g_ref, dxo_ref, dx_ref, dg_ref = refs[2 * npair:]
        dh = None
        for a_ref, b_ref in zip(a_refs, b_refs):
            p = jnp.dot(a_ref[...].astype(BF16), b_ref[...], preferred_element_type=F32)
            dh = p if dh is None else dh + p
        xv = x_ref[...]
        r = lax.rsqrt(jnp.mean(xv * xv, axis=-1, keepdims=True) + EPS)
        yv = xv * r
        dy = dh * g_ref[...]
        dx_ref[...] = dxo_ref[...] + r * (dy - yv * jnp.mean(dy * yv, axis=-1, keepdims=True))

        @pl.when(pl.program_id(0) == 0)
        def _():
            dg_ref[...] = jnp.zeros_like(dg_ref)

        dg_ref[...] += jnp.sum(dh * yv, axis=0, keepdims=True)

    big = pl.BlockSpec((tm, n), lambda i: (i, 0))
    row = pl.BlockSpec((1, n), lambda i: (0, 0))
    in_specs = [pl.BlockSpec((tm, k), lambda i: (i, 0))] * npair + [pl.BlockSpec((k, n), lambda i: (0, 0))] * npair
    return pl.pallas_call(
        body, name=name, out_shape=[jax.ShapeDtypeStruct((m, n), F32), jax.ShapeDtypeStruct((1, n), F32)],
        grid=(m // tm,), in_specs=in_specs + [big, row, big], out_specs=[big, row],
        compiler_params=_params(("arbitrary",), V7X_VMEM_LIMIT_BYTES),
    )(*as_, *bs, x, g, dx_out)


def _mm_tn(name, a, b, colsum_b=False, tm=None, tk=2048):
    t, ma = a.shape
    nb = b.shape[1]
    tm = tm or _pick(ma, (1408, 1152, 1024, 768, 512, 256, 128))
    tk = _pick(t, (tk, 256, 128))
    nk = t // tk

    def body(*refs):
        a_ref, b_ref, o_ref = refs[0], refs[1], refs[2]
        acc_ref = refs[-1]
        kk = pl.program_id(1)
        bv = b_ref[...]

        @pl.when(kk == 0)
        def _():
            acc_ref[...] = jnp.zeros_like(acc_ref)

        acc_ref[...] += lax.dot_general(a_ref[...].astype(BF16), bv.astype(BF16), (((0,), (0,)), ((), ())),
                                        preferred_element_type=F32)
        if colsum_b:
            s_ref = refs[3]

            @pl.when((kk == 0) & (pl.program_id(0) == 0))
            def _():
                s_ref[...] = jnp.zeros_like(s_ref)

            @pl.when(pl.program_id(0) == 0)
            def _():
                s_ref[...] += jnp.sum(bv.astype(F32), axis=0, keepdims=True)

        @pl.when(kk == nk - 1)
        def _():
            o_ref[...] = acc_ref[...].astype(o_ref.dtype)

    out_shape = [jax.ShapeDtypeStruct((ma, nb), BF16)]
    out_specs = [pl.BlockSpec((tm, nb), lambda i, kk: (i, 0))]
    if colsum_b:
        out_shape.append(jax.ShapeDtypeStruct((1, nb), F32))
        out_specs.append(pl.BlockSpec((1, nb), lambda i, kk: (0, 0)))
    res = pl.pallas_call(
        body, name=name, out_shape=out_shape, grid=(ma // tm, nk),
        in_specs=[pl.BlockSpec((tk, tm), lambda i, kk: (kk, i)), pl.BlockSpec((tk, nb), lambda i, kk: (kk, 0))],
        out_specs=out_specs, scratch_shapes=[pltpu.VMEM((tm, nb), F32)],
        compiler_params=_params(("arbitrary", "arbitrary"), V7X_VMEM_LIMIT_BYTES),
    )(a, b)
    return res if colsum_b else res[0]


FFN_CHUNK = 256


def _ffn_down_bwd(name, dx, wd, gq, uq):
    t, d_model = dx.shape
    f = wd.shape[0]
    tm = _pick(t, (256, 128))
    ck = _pick(f, (FFN_CHUNK, 128))
    nt = (((1,), (1,)), ((), ()))
    tn = (((0,), (0,)), ((), ()))
    last = t // tm - 1

    def body(dx_ref, w_ref, g_ref, u_ref, dg_ref, du_ref, dw_ref, acc_ref):
        i = pl.program_id(0)

        @pl.when(i == 0)
        def _():
            acc_ref[...] = jnp.zeros_like(acc_ref)

        dxb = dx_ref[...].astype(BF16)
        for c0 in range(0, f, ck):
            cs = slice(c0, c0 + ck)
            dact = lax.dot_general(dxb, w_ref[cs, :], nt, preferred_element_type=F32)
            gf, uf = g_ref[:, cs].astype(F32), u_ref[:, cs].astype(F32)
            sg = _sigmoid(gf)
            dg_ref[:, cs] = (dact * uf * (sg * (1.0 + gf * (1.0 - sg)))).astype(BF16)
            du_ref[:, cs] = (dact * gf * sg).astype(BF16)
            act = (gf * sg * uf).astype(BF16)
            acc_ref[cs, :] += lax.dot_general(act, dxb, tn, preferred_element_type=F32)

        @pl.when(i == last)
        def _():
            dw_ref[...] = acc_ref[...].astype(BF16)

    wide = pl.BlockSpec((tm, f), lambda i: (i, 0))
    whole = pl.BlockSpec((f, d_model), lambda i: (0, 0))
    return pl.pallas_call(
        body, name=name,
        out_shape=[jax.ShapeDtypeStruct((t, f), BF16), jax.ShapeDtypeStruct((t, f), BF16),
                   jax.ShapeDtypeStruct((f, d_model), BF16)],
        grid=(t // tm,), in_specs=[pl.BlockSpec((tm, d_model), lambda i: (i, 0)), whole, wide, wide],
        out_specs=[wide, wide, whole], scratch_shapes=[pltpu.VMEM((f, d_model), F32)],
        compiler_params=_params(("arbitrary",), V7X_VMEM_LIMIT_BYTES),
    )(dx, wd, gq, uq)


def _rmsnorm_fwd(name, x, g):
    t, d = x.shape
    tr = _pick(t, (512, 256, 128))

    def body(x_ref, g_ref, h_ref):
        xv = x_ref[...]
        r = lax.rsqrt(jnp.mean(xv * xv, axis=-1, keepdims=True) + EPS)
        h_ref[...] = (xv * r * g_ref[...]).astype(BF16)

    return pl.pallas_call(
        body, name=name, out_shape=jax.ShapeDtypeStruct((t, d), BF16), grid=(t // tr,),
        in_specs=[pl.BlockSpec((tr, d), lambda i: (i, 0)), pl.BlockSpec((1, d), lambda i: (0, 0))],
        out_specs=pl.BlockSpec((tr, d), lambda i: (i, 0)), compiler_params=_params(("parallel",)),
    )(x, g)


def _loss_head(name, x, g, target):
    t, d = x.shape
    tr = _pick(t, (512, 256, 128))

    def body(x_ref, g_ref, t_ref, dx_ref, dg_ref, l_ref):
        xv = x_ref[...]
        r = lax.rsqrt(jnp.mean(xv * xv, axis=-1, keepdims=True) + EPS)
        yv = xv * r
        diff = yv * g_ref[...] - t_ref[...]
        dout = diff * (1.0 / d)
        dy = dout * g_ref[...]
        dx_ref[...] = r * (dy - yv * jnp.mean(dy * yv, axis=-1, keepdims=True))

        @pl.when(pl.program_id(0) == 0)
        def _():
            dg_ref[...] = jnp.zeros_like(dg_ref)
            l_ref[...] = jnp.zeros_like(l_ref)

        dg_ref[...] += jnp.sum(dout * yv, axis=0, keepdims=True)
        l_ref[...] += (0.5 / d) * jnp.sum(diff * diff, axis=0, keepdims=True)

    big = pl.BlockSpec((tr, d), lambda i: (i, 0))
    row = pl.BlockSpec((1, d), lambda i: (0, 0))
    return pl.pallas_call(
        body, name=name,
        out_shape=[jax.ShapeDtypeStruct((t, d), F32), jax.ShapeDtypeStruct((1, d), F32),
                   jax.ShapeDtypeStruct((1, d), F32)],
        grid=(t // tr,), in_specs=[big, row, big], out_specs=[big, row, row],
        compiler_params=_params(("arbitrary",)),
    )(x, g, target)


CONV_ROWS = 64
SUBLANES = 8


def _fill_window(win_ref, sh_ref, parts):
    rows = sh_ref.shape[2]
    for cb in range(win_ref.shape[0]):
        for r0, val in parts:
            win_ref[cb, r0:r0 + val.shape[0], :] = val[:, 128 * cb:128 * (cb + 1)]
        win_ref[cb, rows:rows + SUBLANES, :] = jnp.zeros((SUBLANES, 128), F32)
        for b in range(1, SUBLANES):
            sh_ref[b - 1, cb] = win_ref[cb, b:b + rows, :]


def _window_rows(win_ref, sh_ref, o, cb):
    b = o % SUBLANES
    if b == 0:
        return win_ref[cb, o:o + CONV_ROWS, :]
    return sh_ref[b - 1, cb, o - b:o - b + CONV_ROWS, :]


def _window_scratch(rows, c):
    return [pltpu.VMEM((c // 128, rows + SUBLANES, 128), F32), pltpu.VMEM((SUBLANES - 1, c // 128, rows, 128), F32)]


def _dwconv_fwd(name, glu, w_dw, b_dw, ln_g, ln_b):
    t, c = glu.shape
    tt = _pick(t, (256, 128))
    hb = tt // CONV_HALO

    def body(cur_ref, halo_ref, w_ref, b_ref, g_ref, be_ref, dw_ref, s_ref, win_ref, sh_ref):
        i = pl.program_id(0)
        halo = jnp.where(i > 0, halo_ref[...].astype(F32), 0.0)
        _fill_window(win_ref, sh_ref, [(0, halo), (CONV_HALO, cur_ref[...].astype(F32))])
        for r0 in range(0, tt, CONV_ROWS):
            for cb in range(c // 128):
                c0 = 128 * cb
                acc = jnp.zeros((CONV_ROWS, 128), F32) + b_ref[:, c0:c0 + 128]
                for k in range(CONV_WIDTH):
                    o = r0 + k + CONV_HALO - (CONV_WIDTH - 1)
                    acc = acc + w_ref[k:k + 1, c0:c0 + 128] * _window_rows(win_ref, sh_ref, o, cb)
                dw_ref[r0:r0 + CONV_ROWS, c0:c0 + 128] = acc
        u = dw_ref[...]
        mu = jnp.mean(u, axis=-1, keepdims=True)
        uc = u - mu
        rstd = lax.rsqrt(jnp.mean(uc * uc, axis=-1, keepdims=True) + EPS)
        z = uc * rstd * g_ref[...] + be_ref[...]
        s_ref[...] = (z * _sigmoid(z)).astype(BF16)

    big = pl.BlockSpec((tt, c), lambda i: (i, 0))
    row = pl.BlockSpec((1, c), lambda i: (0, 0))
    return pl.pallas_call(
        body, name=name, out_shape=[jax.ShapeDtypeStruct((t, c), F32), jax.ShapeDtypeStruct((t, c), BF16)],
        grid=(t // tt,),
        in_specs=[big, pl.BlockSpec((CONV_HALO, c), lambda i: (jnp.maximum(i * hb - 1, 0), 0)),
                  pl.BlockSpec((CONV_HALO, c), lambda i: (0, 0)), row, row, row],
        out_specs=[big, big], scratch_shapes=_window_scratch(tt + CONV_HALO, c),
        compiler_params=_params(("parallel",), V7X_VMEM_LIMIT_BYTES),
    )(glu, glu, w_dw, b_dw, ln_g, ln_b)


def _ln_silu_bwd(name, dx, w2, dw, ln_g, ln_b):
    t, c = dw.shape
    d_model = dx.shape[1]
    tr = _pick(t, (256, 128))

    def body(dx_ref, w_ref, dw_ref, g_ref, be_ref, o_ref, acc_ref):
        ds = lax.dot_general(dx_ref[...].astype(BF16), w_ref[...], (((1,), (1,)), ((), ())),
                             preferred_element_type=F32)
        u = dw_ref[...]
        mu = jnp.mean(u, axis=-1, keepdims=True)
        uc = u - mu
        rstd = lax.rsqrt(jnp.mean(uc * uc, axis=-1, keepdims=True) + EPS)
        xh = uc * rstd
        z = xh * g_ref[...] + be_ref[...]
        sg = _sigmoid(z)
        dz = ds * (sg * (1.0 + z * (1.0 - sg)))
        dxh = dz * g_ref[...]
        du = rstd * (dxh - jnp.mean(dxh, axis=-1, keepdims=True) - xh * jnp.mean(dxh * xh, axis=-1, keepdims=True))
        o_ref[...] = du

        @pl.when(pl.program_id(0) == 0)
        def _():
            acc_ref[...] = jnp.zeros_like(acc_ref)

        acc_ref[0:1, :] += jnp.sum(dz * xh, axis=0, keepdims=True)
        acc_ref[1:2, :] += jnp.sum(dz, axis=0, keepdims=True)
        acc_ref[2:3, :] += jnp.sum(du, axis=0, keepdims=True)

    big = pl.BlockSpec((tr, c), lambda i: (i, 0))
    row = pl.BlockSpec((1, c), lambda i: (0, 0))
    return pl.pallas_call(
        body, name=name, out_shape=[jax.ShapeDtypeStruct((t, c), F32), jax.ShapeDtypeStruct((8, c), F32)],
        grid=(t // tr,),
        in_specs=[pl.BlockSpec((tr, d_model), lambda i: (i, 0)), pl.BlockSpec((c, d_model), lambda i: (0, 0)),
                  big, row, row],
        out_specs=[big, pl.BlockSpec((8, c), lambda i: (0, 0))],
        compiler_params=_params(("arbitrary",), V7X_VMEM_LIMIT_BYTES),
    )(dx, w2, dw, ln_g, ln_b)


def _dwconv_bwd(name, ddw, a, gt, w_dw):
    t, c = ddw.shape
    tt = _pick(t, (256, 128))
    hb = tt // CONV_HALO
    last = t // tt - 1
    back = CONV_WIDTH - 1

    def body(d_ref, dn_ref, a_ref, ap_ref, g_ref, gp_ref, w_ref, du_ref, dwk_ref, db_ref,
             wd_ref, shd_ref, wg_ref, shg_ref, dg_ref, dwk8_ref):
        i = pl.program_id(0)
        _fill_window(wd_ref, shd_ref, [(0, d_ref[...]), (tt, jnp.where(i < last, dn_ref[...], 0.0))])
        glu_prev = ap_ref[...].astype(F32) * _sigmoid(gp_ref[...].astype(F32))
        av = a_ref[...].astype(F32)
        sg = _sigmoid(g_ref[...].astype(F32))
        _fill_window(wg_ref, shg_ref, [(0, jnp.where(i > 0, glu_prev, 0.0)), (CONV_HALO, av * sg)])

        @pl.when(i == 0)
        def _():
            dwk8_ref[...] = jnp.zeros_like(dwk8_ref)
            db_ref[...] = jnp.zeros_like(db_ref)

        for r0 in range(0, tt, CONV_ROWS):
            for cb in range(c // 128):
                c0 = 128 * cb
                dcur = wd_ref[cb, r0:r0 + CONV_ROWS, :]
                acc = jnp.zeros((CONV_ROWS, 128), F32)
                for k in range(CONV_WIDTH):
                    acc = acc + w_ref[k:k + 1, c0:c0 + 128] * _window_rows(wd_ref, shd_ref, r0 + back - k, cb)
                    p = dcur * _window_rows(wg_ref, shg_ref, r0 + k + CONV_HALO - back, cb)
                    s8 = p[0:SUBLANES]
                    for q in range(SUBLANES, CONV_ROWS, SUBLANES):
                        s8 = s8 + p[q:q + SUBLANES]
                    dwk8_ref[SUBLANES * k:SUBLANES * (k + 1), c0:c0 + 128] += s8
                dg_ref[r0:r0 + CONV_ROWS, c0:c0 + 128] = acc

        @pl.when(i == last)
        def _():
            for k in range(CONV_WIDTH):
                dwk_ref[k:k + 1, :] = jnp.sum(dwk8_ref[SUBLANES * k:SUBLANES * (k + 1), :], axis=0, keepdims=True)
            dwk_ref[CONV_WIDTH:, :] = jnp.zeros((CONV_HALO - CONV_WIDTH, c), F32)
        dglu = dg_ref[...]
        da = dglu * sg
        dgate = dglu * av * sg * (1.0 - sg)
        du_ref[:, 0:c] = da.astype(BF16)
        du_ref[:, c:] = dgate.astype(BF16)
        db_ref[:, 0:c] += jnp.sum(da, axis=0, keepdims=True)
        db_ref[:, c:] += jnp.sum(dgate, axis=0, keepdims=True)

    big = pl.BlockSpec((tt, c), lambda i: (i, 0))
    prev = pl.BlockSpec((CONV_HALO, c), lambda i: (jnp.maximum(i * hb - 1, 0), 0))
    nxt = pl.BlockSpec((CONV_HALO, c), lambda i: (jnp.minimum((i + 1) * hb, t // CONV_HALO - 1), 0))
    return pl.pallas_call(
        body, name=name,
        out_shape=[jax.ShapeDtypeStruct((t, 2 * c), BF16), jax.ShapeDtypeStruct((CONV_HALO, c), F32),
                   jax.ShapeDtypeStruct((1, 2 * c), F32)],
        grid=(t // tt,),
        in_specs=[big, nxt, big, prev, big, prev, pl.BlockSpec((CONV_HALO, c), lambda i: (0, 0))],
        out_specs=[pl.BlockSpec((tt, 2 * c), lambda i: (i, 0)), pl.BlockSpec((CONV_HALO, c), lambda i: (0, 0)),
                   pl.BlockSpec((1, 2 * c), lambda i: (0, 0))],
        scratch_shapes=_window_scratch(tt + CONV_HALO, c) + _window_scratch(tt + CONV_HALO, c)
        + [pltpu.VMEM((tt, c), F32), pltpu.VMEM((SUBLANES * CONV_HALO, c), F32)],
        compiler_params=_params(("arbitrary",), V7X_VMEM_LIMIT_BYTES),
    )(ddw, ddw, a, a, gt, gt, w_dw)


def _bucket_tables():
    i = np.arange(N_BACK)[:, None]
    j = np.arange(2 * N_BACK)[None, :]
    dist = i + N_BACK - j
    valid = (dist >= 0) & (dist <= N_BACK)
    max_exact = N_BUCKETS // 2
    out = []
    for d in DILATIONS:
        n = np.maximum(dist * d, 0)
        nf = np.maximum(n, 1).astype(np.float32)
        large = max_exact + (np.log(nf / np.float32(max_exact)) / np.float32(math.log(REL_MAX_DISTANCE / max_exact))
                             * np.float32(N_BUCKETS - max_exact)).astype(np.int32)
        large = np.minimum(large, N_BUCKETS - 1)
        out.append(np.where(valid, np.where(n < max_exact, n, large), -1))
    return np.stack(out).astype(np.int32)


def _bias_build(name, rel_bias, buckets):
    def body(tbl_ref, bk_ref, o_ref):
        g = pl.program_id(0)
        bk = bk_ref[0]
        for h in range(HEADS_PER_GROUP):
            acc = jnp.zeros(bk.shape, F32)
            for b in range(N_BUCKETS):
                acc = jnp.where(bk == b, tbl_ref[b, g * HEADS_PER_GROUP + h], acc)
            o_ref[h] = jnp.where(bk < 0, NEG_INF, acc)

    return pl.pallas_call(
        body, name=name, out_shape=jax.ShapeDtypeStruct((N_HEADS, N_BACK, 2 * N_BACK), F32), grid=(N_GROUPS,),
        in_specs=[pl.BlockSpec(memory_space=pltpu.SMEM), pl.BlockSpec((1, N_BACK, 2 * N_BACK), lambda g: (g, 0, 0))],
        out_specs=pl.BlockSpec((HEADS_PER_GROUP, N_BACK, 2 * N_BACK), lambda g: (g, 0, 0)),
        compiler_params=_params(("arbitrary",)),
    )(rel_bias, buckets)


def _bias_grad(name, dbs, buckets):
    nd = len(dbs)

    def body(*refs):
        bk = refs[nd][0]
        o_ref = refs[nd + 1]
        lane = lax.broadcasted_iota(jnp.int32, (1, 128), 1)
        db = [sum(r[h] for r in refs[:nd]) for h in range(HEADS_PER_GROUP)]
        for b in range(N_BUCKETS):
            row = jnp.zeros((1, 128), F32)
            for h in range(HEADS_PER_GROUP):
                s = jnp.sum(jnp.where(bk == b, db[h], 0.0), axis=0, keepdims=True)
                s = jnp.sum(s, axis=1, keepdims=True)
                row = jnp.where(lane // 32 == h, s, row)
            o_ref[0, b:b + 1, :] = row

    spec = pl.BlockSpec((HEADS_PER_GROUP, N_BACK, 2 * N_BACK), lambda g: (g, 0, 0))
    return pl.pallas_call(
        body, name=name, out_shape=jax.ShapeDtypeStruct((N_GROUPS, N_BUCKETS, 128), F32), grid=(N_GROUPS,),
        in_specs=[spec] * nd + [pl.BlockSpec((1, N_BACK, 2 * N_BACK), lambda g: (g, 0, 0))],
        out_specs=pl.BlockSpec((1, N_BUCKETS, 128), lambda g: (g, 0, 0)), compiler_params=_params(("arbitrary",)),
    )(*dbs, buckets)


def _head_cols(h):
    return slice(h * HEAD_DIM, (h + 1) * HEAD_DIM)


def _attn_fwd(name, qkv, bias, g):
    t = qkv.shape[0]
    d = DILATIONS[g]
    tq = t // d
    nblk = qkv.shape[1] // GROUP_COLS
    scale = HEAD_DIM ** -0.5

    def body(q_ref, kp_ref, kc_ref, vp_ref, vc_ref, b_ref, o_ref, l_ref):
        m2 = pl.program_id(1)
        col = lax.broadcasted_iota(jnp.int32, (N_BACK, 2 * N_BACK), 1)
        lane = lax.broadcasted_iota(jnp.int32, (N_BACK, 128), 1)
        for sub in range(2):
            rows = slice(N_BACK * sub, N_BACK * (sub + 1))
            lse_tile = jnp.zeros((N_BACK, 128), F32)
            outs = []
            for h in range(HEADS_PER_GROUP):
                hc = _head_cols(h)
                if sub == 0:
                    kk = jnp.concatenate([kp_ref[:, hc], kc_ref[0:N_BACK, hc]], axis=0)
                    vv = jnp.concatenate([vp_ref[:, hc], vc_ref[0:N_BACK, hc]], axis=0)
                else:
                    kk, vv = kc_ref[:, hc], vc_ref[:, hc]
                s = lax.dot_general(q_ref[rows, hc], kk, (((1,), (1,)), ((), ())), preferred_element_type=F32)
                s = s * scale + b_ref[h]
                if sub == 0:
                    s = jnp.where((col >= N_BACK) | (m2 > 0), s, NEG_INF)
                m = jnp.max(s, axis=-1, keepdims=True)
                p = jnp.exp(s - m)
                den = jnp.sum(p, axis=-1, keepdims=True)
                outs.append(jnp.dot(p.astype(BF16), vv, preferred_element_type=F32) / den)
                lse_tile = jnp.where(lane // 32 == h, m + jnp.log(den), lse_tile)
            o_ref[rows, :] = jnp.concatenate(outs, axis=1)
            l_ref[rows, :] = lse_tile

    def blk(part, prev):
        if prev:
            return pl.BlockSpec((N_BACK, GROUP_COLS), lambda r, n: (jnp.maximum(2 * n - 1, 0), r * nblk + part))
        return pl.BlockSpec((2 * N_BACK, GROUP_COLS), lambda r, n: (n, r * nblk + part))

    qv = qkv.reshape(tq, d * qkv.shape[1])
    o, l = pl.pallas_call(
        body, name=name,
        out_shape=[jax.ShapeDtypeStruct((tq, d * GROUP_COLS), F32), jax.ShapeDtypeStruct((tq, d * 128), F32)],
        grid=(d, tq // (2 * N_BACK)),
        in_specs=[blk(0, False), blk(1, True), blk(1, False), blk(2, True), blk(2, False),
                  pl.BlockSpec((HEADS_PER_GROUP, N_BACK, 2 * N_BACK), lambda r, n: (g, 0, 0))],
        out_specs=[pl.BlockSpec((2 * N_BACK, GROUP_COLS), lambda r, n: (n, r)),
                   pl.BlockSpec((2 * N_BACK, 128), lambda r, n: (n, r))],
        compiler_params=_params(("parallel", "parallel")),
    )(qv, qv, qv, qv, qv, bias)
    return o.reshape(t, GROUP_COLS), l.reshape(t, 128)


def _group_weights(l_refs, h):
    ls = [l_ref[:, 32 * h:32 * h + 1] for l_ref in l_refs]
    m = jnp.maximum(jnp.maximum(ls[0], ls[1]), ls[2])
    es = [jnp.exp(l - m) for l in ls]
    tot = es[0] + es[1] + es[2]
    return [e / tot for e in es]


def _attn_merge_out(name, os_, ls, wot, x):
    t, d_model = x.shape
    tr = _pick(t, (512, 256, 128))

    def body(o0, o1, o2, l0, l1, l2, w_ref, x_ref, om_ref, out_ref):
        o_refs = (o0, o1, o2)
        pieces = [[None] * HEADS_PER_GROUP for _ in range(N_GROUPS)]
        for h in range(HEADS_PER_GROUP):
            al = _group_weights((l0, l1, l2), h)
            for g in range(N_GROUPS):
                pieces[g][h] = o_refs[g][:, _head_cols(h)] * al[g]
        om = jnp.concatenate([p for row in pieces for p in row], axis=1).astype(BF16)
        om_ref[...] = om
        out_ref[...] = x_ref[...] + lax.dot_general(om, w_ref[...], (((1,), (1,)), ((), ())),
                                                    preferred_element_type=F32)

    so = pl.BlockSpec((tr, GROUP_COLS), lambda i: (i, 0))
    sl = pl.BlockSpec((tr, 128), lambda i: (i, 0))
    sx = pl.BlockSpec((tr, d_model), lambda i: (i, 0))
    return pl.pallas_call(
        body, name=name,
        out_shape=[jax.ShapeDtypeStruct((t, D_ATTN), BF16), jax.ShapeDtypeStruct((t, d_model), F32)],
        grid=(t // tr,), in_specs=[so] * 3 + [sl] * 3 + [pl.BlockSpec((d_model, D_ATTN), lambda i: (0, 0)), sx],
        out_specs=[pl.BlockSpec((tr, D_ATTN), lambda i: (i, 0)), sx],
        compiler_params=_params(("parallel",), V7X_VMEM_LIMIT_BYTES),
    )(*os_, *ls, wot, x)


def _attn_bwd_prep(name, dx, wot, os_, ls):
    t, d_model = dx.shape
    tr = _pick(t, (512, 256, 128))

    def body(dx_ref, w_ref, o0, o1, o2, l0, l1, l2, d0, d1, d2, c0, c1, c2):
        o_refs, d_refs, c_refs = (o0, o1, o2), (d0, d1, d2), (c0, c1, c2)
        d_out = jnp.dot(dx_ref[...].astype(BF16), w_ref[...], preferred_element_type=F32)
        lane = lax.broadcasted_iota(jnp.int32, (tr, 128), 1)
        dos = [[None] * HEADS_PER_GROUP for _ in range(N_GROUPS)]
        cs = [jnp.zeros((tr, 128), F32) for _ in range(N_GROUPS)]
        for h in range(HEADS_PER_GROUP):
            al = _group_weights((l0, l1, l2), h)
            tot = jnp.zeros((tr, 1), F32)
            for g in range(N_GROUPS):
                dv = d_out[:, g * GROUP_COLS + h * HEAD_DIM:g * GROUP_COLS + (h + 1) * HEAD_DIM]
                tot = tot + al[g] * jnp.sum(dv * o_refs[g][:, _head_cols(h)], axis=-1, keepdims=True)
                dos[g][h] = dv * al[g]
            for g in range(N_GROUPS):
                cs[g] = jnp.where(lane // 32 == h, -al[g] * tot, cs[g])
        for g in range(N_GROUPS):
            d_refs[g][...] = jnp.concatenate(dos[g], axis=1).astype(BF16)
            c_refs[g][...] = cs[g]

    so = pl.BlockSpec((tr, GROUP_COLS), lambda i: (i, 0))
    sl = pl.BlockSpec((tr, 128), lambda i: (i, 0))
    res = pl.pallas_call(
        body, name=name,
        out_shape=[jax.ShapeDtypeStruct((t, GROUP_COLS), BF16)] * 3 + [jax.ShapeDtypeStruct((t, 128), F32)] * 3,
        grid=(t // tr,),
        in_specs=[pl.BlockSpec((tr, d_model), lambda i: (i, 0)), pl.BlockSpec((d_model, D_ATTN), lambda i: (0, 0))]
        + [so] * 3 + [sl] * 3,
        out_specs=[so] * 3 + [sl] * 3, compiler_params=_params(("parallel",), V7X_VMEM_LIMIT_BYTES),
    )(dx, wot, *os_, *ls)
    return res[:3], res[3:]


def _attn_bwd(name, qkv, do, lse, cterm, bias, g):
    t = qkv.shape[0]
    d = DILATIONS[g]
    tq = t // d
    nb = tq // N_BACK
    nblk = qkv.shape[1] // GROUP_COLS
    scale = HEAD_DIM ** -0.5
    nt = (((1,), (1,)), ((), ()))
    tn = (((0,), (0,)), ((), ()))

    def body(qn, qx, kp, kn, vp, vn, don, dox, ln, lx, cn, cx, b_ref, dqkv_ref, db_ref):
        n = pl.program_id(1)
        has_prev = n > 0
        has_next = n < nb - 1

        @pl.when((n == 0) & (pl.program_id(0) == 0))
        def _():
            db_ref[...] = jnp.zeros_like(db_ref)

        row2 = lax.broadcasted_iota(jnp.int32, (2 * N_BACK, N_BACK), 0)
        on_ac = (row2 < N_BACK) | has_next
        dqs, dks, dvs = [], [], []
        for h in range(HEADS_PER_GROUP):
            hc = _head_cols(h)
            st = slice(32 * h, 32 * h + 1)
            b_prev, b_same = b_ref[h, :, 0:N_BACK], b_ref[h, :, N_BACK:]
            q0, k0, k1, v0, v1, d0 = qn[:, hc], kp[:, hc], kn[:, hc], vp[:, hc], vn[:, hc], don[:, hc]
            q_ac = jnp.concatenate([q0, qx[:, hc]], axis=0)
            d_ac = jnp.concatenate([d0, dox[:, hc]], axis=0)
            l_ac = jnp.concatenate([ln[:, st], lx[:, st]], axis=0)
            c_ac = jnp.concatenate([cn[:, st], cx[:, st]], axis=0)
            s_ac = (lax.dot_general(q_ac, k1, nt, preferred_element_type=F32) * scale
                    + jnp.concatenate([b_same, b_prev], axis=0))
            p_ac = jnp.where(on_ac, jnp.exp(s_ac - l_ac), 0.0)
            ds_ac = p_ac * (lax.dot_general(d_ac, v1, nt, preferred_element_type=F32) + c_ac)
            s_b = lax.dot_general(q0, k0, nt, preferred_element_type=F32) * scale + b_prev
            p_b = jnp.where(has_prev, jnp.exp(s_b - ln[:, st]), 0.0)
            ds_b = p_b * (lax.dot_general(d0, v0, nt, preferred_element_type=F32) + cn[:, st])
            ds_a = ds_ac[0:N_BACK]
            dqs.append(scale * jnp.dot(jnp.concatenate([ds_b, ds_a], axis=1).astype(BF16),
                                       jnp.concatenate([k0, k1], axis=0), preferred_element_type=F32))
            dks.append(scale * lax.dot_general(ds_ac.astype(BF16), q_ac, tn, preferred_element_type=F32))
            dvs.append(lax.dot_general(p_ac.astype(BF16), d_ac, tn, preferred_element_type=F32))
            db_ref[h, :, 0:N_BACK] += ds_b
            db_ref[h, :, N_BACK:] += ds_a
        dqkv_ref[...] = jnp.concatenate(dqs + dks + dvs, axis=1).astype(BF16)

    def rows(which):
        if which == "prev":
            return lambda n: jnp.maximum(n - 1, 0)
        if which == "next":
            return lambda n: jnp.minimum(n + 1, nb - 1)
        return lambda n: n

    def qkv_blk(part, which):
        f = rows(which)
        return pl.BlockSpec((N_BACK, GROUP_COLS), lambda r, n: (f(n), r * nblk + part))

    def grp_blk(width, which):
        f = rows(which)
        return pl.BlockSpec((N_BACK, width), lambda r, n: (f(n), r))

    qv = qkv.reshape(tq, d * qkv.shape[1])
    dov = do.reshape(tq, d * GROUP_COLS)
    lv = lse.reshape(tq, d * 128)
    cv = cterm.reshape(tq, d * 128)
    dqkv_g, db = pl.pallas_call(
        body, name=name,
        out_shape=[jax.ShapeDtypeStruct((tq, d * 3 * GROUP_COLS), BF16),
                   jax.ShapeDtypeStruct((HEADS_PER_GROUP, N_BACK, 2 * N_BACK), F32)],
        grid=(d, nb),
        in_specs=[qkv_blk(0, "same"), qkv_blk(0, "next"), qkv_blk(1, "prev"), qkv_blk(1, "same"),
                  qkv_blk(2, "prev"), qkv_blk(2, "same"), grp_blk(GROUP_COLS, "same"), grp_blk(GROUP_COLS, "next"),
                  grp_blk(128, "same"), grp_blk(128, "next"), grp_blk(128, "same"), grp_blk(128, "next"),
                  pl.BlockSpec((HEADS_PER_GROUP, N_BACK, 2 * N_BACK), lambda r, n: (g, 0, 0))],
        out_specs=[pl.BlockSpec((N_BACK, 3 * GROUP_COLS), lambda r, n: (n, r)),
                   pl.BlockSpec((HEADS_PER_GROUP, N_BACK, 2 * N_BACK), lambda r, n: (0, 0, 0))],
        compiler_params=_params(("arbitrary", "arbitrary")),
    )(qv, qv, qv, qv, qv, qv, dov, dov, lv, lv, cv, cv, bias)
    return dqkv_g.reshape(t, 3 * GROUP_COLS), db


def _row(v):
    return v.reshape(1, -1)


def _glu_epi(accs, extras, rows):
    a = (accs[0] + rows[0]).astype(BF16)
    gt = (accs[1] + rows[1]).astype(BF16)
    return a, gt, a.astype(F32) * _sigmoid(gt.astype(F32))


def _swiglu_epi(accs, extras, rows):
    gq, uq = accs[0].astype(BF16), accs[1].astype(BF16)
    gf = gq.astype(F32)
    return gq, uq, gf * _sigmoid(gf) * uq.astype(F32)


def _residual(acc, extras, rows):
    out = acc + extras[0]
    return out + rows[0] if rows else out


def _group_rows(w):
    parts = [w[p * D_ATTN:(p + 1) * D_ATTN].reshape(N_GROUPS, GROUP_COLS, -1) for p in range(3)]
    return jnp.concatenate(parts, axis=1)


def _ungroup_rows(wg):
    return jnp.concatenate([wg[g][p * GROUP_COLS:(p + 1) * GROUP_COLS] for p in range(3) for g in range(N_GROUPS)],
                           axis=0)


def _local_step(x, target, sm, depth, fetch, emit):
    d_model = x.shape[1]
    buckets = jnp.asarray(_bucket_tables())
    bias = _bias_build("bias_build", sm["rel_bias"], buckets)
    saved = []
    for i in range(depth):
        j = i // 2
        rec = {"x_mix": x}
        wm = fetch(2 * i, x)
        h = _rmsnorm_fwd(f"rms_mix_fwd{i}", x, _row(sm["norm_mix"][i]))
        rec.update(h_mix=h, wm=wm)
        if i % 2 == 0:
            c = wm["w2"].shape[0]
            tn = _pick(c, (512, 256, 128))
            b1 = sm["conv_b_pw1"][j]
            a, gt, glu = _mm_nt(f"conv_pw1_fwd{j}", h, [wm["w1t"]] * 2, [0, c // tn], c, _glu_epi, (BF16,) * 3,
                                rows=[_row(b1[:c]), _row(b1[c:])], tn=tn)
            dw, s = _dwconv_fwd(f"dwconv_fwd{j}", glu, wm["wdw"], _row(sm["conv_b_dw"][j]),
                                _row(sm["conv_ln_g"][j]), _row(sm["conv_ln_b"][j]))
            x = _mm_nn(f"conv_pw2_fwd{j}", [s], [wm["w2"]], _residual, F32, extras=[x],
                       rows=[_row(sm["conv_b_pw2"][j])])
            rec.update(a=a, gt=gt, dw=dw, s=s)
        else:
            wq = _group_rows(wm["wqkvt"])
            qkv = _mm_nt(f"attn_qkv_fwd{j}", h, [wq[g] for g in range(N_GROUPS)], [0] * N_GROUPS, D_ATTN,
                         lambda accs, e, r: tuple(accs), (BF16,) * N_GROUPS, tn=D_ATTN)
            og = [_attn_fwd(f"attn_fwd{j}_{g}", qkv[g], bias, g) for g in range(N_GROUPS)]
            os_, ls = [o for o, _ in og], [l for _, l in og]
            om, x = _attn_merge_out(f"attn_out_fwd{j}", os_, ls, wm["wot"], x)
            rec.update(qkv=qkv, os=os_, ls=ls, om=om, wq=wq)
        rec["x_ffn"] = x
        wf = fetch(2 * i + 1, x)
        h2 = _rmsnorm_fwd(f"rms_ffn_fwd{i}", x, _row(sm["norm_ffn"][i]))
        f = wf["wd"].shape[0]
        gq, uq, act = _mm_nt(f"ffn_up_fwd{i}", h2, [wf["wgt"], wf["wut"]], [0, 0], f, _swiglu_epi, (BF16,) * 3)
        x = _mm_nn(f"ffn_down_fwd{i}", [act], [wf["wd"]], _residual, F32, extras=[x])
        rec.update(h_ffn=h2, gq=gq, uq=uq, act=act, wf=wf)
        saved.append(rec)

    dx, g_final, loss_cols = _loss_head("loss_head", x, _row(sm["final_norm"]), target)

    g_mix, g_ffn = [None] * depth, [None] * depth
    nconv = (depth + 1) // 2
    g_b1, g_bdw, g_lng, g_lnb, g_b2 = ([None] * nconv for _ in range(5))
    dbias = []
    for i in reversed(range(depth)):
        j = i // 2
        rec = saved[i]
        wm, wf = rec["wm"], rec["wf"]
        dgate, dup, dwd = _ffn_down_bwd(f"ffn_down_bwd{i}", dx, wf["wd"], rec["gq"], rec["uq"])
        gf = {"wd": dwd,
              "wgt": _mm_tn(f"ffn_gate_dw{i}", dgate, rec["h_ffn"]),
              "wut": _mm_tn(f"ffn_up_dw{i}", dup, rec["h_ffn"])}
        dx, g_ffn[i] = _mm_nn_rms_bwd(f"ffn_up_bwd{i}", [dgate, dup], [wf["wgt"], wf["wut"]], rec["x_ffn"],
                                      _row(sm["norm_ffn"][i]), dx, tm=256)
        dx = emit(2 * i + 1, gf, dx)
        if i % 2 == 0:
            c = wm["w2"].shape[0]
            gm = {}
            gm["w2"], g_b2[j] = _mm_tn(f"conv_pw2_dw{j}", rec["s"], dx, colsum_b=True)
            ddw, sums = _ln_silu_bwd(f"conv_pw2_bwd{j}", dx, wm["w2"], rec["dw"], _row(sm["conv_ln_g"][j]),
                                     _row(sm["conv_ln_b"][j]))
            g_lng[j], g_lnb[j], g_bdw[j] = sums[0], sums[1], sums[2]
            du, dwk, db1 = _dwconv_bwd(f"dwconv_bwd{j}", ddw, rec["a"], rec["gt"], wm["wdw"])
            gm["wdw"] = dwk[:CONV_WIDTH]
            g_b1[j] = db1[0]
            gm["w1t"] = _mm_tn(f"conv_pw1_dw{j}", du, rec["h_mix"])
            dh_terms = ([du], [wm["w1t"]])
        else:
            gm = {"wot": _mm_tn(f"attn_out_dw{j}", dx, rec["om"])}
            dos, cs = _attn_bwd_prep(f"attn_out_bwd{j}", dx, wm["wot"], rec["os"], rec["ls"])
            back = [_attn_bwd(f"attn_bwd{j}_{g}", rec["qkv"][g], dos[g], rec["ls"][g], cs[g], bias, g)
                    for g in range(N_GROUPS)]
            dqkv = [b[0] for b in back]
            dbias.append(jnp.concatenate([b[1] for b in back], axis=0))
            dh_terms = (dqkv, [rec["wq"][g] for g in range(N_GROUPS)])
            gm["wqkvt"] = _ungroup_rows([_mm_tn(f"attn_qkv_dw{j}_{g}", dqkv[g], rec["h_mix"])
                                         for g in range(N_GROUPS)])
        dx, g_mix[i] = _mm_nn_rms_bwd(f"mix_in_bwd{i}", dh_terms[0], dh_terms[1], rec["x_mix"],
                                      _row(sm["norm_mix"][i]), dx)
        dx = emit(2 * i, gm, dx)

    gb = _bias_grad("bias_grad", dbias, buckets)
    g_rel = jnp.transpose(gb[:, :, ::32], (1, 0, 2)).reshape(N_BUCKETS, N_HEADS)
    gsm = {
        "norm_mix": jnp.concatenate(g_mix, axis=0), "norm_ffn": jnp.concatenate(g_ffn, axis=0),
        "final_norm": g_final[0], "conv_b_pw1": jnp.stack(g_b1), "conv_b_dw": jnp.stack(g_bdw),
        "conv_ln_g": jnp.stack(g_lng), "conv_ln_b": jnp.stack(g_lnb),
        "conv_b_pw2": jnp.concatenate(g_b2, axis=0), "rel_bias": g_rel,
    }
    return loss_cols, dx, gsm


SMALL = ("norm_mix", "norm_ffn", "final_norm", "conv_b_pw1", "conv_b_dw", "conv_ln_g", "conv_ln_b", "conv_b_pw2",
         "rel_bias")
SHARDED = (("conv_w_pw1", "w1t", True), ("conv_w_pw2", "w2", False), ("attn_w_qkv", "wqkvt", True),
           ("attn_w_o", "wot", True), ("ffn_w_gate", "wgt", True), ("ffn_w_up", "wut", True),
           ("ffn_w_down", "wd", False))
ORDER = ("norm_mix", "norm_ffn", "final_norm", "conv_w_pw1", "conv_b_pw1", "conv_w_dw", "conv_b_dw", "conv_ln_g",
         "conv_ln_b", "conv_w_pw2", "conv_b_pw2", "attn_w_qkv", "attn_w_o", "rel_bias", "ffn_w_gate", "ffn_w_up",
         "ffn_w_down")
PACK_LANES = 128
PACK_ROW_TILE = 8


def _pack_small(vals):
    flat = jnp.concatenate([vals[n].reshape(-1) for n in SMALL])
    per_tile = PACK_LANES * PACK_ROW_TILE
    return jnp.pad(flat, (0, -flat.shape[0] % per_tile)).reshape(-1, PACK_LANES)


def _unpack_small(pack, like):
    flat, out, pos = pack.reshape(-1), {}, 0
    for n in SMALL:
        out[n] = flat[pos:pos + like[n].size].reshape(like[n].shape)
        pos += like[n].size
    return out


def _dw_blocks(w):
    l, k, c = w.shape
    blk = jnp.transpose(w.reshape(l, k, N_DEV, c // N_DEV), (2, 0, 1, 3)).reshape(N_DEV, l * k, c // N_DEV)
    return jnp.pad(blk, ((0, 0), (0, -(l * k) % 8), (0, 0)))


def kernel(x, norm_mix, norm_ffn, final_norm, conv_w_pw1, conv_b_pw1, conv_w_dw, conv_b_dw, conv_ln_g, conv_ln_b, conv_w_pw2, conv_b_pw2, attn_w_qkv, attn_w_o, rel_bias, ffn_w_gate, ffn_w_up, ffn_w_down, loss_target, m_norm_mix, m_norm_ffn, m_final_norm, m_conv_w_pw1, m_conv_b_pw1, m_conv_w_dw, m_conv_b_dw, m_conv_ln_g, m_conv_ln_b, m_conv_w_pw2, m_conv_b_pw2, m_attn_w_qkv, m_attn_w_o, m_rel_bias, m_ffn_w_gate, m_ffn_w_up, m_ffn_w_down, v_norm_mix, v_norm_ffn, v_final_norm, v_conv_w_pw1, v_conv_b_pw1, v_conv_w_dw, v_conv_b_dw, v_conv_ln_g, v_conv_ln_b, v_conv_w_pw2, v_conv_b_pw2, v_attn_w_qkv, v_attn_w_o, v_rel_bias, v_ffn_w_gate, v_ffn_w_up, v_ffn_w_down):
    w = dict(norm_mix=norm_mix, norm_ffn=norm_ffn, final_norm=final_norm, conv_w_pw1=conv_w_pw1,
             conv_b_pw1=conv_b_pw1, conv_w_dw=conv_w_dw, conv_b_dw=conv_b_dw, conv_ln_g=conv_ln_g,
             conv_ln_b=conv_ln_b, conv_w_pw2=conv_w_pw2, conv_b_pw2=conv_b_pw2, attn_w_qkv=attn_w_qkv,
             attn_w_o=attn_w_o, rel_bias=rel_bias, ffn_w_gate=ffn_w_gate, ffn_w_up=ffn_w_up, ffn_w_down=ffn_w_down)
    m = dict(norm_mix=m_norm_mix, norm_ffn=m_norm_ffn, final_norm=m_final_norm, conv_w_pw1=m_conv_w_pw1,
             conv_b_pw1=m_conv_b_pw1, conv_w_dw=m_conv_w_dw, conv_b_dw=m_conv_b_dw, conv_ln_g=m_conv_ln_g,
             conv_ln_b=m_conv_ln_b, conv_w_pw2=m_conv_w_pw2, conv_b_pw2=m_conv_b_pw2, attn_w_qkv=m_attn_w_qkv,
             attn_w_o=m_attn_w_o, rel_bias=m_rel_bias, ffn_w_gate=m_ffn_w_gate, ffn_w_up=m_ffn_w_up,
             ffn_w_down=m_ffn_w_down)
    v = dict(norm_mix=v_norm_mix, norm_ffn=v_norm_ffn, final_norm=v_final_norm, conv_w_pw1=v_conv_w_pw1,
             conv_b_pw1=v_conv_b_pw1, conv_w_dw=v_conv_w_dw, conv_b_dw=v_conv_b_dw, conv_ln_g=v_conv_ln_g,
             conv_ln_b=v_conv_ln_b, conv_w_pw2=v_conv_w_pw2, conv_b_pw2=v_conv_b_pw2, attn_w_qkv=v_attn_w_qkv,
             attn_w_o=v_attn_w_o, rel_bias=v_rel_bias, ffn_w_gate=v_ffn_w_gate, ffn_w_up=v_ffn_w_up,
             ffn_w_down=v_ffn_w_down)

    me = 4 * lax.axis_index("x") + 2 * lax.axis_index("y") + lax.axis_index("c")
    depth = ffn_w_gate.shape[0]
    n_conv, _, cb = conv_w_dw.shape

    def sublayer(key, layer):
        if key in ("wgt", "wut", "wd"):
            return 2 * layer + 1
        return 4 * layer if key in ("w1t", "w2") else 4 * layer + 2

    def landing(block, own):
        land = lax.empty((N_DEV,) + block.shape, block.dtype)
        return lax.dynamic_update_slice(land, own[None], (me,) + (0,) * block.ndim)

    by_sub = {s: [] for s in range(2 * depth)}
    for name, key, cols in SHARDED:
        sw = (jnp.swapaxes(w[name], 1, 2) if cols else w[name]).astype(BF16)
        for layer in range(sw.shape[0]):
            by_sub[sublayer(key, layer)].append((key, layer, sw[layer]))
    likes = {s: jnp.zeros((sum(sh.size for _, _, sh in by_sub[s]) // 1024, 1024), BF16) for s in by_sub}
    dw_shard = jnp.pad(conv_w_dw.reshape(-1, cb), ((0, -(n_conv * CONV_WIDTH) % 8), (0, 0)))
    like_dw = jnp.zeros(dw_shard.shape, F32)
    groups = [([dw_shard], [landing(dw_shard, dw_shard)])]
    for s in range(2 * depth):
        groups.append(([sh for _, _, sh in by_sub[s]], [landing(sh, sh) for _, _, sh in by_sub[s]]))
    gather, _ = _exchange_start("gather_start", groups, scatter=False, carry=jnp.zeros((8, 128), F32))
    dw_filters = []

    def fetch(s, after):
        pieces = [(gather[s + 1], likes[s])]
        if s == 0:
            pieces.append((gather[0], like_dw))
        landed = _exchange_wait(f"gather_wait{s}", pieces, after)
        out = {key: g.reshape(g.shape[0] * g.shape[1], g.shape[2]) for (key, _, _), g in zip(by_sub[s], landed[0])}
        if s == 0:
            dw_all = landed[1][0]
            full = jnp.transpose(dw_all[:, :n_conv * CONV_WIDTH].reshape(N_DEV, n_conv, CONV_WIDTH, cb), (1, 2, 0, 3))
            full = jnp.pad(full.reshape(n_conv, CONV_WIDTH, N_DEV * cb), ((0, 0), (0, CONV_HALO - CONV_WIDTH), (0, 0)))
            dw_filters.extend(full[layer] for layer in range(n_conv))
        if "w1t" in out:
            out["wdw"] = dw_filters[s // 4]
        return out

    scatter, dw_grads = {}, {}

    def emit(s, gd, carry):
        parts = [gd[key].reshape(N_DEV, -1, gd[key].shape[1]) for key, _, _ in by_sub[s]]
        if "wdw" in gd:
            dw_grads[s // 4] = gd["wdw"]
        if s == 0:
            parts.append(_dw_blocks(jnp.stack([dw_grads[layer] for layer in range(n_conv)])))
        lands = [landing(p[0], lax.dynamic_index_in_dim(p, me, 0, keepdims=False)) for p in parts]
        groups = [(parts[:len(by_sub[s])], lands[:len(by_sub[s])])]
        if s == 0:
            groups.append((parts[-1:], lands[-1:]))
        if s == 0:
            scatter[s], _ = _exchange_start(f"scatter_start{s}", groups, scatter=True, carry=jnp.zeros((8, 128), F32))
            return carry
        scatter[s], carry = _exchange_start(f"scatter_start{s}", groups, scatter=True, carry=carry)
        return carry

    sm = {n: w[n] for n in SMALL}
    loss_cols, dx, gsm = _local_step(x[0], loss_target[0], sm, depth, fetch, emit)
    loss = lax.psum(jnp.sum(loss_cols), ("x", "y", "c"))

    grads, summed, delta, new_m, new_v = {}, {}, {}, {}, {}

    def reduce_pieces(subs, landed):
        for s, recv in zip(subs, landed):
            for (key, layer, _), r in zip(by_sub[s], recv):
                summed[key, layer] = _sum8(f"sum_{key}{layer}", r)

    def update(names):
        for name in names:
            shape = w[name].shape
            res = _adamw(f"adamw_{name}",
                         *[t.reshape(-1, shape[-1]) for t in (grads[name], w[name], m[name], v[name])])
            delta[name], new_m[name], new_v[name] = (t.reshape(shape) for t in res)

    def stacked(name, key, cols):
        g = jnp.stack([summed[key, layer] for layer in range(w[name].shape[0])])
        return jnp.swapaxes(g, 1, 2) if cols else g

    early = sorted((s for s in scatter if s != 0), reverse=True)
    reduce_pieces(early, _exchange_wait("scatter_wait_early", [(scatter[s][0], likes[s]) for s in early], dx))
    late_names = [name for name, key, _ in SHARDED if any(k == key for k, _, _ in by_sub[0])]
    for name, key, cols in SHARDED:
        if name not in late_names:
            grads[name] = stacked(name, key, cols)
    early_names = [name for name, _, _ in SHARDED if name not in late_names]
    update(early_names)
    pack = _pack_small(gsm)
    ((pack_all,),) = _exchange("gather_small_grads", [([pack], pack)], scatter=False)
    pack_sum = _sum8("sum_small", pack_all)
    grads.update(_unpack_small(pack_sum, sm))

    landed = _exchange_wait("scatter_wait_last", [(scatter[0][0], likes[0]), (scatter[0][1], like_dw)],
                            new_v[early_names[-1]])
    reduce_pieces([0], landed[:1])
    for name, key, cols in SHARDED:
        if name in late_names:
            grads[name] = stacked(name, key, cols)
    grads["conv_w_dw"] = _sum8("sum_wdw", landed[1][0])[:n_conv * CONV_WIDTH].reshape(conv_w_dw.shape)
    update(late_names + ["conv_w_dw"])
    res = _adamw("adamw_small", pack_sum, _pack_small(sm), _pack_small({n: m[n] for n in SMALL}),
                 _pack_small({n: v[n] for n in SMALL}))
    for dst, t in zip((delta, new_m, new_v), res):
        dst.update(_unpack_small(t, sm))

    outs = [loss, dx[None]]
    for d in (grads, delta, new_m, new_v):
        outs += [d[n] for n in ORDER]
    return tuple(outs)
```

```python
import functools
import math

import numpy as np
import jax
import jax.numpy as jnp
from jax import lax
from jax.experimental import pallas as pl
from jax.experimental.pallas import tpu as pltpu

F32 = jnp.float32
BF16 = jnp.bfloat16

N_DEV = 8
HEAD_DIM = 64
HEADS_PER_GROUP = 4
GROUP_COLS = HEADS_PER_GROUP * HEAD_DIM
DILATIONS = (1, 4, 16)
N_BACK = 128
N_GROUPS = 3
N_HEADS = 12
D_ATTN = 768
N_BUCKETS = 32
REL_MAX_DISTANCE = 2048
CONV_WIDTH = 31
CONV_HALO = 32
EPS = 1e-6
NEG_INF = -1e30
ADAM_LR, ADAM_B1, ADAM_B2, ADAM_EPS, ADAM_WD, ADAM_STEP = 0.001, 0.9, 0.999, 1e-08, 0.01, 10
V7X_VMEM_LIMIT_BYTES = 56 * 1024 * 1024
MESH = pl.DeviceIdType.MESH
ANY = pl.BlockSpec(memory_space=pl.ANY)


def _pick(n, prefs):
    for p in prefs:
        if n % p == 0:
            return p
    return n


def _params(sem, vmem=None):
    return pltpu.CompilerParams(dimension_semantics=sem, vmem_limit_bytes=vmem)


def _sigmoid(x):
    return 1.0 / (1.0 + jnp.exp(-x))


def _exchange(name, groups, scatter):
    n_arr = [len(arrs) for arrs, _ in groups]
    n_in = sum(n_arr) + len(groups)
    ng = len(groups)

    def body(*refs):
        ins, outs, (send_sems, recv_sems, local_sems) = refs[:n_in], refs[n_in:-3], refs[-3:]
        x, y, c = lax.axis_index("x"), lax.axis_index("y"), lax.axis_index("c")
        me = 4 * x + 2 * y + c
        pos_in = pos_out = 0
        plans = []
        for gi in range(ng):
            srcs = ins[pos_in:pos_in + n_arr[gi]]
            like = ins[pos_in + n_arr[gi]]
            dsts = outs[pos_out:pos_out + n_arr[gi]]
            pos_in += n_arr[gi] + 1
            pos_out += n_arr[gi]
            plans.append((gi, srcs, like, dsts))
        local = []
        for gi, srcs, like, dsts in plans:
            for s, d in zip(srcs, dsts):
                cp = pltpu.make_async_copy(s.at[me] if scatter else s, d.at[me], local_sems.at[gi])
                cp.start()
                local.append(cp)
        for delta in range(1, N_DEV):
            dx, dy, dc = (delta >> 2) & 1, (delta >> 1) & 1, delta & 1
            px, py, pc = (1 - x if dx else x), (1 - y if dy else y), (1 - c if dc else c)
            peer = 4 * px + 2 * py + pc
            for gi, srcs, like, dsts in plans:
                for s, d in zip(srcs, dsts):
                    pltpu.make_async_remote_copy(
                        src_ref=s.at[peer] if scatter else s, dst_ref=d.at[me],
                        send_sem=send_sems.at[gi, delta - 1], recv_sem=recv_sems.at[gi, delta - 1],
                        device_id=(px, py, pc), device_id_type=MESH).start()
        for delta in range(1, N_DEV):
            for gi, srcs, like, dsts in plans:
                pltpu.make_async_remote_copy(
                    src_ref=like, dst_ref=like, send_sem=send_sems.at[gi, delta - 1],
                    recv_sem=recv_sems.at[gi, delta - 1], device_id=(x, y, c), device_id_type=MESH).wait()
        for cp in local:
            cp.wait()

    operands, out_shape = [], []
    for arrs, like in groups:
        operands += list(arrs) + [like]
        for a in arrs:
            blk = a.shape[1:] if scatter else a.shape
            out_shape.append(jax.ShapeDtypeStruct((N_DEV,) + tuple(blk), a.dtype))
    outs = pl.pallas_call(
        body, name=name, out_shape=out_shape, in_specs=[ANY] * len(operands), out_specs=[ANY] * len(out_shape),
        scratch_shapes=[pltpu.SemaphoreType.DMA((ng, N_DEV - 1)), pltpu.SemaphoreType.DMA((ng, N_DEV - 1)),
                        pltpu.SemaphoreType.DMA((ng,))],
        compiler_params=pltpu.CompilerParams(has_side_effects=True),
    )(*operands)
    res, pos = [], 0
    for n in n_arr:
        res.append(list(outs[pos:pos + n]))
        pos += n
    return res


HBM = pl.BlockSpec(memory_space=pltpu.HBM)
SEM = pl.BlockSpec(memory_space=pltpu.SEMAPHORE)
EFFECT = pltpu.SideEffectType.DATAFLOW_SIDE_EFFECTING


def _in_hbm(a):
    return pltpu.with_memory_space_constraint(a, pltpu.HBM)


def _exchange_start(name, groups, scatter, carry):
    ns = [len(s) for s, _ in groups]
    n_in = 2 * sum(ns)

    def body(*refs):
        ins, outs = refs[:n_in], refs[n_in + 1:]
        x, y, c = lax.axis_index("x"), lax.axis_index("y"), lax.axis_index("c")
        me = 4 * x + 2 * y + c
        pi = po = 0
        for n in ns:
            srcs, lands = ins[pi:pi + n], ins[pi + n:pi + 2 * n]
            send_sems, recv_sems = outs[po], outs[po + 1]
            pi += 2 * n
            po += 2 + 2 * n
            for delta in range(1, N_DEV):
                dx, dy, dc = (delta >> 2) & 1, (delta >> 1) & 1, delta & 1
                px, py, pc = (1 - x if dx else x), (1 - y if dy else y), (1 - c if dc else c)
                peer = 4 * px + 2 * py + pc
                for s, d in zip(srcs, lands):
                    pltpu.make_async_remote_copy(
                        src_ref=s.at[peer] if scatter else s, dst_ref=d.at[me], send_sem=send_sems.at[delta - 1],
                        recv_sem=recv_sems.at[delta - 1], device_id=(px, py, pc), device_id_type=MESH).start()

    operands, out_shape, out_specs, aliases = [], [], [], {}
    for srcs, lands in groups:
        out_shape += [pltpu.SemaphoreType.DMA((N_DEV - 1,))] * 2
        out_specs += [SEM, SEM]
        for a in list(srcs) + list(lands):
            aliases[len(operands)] = len(out_shape)
            operands.append(_in_hbm(a))
            out_shape.append(pltpu.HBM(a.shape, a.dtype))
            out_specs.append(HBM)
    aliases[len(operands)] = len(out_shape)
    operands.append(_in_hbm(carry))
    out_shape.append(pltpu.HBM(carry.shape, carry.dtype))
    out_specs.append(HBM)
    outs = pl.pallas_call(
        body, name=name, out_shape=out_shape, in_specs=[HBM] * len(operands), out_specs=out_specs,
        input_output_aliases=aliases, compiler_params=pltpu.CompilerParams(has_side_effects=EFFECT),
    )(*operands)
    handles, po = [], 0
    for n in ns:
        handles.append((outs[po], outs[po + 1], list(outs[po + 2:po + 2 + n]), list(outs[po + 2 + n:po + 2 + 2 * n])))
        po += 2 + 2 * n
    return handles, outs[-1]


def _exchange_wait(name, pieces, after):
    ns = [len(h[2]) for h, _ in pieces]

    def body(*refs):
        x, y, c = lax.axis_index("x"), lax.axis_index("y"), lax.axis_index("c")
        pi = 0
        for n in ns:
            send_sems, recv_sems, like = refs[pi + 2 * n], refs[pi + 2 * n + 1], refs[pi + 2 * n + 2]
            pi += 2 * n + 3
            for delta in range(1, N_DEV):
                cp = pltpu.make_async_remote_copy(
                    src_ref=like, dst_ref=like, send_sem=send_sems.at[delta - 1], recv_sem=recv_sems.at[delta - 1],
                    device_id=(x, y, c), device_id_type=MESH)
                cp.wait_send()
                cp.wait_recv()

    operands, in_specs, out_shape, aliases = [], [], [], {}
    for (send_sems, recv_sems, srcs, lands), like in pieces:
        for a in srcs + lands:
            aliases[len(operands)] = len(out_shape)
            operands.append(a)
            in_specs.append(HBM)
            out_shape.append(pltpu.HBM(a.shape, a.dtype))
        operands += [send_sems, recv_sems, like]
        in_specs += [SEM, SEM, ANY]
    operands.append(after)
    in_specs.append(ANY)
    outs = pl.pallas_call(
        body, name=name, out_shape=out_shape, in_specs=in_specs, out_specs=[HBM] * len(out_shape),
        input_output_aliases=aliases, compiler_params=pltpu.CompilerParams(has_side_effects=EFFECT),
    )(*operands)
    res, po = [], 0
    for n in ns:
        res.append(list(outs[po + n:po + 2 * n]))
        po += 2 * n
    return res


def _sum8(name, r):
    _, rows, cols = r.shape
    tr = _pick(rows, (256, 128, 64, 32, 16, 8))

    def body(r_ref, o_ref):
        acc = r_ref[0].astype(F32)
        for p in range(1, N_DEV):
            acc = acc + r_ref[p].astype(F32)
        o_ref[...] = acc

    return pl.pallas_call(
        body, name=name, out_shape=jax.ShapeDtypeStruct((rows, cols), F32), grid=(rows // tr,),
        in_specs=[pl.BlockSpec((N_DEV, tr, cols), lambda i: (0, i, 0))],
        out_specs=pl.BlockSpec((tr, cols), lambda i: (i, 0)), compiler_params=_params(("parallel",)),
    )(r)


def _adamw(name, g, w, m, v):
    rows, cols = g.shape
    tr = _pick(rows, (512, 256, 128, 64, 32, 16, 8))
    c1 = 1.0 - ADAM_B1 ** ADAM_STEP
    c2 = 1.0 - ADAM_B2 ** ADAM_STEP

    def body(g_ref, w_ref, m_ref, v_ref, d_ref, nm_ref, nv_ref):
        gv = g_ref[...]
        nm = ADAM_B1 * m_ref[...] + (1.0 - ADAM_B1) * gv
        nv = ADAM_B2 * v_ref[...] + (1.0 - ADAM_B2) * (gv * gv)
        d_ref[...] = -ADAM_LR * ((nm / c1) / (jnp.sqrt(nv / c2) + ADAM_EPS) + ADAM_WD * w_ref[...])
        nm_ref[...] = nm
        nv_ref[...] = nv

    spec = pl.BlockSpec((tr, cols), lambda i: (i, 0))
    return pl.pallas_call(
        body, name=name, out_shape=[jax.ShapeDtypeStruct((rows, cols), F32)] * 3, grid=(rows // tr,),
        in_specs=[spec] * 4, out_specs=[spec] * 3, compiler_params=_params(("parallel",)),
    )(g, w, m, v)


def _mm_nt(name, a, ws, w_offs, n, epi, out_dtypes, extras=(), rows=(), tm=None, tn=None):
    m, k = a.shape
    tm = tm or _pick(m, (512, 256, 128))
    tn = tn or _pick(n, (1408, 1152, 1024, 512, 256, 128))
    nw, ne, nr = len(ws), len(extras), len(rows)

    def body(*refs):
        a_ref, w_refs = refs[0], refs[1:1 + nw]
        e_refs, r_refs = refs[1 + nw:1 + nw + ne], refs[1 + nw + ne:1 + nw + ne + nr]
        o_refs = refs[1 + nw + ne + nr:]
        av = a_ref[...].astype(BF16)
        accs = [lax.dot_general(av, w[...], (((1,), (1,)), ((), ())), preferred_element_type=F32) for w in w_refs]
        outs = epi(accs, [e[...] for e in e_refs], [r[...] for r in r_refs])
        for o_ref, o in zip(o_refs, outs):
            o_ref[...] = o.astype(o_ref.dtype)

    in_specs = [pl.BlockSpec((tm, k), lambda j, i: (i, 0))]
    in_specs += [pl.BlockSpec((tn, k), functools.partial(lambda j, i, off: (j + off, 0), off=off)) for off in w_offs]
    in_specs += [pl.BlockSpec((tm, tn), lambda j, i: (i, j))] * ne
    in_specs += [pl.BlockSpec((1, tn), lambda j, i: (0, j))] * nr
    return pl.pallas_call(
        body, name=name, out_shape=[jax.ShapeDtypeStruct((m, n), dt) for dt in out_dtypes],
        grid=(n // tn, m // tm), in_specs=in_specs,
        out_specs=[pl.BlockSpec((tm, tn), lambda j, i: (i, j))] * len(out_dtypes),
        compiler_params=_params(("parallel", "parallel"), V7X_VMEM_LIMIT_BYTES),
    )(a, *ws, *extras, *rows)


def _mm_nn(name, as_, bs, epi, out_dtype, extras=(), rows=(), tm=None):
    m, k = as_[0].shape
    n = bs[0].shape[1]
    tm = tm or _pick(m, (512, 256, 128))
    npair, ne, nr = len(as_), len(extras), len(rows)

    def body(*refs):
        a_refs, b_refs = refs[:npair], refs[npair:2 * npair]
        e_refs, r_refs = refs[2 * npair:2 * npair + ne], refs[2 * npair + ne:2 * npair + ne + nr]
        o_ref = refs[-1]
        acc = None
        for a_ref, b_ref in zip(a_refs, b_refs):
            p = jnp.dot(a_ref[...].astype(BF16), b_ref[...], preferred_element_type=F32)
            acc = p if acc is None else acc + p
        o_ref[...] = epi(acc, [e[...] for e in e_refs], [r[...] for r in r_refs]).astype(o_ref.dtype)

    in_specs = [pl.BlockSpec((tm, k), lambda i: (i, 0))] * npair
    in_specs += [pl.BlockSpec((k, n), lambda i: (0, 0))] * npair
    in_specs += [pl.BlockSpec((tm, n), lambda i: (i, 0))] * ne
    in_specs += [pl.BlockSpec((1, n), lambda i: (0, 0))] * nr
    return pl.pallas_call(
        body, name=name, out_shape=jax.ShapeDtypeStruct((m, n), out_dtype), grid=(m // tm,), in_specs=in_specs,
        out_specs=pl.BlockSpec((tm, n), lambda i: (i, 0)),
        compiler_params=_params(("parallel",), V7X_VMEM_LIMIT_BYTES),
    )(*as_, *bs, *extras, *rows)


def _mm_nn_rms_bwd(name, as_, bs, x, g, dx_out, tm=None):
    m, k = as_[0].shape
    n = bs[0].shape[1]
    tm = tm or _pick(m, (512, 256, 128))
    npair = len(as_)

    def body(*refs):
        a_refs, b_refs = refs[:npair], refs[npair:2 * npair]
        x_ref, g_ref, dxo_ref, dx_ref, dg_ref = refs[2 * npair:]
        dh = None
        for a_ref, b_ref in zip(a_refs, b_refs):
            p = jnp.dot(a_ref[...].astype(BF16), b_ref[...], preferred_element_type=F32)
            dh = p if dh is None else dh + p
        xv = x_ref[...]
        r = lax.rsqrt(jnp.mean(xv * xv, axis=-1, keepdims=True) + EPS)
        yv = xv * r
        dy = dh * g_ref[...]
        dx_ref[...] = dxo_ref[...] + r * (dy - yv * jnp.mean(dy * yv, axis=-1, keepdims=True))

        @pl.when(pl.program_id(0) == 0)
        def _():
            dg_ref[...] = jnp.zeros_like(dg_ref)

        dg_ref[...] += jnp.sum(dh * yv, axis=0, keepdims=True)

    big = pl.BlockSpec((tm, n), lambda i: (i, 0))
    row = pl.BlockSpec((1, n), lambda i: (0, 0))
    in_specs = [pl.BlockSpec((tm, k), lambda i: (i, 0))] * npair + [pl.BlockSpec((k, n), lambda i: (0, 0))] * npair
    return pl.pallas_call(
        body, name=name, out_shape=[jax.ShapeDtypeStruct((m, n), F32), jax.ShapeDtypeStruct((1, n), F32)],
        grid=(m // tm,), in_specs=in_specs + [big, row, big], out_specs=[big, row],
        compiler_params=_params(("arbitrary",), V7X_VMEM_LIMIT_BYTES),
    )(*as_, *bs, x, g, dx_out)


def _mm_tn(name, a, b, colsum_b=False, tm=None, tk=2048):
    t, ma = a.shape
    nb = b.shape[1]
    tm = tm or _pick(ma, (1408, 1152, 1024, 768, 512, 256, 128))
    tk = _pick(t, (tk, 256, 128))
    nk = t // tk

    def body(*refs):
        a_ref, b_ref, o_ref = refs[0], refs[1], refs[2]
        acc_ref = refs[-1]
        kk = pl.program_id(1)
        bv = b_ref[...]

        @pl.when(kk == 0)
        def _():
            acc_ref[...] = jnp.zeros_like(acc_ref)

        acc_ref[...] += lax.dot_general(a_ref[...].astype(BF16), bv.astype(BF16), (((0,), (0,)), ((), ())),
                                        preferred_element_type=F32)
        if colsum_b:
            s_ref = refs[3]

            @pl.when((kk == 0) & (pl.program_id(0) == 0))
            def _():
                s_ref[...] = jnp.zeros_like(s_ref)

            @pl.when(pl.program_id(0) == 0)
            def _():
                s_ref[...] += jnp.sum(bv.astype(F32), axis=0, keepdims=True)

        @pl.when(kk == nk - 1)
        def _():
            o_ref[...] = acc_ref[...].astype(o_ref.dtype)

    out_shape = [jax.ShapeDtypeStruct((ma, nb), BF16)]
    out_specs = [pl.BlockSpec((tm, nb), lambda i, kk: (i, 0))]
    if colsum_b:
        out_shape.append(jax.ShapeDtypeStruct((1, nb), F32))
        out_specs.append(pl.BlockSpec((1, nb), lambda i, kk: (0, 0)))
    res = pl.pallas_call(
        body, name=name, out_shape=out_shape, grid=(ma // tm, nk),
        in_specs=[pl.BlockSpec((tk, tm), lambda i, kk: (kk, i)), pl.BlockSpec((tk, nb), lambda i, kk: (kk, 0))],
        out_specs=out_specs, scratch_shapes=[pltpu.VMEM((tm, nb), F32)],
        compiler_params=_params(("arbitrary", "arbitrary"), V7X_VMEM_LIMIT_BYTES),
    )(a, b)
    return res if colsum_b else res[0]


FFN_CHUNK = 256


def _ffn_down_bwd(name, dx, wd, gq, uq):
    t, d_model = dx.shape
    f = wd.shape[0]
    tm = _pick(t, (256, 128))
    ck = _pick(f, (FFN_CHUNK, 128))
    nt = (((1,), (1,)), ((), ()))
    tn = (((0,), (0,)), ((), ()))
    last = t // tm - 1

    def body(dx_ref, w_ref, g_ref, u_ref, dg_ref, du_ref, dw_ref, acc_ref):
        i = pl.program_id(0)

        @pl.when(i == 0)
        def _():
            acc_ref[...] = jnp.zeros_like(acc_ref)

        dxb = dx_ref[...].astype(BF16)
        for c0 in range(0, f, ck):
            cs = slice(c0, c0 + ck)
            dact = lax.dot_general(dxb, w_ref[cs, :], nt, preferred_element_type=F32)
            gf, uf = g_ref[:, cs].astype(F32), u_ref[:, cs].astype(F32)
            sg = _sigmoid(gf)
            dg_ref[:, cs] = (dact * uf * (sg * (1.0 + gf * (1.0 - sg)))).astype(BF16)
            du_ref[:, cs] = (dact * gf * sg).astype(BF16)
            act = (gf * sg * uf).astype(BF16)
            acc_ref[cs, :] += lax.dot_general(act, dxb, tn, preferred_element_type=F32)

        @pl.when(i == last)
        def _():
            dw_ref[...] = acc_ref[...].astype(BF16)

    wide = pl.BlockSpec((tm, f), lambda i: (i, 0))
    whole = pl.BlockSpec((f, d_model), lambda i: (0, 0))
    return pl.pallas_call(
        body, name=name,
        out_shape=[jax.ShapeDtypeStruct((t, f), BF16), jax.ShapeDtypeStruct((t, f), BF16),
                   jax.ShapeDtypeStruct((f, d_model), BF16)],
        grid=(t // tm,), in_specs=[pl.BlockSpec((tm, d_model), lambda i: (i, 0)), whole, wide, wide],
        out_specs=[wide, wide, whole], scratch_shapes=[pltpu.VMEM((f, d_model), F32)],
        compiler_params=_params(("arbitrary",), V7X_VMEM_LIMIT_BYTES),
    )(dx, wd, gq, uq)


def _rmsnorm_fwd(name, x, g):
    t, d = x.shape
    tr = _pick(t, (512, 256, 128))

    def body(x_ref, g_ref, h_ref):
        xv = x_ref[...]
        r = lax.rsqrt(jnp.mean(xv * xv, axis=-1, keepdims=True) + EPS)
        h_ref[...] = (xv * r * g_ref[...]).astype(BF16)

    return pl.pallas_call(
        body, name=name, out_shape=jax.ShapeDtypeStruct((t, d), BF16), grid=(t // tr,),
        in_specs=[pl.BlockSpec((tr, d), lambda i: (i, 0)), pl.BlockSpec((1, d), lambda i: (0, 0))],
        out_specs=pl.BlockSpec((tr, d), lambda i: (i, 0)), compiler_params=_params(("parallel",)),
    )(x, g)


def _loss_head(name, x, g, target):
    t, d = x.shape
    tr = _pick(t, (512, 256, 128))

    def body(x_ref, g_ref, t_ref, dx_ref, dg_ref, l_ref):
        xv = x_ref[...]
        r = lax.rsqrt(jnp.mean(xv * xv, axis=-1, keepdims=True) + EPS)
        yv = xv * r
        diff = yv * g_ref[...] - t_ref[...]
        dout = diff * (1.0 / d)
        dy = dout * g_ref[...]
        dx_ref[...] = r * (dy - yv * jnp.mean(dy * yv, axis=-1, keepdims=True))

        @pl.when(pl.program_id(0) == 0)
        def _():
            dg_ref[...] = jnp.zeros_like(dg_ref)
            l_ref[...] = jnp.zeros_like(l_ref)

        dg_ref[...] += jnp.sum(dout * yv, axis=0, keepdims=True)
        l_ref[...] += (0.5 / d) * jnp.sum(diff * diff, axis=0, keepdims=True)

    big = pl.BlockSpec((tr, d), lambda i: (i, 0))
    row = pl.BlockSpec((1, d), lambda i: (0, 0))
    return pl.pallas_call(
        body, name=name,
        out_shape=[jax.ShapeDtypeStruct((t, d), F32), jax.ShapeDtypeStruct((1, d), F32),
                   jax.ShapeDtypeStruct((1, d), F32)],
        grid=(t // tr,), in_specs=[big, row, big], out_specs=[big, row, row],
        compiler_params=_params(("arbitrary",)),
    )(x, g, target)


CONV_ROWS = 64
SUBLANES = 8


def _fill_window(win_ref, sh_ref, parts):
    rows = sh_ref.shape[2]
    for cb in range(win_ref.shape[0]):
        for r0, val in parts:
            win_ref[cb, r0:r0 + val.shape[0], :] = val[:, 128 * cb:128 * (cb + 1)]
        win_ref[cb, rows:rows + SUBLANES, :] = jnp.zeros((SUBLANES, 128), F32)
        for b in range(1, SUBLANES):
            sh_ref[b - 1, cb] = win_ref[cb, b:b + rows, :]


def _window_rows(win_ref, sh_ref, o, cb):
    b = o % SUBLANES
    if b == 0:
        return win_ref[cb, o:o + CONV_ROWS, :]
    return sh_ref[b - 1, cb, o - b:o - b + CONV_ROWS, :]


def _window_scratch(rows, c):
    return [pltpu.VMEM((c // 128, rows + SUBLANES, 128), F32), pltpu.VMEM((SUBLANES - 1, c // 128, rows, 128), F32)]


def _dwconv_fwd(name, glu, w_dw, b_dw, ln_g, ln_b):
    t, c = glu.shape
    tt = _pick(t, (256, 128))
    hb = tt // CONV_HALO

    def body(cur_ref, halo_ref, w_ref, b_ref, g_ref, be_ref, dw_ref, s_ref, win_ref, sh_ref):
        i = pl.program_id(0)
        halo = jnp.where(i > 0, halo_ref[...].astype(F32), 0.0)
        _fill_window(win_ref, sh_ref, [(0, halo), (CONV_HALO, cur_ref[...].astype(F32))])
        for r0 in range(0, tt, CONV_ROWS):
            for cb in range(c // 128):
                c0 = 128 * cb
                acc = jnp.zeros((CONV_ROWS, 128), F32) + b_ref[:, c0:c0 + 128]
                for k in range(CONV_WIDTH):
                    o = r0 + k + CONV_HALO - (CONV_WIDTH - 1)
                    acc = acc + w_ref[k:k + 1, c0:c0 + 128] * _window_rows(win_ref, sh_ref, o, cb)
                dw_ref[r0:r0 + CONV_ROWS, c0:c0 + 128] = acc
        u = dw_ref[...]
        mu = jnp.mean(u, axis=-1, keepdims=True)
        uc = u - mu
        rstd = lax.rsqrt(jnp.mean(uc * uc, axis=-1, keepdims=True) + EPS)
        z = uc * rstd * g_ref[...] + be_ref[...]
        s_ref[...] = (z * _sigmoid(z)).astype(BF16)

    big = pl.BlockSpec((tt, c), lambda i: (i, 0))
    row = pl.BlockSpec((1, c), lambda i: (0, 0))
    return pl.pallas_call(
        body, name=name, out_shape=[jax.ShapeDtypeStruct((t, c), F32), jax.ShapeDtypeStruct((t, c), BF16)],
        grid=(t // tt,),
        in_specs=[big, pl.BlockSpec((CONV_HALO, c), lambda i: (jnp.maximum(i * hb - 1, 0), 0)),
                  pl.BlockSpec((CONV_HALO, c), lambda i: (0, 0)), row, row, row],
        out_specs=[big, big], scratch_shapes=_window_scratch(tt + CONV_HALO, c),
        compiler_params=_params(("parallel",), V7X_VMEM_LIMIT_BYTES),
    )(glu, glu, w_dw, b_dw, ln_g, ln_b)


def _ln_silu_bwd(name, dx, w2, dw, ln_g, ln_b):
    t, c = dw.shape
    d_model = dx.shape[1]
    tr = _pick(t, (256, 128))

    def body(dx_ref, w_ref, dw_ref, g_ref, be_ref, o_ref, acc_ref):
        ds = lax.dot_general(dx_ref[...].astype(BF16), w_ref[...], (((1,), (1,)), ((), ())),
                             preferred_element_type=F32)
        u = dw_ref[...]
        mu = jnp.mean(u, axis=-1, keepdims=True)
        uc = u - mu
        rstd = lax.rsqrt(jnp.mean(uc * uc, axis=-1, keepdims=True) + EPS)
        xh = uc * rstd
        z = xh * g_ref[...] + be_ref[...]
        sg = _sigmoid(z)
        dz = ds * (sg * (1.0 + z * (1.0 - sg)))
        dxh = dz * g_ref[...]
        du = rstd * (dxh - jnp.mean(dxh, axis=-1, keepdims=True) - xh * jnp.mean(dxh * xh, axis=-1, keepdims=True))
        o_ref[...] = du

        @pl.when(pl.program_id(0) == 0)
        def _():
            acc_ref[...] = jnp.zeros_like(acc_ref)

        acc_ref[0:1, :] += jnp.sum(dz * xh, axis=0, keepdims=True)
        acc_ref[1:2, :] += jnp.sum(dz, axis=0, keepdims=True)
        acc_ref[2:3, :] += jnp.sum(du, axis=0, keepdims=True)

    big = pl.BlockSpec((tr, c), lambda i: (i, 0))
    row = pl.BlockSpec((1, c), lambda i: (0, 0))
    return pl.pallas_call(
        body, name=name, out_shape=[jax.ShapeDtypeStruct((t, c), F32), jax.ShapeDtypeStruct((8, c), F32)],
        grid=(t // tr,),
        in_specs=[pl.BlockSpec((tr, d_model), lambda i: (i, 0)), pl.BlockSpec((c, d_model), lambda i: (0, 0)),
                  big, row, row],
        out_specs=[big, pl.BlockSpec((8, c), lambda i: (0, 0))],
        compiler_params=_params(("arbitrary",), V7X_VMEM_LIMIT_BYTES),
    )(dx, w2, dw, ln_g, ln_b)


def _dwconv_bwd(name, ddw, a, gt, w_dw):
    t, c = ddw.shape
    tt = _pick(t, (256, 128))
    hb = tt // CONV_HALO
    last = t // tt - 1
    back = CONV_WIDTH - 1

    def body(d_ref, dn_ref, a_ref, ap_ref, g_ref, gp_ref, w_ref, du_ref, dwk_ref, db_ref,
             wd_ref, shd_ref, wg_ref, shg_ref, dg_ref, dwk8_ref):
        i = pl.program_id(0)
        _fill_window(wd_ref, shd_ref, [(0, d_ref[...]), (tt, jnp.where(i < last, dn_ref[...], 0.0))])
        glu_prev = ap_ref[...].astype(F32) * _sigmoid(gp_ref[...].astype(F32))
        av = a_ref[...].astype(F32)
        sg = _sigmoid(g_ref[...].astype(F32))
        _fill_window(wg_ref, shg_ref, [(0, jnp.where(i > 0, glu_prev, 0.0)), (CONV_HALO, av * sg)])

        @pl.when(i == 0)
        def _():
            dwk8_ref[...] = jnp.zeros_like(dwk8_ref)
            db_ref[...] = jnp.zeros_like(db_ref)

        for r0 in range(0, tt, CONV_ROWS):
            for cb in range(c // 128):
                c0 = 128 * cb
                dcur = wd_ref[cb, r0:r0 + CONV_ROWS, :]
                acc = jnp.zeros((CONV_ROWS, 128), F32)
                for k in range(CONV_WIDTH):
                    acc = acc + w_ref[k:k + 1, c0:c0 + 128] * _window_rows(wd_ref, shd_ref, r0 + back - k, cb)
                    p = dcur * _window_rows(wg_ref, shg_ref, r0 + k + CONV_HALO - back, cb)
                    s8 = p[0:SUBLANES]
                    for q in range(SUBLANES, CONV_ROWS, SUBLANES):
                        s8 = s8 + p[q:q + SUBLANES]
                    dwk8_ref[SUBLANES * k:SUBLANES * (k + 1), c0:c0 + 128] += s8
                dg_ref[r0:r0 + CONV_ROWS, c0:c0 + 128] = acc

        @pl.when(i == last)
        def _():
            for k in range(CONV_WIDTH):
                dwk_ref[k:k + 1, :] = jnp.sum(dwk8_ref[SUBLANES * k:SUBLANES * (k + 1), :], axis=0, keepdims=True)
            dwk_ref[CONV_WIDTH:, :] = jnp.zeros((CONV_HALO - CONV_WIDTH, c), F32)
        dglu = dg_ref[...]
        da = dglu * sg
        dgate = dglu * av * sg * (1.0 - sg)
        du_ref[:, 0:c] = da.astype(BF16)
        du_ref[:, c:] = dgate.astype(BF16)
        db_ref[:, 0:c] += jnp.sum(da, axis=0, keepdims=True)
        db_ref[:, c:] += jnp.sum(dgate, axis=0, keepdims=True)

    big = pl.BlockSpec((tt, c), lambda i: (i, 0))
    prev = pl.BlockSpec((CONV_HALO, c), lambda i: (jnp.maximum(i * hb - 1, 0), 0))
    nxt = pl.BlockSpec((CONV_HALO, c), lambda i: (jnp.minimum((i + 1) * hb, t // CONV_HALO - 1), 0))
    return pl.pallas_call(
        body, name=name,
        out_shape=[jax.ShapeDtypeStruct((t, 2 * c), BF16), jax.ShapeDtypeStruct((CONV_HALO, c), F32),
                   jax.ShapeDtypeStruct((1, 2 * c), F32)],
        grid=(t // tt,),
        in_specs=[big, nxt, big, prev, big, prev, pl.BlockSpec((CONV_HALO, c), lambda i: (0, 0))],
        out_specs=[pl.BlockSpec((tt, 2 * c), lambda i: (i, 0)), pl.BlockSpec((CONV_HALO, c), lambda i: (0, 0)),
                   pl.BlockSpec((1, 2 * c), lambda i: (0, 0))],
        scratch_shapes=_window_scratch(tt + CONV_HALO, c) + _window_scratch(tt + CONV_HALO, c)
        + [pltpu.VMEM((tt, c), F32), pltpu.VMEM((SUBLANES * CONV_HALO, c), F32)],
        compiler_params=_params(("arbitrary",), V7X_VMEM_LIMIT_BYTES),
    )(ddw, ddw, a, a, gt, gt, w_dw)


def _bucket_tables():
    i = np.arange(N_BACK)[:, None]
    j = np.arange(2 * N_BACK)[None, :]
    dist = i + N_BACK - j
    valid = (dist >= 0) & (dist <= N_BACK)
    max_exact = N_BUCKETS // 2
    out = []
    for d in DILATIONS:
        n = np.maximum(dist * d, 0)
        nf = np.maximum(n, 1).astype(np.float32)
        large = max_exact + (np.log(nf / np.float32(max_exact)) / np.float32(math.log(REL_MAX_DISTANCE / max_exact))
                             * np.float32(N_BUCKETS - max_exact)).astype(np.int32)
        large = np.minimum(large, N_BUCKETS - 1)
        out.append(np.where(valid, np.where(n < max_exact, n, large), -1))
    return np.stack(out).astype(np.int32)


def _bias_build(name, rel_bias, buckets):
    def body(tbl_ref, bk_ref, o_ref):
        g = pl.program_id(0)
        bk = bk_ref[0]
        for h in range(HEADS_PER_GROUP):
            acc = jnp.zeros(bk.shape, F32)
            for b in range(N_BUCKETS):
                acc = jnp.where(bk == b, tbl_ref[b, g * HEADS_PER_GROUP + h], acc)
            o_ref[h] = jnp.where(bk < 0, NEG_INF, acc)

    return pl.pallas_call(
        body, name=name, out_shape=jax.ShapeDtypeStruct((N_HEADS, N_BACK, 2 * N_BACK), F32), grid=(N_GROUPS,),
        in_specs=[pl.BlockSpec(memory_space=pltpu.SMEM), pl.BlockSpec((1, N_BACK, 2 * N_BACK), lambda g: (g, 0, 0))],
        out_specs=pl.BlockSpec((HEADS_PER_GROUP, N_BACK, 2 * N_BACK), lambda g: (g, 0, 0)),
        compiler_params=_params(("arbitrary",)),
    )(rel_bias, buckets)


def _bias_grad(name, dbs, buckets):
    nd = len(dbs)

    def body(*refs):
        bk = refs[nd][0]
        o_ref = refs[nd + 1]
        lane = lax.broadcasted_iota(jnp.int32, (1, 128), 1)
        db = [sum(r[h] for r in refs[:nd]) for h in range(HEADS_PER_GROUP)]
        for b in range(N_BUCKETS):
            row = jnp.zeros((1, 128), F32)
            for h in range(HEADS_PER_GROUP):
                s = jnp.sum(jnp.where(bk == b, db[h], 0.0), axis=0, keepdims=True)
                s = jnp.sum(s, axis=1, keepdims=True)
                row = jnp.where(lane // 32 == h, s, row)
            o_ref[0, b:b + 1, :] = row

    spec = pl.BlockSpec((HEADS_PER_GROUP, N_BACK, 2 * N_BACK), lambda g: (g, 0, 0))
    return pl.pallas_call(
        body, name=name, out_shape=jax.ShapeDtypeStruct((N_GROUPS, N_BUCKETS, 128), F32), grid=(N_GROUPS,),
        in_specs=[spec] * nd + [pl.BlockSpec((1, N_BACK, 2 * N_BACK), lambda g: (g, 0, 0))],
        out_specs=pl.BlockSpec((1, N_BUCKETS, 128), lambda g: (g, 0, 0)), compiler_params=_params(("arbitrary",)),
    )(*dbs, buckets)


def _head_cols(h):
    return slice(h * HEAD_DIM, (h + 1) * HEAD_DIM)


def _attn_fwd(name, qkv, bias, g):
    t = qkv.shape[0]
    d = DILATIONS[g]
    tq = t // d
    nblk = qkv.shape[1] // GROUP_COLS
    scale = HEAD_DIM ** -0.5

    def body(q_ref, kp_ref, kc_ref, vp_ref, vc_ref, b_ref, o_ref, l_ref):
        m2 = pl.program_id(1)
        col = lax.broadcasted_iota(jnp.int32, (N_BACK, 2 * N_BACK), 1)
        lane = lax.broadcasted_iota(jnp.int32, (N_BACK, 128), 1)
        for sub in range(2):
            rows = slice(N_BACK * sub, N_BACK * (sub + 1))
            lse_tile = jnp.zeros((N_BACK, 128), F32)
            outs = []
            for h in range(HEADS_PER_GROUP):
                hc = _head_cols(h)
                if sub == 0:
                    kk = jnp.concatenate([kp_ref[:, hc], kc_ref[0:N_BACK, hc]], axis=0)
                    vv = jnp.concatenate([vp_ref[:, hc], vc_ref[0:N_BACK, hc]], axis=0)
                else:
                    kk, vv = kc_ref[:, hc], vc_ref[:, hc]
                s = lax.dot_general(q_ref[rows, hc], kk, (((1,), (1,)), ((), ())), preferred_element_type=F32)
                s = s * scale + b_ref[h]
                if sub == 0:
                    s = jnp.where((col >= N_BACK) | (m2 > 0), s, NEG_INF)
                m = jnp.max(s, axis=-1, keepdims=True)
                p = jnp.exp(s - m)
                den = jnp.sum(p, axis=-1, keepdims=True)
                outs.append(jnp.dot(p.astype(BF16), vv, preferred_element_type=F32) / den)
                lse_tile = jnp.where(lane // 32 == h, m + jnp.log(den), lse_tile)
            o_ref[rows, :] = jnp.concatenate(outs, axis=1)
            l_ref[rows, :] = lse_tile

    def blk(part, prev):
        if prev:
            return pl.BlockSpec((N_BACK, GROUP_COLS), lambda r, n: (jnp.maximum(2 * n - 1, 0), r * nblk + part))
        return pl.BlockSpec((2 * N_BACK, GROUP_COLS), lambda r, n: (n, r * nblk + part))

    qv = qkv.reshape(tq, d * qkv.shape[1])
    o, l = pl.pallas_call(
        body, name=name,
        out_shape=[jax.ShapeDtypeStruct((tq, d * GROUP_COLS), F32), jax.ShapeDtypeStruct((tq, d * 128), F32)],
        grid=(d, tq // (2 * N_BACK)),
        in_specs=[blk(0, False), blk(1, True), blk(1, False), blk(2, True), blk(2, False),
                  pl.BlockSpec((HEADS_PER_GROUP, N_BACK, 2 * N_BACK), lambda r, n: (g, 0, 0))],
        out_specs=[pl.BlockSpec((2 * N_BACK, GROUP_COLS), lambda r, n: (n, r)),
                   pl.BlockSpec((2 * N_BACK, 128), lambda r, n: (n, r))],
        compiler_params=_params(("parallel", "parallel")),
    )(qv, qv, qv, qv, qv, bias)
    return o.reshape(t, GROUP_COLS), l.reshape(t, 128)


def _group_weights(l_refs, h):
    ls = [l_ref[:, 32 * h:32 * h + 1] for l_ref in l_refs]
    m = jnp.maximum(jnp.maximum(ls[0], ls[1]), ls[2])
    es = [jnp.exp(l - m) for l in ls]
    tot = es[0] + es[1] + es[2]
    return [e / tot for e in es]


def _attn_merge_out(name, os_, ls, wot, x):
    t, d_model = x.shape
    tr = _pick(t, (512, 256, 128))

    def body(o0, o1, o2, l0, l1, l2, w_ref, x_ref, om_ref, out_ref):
        o_refs = (o0, o1, o2)
        pieces = [[None] * HEADS_PER_GROUP for _ in range(N_GROUPS)]
        for h in range(HEADS_PER_GROUP):
            al = _group_weights((l0, l1, l2), h)
            for g in range(N_GROUPS):
                pieces[g][h] = o_refs[g][:, _head_cols(h)] * al[g]
        om = jnp.concatenate([p for row in pieces for p in row], axis=1).astype(BF16)
        om_ref[...] = om
        out_ref[...] = x_ref[...] + lax.dot_general(om, w_ref[...], (((1,), (1,)), ((), ())),
                                                    preferred_element_type=F32)

    so = pl.BlockSpec((tr, GROUP_COLS), lambda i: (i, 0))
    sl = pl.BlockSpec((tr, 128), lambda i: (i, 0))
    sx = pl.BlockSpec((tr, d_model), lambda i: (i, 0))
    return pl.pallas_call(
        body, name=name,
        out_shape=[jax.ShapeDtypeStruct((t, D_ATTN), BF16), jax.ShapeDtypeStruct((t, d_model), F32)],
        grid=(t // tr,), in_specs=[so] * 3 + [sl] * 3 + [pl.BlockSpec((d_model, D_ATTN), lambda i: (0, 0)), sx],
        out_specs=[pl.BlockSpec((tr, D_ATTN), lambda i: (i, 0)), sx],
        compiler_params=_params(("parallel",), V7X_VMEM_LIMIT_BYTES),
    )(*os_, *ls, wot, x)


def _attn_bwd_prep(name, dx, wot, os_, ls):
    t, d_model = dx.shape
    tr = _pick(t, (512, 256, 128))

    def body(dx_ref, w_ref, o0, o1, o2, l0, l1, l2, d0, d1, d2, c0, c1, c2):
        o_refs, d_refs, c_refs = (o0, o1, o2), (d0, d1, d2), (c0, c1, c2)
        d_out = jnp.dot(dx_ref[...].astype(BF16), w_ref[...], preferred_element_type=F32)
        lane = lax.broadcasted_iota(jnp.int32, (tr, 128), 1)
        dos = [[None] * HEADS_PER_GROUP for _ in range(N_GROUPS)]
        cs = [jnp.zeros((tr, 128), F32) for _ in range(N_GROUPS)]
        for h in range(HEADS_PER_GROUP):
            al = _group_weights((l0, l1, l2), h)
            tot = jnp.zeros((tr, 1), F32)
            for g in range(N_GROUPS):
                dv = d_out[:, g * GROUP_COLS + h * HEAD_DIM:g * GROUP_COLS + (h + 1) * HEAD_DIM]
                tot = tot + al[g] * jnp.sum(dv * o_refs[g][:, _head_cols(h)], axis=-1, keepdims=True)
                dos[g][h] = dv * al[g]
            for g in range(N_GROUPS):
                cs[g] = jnp.where(lane // 32 == h, -al[g] * tot, cs[g])
        for g in range(N_GROUPS):
            d_refs[g][...] = jnp.concatenate(dos[g], axis=1).astype(BF16)
            c_refs[g][...] = cs[g]

    so = pl.BlockSpec((tr, GROUP_COLS), lambda i: (i, 0))
    sl = pl.BlockSpec((tr, 128), lambda i: (i, 0))
    res = pl.pallas_call(
        body, name=name,
        out_shape=[jax.ShapeDtypeStruct((t, GROUP_COLS), BF16)] * 3 + [jax.ShapeDtypeStruct((t, 128), F32)] * 3,
        grid=(t // tr,),
        in_specs=[pl.BlockSpec((tr, d_model), lambda i: (i, 0)), pl.BlockSpec((d_model, D_ATTN), lambda i: (0, 0))]
        + [so] * 3 + [sl] * 3,
        out_specs=[so] * 3 + [sl] * 3, compiler_params=_params(("parallel",), V7X_VMEM_LIMIT_BYTES),
    )(dx, wot, *os_, *ls)
    return res[:3], res[3:]


def _attn_bwd(name, qkv, do, lse, cterm, bias, g):
    t = qkv.shape[0]
    d = DILATIONS[g]
    tq = t // d
    nb = tq // N_BACK
    nblk = qkv.shape[1] // GROUP_COLS
    scale = HEAD_DIM ** -0.5
    nt = (((1,), (1,)), ((), ()))
    tn = (((0,), (0,)), ((), ()))

    def body(q2, qx, kp, k2, vp, v2, do2, dox, l2, lx, c2, cx, b_ref, dqkv_ref, db_ref):
        m2 = pl.program_id(1)

        @pl.when((m2 == 0) & (pl.program_id(0) == 0))
        def _():
            db_ref[...] = jnp.zeros_like(db_ref)

        row2 = lax.broadcasted_iota(jnp.int32, (2 * N_BACK, N_BACK), 0)
        lo, hi = slice(0, N_BACK), slice(N_BACK, 2 * N_BACK)
        for sub in range(2):
            rows = hi if sub else lo
            has_prev = True if sub else m2 > 0
            has_next = 2 * m2 + 2 < nb if sub else True
            on_ac = (row2 < N_BACK) | has_next
            dqs, dks, dvs = [], [], []
            for h in range(HEADS_PER_GROUP):
                hc = _head_cols(h)
                st = slice(32 * h, 32 * h + 1)
                b_prev, b_same = b_ref[h, :, 0:N_BACK], b_ref[h, :, N_BACK:]
                q0, k1, v1, d0 = q2[rows, hc], k2[rows, hc], v2[rows, hc], do2[rows, hc]
                l0, c0 = l2[rows, st], c2[rows, st]
                if sub:
                    k0, v0 = k2[lo, hc], v2[lo, hc]
                    q_ac = jnp.concatenate([q0, qx[:, hc]], axis=0)
                    d_ac = jnp.concatenate([d0, dox[:, hc]], axis=0)
                    l_ac = jnp.concatenate([l0, lx[:, st]], axis=0)
                    c_ac = jnp.concatenate([c0, cx[:, st]], axis=0)
                else:
                    k0, v0 = kp[:, hc], vp[:, hc]
                    q_ac, d_ac, l_ac, c_ac = q2[:, hc], do2[:, hc], l2[:, st], c2[:, st]
                s_ac = (lax.dot_general(q_ac, k1, nt, preferred_element_type=F32) * scale
                        + jnp.concatenate([b_same, b_prev], axis=0))
                p_ac = jnp.where(on_ac, jnp.exp(s_ac - l_ac), 0.0)
                ds_ac = p_ac * (lax.dot_general(d_ac, v1, nt, preferred_element_type=F32) + c_ac)
                s_b = lax.dot_general(q0, k0, nt, preferred_element_type=F32) * scale + b_prev
                p_b = jnp.where(has_prev, jnp.exp(s_b - l0), 0.0)
                ds_b = p_b * (lax.dot_general(d0, v0, nt, preferred_element_type=F32) + c0)
                ds_a = ds_ac[0:N_BACK]
                dqs.append(scale * jnp.dot(jnp.concatenate([ds_b, ds_a], axis=1).astype(BF16),
                                           jnp.concatenate([k0, k1], axis=0), preferred_element_type=F32))
                dks.append(scale * lax.dot_general(ds_ac.astype(BF16), q_ac, tn, preferred_element_type=F32))
                dvs.append(lax.dot_general(p_ac.astype(BF16), d_ac, tn, preferred_element_type=F32))
                db_ref[h, :, 0:N_BACK] += ds_b
                db_ref[h, :, N_BACK:] += ds_a
            dqkv_ref[rows, :] = jnp.concatenate(dqs + dks + dvs, axis=1).astype(BF16)

    def blk(width, which, col):
        if which == "prev":
            return pl.BlockSpec((N_BACK, width), lambda r, n: (jnp.maximum(2 * n - 1, 0), col(r)))
        if which == "next":
            return pl.BlockSpec((N_BACK, width), lambda r, n: (jnp.minimum(2 * n + 2, nb - 1), col(r)))
        return pl.BlockSpec((2 * N_BACK, width), lambda r, n: (n, col(r)))

    def qkv_blk(part, which):
        return blk(GROUP_COLS, which, lambda r: r * nblk + part)

    def grp_blk(width, which):
        return blk(width, which, lambda r: r)

    qv = qkv.reshape(tq, d * qkv.shape[1])
    dov = do.reshape(tq, d * GROUP_COLS)
    lv = lse.reshape(tq, d * 128)
    cv = cterm.reshape(tq, d * 128)
    dqkv_g, db = pl.pallas_call(
        body, name=name,
        out_shape=[jax.ShapeDtypeStruct((tq, d * 3 * GROUP_COLS), BF16),
                   jax.ShapeDtypeStruct((HEADS_PER_GROUP, N_BACK, 2 * N_BACK), F32)],
        grid=(d, nb // 2),
        in_specs=[qkv_blk(0, "same"), qkv_blk(0, "next"), qkv_blk(1, "prev"), qkv_blk(1, "same"),
                  qkv_blk(2, "prev"), qkv_blk(2, "same"), grp_blk(GROUP_COLS, "same"), grp_blk(GROUP_COLS, "next"),
                  grp_blk(128, "same"), grp_blk(128, "next"), grp_blk(128, "same"), grp_blk(128, "next"),
                  pl.BlockSpec((HEADS_PER_GROUP, N_BACK, 2 * N_BACK), lambda r, n: (g, 0, 0))],
        out_specs=[pl.BlockSpec((2 * N_BACK, 3 * GROUP_COLS), lambda r, n: (n, r)),
                   pl.BlockSpec((HEADS_PER_GROUP, N_BACK, 2 * N_BACK), lambda r, n: (0, 0, 0))],
        compiler_params=_params(("arbitrary", "arbitrary")),
    )(qv, qv, qv, qv, qv, qv, dov, dov, lv, lv, cv, cv, bias)
    return dqkv_g.reshape(t, 3 * GROUP_COLS), db


def _row(v):
    return v.reshape(1, -1)


def _glu_epi(accs, extras, rows):
    a = (accs[0] + rows[0]).astype(BF16)
    gt = (accs[1] + rows[1]).astype(BF16)
    return a, gt, a.astype(F32) * _sigmoid(gt.astype(F32))


def _swiglu_epi(accs, extras, rows):
    gq, uq = accs[0].astype(BF16), accs[1].astype(BF16)
    gf = gq.astype(F32)
    return gq, uq, gf * _sigmoid(gf) * uq.astype(F32)


def _residual(acc, extras, rows):
    out = acc + extras[0]
    return out + rows[0] if rows else out


def _group_rows(w):
    parts = [w[p * D_ATTN:(p + 1) * D_ATTN].reshape(N_GROUPS, GROUP_COLS, -1) for p in range(3)]
    return jnp.concatenate(parts, axis=1)


def _ungroup_rows(wg):
    return jnp.concatenate([wg[g][p * GROUP_COLS:(p + 1) * GROUP_COLS] for p in range(3) for g in range(N_GROUPS)],
                           axis=0)


def _local_step(x, target, sm, depth, fetch, emit):
    d_model = x.shape[1]
    buckets = jnp.asarray(_bucket_tables())
    bias = _bias_build("bias_build", sm["rel_bias"], buckets)
    saved = []
    for i in range(depth):
        j = i // 2
        rec = {"x_mix": x}
        wm = fetch(2 * i, x)
        h = _rmsnorm_fwd(f"rms_mix_fwd{i}", x, _row(sm["norm_mix"][i]))
        rec.update(h_mix=h, wm=wm)
        if i % 2 == 0:
            c = wm["w2"].shape[0]
            tn = _pick(c, (512, 256, 128))
            b1 = sm["conv_b_pw1"][j]
            a, gt, glu = _mm_nt(f"conv_pw1_fwd{j}", h, [wm["w1t"]] * 2, [0, c // tn], c, _glu_epi, (BF16,) * 3,
                                rows=[_row(b1[:c]), _row(b1[c:])], tn=tn)
            dw, s = _dwconv_fwd(f"dwconv_fwd{j}", glu, wm["wdw"], _row(sm["conv_b_dw"][j]),
                                _row(sm["conv_ln_g"][j]), _row(sm["conv_ln_b"][j]))
            x = _mm_nn(f"conv_pw2_fwd{j}", [s], [wm["w2"]], _residual, F32, extras=[x],
                       rows=[_row(sm["conv_b_pw2"][j])])
            rec.update(a=a, gt=gt, dw=dw, s=s)
        else:
            wq = _group_rows(wm["wqkvt"])
            qkv = _mm_nt(f"attn_qkv_fwd{j}", h, [wq[g] for g in range(N_GROUPS)], [0] * N_GROUPS, D_ATTN,
                         lambda accs, e, r: tuple(accs), (BF16,) * N_GROUPS, tn=D_ATTN)
            og = [_attn_fwd(f"attn_fwd{j}_{g}", qkv[g], bias, g) for g in range(N_GROUPS)]
            os_, ls = [o for o, _ in og], [l for _, l in og]
            om, x = _attn_merge_out(f"attn_out_fwd{j}", os_, ls, wm["wot"], x)
            rec.update(qkv=qkv, os=os_, ls=ls, om=om, wq=wq)
        rec["x_ffn"] = x
        wf = fetch(2 * i + 1, x)
        h2 = _rmsnorm_fwd(f"rms_ffn_fwd{i}", x, _row(sm["norm_ffn"][i]))
        f = wf["wd"].shape[0]
        gq, uq, act = _mm_nt(f"ffn_up_fwd{i}", h2, [wf["wgt"], wf["wut"]], [0, 0], f, _swiglu_epi, (BF16,) * 3)
        x = _mm_nn(f"ffn_down_fwd{i}", [act], [wf["wd"]], _residual, F32, extras=[x])
        rec.update(h_ffn=h2, gq=gq, uq=uq, act=act, wf=wf)
        saved.append(rec)

    dx, g_final, loss_cols = _loss_head("loss_head", x, _row(sm["final_norm"]), target)

    g_mix, g_ffn = [None] * depth, [None] * depth
    nconv = (depth + 1) // 2
    g_b1, g_bdw, g_lng, g_lnb, g_b2 = ([None] * nconv for _ in range(5))
    dbias = []
    for i in reversed(range(depth)):
        j = i // 2
        rec = saved[i]
        wm, wf = rec["wm"], rec["wf"]
        dgate, dup, dwd = _ffn_down_bwd(f"ffn_down_bwd{i}", dx, wf["wd"], rec["gq"], rec["uq"])
        gf = {"wd": dwd,
              "wgt": _mm_tn(f"ffn_gate_dw{i}", dgate, rec["h_ffn"]),
              "wut": _mm_tn(f"ffn_up_dw{i}", dup, rec["h_ffn"])}
        dx, g_ffn[i] = _mm_nn_rms_bwd(f"ffn_up_bwd{i}", [dgate, dup], [wf["wgt"], wf["wut"]], rec["x_ffn"],
                                      _row(sm["norm_ffn"][i]), dx, tm=256)
        dx = emit(2 * i + 1, gf, dx)
        if i % 2 == 0:
            c = wm["w2"].shape[0]
            gm = {}
            gm["w2"], g_b2[j] = _mm_tn(f"conv_pw2_dw{j}", rec["s"], dx, colsum_b=True)
            ddw, sums = _ln_silu_bwd(f"conv_pw2_bwd{j}", dx, wm["w2"], rec["dw"], _row(sm["conv_ln_g"][j]),
                                     _row(sm["conv_ln_b"][j]))
            g_lng[j], g_lnb[j], g_bdw[j] = sums[0], sums[1], sums[2]
            du, dwk, db1 = _dwconv_bwd(f"dwconv_bwd{j}", ddw, rec["a"], rec["gt"], wm["wdw"])
            gm["wdw"] = dwk[:CONV_WIDTH]
            g_b1[j] = db1[0]
            gm["w1t"] = _mm_tn(f"conv_pw1_dw{j}", du, rec["h_mix"])
            dh_terms = ([du], [wm["w1t"]])
        else:
            gm = {"wot": _mm_tn(f"attn_out_dw{j}", dx, rec["om"])}
            dos, cs = _attn_bwd_prep(f"attn_out_bwd{j}", dx, wm["wot"], rec["os"], rec["ls"])
            back = [_attn_bwd(f"attn_bwd{j}_{g}", rec["qkv"][g], dos[g], rec["ls"][g], cs[g], bias, g)
                    for g in range(N_GROUPS)]
            dqkv = [b[0] for b in back]
            dbias.append(jnp.concatenate([b[1] for b in back], axis=0))
            dh_terms = (dqkv, [rec["wq"][g] for g in range(N_GROUPS)])
            gm["wqkvt"] = _ungroup_rows([_mm_tn(f"attn_qkv_dw{j}_{g}", dqkv[g], rec["h_mix"])
                                         for g in range(N_GROUPS)])
        dx, g_mix[i] = _mm_nn_rms_bwd(f"mix_in_bwd{i}", dh_terms[0], dh_terms[1], rec["x_mix"],
                                      _row(sm["norm_mix"][i]), dx)
        dx = emit(2 * i, gm, dx)

    gb = _bias_grad("bias_grad", dbias, buckets)
    g_rel = jnp.transpose(gb[:, :, ::32], (1, 0, 2)).reshape(N_BUCKETS, N_HEADS)
    gsm = {
        "norm_mix": jnp.concatenate(g_mix, axis=0), "norm_ffn": jnp.concatenate(g_ffn, axis=0),
        "final_norm": g_final[0], "conv_b_pw1": jnp.stack(g_b1), "conv_b_dw": jnp.stack(g_bdw),
        "conv_ln_g": jnp.stack(g_lng), "conv_ln_b": jnp.stack(g_lnb),
        "conv_b_pw2": jnp.concatenate(g_b2, axis=0), "rel_bias": g_rel,
    }
    return loss_cols, dx, gsm


SMALL = ("norm_mix", "norm_ffn", "final_norm", "conv_b_pw1", "conv_b_dw", "conv_ln_g", "conv_ln_b", "conv_b_pw2",
         "rel_bias")
SHARDED = (("conv_w_pw1", "w1t", True), ("conv_w_pw2", "w2", False), ("attn_w_qkv", "wqkvt", True),
           ("attn_w_o", "wot", True), ("ffn_w_gate", "wgt", True), ("ffn_w_up", "wut", True),
           ("ffn_w_down", "wd", False))
ORDER = ("norm_mix", "norm_ffn", "final_norm", "conv_w_pw1", "conv_b_pw1", "conv_w_dw", "conv_b_dw", "conv_ln_g",
         "conv_ln_b", "conv_w_pw2", "conv_b_pw2", "attn_w_qkv", "attn_w_o", "rel_bias", "ffn_w_gate", "ffn_w_up",
         "ffn_w_down")
PACK_LANES = 128
PACK_ROW_TILE = 8


def _pack_small(vals):
    flat = jnp.concatenate([vals[n].reshape(-1) for n in SMALL])
    per_tile = PACK_LANES * PACK_ROW_TILE
    return jnp.pad(flat, (0, -flat.shape[0] % per_tile)).reshape(-1, PACK_LANES)


def _unpack_small(pack, like):
    flat, out, pos = pack.reshape(-1), {}, 0
    for n in SMALL:
        out[n] = flat[pos:pos + like[n].size].reshape(like[n].shape)
        pos += like[n].size
    return out


def _dw_blocks(w):
    l, k, c = w.shape
    blk = jnp.transpose(w.reshape(l, k, N_DEV, c // N_DEV), (2, 0, 1, 3)).reshape(N_DEV, l * k, c // N_DEV)
    return jnp.pad(blk, ((0, 0), (0, -(l * k) % 8), (0, 0)))


def kernel(x, norm_mix, norm_ffn, final_norm, conv_w_pw1, conv_b_pw1, conv_w_dw, conv_b_dw, conv_ln_g, conv_ln_b, conv_w_pw2, conv_b_pw2, attn_w_qkv, attn_w_o, rel_bias, ffn_w_gate, ffn_w_up, ffn_w_down, loss_target, m_norm_mix, m_norm_ffn, m_final_norm, m_conv_w_pw1, m_conv_b_pw1, m_conv_w_dw, m_conv_b_dw, m_conv_ln_g, m_conv_ln_b, m_conv_w_pw2, m_conv_b_pw2, m_attn_w_qkv, m_attn_w_o, m_rel_bias, m_ffn_w_gate, m_ffn_w_up, m_ffn_w_down, v_norm_mix, v_norm_ffn, v_final_norm, v_conv_w_pw1, v_conv_b_pw1, v_conv_w_dw, v_conv_b_dw, v_conv_ln_g, v_conv_ln_b, v_conv_w_pw2, v_conv_b_pw2, v_attn_w_qkv, v_attn_w_o, v_rel_bias, v_ffn_w_gate, v_ffn_w_up, v_ffn_w_down):
    w = dict(norm_mix=norm_mix, norm_ffn=norm_ffn, final_norm=final_norm, conv_w_pw1=conv_w_pw1,
             conv_b_pw1=conv_b_pw1, conv_w_dw=conv_w_dw, conv_b_dw=conv_b_dw, conv_ln_g=conv_ln_g,
             conv_ln_b=conv_ln_b, conv_w_pw2=conv_w_pw2, conv_b_pw2=conv_b_pw2, attn_w_qkv=attn_w_qkv,
             attn_w_o=attn_w_o, rel_bias=rel_bias, ffn_w_gate=ffn_w_gate, ffn_w_up=ffn_w_up, ffn_w_down=ffn_w_down)
    m = dict(norm_mix=m_norm_mix, norm_ffn=m_norm_ffn, final_norm=m_final_norm, conv_w_pw1=m_conv_w_pw1,
             conv_b_pw1=m_conv_b_pw1, conv_w_dw=m_conv_w_dw, conv_b_dw=m_conv_b_dw, conv_ln_g=m_conv_ln_g,
             conv_ln_b=m_conv_ln_b, conv_w_pw2=m_conv_w_pw2, conv_b_pw2=m_conv_b_pw2, attn_w_qkv=m_attn_w_qkv,
             attn_w_o=m_attn_w_o, rel_bias=m_rel_bias, ffn_w_gate=m_ffn_w_gate, ffn_w_up=m_ffn_w_up,
             ffn_w_down=m_ffn_w_down)
    v = dict(norm_mix=v_norm_mix, norm_ffn=v_norm_ffn, final_norm=v_final_norm, conv_w_pw1=v_conv_w_pw1,
             conv_b_pw1=v_conv_b_pw1, conv_w_dw=v_conv_w_dw, conv_b_dw=v_conv_b_dw, conv_ln_g=v_conv_ln_g,
             conv_ln_b=v_conv_ln_b, conv_w_pw2=v_conv_w_pw2, conv_b_pw2=v_conv_b_pw2, attn_w_qkv=v_attn_w_qkv,
             attn_w_o=v_attn_w_o, rel_bias=v_rel_bias, ffn_w_gate=v_ffn_w_gate, ffn_w_up=v_ffn_w_up,
             ffn_w_down=v_ffn_w_down)

    me = 4 * lax.axis_index("x") + 2 * lax.axis_index("y") + lax.axis_index("c")
    depth = ffn_w_gate.shape[0]
    n_conv, _, cb = conv_w_dw.shape

    def sublayer(key, layer):
        if key in ("wgt", "wut", "wd"):
            return 2 * layer + 1
        return 4 * layer if key in ("w1t", "w2") else 4 * layer + 2

    def landing(block, own):
        land = lax.empty((N_DEV,) + block.shape, block.dtype)
        return lax.dynamic_update_slice(land, own[None], (me,) + (0,) * block.ndim)

    by_sub = {s: [] for s in range(2 * depth)}
    for name, key, cols in SHARDED:
        sw = (jnp.swapaxes(w[name], 1, 2) if cols else w[name]).astype(BF16)
        for layer in range(sw.shape[0]):
            by_sub[sublayer(key, layer)].append((key, layer, sw[layer]))
    likes = {s: jnp.zeros((sum(sh.size for _, _, sh in by_sub[s]) // 1024, 1024), BF16) for s in by_sub}
    dw_shard = jnp.pad(conv_w_dw.reshape(-1, cb), ((0, -(n_conv * CONV_WIDTH) % 8), (0, 0)))
    like_dw = jnp.zeros(dw_shard.shape, F32)
    groups = [([dw_shard], [landing(dw_shard, dw_shard)])]
    for s in range(2 * depth):
        groups.append(([sh for _, _, sh in by_sub[s]], [landing(sh, sh) for _, _, sh in by_sub[s]]))
    gather, _ = _exchange_start("gather_start", groups, scatter=False, carry=jnp.zeros((8, 128), F32))
    dw_filters = []

    def fetch(s, after):
        pieces = [(gather[s + 1], likes[s])]
        if s == 0:
            pieces.append((gather[0], like_dw))
        landed = _exchange_wait(f"gather_wait{s}", pieces, after)
        out = {key: g.reshape(g.shape[0] * g.shape[1], g.shape[2]) for (key, _, _), g in zip(by_sub[s], landed[0])}
        if s == 0:
            dw_all = landed[1][0]
            full = jnp.transpose(dw_all[:, :n_conv * CONV_WIDTH].reshape(N_DEV, n_conv, CONV_WIDTH, cb), (1, 2, 0, 3))
            full = jnp.pad(full.reshape(n_conv, CONV_WIDTH, N_DEV * cb), ((0, 0), (0, CONV_HALO - CONV_WIDTH), (0, 0)))
            dw_filters.extend(full[layer] for layer in range(n_conv))
        if "w1t" in out:
            out["wdw"] = dw_filters[s // 4]
        return out

    scatter, dw_grads, started = {}, {}, {}

    def emit(s, gd, carry):
        parts = [gd[key].reshape(N_DEV, -1, gd[key].shape[1]) for key, _, _ in by_sub[s]]
        if "wdw" in gd:
            dw_grads[s // 4] = gd["wdw"]
        if s == 0:
            parts.append(_dw_blocks(jnp.stack([dw_grads[layer] for layer in range(n_conv)])))
        lands = [landing(p[0], lax.dynamic_index_in_dim(p, me, 0, keepdims=False)) for p in parts]
        groups = [(parts[:len(by_sub[s])], lands[:len(by_sub[s])])]
        if s == 0:
            groups.append((parts[-1:], lands[-1:]))
        if s == 0:
            scatter[s], started[0] = _exchange_start(f"scatter_start{s}", groups, scatter=True,
                                                     carry=jnp.zeros((8, 128), F32))
            return carry
        scatter[s], carry = _exchange_start(f"scatter_start{s}", groups, scatter=True, carry=carry)
        return carry

    sm = {n: w[n] for n in SMALL}
    loss_cols, dx, gsm = _local_step(x[0], loss_target[0], sm, depth, fetch, emit)
    loss = lax.psum(jnp.sum(loss_cols), ("x", "y", "c"))

    grads, summed, delta, new_m, new_v = {}, {}, {}, {}, {}

    def reduce_pieces(subs, landed):
        for s, recv in zip(subs, landed):
            for (key, layer, _), r in zip(by_sub[s], recv):
                summed[key, layer] = _sum8(f"sum_{key}{layer}", r)

    def update(names):
        for name in names:
            shape = w[name].shape
            res = _adamw(f"adamw_{name}",
                         *[t.reshape(-1, shape[-1]) for t in (grads[name], w[name], m[name], v[name])])
            delta[name], new_m[name], new_v[name] = (t.reshape(shape) for t in res)

    def stacked(name, key, cols):
        g = jnp.stack([summed[key, layer] for layer in range(w[name].shape[0])])
        return jnp.swapaxes(g, 1, 2) if cols else g

    early = sorted((s for s in scatter if s != 0), reverse=True)
    reduce_pieces(early, _exchange_wait("scatter_wait_early", [(scatter[s][0], likes[s]) for s in early],
                                        started[0]))
    late_names = [name for name, key, _ in SHARDED if any(k == key for k, _, _ in by_sub[0])]
    for name, key, cols in SHARDED:
        if name not in late_names:
            grads[name] = stacked(name, key, cols)
    early_names = [name for name, _, _ in SHARDED if name not in late_names]
    update(early_names)
    pack = _pack_small(gsm)
    ((pack_all,),) = _exchange("gather_small_grads", [([pack], pack)], scatter=False)
    pack_sum = _sum8("sum_small", pack_all)
    grads.update(_unpack_small(pack_sum, sm))

    landed = _exchange_wait("scatter_wait_last", [(scatter[0][0], likes[0]), (scatter[0][1], like_dw)],
                            new_v[early_names[-1]])
    reduce_pieces([0], landed[:1])
    for name, key, cols in SHARDED:
        if name in late_names:
            grads[name] = stacked(name, key, cols)
    grads["conv_w_dw"] = _sum8("sum_wdw", landed[1][0])[:n_conv * CONV_WIDTH].reshape(conv_w_dw.shape)
    update(late_names + ["conv_w_dw"])
    res = _adamw("adamw_small", pack_sum, _pack_small(sm), _pack_small({n: m[n] for n in SMALL}),
                 _pack_small({n: v[n] for n in SMALL}))
    for dst, t in zip((delta, new_m, new_v), res):
        dst.update(_unpack_small(t, sm))

    outs = [loss, dx[None]]
    for d in (grads, delta, new_m, new_v):
        outs += [d[n] for n in ORDER]
    return tuple(outs)
```

```python
import functools
import math

import numpy as np
import jax
import jax.numpy as jnp
from jax import lax
from jax.experimental import pallas as pl
from jax.experimental.pallas import tpu as pltpu

F32 = jnp.float32
BF16 = jnp.bfloat16

N_DEV = 8
HEAD_DIM = 64
HEADS_PER_GROUP = 4
GROUP_COLS = HEADS_PER_GROUP * HEAD_DIM
DILATIONS = (1, 4, 16)
N_BACK = 128
N_GROUPS = 3
N_HEADS = 12
D_ATTN = 768
N_BUCKETS = 32
REL_MAX_DISTANCE = 2048
CONV_WIDTH = 31
CONV_HALO = 32
EPS = 1e-6
NEG_INF = -1e30
ADAM_LR, ADAM_B1, ADAM_B2, ADAM_EPS, ADAM_WD, ADAM_STEP = 0.001, 0.9, 0.999, 1e-08, 0.01, 10
V7X_VMEM_LIMIT_BYTES = 56 * 1024 * 1024
MESH = pl.DeviceIdType.MESH
ANY = pl.BlockSpec(memory_space=pl.ANY)


def _pick(n, prefs):
    for p in prefs:
        if n % p == 0:
            return p
    return n


def _params(sem, vmem=None):
    return pltpu.CompilerParams(dimension_semantics=sem, vmem_limit_bytes=vmem)


def _sigmoid(x):
    return 1.0 / (1.0 + jnp.exp(-x))


def _exchange(name, groups, scatter):
    n_arr = [len(arrs) for arrs, _ in groups]
    n_in = sum(n_arr) + len(groups)
    ng = len(groups)

    def body(*refs):
        ins, outs, (send_sems, recv_sems, local_sems) = refs[:n_in], refs[n_in:-3], refs[-3:]
        x, y, c = lax.axis_index("x"), lax.axis_index("y"), lax.axis_index("c")
        me = 4 * x + 2 * y + c
        pos_in = pos_out = 0
        plans = []
        for gi in range(ng):
            srcs = ins[pos_in:pos_in + n_arr[gi]]
            like = ins[pos_in + n_arr[gi]]
            dsts = outs[pos_out:pos_out + n_arr[gi]]
            pos_in += n_arr[gi] + 1
            pos_out += n_arr[gi]
            plans.append((gi, srcs, like, dsts))
        local = []
        for gi, srcs, like, dsts in plans:
            for s, d in zip(srcs, dsts):
                cp = pltpu.make_async_copy(s.at[me] if scatter else s, d.at[me], local_sems.at[gi])
                cp.start()
                local.append(cp)
        for delta in range(1, N_DEV):
            dx, dy, dc = (delta >> 2) & 1, (delta >> 1) & 1, delta & 1
            px, py, pc = (1 - x if dx else x), (1 - y if dy else y), (1 - c if dc else c)
            peer = 4 * px + 2 * py + pc
            for gi, srcs, like, dsts in plans:
                for s, d in zip(srcs, dsts):
                    pltpu.make_async_remote_copy(
                        src_ref=s.at[peer] if scatter else s, dst_ref=d.at[me],
                        send_sem=send_sems.at[gi, delta - 1], recv_sem=recv_sems.at[gi, delta - 1],
                        device_id=(px, py, pc), device_id_type=MESH).start()
        for delta in range(1, N_DEV):
            for gi, srcs, like, dsts in plans:
                pltpu.make_async_remote_copy(
                    src_ref=like, dst_ref=like, send_sem=send_sems.at[gi, delta - 1],
                    recv_sem=recv_sems.at[gi, delta - 1], device_id=(x, y, c), device_id_type=MESH).wait()
        for cp in local:
            cp.wait()

    operands, out_shape = [], []
    for arrs, like in groups:
        operands += list(arrs) + [like]
        for a in arrs:
            blk = a.shape[1:] if scatter else a.shape
            out_shape.append(jax.ShapeDtypeStruct((N_DEV,) + tuple(blk), a.dtype))
    outs = pl.pallas_call(
        body, name=name, out_shape=out_shape, in_specs=[ANY] * len(operands), out_specs=[ANY] * len(out_shape),
        scratch_shapes=[pltpu.SemaphoreType.DMA((ng, N_DEV - 1)), pltpu.SemaphoreType.DMA((ng, N_DEV - 1)),
                        pltpu.SemaphoreType.DMA((ng,))],
        compiler_params=pltpu.CompilerParams(has_side_effects=True),
    )(*operands)
    res, pos = [], 0
    for n in n_arr:
        res.append(list(outs[pos:pos + n]))
        pos += n
    return res


HBM = pl.BlockSpec(memory_space=pltpu.HBM)
SEM = pl.BlockSpec(memory_space=pltpu.SEMAPHORE)
EFFECT = pltpu.SideEffectType.DATAFLOW_SIDE_EFFECTING


def _in_hbm(a):
    return pltpu.with_memory_space_constraint(a, pltpu.HBM)


def _exchange_start(name, groups, scatter, carry):
    ns = [len(s) for s, _ in groups]
    n_in = 2 * sum(ns)

    def body(*refs):
        ins, outs = refs[:n_in], refs[n_in + 1:]
        x, y, c = lax.axis_index("x"), lax.axis_index("y"), lax.axis_index("c")
        me = 4 * x + 2 * y + c
        pi = po = 0
        for n in ns:
            srcs, lands = ins[pi:pi + n], ins[pi + n:pi + 2 * n]
            send_sems, recv_sems = outs[po], outs[po + 1]
            pi += 2 * n
            po += 2 + 2 * n
            for delta in range(1, N_DEV):
                dx, dy, dc = (delta >> 2) & 1, (delta >> 1) & 1, delta & 1
                px, py, pc = (1 - x if dx else x), (1 - y if dy else y), (1 - c if dc else c)
                peer = 4 * px + 2 * py + pc
                for s, d in zip(srcs, lands):
                    pltpu.make_async_remote_copy(
                        src_ref=s.at[peer] if scatter else s, dst_ref=d.at[me], send_sem=send_sems.at[delta - 1],
                        recv_sem=recv_sems.at[delta - 1], device_id=(px, py, pc), device_id_type=MESH).start()

    operands, out_shape, out_specs, aliases = [], [], [], {}
    for srcs, lands in groups:
        out_shape += [pltpu.SemaphoreType.DMA((N_DEV - 1,))] * 2
        out_specs += [SEM, SEM]
        for a in list(srcs) + list(lands):
            aliases[len(operands)] = len(out_shape)
            operands.append(_in_hbm(a))
            out_shape.append(pltpu.HBM(a.shape, a.dtype))
            out_specs.append(HBM)
    aliases[len(operands)] = len(out_shape)
    operands.append(_in_hbm(carry))
    out_shape.append(pltpu.HBM(carry.shape, carry.dtype))
    out_specs.append(HBM)
    outs = pl.pallas_call(
        body, name=name, out_shape=out_shape, in_specs=[HBM] * len(operands), out_specs=out_specs,
        input_output_aliases=aliases, compiler_params=pltpu.CompilerParams(has_side_effects=EFFECT),
    )(*operands)
    handles, po = [], 0
    for n in ns:
        handles.append((outs[po], outs[po + 1], list(outs[po + 2:po + 2 + n]), list(outs[po + 2 + n:po + 2 + 2 * n])))
        po += 2 + 2 * n
    return handles, outs[-1]


def _exchange_wait(name, pieces, after):
    ns = [len(h[2]) for h, _ in pieces]

    def body(*refs):
        x, y, c = lax.axis_index("x"), lax.axis_index("y"), lax.axis_index("c")
        pi = 0
        for n in ns:
            send_sems, recv_sems, like = refs[pi + 2 * n], refs[pi + 2 * n + 1], refs[pi + 2 * n + 2]
            pi += 2 * n + 3
            for delta in range(1, N_DEV):
                cp = pltpu.make_async_remote_copy(
                    src_ref=like, dst_ref=like, send_sem=send_sems.at[delta - 1], recv_sem=recv_sems.at[delta - 1],
                    device_id=(x, y, c), device_id_type=MESH)
                cp.wait_send()
                cp.wait_recv()

    operands, in_specs, out_shape, aliases = [], [], [], {}
    for (send_sems, recv_sems, srcs, lands), like in pieces:
        for a in srcs + lands:
            aliases[len(operands)] = len(out_shape)
            operands.append(a)
            in_specs.append(HBM)
            out_shape.append(pltpu.HBM(a.shape, a.dtype))
        operands += [send_sems, recv_sems, like]
        in_specs += [SEM, SEM, ANY]
    operands.append(after)
    in_specs.append(ANY)
    outs = pl.pallas_call(
        body, name=name, out_shape=out_shape, in_specs=in_specs, out_specs=[HBM] * len(out_shape),
        input_output_aliases=aliases, compiler_params=pltpu.CompilerParams(has_side_effects=EFFECT),
    )(*operands)
    res, po = [], 0
    for n in ns:
        res.append(list(outs[po + n:po + 2 * n]))
        po += 2 * n
    return res


def _sum8(name, r):
    _, rows, cols = r.shape
    tr = _pick(rows, (256, 128, 64, 32, 16, 8))

    def body(r_ref, o_ref):
        acc = r_ref[0].astype(F32)
        for p in range(1, N_DEV):
            acc = acc + r_ref[p].astype(F32)
        o_ref[...] = acc

    return pl.pallas_call(
        body, name=name, out_shape=jax.ShapeDtypeStruct((rows, cols), F32), grid=(rows // tr,),
        in_specs=[pl.BlockSpec((N_DEV, tr, cols), lambda i: (0, i, 0))],
        out_specs=pl.BlockSpec((tr, cols), lambda i: (i, 0)), compiler_params=_params(("parallel",)),
    )(r)


def _adamw(name, g, w, m, v):
    rows, cols = g.shape
    tr = _pick(rows, (512, 256, 128, 64, 32, 16, 8))
    c1 = 1.0 - ADAM_B1 ** ADAM_STEP
    c2 = 1.0 - ADAM_B2 ** ADAM_STEP

    def body(g_ref, w_ref, m_ref, v_ref, d_ref, nm_ref, nv_ref):
        gv = g_ref[...]
        nm = ADAM_B1 * m_ref[...] + (1.0 - ADAM_B1) * gv
        nv = ADAM_B2 * v_ref[...] + (1.0 - ADAM_B2) * (gv * gv)
        d_ref[...] = -ADAM_LR * ((nm / c1) / (jnp.sqrt(nv / c2) + ADAM_EPS) + ADAM_WD * w_ref[...])
        nm_ref[...] = nm
        nv_ref[...] = nv

    spec = pl.BlockSpec((tr, cols), lambda i: (i, 0))
    return pl.pallas_call(
        body, name=name, out_shape=[jax.ShapeDtypeStruct((rows, cols), F32)] * 3, grid=(rows // tr,),
        in_specs=[spec] * 4, out_specs=[spec] * 3, compiler_params=_params(("parallel",)),
    )(g, w, m, v)


def _mm_nt(name, a, ws, w_offs, n, epi, out_dtypes, extras=(), rows=(), tm=None, tn=None):
    m, k = a.shape
    tm = tm or _pick(m, (512, 256, 128))
    tn = tn or _pick(n, (1408, 1152, 1024, 512, 256, 128))
    nw, ne, nr = len(ws), len(extras), len(rows)

    def body(*refs):
        a_ref, w_refs = refs[0], refs[1:1 + nw]
        e_refs, r_refs = refs[1 + nw:1 + nw + ne], refs[1 + nw + ne:1 + nw + ne + nr]
        o_refs = refs[1 + nw + ne + nr:]
        av = a_ref[...].astype(BF16)
        accs = [lax.dot_general(av, w[...], (((1,), (1,)), ((), ())), preferred_element_type=F32) for w in w_refs]
        outs = epi(accs, [e[...] for e in e_refs], [r[...] for r in r_refs])
        for o_ref, o in zip(o_refs, outs):
            o_ref[...] = o.astype(o_ref.dtype)

    in_specs = [pl.BlockSpec((tm, k), lambda j, i: (i, 0))]
    in_specs += [pl.BlockSpec((tn, k), functools.partial(lambda j, i, off: (j + off, 0), off=off)) for off in w_offs]
    in_specs += [pl.BlockSpec((tm, tn), lambda j, i: (i, j))] * ne
    in_specs += [pl.BlockSpec((1, tn), lambda j, i: (0, j))] * nr
    return pl.pallas_call(
        body, name=name, out_shape=[jax.ShapeDtypeStruct((m, n), dt) for dt in out_dtypes],
        grid=(n // tn, m // tm), in_specs=in_specs,
        out_specs=[pl.BlockSpec((tm, tn), lambda j, i: (i, j))] * len(out_dtypes),
        compiler_params=_params(("parallel", "parallel"), V7X_VMEM_LIMIT_BYTES),
    )(a, *ws, *extras, *rows)


def _rms_rows(x, gain):
    r = lax.rsqrt(jnp.mean(x * x, axis=-1, keepdims=True) + EPS)
    return (x * r * gain).astype(BF16)


def _project_residual(name, a, b, x, bias=None, gain=None):
    m, k = a.shape
    n = b.shape[1]
    tm = _pick(m, (512, 256, 128))
    rows = [r for r in (bias, gain) if r is not None]

    def body(*refs):
        a_ref, b_ref, x_ref = refs[:3]
        r_refs = list(refs[3:3 + len(rows)])
        out = x_ref[...] + jnp.dot(a_ref[...], b_ref[...], preferred_element_type=F32)
        if bias is not None:
            out = out + r_refs.pop(0)[...]
        refs[3 + len(rows)][...] = out
        if gain is not None:
            refs[4 + len(rows)][...] = _rms_rows(out, r_refs.pop(0)[...])

    big = pl.BlockSpec((tm, n), lambda i: (i, 0))
    out_shape = [jax.ShapeDtypeStruct((m, n), F32)] + ([jax.ShapeDtypeStruct((m, n), BF16)] if gain is not None else [])
    res = pl.pallas_call(
        body, name=name, out_shape=out_shape, grid=(m // tm,),
        in_specs=[pl.BlockSpec((tm, k), lambda i: (i, 0)), pl.BlockSpec((k, n), lambda i: (0, 0)), big]
        + [pl.BlockSpec((1, n), lambda i: (0, 0))] * len(rows),
        out_specs=[big] * len(out_shape), compiler_params=_params(("parallel",), V7X_VMEM_LIMIT_BYTES),
    )(a, b, x, *rows)
    return res if gain is not None else (res[0], None)


def _mm_nn_rms_bwd(name, as_, bs, x, g, dx_out, tm=None):
    m, k = as_[0].shape
    n = bs[0].shape[1]
    tm = tm or _pick(m, (512, 256, 128))
    npair = len(as_)

    def body(*refs):
        a_refs, b_refs = refs[:npair], refs[npair:2 * npair]
        x_ref, g_ref, dxo_ref, dx_ref, dg_ref = refs[2 * npair:]
        dh = None
        for a_ref, b_ref in zip(a_refs, b_refs):
            p = jnp.dot(a_ref[...].astype(BF16), b_ref[...], preferred_element_type=F32)
            dh = p if dh is None else dh + p
        xv = x_ref[...]
        r = lax.rsqrt(jnp.mean(xv * xv, axis=-1, keepdims=True) + EPS)
        yv = xv * r
        dy = dh * g_ref[...]
        dx_ref[...] = dxo_ref[...] + r * (dy - yv * jnp.mean(dy * yv, axis=-1, keepdims=True))

        @pl.when(pl.program_id(0) == 0)
        def _():
            dg_ref[...] = jnp.zeros_like(dg_ref)

        dg_ref[...] += jnp.sum(dh * yv, axis=0, keepdims=True)

    big = pl.BlockSpec((tm, n), lambda i: (i, 0))
    row = pl.BlockSpec((1, n), lambda i: (0, 0))
    in_specs = [pl.BlockSpec((tm, k), lambda i: (i, 0))] * npair + [pl.BlockSpec((k, n), lambda i: (0, 0))] * npair
    return pl.pallas_call(
        body, name=name, out_shape=[jax.ShapeDtypeStruct((m, n), F32), jax.ShapeDtypeStruct((1, n), F32)],
        grid=(m // tm,), in_specs=in_specs + [big, row, big], out_specs=[big, row],
        compiler_params=_params(("arbitrary",), V7X_VMEM_LIMIT_BYTES),
    )(*as_, *bs, x, g, dx_out)


def _mm_tn(name, a, b, colsum_b=False, tm=None, tk=2048):
    t, ma = a.shape
    nb = b.shape[1]
    tm = tm or _pick(ma, (1408, 1152, 1024, 768, 512, 256, 128))
    tk = _pick(t, (tk, 256, 128))
    nk = t // tk

    def body(*refs):
        a_ref, b_ref, o_ref = refs[0], refs[1], refs[2]
        acc_ref = refs[-1]
        kk = pl.program_id(1)
        bv = b_ref[...]

        @pl.when(kk == 0)
        def _():
            acc_ref[...] = jnp.zeros_like(acc_ref)

        acc_ref[...] += lax.dot_general(a_ref[...].astype(BF16), bv.astype(BF16), (((0,), (0,)), ((), ())),
                                        preferred_element_type=F32)
        if colsum_b:
            s_ref = refs[3]

            @pl.when((kk == 0) & (pl.program_id(0) == 0))
            def _():
                s_ref[...] = jnp.zeros_like(s_ref)

            @pl.when(pl.program_id(0) == 0)
            def _():
                s_ref[...] += jnp.sum(bv.astype(F32), axis=0, keepdims=True)

        @pl.when(kk == nk - 1)
        def _():
            o_ref[...] = acc_ref[...].astype(o_ref.dtype)

    out_shape = [jax.ShapeDtypeStruct((ma, nb), BF16)]
    out_specs = [pl.BlockSpec((tm, nb), lambda i, kk: (i, 0))]
    if colsum_b:
        out_shape.append(jax.ShapeDtypeStruct((1, nb), F32))
        out_specs.append(pl.BlockSpec((1, nb), lambda i, kk: (0, 0)))
    res = pl.pallas_call(
        body, name=name, out_shape=out_shape, grid=(ma // tm, nk),
        in_specs=[pl.BlockSpec((tk, tm), lambda i, kk: (kk, i)), pl.BlockSpec((tk, nb), lambda i, kk: (kk, 0))],
        out_specs=out_specs, scratch_shapes=[pltpu.VMEM((tm, nb), F32)],
        compiler_params=_params(("arbitrary", "arbitrary"), V7X_VMEM_LIMIT_BYTES),
    )(a, b)
    return res if colsum_b else res[0]


FFN_CHUNK = 256


def _ffn_down_bwd(name, dx, wd, gq, uq):
    t, d_model = dx.shape
    f = wd.shape[0]
    tm = _pick(t, (256, 128))
    ck = _pick(f, (FFN_CHUNK, 128))
    nt = (((1,), (1,)), ((), ()))
    tn = (((0,), (0,)), ((), ()))
    last = t // tm - 1

    def body(dx_ref, w_ref, g_ref, u_ref, dg_ref, du_ref, dw_ref, acc_ref):
        i = pl.program_id(0)

        @pl.when(i == 0)
        def _():
            acc_ref[...] = jnp.zeros_like(acc_ref)

        dxb = dx_ref[...].astype(BF16)
        for c0 in range(0, f, ck):
            cs = slice(c0, c0 + ck)
            dact = lax.dot_general(dxb, w_ref[cs, :], nt, preferred_element_type=F32)
            gf, uf = g_ref[:, cs].astype(F32), u_ref[:, cs].astype(F32)
            sg = _sigmoid(gf)
            dg_ref[:, cs] = (dact * uf * (sg * (1.0 + gf * (1.0 - sg)))).astype(BF16)
            du_ref[:, cs] = (dact * gf * sg).astype(BF16)
            act = (gf * sg * uf).astype(BF16)
            acc_ref[cs, :] += lax.dot_general(act, dxb, tn, preferred_element_type=F32)

        @pl.when(i == last)
        def _():
            dw_ref[...] = acc_ref[...].astype(BF16)

    wide = pl.BlockSpec((tm, f), lambda i: (i, 0))
    whole = pl.BlockSpec((f, d_model), lambda i: (0, 0))
    return pl.pallas_call(
        body, name=name,
        out_shape=[jax.ShapeDtypeStruct((t, f), BF16), jax.ShapeDtypeStruct((t, f), BF16),
                   jax.ShapeDtypeStruct((f, d_model), BF16)],
        grid=(t // tm,), in_specs=[pl.BlockSpec((tm, d_model), lambda i: (i, 0)), whole, wide, wide],
        out_specs=[wide, wide, whole], scratch_shapes=[pltpu.VMEM((f, d_model), F32)],
        compiler_params=_params(("arbitrary",), V7X_VMEM_LIMIT_BYTES),
    )(dx, wd, gq, uq)


def _rmsnorm_fwd(name, x, g):
    t, d = x.shape
    tr = _pick(t, (512, 256, 128))

    def body(x_ref, g_ref, h_ref):
        xv = x_ref[...]
        r = lax.rsqrt(jnp.mean(xv * xv, axis=-1, keepdims=True) + EPS)
        h_ref[...] = (xv * r * g_ref[...]).astype(BF16)

    return pl.pallas_call(
        body, name=name, out_shape=jax.ShapeDtypeStruct((t, d), BF16), grid=(t // tr,),
        in_specs=[pl.BlockSpec((tr, d), lambda i: (i, 0)), pl.BlockSpec((1, d), lambda i: (0, 0))],
        out_specs=pl.BlockSpec((tr, d), lambda i: (i, 0)), compiler_params=_params(("parallel",)),
    )(x, g)


def _loss_head(name, x, g, target):
    t, d = x.shape
    tr = _pick(t, (512, 256, 128))

    def body(x_ref, g_ref, t_ref, dx_ref, dg_ref, l_ref):
        xv = x_ref[...]
        r = lax.rsqrt(jnp.mean(xv * xv, axis=-1, keepdims=True) + EPS)
        yv = xv * r
        diff = yv * g_ref[...] - t_ref[...]
        dout = diff * (1.0 / d)
        dy = dout * g_ref[...]
        dx_ref[...] = r * (dy - yv * jnp.mean(dy * yv, axis=-1, keepdims=True))

        @pl.when(pl.program_id(0) == 0)
        def _():
            dg_ref[...] = jnp.zeros_like(dg_ref)
            l_ref[...] = jnp.zeros_like(l_ref)

        dg_ref[...] += jnp.sum(dout * yv, axis=0, keepdims=True)
        l_ref[...] += (0.5 / d) * jnp.sum(diff * diff, axis=0, keepdims=True)

    big = pl.BlockSpec((tr, d), lambda i: (i, 0))
    row = pl.BlockSpec((1, d), lambda i: (0, 0))
    return pl.pallas_call(
        body, name=name,
        out_shape=[jax.ShapeDtypeStruct((t, d), F32), jax.ShapeDtypeStruct((1, d), F32),
                   jax.ShapeDtypeStruct((1, d), F32)],
        grid=(t // tr,), in_specs=[big, row, big], out_specs=[big, row, row],
        compiler_params=_params(("arbitrary",)),
    )(x, g, target)


CONV_ROWS = 64
SUBLANES = 8


def _fill_window(win_ref, sh_ref, parts):
    rows = sh_ref.shape[2]
    for cb in range(win_ref.shape[0]):
        for r0, val in parts:
            win_ref[cb, r0:r0 + val.shape[0], :] = val[:, 128 * cb:128 * (cb + 1)]
        win_ref[cb, rows:rows + SUBLANES, :] = jnp.zeros((SUBLANES, 128), F32)
        for b in range(1, SUBLANES):
            sh_ref[b - 1, cb] = win_ref[cb, b:b + rows, :]


def _window_rows(win_ref, sh_ref, o, cb):
    b = o % SUBLANES
    if b == 0:
        return win_ref[cb, o:o + CONV_ROWS, :]
    return sh_ref[b - 1, cb, o - b:o - b + CONV_ROWS, :]


def _window_scratch(rows, c):
    return [pltpu.VMEM((c // 128, rows + SUBLANES, 128), F32), pltpu.VMEM((SUBLANES - 1, c // 128, rows, 128), F32)]


def _dwconv_fwd(name, glu, w_dw, b_dw, ln_g, ln_b):
    t, c = glu.shape
    tt = _pick(t, (256, 128))
    hb = tt // CONV_HALO

    def body(cur_ref, halo_ref, w_ref, b_ref, g_ref, be_ref, dw_ref, s_ref, win_ref, sh_ref):
        i = pl.program_id(0)
        halo = jnp.where(i > 0, halo_ref[...].astype(F32), 0.0)
        _fill_window(win_ref, sh_ref, [(0, halo), (CONV_HALO, cur_ref[...].astype(F32))])
        for r0 in range(0, tt, CONV_ROWS):
            for cb in range(c // 128):
                c0 = 128 * cb
                acc = jnp.zeros((CONV_ROWS, 128), F32) + b_ref[:, c0:c0 + 128]
                for k in range(CONV_WIDTH):
                    o = r0 + k + CONV_HALO - (CONV_WIDTH - 1)
                    acc = acc + w_ref[k:k + 1, c0:c0 + 128] * _window_rows(win_ref, sh_ref, o, cb)
                dw_ref[r0:r0 + CONV_ROWS, c0:c0 + 128] = acc
        u = dw_ref[...]
        mu = jnp.mean(u, axis=-1, keepdims=True)
        uc = u - mu
        rstd = lax.rsqrt(jnp.mean(uc * uc, axis=-1, keepdims=True) + EPS)
        z = uc * rstd * g_ref[...] + be_ref[...]
        s_ref[...] = (z * _sigmoid(z)).astype(BF16)

    big = pl.BlockSpec((tt, c), lambda i: (i, 0))
    row = pl.BlockSpec((1, c), lambda i: (0, 0))
    return pl.pallas_call(
        body, name=name, out_shape=[jax.ShapeDtypeStruct((t, c), F32), jax.ShapeDtypeStruct((t, c), BF16)],
        grid=(t // tt,),
        in_specs=[big, pl.BlockSpec((CONV_HALO, c), lambda i: (jnp.maximum(i * hb - 1, 0), 0)),
                  pl.BlockSpec((CONV_HALO, c), lambda i: (0, 0)), row, row, row],
        out_specs=[big, big], scratch_shapes=_window_scratch(tt + CONV_HALO, c),
        compiler_params=_params(("parallel",), V7X_VMEM_LIMIT_BYTES),
    )(glu, glu, w_dw, b_dw, ln_g, ln_b)


def _ln_silu_bwd(name, dx, w2, dw, ln_g, ln_b):
    t, c = dw.shape
    d_model = dx.shape[1]
    tr = _pick(t, (256, 128))

    def body(dx_ref, w_ref, dw_ref, g_ref, be_ref, o_ref, acc_ref):
        ds = lax.dot_general(dx_ref[...].astype(BF16), w_ref[...], (((1,), (1,)), ((), ())),
                             preferred_element_type=F32)
        u = dw_ref[...]
        mu = jnp.mean(u, axis=-1, keepdims=True)
        uc = u - mu
        rstd = lax.rsqrt(jnp.mean(uc * uc, axis=-1, keepdims=True) + EPS)
        xh = uc * rstd
        z = xh * g_ref[...] + be_ref[...]
        sg = _sigmoid(z)
        dz = ds * (sg * (1.0 + z * (1.0 - sg)))
        dxh = dz * g_ref[...]
        du = rstd * (dxh - jnp.mean(dxh, axis=-1, keepdims=True) - xh * jnp.mean(dxh * xh, axis=-1, keepdims=True))
        o_ref[...] = du

        @pl.when(pl.program_id(0) == 0)
        def _():
            acc_ref[...] = jnp.zeros_like(acc_ref)

        acc_ref[0:1, :] += jnp.sum(dz * xh, axis=0, keepdims=True)
        acc_ref[1:2, :] += jnp.sum(dz, axis=0, keepdims=True)
        acc_ref[2:3, :] += jnp.sum(du, axis=0, keepdims=True)

    big = pl.BlockSpec((tr, c), lambda i: (i, 0))
    row = pl.BlockSpec((1, c), lambda i: (0, 0))
    return pl.pallas_call(
        body, name=name, out_shape=[jax.ShapeDtypeStruct((t, c), F32), jax.ShapeDtypeStruct((8, c), F32)],
        grid=(t // tr,),
        in_specs=[pl.BlockSpec((tr, d_model), lambda i: (i, 0)), pl.BlockSpec((c, d_model), lambda i: (0, 0)),
                  big, row, row],
        out_specs=[big, pl.BlockSpec((8, c), lambda i: (0, 0))],
        compiler_params=_params(("arbitrary",), V7X_VMEM_LIMIT_BYTES),
    )(dx, w2, dw, ln_g, ln_b)


def _dwconv_bwd(name, ddw, a, gt, w_dw):
    t, c = ddw.shape
    tt = _pick(t, (256, 128))
    hb = tt // CONV_HALO
    last = t // tt - 1
    back = CONV_WIDTH - 1

    def body(d_ref, dn_ref, a_ref, ap_ref, g_ref, gp_ref, w_ref, du_ref, dwk_ref, db_ref,
             wd_ref, shd_ref, wg_ref, shg_ref, dg_ref, dwk8_ref):
        i = pl.program_id(0)
        _fill_window(wd_ref, shd_ref, [(0, d_ref[...]), (tt, jnp.where(i < last, dn_ref[...], 0.0))])
        glu_prev = ap_ref[...].astype(F32) * _sigmoid(gp_ref[...].astype(F32))
        av = a_ref[...].astype(F32)
        sg = _sigmoid(g_ref[...].astype(F32))
        _fill_window(wg_ref, shg_ref, [(0, jnp.where(i > 0, glu_prev, 0.0)), (CONV_HALO, av * sg)])

        @pl.when(i == 0)
        def _():
            dwk8_ref[...] = jnp.zeros_like(dwk8_ref)
            db_ref[...] = jnp.zeros_like(db_ref)

        for r0 in range(0, tt, CONV_ROWS):
            for cb in range(c // 128):
                c0 = 128 * cb
                dcur = wd_ref[cb, r0:r0 + CONV_ROWS, :]
                acc = jnp.zeros((CONV_ROWS, 128), F32)
                for k in range(CONV_WIDTH):
                    acc = acc + w_ref[k:k + 1, c0:c0 + 128] * _window_rows(wd_ref, shd_ref, r0 + back - k, cb)
                    p = dcur * _window_rows(wg_ref, shg_ref, r0 + k + CONV_HALO - back, cb)
                    s8 = p[0:SUBLANES]
                    for q in range(SUBLANES, CONV_ROWS, SUBLANES):
                        s8 = s8 + p[q:q + SUBLANES]
                    dwk8_ref[SUBLANES * k:SUBLANES * (k + 1), c0:c0 + 128] += s8
                dg_ref[r0:r0 + CONV_ROWS, c0:c0 + 128] = acc

        @pl.when(i == last)
        def _():
            for k in range(CONV_WIDTH):
                dwk_ref[k:k + 1, :] = jnp.sum(dwk8_ref[SUBLANES * k:SUBLANES * (k + 1), :], axis=0, keepdims=True)
            dwk_ref[CONV_WIDTH:, :] = jnp.zeros((CONV_HALO - CONV_WIDTH, c), F32)
        dglu = dg_ref[...]
        da = dglu * sg
        dgate = dglu * av * sg * (1.0 - sg)
        du_ref[:, 0:c] = da.astype(BF16)
        du_ref[:, c:] = dgate.astype(BF16)
        db_ref[:, 0:c] += jnp.sum(da, axis=0, keepdims=True)
        db_ref[:, c:] += jnp.sum(dgate, axis=0, keepdims=True)

    big = pl.BlockSpec((tt, c), lambda i: (i, 0))
    prev = pl.BlockSpec((CONV_HALO, c), lambda i: (jnp.maximum(i * hb - 1, 0), 0))
    nxt = pl.BlockSpec((CONV_HALO, c), lambda i: (jnp.minimum((i + 1) * hb, t // CONV_HALO - 1), 0))
    return pl.pallas_call(
        body, name=name,
        out_shape=[jax.ShapeDtypeStruct((t, 2 * c), BF16), jax.ShapeDtypeStruct((CONV_HALO, c), F32),
                   jax.ShapeDtypeStruct((1, 2 * c), F32)],
        grid=(t // tt,),
        in_specs=[big, nxt, big, prev, big, prev, pl.BlockSpec((CONV_HALO, c), lambda i: (0, 0))],
        out_specs=[pl.BlockSpec((tt, 2 * c), lambda i: (i, 0)), pl.BlockSpec((CONV_HALO, c), lambda i: (0, 0)),
                   pl.BlockSpec((1, 2 * c), lambda i: (0, 0))],
        scratch_shapes=_window_scratch(tt + CONV_HALO, c) + _window_scratch(tt + CONV_HALO, c)
        + [pltpu.VMEM((tt, c), F32), pltpu.VMEM((SUBLANES * CONV_HALO, c), F32)],
        compiler_params=_params(("arbitrary",), V7X_VMEM_LIMIT_BYTES),
    )(ddw, ddw, a, a, gt, gt, w_dw)


def _bucket_tables():
    i = np.arange(N_BACK)[:, None]
    j = np.arange(2 * N_BACK)[None, :]
    dist = i + N_BACK - j
    valid = (dist >= 0) & (dist <= N_BACK)
    max_exact = N_BUCKETS // 2
    out = []
    for d in DILATIONS:
        n = np.maximum(dist * d, 0)
        nf = np.maximum(n, 1).astype(np.float32)
        large = max_exact + (np.log(nf / np.float32(max_exact)) / np.float32(math.log(REL_MAX_DISTANCE / max_exact))
                             * np.float32(N_BUCKETS - max_exact)).astype(np.int32)
        large = np.minimum(large, N_BUCKETS - 1)
        out.append(np.where(valid, np.where(n < max_exact, n, large), -1))
    return np.stack(out).astype(np.int32)


def _bias_build(name, rel_bias, buckets):
    def body(tbl_ref, bk_ref, o_ref):
        g = pl.program_id(0)
        bk = bk_ref[0]
        for h in range(HEADS_PER_GROUP):
            acc = jnp.zeros(bk.shape, F32)
            for b in range(N_BUCKETS):
                acc = jnp.where(bk == b, tbl_ref[b, g * HEADS_PER_GROUP + h], acc)
            o_ref[h] = jnp.where(bk < 0, NEG_INF, acc)

    return pl.pallas_call(
        body, name=name, out_shape=jax.ShapeDtypeStruct((N_HEADS, N_BACK, 2 * N_BACK), F32), grid=(N_GROUPS,),
        in_specs=[pl.BlockSpec(memory_space=pltpu.SMEM), pl.BlockSpec((1, N_BACK, 2 * N_BACK), lambda g: (g, 0, 0))],
        out_specs=pl.BlockSpec((HEADS_PER_GROUP, N_BACK, 2 * N_BACK), lambda g: (g, 0, 0)),
        compiler_params=_params(("arbitrary",)),
    )(rel_bias, buckets)


def _bias_grad(name, dbs, buckets):
    nd = len(dbs)

    def body(*refs):
        bk = refs[nd][0]
        o_ref = refs[nd + 1]
        lane = lax.broadcasted_iota(jnp.int32, (1, 128), 1)
        db = [sum(r[h] for r in refs[:nd]) for h in range(HEADS_PER_GROUP)]
        for b in range(N_BUCKETS):
            row = jnp.zeros((1, 128), F32)
            for h in range(HEADS_PER_GROUP):
                s = jnp.sum(jnp.where(bk == b, db[h], 0.0), axis=0, keepdims=True)
                s = jnp.sum(s, axis=1, keepdims=True)
                row = jnp.where(lane // 32 == h, s, row)
            o_ref[0, b:b + 1, :] = row

    spec = pl.BlockSpec((HEADS_PER_GROUP, N_BACK, 2 * N_BACK), lambda g: (g, 0, 0))
    return pl.pallas_call(
        body, name=name, out_shape=jax.ShapeDtypeStruct((N_GROUPS, N_BUCKETS, 128), F32), grid=(N_GROUPS,),
        in_specs=[spec] * nd + [pl.BlockSpec((1, N_BACK, 2 * N_BACK), lambda g: (g, 0, 0))],
        out_specs=pl.BlockSpec((1, N_BUCKETS, 128), lambda g: (g, 0, 0)), compiler_params=_params(("arbitrary",)),
    )(*dbs, buckets)


def _head_cols(h):
    return slice(h * HEAD_DIM, (h + 1) * HEAD_DIM)


def _attn_fwd(name, qkv, bias, g):
    t = qkv.shape[0]
    d = DILATIONS[g]
    tq = t // d
    nblk = qkv.shape[1] // GROUP_COLS
    scale = HEAD_DIM ** -0.5

    def body(q_ref, kp_ref, kc_ref, vp_ref, vc_ref, b_ref, o_ref, l_ref):
        m2 = pl.program_id(1)
        col = lax.broadcasted_iota(jnp.int32, (N_BACK, 2 * N_BACK), 1)
        lane = lax.broadcasted_iota(jnp.int32, (N_BACK, 128), 1)
        for sub in range(2):
            rows = slice(N_BACK * sub, N_BACK * (sub + 1))
            lse_tile = jnp.zeros((N_BACK, 128), F32)
            outs = []
            for h in range(HEADS_PER_GROUP):
                hc = _head_cols(h)
                if sub == 0:
                    kk = jnp.concatenate([kp_ref[:, hc], kc_ref[0:N_BACK, hc]], axis=0)
                    vv = jnp.concatenate([vp_ref[:, hc], vc_ref[0:N_BACK, hc]], axis=0)
                else:
                    kk, vv = kc_ref[:, hc], vc_ref[:, hc]
                s = lax.dot_general(q_ref[rows, hc], kk, (((1,), (1,)), ((), ())), preferred_element_type=F32)
                s = s * scale + b_ref[h]
                if sub == 0:
                    s = jnp.where((col >= N_BACK) | (m2 > 0), s, NEG_INF)
                m = jnp.max(s, axis=-1, keepdims=True)
                p = jnp.exp(s - m)
                den = jnp.sum(p, axis=-1, keepdims=True)
                outs.append(jnp.dot(p.astype(BF16), vv, preferred_element_type=F32) / den)
                lse_tile = jnp.where(lane // 32 == h, m + jnp.log(den), lse_tile)
            o_ref[rows, :] = jnp.concatenate(outs, axis=1)
            l_ref[rows, :] = lse_tile

    def blk(part, prev):
        if prev:
            return pl.BlockSpec((N_BACK, GROUP_COLS), lambda r, n: (jnp.maximum(2 * n - 1, 0), r * nblk + part))
        return pl.BlockSpec((2 * N_BACK, GROUP_COLS), lambda r, n: (n, r * nblk + part))

    qv = qkv.reshape(tq, d * qkv.shape[1])
    o, l = pl.pallas_call(
        body, name=name,
        out_shape=[jax.ShapeDtypeStruct((tq, d * GROUP_COLS), F32), jax.ShapeDtypeStruct((tq, d * 128), F32)],
        grid=(d, tq // (2 * N_BACK)),
        in_specs=[blk(0, False), blk(1, True), blk(1, False), blk(2, True), blk(2, False),
                  pl.BlockSpec((HEADS_PER_GROUP, N_BACK, 2 * N_BACK), lambda r, n: (g, 0, 0))],
        out_specs=[pl.BlockSpec((2 * N_BACK, GROUP_COLS), lambda r, n: (n, r)),
                   pl.BlockSpec((2 * N_BACK, 128), lambda r, n: (n, r))],
        compiler_params=_params(("parallel", "parallel")),
    )(qv, qv, qv, qv, qv, bias)
    return o.reshape(t, GROUP_COLS), l.reshape(t, 128)


def _group_weights(l_refs, h):
    ls = [l_ref[:, 32 * h:32 * h + 1] for l_ref in l_refs]
    m = jnp.maximum(jnp.maximum(ls[0], ls[1]), ls[2])
    es = [jnp.exp(l - m) for l in ls]
    tot = es[0] + es[1] + es[2]
    return [e / tot for e in es]


def _attn_merge_out(name, os_, ls, wot, x, gain):
    t, d_model = x.shape
    tr = _pick(t, (512, 256, 128))

    def body(o0, o1, o2, l0, l1, l2, w_ref, x_ref, g_ref, om_ref, out_ref, h_ref):
        o_refs = (o0, o1, o2)
        pieces = [[None] * HEADS_PER_GROUP for _ in range(N_GROUPS)]
        for h in range(HEADS_PER_GROUP):
            al = _group_weights((l0, l1, l2), h)
            for g in range(N_GROUPS):
                pieces[g][h] = o_refs[g][:, _head_cols(h)] * al[g]
        om = jnp.concatenate([p for row in pieces for p in row], axis=1).astype(BF16)
        om_ref[...] = om
        out = x_ref[...] + lax.dot_general(om, w_ref[...], (((1,), (1,)), ((), ())), preferred_element_type=F32)
        out_ref[...] = out
        h_ref[...] = _rms_rows(out, g_ref[...])

    so = pl.BlockSpec((tr, GROUP_COLS), lambda i: (i, 0))
    sl = pl.BlockSpec((tr, 128), lambda i: (i, 0))
    sx = pl.BlockSpec((tr, d_model), lambda i: (i, 0))
    return pl.pallas_call(
        body, name=name,
        out_shape=[jax.ShapeDtypeStruct((t, D_ATTN), BF16), jax.ShapeDtypeStruct((t, d_model), F32),
                   jax.ShapeDtypeStruct((t, d_model), BF16)],
        grid=(t // tr,),
        in_specs=[so] * 3 + [sl] * 3 + [pl.BlockSpec((d_model, D_ATTN), lambda i: (0, 0)), sx,
                                        pl.BlockSpec((1, d_model), lambda i: (0, 0))],
        out_specs=[pl.BlockSpec((tr, D_ATTN), lambda i: (i, 0)), sx, sx],
        compiler_params=_params(("parallel",), V7X_VMEM_LIMIT_BYTES),
    )(*os_, *ls, wot, x, gain)


def _attn_bwd_prep(name, dx, wot, os_, ls):
    t, d_model = dx.shape
    tr = _pick(t, (512, 256, 128))

    def body(dx_ref, w_ref, o0, o1, o2, l0, l1, l2, d0, d1, d2, c0, c1, c2):
        o_refs, d_refs, c_refs = (o0, o1, o2), (d0, d1, d2), (c0, c1, c2)
        d_out = jnp.dot(dx_ref[...].astype(BF16), w_ref[...], preferred_element_type=F32)
        lane = lax.broadcasted_iota(jnp.int32, (tr, 128), 1)
        dos = [[None] * HEADS_PER_GROUP for _ in range(N_GROUPS)]
        cs = [jnp.zeros((tr, 128), F32) for _ in range(N_GROUPS)]
        for h in range(HEADS_PER_GROUP):
            al = _group_weights((l0, l1, l2), h)
            tot = jnp.zeros((tr, 1), F32)
            for g in range(N_GROUPS):
                dv = d_out[:, g * GROUP_COLS + h * HEAD_DIM:g * GROUP_COLS + (h + 1) * HEAD_DIM]
                tot = tot + al[g] * jnp.sum(dv * o_refs[g][:, _head_cols(h)], axis=-1, keepdims=True)
                dos[g][h] = dv * al[g]
            for g in range(N_GROUPS):
                cs[g] = jnp.where(lane // 32 == h, -al[g] * tot, cs[g])
        for g in range(N_GROUPS):
            d_refs[g][...] = jnp.concatenate(dos[g], axis=1).astype(BF16)
            c_refs[g][...] = cs[g]

    so = pl.BlockSpec((tr, GROUP_COLS), lambda i: (i, 0))
    sl = pl.BlockSpec((tr, 128), lambda i: (i, 0))
    res = pl.pallas_call(
        body, name=name,
        out_shape=[jax.ShapeDtypeStruct((t, GROUP_COLS), BF16)] * 3 + [jax.ShapeDtypeStruct((t, 128), F32)] * 3,
        grid=(t // tr,),
        in_specs=[pl.BlockSpec((tr, d_model), lambda i: (i, 0)), pl.BlockSpec((d_model, D_ATTN), lambda i: (0, 0))]
        + [so] * 3 + [sl] * 3,
        out_specs=[so] * 3 + [sl] * 3, compiler_params=_params(("parallel",), V7X_VMEM_LIMIT_BYTES),
    )(dx, wot, *os_, *ls)
    return res[:3], res[3:]


def _attn_bwd(name, qkv, do, lse, cterm, bias, g):
    t = qkv.shape[0]
    d = DILATIONS[g]
    tq = t // d
    nb = tq // N_BACK
    nblk = qkv.shape[1] // GROUP_COLS
    scale = HEAD_DIM ** -0.5
    nt = (((1,), (1,)), ((), ()))
    tn = (((0,), (0,)), ((), ()))

    def body(q2, qx, kp, k2, vp, v2, do2, dox, l2, lx, c2, cx, b_ref, dqkv_ref, db_ref):
        m2 = pl.program_id(1)

        @pl.when((m2 == 0) & (pl.program_id(0) == 0))
        def _():
            db_ref[...] = jnp.zeros_like(db_ref)

        row2 = lax.broadcasted_iota(jnp.int32, (2 * N_BACK, N_BACK), 0)
        lo, hi = slice(0, N_BACK), slice(N_BACK, 2 * N_BACK)
        for sub in range(2):
            rows = hi if sub else lo
            has_prev = True if sub else m2 > 0
            has_next = 2 * m2 + 2 < nb if sub else True
            on_ac = (row2 < N_BACK) | has_next
            dqs, dks, dvs = [], [], []
            for h in range(HEADS_PER_GROUP):
                hc = _head_cols(h)
                st = slice(32 * h, 32 * h + 1)
                b_prev, b_same = b_ref[h, :, 0:N_BACK], b_ref[h, :, N_BACK:]
                q0, k1, v1, d0 = q2[rows, hc], k2[rows, hc], v2[rows, hc], do2[rows, hc]
                l0, c0 = l2[rows, st], c2[rows, st]
                if sub:
                    k0, v0 = k2[lo, hc], v2[lo, hc]
                    q_ac = jnp.concatenate([q0, qx[:, hc]], axis=0)
                    d_ac = jnp.concatenate([d0, dox[:, hc]], axis=0)
                    l_ac = jnp.concatenate([l0, lx[:, st]], axis=0)
                    c_ac = jnp.concatenate([c0, cx[:, st]], axis=0)
                else:
                    k0, v0 = kp[:, hc], vp[:, hc]
                    q_ac, d_ac, l_ac, c_ac = q2[:, hc], do2[:, hc], l2[:, st], c2[:, st]
                s_ac = (lax.dot_general(q_ac, k1, nt, preferred_element_type=F32) * scale
                        + jnp.concatenate([b_same, b_prev], axis=0))
                p_ac = jnp.where(on_ac, jnp.exp(s_ac - l_ac), 0.0)
                ds_ac = p_ac * (lax.dot_general(d_ac, v1, nt, preferred_element_type=F32) + c_ac)
                s_b = lax.dot_general(q0, k0, nt, preferred_element_type=F32) * scale + b_prev
                p_b = jnp.where(has_prev, jnp.exp(s_b - l0), 0.0)
                ds_b = p_b * (lax.dot_general(d0, v0, nt, preferred_element_type=F32) + c0)
                ds_a = ds_ac[0:N_BACK]
                dqs.append(scale * jnp.dot(jnp.concatenate([ds_b, ds_a], axis=1).astype(BF16),
                                           jnp.concatenate([k0, k1], axis=0), preferred_element_type=F32))
                dks.append(scale * lax.dot_general(ds_ac.astype(BF16), q_ac, tn, preferred_element_type=F32))
                dvs.append(lax.dot_general(p_ac.astype(BF16), d_ac, tn, preferred_element_type=F32))
                db_ref[h, :, 0:N_BACK] += ds_b
                db_ref[h, :, N_BACK:] += ds_a
            dqkv_ref[rows, :] = jnp.concatenate(dqs + dks + dvs, axis=1).astype(BF16)

    def blk(width, which, col):
        if which == "prev":
            return pl.BlockSpec((N_BACK, width), lambda r, n: (jnp.maximum(2 * n - 1, 0), col(r)))
        if which == "next":
            return pl.BlockSpec((N_BACK, width), lambda r, n: (jnp.minimum(2 * n + 2, nb - 1), col(r)))
        return pl.BlockSpec((2 * N_BACK, width), lambda r, n: (n, col(r)))

    def qkv_blk(part, which):
        return blk(GROUP_COLS, which, lambda r: r * nblk + part)

    def grp_blk(width, which):
        return blk(width, which, lambda r: r)

    qv = qkv.reshape(tq, d * qkv.shape[1])
    dov = do.reshape(tq, d * GROUP_COLS)
    lv = lse.reshape(tq, d * 128)
    cv = cterm.reshape(tq, d * 128)
    dqkv_g, db = pl.pallas_call(
        body, name=name,
        out_shape=[jax.ShapeDtypeStruct((tq, d * 3 * GROUP_COLS), BF16),
                   jax.ShapeDtypeStruct((HEADS_PER_GROUP, N_BACK, 2 * N_BACK), F32)],
        grid=(d, nb // 2),
        in_specs=[qkv_blk(0, "same"), qkv_blk(0, "next"), qkv_blk(1, "prev"), qkv_blk(1, "same"),
                  qkv_blk(2, "prev"), qkv_blk(2, "same"), grp_blk(GROUP_COLS, "same"), grp_blk(GROUP_COLS, "next"),
                  grp_blk(128, "same"), grp_blk(128, "next"), grp_blk(128, "same"), grp_blk(128, "next"),
                  pl.BlockSpec((HEADS_PER_GROUP, N_BACK, 2 * N_BACK), lambda r, n: (g, 0, 0))],
        out_specs=[pl.BlockSpec((2 * N_BACK, 3 * GROUP_COLS), lambda r, n: (n, r)),
                   pl.BlockSpec((HEADS_PER_GROUP, N_BACK, 2 * N_BACK), lambda r, n: (0, 0, 0))],
        compiler_params=_params(("arbitrary", "arbitrary")),
    )(qv, qv, qv, qv, qv, qv, dov, dov, lv, lv, cv, cv, bias)
    return dqkv_g.reshape(t, 3 * GROUP_COLS), db


def _row(v):
    return v.reshape(1, -1)


def _glu_epi(accs, extras, rows):
    a = (accs[0] + rows[0]).astype(BF16)
    gt = (accs[1] + rows[1]).astype(BF16)
    return a, gt, a.astype(F32) * _sigmoid(gt.astype(F32))


def _swiglu_epi(accs, extras, rows):
    gq, uq = accs[0].astype(BF16), accs[1].astype(BF16)
    gf = gq.astype(F32)
    return gq, uq, gf * _sigmoid(gf) * uq.astype(F32)


def _group_rows(w):
    parts = [w[p * D_ATTN:(p + 1) * D_ATTN].reshape(N_GROUPS, GROUP_COLS, -1) for p in range(3)]
    return jnp.concatenate(parts, axis=1)


def _ungroup_rows(wg):
    return jnp.concatenate([wg[g][p * GROUP_COLS:(p + 1) * GROUP_COLS] for p in range(3) for g in range(N_GROUPS)],
                           axis=0)


def _local_step(x, target, sm, depth, fetch, emit):
    d_model = x.shape[1]
    buckets = jnp.asarray(_bucket_tables())
    bias = _bias_build("bias_build", sm["rel_bias"], buckets)
    saved = []
    h = _rmsnorm_fwd("rms_mix_fwd0", x, _row(sm["norm_mix"][0]))
    for i in range(depth):
        j = i // 2
        rec = {"x_mix": x}
        wm = fetch(2 * i, x)
        rec.update(h_mix=h, wm=wm)
        ffn_gain = _row(sm["norm_ffn"][i])
        if i % 2 == 0:
            c = wm["w2"].shape[0]
            tn = _pick(c, (512, 256, 128))
            b1 = sm["conv_b_pw1"][j]
            a, gt, glu = _mm_nt(f"conv_pw1_fwd{j}", h, [wm["w1t"]] * 2, [0, c // tn], c, _glu_epi, (BF16,) * 3,
                                rows=[_row(b1[:c]), _row(b1[c:])], tn=tn)
            dw, s = _dwconv_fwd(f"dwconv_fwd{j}", glu, wm["wdw"], _row(sm["conv_b_dw"][j]),
                                _row(sm["conv_ln_g"][j]), _row(sm["conv_ln_b"][j]))
            x, h2 = _project_residual(f"conv_pw2_fwd{j}", s, wm["w2"], x, bias=_row(sm["conv_b_pw2"][j]),
                                      gain=ffn_gain)
            rec.update(a=a, gt=gt, dw=dw, s=s)
        else:
            wq = _group_rows(wm["wqkvt"])
            qkv = _mm_nt(f"attn_qkv_fwd{j}", h, [wq[g] for g in range(N_GROUPS)], [0] * N_GROUPS, D_ATTN,
                         lambda accs, e, r: tuple(accs), (BF16,) * N_GROUPS, tn=D_ATTN)
            og = [_attn_fwd(f"attn_fwd{j}_{g}", qkv[g], bias, g) for g in range(N_GROUPS)]
            os_, ls = [o for o, _ in og], [l for _, l in og]
            om, x, h2 = _attn_merge_out(f"attn_out_fwd{j}", os_, ls, wm["wot"], x, ffn_gain)
            rec.update(qkv=qkv, os=os_, ls=ls, om=om, wq=wq)
        rec["x_ffn"] = x
        wf = fetch(2 * i + 1, x)
        f = wf["wd"].shape[0]
        gq, uq, act = _mm_nt(f"ffn_up_fwd{i}", h2, [wf["wgt"], wf["wut"]], [0, 0], f, _swiglu_epi, (BF16,) * 3)
        x, h = _project_residual(f"ffn_down_fwd{i}", act, wf["wd"], x,
                                 gain=_row(sm["norm_mix"][i + 1]) if i + 1 < depth else None)
        rec.update(h_ffn=h2, gq=gq, uq=uq, wf=wf)
        saved.append(rec)

    dx, g_final, loss_cols = _loss_head("loss_head", x, _row(sm["final_norm"]), target)

    g_mix, g_ffn = [None] * depth, [None] * depth
    nconv = (depth + 1) // 2
    g_b1, g_bdw, g_lng, g_lnb, g_b2 = ([None] * nconv for _ in range(5))
    dbias = []
    for i in reversed(range(depth)):
        j = i // 2
        rec = saved[i]
        wm, wf = rec["wm"], rec["wf"]
        dgate, dup, dwd = _ffn_down_bwd(f"ffn_down_bwd{i}", dx, wf["wd"], rec["gq"], rec["uq"])
        gf = {"wd": dwd,
              "wgt": _mm_tn(f"ffn_gate_dw{i}", dgate, rec["h_ffn"]),
              "wut": _mm_tn(f"ffn_up_dw{i}", dup, rec["h_ffn"])}
        dx, g_ffn[i] = _mm_nn_rms_bwd(f"ffn_up_bwd{i}", [dgate, dup], [wf["wgt"], wf["wut"]], rec["x_ffn"],
                                      _row(sm["norm_ffn"][i]), dx, tm=256)
        dx = emit(2 * i + 1, gf, dx)
        if i % 2 == 0:
            c = wm["w2"].shape[0]
            gm = {}
            gm["w2"], g_b2[j] = _mm_tn(f"conv_pw2_dw{j}", rec["s"], dx, colsum_b=True)
            ddw, sums = _ln_silu_bwd(f"conv_pw2_bwd{j}", dx, wm["w2"], rec["dw"], _row(sm["conv_ln_g"][j]),
                                     _row(sm["conv_ln_b"][j]))
            g_lng[j], g_lnb[j], g_bdw[j] = sums[0], sums[1], sums[2]
            du, dwk, db1 = _dwconv_bwd(f"dwconv_bwd{j}", ddw, rec["a"], rec["gt"], wm["wdw"])
            gm["wdw"] = dwk[:CONV_WIDTH]
            g_b1[j] = db1[0]
            gm["w1t"] = _mm_tn(f"conv_pw1_dw{j}", du, rec["h_mix"])
            dh_terms = ([du], [wm["w1t"]])
        else:
            gm = {"wot": _mm_tn(f"attn_out_dw{j}", dx, rec["om"])}
            dos, cs = _attn_bwd_prep(f"attn_out_bwd{j}", dx, wm["wot"], rec["os"], rec["ls"])
            back = [_attn_bwd(f"attn_bwd{j}_{g}", rec["qkv"][g], dos[g], rec["ls"][g], cs[g], bias, g)
                    for g in range(N_GROUPS)]
            dqkv = [b[0] for b in back]
            dbias.append(jnp.concatenate([b[1] for b in back], axis=0))
            dh_terms = (dqkv, [rec["wq"][g] for g in range(N_GROUPS)])
            gm["wqkvt"] = _ungroup_rows([_mm_tn(f"attn_qkv_dw{j}_{g}", dqkv[g], rec["h_mix"])
                                         for g in range(N_GROUPS)])
        dx, g_mix[i] = _mm_nn_rms_bwd(f"mix_in_bwd{i}", dh_terms[0], dh_terms[1], rec["x_mix"],
                                      _row(sm["norm_mix"][i]), dx)
        dx = emit(2 * i, gm, dx)

    gb = _bias_grad("bias_grad", dbias, buckets)
    g_rel = jnp.transpose(gb[:, :, ::32], (1, 0, 2)).reshape(N_BUCKETS, N_HEADS)
    gsm = {
        "norm_mix": jnp.concatenate(g_mix, axis=0), "norm_ffn": jnp.concatenate(g_ffn, axis=0),
        "final_norm": g_final[0], "conv_b_pw1": jnp.stack(g_b1), "conv_b_dw": jnp.stack(g_bdw),
        "conv_ln_g": jnp.stack(g_lng), "conv_ln_b": jnp.stack(g_lnb),
        "conv_b_pw2": jnp.concatenate(g_b2, axis=0), "rel_bias": g_rel,
    }
    return loss_cols, dx, gsm


SMALL = ("norm_mix", "norm_ffn", "final_norm", "conv_b_pw1", "conv_b_dw", "conv_ln_g", "conv_ln_b", "conv_b_pw2",
         "rel_bias")
SHARDED = (("conv_w_pw1", "w1t", True), ("conv_w_pw2", "w2", False), ("attn_w_qkv", "wqkvt", True),
           ("attn_w_o", "wot", True), ("ffn_w_gate", "wgt", True), ("ffn_w_up", "wut", True),
           ("ffn_w_down", "wd", False))
ORDER = ("norm_mix", "norm_ffn", "final_norm", "conv_w_pw1", "conv_b_pw1", "conv_w_dw", "conv_b_dw", "conv_ln_g",
         "conv_ln_b", "conv_w_pw2", "conv_b_pw2", "attn_w_qkv", "attn_w_o", "rel_bias", "ffn_w_gate", "ffn_w_up",
         "ffn_w_down")
PACK_LANES = 128
PACK_ROW_TILE = 8


def _pack_small(vals):
    flat = jnp.concatenate([vals[n].reshape(-1) for n in SMALL])
    per_tile = PACK_LANES * PACK_ROW_TILE
    return jnp.pad(flat, (0, -flat.shape[0] % per_tile)).reshape(-1, PACK_LANES)


def _unpack_small(pack, like):
    flat, out, pos = pack.reshape(-1), {}, 0
    for n in SMALL:
        out[n] = flat[pos:pos + like[n].size].reshape(like[n].shape)
        pos += like[n].size
    return out


def _dw_blocks(w):
    l, k, c = w.shape
    blk = jnp.transpose(w.reshape(l, k, N_DEV, c // N_DEV), (2, 0, 1, 3)).reshape(N_DEV, l * k, c // N_DEV)
    return jnp.pad(blk, ((0, 0), (0, -(l * k) % 8), (0, 0)))


def kernel(x, norm_mix, norm_ffn, final_norm, conv_w_pw1, conv_b_pw1, conv_w_dw, conv_b_dw, conv_ln_g, conv_ln_b, conv_w_pw2, conv_b_pw2, attn_w_qkv, attn_w_o, rel_bias, ffn_w_gate, ffn_w_up, ffn_w_down, loss_target, m_norm_mix, m_norm_ffn, m_final_norm, m_conv_w_pw1, m_conv_b_pw1, m_conv_w_dw, m_conv_b_dw, m_conv_ln_g, m_conv_ln_b, m_conv_w_pw2, m_conv_b_pw2, m_attn_w_qkv, m_attn_w_o, m_rel_bias, m_ffn_w_gate, m_ffn_w_up, m_ffn_w_down, v_norm_mix, v_norm_ffn, v_final_norm, v_conv_w_pw1, v_conv_b_pw1, v_conv_w_dw, v_conv_b_dw, v_conv_ln_g, v_conv_ln_b, v_conv_w_pw2, v_conv_b_pw2, v_attn_w_qkv, v_attn_w_o, v_rel_bias, v_ffn_w_gate, v_ffn_w_up, v_ffn_w_down):
    w = dict(norm_mix=norm_mix, norm_ffn=norm_ffn, final_norm=final_norm, conv_w_pw1=conv_w_pw1,
             conv_b_pw1=conv_b_pw1, conv_w_dw=conv_w_dw, conv_b_dw=conv_b_dw, conv_ln_g=conv_ln_g,
             conv_ln_b=conv_ln_b, conv_w_pw2=conv_w_pw2, conv_b_pw2=conv_b_pw2, attn_w_qkv=attn_w_qkv,
             attn_w_o=attn_w_o, rel_bias=rel_bias, ffn_w_gate=ffn_w_gate, ffn_w_up=ffn_w_up, ffn_w_down=ffn_w_down)
    m = dict(norm_mix=m_norm_mix, norm_ffn=m_norm_ffn, final_norm=m_final_norm, conv_w_pw1=m_conv_w_pw1,
             conv_b_pw1=m_conv_b_pw1, conv_w_dw=m_conv_w_dw, conv_b_dw=m_conv_b_dw, conv_ln_g=m_conv_ln_g,
             conv_ln_b=m_conv_ln_b, conv_w_pw2=m_conv_w_pw2, conv_b_pw2=m_conv_b_pw2, attn_w_qkv=m_attn_w_qkv,
             attn_w_o=m_attn_w_o, rel_bias=m_rel_bias, ffn_w_gate=m_ffn_w_gate, ffn_w_up=m_ffn_w_up,
             ffn_w_down=m_ffn_w_down)
    v = dict(norm_mix=v_norm_mix, norm_ffn=v_norm_ffn, final_norm=v_final_norm, conv_w_pw1=v_conv_w_pw1,
             conv_b_pw1=v_conv_b_pw1, conv_w_dw=v_conv_w_dw, conv_b_dw=v_conv_b_dw, conv_ln_g=v_conv_ln_g,
             conv_ln_b=v_conv_ln_b, conv_w_pw2=v_conv_w_pw2, conv_b_pw2=v_conv_b_pw2, attn_w_qkv=v_attn_w_qkv,
             attn_w_o=v_attn_w_o, rel_bias=v_rel_bias, ffn_w_gate=v_ffn_w_gate, ffn_w_up=v_ffn_w_up,
             ffn_w_down=v_ffn_w_down)

    me = 4 * lax.axis_index("x") + 2 * lax.axis_index("y") + lax.axis_index("c")
    depth = ffn_w_gate.shape[0]
    n_conv, _, cb = conv_w_dw.shape

    def sublayer(key, layer):
        if key in ("wgt", "wut", "wd"):
            return 2 * layer + 1
        return 4 * layer if key in ("w1t", "w2") else 4 * layer + 2

    def landing(block, own):
        land = lax.empty((N_DEV,) + block.shape, block.dtype)
        return lax.dynamic_update_slice(land, own[None], (me,) + (0,) * block.ndim)

    by_sub = {s: [] for s in range(2 * depth)}
    for name, key, cols in SHARDED:
        sw = (jnp.swapaxes(w[name], 1, 2) if cols else w[name]).astype(BF16)
        for layer in range(sw.shape[0]):
            by_sub[sublayer(key, layer)].append((key, layer, sw[layer]))
    likes = {s: jnp.zeros((sum(sh.size for _, _, sh in by_sub[s]) // 1024, 1024), BF16) for s in by_sub}
    dw_shard = jnp.pad(conv_w_dw.reshape(-1, cb), ((0, -(n_conv * CONV_WIDTH) % 8), (0, 0)))
    like_dw = jnp.zeros(dw_shard.shape, F32)
    groups = [([dw_shard], [landing(dw_shard, dw_shard)])]
    for s in range(2 * depth):
        groups.append(([sh for _, _, sh in by_sub[s]], [landing(sh, sh) for _, _, sh in by_sub[s]]))
    gather, _ = _exchange_start("gather_start", groups, scatter=False, carry=jnp.zeros((8, 128), F32))
    dw_filters = []

    def fetch(s, after):
        pieces = [(gather[s + 1], likes[s])]
        if s == 0:
            pieces.append((gather[0], like_dw))
        landed = _exchange_wait(f"gather_wait{s}", pieces, after)
        out = {key: g.reshape(g.shape[0] * g.shape[1], g.shape[2]) for (key, _, _), g in zip(by_sub[s], landed[0])}
        if s == 0:
            dw_all = landed[1][0]
            full = jnp.transpose(dw_all[:, :n_conv * CONV_WIDTH].reshape(N_DEV, n_conv, CONV_WIDTH, cb), (1, 2, 0, 3))
            full = jnp.pad(full.reshape(n_conv, CONV_WIDTH, N_DEV * cb), ((0, 0), (0, CONV_HALO - CONV_WIDTH), (0, 0)))
            dw_filters.extend(full[layer] for layer in range(n_conv))
        if "w1t" in out:
            out["wdw"] = dw_filters[s // 4]
        return out

    scatter, dw_grads, started = {}, {}, {}

    def emit(s, gd, carry):
        parts = [gd[key].reshape(N_DEV, -1, gd[key].shape[1]) for key, _, _ in by_sub[s]]
        if "wdw" in gd:
            dw_grads[s // 4] = gd["wdw"]
        if s == 0:
            parts.append(_dw_blocks(jnp.stack([dw_grads[layer] for layer in range(n_conv)])))
        lands = [landing(p[0], lax.dynamic_index_in_dim(p, me, 0, keepdims=False)) for p in parts]
        groups = [(parts[:len(by_sub[s])], lands[:len(by_sub[s])])]
        if s == 0:
            groups.append((parts[-1:], lands[-1:]))
        if s == 0:
            scatter[s], started[0] = _exchange_start(f"scatter_start{s}", groups, scatter=True,
                                                     carry=jnp.zeros((8, 128), F32))
            return carry
        scatter[s], carry = _exchange_start(f"scatter_start{s}", groups, scatter=True, carry=carry)
        return carry

    sm = {n: w[n] for n in SMALL}
    loss_cols, dx, gsm = _local_step(x[0], loss_target[0], sm, depth, fetch, emit)
    loss = lax.psum(jnp.sum(loss_cols), ("x", "y", "c"))

    grads, summed, delta, new_m, new_v = {}, {}, {}, {}, {}

    def reduce_pieces(subs, landed):
        for s, recv in zip(subs, landed):
            for (key, layer, _), r in zip(by_sub[s], recv):
                summed[key, layer] = _sum8(f"sum_{key}{layer}", r)

    def update(names):
        for name in names:
            shape = w[name].shape
            res = _adamw(f"adamw_{name}",
                         *[t.reshape(-1, shape[-1]) for t in (grads[name], w[name], m[name], v[name])])
            delta[name], new_m[name], new_v[name] = (t.reshape(shape) for t in res)

    def stacked(name, key, cols):
        g = jnp.stack([summed[key, layer] for layer in range(w[name].shape[0])])
        return jnp.swapaxes(g, 1, 2) if cols else g

    early = sorted((s for s in scatter if s != 0), reverse=True)
    reduce_pieces(early, _exchange_wait("scatter_wait_early", [(scatter[s][0], likes[s]) for s in early],
                                        started[0]))
    late_names = [name for name, key, _ in SHARDED if any(k == key for k, _, _ in by_sub[0])]
    for name, key, cols in SHARDED:
        if name not in late_names:
            grads[name] = stacked(name, key, cols)
    early_names = [name for name, _, _ in SHARDED if name not in late_names]
    update(early_names)
    pack = _pack_small(gsm)
    ((pack_all,),) = _exchange("gather_small_grads", [([pack], pack)], scatter=False)
    pack_sum = _sum8("sum_small", pack_all)
    grads.update(_unpack_small(pack_sum, sm))

    landed = _exchange_wait("scatter_wait_last", [(scatter[0][0], likes[0]), (scatter[0][1], like_dw)],
                            new_v[early_names[-1]])
    reduce_pieces([0], landed[:1])
    for name, key, cols in SHARDED:
        if name in late_names:
            grads[name] = stacked(name, key, cols)
    grads["conv_w_dw"] = _sum8("sum_wdw", landed[1][0])[:n_conv * CONV_WIDTH].reshape(conv_w_dw.shape)
    update(late_names + ["conv_w_dw"])
    res = _adamw("adamw_small", pack_sum, _pack_small(sm), _pack_small({n: m[n] for n in SMALL}),
                 _pack_small({n: v[n] for n in SMALL}))
    for dst, t in zip((delta, new_m, new_v), res):
        dst.update(_unpack_small(t, sm))

    outs = [loss, dx[None]]
    for d in (grads, delta, new_m, new_v):
        outs += [d[n] for n in ORDER]
    return tuple(outs)
```

```python
import functools
import math

import numpy as np
import jax
import jax.numpy as jnp
from jax import lax
from jax.experimental import pallas as pl
from jax.experimental.pallas import tpu as pltpu

F32 = jnp.float32
BF16 = jnp.bfloat16

N_DEV = 8
HEAD_DIM = 64
HEADS_PER_GROUP = 4
GROUP_COLS = HEADS_PER_GROUP * HEAD_DIM
DILATIONS = (1, 4, 16)
N_BACK = 128
N_GROUPS = 3
N_HEADS = 12
D_ATTN = 768
N_BUCKETS = 32
REL_MAX_DISTANCE = 2048
CONV_WIDTH = 31
CONV_HALO = 32
EPS = 1e-6
NEG_INF = -1e30
ADAM_LR, ADAM_B1, ADAM_B2, ADAM_EPS, ADAM_WD, ADAM_STEP = 0.001, 0.9, 0.999, 1e-08, 0.01, 10
V7X_VMEM_LIMIT_BYTES = 56 * 1024 * 1024
MESH = pl.DeviceIdType.MESH
ANY = pl.BlockSpec(memory_space=pl.ANY)


def _pick(n, prefs):
    for p in prefs:
        if n % p == 0:
            return p
    return n


def _params(sem, vmem=None):
    return pltpu.CompilerParams(dimension_semantics=sem, vmem_limit_bytes=vmem)


def _sigmoid(x):
    return 1.0 / (1.0 + jnp.exp(-x))


def _exchange(name, groups, scatter):
    n_arr = [len(arrs) for arrs, _ in groups]
    n_in = sum(n_arr) + len(groups)
    ng = len(groups)

    def body(*refs):
        ins, outs, (send_sems, recv_sems, local_sems) = refs[:n_in], refs[n_in:-3], refs[-3:]
        x, y, c = lax.axis_index("x"), lax.axis_index("y"), lax.axis_index("c")
        me = 4 * x + 2 * y + c
        pos_in = pos_out = 0
        plans = []
        for gi in range(ng):
            srcs = ins[pos_in:pos_in + n_arr[gi]]
            like = ins[pos_in + n_arr[gi]]
            dsts = outs[pos_out:pos_out + n_arr[gi]]
            pos_in += n_arr[gi] + 1
            pos_out += n_arr[gi]
            plans.append((gi, srcs, like, dsts))
        local = []
        for gi, srcs, like, dsts in plans:
            for s, d in zip(srcs, dsts):
                cp = pltpu.make_async_copy(s.at[me] if scatter else s, d.at[me], local_sems.at[gi])
                cp.start()
                local.append(cp)
        for delta in range(1, N_DEV):
            dx, dy, dc = (delta >> 2) & 1, (delta >> 1) & 1, delta & 1
            px, py, pc = (1 - x if dx else x), (1 - y if dy else y), (1 - c if dc else c)
            peer = 4 * px + 2 * py + pc
            for gi, srcs, like, dsts in plans:
                for s, d in zip(srcs, dsts):
                    pltpu.make_async_remote_copy(
                        src_ref=s.at[peer] if scatter else s, dst_ref=d.at[me],
                        send_sem=send_sems.at[gi, delta - 1], recv_sem=recv_sems.at[gi, delta - 1],
                        device_id=(px, py, pc), device_id_type=MESH).start()
        for delta in range(1, N_DEV):
            for gi, srcs, like, dsts in plans:
                pltpu.make_async_remote_copy(
                    src_ref=like, dst_ref=like, send_sem=send_sems.at[gi, delta - 1],
                    recv_sem=recv_sems.at[gi, delta - 1], device_id=(x, y, c), device_id_type=MESH).wait()
        for cp in local:
            cp.wait()

    operands, out_shape = [], []
    for arrs, like in groups:
        operands += list(arrs) + [like]
        for a in arrs:
            blk = a.shape[1:] if scatter else a.shape
            out_shape.append(jax.ShapeDtypeStruct((N_DEV,) + tuple(blk), a.dtype))
    outs = pl.pallas_call(
        body, name=name, out_shape=out_shape, in_specs=[ANY] * len(operands), out_specs=[ANY] * len(out_shape),
        scratch_shapes=[pltpu.SemaphoreType.DMA((ng, N_DEV - 1)), pltpu.SemaphoreType.DMA((ng, N_DEV - 1)),
                        pltpu.SemaphoreType.DMA((ng,))],
        compiler_params=pltpu.CompilerParams(has_side_effects=True),
    )(*operands)
    res, pos = [], 0
    for n in n_arr:
        res.append(list(outs[pos:pos + n]))
        pos += n
    return res


HBM = pl.BlockSpec(memory_space=pltpu.HBM)
SEM = pl.BlockSpec(memory_space=pltpu.SEMAPHORE)
EFFECT = pltpu.SideEffectType.DATAFLOW_SIDE_EFFECTING


def _in_hbm(a):
    return pltpu.with_memory_space_constraint(a, pltpu.HBM)


def _exchange_start(name, groups, scatter, carry):
    ns = [len(s) for s, _ in groups]
    n_in = 2 * sum(ns)

    def body(*refs):
        ins, outs = refs[:n_in], refs[n_in + 1:]
        x, y, c = lax.axis_index("x"), lax.axis_index("y"), lax.axis_index("c")
        me = 4 * x + 2 * y + c
        pi = po = 0
        for n in ns:
            srcs, lands = ins[pi:pi + n], ins[pi + n:pi + 2 * n]
            send_sems, recv_sems = outs[po], outs[po + 1]
            pi += 2 * n
            po += 2 + 2 * n
            for delta in range(1, N_DEV):
                dx, dy, dc = (delta >> 2) & 1, (delta >> 1) & 1, delta & 1
                px, py, pc = (1 - x if dx else x), (1 - y if dy else y), (1 - c if dc else c)
                peer = 4 * px + 2 * py + pc
                for s, d in zip(srcs, lands):
                    pltpu.make_async_remote_copy(
                        src_ref=s.at[peer] if scatter else s, dst_ref=d.at[me], send_sem=send_sems.at[delta - 1],
                        recv_sem=recv_sems.at[delta - 1], device_id=(px, py, pc), device_id_type=MESH).start()

    operands, out_shape, out_specs, aliases = [], [], [], {}
    for srcs, lands in groups:
        out_shape += [pltpu.SemaphoreType.DMA((N_DEV - 1,))] * 2
        out_specs += [SEM, SEM]
        for a in list(srcs) + list(lands):
            aliases[len(operands)] = len(out_shape)
            operands.append(_in_hbm(a))
            out_shape.append(pltpu.HBM(a.shape, a.dtype))
            out_specs.append(HBM)
    aliases[len(operands)] = len(out_shape)
    operands.append(_in_hbm(carry))
    out_shape.append(pltpu.HBM(carry.shape, carry.dtype))
    out_specs.append(HBM)
    outs = pl.pallas_call(
        body, name=name, out_shape=out_shape, in_specs=[HBM] * len(operands), out_specs=out_specs,
        input_output_aliases=aliases, compiler_params=pltpu.CompilerParams(has_side_effects=EFFECT),
    )(*operands)
    handles, po = [], 0
    for n in ns:
        handles.append((outs[po], outs[po + 1], list(outs[po + 2:po + 2 + n]), list(outs[po + 2 + n:po + 2 + 2 * n])))
        po += 2 + 2 * n
    return handles, outs[-1]


def _exchange_wait(name, pieces, after):
    ns = [len(h[2]) for h, _ in pieces]

    def body(*refs):
        x, y, c = lax.axis_index("x"), lax.axis_index("y"), lax.axis_index("c")
        pi = 0
        for n in ns:
            send_sems, recv_sems, like = refs[pi + 2 * n], refs[pi + 2 * n + 1], refs[pi + 2 * n + 2]
            pi += 2 * n + 3
            for delta in range(1, N_DEV):
                cp = pltpu.make_async_remote_copy(
                    src_ref=like, dst_ref=like, send_sem=send_sems.at[delta - 1], recv_sem=recv_sems.at[delta - 1],
                    device_id=(x, y, c), device_id_type=MESH)
                cp.wait_send()
                cp.wait_recv()

    operands, in_specs, out_shape, aliases = [], [], [], {}
    for (send_sems, recv_sems, srcs, lands), like in pieces:
        for a in srcs + lands:
            aliases[len(operands)] = len(out_shape)
            operands.append(a)
            in_specs.append(HBM)
            out_shape.append(pltpu.HBM(a.shape, a.dtype))
        operands += [send_sems, recv_sems, like]
        in_specs += [SEM, SEM, ANY]
    operands.append(after)
    in_specs.append(ANY)
    outs = pl.pallas_call(
        body, name=name, out_shape=out_shape, in_specs=in_specs, out_specs=[HBM] * len(out_shape),
        input_output_aliases=aliases, compiler_params=pltpu.CompilerParams(has_side_effects=EFFECT),
    )(*operands)
    res, po = [], 0
    for n in ns:
        res.append(list(outs[po + n:po + 2 * n]))
        po += 2 * n
    return res


def _sum8(name, r):
    _, rows, cols = r.shape
    tr = _pick(rows, (256, 128, 64, 32, 16, 8))

    def body(r_ref, o_ref):
        acc = r_ref[0].astype(F32)
        for p in range(1, N_DEV):
            acc = acc + r_ref[p].astype(F32)
        o_ref[...] = acc

    return pl.pallas_call(
        body, name=name, out_shape=jax.ShapeDtypeStruct((rows, cols), F32), grid=(rows // tr,),
        in_specs=[pl.BlockSpec((N_DEV, tr, cols), lambda i: (0, i, 0))],
        out_specs=pl.BlockSpec((tr, cols), lambda i: (i, 0)), compiler_params=_params(("parallel",)),
    )(r)


def _adamw(name, g, w, m, v):
    rows, cols = g.shape
    tr = _pick(rows, (512, 256, 128, 64, 32, 16, 8))
    c1 = 1.0 - ADAM_B1 ** ADAM_STEP
    c2 = 1.0 - ADAM_B2 ** ADAM_STEP

    def body(g_ref, w_ref, m_ref, v_ref, d_ref, nm_ref, nv_ref):
        gv = g_ref[...]
        nm = ADAM_B1 * m_ref[...] + (1.0 - ADAM_B1) * gv
        nv = ADAM_B2 * v_ref[...] + (1.0 - ADAM_B2) * (gv * gv)
        d_ref[...] = -ADAM_LR * ((nm / c1) / (jnp.sqrt(nv / c2) + ADAM_EPS) + ADAM_WD * w_ref[...])
        nm_ref[...] = nm
        nv_ref[...] = nv

    spec = pl.BlockSpec((tr, cols), lambda i: (i, 0))
    return pl.pallas_call(
        body, name=name, out_shape=[jax.ShapeDtypeStruct((rows, cols), F32)] * 3, grid=(rows // tr,),
        in_specs=[spec] * 4, out_specs=[spec] * 3, compiler_params=_params(("parallel",)),
    )(g, w, m, v)


def _mm_nt(name, a, ws, w_offs, n, epi, out_dtypes, extras=(), rows=(), tm=None, tn=None):
    m, k = a.shape
    tm = tm or _pick(m, (512, 256, 128))
    tn = tn or _pick(n, (1408, 1152, 1024, 512, 256, 128))
    nw, ne, nr = len(ws), len(extras), len(rows)

    def body(*refs):
        a_ref, w_refs = refs[0], refs[1:1 + nw]
        e_refs, r_refs = refs[1 + nw:1 + nw + ne], refs[1 + nw + ne:1 + nw + ne + nr]
        o_refs = refs[1 + nw + ne + nr:]
        av = a_ref[...].astype(BF16)
        accs = [lax.dot_general(av, w[...], (((1,), (1,)), ((), ())), preferred_element_type=F32) for w in w_refs]
        outs = epi(accs, [e[...] for e in e_refs], [r[...] for r in r_refs])
        for o_ref, o in zip(o_refs, outs):
            o_ref[...] = o.astype(o_ref.dtype)

    in_specs = [pl.BlockSpec((tm, k), lambda j, i: (i, 0))]
    in_specs += [pl.BlockSpec((tn, k), functools.partial(lambda j, i, off: (j + off, 0), off=off)) for off in w_offs]
    in_specs += [pl.BlockSpec((tm, tn), lambda j, i: (i, j))] * ne
    in_specs += [pl.BlockSpec((1, tn), lambda j, i: (0, j))] * nr
    return pl.pallas_call(
        body, name=name, out_shape=[jax.ShapeDtypeStruct((m, n), dt) for dt in out_dtypes],
        grid=(n // tn, m // tm), in_specs=in_specs,
        out_specs=[pl.BlockSpec((tm, tn), lambda j, i: (i, j))] * len(out_dtypes),
        compiler_params=_params(("parallel", "parallel"), V7X_VMEM_LIMIT_BYTES),
    )(a, *ws, *extras, *rows)


def _rms_rows(x, gain):
    r = lax.rsqrt(jnp.mean(x * x, axis=-1, keepdims=True) + EPS)
    return (x * r * gain).astype(BF16)


def _project_residual(name, a, b, x, bias=None, gain=None):
    m, k = a.shape
    n = b.shape[1]
    tm = _pick(m, (512, 256, 128))
    rows = [r for r in (bias, gain) if r is not None]

    def body(*refs):
        a_ref, b_ref, x_ref = refs[:3]
        r_refs = list(refs[3:3 + len(rows)])
        out = x_ref[...] + jnp.dot(a_ref[...], b_ref[...], preferred_element_type=F32)
        if bias is not None:
            out = out + r_refs.pop(0)[...]
        refs[3 + len(rows)][...] = out
        if gain is not None:
            refs[4 + len(rows)][...] = _rms_rows(out, r_refs.pop(0)[...])

    big = pl.BlockSpec((tm, n), lambda i: (i, 0))
    out_shape = [jax.ShapeDtypeStruct((m, n), F32)] + ([jax.ShapeDtypeStruct((m, n), BF16)] if gain is not None else [])
    res = pl.pallas_call(
        body, name=name, out_shape=out_shape, grid=(m // tm,),
        in_specs=[pl.BlockSpec((tm, k), lambda i: (i, 0)), pl.BlockSpec((k, n), lambda i: (0, 0)), big]
        + [pl.BlockSpec((1, n), lambda i: (0, 0))] * len(rows),
        out_specs=[big] * len(out_shape), compiler_params=_params(("parallel",), V7X_VMEM_LIMIT_BYTES),
    )(a, b, x, *rows)
    return res if gain is not None else (res[0], None)


def _mm_nn_rms_bwd(name, as_, bs, x, g, dx_out, tm=None):
    m, k = as_[0].shape
    n = bs[0].shape[1]
    tm = tm or _pick(m, (512, 256, 128))
    npair = len(as_)

    def body(*refs):
        a_refs, b_refs = refs[:npair], refs[npair:2 * npair]
        x_ref, g_ref, dxo_ref, dx_ref, dg_ref = refs[2 * npair:]
        dh = None
        for a_ref, b_ref in zip(a_refs, b_refs):
            p = jnp.dot(a_ref[...].astype(BF16), b_ref[...], preferred_element_type=F32)
            dh = p if dh is None else dh + p
        xv = x_ref[...]
        r = lax.rsqrt(jnp.mean(xv * xv, axis=-1, keepdims=True) + EPS)
        yv = xv * r
        dy = dh * g_ref[...]
        dx_ref[...] = dxo_ref[...] + r * (dy - yv * jnp.mean(dy * yv, axis=-1, keepdims=True))

        @pl.when(pl.program_id(0) == 0)
        def _():
            dg_ref[...] = jnp.zeros_like(dg_ref)

        dg_ref[...] += jnp.sum(dh * yv, axis=0, keepdims=True)

    big = pl.BlockSpec((tm, n), lambda i: (i, 0))
    row = pl.BlockSpec((1, n), lambda i: (0, 0))
    in_specs = [pl.BlockSpec((tm, k), lambda i: (i, 0))] * npair + [pl.BlockSpec((k, n), lambda i: (0, 0))] * npair
    return pl.pallas_call(
        body, name=name, out_shape=[jax.ShapeDtypeStruct((m, n), F32), jax.ShapeDtypeStruct((1, n), F32)],
        grid=(m // tm,), in_specs=in_specs + [big, row, big], out_specs=[big, row],
        compiler_params=_params(("arbitrary",), V7X_VMEM_LIMIT_BYTES),
    )(*as_, *bs, x, g, dx_out)


def _mm_tn(name, a, b, colsum_b=False, tm=None, tk=2048):
    t, ma = a.shape
    nb = b.shape[1]
    tm = tm or _pick(ma, (1408, 1152, 1024, 768, 512, 256, 128))
    tk = _pick(t, (tk, 256, 128))
    nk = t // tk

    def body(*refs):
        a_ref, b_ref, o_ref = refs[0], refs[1], refs[2]
        acc_ref = refs[-1]
        kk = pl.program_id(1)
        bv = b_ref[...]

        @pl.when(kk == 0)
        def _():
            acc_ref[...] = jnp.zeros_like(acc_ref)

        acc_ref[...] += lax.dot_general(a_ref[...].astype(BF16), bv.astype(BF16), (((0,), (0,)), ((), ())),
                                        preferred_element_type=F32)
        if colsum_b:
            s_ref = refs[3]

            @pl.when((kk == 0) & (pl.program_id(0) == 0))
            def _():
                s_ref[...] = jnp.zeros_like(s_ref)

            @pl.when(pl.program_id(0) == 0)
            def _():
                s_ref[...] += jnp.sum(bv.astype(F32), axis=0, keepdims=True)

        @pl.when(kk == nk - 1)
        def _():
            o_ref[...] = acc_ref[...].astype(o_ref.dtype)

    out_shape = [jax.ShapeDtypeStruct((ma, nb), BF16)]
    out_specs = [pl.BlockSpec((tm, nb), lambda i, kk: (i, 0))]
    if colsum_b:
        out_shape.append(jax.ShapeDtypeStruct((1, nb), F32))
        out_specs.append(pl.BlockSpec((1, nb), lambda i, kk: (0, 0)))
    res = pl.pallas_call(
        body, name=name, out_shape=out_shape, grid=(ma // tm, nk),
        in_specs=[pl.BlockSpec((tk, tm), lambda i, kk: (kk, i)), pl.BlockSpec((tk, nb), lambda i, kk: (kk, 0))],
        out_specs=out_specs, scratch_shapes=[pltpu.VMEM((tm, nb), F32)],
        compiler_params=_params(("arbitrary", "arbitrary"), V7X_VMEM_LIMIT_BYTES),
    )(a, b)
    return res if colsum_b else res[0]


def _mm_tn_pair(name, a1, a2, b, tk=1024):
    t, ma = a1.shape
    nb = b.shape[1]
    tm = _pick(ma, (1408, 1152, 1024, 768, 512, 256, 128))
    tk = _pick(t, (tk, 256, 128))
    nk = t // tk
    dims = (((0,), (0,)), ((), ()))

    def body(a1_ref, a2_ref, b_ref, o1_ref, o2_ref, acc1_ref, acc2_ref):
        kk = pl.program_id(1)
        bv = b_ref[...]

        @pl.when(kk == 0)
        def _():
            acc1_ref[...] = jnp.zeros_like(acc1_ref)
            acc2_ref[...] = jnp.zeros_like(acc2_ref)

        acc1_ref[...] += lax.dot_general(a1_ref[...], bv, dims, preferred_element_type=F32)
        acc2_ref[...] += lax.dot_general(a2_ref[...], bv, dims, preferred_element_type=F32)

        @pl.when(kk == nk - 1)
        def _():
            o1_ref[...] = acc1_ref[...].astype(BF16)
            o2_ref[...] = acc2_ref[...].astype(BF16)

    sa = pl.BlockSpec((tk, tm), lambda i, kk: (kk, i))
    so = pl.BlockSpec((tm, nb), lambda i, kk: (i, 0))
    return pl.pallas_call(
        body, name=name, out_shape=[jax.ShapeDtypeStruct((ma, nb), BF16)] * 2, grid=(ma // tm, nk),
        in_specs=[sa, sa, pl.BlockSpec((tk, nb), lambda i, kk: (kk, 0))], out_specs=[so, so],
        scratch_shapes=[pltpu.VMEM((tm, nb), F32)] * 2,
        compiler_params=_params(("arbitrary", "arbitrary"), V7X_VMEM_LIMIT_BYTES),
    )(a1, a2, b)


FFN_CHUNK = 256


def _ffn_down_bwd(name, dx, wd, gq, uq):
    t, d_model = dx.shape
    f = wd.shape[0]
    tm = _pick(t, (256, 128))
    ck = _pick(f, (FFN_CHUNK, 128))
    nt = (((1,), (1,)), ((), ()))
    tn = (((0,), (0,)), ((), ()))
    last = t // tm - 1

    def body(dx_ref, w_ref, g_ref, u_ref, dg_ref, du_ref, dw_ref, acc_ref):
        i = pl.program_id(0)

        @pl.when(i == 0)
        def _():
            acc_ref[...] = jnp.zeros_like(acc_ref)

        dxb = dx_ref[...].astype(BF16)
        for c0 in range(0, f, ck):
            cs = slice(c0, c0 + ck)
            dact = lax.dot_general(dxb, w_ref[cs, :], nt, preferred_element_type=F32)
            gf, uf = g_ref[:, cs].astype(F32), u_ref[:, cs].astype(F32)
            sg = _sigmoid(gf)
            dg_ref[:, cs] = (dact * uf * (sg * (1.0 + gf * (1.0 - sg)))).astype(BF16)
            du_ref[:, cs] = (dact * gf * sg).astype(BF16)
            act = (gf * sg * uf).astype(BF16)
            acc_ref[cs, :] += lax.dot_general(act, dxb, tn, preferred_element_type=F32)

        @pl.when(i == last)
        def _():
            dw_ref[...] = acc_ref[...].astype(BF16)

    wide = pl.BlockSpec((tm, f), lambda i: (i, 0))
    whole = pl.BlockSpec((f, d_model), lambda i: (0, 0))
    return pl.pallas_call(
        body, name=name,
        out_shape=[jax.ShapeDtypeStruct((t, f), BF16), jax.ShapeDtypeStruct((t, f), BF16),
                   jax.ShapeDtypeStruct((f, d_model), BF16)],
        grid=(t // tm,), in_specs=[pl.BlockSpec((tm, d_model), lambda i: (i, 0)), whole, wide, wide],
        out_specs=[wide, wide, whole], scratch_shapes=[pltpu.VMEM((f, d_model), F32)],
        compiler_params=_params(("arbitrary",), V7X_VMEM_LIMIT_BYTES),
    )(dx, wd, gq, uq)


def _rmsnorm_fwd(name, x, g):
    t, d = x.shape
    tr = _pick(t, (512, 256, 128))

    def body(x_ref, g_ref, h_ref):
        xv = x_ref[...]
        r = lax.rsqrt(jnp.mean(xv * xv, axis=-1, keepdims=True) + EPS)
        h_ref[...] = (xv * r * g_ref[...]).astype(BF16)

    return pl.pallas_call(
        body, name=name, out_shape=jax.ShapeDtypeStruct((t, d), BF16), grid=(t // tr,),
        in_specs=[pl.BlockSpec((tr, d), lambda i: (i, 0)), pl.BlockSpec((1, d), lambda i: (0, 0))],
        out_specs=pl.BlockSpec((tr, d), lambda i: (i, 0)), compiler_params=_params(("parallel",)),
    )(x, g)


def _loss_head(name, x, g, target):
    t, d = x.shape
    tr = _pick(t, (512, 256, 128))

    def body(x_ref, g_ref, t_ref, dx_ref, dg_ref, l_ref):
        xv = x_ref[...]
        r = lax.rsqrt(jnp.mean(xv * xv, axis=-1, keepdims=True) + EPS)
        yv = xv * r
        diff = yv * g_ref[...] - t_ref[...]
        dout = diff * (1.0 / d)
        dy = dout * g_ref[...]
        dx_ref[...] = r * (dy - yv * jnp.mean(dy * yv, axis=-1, keepdims=True))

        @pl.when(pl.program_id(0) == 0)
        def _():
            dg_ref[...] = jnp.zeros_like(dg_ref)
            l_ref[...] = jnp.zeros_like(l_ref)

        dg_ref[...] += jnp.sum(dout * yv, axis=0, keepdims=True)
        l_ref[...] += (0.5 / d) * jnp.sum(diff * diff, axis=0, keepdims=True)

    big = pl.BlockSpec((tr, d), lambda i: (i, 0))
    row = pl.BlockSpec((1, d), lambda i: (0, 0))
    return pl.pallas_call(
        body, name=name,
        out_shape=[jax.ShapeDtypeStruct((t, d), F32), jax.ShapeDtypeStruct((1, d), F32),
                   jax.ShapeDtypeStruct((1, d), F32)],
        grid=(t // tr,), in_specs=[big, row, big], out_specs=[big, row, row],
        compiler_params=_params(("arbitrary",)),
    )(x, g, target)


CONV_ROWS = 64
SUBLANES = 8


def _fill_window(win_ref, sh_ref, parts):
    rows = sh_ref.shape[2]
    for cb in range(win_ref.shape[0]):
        for r0, val in parts:
            win_ref[cb, r0:r0 + val.shape[0], :] = val[:, 128 * cb:128 * (cb + 1)]
        win_ref[cb, rows:rows + SUBLANES, :] = jnp.zeros((SUBLANES, 128), F32)
        for b in range(1, SUBLANES):
            sh_ref[b - 1, cb] = win_ref[cb, b:b + rows, :]


def _window_rows(win_ref, sh_ref, o, cb):
    b = o % SUBLANES
    if b == 0:
        return win_ref[cb, o:o + CONV_ROWS, :]
    return sh_ref[b - 1, cb, o - b:o - b + CONV_ROWS, :]


def _window_scratch(rows, c):
    return [pltpu.VMEM((c // 128, rows + SUBLANES, 128), F32), pltpu.VMEM((SUBLANES - 1, c // 128, rows, 128), F32)]


def _dwconv_fwd(name, glu, w_dw, b_dw, ln_g, ln_b):
    t, c = glu.shape
    tt = _pick(t, (256, 128))
    hb = tt // CONV_HALO

    def body(cur_ref, halo_ref, w_ref, b_ref, g_ref, be_ref, dw_ref, s_ref, win_ref, sh_ref):
        i = pl.program_id(0)
        halo = jnp.where(i > 0, halo_ref[...].astype(F32), 0.0)
        _fill_window(win_ref, sh_ref, [(0, halo), (CONV_HALO, cur_ref[...].astype(F32))])
        for r0 in range(0, tt, CONV_ROWS):
            for cb in range(c // 128):
                c0 = 128 * cb
                acc = jnp.zeros((CONV_ROWS, 128), F32) + b_ref[:, c0:c0 + 128]
                for k in range(CONV_WIDTH):
                    o = r0 + k + CONV_HALO - (CONV_WIDTH - 1)
                    acc = acc + w_ref[k:k + 1, c0:c0 + 128] * _window_rows(win_ref, sh_ref, o, cb)
                dw_ref[r0:r0 + CONV_ROWS, c0:c0 + 128] = acc
        u = dw_ref[...]
        mu = jnp.mean(u, axis=-1, keepdims=True)
        uc = u - mu
        rstd = lax.rsqrt(jnp.mean(uc * uc, axis=-1, keepdims=True) + EPS)
        z = uc * rstd * g_ref[...] + be_ref[...]
        s_ref[...] = (z * _sigmoid(z)).astype(BF16)

    big = pl.BlockSpec((tt, c), lambda i: (i, 0))
    row = pl.BlockSpec((1, c), lambda i: (0, 0))
    return pl.pallas_call(
        body, name=name, out_shape=[jax.ShapeDtypeStruct((t, c), F32), jax.ShapeDtypeStruct((t, c), BF16)],
        grid=(t // tt,),
        in_specs=[big, pl.BlockSpec((CONV_HALO, c), lambda i: (jnp.maximum(i * hb - 1, 0), 0)),
                  pl.BlockSpec((CONV_HALO, c), lambda i: (0, 0)), row, row, row],
        out_specs=[big, big], scratch_shapes=_window_scratch(tt + CONV_HALO, c),
        compiler_params=_params(("parallel",), V7X_VMEM_LIMIT_BYTES),
    )(glu, glu, w_dw, b_dw, ln_g, ln_b)


def _ln_silu_bwd(name, dx, w2, dw, ln_g, ln_b):
    t, c = dw.shape
    d_model = dx.shape[1]
    tr = _pick(t, (256, 128))

    def body(dx_ref, w_ref, dw_ref, g_ref, be_ref, o_ref, acc_ref):
        ds = lax.dot_general(dx_ref[...].astype(BF16), w_ref[...], (((1,), (1,)), ((), ())),
                             preferred_element_type=F32)
        u = dw_ref[...]
        mu = jnp.mean(u, axis=-1, keepdims=True)
        uc = u - mu
        rstd = lax.rsqrt(jnp.mean(uc * uc, axis=-1, keepdims=True) + EPS)
        xh = uc * rstd
        z = xh * g_ref[...] + be_ref[...]
        sg = _sigmoid(z)
        dz = ds * (sg * (1.0 + z * (1.0 - sg)))
        dxh = dz * g_ref[...]
        du = rstd * (dxh - jnp.mean(dxh, axis=-1, keepdims=True) - xh * jnp.mean(dxh * xh, axis=-1, keepdims=True))
        o_ref[...] = du

        @pl.when(pl.program_id(0) == 0)
        def _():
            acc_ref[...] = jnp.zeros_like(acc_ref)

        acc_ref[0:1, :] += jnp.sum(dz * xh, axis=0, keepdims=True)
        acc_ref[1:2, :] += jnp.sum(dz, axis=0, keepdims=True)
        acc_ref[2:3, :] += jnp.sum(du, axis=0, keepdims=True)

    big = pl.BlockSpec((tr, c), lambda i: (i, 0))
    row = pl.BlockSpec((1, c), lambda i: (0, 0))
    return pl.pallas_call(
        body, name=name, out_shape=[jax.ShapeDtypeStruct((t, c), F32), jax.ShapeDtypeStruct((8, c), F32)],
        grid=(t // tr,),
        in_specs=[pl.BlockSpec((tr, d_model), lambda i: (i, 0)), pl.BlockSpec((c, d_model), lambda i: (0, 0)),
                  big, row, row],
        out_specs=[big, pl.BlockSpec((8, c), lambda i: (0, 0))],
        compiler_params=_params(("arbitrary",), V7X_VMEM_LIMIT_BYTES),
    )(dx, w2, dw, ln_g, ln_b)


def _dwconv_bwd(name, ddw, a, gt, w_dw):
    t, c = ddw.shape
    tt = _pick(t, (256, 128))
    hb = tt // CONV_HALO
    last = t // tt - 1
    back = CONV_WIDTH - 1

    def body(d_ref, dn_ref, a_ref, ap_ref, g_ref, gp_ref, w_ref, du_ref, dwk_ref, db_ref,
             wd_ref, shd_ref, wg_ref, shg_ref, dg_ref, dwk8_ref):
        i = pl.program_id(0)
        _fill_window(wd_ref, shd_ref, [(0, d_ref[...]), (tt, jnp.where(i < last, dn_ref[...], 0.0))])
        glu_prev = ap_ref[...].astype(F32) * _sigmoid(gp_ref[...].astype(F32))
        av = a_ref[...].astype(F32)
        sg = _sigmoid(g_ref[...].astype(F32))
        _fill_window(wg_ref, shg_ref, [(0, jnp.where(i > 0, glu_prev, 0.0)), (CONV_HALO, av * sg)])

        @pl.when(i == 0)
        def _():
            dwk8_ref[...] = jnp.zeros_like(dwk8_ref)
            db_ref[...] = jnp.zeros_like(db_ref)

        for r0 in range(0, tt, CONV_ROWS):
            for cb in range(c // 128):
                c0 = 128 * cb
                dcur = wd_ref[cb, r0:r0 + CONV_ROWS, :]
                acc = jnp.zeros((CONV_ROWS, 128), F32)
                for k in range(CONV_WIDTH):
                    acc = acc + w_ref[k:k + 1, c0:c0 + 128] * _window_rows(wd_ref, shd_ref, r0 + back - k, cb)
                    p = dcur * _window_rows(wg_ref, shg_ref, r0 + k + CONV_HALO - back, cb)
                    s8 = p[0:SUBLANES]
                    for q in range(SUBLANES, CONV_ROWS, SUBLANES):
                        s8 = s8 + p[q:q + SUBLANES]
                    dwk8_ref[SUBLANES * k:SUBLANES * (k + 1), c0:c0 + 128] += s8
                dg_ref[r0:r0 + CONV_ROWS, c0:c0 + 128] = acc

        @pl.when(i == last)
        def _():
            for k in range(CONV_WIDTH):
                dwk_ref[k:k + 1, :] = jnp.sum(dwk8_ref[SUBLANES * k:SUBLANES * (k + 1), :], axis=0, keepdims=True)
            dwk_ref[CONV_WIDTH:, :] = jnp.zeros((CONV_HALO - CONV_WIDTH, c), F32)
        dglu = dg_ref[...]
        da = dglu * sg
        dgate = dglu * av * sg * (1.0 - sg)
        du_ref[:, 0:c] = da.astype(BF16)
        du_ref[:, c:] = dgate.astype(BF16)
        db_ref[:, 0:c] += jnp.sum(da, axis=0, keepdims=True)
        db_ref[:, c:] += jnp.sum(dgate, axis=0, keepdims=True)

    big = pl.BlockSpec((tt, c), lambda i: (i, 0))
    prev = pl.BlockSpec((CONV_HALO, c), lambda i: (jnp.maximum(i * hb - 1, 0), 0))
    nxt = pl.BlockSpec((CONV_HALO, c), lambda i: (jnp.minimum((i + 1) * hb, t // CONV_HALO - 1), 0))
    return pl.pallas_call(
        body, name=name,
        out_shape=[jax.ShapeDtypeStruct((t, 2 * c), BF16), jax.ShapeDtypeStruct((CONV_HALO, c), F32),
                   jax.ShapeDtypeStruct((1, 2 * c), F32)],
        grid=(t // tt,),
        in_specs=[big, nxt, big, prev, big, prev, pl.BlockSpec((CONV_HALO, c), lambda i: (0, 0))],
        out_specs=[pl.BlockSpec((tt, 2 * c), lambda i: (i, 0)), pl.BlockSpec((CONV_HALO, c), lambda i: (0, 0)),
                   pl.BlockSpec((1, 2 * c), lambda i: (0, 0))],
        scratch_shapes=_window_scratch(tt + CONV_HALO, c) + _window_scratch(tt + CONV_HALO, c)
        + [pltpu.VMEM((tt, c), F32), pltpu.VMEM((SUBLANES * CONV_HALO, c), F32)],
        compiler_params=_params(("arbitrary",), V7X_VMEM_LIMIT_BYTES),
    )(ddw, ddw, a, a, gt, gt, w_dw)


def _bucket_tables():
    i = np.arange(N_BACK)[:, None]
    j = np.arange(2 * N_BACK)[None, :]
    dist = i + N_BACK - j
    valid = (dist >= 0) & (dist <= N_BACK)
    max_exact = N_BUCKETS // 2
    out = []
    for d in DILATIONS:
        n = np.maximum(dist * d, 0)
        nf = np.maximum(n, 1).astype(np.float32)
        large = max_exact + (np.log(nf / np.float32(max_exact)) / np.float32(math.log(REL_MAX_DISTANCE / max_exact))
                             * np.float32(N_BUCKETS - max_exact)).astype(np.int32)
        large = np.minimum(large, N_BUCKETS - 1)
        out.append(np.where(valid, np.where(n < max_exact, n, large), -1))
    return np.stack(out).astype(np.int32)


def _bias_build(name, rel_bias, buckets):
    def body(tbl_ref, bk_ref, o_ref):
        g = pl.program_id(0)
        bk = bk_ref[0]
        for h in range(HEADS_PER_GROUP):
            acc = jnp.zeros(bk.shape, F32)
            for b in range(N_BUCKETS):
                acc = jnp.where(bk == b, tbl_ref[b, g * HEADS_PER_GROUP + h], acc)
            o_ref[h] = jnp.where(bk < 0, NEG_INF, acc)

    return pl.pallas_call(
        body, name=name, out_shape=jax.ShapeDtypeStruct((N_HEADS, N_BACK, 2 * N_BACK), F32), grid=(N_GROUPS,),
        in_specs=[pl.BlockSpec(memory_space=pltpu.SMEM), pl.BlockSpec((1, N_BACK, 2 * N_BACK), lambda g: (g, 0, 0))],
        out_specs=pl.BlockSpec((HEADS_PER_GROUP, N_BACK, 2 * N_BACK), lambda g: (g, 0, 0)),
        compiler_params=_params(("arbitrary",)),
    )(rel_bias, buckets)


def _bias_grad(name, dbs, buckets):
    nd = len(dbs)

    def body(*refs):
        bk = refs[nd][0]
        o_ref = refs[nd + 1]
        lane = lax.broadcasted_iota(jnp.int32, (1, 128), 1)
        db = [sum(r[h] for r in refs[:nd]) for h in range(HEADS_PER_GROUP)]
        for b in range(N_BUCKETS):
            row = jnp.zeros((1, 128), F32)
            for h in range(HEADS_PER_GROUP):
                s = jnp.sum(jnp.where(bk == b, db[h], 0.0), axis=0, keepdims=True)
                s = jnp.sum(s, axis=1, keepdims=True)
                row = jnp.where(lane // 32 == h, s, row)
            o_ref[0, b:b + 1, :] = row

    spec = pl.BlockSpec((HEADS_PER_GROUP, N_BACK, 2 * N_BACK), lambda g: (g, 0, 0))
    return pl.pallas_call(
        body, name=name, out_shape=jax.ShapeDtypeStruct((N_GROUPS, N_BUCKETS, 128), F32), grid=(N_GROUPS,),
        in_specs=[spec] * nd + [pl.BlockSpec((1, N_BACK, 2 * N_BACK), lambda g: (g, 0, 0))],
        out_specs=pl.BlockSpec((1, N_BUCKETS, 128), lambda g: (g, 0, 0)), compiler_params=_params(("arbitrary",)),
    )(*dbs, buckets)


def _head_cols(h):
    return slice(h * HEAD_DIM, (h + 1) * HEAD_DIM)


def _attn_fwd(name, qkv, bias, g):
    t = qkv.shape[0]
    d = DILATIONS[g]
    tq = t // d
    nblk = qkv.shape[1] // GROUP_COLS
    scale = HEAD_DIM ** -0.5

    def body(q_ref, kp_ref, kc_ref, vp_ref, vc_ref, b_ref, o_ref, l_ref):
        m2 = pl.program_id(1)
        col = lax.broadcasted_iota(jnp.int32, (N_BACK, 2 * N_BACK), 1)
        lane = lax.broadcasted_iota(jnp.int32, (N_BACK, 128), 1)
        for sub in range(2):
            rows = slice(N_BACK * sub, N_BACK * (sub + 1))
            lse_tile = jnp.zeros((N_BACK, 128), F32)
            outs = []
            for h in range(HEADS_PER_GROUP):
                hc = _head_cols(h)
                if sub == 0:
                    kk = jnp.concatenate([kp_ref[:, hc], kc_ref[0:N_BACK, hc]], axis=0)
                    vv = jnp.concatenate([vp_ref[:, hc], vc_ref[0:N_BACK, hc]], axis=0)
                else:
                    kk, vv = kc_ref[:, hc], vc_ref[:, hc]
                s = lax.dot_general(q_ref[rows, hc], kk, (((1,), (1,)), ((), ())), preferred_element_type=F32)
                s = s * scale + b_ref[h]
                if sub == 0:
                    s = jnp.where((col >= N_BACK) | (m2 > 0), s, NEG_INF)
                m = jnp.max(s, axis=-1, keepdims=True)
                p = jnp.exp(s - m)
                den = jnp.sum(p, axis=-1, keepdims=True)
                outs.append(jnp.dot(p.astype(BF16), vv, preferred_element_type=F32) / den)
                lse_tile = jnp.where(lane // 32 == h, m + jnp.log(den), lse_tile)
            o_ref[rows, :] = jnp.concatenate(outs, axis=1)
            l_ref[rows, :] = lse_tile

    def blk(part, prev):
        if prev:
            return pl.BlockSpec((N_BACK, GROUP_COLS), lambda r, n: (jnp.maximum(2 * n - 1, 0), r * nblk + part))
        return pl.BlockSpec((2 * N_BACK, GROUP_COLS), lambda r, n: (n, r * nblk + part))

    qv = qkv.reshape(tq, d * qkv.shape[1])
    o, l = pl.pallas_call(
        body, name=name,
        out_shape=[jax.ShapeDtypeStruct((tq, d * GROUP_COLS), F32), jax.ShapeDtypeStruct((tq, d * 128), F32)],
        grid=(d, tq // (2 * N_BACK)),
        in_specs=[blk(0, False), blk(1, True), blk(1, False), blk(2, True), blk(2, False),
                  pl.BlockSpec((HEADS_PER_GROUP, N_BACK, 2 * N_BACK), lambda r, n: (g, 0, 0))],
        out_specs=[pl.BlockSpec((2 * N_BACK, GROUP_COLS), lambda r, n: (n, r)),
                   pl.BlockSpec((2 * N_BACK, 128), lambda r, n: (n, r))],
        compiler_params=_params(("parallel", "parallel")),
    )(qv, qv, qv, qv, qv, bias)
    return o.reshape(t, GROUP_COLS), l.reshape(t, 128)


def _group_weights(l_refs, h):
    ls = [l_ref[:, 32 * h:32 * h + 1] for l_ref in l_refs]
    m = jnp.maximum(jnp.maximum(ls[0], ls[1]), ls[2])
    es = [jnp.exp(l - m) for l in ls]
    tot = es[0] + es[1] + es[2]
    return [e / tot for e in es]


def _attn_merge_out(name, os_, ls, wot, x, gain):
    t, d_model = x.shape
    tr = _pick(t, (512, 256, 128))

    def body(o0, o1, o2, l0, l1, l2, w_ref, x_ref, g_ref, om_ref, out_ref, h_ref):
        o_refs = (o0, o1, o2)
        pieces = [[None] * HEADS_PER_GROUP for _ in range(N_GROUPS)]
        for h in range(HEADS_PER_GROUP):
            al = _group_weights((l0, l1, l2), h)
            for g in range(N_GROUPS):
                pieces[g][h] = o_refs[g][:, _head_cols(h)] * al[g]
        om = jnp.concatenate([p for row in pieces for p in row], axis=1).astype(BF16)
        om_ref[...] = om
        out = x_ref[...] + lax.dot_general(om, w_ref[...], (((1,), (1,)), ((), ())), preferred_element_type=F32)
        out_ref[...] = out
        h_ref[...] = _rms_rows(out, g_ref[...])

    so = pl.BlockSpec((tr, GROUP_COLS), lambda i: (i, 0))
    sl = pl.BlockSpec((tr, 128), lambda i: (i, 0))
    sx = pl.BlockSpec((tr, d_model), lambda i: (i, 0))
    return pl.pallas_call(
        body, name=name,
        out_shape=[jax.ShapeDtypeStruct((t, D_ATTN), BF16), jax.ShapeDtypeStruct((t, d_model), F32),
                   jax.ShapeDtypeStruct((t, d_model), BF16)],
        grid=(t // tr,),
        in_specs=[so] * 3 + [sl] * 3 + [pl.BlockSpec((d_model, D_ATTN), lambda i: (0, 0)), sx,
                                        pl.BlockSpec((1, d_model), lambda i: (0, 0))],
        out_specs=[pl.BlockSpec((tr, D_ATTN), lambda i: (i, 0)), sx, sx],
        compiler_params=_params(("parallel",), V7X_VMEM_LIMIT_BYTES),
    )(*os_, *ls, wot, x, gain)


def _attn_bwd_prep(name, dx, wot, os_, ls):
    t, d_model = dx.shape
    tr = _pick(t, (512, 256, 128))

    def body(dx_ref, w_ref, o0, o1, o2, l0, l1, l2, d0, d1, d2, c0, c1, c2):
        o_refs, d_refs, c_refs = (o0, o1, o2), (d0, d1, d2), (c0, c1, c2)
        d_out = jnp.dot(dx_ref[...].astype(BF16), w_ref[...], preferred_element_type=F32)
        lane = lax.broadcasted_iota(jnp.int32, (tr, 128), 1)
        dos = [[None] * HEADS_PER_GROUP for _ in range(N_GROUPS)]
        cs = [jnp.zeros((tr, 128), F32) for _ in range(N_GROUPS)]
        for h in range(HEADS_PER_GROUP):
            al = _group_weights((l0, l1, l2), h)
            tot = jnp.zeros((tr, 1), F32)
            for g in range(N_GROUPS):
                dv = d_out[:, g * GROUP_COLS + h * HEAD_DIM:g * GROUP_COLS + (h + 1) * HEAD_DIM]
                tot = tot + al[g] * jnp.sum(dv * o_refs[g][:, _head_cols(h)], axis=-1, keepdims=True)
                dos[g][h] = dv * al[g]
            for g in range(N_GROUPS):
                cs[g] = jnp.where(lane // 32 == h, -al[g] * tot, cs[g])
        for g in range(N_GROUPS):
            d_refs[g][...] = jnp.concatenate(dos[g], axis=1).astype(BF16)
            c_refs[g][...] = cs[g]

    so = pl.BlockSpec((tr, GROUP_COLS), lambda i: (i, 0))
    sl = pl.BlockSpec((tr, 128), lambda i: (i, 0))
    res = pl.pallas_call(
        body, name=name,
        out_shape=[jax.ShapeDtypeStruct((t, GROUP_COLS), BF16)] * 3 + [jax.ShapeDtypeStruct((t, 128), F32)] * 3,
        grid=(t // tr,),
        in_specs=[pl.BlockSpec((tr, d_model), lambda i: (i, 0)), pl.BlockSpec((d_model, D_ATTN), lambda i: (0, 0))]
        + [so] * 3 + [sl] * 3,
        out_specs=[so] * 3 + [sl] * 3, compiler_params=_params(("parallel",), V7X_VMEM_LIMIT_BYTES),
    )(dx, wot, *os_, *ls)
    return res[:3], res[3:]


def _attn_bwd(name, qkv, do, lse, cterm, bias, g):
    t = qkv.shape[0]
    d = DILATIONS[g]
    tq = t // d
    nb = tq // N_BACK
    nblk = qkv.shape[1] // GROUP_COLS
    scale = HEAD_DIM ** -0.5
    nt = (((1,), (1,)), ((), ()))
    tn = (((0,), (0,)), ((), ()))

    def body(q2, qx, kp, k2, vp, v2, do2, dox, l2, lx, c2, cx, b_ref, dqkv_ref, db_ref):
        m2 = pl.program_id(1)

        @pl.when((m2 == 0) & (pl.program_id(0) == 0))
        def _():
            db_ref[...] = jnp.zeros_like(db_ref)

        row2 = lax.broadcasted_iota(jnp.int32, (2 * N_BACK, N_BACK), 0)
        lo, hi = slice(0, N_BACK), slice(N_BACK, 2 * N_BACK)
        for sub in range(2):
            rows = hi if sub else lo
            has_prev = True if sub else m2 > 0
            has_next = 2 * m2 + 2 < nb if sub else True
            on_ac = (row2 < N_BACK) | has_next
            dqs, dks, dvs = [], [], []
            for h in range(HEADS_PER_GROUP):
                hc = _head_cols(h)
                st = slice(32 * h, 32 * h + 1)
                b_prev, b_same = b_ref[h, :, 0:N_BACK], b_ref[h, :, N_BACK:]
                q0, k1, v1, d0 = q2[rows, hc], k2[rows, hc], v2[rows, hc], do2[rows, hc]
                l0, c0 = l2[rows, st], c2[rows, st]
                if sub:
                    k0, v0 = k2[lo, hc], v2[lo, hc]
                    q_ac = jnp.concatenate([q0, qx[:, hc]], axis=0)
                    d_ac = jnp.concatenate([d0, dox[:, hc]], axis=0)
                    l_ac = jnp.concatenate([l0, lx[:, st]], axis=0)
                    c_ac = jnp.concatenate([c0, cx[:, st]], axis=0)
                else:
                    k0, v0 = kp[:, hc], vp[:, hc]
                    q_ac, d_ac, l_ac, c_ac = q2[:, hc], do2[:, hc], l2[:, st], c2[:, st]
                s_ac = (lax.dot_general(q_ac, k1, nt, preferred_element_type=F32) * scale
                        + jnp.concatenate([b_same, b_prev], axis=0))
                p_ac = jnp.where(on_ac, jnp.exp(s_ac - l_ac), 0.0)
                ds_ac = p_ac * (lax.dot_general(d_ac, v1, nt, preferred_element_type=F32) + c_ac)
                s_b = lax.dot_general(q0, k0, nt, preferred_element_type=F32) * scale + b_prev
                p_b = jnp.where(has_prev, jnp.exp(s_b - l0), 0.0)
                ds_b = p_b * (lax.dot_general(d0, v0, nt, preferred_element_type=F32) + c0)
                ds_a = ds_ac[0:N_BACK]
                dqs.append(scale * jnp.dot(jnp.concatenate([ds_b, ds_a], axis=1).astype(BF16),
                                           jnp.concatenate([k0, k1], axis=0), preferred_element_type=F32))
                dks.append(scale * lax.dot_general(ds_ac.astype(BF16), q_ac, tn, preferred_element_type=F32))
                dvs.append(lax.dot_general(p_ac.astype(BF16), d_ac, tn, preferred_element_type=F32))
                db_ref[h, :, 0:N_BACK] += ds_b
                db_ref[h, :, N_BACK:] += ds_a
            dqkv_ref[rows, :] = jnp.concatenate(dqs + dks + dvs, axis=1).astype(BF16)

    def blk(width, which, col):
        if which == "prev":
            return pl.BlockSpec((N_BACK, width), lambda r, n: (jnp.maximum(2 * n - 1, 0), col(r)))
        if which == "next":
            return pl.BlockSpec((N_BACK, width), lambda r, n: (jnp.minimum(2 * n + 2, nb - 1), col(r)))
        return pl.BlockSpec((2 * N_BACK, width), lambda r, n: (n, col(r)))

    def qkv_blk(part, which):
        return blk(GROUP_COLS, which, lambda r: r * nblk + part)

    def grp_blk(width, which):
        return blk(width, which, lambda r: r)

    qv = qkv.reshape(tq, d * qkv.shape[1])
    dov = do.reshape(tq, d * GROUP_COLS)
    lv = lse.reshape(tq, d * 128)
    cv = cterm.reshape(tq, d * 128)
    dqkv_g, db = pl.pallas_call(
        body, name=name,
        out_shape=[jax.ShapeDtypeStruct((tq, d * 3 * GROUP_COLS), BF16),
                   jax.ShapeDtypeStruct((HEADS_PER_GROUP, N_BACK, 2 * N_BACK), F32)],
        grid=(d, nb // 2),
        in_specs=[qkv_blk(0, "same"), qkv_blk(0, "next"), qkv_blk(1, "prev"), qkv_blk(1, "same"),
                  qkv_blk(2, "prev"), qkv_blk(2, "same"), grp_blk(GROUP_COLS, "same"), grp_blk(GROUP_COLS, "next"),
                  grp_blk(128, "same"), grp_blk(128, "next"), grp_blk(128, "same"), grp_blk(128, "next"),
                  pl.BlockSpec((HEADS_PER_GROUP, N_BACK, 2 * N_BACK), lambda r, n: (g, 0, 0))],
        out_specs=[pl.BlockSpec((2 * N_BACK, 3 * GROUP_COLS), lambda r, n: (n, r)),
                   pl.BlockSpec((HEADS_PER_GROUP, N_BACK, 2 * N_BACK), lambda r, n: (0, 0, 0))],
        compiler_params=_params(("arbitrary", "arbitrary")),
    )(qv, qv, qv, qv, qv, qv, dov, dov, lv, lv, cv, cv, bias)
    return dqkv_g.reshape(t, 3 * GROUP_COLS), db


def _row(v):
    return v.reshape(1, -1)


def _glu_epi(accs, extras, rows):
    a = (accs[0] + rows[0]).astype(BF16)
    gt = (accs[1] + rows[1]).astype(BF16)
    return a, gt, a.astype(F32) * _sigmoid(gt.astype(F32))


def _swiglu_epi(accs, extras, rows):
    gq, uq = accs[0].astype(BF16), accs[1].astype(BF16)
    gf = gq.astype(F32)
    return gq, uq, gf * _sigmoid(gf) * uq.astype(F32)


def _group_rows(w):
    parts = [w[p * D_ATTN:(p + 1) * D_ATTN].reshape(N_GROUPS, GROUP_COLS, -1) for p in range(3)]
    return jnp.concatenate(parts, axis=1)


def _ungroup_rows(wg):
    return jnp.concatenate([wg[g][p * GROUP_COLS:(p + 1) * GROUP_COLS] for p in range(3) for g in range(N_GROUPS)],
                           axis=0)


def _local_step(x, target, sm, depth, fetch, emit):
    d_model = x.shape[1]
    buckets = jnp.asarray(_bucket_tables())
    bias = _bias_build("bias_build", sm["rel_bias"], buckets)
    saved = []
    h = _rmsnorm_fwd("rms_mix_fwd0", x, _row(sm["norm_mix"][0]))
    for i in range(depth):
        j = i // 2
        rec = {"x_mix": x}
        wm = fetch(2 * i, x)
        rec.update(h_mix=h, wm=wm)
        ffn_gain = _row(sm["norm_ffn"][i])
        if i % 2 == 0:
            c = wm["w1t"].shape[0] // 2
            tn = _pick(c, (1024, 512, 256, 128))
            b1 = sm["conv_b_pw1"][j]
            a, gt, glu = _mm_nt(f"conv_pw1_fwd{j}", h, [wm["w1t"]] * 2, [0, c // tn], c, _glu_epi, (BF16,) * 3,
                                rows=[_row(b1[:c]), _row(b1[c:])], tn=tn)
            wm.update(fetch(2 * i, a, 1))
            dw, s = _dwconv_fwd(f"dwconv_fwd{j}", glu, wm["wdw"], _row(sm["conv_b_dw"][j]),
                                _row(sm["conv_ln_g"][j]), _row(sm["conv_ln_b"][j]))
            x, h2 = _project_residual(f"conv_pw2_fwd{j}", s, wm["w2"], x, bias=_row(sm["conv_b_pw2"][j]),
                                      gain=ffn_gain)
            rec.update(a=a, gt=gt, dw=dw, s=s)
        else:
            wq = _group_rows(wm["wqkvt"])
            qkv = _mm_nt(f"attn_qkv_fwd{j}", h, [wq[g] for g in range(N_GROUPS)], [0] * N_GROUPS, D_ATTN,
                         lambda accs, e, r: tuple(accs), (BF16,) * N_GROUPS, tn=D_ATTN)
            og = [_attn_fwd(f"attn_fwd{j}_{g}", qkv[g], bias, g) for g in range(N_GROUPS)]
            os_, ls = [o for o, _ in og], [l for _, l in og]
            om, x, h2 = _attn_merge_out(f"attn_out_fwd{j}", os_, ls, wm["wot"], x, ffn_gain)
            rec.update(qkv=qkv, os=os_, ls=ls, om=om, wq=wq)
        rec["x_ffn"] = x
        wf = fetch(2 * i + 1, x)
        f = wf["wd"].shape[0]
        gq, uq, act = _mm_nt(f"ffn_up_fwd{i}", h2, [wf["wgt"], wf["wut"]], [0, 0], f, _swiglu_epi, (BF16,) * 3)
        x, h = _project_residual(f"ffn_down_fwd{i}", act, wf["wd"], x,
                                 gain=_row(sm["norm_mix"][i + 1]) if i + 1 < depth else None)
        rec.update(h_ffn=h2, gq=gq, uq=uq, wf=wf)
        saved.append(rec)

    dx, g_final, loss_cols = _loss_head("loss_head", x, _row(sm["final_norm"]), target)

    g_mix, g_ffn = [None] * depth, [None] * depth
    nconv = (depth + 1) // 2
    g_b1, g_bdw, g_lng, g_lnb, g_b2 = ([None] * nconv for _ in range(5))
    dbias = []
    for i in reversed(range(depth)):
        j = i // 2
        rec = saved[i]
        wm, wf = rec["wm"], rec["wf"]
        dgate, dup, dwd = _ffn_down_bwd(f"ffn_down_bwd{i}", dx, wf["wd"], rec["gq"], rec["uq"])
        gf = {"wd": dwd}
        gf["wgt"], gf["wut"] = _mm_tn_pair(f"ffn_gate_up_dw{i}", dgate, dup, rec["h_ffn"])
        dx, g_ffn[i] = _mm_nn_rms_bwd(f"ffn_up_bwd{i}", [dgate, dup], [wf["wgt"], wf["wut"]], rec["x_ffn"],
                                      _row(sm["norm_ffn"][i]), dx, tm=256)
        dx = emit(2 * i + 1, gf, dx)
        if i % 2 == 0:
            c = wm["w2"].shape[0]
            gm = {}
            gm["w2"], g_b2[j] = _mm_tn(f"conv_pw2_dw{j}", rec["s"], dx, colsum_b=True)
            ddw, sums = _ln_silu_bwd(f"conv_pw2_bwd{j}", dx, wm["w2"], rec["dw"], _row(sm["conv_ln_g"][j]),
                                     _row(sm["conv_ln_b"][j]))
            g_lng[j], g_lnb[j], g_bdw[j] = sums[0], sums[1], sums[2]
            du, dwk, db1 = _dwconv_bwd(f"dwconv_bwd{j}", ddw, rec["a"], rec["gt"], wm["wdw"])
            gm["wdw"] = dwk[:CONV_WIDTH]
            g_b1[j] = db1[0]
            gm["w1t"] = _mm_tn(f"conv_pw1_dw{j}", du, rec["h_mix"])
            dh_terms = ([du], [wm["w1t"]])
        else:
            gm = {"wot": _mm_tn(f"attn_out_dw{j}", dx, rec["om"])}
            dos, cs = _attn_bwd_prep(f"attn_out_bwd{j}", dx, wm["wot"], rec["os"], rec["ls"])
            back = [_attn_bwd(f"attn_bwd{j}_{g}", rec["qkv"][g], dos[g], rec["ls"][g], cs[g], bias, g)
                    for g in range(N_GROUPS)]
            dqkv = [b[0] for b in back]
            dbias.append(jnp.concatenate([b[1] for b in back], axis=0))
            dh_terms = (dqkv, [rec["wq"][g] for g in range(N_GROUPS)])
            gm["wqkvt"] = _ungroup_rows([_mm_tn(f"attn_qkv_dw{j}_{g}", dqkv[g], rec["h_mix"])
                                         for g in range(N_GROUPS)])
        dx, g_mix[i] = _mm_nn_rms_bwd(f"mix_in_bwd{i}", dh_terms[0], dh_terms[1], rec["x_mix"],
                                      _row(sm["norm_mix"][i]), dx)
        dx = emit(2 * i, gm, dx)

    gb = _bias_grad("bias_grad", dbias, buckets)
    g_rel = jnp.transpose(gb[:, :, ::32], (1, 0, 2)).reshape(N_BUCKETS, N_HEADS)
    gsm = {
        "norm_mix": jnp.concatenate(g_mix, axis=0), "norm_ffn": jnp.concatenate(g_ffn, axis=0),
        "final_norm": g_final[0], "conv_b_pw1": jnp.stack(g_b1), "conv_b_dw": jnp.stack(g_bdw),
        "conv_ln_g": jnp.stack(g_lng), "conv_ln_b": jnp.stack(g_lnb),
        "conv_b_pw2": jnp.concatenate(g_b2, axis=0), "rel_bias": g_rel,
    }
    return loss_cols, dx, gsm


SMALL = ("norm_mix", "norm_ffn", "final_norm", "conv_b_pw1", "conv_b_dw", "conv_ln_g", "conv_ln_b", "conv_b_pw2",
         "rel_bias")
SHARDED = (("conv_w_pw1", "w1t", True), ("conv_w_pw2", "w2", False), ("attn_w_qkv", "wqkvt", True),
           ("attn_w_o", "wot", True), ("ffn_w_gate", "wgt", True), ("ffn_w_up", "wut", True),
           ("ffn_w_down", "wd", False))
ORDER = ("norm_mix", "norm_ffn", "final_norm", "conv_w_pw1", "conv_b_pw1", "conv_w_dw", "conv_b_dw", "conv_ln_g",
         "conv_ln_b", "conv_w_pw2", "conv_b_pw2", "attn_w_qkv", "attn_w_o", "rel_bias", "ffn_w_gate", "ffn_w_up",
         "ffn_w_down")
PACK_LANES = 128
PACK_ROW_TILE = 8


def _pack_small(vals):
    flat = jnp.concatenate([vals[n].reshape(-1) for n in SMALL])
    per_tile = PACK_LANES * PACK_ROW_TILE
    return jnp.pad(flat, (0, -flat.shape[0] % per_tile)).reshape(-1, PACK_LANES)


def _unpack_small(pack, like):
    flat, out, pos = pack.reshape(-1), {}, 0
    for n in SMALL:
        out[n] = flat[pos:pos + like[n].size].reshape(like[n].shape)
        pos += like[n].size
    return out


def _dw_blocks(w):
    l, k, c = w.shape
    blk = jnp.transpose(w.reshape(l, k, N_DEV, c // N_DEV), (2, 0, 1, 3)).reshape(N_DEV, l * k, c // N_DEV)
    return jnp.pad(blk, ((0, 0), (0, -(l * k) % 8), (0, 0)))


def kernel(x, norm_mix, norm_ffn, final_norm, conv_w_pw1, conv_b_pw1, conv_w_dw, conv_b_dw, conv_ln_g, conv_ln_b, conv_w_pw2, conv_b_pw2, attn_w_qkv, attn_w_o, rel_bias, ffn_w_gate, ffn_w_up, ffn_w_down, loss_target, m_norm_mix, m_norm_ffn, m_final_norm, m_conv_w_pw1, m_conv_b_pw1, m_conv_w_dw, m_conv_b_dw, m_conv_ln_g, m_conv_ln_b, m_conv_w_pw2, m_conv_b_pw2, m_attn_w_qkv, m_attn_w_o, m_rel_bias, m_ffn_w_gate, m_ffn_w_up, m_ffn_w_down, v_norm_mix, v_norm_ffn, v_final_norm, v_conv_w_pw1, v_conv_b_pw1, v_conv_w_dw, v_conv_b_dw, v_conv_ln_g, v_conv_ln_b, v_conv_w_pw2, v_conv_b_pw2, v_attn_w_qkv, v_attn_w_o, v_rel_bias, v_ffn_w_gate, v_ffn_w_up, v_ffn_w_down):
    w = dict(norm_mix=norm_mix, norm_ffn=norm_ffn, final_norm=final_norm, conv_w_pw1=conv_w_pw1,
             conv_b_pw1=conv_b_pw1, conv_w_dw=conv_w_dw, conv_b_dw=conv_b_dw, conv_ln_g=conv_ln_g,
             conv_ln_b=conv_ln_b, conv_w_pw2=conv_w_pw2, conv_b_pw2=conv_b_pw2, attn_w_qkv=attn_w_qkv,
             attn_w_o=attn_w_o, rel_bias=rel_bias, ffn_w_gate=ffn_w_gate, ffn_w_up=ffn_w_up, ffn_w_down=ffn_w_down)
    m = dict(norm_mix=m_norm_mix, norm_ffn=m_norm_ffn, final_norm=m_final_norm, conv_w_pw1=m_conv_w_pw1,
             conv_b_pw1=m_conv_b_pw1, conv_w_dw=m_conv_w_dw, conv_b_dw=m_conv_b_dw, conv_ln_g=m_conv_ln_g,
             conv_ln_b=m_conv_ln_b, conv_w_pw2=m_conv_w_pw2, conv_b_pw2=m_conv_b_pw2, attn_w_qkv=m_attn_w_qkv,
             attn_w_o=m_attn_w_o, rel_bias=m_rel_bias, ffn_w_gate=m_ffn_w_gate, ffn_w_up=m_ffn_w_up,
             ffn_w_down=m_ffn_w_down)
    v = dict(norm_mix=v_norm_mix, norm_ffn=v_norm_ffn, final_norm=v_final_norm, conv_w_pw1=v_conv_w_pw1,
             conv_b_pw1=v_conv_b_pw1, conv_w_dw=v_conv_w_dw, conv_b_dw=v_conv_b_dw, conv_ln_g=v_conv_ln_g,
             conv_ln_b=v_conv_ln_b, conv_w_pw2=v_conv_w_pw2, conv_b_pw2=v_conv_b_pw2, attn_w_qkv=v_attn_w_qkv,
             attn_w_o=v_attn_w_o, rel_bias=v_rel_bias, ffn_w_gate=v_ffn_w_gate, ffn_w_up=v_ffn_w_up,
             ffn_w_down=v_ffn_w_down)

    me = 4 * lax.axis_index("x") + 2 * lax.axis_index("y") + lax.axis_index("c")
    depth = ffn_w_gate.shape[0]
    n_conv, _, cb = conv_w_dw.shape

    def sublayer(key, layer):
        if key in ("wgt", "wut", "wd"):
            return 2 * layer + 1
        return 4 * layer if key in ("w1t", "w2") else 4 * layer + 2

    def landing(block, own):
        land = lax.empty((N_DEV,) + block.shape, block.dtype)
        return lax.dynamic_update_slice(land, own[None], (me,) + (0,) * block.ndim)

    by_sub = {s: [] for s in range(2 * depth)}
    for name, key, cols in SHARDED:
        sw = (jnp.swapaxes(w[name], 1, 2) if cols else w[name]).astype(BF16)
        for layer in range(sw.shape[0]):
            by_sub[sublayer(key, layer)].append((key, layer, sw[layer]))
    likes = {s: jnp.zeros((sum(sh.size for _, _, sh in by_sub[s]) // 1024, 1024), BF16) for s in by_sub}
    dw_shard = jnp.pad(conv_w_dw.reshape(-1, cb), ((0, -(n_conv * CONV_WIDTH) % 8), (0, 0)))
    like_dw = jnp.zeros(dw_shard.shape, F32)
    stages = {}
    for s in range(2 * depth):
        for entry in by_sub[s]:
            stages.setdefault((s, 0 if entry[0] != "w2" else 1), []).append(entry)
    stage_order = sorted(stages)
    groups = [([sh for _, _, sh in stages[st]], [landing(sh, sh) for _, _, sh in stages[st]]) for st in stage_order]
    groups.insert(1, ([dw_shard], [landing(dw_shard, dw_shard)]))
    gather, _ = _exchange_start("gather_start", groups, scatter=False, carry=jnp.zeros((8, 128), F32))
    handle = dict(zip(stage_order, gather[:1] + gather[2:]))
    dw_filters = []

    def fetch(s, after, part=0):
        st = (s, part)
        like = jnp.zeros((sum(sh.size for _, _, sh in stages[st]) // 1024, 1024), BF16)
        pieces = [(handle[st], like)]
        if st == (0, 1):
            pieces.append((gather[1], like_dw))
        landed = _exchange_wait(f"gather_wait{s}_{part}", pieces, after)
        out = {key: g.reshape(g.shape[0] * g.shape[1], g.shape[2]) for (key, _, _), g in zip(stages[st], landed[0])}
        if st == (0, 1):
            dw_all = landed[1][0]
            full = jnp.transpose(dw_all[:, :n_conv * CONV_WIDTH].reshape(N_DEV, n_conv, CONV_WIDTH, cb), (1, 2, 0, 3))
            full = jnp.pad(full.reshape(n_conv, CONV_WIDTH, N_DEV * cb), ((0, 0), (0, CONV_HALO - CONV_WIDTH), (0, 0)))
            dw_filters.extend(full[layer] for layer in range(n_conv))
        if "w2" in out:
            out["wdw"] = dw_filters[s // 4]
        return out

    scatter, dw_grads, started = {}, {}, {}

    def emit(s, gd, carry):
        parts = [gd[key].reshape(N_DEV, -1, gd[key].shape[1]) for key, _, _ in by_sub[s]]
        if "wdw" in gd:
            dw_grads[s // 4] = gd["wdw"]
        if s == 0:
            parts.append(_dw_blocks(jnp.stack([dw_grads[layer] for layer in range(n_conv)])))
        lands = [landing(p[0], lax.dynamic_index_in_dim(p, me, 0, keepdims=False)) for p in parts]
        groups = [(parts[:len(by_sub[s])], lands[:len(by_sub[s])])]
        if s == 0:
            groups.append((parts[-1:], lands[-1:]))
        if s == 0:
            scatter[s], started[0] = _exchange_start(f"scatter_start{s}", groups, scatter=True,
                                                     carry=jnp.zeros((8, 128), F32))
            return carry
        scatter[s], carry = _exchange_start(f"scatter_start{s}", groups, scatter=True, carry=carry)
        return carry

    sm = {n: w[n] for n in SMALL}
    loss_cols, dx, gsm = _local_step(x[0], loss_target[0], sm, depth, fetch, emit)
    loss = lax.psum(jnp.sum(loss_cols), ("x", "y", "c"))

    grads, summed, delta, new_m, new_v = {}, {}, {}, {}, {}

    def reduce_pieces(subs, landed):
        for s, recv in zip(subs, landed):
            for (key, layer, _), r in zip(by_sub[s], recv):
                summed[key, layer] = _sum8(f"sum_{key}{layer}", r)

    def update(names):
        for name in names:
            shape = w[name].shape
            res = _adamw(f"adamw_{name}",
                         *[t.reshape(-1, shape[-1]) for t in (grads[name], w[name], m[name], v[name])])
            delta[name], new_m[name], new_v[name] = (t.reshape(shape) for t in res)

    def stacked(name, key, cols):
        g = jnp.stack([summed[key, layer] for layer in range(w[name].shape[0])])
        return jnp.swapaxes(g, 1, 2) if cols else g

    early = sorted((s for s in scatter if s != 0), reverse=True)
    reduce_pieces(early, _exchange_wait("scatter_wait_early", [(scatter[s][0], likes[s]) for s in early],
                                        started[0]))
    late_names = [name for name, key, _ in SHARDED if any(k == key for k, _, _ in by_sub[0])]
    for name, key, cols in SHARDED:
        if name not in late_names:
            grads[name] = stacked(name, key, cols)
    early_names = [name for name, _, _ in SHARDED if name not in late_names]
    update(early_names)
    pack = _pack_small(gsm)
    ((pack_all,),) = _exchange("gather_small_grads", [([pack], pack)], scatter=False)
    pack_sum = _sum8("sum_small", pack_all)
    grads.update(_unpack_small(pack_sum, sm))

    landed = _exchange_wait("scatter_wait_last", [(scatter[0][0], likes[0]), (scatter[0][1], like_dw)],
                            new_v[early_names[-1]])
    reduce_pieces([0], landed[:1])
    for name, key, cols in SHARDED:
        if name in late_names:
            grads[name] = stacked(name, key, cols)
    grads["conv_w_dw"] = _sum8("sum_wdw", landed[1][0])[:n_conv * CONV_WIDTH].reshape(conv_w_dw.shape)
    update(late_names + ["conv_w_dw"])
    res = _adamw("adamw_small", pack_sum, _pack_small(sm), _pack_small({n: m[n] for n in SMALL}),
                 _pack_small({n: v[n] for n in SMALL}))
    for dst, t in zip((delta, new_m, new_v), res):
        dst.update(_unpack_small(t, sm))

    outs = [loss, dx[None]]
    for d in (grads, delta, new_m, new_v):
        outs += [d[n] for n in ORDER]
    return tuple(outs)
```

```python
import functools
import math

import numpy as np
import jax
import jax.numpy as jnp
from jax import lax
from jax.experimental import pallas as pl
from jax.experimental.pallas import tpu as pltpu

F32 = jnp.float32
BF16 = jnp.bfloat16

N_DEV = 8
HEAD_DIM = 64
HEADS_PER_GROUP = 4
GROUP_COLS = HEADS_PER_GROUP * HEAD_DIM
DILATIONS = (1, 4, 16)
N_BACK = 128
N_GROUPS = 3
N_HEADS = 12
D_ATTN = 768
N_BUCKETS = 32
REL_MAX_DISTANCE = 2048
CONV_WIDTH = 31
CONV_HALO = 32
EPS = 1e-6
NEG_INF = -1e30
ADAM_LR, ADAM_B1, ADAM_B2, ADAM_EPS, ADAM_WD, ADAM_STEP = 0.001, 0.9, 0.999, 1e-08, 0.01, 10
V7X_VMEM_LIMIT_BYTES = 56 * 1024 * 1024
MESH = pl.DeviceIdType.MESH
ANY = pl.BlockSpec(memory_space=pl.ANY)


def _pick(n, prefs):
    for p in prefs:
        if n % p == 0:
            return p
    return n


def _params(sem, vmem=None):
    return pltpu.CompilerParams(dimension_semantics=sem, vmem_limit_bytes=vmem)


def _sigmoid(x):
    return 1.0 / (1.0 + jnp.exp(-x))


def _exchange(name, groups, scatter):
    n_arr = [len(arrs) for arrs, _ in groups]
    n_in = sum(n_arr) + len(groups)
    ng = len(groups)

    def body(*refs):
        ins, outs, (send_sems, recv_sems, local_sems) = refs[:n_in], refs[n_in:-3], refs[-3:]
        x, y, c = lax.axis_index("x"), lax.axis_index("y"), lax.axis_index("c")
        me = 4 * x + 2 * y + c
        pos_in = pos_out = 0
        plans = []
        for gi in range(ng):
            srcs = ins[pos_in:pos_in + n_arr[gi]]
            like = ins[pos_in + n_arr[gi]]
            dsts = outs[pos_out:pos_out + n_arr[gi]]
            pos_in += n_arr[gi] + 1
            pos_out += n_arr[gi]
            plans.append((gi, srcs, like, dsts))
        local = []
        for gi, srcs, like, dsts in plans:
            for s, d in zip(srcs, dsts):
                cp = pltpu.make_async_copy(s.at[me] if scatter else s, d.at[me], local_sems.at[gi])
                cp.start()
                local.append(cp)
        for delta in range(1, N_DEV):
            dx, dy, dc = (delta >> 2) & 1, (delta >> 1) & 1, delta & 1
            px, py, pc = (1 - x if dx else x), (1 - y if dy else y), (1 - c if dc else c)
            peer = 4 * px + 2 * py + pc
            for gi, srcs, like, dsts in plans:
                for s, d in zip(srcs, dsts):
                    pltpu.make_async_remote_copy(
                        src_ref=s.at[peer] if scatter else s, dst_ref=d.at[me],
                        send_sem=send_sems.at[gi, delta - 1], recv_sem=recv_sems.at[gi, delta - 1],
                        device_id=(px, py, pc), device_id_type=MESH).start()
        for delta in range(1, N_DEV):
            for gi, srcs, like, dsts in plans:
                pltpu.make_async_remote_copy(
                    src_ref=like, dst_ref=like, send_sem=send_sems.at[gi, delta - 1],
                    recv_sem=recv_sems.at[gi, delta - 1], device_id=(x, y, c), device_id_type=MESH).wait()
        for cp in local:
            cp.wait()

    operands, out_shape = [], []
    for arrs, like in groups:
        operands += list(arrs) + [like]
        for a in arrs:
            blk = a.shape[1:] if scatter else a.shape
            out_shape.append(jax.ShapeDtypeStruct((N_DEV,) + tuple(blk), a.dtype))
    outs = pl.pallas_call(
        body, name=name, out_shape=out_shape, in_specs=[ANY] * len(operands), out_specs=[ANY] * len(out_shape),
        scratch_shapes=[pltpu.SemaphoreType.DMA((ng, N_DEV - 1)), pltpu.SemaphoreType.DMA((ng, N_DEV - 1)),
                        pltpu.SemaphoreType.DMA((ng,))],
        compiler_params=pltpu.CompilerParams(has_side_effects=True),
    )(*operands)
    res, pos = [], 0
    for n in n_arr:
        res.append(list(outs[pos:pos + n]))
        pos += n
    return res


HBM = pl.BlockSpec(memory_space=pltpu.HBM)
SEM = pl.BlockSpec(memory_space=pltpu.SEMAPHORE)
EFFECT = pltpu.SideEffectType.DATAFLOW_SIDE_EFFECTING


def _in_hbm(a):
    return pltpu.with_memory_space_constraint(a, pltpu.HBM)


def _exchange_start(name, groups, scatter, carry):
    ns = [len(s) for s, _ in groups]
    n_in = 2 * sum(ns)

    def body(*refs):
        ins, outs = refs[:n_in], refs[n_in + 1:]
        x, y, c = lax.axis_index("x"), lax.axis_index("y"), lax.axis_index("c")
        me = 4 * x + 2 * y + c
        pi = po = 0
        for n in ns:
            srcs, lands = ins[pi:pi + n], ins[pi + n:pi + 2 * n]
            send_sems, recv_sems = outs[po], outs[po + 1]
            pi += 2 * n
            po += 2 + 2 * n
            for delta in range(1, N_DEV):
                dx, dy, dc = (delta >> 2) & 1, (delta >> 1) & 1, delta & 1
                px, py, pc = (1 - x if dx else x), (1 - y if dy else y), (1 - c if dc else c)
                peer = 4 * px + 2 * py + pc
                for s, d in zip(srcs, lands):
                    pltpu.make_async_remote_copy(
                        src_ref=s.at[peer] if scatter else s, dst_ref=d.at[me], send_sem=send_sems.at[delta - 1],
                        recv_sem=recv_sems.at[delta - 1], device_id=(px, py, pc), device_id_type=MESH).start()

    operands, out_shape, out_specs, aliases = [], [], [], {}
    for srcs, lands in groups:
        out_shape += [pltpu.SemaphoreType.DMA((N_DEV - 1,))] * 2
        out_specs += [SEM, SEM]
        for a in list(srcs) + list(lands):
            aliases[len(operands)] = len(out_shape)
            operands.append(_in_hbm(a))
            out_shape.append(pltpu.HBM(a.shape, a.dtype))
            out_specs.append(HBM)
    aliases[len(operands)] = len(out_shape)
    operands.append(_in_hbm(carry))
    out_shape.append(pltpu.HBM(carry.shape, carry.dtype))
    out_specs.append(HBM)
    outs = pl.pallas_call(
        body, name=name, out_shape=out_shape, in_specs=[HBM] * len(operands), out_specs=out_specs,
        input_output_aliases=aliases, compiler_params=pltpu.CompilerParams(has_side_effects=EFFECT),
    )(*operands)
    handles, po = [], 0
    for n in ns:
        handles.append((outs[po], outs[po + 1], list(outs[po + 2:po + 2 + n]), list(outs[po + 2 + n:po + 2 + 2 * n])))
        po += 2 + 2 * n
    return handles, outs[-1]


def _exchange_wait(name, pieces, after):
    ns = [len(h[2]) for h, _ in pieces]

    def body(*refs):
        x, y, c = lax.axis_index("x"), lax.axis_index("y"), lax.axis_index("c")
        pi = 0
        for n in ns:
            send_sems, recv_sems, like = refs[pi + 2 * n], refs[pi + 2 * n + 1], refs[pi + 2 * n + 2]
            pi += 2 * n + 3
            for delta in range(1, N_DEV):
                cp = pltpu.make_async_remote_copy(
                    src_ref=like, dst_ref=like, send_sem=send_sems.at[delta - 1], recv_sem=recv_sems.at[delta - 1],
                    device_id=(x, y, c), device_id_type=MESH)
                cp.wait_send()
                cp.wait_recv()

    operands, in_specs, out_shape, aliases = [], [], [], {}
    for (send_sems, recv_sems, srcs, lands), like in pieces:
        for a in srcs + lands:
            aliases[len(operands)] = len(out_shape)
            operands.append(a)
            in_specs.append(HBM)
            out_shape.append(pltpu.HBM(a.shape, a.dtype))
        operands += [send_sems, recv_sems, like]
        in_specs += [SEM, SEM, ANY]
    operands.append(after)
    in_specs.append(ANY)
    outs = pl.pallas_call(
        body, name=name, out_shape=out_shape, in_specs=in_specs, out_specs=[HBM] * len(out_shape),
        input_output_aliases=aliases, compiler_params=pltpu.CompilerParams(has_side_effects=EFFECT),
    )(*operands)
    res, po = [], 0
    for n in ns:
        res.append(list(outs[po + n:po + 2 * n]))
        po += 2 * n
    return res


def _sum8(name, r):
    _, rows, cols = r.shape
    tr = _pick(rows, (256, 128, 64, 32, 16, 8))

    def body(r_ref, o_ref):
        acc = r_ref[0].astype(F32)
        for p in range(1, N_DEV):
            acc = acc + r_ref[p].astype(F32)
        o_ref[...] = acc

    return pl.pallas_call(
        body, name=name, out_shape=jax.ShapeDtypeStruct((rows, cols), F32), grid=(rows // tr,),
        in_specs=[pl.BlockSpec((N_DEV, tr, cols), lambda i: (0, i, 0))],
        out_specs=pl.BlockSpec((tr, cols), lambda i: (i, 0)), compiler_params=_params(("parallel",)),
    )(r)


def _adamw(name, g, w, m, v):
    rows, cols = g.shape
    tr = _pick(rows, (512, 256, 128, 64, 32, 16, 8))
    c1 = 1.0 - ADAM_B1 ** ADAM_STEP
    c2 = 1.0 - ADAM_B2 ** ADAM_STEP

    def body(g_ref, w_ref, m_ref, v_ref, d_ref, nm_ref, nv_ref):
        gv = g_ref[...]
        nm = ADAM_B1 * m_ref[...] + (1.0 - ADAM_B1) * gv
        nv = ADAM_B2 * v_ref[...] + (1.0 - ADAM_B2) * (gv * gv)
        d_ref[...] = -ADAM_LR * ((nm / c1) / (jnp.sqrt(nv / c2) + ADAM_EPS) + ADAM_WD * w_ref[...])
        nm_ref[...] = nm
        nv_ref[...] = nv

    spec = pl.BlockSpec((tr, cols), lambda i: (i, 0))
    return pl.pallas_call(
        body, name=name, out_shape=[jax.ShapeDtypeStruct((rows, cols), F32)] * 3, grid=(rows // tr,),
        in_specs=[spec] * 4, out_specs=[spec] * 3, compiler_params=_params(("parallel",)),
    )(g, w, m, v)


def _mm_nt(name, a, ws, w_offs, n, epi, out_dtypes, extras=(), rows=(), tm=None, tn=None):
    m, k = a.shape
    tm = tm or _pick(m, (512, 256, 128))
    tn = tn or _pick(n, (1408, 1152, 1024, 512, 256, 128))
    nw, ne, nr = len(ws), len(extras), len(rows)

    def body(*refs):
        a_ref, w_refs = refs[0], refs[1:1 + nw]
        e_refs, r_refs = refs[1 + nw:1 + nw + ne], refs[1 + nw + ne:1 + nw + ne + nr]
        o_refs = refs[1 + nw + ne + nr:]
        av = a_ref[...].astype(BF16)
        accs = [lax.dot_general(av, w[...], (((1,), (1,)), ((), ())), preferred_element_type=F32) for w in w_refs]
        outs = epi(accs, [e[...] for e in e_refs], [r[...] for r in r_refs])
        for o_ref, o in zip(o_refs, outs):
            o_ref[...] = o.astype(o_ref.dtype)

    in_specs = [pl.BlockSpec((tm, k), lambda j, i: (i, 0))]
    in_specs += [pl.BlockSpec((tn, k), functools.partial(lambda j, i, off: (j + off, 0), off=off)) for off in w_offs]
    in_specs += [pl.BlockSpec((tm, tn), lambda j, i: (i, j))] * ne
    in_specs += [pl.BlockSpec((1, tn), lambda j, i: (0, j))] * nr
    return pl.pallas_call(
        body, name=name, out_shape=[jax.ShapeDtypeStruct((m, n), dt) for dt in out_dtypes],
        grid=(n // tn, m // tm), in_specs=in_specs,
        out_specs=[pl.BlockSpec((tm, tn), lambda j, i: (i, j))] * len(out_dtypes),
        compiler_params=_params(("parallel", "parallel"), V7X_VMEM_LIMIT_BYTES),
    )(a, *ws, *extras, *rows)


def _rms_rows(x, gain):
    r = lax.rsqrt(jnp.mean(x * x, axis=-1, keepdims=True) + EPS)
    return (x * r * gain).astype(BF16)


def _project_residual(name, a, b, x, bias=None, gain=None):
    m, k = a.shape
    n = b.shape[1]
    tm = _pick(m, (512, 256, 128))
    rows = [r for r in (bias, gain) if r is not None]

    def body(*refs):
        a_ref, b_ref, x_ref = refs[:3]
        r_refs = list(refs[3:3 + len(rows)])
        out = x_ref[...] + jnp.dot(a_ref[...], b_ref[...], preferred_element_type=F32)
        if bias is not None:
            out = out + r_refs.pop(0)[...]
        refs[3 + len(rows)][...] = out
        if gain is not None:
            refs[4 + len(rows)][...] = _rms_rows(out, r_refs.pop(0)[...])

    big = pl.BlockSpec((tm, n), lambda i: (i, 0))
    out_shape = [jax.ShapeDtypeStruct((m, n), F32)] + ([jax.ShapeDtypeStruct((m, n), BF16)] if gain is not None else [])
    res = pl.pallas_call(
        body, name=name, out_shape=out_shape, grid=(m // tm,),
        in_specs=[pl.BlockSpec((tm, k), lambda i: (i, 0)), pl.BlockSpec((k, n), lambda i: (0, 0)), big]
        + [pl.BlockSpec((1, n), lambda i: (0, 0))] * len(rows),
        out_specs=[big] * len(out_shape), compiler_params=_params(("parallel",), V7X_VMEM_LIMIT_BYTES),
    )(a, b, x, *rows)
    return res if gain is not None else (res[0], None)


def _mm_nn_rms_bwd(name, as_, bs, x, g, dx_out, tm=None):
    m, k = as_[0].shape
    n = bs[0].shape[1]
    tm = tm or _pick(m, (512, 256, 128))
    npair = len(as_)

    def body(*refs):
        a_refs, b_refs = refs[:npair], refs[npair:2 * npair]
        x_ref, g_ref, dxo_ref, dx_ref, dg_ref = refs[2 * npair:]
        dh = None
        for a_ref, b_ref in zip(a_refs, b_refs):
            p = jnp.dot(a_ref[...].astype(BF16), b_ref[...], preferred_element_type=F32)
            dh = p if dh is None else dh + p
        xv = x_ref[...]
        r = lax.rsqrt(jnp.mean(xv * xv, axis=-1, keepdims=True) + EPS)
        yv = xv * r
        dy = dh * g_ref[...]
        dx_ref[...] = dxo_ref[...] + r * (dy - yv * jnp.mean(dy * yv, axis=-1, keepdims=True))

        @pl.when(pl.program_id(0) == 0)
        def _():
            dg_ref[...] = jnp.zeros_like(dg_ref)

        dg_ref[...] += jnp.sum(dh * yv, axis=0, keepdims=True)

    big = pl.BlockSpec((tm, n), lambda i: (i, 0))
    row = pl.BlockSpec((1, n), lambda i: (0, 0))
    in_specs = [pl.BlockSpec((tm, k), lambda i: (i, 0))] * npair + [pl.BlockSpec((k, n), lambda i: (0, 0))] * npair
    return pl.pallas_call(
        body, name=name, out_shape=[jax.ShapeDtypeStruct((m, n), F32), jax.ShapeDtypeStruct((1, n), F32)],
        grid=(m // tm,), in_specs=in_specs + [big, row, big], out_specs=[big, row],
        compiler_params=_params(("arbitrary",), V7X_VMEM_LIMIT_BYTES),
    )(*as_, *bs, x, g, dx_out)


def _mm_tn(name, a, b, colsum_b=False, tm=None, tk=2048):
    t, ma = a.shape
    nb = b.shape[1]
    tm = tm or _pick(ma, (1408, 1152, 1024, 768, 512, 256, 128))
    tk = _pick(t, (tk, 256, 128))
    nk = t // tk

    def body(*refs):
        a_ref, b_ref, o_ref = refs[0], refs[1], refs[2]
        acc_ref = refs[-1]
        kk = pl.program_id(1)
        bv = b_ref[...]

        @pl.when(kk == 0)
        def _():
            acc_ref[...] = jnp.zeros_like(acc_ref)

        acc_ref[...] += lax.dot_general(a_ref[...].astype(BF16), bv.astype(BF16), (((0,), (0,)), ((), ())),
                                        preferred_element_type=F32)
        if colsum_b:
            s_ref = refs[3]

            @pl.when((kk == 0) & (pl.program_id(0) == 0))
            def _():
                s_ref[...] = jnp.zeros_like(s_ref)

            @pl.when(pl.program_id(0) == 0)
            def _():
                s_ref[...] += jnp.sum(bv.astype(F32), axis=0, keepdims=True)

        @pl.when(kk == nk - 1)
        def _():
            o_ref[...] = acc_ref[...].astype(o_ref.dtype)

    out_shape = [jax.ShapeDtypeStruct((ma, nb), BF16)]
    out_specs = [pl.BlockSpec((tm, nb), lambda i, kk: (i, 0))]
    if colsum_b:
        out_shape.append(jax.ShapeDtypeStruct((1, nb), F32))
        out_specs.append(pl.BlockSpec((1, nb), lambda i, kk: (0, 0)))
    res = pl.pallas_call(
        body, name=name, out_shape=out_shape, grid=(ma // tm, nk),
        in_specs=[pl.BlockSpec((tk, tm), lambda i, kk: (kk, i)), pl.BlockSpec((tk, nb), lambda i, kk: (kk, 0))],
        out_specs=out_specs, scratch_shapes=[pltpu.VMEM((tm, nb), F32)],
        compiler_params=_params(("arbitrary", "arbitrary"), V7X_VMEM_LIMIT_BYTES),
    )(a, b)
    return res if colsum_b else res[0]


def _mm_tn_pair(name, a1, a2, b, tk=1024):
    t, ma = a1.shape
    nb = b.shape[1]
    tm = _pick(ma, (1408, 1152, 1024, 768, 512, 256, 128))
    tk = _pick(t, (tk, 256, 128))
    nk = t // tk
    dims = (((0,), (0,)), ((), ()))

    def body(a1_ref, a2_ref, b_ref, o1_ref, o2_ref, acc1_ref, acc2_ref):
        kk = pl.program_id(1)
        bv = b_ref[...]

        @pl.when(kk == 0)
        def _():
            acc1_ref[...] = jnp.zeros_like(acc1_ref)
            acc2_ref[...] = jnp.zeros_like(acc2_ref)

        acc1_ref[...] += lax.dot_general(a1_ref[...], bv, dims, preferred_element_type=F32)
        acc2_ref[...] += lax.dot_general(a2_ref[...], bv, dims, preferred_element_type=F32)

        @pl.when(kk == nk - 1)
        def _():
            o1_ref[...] = acc1_ref[...].astype(BF16)
            o2_ref[...] = acc2_ref[...].astype(BF16)

    sa = pl.BlockSpec((tk, tm), lambda i, kk: (kk, i))
    so = pl.BlockSpec((tm, nb), lambda i, kk: (i, 0))
    return pl.pallas_call(
        body, name=name, out_shape=[jax.ShapeDtypeStruct((ma, nb), BF16)] * 2, grid=(ma // tm, nk),
        in_specs=[sa, sa, pl.BlockSpec((tk, nb), lambda i, kk: (kk, 0))], out_specs=[so, so],
        scratch_shapes=[pltpu.VMEM((tm, nb), F32)] * 2,
        compiler_params=_params(("arbitrary", "arbitrary"), V7X_VMEM_LIMIT_BYTES),
    )(a1, a2, b)


FFN_CHUNK = 256


def _ffn_down_bwd(name, dx, wd, gq, uq):
    t, d_model = dx.shape
    f = wd.shape[0]
    tm = _pick(t, (256, 128))
    ck = _pick(f, (FFN_CHUNK, 128))
    nt = (((1,), (1,)), ((), ()))
    tn = (((0,), (0,)), ((), ()))
    last = t // tm - 1

    def body(dx_ref, w_ref, g_ref, u_ref, dg_ref, du_ref, dw_ref, acc_ref):
        i = pl.program_id(0)

        @pl.when(i == 0)
        def _():
            acc_ref[...] = jnp.zeros_like(acc_ref)

        dxb = dx_ref[...].astype(BF16)
        for c0 in range(0, f, ck):
            cs = slice(c0, c0 + ck)
            dact = lax.dot_general(dxb, w_ref[cs, :], nt, preferred_element_type=F32)
            gf, uf = g_ref[:, cs].astype(F32), u_ref[:, cs].astype(F32)
            sg = _sigmoid(gf)
            dg_ref[:, cs] = (dact * uf * (sg * (1.0 + gf * (1.0 - sg)))).astype(BF16)
            du_ref[:, cs] = (dact * gf * sg).astype(BF16)
            act = (gf * sg * uf).astype(BF16)
            acc_ref[cs, :] += lax.dot_general(act, dxb, tn, preferred_element_type=F32)

        @pl.when(i == last)
        def _():
            dw_ref[...] = acc_ref[...].astype(BF16)

    wide = pl.BlockSpec((tm, f), lambda i: (i, 0))
    whole = pl.BlockSpec((f, d_model), lambda i: (0, 0))
    return pl.pallas_call(
        body, name=name,
        out_shape=[jax.ShapeDtypeStruct((t, f), BF16), jax.ShapeDtypeStruct((t, f), BF16),
                   jax.ShapeDtypeStruct((f, d_model), BF16)],
        grid=(t // tm,), in_specs=[pl.BlockSpec((tm, d_model), lambda i: (i, 0)), whole, wide, wide],
        out_specs=[wide, wide, whole], scratch_shapes=[pltpu.VMEM((f, d_model), F32)],
        compiler_params=_params(("arbitrary",), V7X_VMEM_LIMIT_BYTES),
    )(dx, wd, gq, uq)


def _rmsnorm_fwd(name, x, g):
    t, d = x.shape
    tr = _pick(t, (512, 256, 128))

    def body(x_ref, g_ref, h_ref):
        xv = x_ref[...]
        r = lax.rsqrt(jnp.mean(xv * xv, axis=-1, keepdims=True) + EPS)
        h_ref[...] = (xv * r * g_ref[...]).astype(BF16)

    return pl.pallas_call(
        body, name=name, out_shape=jax.ShapeDtypeStruct((t, d), BF16), grid=(t // tr,),
        in_specs=[pl.BlockSpec((tr, d), lambda i: (i, 0)), pl.BlockSpec((1, d), lambda i: (0, 0))],
        out_specs=pl.BlockSpec((tr, d), lambda i: (i, 0)), compiler_params=_params(("parallel",)),
    )(x, g)


def _loss_head(name, x, g, target):
    t, d = x.shape
    tr = _pick(t, (512, 256, 128))

    def body(x_ref, g_ref, t_ref, dx_ref, dg_ref, l_ref):
        xv = x_ref[...]
        r = lax.rsqrt(jnp.mean(xv * xv, axis=-1, keepdims=True) + EPS)
        yv = xv * r
        diff = yv * g_ref[...] - t_ref[...]
        dout = diff * (1.0 / d)
        dy = dout * g_ref[...]
        dx_ref[...] = r * (dy - yv * jnp.mean(dy * yv, axis=-1, keepdims=True))

        @pl.when(pl.program_id(0) == 0)
        def _():
            dg_ref[...] = jnp.zeros_like(dg_ref)
            l_ref[...] = jnp.zeros_like(l_ref)

        dg_ref[...] += jnp.sum(dout * yv, axis=0, keepdims=True)
        l_ref[...] += (0.5 / d) * jnp.sum(diff * diff, axis=0, keepdims=True)

    big = pl.BlockSpec((tr, d), lambda i: (i, 0))
    row = pl.BlockSpec((1, d), lambda i: (0, 0))
    return pl.pallas_call(
        body, name=name,
        out_shape=[jax.ShapeDtypeStruct((t, d), F32), jax.ShapeDtypeStruct((1, d), F32),
                   jax.ShapeDtypeStruct((1, d), F32)],
        grid=(t // tr,), in_specs=[big, row, big], out_specs=[big, row, row],
        compiler_params=_params(("arbitrary",)),
    )(x, g, target)


CONV_ROWS = 64
SUBLANES = 8


def _fill_window(win_ref, sh_ref, parts):
    rows = sh_ref.shape[2]
    for cb in range(win_ref.shape[0]):
        for r0, val in parts:
            win_ref[cb, r0:r0 + val.shape[0], :] = val[:, 128 * cb:128 * (cb + 1)]
        win_ref[cb, rows:rows + SUBLANES, :] = jnp.zeros((SUBLANES, 128), F32)
        for b in range(1, SUBLANES):
            sh_ref[b - 1, cb] = win_ref[cb, b:b + rows, :]


def _window_rows(win_ref, sh_ref, o, cb):
    b = o % SUBLANES
    if b == 0:
        return win_ref[cb, o:o + CONV_ROWS, :]
    return sh_ref[b - 1, cb, o - b:o - b + CONV_ROWS, :]


def _window_scratch(rows, c):
    return [pltpu.VMEM((c // 128, rows + SUBLANES, 128), F32), pltpu.VMEM((SUBLANES - 1, c // 128, rows, 128), F32)]


def _dwconv_fwd(name, glu, w_dw, b_dw, ln_g, ln_b):
    t, c = glu.shape
    tt = _pick(t, (256, 128))
    hb = tt // CONV_HALO

    def body(cur_ref, halo_ref, w_ref, b_ref, g_ref, be_ref, dw_ref, s_ref, win_ref, sh_ref):
        i = pl.program_id(0)
        halo = jnp.where(i > 0, halo_ref[...].astype(F32), 0.0)
        _fill_window(win_ref, sh_ref, [(0, halo), (CONV_HALO, cur_ref[...].astype(F32))])
        for r0 in range(0, tt, CONV_ROWS):
            for cb in range(c // 128):
                c0 = 128 * cb
                acc = jnp.zeros((CONV_ROWS, 128), F32) + b_ref[:, c0:c0 + 128]
                for k in range(CONV_WIDTH):
                    o = r0 + k + CONV_HALO - (CONV_WIDTH - 1)
                    acc = acc + w_ref[k:k + 1, c0:c0 + 128] * _window_rows(win_ref, sh_ref, o, cb)
                dw_ref[r0:r0 + CONV_ROWS, c0:c0 + 128] = acc
        u = dw_ref[...]
        mu = jnp.mean(u, axis=-1, keepdims=True)
        uc = u - mu
        rstd = lax.rsqrt(jnp.mean(uc * uc, axis=-1, keepdims=True) + EPS)
        z = uc * rstd * g_ref[...] + be_ref[...]
        s_ref[...] = (z * _sigmoid(z)).astype(BF16)

    big = pl.BlockSpec((tt, c), lambda i: (i, 0))
    row = pl.BlockSpec((1, c), lambda i: (0, 0))
    return pl.pallas_call(
        body, name=name, out_shape=[jax.ShapeDtypeStruct((t, c), F32), jax.ShapeDtypeStruct((t, c), BF16)],
        grid=(t // tt,),
        in_specs=[big, pl.BlockSpec((CONV_HALO, c), lambda i: (jnp.maximum(i * hb - 1, 0), 0)),
                  pl.BlockSpec((CONV_HALO, c), lambda i: (0, 0)), row, row, row],
        out_specs=[big, big], scratch_shapes=_window_scratch(tt + CONV_HALO, c),
        compiler_params=_params(("parallel",), V7X_VMEM_LIMIT_BYTES),
    )(glu, glu, w_dw, b_dw, ln_g, ln_b)


def _ln_silu_bwd(name, dx, w2, dw, ln_g, ln_b):
    t, c = dw.shape
    d_model = dx.shape[1]
    tr = _pick(t, (256, 128))

    def body(dx_ref, w_ref, dw_ref, g_ref, be_ref, o_ref, acc_ref):
        ds = lax.dot_general(dx_ref[...].astype(BF16), w_ref[...], (((1,), (1,)), ((), ())),
                             preferred_element_type=F32)
        u = dw_ref[...]
        mu = jnp.mean(u, axis=-1, keepdims=True)
        uc = u - mu
        rstd = lax.rsqrt(jnp.mean(uc * uc, axis=-1, keepdims=True) + EPS)
        xh = uc * rstd
        z = xh * g_ref[...] + be_ref[...]
        sg = _sigmoid(z)
        dz = ds * (sg * (1.0 + z * (1.0 - sg)))
        dxh = dz * g_ref[...]
        du = rstd * (dxh - jnp.mean(dxh, axis=-1, keepdims=True) - xh * jnp.mean(dxh * xh, axis=-1, keepdims=True))
        o_ref[...] = du

        @pl.when(pl.program_id(0) == 0)
        def _():
            acc_ref[...] = jnp.zeros_like(acc_ref)

        acc_ref[0:1, :] += jnp.sum(dz * xh, axis=0, keepdims=True)
        acc_ref[1:2, :] += jnp.sum(dz, axis=0, keepdims=True)
        acc_ref[2:3, :] += jnp.sum(du, axis=0, keepdims=True)

    big = pl.BlockSpec((tr, c), lambda i: (i, 0))
    row = pl.BlockSpec((1, c), lambda i: (0, 0))
    return pl.pallas_call(
        body, name=name, out_shape=[jax.ShapeDtypeStruct((t, c), F32), jax.ShapeDtypeStruct((8, c), F32)],
        grid=(t // tr,),
        in_specs=[pl.BlockSpec((tr, d_model), lambda i: (i, 0)), pl.BlockSpec((c, d_model), lambda i: (0, 0)),
                  big, row, row],
        out_specs=[big, pl.BlockSpec((8, c), lambda i: (0, 0))],
        compiler_params=_params(("arbitrary",), V7X_VMEM_LIMIT_BYTES),
    )(dx, w2, dw, ln_g, ln_b)


def _dwconv_bwd(name, ddw, a, gt, w_dw):
    t, c = ddw.shape
    tt = _pick(t, (256, 128))
    hb = tt // CONV_HALO
    last = t // tt - 1
    back = CONV_WIDTH - 1

    def body(d_ref, dn_ref, a_ref, ap_ref, g_ref, gp_ref, w_ref, du_ref, dwk_ref, db_ref,
             wd_ref, shd_ref, wg_ref, shg_ref, dg_ref, dwk8_ref):
        i = pl.program_id(0)
        _fill_window(wd_ref, shd_ref, [(0, d_ref[...]), (tt, jnp.where(i < last, dn_ref[...], 0.0))])
        glu_prev = ap_ref[...].astype(F32) * _sigmoid(gp_ref[...].astype(F32))
        av = a_ref[...].astype(F32)
        sg = _sigmoid(g_ref[...].astype(F32))
        _fill_window(wg_ref, shg_ref, [(0, jnp.where(i > 0, glu_prev, 0.0)), (CONV_HALO, av * sg)])

        @pl.when(i == 0)
        def _():
            dwk8_ref[...] = jnp.zeros_like(dwk8_ref)
            db_ref[...] = jnp.zeros_like(db_ref)

        for r0 in range(0, tt, CONV_ROWS):
            for cb in range(c // 128):
                c0 = 128 * cb
                dcur = wd_ref[cb, r0:r0 + CONV_ROWS, :]
                acc = jnp.zeros((CONV_ROWS, 128), F32)
                for k in range(CONV_WIDTH):
                    acc = acc + w_ref[k:k + 1, c0:c0 + 128] * _window_rows(wd_ref, shd_ref, r0 + back - k, cb)
                    p = dcur * _window_rows(wg_ref, shg_ref, r0 + k + CONV_HALO - back, cb)
                    s8 = p[0:SUBLANES]
                    for q in range(SUBLANES, CONV_ROWS, SUBLANES):
                        s8 = s8 + p[q:q + SUBLANES]
                    dwk8_ref[SUBLANES * k:SUBLANES * (k + 1), c0:c0 + 128] += s8
                dg_ref[r0:r0 + CONV_ROWS, c0:c0 + 128] = acc

        @pl.when(i == last)
        def _():
            for k in range(CONV_WIDTH):
                dwk_ref[k:k + 1, :] = jnp.sum(dwk8_ref[SUBLANES * k:SUBLANES * (k + 1), :], axis=0, keepdims=True)
            dwk_ref[CONV_WIDTH:, :] = jnp.zeros((CONV_HALO - CONV_WIDTH, c), F32)
        dglu = dg_ref[...]
        da = dglu * sg
        dgate = dglu * av * sg * (1.0 - sg)
        du_ref[:, 0:c] = da.astype(BF16)
        du_ref[:, c:] = dgate.astype(BF16)
        db_ref[:, 0:c] += jnp.sum(da, axis=0, keepdims=True)
        db_ref[:, c:] += jnp.sum(dgate, axis=0, keepdims=True)

    big = pl.BlockSpec((tt, c), lambda i: (i, 0))
    prev = pl.BlockSpec((CONV_HALO, c), lambda i: (jnp.maximum(i * hb - 1, 0), 0))
    nxt = pl.BlockSpec((CONV_HALO, c), lambda i: (jnp.minimum((i + 1) * hb, t // CONV_HALO - 1), 0))
    return pl.pallas_call(
        body, name=name,
        out_shape=[jax.ShapeDtypeStruct((t, 2 * c), BF16), jax.ShapeDtypeStruct((CONV_HALO, c), F32),
                   jax.ShapeDtypeStruct((1, 2 * c), F32)],
        grid=(t // tt,),
        in_specs=[big, nxt, big, prev, big, prev, pl.BlockSpec((CONV_HALO, c), lambda i: (0, 0))],
        out_specs=[pl.BlockSpec((tt, 2 * c), lambda i: (i, 0)), pl.BlockSpec((CONV_HALO, c), lambda i: (0, 0)),
                   pl.BlockSpec((1, 2 * c), lambda i: (0, 0))],
        scratch_shapes=_window_scratch(tt + CONV_HALO, c) + _window_scratch(tt + CONV_HALO, c)
        + [pltpu.VMEM((tt, c), F32), pltpu.VMEM((SUBLANES * CONV_HALO, c), F32)],
        compiler_params=_params(("arbitrary",), V7X_VMEM_LIMIT_BYTES),
    )(ddw, ddw, a, a, gt, gt, w_dw)


def _bucket_tables():
    i = np.arange(N_BACK)[:, None]
    j = np.arange(2 * N_BACK)[None, :]
    dist = i + N_BACK - j
    valid = (dist >= 0) & (dist <= N_BACK)
    max_exact = N_BUCKETS // 2
    out = []
    for d in DILATIONS:
        n = np.maximum(dist * d, 0)
        nf = np.maximum(n, 1).astype(np.float32)
        large = max_exact + (np.log(nf / np.float32(max_exact)) / np.float32(math.log(REL_MAX_DISTANCE / max_exact))
                             * np.float32(N_BUCKETS - max_exact)).astype(np.int32)
        large = np.minimum(large, N_BUCKETS - 1)
        out.append(np.where(valid, np.where(n < max_exact, n, large), -1))
    return np.stack(out).astype(np.int32)


def _bias_build(name, rel_bias, buckets):
    def body(tbl_ref, bk_ref, o_ref):
        g = pl.program_id(0)
        bk = bk_ref[0]
        for h in range(HEADS_PER_GROUP):
            acc = jnp.zeros(bk.shape, F32)
            for b in range(N_BUCKETS):
                acc = jnp.where(bk == b, tbl_ref[b, g * HEADS_PER_GROUP + h], acc)
            o_ref[h] = jnp.where(bk < 0, NEG_INF, acc)

    return pl.pallas_call(
        body, name=name, out_shape=jax.ShapeDtypeStruct((N_HEADS, N_BACK, 2 * N_BACK), F32), grid=(N_GROUPS,),
        in_specs=[pl.BlockSpec(memory_space=pltpu.SMEM), pl.BlockSpec((1, N_BACK, 2 * N_BACK), lambda g: (g, 0, 0))],
        out_specs=pl.BlockSpec((HEADS_PER_GROUP, N_BACK, 2 * N_BACK), lambda g: (g, 0, 0)),
        compiler_params=_params(("arbitrary",)),
    )(rel_bias, buckets)


def _bias_grad(name, dbs, buckets):
    nd = len(dbs)

    def body(*refs):
        bk = refs[nd][0]
        o_ref = refs[nd + 1]
        lane = lax.broadcasted_iota(jnp.int32, (1, 128), 1)
        db = [sum(r[h] for r in refs[:nd]) for h in range(HEADS_PER_GROUP)]
        for b in range(N_BUCKETS):
            row = jnp.zeros((1, 128), F32)
            for h in range(HEADS_PER_GROUP):
                s = jnp.sum(jnp.where(bk == b, db[h], 0.0), axis=0, keepdims=True)
                s = jnp.sum(s, axis=1, keepdims=True)
                row = jnp.where(lane // 32 == h, s, row)
            o_ref[0, b:b + 1, :] = row

    spec = pl.BlockSpec((HEADS_PER_GROUP, N_BACK, 2 * N_BACK), lambda g: (g, 0, 0))
    return pl.pallas_call(
        body, name=name, out_shape=jax.ShapeDtypeStruct((N_GROUPS, N_BUCKETS, 128), F32), grid=(N_GROUPS,),
        in_specs=[spec] * nd + [pl.BlockSpec((1, N_BACK, 2 * N_BACK), lambda g: (g, 0, 0))],
        out_specs=pl.BlockSpec((1, N_BUCKETS, 128), lambda g: (g, 0, 0)), compiler_params=_params(("arbitrary",)),
    )(*dbs, buckets)


def _head_cols(h):
    return slice(h * HEAD_DIM, (h + 1) * HEAD_DIM)


def _store_dilated(o_ref, tile_ref, val, d):
    nlb, rows, _ = tile_ref.shape
    width = 128 * nlb
    for lb in range(nlb):
        tile_ref[lb] = val[:, 128 * lb:128 * (lb + 1)]
    for r in range(d):
        for lb in range(nlb):
            c0 = r * width + 128 * lb
            o_ref[:, c0:c0 + 128] = tile_ref[lb, pl.ds(r, rows // d, stride=d), :].astype(o_ref.dtype)


def _attn_qkv_fwd(name, h, wq):
    t, k = h.shape
    tm = _pick(t, (512, 256))
    width = wq.shape[1]

    def body(h_ref, w0, w1, w2, o0, o1, o2, tile_ref):
        hv = h_ref[...]
        for g, (w_ref, o_ref) in enumerate(zip((w0, w1, w2), (o0, o1, o2))):
            acc = lax.dot_general(hv, w_ref[...], (((1,), (1,)), ((), ())), preferred_element_type=F32)
            if DILATIONS[g] == 1:
                o_ref[...] = acc.astype(BF16)
            else:
                _store_dilated(o_ref, tile_ref, acc, DILATIONS[g])

    return pl.pallas_call(
        body, name=name,
        out_shape=[jax.ShapeDtypeStruct((t // d, d * width), BF16) for d in DILATIONS], grid=(t // tm,),
        in_specs=[pl.BlockSpec((tm, k), lambda i: (i, 0))] + [pl.BlockSpec((width, k), lambda i: (0, 0))] * 3,
        out_specs=[pl.BlockSpec((tm // d, d * width), lambda i: (i, 0)) for d in DILATIONS],
        scratch_shapes=[pltpu.VMEM((width // 128, tm, 128), F32)],
        compiler_params=_params(("parallel",), V7X_VMEM_LIMIT_BYTES),
    )(h, wq[0], wq[1], wq[2])


def _attn_fwd(name, qkv, bias, g):
    d = DILATIONS[g]
    tq = qkv.shape[0]
    t = tq * d
    nblk = 3
    scale = HEAD_DIM ** -0.5

    def body(q_ref, kp_ref, kc_ref, vp_ref, vc_ref, b_ref, o_ref, l_ref):
        m2 = pl.program_id(1)
        col = lax.broadcasted_iota(jnp.int32, (N_BACK, 2 * N_BACK), 1)
        lane = lax.broadcasted_iota(jnp.int32, (N_BACK, 128), 1)
        for sub in range(2):
            rows = slice(N_BACK * sub, N_BACK * (sub + 1))
            lse_tile = jnp.zeros((N_BACK, 128), F32)
            outs = []
            for h in range(HEADS_PER_GROUP):
                hc = _head_cols(h)
                if sub == 0:
                    kk = jnp.concatenate([kp_ref[:, hc], kc_ref[0:N_BACK, hc]], axis=0)
                    vv = jnp.concatenate([vp_ref[:, hc], vc_ref[0:N_BACK, hc]], axis=0)
                else:
                    kk, vv = kc_ref[:, hc], vc_ref[:, hc]
                s = lax.dot_general(q_ref[rows, hc], kk, (((1,), (1,)), ((), ())), preferred_element_type=F32)
                s = s * scale + b_ref[h]
                if sub == 0:
                    s = jnp.where((col >= N_BACK) | (m2 > 0), s, NEG_INF)
                m = jnp.max(s, axis=-1, keepdims=True)
                p = jnp.exp(s - m)
                den = jnp.sum(p, axis=-1, keepdims=True)
                outs.append(jnp.dot(p.astype(BF16), vv, preferred_element_type=F32) / den)
                lse_tile = jnp.where(lane // 32 == h, m + jnp.log(den), lse_tile)
            o_ref[rows, :] = jnp.concatenate(outs, axis=1)
            l_ref[rows, :] = lse_tile

    def blk(part, prev):
        if prev:
            return pl.BlockSpec((N_BACK, GROUP_COLS), lambda r, n: (jnp.maximum(2 * n - 1, 0), r * nblk + part))
        return pl.BlockSpec((2 * N_BACK, GROUP_COLS), lambda r, n: (n, r * nblk + part))

    o, l = pl.pallas_call(
        body, name=name,
        out_shape=[jax.ShapeDtypeStruct((tq, d * GROUP_COLS), F32), jax.ShapeDtypeStruct((tq, d * 128), F32)],
        grid=(d, tq // (2 * N_BACK)),
        in_specs=[blk(0, False), blk(1, True), blk(1, False), blk(2, True), blk(2, False),
                  pl.BlockSpec((HEADS_PER_GROUP, N_BACK, 2 * N_BACK), lambda r, n: (g, 0, 0))],
        out_specs=[pl.BlockSpec((2 * N_BACK, GROUP_COLS), lambda r, n: (n, r)),
                   pl.BlockSpec((2 * N_BACK, 128), lambda r, n: (n, r))],
        compiler_params=_params(("parallel", "parallel")),
    )(qkv, qkv, qkv, qkv, qkv, bias)
    return o.reshape(t, GROUP_COLS), l.reshape(t, 128), l


def _group_weights(l_refs, h):
    ls = [l_ref[:, 32 * h:32 * h + 1] for l_ref in l_refs]
    m = jnp.maximum(jnp.maximum(ls[0], ls[1]), ls[2])
    es = [jnp.exp(l - m) for l in ls]
    tot = es[0] + es[1] + es[2]
    return [e / tot for e in es]


def _attn_merge_out(name, os_, ls, wot, x, gain):
    t, d_model = x.shape
    tr = _pick(t, (512, 256, 128))

    def body(o0, o1, o2, l0, l1, l2, w_ref, x_ref, g_ref, om_ref, out_ref, h_ref):
        o_refs = (o0, o1, o2)
        pieces = [[None] * HEADS_PER_GROUP for _ in range(N_GROUPS)]
        for h in range(HEADS_PER_GROUP):
            al = _group_weights((l0, l1, l2), h)
            for g in range(N_GROUPS):
                pieces[g][h] = o_refs[g][:, _head_cols(h)] * al[g]
        om = jnp.concatenate([p for row in pieces for p in row], axis=1).astype(BF16)
        om_ref[...] = om
        out = x_ref[...] + lax.dot_general(om, w_ref[...], (((1,), (1,)), ((), ())), preferred_element_type=F32)
        out_ref[...] = out
        h_ref[...] = _rms_rows(out, g_ref[...])

    so = pl.BlockSpec((tr, GROUP_COLS), lambda i: (i, 0))
    sl = pl.BlockSpec((tr, 128), lambda i: (i, 0))
    sx = pl.BlockSpec((tr, d_model), lambda i: (i, 0))
    return pl.pallas_call(
        body, name=name,
        out_shape=[jax.ShapeDtypeStruct((t, D_ATTN), BF16), jax.ShapeDtypeStruct((t, d_model), F32),
                   jax.ShapeDtypeStruct((t, d_model), BF16)],
        grid=(t // tr,),
        in_specs=[so] * 3 + [sl] * 3 + [pl.BlockSpec((d_model, D_ATTN), lambda i: (0, 0)), sx,
                                        pl.BlockSpec((1, d_model), lambda i: (0, 0))],
        out_specs=[pl.BlockSpec((tr, D_ATTN), lambda i: (i, 0)), sx, sx],
        compiler_params=_params(("parallel",), V7X_VMEM_LIMIT_BYTES),
    )(*os_, *ls, wot, x, gain)


def _attn_bwd_prep(name, dx, wot, os_, ls):
    t, d_model = dx.shape
    tr = _pick(t, (512, 256))

    def body(dx_ref, w_ref, o0, o1, o2, l0, l1, l2, d0, d1, d2, c0, c1, c2, dtile_ref, ctile_ref):
        o_refs, d_refs, c_refs = (o0, o1, o2), (d0, d1, d2), (c0, c1, c2)
        d_out = jnp.dot(dx_ref[...].astype(BF16), w_ref[...], preferred_element_type=F32)
        lane = lax.broadcasted_iota(jnp.int32, (tr, 128), 1)
        dos = [[None] * HEADS_PER_GROUP for _ in range(N_GROUPS)]
        cs = [jnp.zeros((tr, 128), F32) for _ in range(N_GROUPS)]
        for h in range(HEADS_PER_GROUP):
            al = _group_weights((l0, l1, l2), h)
            tot = jnp.zeros((tr, 1), F32)
            for g in range(N_GROUPS):
                dv = d_out[:, g * GROUP_COLS + h * HEAD_DIM:g * GROUP_COLS + (h + 1) * HEAD_DIM]
                tot = tot + al[g] * jnp.sum(dv * o_refs[g][:, _head_cols(h)], axis=-1, keepdims=True)
                dos[g][h] = dv * al[g]
            for g in range(N_GROUPS):
                cs[g] = jnp.where(lane // 32 == h, -al[g] * tot, cs[g])
        for g in range(N_GROUPS):
            do_g = jnp.concatenate(dos[g], axis=1)
            if DILATIONS[g] == 1:
                d_refs[g][...] = do_g.astype(BF16)
                c_refs[g][...] = cs[g]
            else:
                _store_dilated(d_refs[g], dtile_ref, do_g, DILATIONS[g])
                _store_dilated(c_refs[g], ctile_ref, cs[g], DILATIONS[g])

    so = pl.BlockSpec((tr, GROUP_COLS), lambda i: (i, 0))
    sl = pl.BlockSpec((tr, 128), lambda i: (i, 0))
    res = pl.pallas_call(
        body, name=name,
        out_shape=[jax.ShapeDtypeStruct((t // d, d * GROUP_COLS), BF16) for d in DILATIONS]
        + [jax.ShapeDtypeStruct((t // d, d * 128), F32) for d in DILATIONS],
        grid=(t // tr,),
        in_specs=[pl.BlockSpec((tr, d_model), lambda i: (i, 0)), pl.BlockSpec((d_model, D_ATTN), lambda i: (0, 0))]
        + [so] * 3 + [sl] * 3,
        out_specs=[pl.BlockSpec((tr // d, d * GROUP_COLS), lambda i: (i, 0)) for d in DILATIONS]
        + [pl.BlockSpec((tr // d, d * 128), lambda i: (i, 0)) for d in DILATIONS],
        scratch_shapes=[pltpu.VMEM((GROUP_COLS // 128, tr, 128), F32), pltpu.VMEM((1, tr, 128), F32)],
        compiler_params=_params(("parallel",), V7X_VMEM_LIMIT_BYTES),
    )(dx, wot, *os_, *ls)
    return res[:3], res[3:]


def _attn_bwd(name, qkv, do, lse, cterm, bias, g):
    d = DILATIONS[g]
    tq = qkv.shape[0]
    t = tq * d
    nb = tq // N_BACK
    nblk = 3
    scale = HEAD_DIM ** -0.5
    nt = (((1,), (1,)), ((), ()))
    tn = (((0,), (0,)), ((), ()))

    def body(q2, qx, kp, k2, vp, v2, do2, dox, l2, lx, c2, cx, b_ref, dqkv_ref, db_ref):
        m2 = pl.program_id(1)

        @pl.when((m2 == 0) & (pl.program_id(0) == 0))
        def _():
            db_ref[...] = jnp.zeros_like(db_ref)

        row2 = lax.broadcasted_iota(jnp.int32, (2 * N_BACK, N_BACK), 0)
        lo, hi = slice(0, N_BACK), slice(N_BACK, 2 * N_BACK)
        for sub in range(2):
            rows = hi if sub else lo
            has_prev = True if sub else m2 > 0
            has_next = 2 * m2 + 2 < nb if sub else True
            on_ac = (row2 < N_BACK) | has_next
            dqs, dks, dvs = [], [], []
            for h in range(HEADS_PER_GROUP):
                hc = _head_cols(h)
                st = slice(32 * h, 32 * h + 1)
                b_prev, b_same = b_ref[h, :, 0:N_BACK], b_ref[h, :, N_BACK:]
                q0, k1, v1, d0 = q2[rows, hc], k2[rows, hc], v2[rows, hc], do2[rows, hc]
                l0, c0 = l2[rows, st], c2[rows, st]
                if sub:
                    k0, v0 = k2[lo, hc], v2[lo, hc]
                    q_ac = jnp.concatenate([q0, qx[:, hc]], axis=0)
                    d_ac = jnp.concatenate([d0, dox[:, hc]], axis=0)
                    l_ac = jnp.concatenate([l0, lx[:, st]], axis=0)
                    c_ac = jnp.concatenate([c0, cx[:, st]], axis=0)
                else:
                    k0, v0 = kp[:, hc], vp[:, hc]
                    q_ac, d_ac, l_ac, c_ac = q2[:, hc], do2[:, hc], l2[:, st], c2[:, st]
                s_ac = (lax.dot_general(q_ac, k1, nt, preferred_element_type=F32) * scale
                        + jnp.concatenate([b_same, b_prev], axis=0))
                p_ac = jnp.where(on_ac, jnp.exp(s_ac - l_ac), 0.0)
                ds_ac = p_ac * (lax.dot_general(d_ac, v1, nt, preferred_element_type=F32) + c_ac)
                s_b = lax.dot_general(q0, k0, nt, preferred_element_type=F32) * scale + b_prev
                p_b = jnp.where(has_prev, jnp.exp(s_b - l0), 0.0)
                ds_b = p_b * (lax.dot_general(d0, v0, nt, preferred_element_type=F32) + c0)
                ds_a = ds_ac[0:N_BACK]
                dqs.append(scale * jnp.dot(jnp.concatenate([ds_b, ds_a], axis=1).astype(BF16),
                                           jnp.concatenate([k0, k1], axis=0), preferred_element_type=F32))
                dks.append(scale * lax.dot_general(ds_ac.astype(BF16), q_ac, tn, preferred_element_type=F32))
                dvs.append(lax.dot_general(p_ac.astype(BF16), d_ac, tn, preferred_element_type=F32))
                db_ref[h, :, 0:N_BACK] += ds_b
                db_ref[h, :, N_BACK:] += ds_a
            dqkv_ref[rows, :] = jnp.concatenate(dqs + dks + dvs, axis=1).astype(BF16)

    def blk(width, which, col):
        if which == "prev":
            return pl.BlockSpec((N_BACK, width), lambda r, n: (jnp.maximum(2 * n - 1, 0), col(r)))
        if which == "next":
            return pl.BlockSpec((N_BACK, width), lambda r, n: (jnp.minimum(2 * n + 2, nb - 1), col(r)))
        return pl.BlockSpec((2 * N_BACK, width), lambda r, n: (n, col(r)))

    def qkv_blk(part, which):
        return blk(GROUP_COLS, which, lambda r: r * nblk + part)

    def grp_blk(width, which):
        return blk(width, which, lambda r: r)

    qv, dov, lv, cv = qkv, do, lse, cterm
    dqkv_g, db = pl.pallas_call(
        body, name=name,
        out_shape=[jax.ShapeDtypeStruct((tq, d * 3 * GROUP_COLS), BF16),
                   jax.ShapeDtypeStruct((HEADS_PER_GROUP, N_BACK, 2 * N_BACK), F32)],
        grid=(d, nb // 2),
        in_specs=[qkv_blk(0, "same"), qkv_blk(0, "next"), qkv_blk(1, "prev"), qkv_blk(1, "same"),
                  qkv_blk(2, "prev"), qkv_blk(2, "same"), grp_blk(GROUP_COLS, "same"), grp_blk(GROUP_COLS, "next"),
                  grp_blk(128, "same"), grp_blk(128, "next"), grp_blk(128, "same"), grp_blk(128, "next"),
                  pl.BlockSpec((HEADS_PER_GROUP, N_BACK, 2 * N_BACK), lambda r, n: (g, 0, 0))],
        out_specs=[pl.BlockSpec((2 * N_BACK, 3 * GROUP_COLS), lambda r, n: (n, r)),
                   pl.BlockSpec((HEADS_PER_GROUP, N_BACK, 2 * N_BACK), lambda r, n: (0, 0, 0))],
        compiler_params=_params(("arbitrary", "arbitrary")),
    )(qv, qv, qv, qv, qv, qv, dov, dov, lv, lv, cv, cv, bias)
    return dqkv_g.reshape(t, 3 * GROUP_COLS), db


def _row(v):
    return v.reshape(1, -1)


def _glu_epi(accs, extras, rows):
    a = (accs[0] + rows[0]).astype(BF16)
    gt = (accs[1] + rows[1]).astype(BF16)
    return a, gt, a.astype(F32) * _sigmoid(gt.astype(F32))


def _swiglu_epi(accs, extras, rows):
    gq, uq = accs[0].astype(BF16), accs[1].astype(BF16)
    gf = gq.astype(F32)
    return gq, uq, gf * _sigmoid(gf) * uq.astype(F32)


def _group_rows(w):
    parts = [w[p * D_ATTN:(p + 1) * D_ATTN].reshape(N_GROUPS, GROUP_COLS, -1) for p in range(3)]
    return jnp.concatenate(parts, axis=1)


def _ungroup_rows(wg):
    return jnp.concatenate([wg[g][p * GROUP_COLS:(p + 1) * GROUP_COLS] for p in range(3) for g in range(N_GROUPS)],
                           axis=0)


def _local_step(x, target, sm, depth, fetch, emit):
    d_model = x.shape[1]
    buckets = jnp.asarray(_bucket_tables())
    bias = _bias_build("bias_build", sm["rel_bias"], buckets)
    saved = []
    h = _rmsnorm_fwd("rms_mix_fwd0", x, _row(sm["norm_mix"][0]))
    for i in range(depth):
        j = i // 2
        rec = {"x_mix": x}
        wm = fetch(2 * i, x)
        rec.update(h_mix=h, wm=wm)
        ffn_gain = _row(sm["norm_ffn"][i])
        if i % 2 == 0:
            c = wm["w1t"].shape[0] // 2
            tn = _pick(c, (1024, 512, 256, 128))
            b1 = sm["conv_b_pw1"][j]
            a, gt, glu = _mm_nt(f"conv_pw1_fwd{j}", h, [wm["w1t"]] * 2, [0, c // tn], c, _glu_epi, (BF16,) * 3,
                                rows=[_row(b1[:c]), _row(b1[c:])], tn=tn)
            wm.update(fetch(2 * i, a, 1))
            dw, s = _dwconv_fwd(f"dwconv_fwd{j}", glu, wm["wdw"], _row(sm["conv_b_dw"][j]),
                                _row(sm["conv_ln_g"][j]), _row(sm["conv_ln_b"][j]))
            x, h2 = _project_residual(f"conv_pw2_fwd{j}", s, wm["w2"], x, bias=_row(sm["conv_b_pw2"][j]),
                                      gain=ffn_gain)
            rec.update(a=a, gt=gt, dw=dw, s=s)
        else:
            wq = _group_rows(wm["wqkvt"])
            qkv = _attn_qkv_fwd(f"attn_qkv_fwd{j}", h, wq)
            og = [_attn_fwd(f"attn_fwd{j}_{g}", qkv[g], bias, g) for g in range(N_GROUPS)]
            os_, ls = [o[0] for o in og], [o[1] for o in og]
            om, x, h2 = _attn_merge_out(f"attn_out_fwd{j}", os_, ls, wm["wot"], x, ffn_gain)
            rec.update(qkv=qkv, os=os_, ls=ls, lse_views=[o[2] for o in og], om=om, wq=wq)
        rec["x_ffn"] = x
        wf = fetch(2 * i + 1, x)
        f = wf["wd"].shape[0]
        gq, uq, act = _mm_nt(f"ffn_up_fwd{i}", h2, [wf["wgt"], wf["wut"]], [0, 0], f, _swiglu_epi, (BF16,) * 3)
        x, h = _project_residual(f"ffn_down_fwd{i}", act, wf["wd"], x,
                                 gain=_row(sm["norm_mix"][i + 1]) if i + 1 < depth else None)
        rec.update(h_ffn=h2, gq=gq, uq=uq, wf=wf)
        saved.append(rec)

    dx, g_final, loss_cols = _loss_head("loss_head", x, _row(sm["final_norm"]), target)

    g_mix, g_ffn = [None] * depth, [None] * depth
    nconv = (depth + 1) // 2
    g_b1, g_bdw, g_lng, g_lnb, g_b2 = ([None] * nconv for _ in range(5))
    dbias = []
    for i in reversed(range(depth)):
        j = i // 2
        rec = saved[i]
        wm, wf = rec["wm"], rec["wf"]
        dgate, dup, dwd = _ffn_down_bwd(f"ffn_down_bwd{i}", dx, wf["wd"], rec["gq"], rec["uq"])
        gf = {"wd": dwd}
        gf["wgt"], gf["wut"] = _mm_tn_pair(f"ffn_gate_up_dw{i}", dgate, dup, rec["h_ffn"])
        dx, g_ffn[i] = _mm_nn_rms_bwd(f"ffn_up_bwd{i}", [dgate, dup], [wf["wgt"], wf["wut"]], rec["x_ffn"],
                                      _row(sm["norm_ffn"][i]), dx, tm=256)
        dx = emit(2 * i + 1, gf, dx)
        if i % 2 == 0:
            c = wm["w2"].shape[0]
            gm = {}
            gm["w2"], g_b2[j] = _mm_tn(f"conv_pw2_dw{j}", rec["s"], dx, colsum_b=True)
            ddw, sums = _ln_silu_bwd(f"conv_pw2_bwd{j}", dx, wm["w2"], rec["dw"], _row(sm["conv_ln_g"][j]),
                                     _row(sm["conv_ln_b"][j]))
            g_lng[j], g_lnb[j], g_bdw[j] = sums[0], sums[1], sums[2]
            du, dwk, db1 = _dwconv_bwd(f"dwconv_bwd{j}", ddw, rec["a"], rec["gt"], wm["wdw"])
            gm["wdw"] = dwk[:CONV_WIDTH]
            g_b1[j] = db1[0]
            gm["w1t"] = _mm_tn(f"conv_pw1_dw{j}", du, rec["h_mix"])
            dh_terms = ([du], [wm["w1t"]])
        else:
            gm = {"wot": _mm_tn(f"attn_out_dw{j}", dx, rec["om"])}
            dos, cs = _attn_bwd_prep(f"attn_out_bwd{j}", dx, wm["wot"], rec["os"], rec["ls"])
            back = [_attn_bwd(f"attn_bwd{j}_{g}", rec["qkv"][g], dos[g], rec["lse_views"][g], cs[g], bias, g)
                    for g in range(N_GROUPS)]
            dqkv = [b[0] for b in back]
            dbias.append(jnp.concatenate([b[1] for b in back], axis=0))
            dh_terms = (dqkv, [rec["wq"][g] for g in range(N_GROUPS)])
            gm["wqkvt"] = _ungroup_rows([_mm_tn(f"attn_qkv_dw{j}_{g}", dqkv[g], rec["h_mix"])
                                         for g in range(N_GROUPS)])
        dx, g_mix[i] = _mm_nn_rms_bwd(f"mix_in_bwd{i}", dh_terms[0], dh_terms[1], rec["x_mix"],
                                      _row(sm["norm_mix"][i]), dx)
        dx = emit(2 * i, gm, dx)

    gb = _bias_grad("bias_grad", dbias, buckets)
    g_rel = jnp.transpose(gb[:, :, ::32], (1, 0, 2)).reshape(N_BUCKETS, N_HEADS)
    gsm = {
        "norm_mix": jnp.concatenate(g_mix, axis=0), "norm_ffn": jnp.concatenate(g_ffn, axis=0),
        "final_norm": g_final[0], "conv_b_pw1": jnp.stack(g_b1), "conv_b_dw": jnp.stack(g_bdw),
        "conv_ln_g": jnp.stack(g_lng), "conv_ln_b": jnp.stack(g_lnb),
        "conv_b_pw2": jnp.concatenate(g_b2, axis=0), "rel_bias": g_rel,
    }
    return loss_cols, dx, gsm


SMALL = ("norm_mix", "norm_ffn", "final_norm", "conv_b_pw1", "conv_b_dw", "conv_ln_g", "conv_ln_b", "conv_b_pw2",
         "rel_bias")
SHARDED = (("conv_w_pw1", "w1t", True), ("conv_w_pw2", "w2", False), ("attn_w_qkv", "wqkvt", True),
           ("attn_w_o", "wot", True), ("ffn_w_gate", "wgt", True), ("ffn_w_up", "wut", True),
           ("ffn_w_down", "wd", False))
ORDER = ("norm_mix", "norm_ffn", "final_norm", "conv_w_pw1", "conv_b_pw1", "conv_w_dw", "conv_b_dw", "conv_ln_g",
         "conv_ln_b", "conv_w_pw2", "conv_b_pw2", "attn_w_qkv", "attn_w_o", "rel_bias", "ffn_w_gate", "ffn_w_up",
         "ffn_w_down")
PACK_LANES = 128
PACK_ROW_TILE = 8


def _pack_small(vals):
    flat = jnp.concatenate([vals[n].reshape(-1) for n in SMALL])
    per_tile = PACK_LANES * PACK_ROW_TILE
    return jnp.pad(flat, (0, -flat.shape[0] % per_tile)).reshape(-1, PACK_LANES)


def _unpack_small(pack, like):
    flat, out, pos = pack.reshape(-1), {}, 0
    for n in SMALL:
        out[n] = flat[pos:pos + like[n].size].reshape(like[n].shape)
        pos += like[n].size
    return out


def _dw_blocks(w):
    l, k, c = w.shape
    blk = jnp.transpose(w.reshape(l, k, N_DEV, c // N_DEV), (2, 0, 1, 3)).reshape(N_DEV, l * k, c // N_DEV)
    return jnp.pad(blk, ((0, 0), (0, -(l * k) % 8), (0, 0)))


def kernel(x, norm_mix, norm_ffn, final_norm, conv_w_pw1, conv_b_pw1, conv_w_dw, conv_b_dw, conv_ln_g, conv_ln_b, conv_w_pw2, conv_b_pw2, attn_w_qkv, attn_w_o, rel_bias, ffn_w_gate, ffn_w_up, ffn_w_down, loss_target, m_norm_mix, m_norm_ffn, m_final_norm, m_conv_w_pw1, m_conv_b_pw1, m_conv_w_dw, m_conv_b_dw, m_conv_ln_g, m_conv_ln_b, m_conv_w_pw2, m_conv_b_pw2, m_attn_w_qkv, m_attn_w_o, m_rel_bias, m_ffn_w_gate, m_ffn_w_up, m_ffn_w_down, v_norm_mix, v_norm_ffn, v_final_norm, v_conv_w_pw1, v_conv_b_pw1, v_conv_w_dw, v_conv_b_dw, v_conv_ln_g, v_conv_ln_b, v_conv_w_pw2, v_conv_b_pw2, v_attn_w_qkv, v_attn_w_o, v_rel_bias, v_ffn_w_gate, v_ffn_w_up, v_ffn_w_down):
    w = dict(norm_mix=norm_mix, norm_ffn=norm_ffn, final_norm=final_norm, conv_w_pw1=conv_w_pw1,
             conv_b_pw1=conv_b_pw1, conv_w_dw=conv_w_dw, conv_b_dw=conv_b_dw, conv_ln_g=conv_ln_g,
             conv_ln_b=conv_ln_b, conv_w_pw2=conv_w_pw2, conv_b_pw2=conv_b_pw2, attn_w_qkv=attn_w_qkv,
             attn_w_o=attn_w_o, rel_bias=rel_bias, ffn_w_gate=ffn_w_gate, ffn_w_up=ffn_w_up, ffn_w_down=ffn_w_down)
    m = dict(norm_mix=m_norm_mix, norm_ffn=m_norm_ffn, final_norm=m_final_norm, conv_w_pw1=m_conv_w_pw1,
             conv_b_pw1=m_conv_b_pw1, conv_w_dw=m_conv_w_dw, conv_b_dw=m_conv_b_dw, conv_ln_g=m_conv_ln_g,
             conv_ln_b=m_conv_ln_b, conv_w_pw2=m_conv_w_pw2, conv_b_pw2=m_conv_b_pw2, attn_w_qkv=m_attn_w_qkv,
             attn_w_o=m_attn_w_o, rel_bias=m_rel_bias, ffn_w_gate=m_ffn_w_gate, ffn_w_up=m_ffn_w_up,
             ffn_w_down=m_ffn_w_down)
    v = dict(norm_mix=v_norm_mix, norm_ffn=v_norm_ffn, final_norm=v_final_norm, conv_w_pw1=v_conv_w_pw1,
             conv_b_pw1=v_conv_b_pw1, conv_w_dw=v_conv_w_dw, conv_b_dw=v_conv_b_dw, conv_ln_g=v_conv_ln_g,
             conv_ln_b=v_conv_ln_b, conv_w_pw2=v_conv_w_pw2, conv_b_pw2=v_conv_b_pw2, attn_w_qkv=v_attn_w_qkv,
             attn_w_o=v_attn_w_o, rel_bias=v_rel_bias, ffn_w_gate=v_ffn_w_gate, ffn_w_up=v_ffn_w_up,
             ffn_w_down=v_ffn_w_down)

    me = 4 * lax.axis_index("x") + 2 * lax.axis_index("y") + lax.axis_index("c")
    depth = ffn_w_gate.shape[0]
    n_conv, _, cb = conv_w_dw.shape

    def sublayer(key, layer):
        if key in ("wgt", "wut", "wd"):
            return 2 * layer + 1
        return 4 * layer if key in ("w1t", "w2") else 4 * layer + 2

    def landing(block, own):
        land = lax.empty((N_DEV,) + block.shape, block.dtype)
        return lax.dynamic_update_slice(land, own[None], (me,) + (0,) * block.ndim)

    by_sub = {s: [] for s in range(2 * depth)}
    for name, key, cols in SHARDED:
        sw = (jnp.swapaxes(w[name], 1, 2) if cols else w[name]).astype(BF16)
        for layer in range(sw.shape[0]):
            by_sub[sublayer(key, layer)].append((key, layer, sw[layer]))
    likes = {s: jnp.zeros((sum(sh.size for _, _, sh in by_sub[s]) // 1024, 1024), BF16) for s in by_sub}
    dw_shard = jnp.pad(conv_w_dw.reshape(-1, cb), ((0, -(n_conv * CONV_WIDTH) % 8), (0, 0)))
    like_dw = jnp.zeros(dw_shard.shape, F32)
    stages = {}
    for s in range(2 * depth):
        for entry in by_sub[s]:
            stages.setdefault((s, 0 if entry[0] != "w2" else 1), []).append(entry)
    stage_order = sorted(stages)
    groups = [([sh for _, _, sh in stages[st]], [landing(sh, sh) for _, _, sh in stages[st]]) for st in stage_order]
    groups.insert(1, ([dw_shard], [landing(dw_shard, dw_shard)]))
    gather, _ = _exchange_start("gather_start", groups, scatter=False, carry=jnp.zeros((8, 128), F32))
    handle = dict(zip(stage_order, gather[:1] + gather[2:]))
    dw_filters = []

    def fetch(s, after, part=0):
        st = (s, part)
        like = jnp.zeros((sum(sh.size for _, _, sh in stages[st]) // 1024, 1024), BF16)
        pieces = [(handle[st], like)]
        if st == (0, 1):
            pieces.append((gather[1], like_dw))
        landed = _exchange_wait(f"gather_wait{s}_{part}", pieces, after)
        out = {key: g.reshape(g.shape[0] * g.shape[1], g.shape[2]) for (key, _, _), g in zip(stages[st], landed[0])}
        if st == (0, 1):
            dw_all = landed[1][0]
            full = jnp.transpose(dw_all[:, :n_conv * CONV_WIDTH].reshape(N_DEV, n_conv, CONV_WIDTH, cb), (1, 2, 0, 3))
            full = jnp.pad(full.reshape(n_conv, CONV_WIDTH, N_DEV * cb), ((0, 0), (0, CONV_HALO - CONV_WIDTH), (0, 0)))
            dw_filters.extend(full[layer] for layer in range(n_conv))
        if "w2" in out:
            out["wdw"] = dw_filters[s // 4]
        return out

    scatter, dw_grads, started = {}, {}, {}

    def emit(s, gd, carry):
        parts = [gd[key].reshape(N_DEV, -1, gd[key].shape[1]) for key, _, _ in by_sub[s]]
        if "wdw" in gd:
            dw_grads[s // 4] = gd["wdw"]
        if s == 0:
            parts.append(_dw_blocks(jnp.stack([dw_grads[layer] for layer in range(n_conv)])))
        lands = [landing(p[0], lax.dynamic_index_in_dim(p, me, 0, keepdims=False)) for p in parts]
        groups = [(parts[:len(by_sub[s])], lands[:len(by_sub[s])])]
        if s == 0:
            groups.append((parts[-1:], lands[-1:]))
        if s == 0:
            scatter[s], started[0] = _exchange_start(f"scatter_start{s}", groups, scatter=True,
                                                     carry=jnp.zeros((8, 128), F32))
            return carry
        scatter[s], carry = _exchange_start(f"scatter_start{s}", groups, scatter=True, carry=carry)
        return carry

    sm = {n: w[n] for n in SMALL}
    loss_cols, dx, gsm = _local_step(x[0], loss_target[0], sm, depth, fetch, emit)
    loss = lax.psum(jnp.sum(loss_cols), ("x", "y", "c"))

    grads, summed, delta, new_m, new_v = {}, {}, {}, {}, {}

    def reduce_pieces(subs, landed):
        for s, recv in zip(subs, landed):
            for (key, layer, _), r in zip(by_sub[s], recv):
                summed[key, layer] = _sum8(f"sum_{key}{layer}", r)

    def update(names):
        for name in names:
            shape = w[name].shape
            res = _adamw(f"adamw_{name}",
                         *[t.reshape(-1, shape[-1]) for t in (grads[name], w[name], m[name], v[name])])
            delta[name], new_m[name], new_v[name] = (t.reshape(shape) for t in res)

    def stacked(name, key, cols):
        g = jnp.stack([summed[key, layer] for layer in range(w[name].shape[0])])
        return jnp.swapaxes(g, 1, 2) if cols else g

    early = sorted((s for s in scatter if s != 0), reverse=True)
    reduce_pieces(early, _exchange_wait("scatter_wait_early", [(scatter[s][0], likes[s]) for s in early],
                                        started[0]))
    late_names = [name for name, key, _ in SHARDED if any(k == key for k, _, _ in by_sub[0])]
    for name, key, cols in SHARDED:
        if name not in late_names:
            grads[name] = stacked(name, key, cols)
    early_names = [name for name, _, _ in SHARDED if name not in late_names]
    update(early_names)
    pack = _pack_small(gsm)
    ((pack_all,),) = _exchange("gather_small_grads", [([pack], pack)], scatter=False)
    pack_sum = _sum8("sum_small", pack_all)
    grads.update(_unpack_small(pack_sum, sm))

    landed = _exchange_wait("scatter_wait_last", [(scatter[0][0], likes[0]), (scatter[0][1], like_dw)],
                            new_v[early_names[-1]])
    reduce_pieces([0], landed[:1])
    for name, key, cols in SHARDED:
        if name in late_names:
            grads[name] = stacked(name, key, cols)
    grads["conv_w_dw"] = _sum8("sum_wdw", landed[1][0])[:n_conv * CONV_WIDTH].reshape(conv_w_dw.shape)
    update(late_names + ["conv_w_dw"])
    res = _adamw("adamw_small", pack_sum, _pack_small(sm), _pack_small({n: m[n] for n in SMALL}),
                 _pack_small({n: v[n] for n in SMALL}))
    for dst, t in zip((delta, new_m, new_v), res):
        dst.update(_unpack_small(t, sm))

    outs = [loss, dx[None]]
    for d in (grads, delta, new_m, new_v):
        outs += [d[n] for n in ORDER]
    return tuple(outs)
```

```python
import functools
import math

import numpy as np
import jax
import jax.numpy as jnp
from jax import lax
from jax.experimental import pallas as pl
from jax.experimental.pallas import tpu as pltpu

F32 = jnp.float32
BF16 = jnp.bfloat16

N_DEV = 8
HEAD_DIM = 64
HEADS_PER_GROUP = 4
GROUP_COLS = HEADS_PER_GROUP * HEAD_DIM
DILATIONS = (1, 4, 16)
N_BACK = 128
N_GROUPS = 3
N_HEADS = 12
D_ATTN = 768
N_BUCKETS = 32
REL_MAX_DISTANCE = 2048
CONV_WIDTH = 31
CONV_HALO = 32
EPS = 1e-6
NEG_INF = -1e30
ADAM_LR, ADAM_B1, ADAM_B2, ADAM_EPS, ADAM_WD, ADAM_STEP = 0.001, 0.9, 0.999, 1e-08, 0.01, 10
V7X_VMEM_LIMIT_BYTES = 56 * 1024 * 1024
MESH = pl.DeviceIdType.MESH
ANY = pl.BlockSpec(memory_space=pl.ANY)


def _pick(n, prefs):
    for p in prefs:
        if n % p == 0:
            return p
    return n


def _params(sem, vmem=None):
    return pltpu.CompilerParams(dimension_semantics=sem, vmem_limit_bytes=vmem)


def _sigmoid(x):
    return 1.0 / (1.0 + jnp.exp(-x))


def _exchange(name, groups, scatter):
    n_arr = [len(arrs) for arrs, _ in groups]
    n_in = sum(n_arr) + len(groups)
    ng = len(groups)

    def body(*refs):
        ins, outs, (send_sems, recv_sems, local_sems) = refs[:n_in], refs[n_in:-3], refs[-3:]
        x, y, c = lax.axis_index("x"), lax.axis_index("y"), lax.axis_index("c")
        me = 4 * x + 2 * y + c
        pos_in = pos_out = 0
        plans = []
        for gi in range(ng):
            srcs = ins[pos_in:pos_in + n_arr[gi]]
            like = ins[pos_in + n_arr[gi]]
            dsts = outs[pos_out:pos_out + n_arr[gi]]
            pos_in += n_arr[gi] + 1
            pos_out += n_arr[gi]
            plans.append((gi, srcs, like, dsts))
        local = []
        for gi, srcs, like, dsts in plans:
            for s, d in zip(srcs, dsts):
                cp = pltpu.make_async_copy(s.at[me] if scatter else s, d.at[me], local_sems.at[gi])
                cp.start()
                local.append(cp)
        for delta in range(1, N_DEV):
            dx, dy, dc = (delta >> 2) & 1, (delta >> 1) & 1, delta & 1
            px, py, pc = (1 - x if dx else x), (1 - y if dy else y), (1 - c if dc else c)
            peer = 4 * px + 2 * py + pc
            for gi, srcs, like, dsts in plans:
                for s, d in zip(srcs, dsts):
                    pltpu.make_async_remote_copy(
                        src_ref=s.at[peer] if scatter else s, dst_ref=d.at[me],
                        send_sem=send_sems.at[gi, delta - 1], recv_sem=recv_sems.at[gi, delta - 1],
                        device_id=(px, py, pc), device_id_type=MESH).start()
        for delta in range(1, N_DEV):
            for gi, srcs, like, dsts in plans:
                pltpu.make_async_remote_copy(
                    src_ref=like, dst_ref=like, send_sem=send_sems.at[gi, delta - 1],
                    recv_sem=recv_sems.at[gi, delta - 1], device_id=(x, y, c), device_id_type=MESH).wait()
        for cp in local:
            cp.wait()

    operands, out_shape = [], []
    for arrs, like in groups:
        operands += list(arrs) + [like]
        for a in arrs:
            blk = a.shape[1:] if scatter else a.shape
            out_shape.append(jax.ShapeDtypeStruct((N_DEV,) + tuple(blk), a.dtype))
    outs = pl.pallas_call(
        body, name=name, out_shape=out_shape, in_specs=[ANY] * len(operands), out_specs=[ANY] * len(out_shape),
        scratch_shapes=[pltpu.SemaphoreType.DMA((ng, N_DEV - 1)), pltpu.SemaphoreType.DMA((ng, N_DEV - 1)),
                        pltpu.SemaphoreType.DMA((ng,))],
        compiler_params=pltpu.CompilerParams(has_side_effects=True),
    )(*operands)
    res, pos = [], 0
    for n in n_arr:
        res.append(list(outs[pos:pos + n]))
        pos += n
    return res


HBM = pl.BlockSpec(memory_space=pltpu.HBM)
SEM = pl.BlockSpec(memory_space=pltpu.SEMAPHORE)
EFFECT = pltpu.SideEffectType.DATAFLOW_SIDE_EFFECTING


def _in_hbm(a):
    return pltpu.with_memory_space_constraint(a, pltpu.HBM)


def _exchange_start(name, groups, scatter, carry):
    ns = [len(s) for s, _ in groups]
    n_in = 2 * sum(ns)

    def body(*refs):
        ins, outs = refs[:n_in], refs[n_in + 1:]
        x, y, c = lax.axis_index("x"), lax.axis_index("y"), lax.axis_index("c")
        me = 4 * x + 2 * y + c
        pi = po = 0
        for n in ns:
            srcs, lands = ins[pi:pi + n], ins[pi + n:pi + 2 * n]
            send_sems, recv_sems = outs[po], outs[po + 1]
            pi += 2 * n
            po += 2 + 2 * n
            for delta in range(1, N_DEV):
                dx, dy, dc = (delta >> 2) & 1, (delta >> 1) & 1, delta & 1
                px, py, pc = (1 - x if dx else x), (1 - y if dy else y), (1 - c if dc else c)
                peer = 4 * px + 2 * py + pc
                for s, d in zip(srcs, lands):
                    pltpu.make_async_remote_copy(
                        src_ref=s.at[peer] if scatter else s, dst_ref=d.at[me], send_sem=send_sems.at[delta - 1],
                        recv_sem=recv_sems.at[delta - 1], device_id=(px, py, pc), device_id_type=MESH).start()

    operands, out_shape, out_specs, aliases = [], [], [], {}
    for srcs, lands in groups:
        out_shape += [pltpu.SemaphoreType.DMA((N_DEV - 1,))] * 2
        out_specs += [SEM, SEM]
        for a in list(srcs) + list(lands):
            aliases[len(operands)] = len(out_shape)
            operands.append(_in_hbm(a))
            out_shape.append(pltpu.HBM(a.shape, a.dtype))
            out_specs.append(HBM)
    aliases[len(operands)] = len(out_shape)
    operands.append(_in_hbm(carry))
    out_shape.append(pltpu.HBM(carry.shape, carry.dtype))
    out_specs.append(HBM)
    outs = pl.pallas_call(
        body, name=name, out_shape=out_shape, in_specs=[HBM] * len(operands), out_specs=out_specs,
        input_output_aliases=aliases, compiler_params=pltpu.CompilerParams(has_side_effects=EFFECT),
    )(*operands)
    handles, po = [], 0
    for n in ns:
        handles.append((outs[po], outs[po + 1], list(outs[po + 2:po + 2 + n]), list(outs[po + 2 + n:po + 2 + 2 * n])))
        po += 2 + 2 * n
    return handles, outs[-1]


def _exchange_wait(name, pieces, after):
    ns = [len(h[2]) for h, _ in pieces]

    def body(*refs):
        x, y, c = lax.axis_index("x"), lax.axis_index("y"), lax.axis_index("c")
        pi = 0
        for n in ns:
            send_sems, recv_sems, like = refs[pi + 2 * n], refs[pi + 2 * n + 1], refs[pi + 2 * n + 2]
            pi += 2 * n + 3
            for delta in range(1, N_DEV):
                cp = pltpu.make_async_remote_copy(
                    src_ref=like, dst_ref=like, send_sem=send_sems.at[delta - 1], recv_sem=recv_sems.at[delta - 1],
                    device_id=(x, y, c), device_id_type=MESH)
                cp.wait_send()
                cp.wait_recv()

    operands, in_specs, out_shape, aliases = [], [], [], {}
    for (send_sems, recv_sems, srcs, lands), like in pieces:
        for a in srcs + lands:
            aliases[len(operands)] = len(out_shape)
            operands.append(a)
            in_specs.append(HBM)
            out_shape.append(pltpu.HBM(a.shape, a.dtype))
        operands += [send_sems, recv_sems, like]
        in_specs += [SEM, SEM, ANY]
    operands.append(after)
    in_specs.append(ANY)
    outs = pl.pallas_call(
        body, name=name, out_shape=out_shape, in_specs=in_specs, out_specs=[HBM] * len(out_shape),
        input_output_aliases=aliases, compiler_params=pltpu.CompilerParams(has_side_effects=EFFECT),
    )(*operands)
    res, po = [], 0
    for n in ns:
        res.append(list(outs[po + n:po + 2 * n]))
        po += 2 * n
    return res


def _sum8(name, r):
    _, rows, cols = r.shape
    tr = _pick(rows, (256, 128, 64, 32, 16, 8))

    def body(r_ref, o_ref):
        acc = r_ref[0].astype(F32)
        for p in range(1, N_DEV):
            acc = acc + r_ref[p].astype(F32)
        o_ref[...] = acc

    return pl.pallas_call(
        body, name=name, out_shape=jax.ShapeDtypeStruct((rows, cols), F32), grid=(rows // tr,),
        in_specs=[pl.BlockSpec((N_DEV, tr, cols), lambda i: (0, i, 0))],
        out_specs=pl.BlockSpec((tr, cols), lambda i: (i, 0)), compiler_params=_params(("parallel",)),
    )(r)


def _adamw(name, g, w, m, v):
    rows, cols = g.shape
    tr = _pick(rows, (512, 256, 128, 64, 32, 16, 8))
    c1 = 1.0 - ADAM_B1 ** ADAM_STEP
    c2 = 1.0 - ADAM_B2 ** ADAM_STEP

    def body(g_ref, w_ref, m_ref, v_ref, d_ref, nm_ref, nv_ref):
        gv = g_ref[...]
        nm = ADAM_B1 * m_ref[...] + (1.0 - ADAM_B1) * gv
        nv = ADAM_B2 * v_ref[...] + (1.0 - ADAM_B2) * (gv * gv)
        d_ref[...] = -ADAM_LR * ((nm / c1) / (jnp.sqrt(nv / c2) + ADAM_EPS) + ADAM_WD * w_ref[...])
        nm_ref[...] = nm
        nv_ref[...] = nv

    spec = pl.BlockSpec((tr, cols), lambda i: (i, 0))
    return pl.pallas_call(
        body, name=name, out_shape=[jax.ShapeDtypeStruct((rows, cols), F32)] * 3, grid=(rows // tr,),
        in_specs=[spec] * 4, out_specs=[spec] * 3, compiler_params=_params(("parallel",)),
    )(g, w, m, v)


def _mm_nt(name, a, ws, w_offs, n, epi, out_dtypes, extras=(), rows=(), tm=None, tn=None):
    m, k = a.shape
    tm = tm or _pick(m, (512, 256, 128))
    tn = tn or _pick(n, (1408, 1152, 1024, 512, 256, 128))
    nw, ne, nr = len(ws), len(extras), len(rows)

    def body(*refs):
        a_ref, w_refs = refs[0], refs[1:1 + nw]
        e_refs, r_refs = refs[1 + nw:1 + nw + ne], refs[1 + nw + ne:1 + nw + ne + nr]
        o_refs = refs[1 + nw + ne + nr:]
        av = a_ref[...].astype(BF16)
        accs = [lax.dot_general(av, w[...], (((1,), (1,)), ((), ())), preferred_element_type=F32) for w in w_refs]
        outs = epi(accs, [e[...] for e in e_refs], [r[...] for r in r_refs])
        for o_ref, o in zip(o_refs, outs):
            o_ref[...] = o.astype(o_ref.dtype)

    in_specs = [pl.BlockSpec((tm, k), lambda j, i: (i, 0))]
    in_specs += [pl.BlockSpec((tn, k), functools.partial(lambda j, i, off: (j + off, 0), off=off)) for off in w_offs]
    in_specs += [pl.BlockSpec((tm, tn), lambda j, i: (i, j))] * ne
    in_specs += [pl.BlockSpec((1, tn), lambda j, i: (0, j))] * nr
    return pl.pallas_call(
        body, name=name, out_shape=[jax.ShapeDtypeStruct((m, n), dt) for dt in out_dtypes],
        grid=(n // tn, m // tm), in_specs=in_specs,
        out_specs=[pl.BlockSpec((tm, tn), lambda j, i: (i, j))] * len(out_dtypes),
        compiler_params=_params(("parallel", "parallel"), V7X_VMEM_LIMIT_BYTES),
    )(a, *ws, *extras, *rows)


def _rms_rows(x, gain):
    r = lax.rsqrt(jnp.mean(x * x, axis=-1, keepdims=True) + EPS)
    return (x * r * gain).astype(BF16)


def _project_residual(name, a, b, x, bias=None, gain=None):
    m, k = a.shape
    n = b.shape[1]
    tm = _pick(m, (512, 256, 128))
    rows = [r for r in (bias, gain) if r is not None]

    def body(*refs):
        a_ref, b_ref, x_ref = refs[:3]
        r_refs = list(refs[3:3 + len(rows)])
        out = x_ref[...] + jnp.dot(a_ref[...], b_ref[...], preferred_element_type=F32)
        if bias is not None:
            out = out + r_refs.pop(0)[...]
        refs[3 + len(rows)][...] = out
        if gain is not None:
            refs[4 + len(rows)][...] = _rms_rows(out, r_refs.pop(0)[...])

    big = pl.BlockSpec((tm, n), lambda i: (i, 0))
    out_shape = [jax.ShapeDtypeStruct((m, n), F32)] + ([jax.ShapeDtypeStruct((m, n), BF16)] if gain is not None else [])
    res = pl.pallas_call(
        body, name=name, out_shape=out_shape, grid=(m // tm,),
        in_specs=[pl.BlockSpec((tm, k), lambda i: (i, 0)), pl.BlockSpec((k, n), lambda i: (0, 0)), big]
        + [pl.BlockSpec((1, n), lambda i: (0, 0))] * len(rows),
        out_specs=[big] * len(out_shape), compiler_params=_params(("parallel",), V7X_VMEM_LIMIT_BYTES),
    )(a, b, x, *rows)
    return res if gain is not None else (res[0], None)


def _mm_nn_rms_bwd(name, as_, bs, x, g, dx_out, tm=None):
    m, k = as_[0].shape
    n = bs[0].shape[1]
    tm = tm or _pick(m, (512, 256, 128))
    npair = len(as_)

    def body(*refs):
        a_refs, b_refs = refs[:npair], refs[npair:2 * npair]
        x_ref, g_ref, dxo_ref, dx_ref, dg_ref = refs[2 * npair:]
        dh = None
        for a_ref, b_ref in zip(a_refs, b_refs):
            p = jnp.dot(a_ref[...].astype(BF16), b_ref[...], preferred_element_type=F32)
            dh = p if dh is None else dh + p
        xv = x_ref[...]
        r = lax.rsqrt(jnp.mean(xv * xv, axis=-1, keepdims=True) + EPS)
        yv = xv * r
        dy = dh * g_ref[...]
        dx_ref[...] = dxo_ref[...] + r * (dy - yv * jnp.mean(dy * yv, axis=-1, keepdims=True))

        @pl.when(pl.program_id(0) == 0)
        def _():
            dg_ref[...] = jnp.zeros_like(dg_ref)

        dg_ref[...] += jnp.sum(dh * yv, axis=0, keepdims=True)

    big = pl.BlockSpec((tm, n), lambda i: (i, 0))
    row = pl.BlockSpec((1, n), lambda i: (0, 0))
    in_specs = [pl.BlockSpec((tm, k), lambda i: (i, 0))] * npair + [pl.BlockSpec((k, n), lambda i: (0, 0))] * npair
    return pl.pallas_call(
        body, name=name, out_shape=[jax.ShapeDtypeStruct((m, n), F32), jax.ShapeDtypeStruct((1, n), F32)],
        grid=(m // tm,), in_specs=in_specs + [big, row, big], out_specs=[big, row],
        compiler_params=_params(("arbitrary",), V7X_VMEM_LIMIT_BYTES),
    )(*as_, *bs, x, g, dx_out)


def _mm_tn(name, a, b, colsum_b=False, tm=None, tk=2048):
    t, ma = a.shape
    nb = b.shape[1]
    tm = tm or _pick(ma, (1408, 1152, 1024, 768, 512, 256, 128))
    tk = _pick(t, (tk, 256, 128))
    nk = t // tk

    def body(*refs):
        a_ref, b_ref, o_ref = refs[0], refs[1], refs[2]
        acc_ref = refs[-1]
        kk = pl.program_id(1)
        bv = b_ref[...]

        @pl.when(kk == 0)
        def _():
            acc_ref[...] = jnp.zeros_like(acc_ref)

        acc_ref[...] += lax.dot_general(a_ref[...].astype(BF16), bv.astype(BF16), (((0,), (0,)), ((), ())),
                                        preferred_element_type=F32)
        if colsum_b:
            s_ref = refs[3]

            @pl.when((kk == 0) & (pl.program_id(0) == 0))
            def _():
                s_ref[...] = jnp.zeros_like(s_ref)

            @pl.when(pl.program_id(0) == 0)
            def _():
                s_ref[...] += jnp.sum(bv.astype(F32), axis=0, keepdims=True)

        @pl.when(kk == nk - 1)
        def _():
            o_ref[...] = acc_ref[...].astype(o_ref.dtype)

    out_shape = [jax.ShapeDtypeStruct((ma, nb), BF16)]
    out_specs = [pl.BlockSpec((tm, nb), lambda i, kk: (i, 0))]
    if colsum_b:
        out_shape.append(jax.ShapeDtypeStruct((1, nb), F32))
        out_specs.append(pl.BlockSpec((1, nb), lambda i, kk: (0, 0)))
    res = pl.pallas_call(
        body, name=name, out_shape=out_shape, grid=(ma // tm, nk),
        in_specs=[pl.BlockSpec((tk, tm), lambda i, kk: (kk, i)), pl.BlockSpec((tk, nb), lambda i, kk: (kk, 0))],
        out_specs=out_specs, scratch_shapes=[pltpu.VMEM((tm, nb), F32)],
        compiler_params=_params(("arbitrary", "arbitrary"), V7X_VMEM_LIMIT_BYTES),
    )(a, b)
    return res if colsum_b else res[0]


def _mm_tn_pair(name, a1, a2, b, tk=1024):
    t, ma = a1.shape
    nb = b.shape[1]
    tm = _pick(ma, (1408, 1152, 1024, 768, 512, 256, 128))
    tk = _pick(t, (tk, 256, 128))
    nk = t // tk
    dims = (((0,), (0,)), ((), ()))

    def body(a1_ref, a2_ref, b_ref, o1_ref, o2_ref, acc1_ref, acc2_ref):
        kk = pl.program_id(1)
        bv = b_ref[...]

        @pl.when(kk == 0)
        def _():
            acc1_ref[...] = jnp.zeros_like(acc1_ref)
            acc2_ref[...] = jnp.zeros_like(acc2_ref)

        acc1_ref[...] += lax.dot_general(a1_ref[...], bv, dims, preferred_element_type=F32)
        acc2_ref[...] += lax.dot_general(a2_ref[...], bv, dims, preferred_element_type=F32)

        @pl.when(kk == nk - 1)
        def _():
            o1_ref[...] = acc1_ref[...].astype(BF16)
            o2_ref[...] = acc2_ref[...].astype(BF16)

    sa = pl.BlockSpec((tk, tm), lambda i, kk: (kk, i))
    so = pl.BlockSpec((tm, nb), lambda i, kk: (i, 0))
    return pl.pallas_call(
        body, name=name, out_shape=[jax.ShapeDtypeStruct((ma, nb), BF16)] * 2, grid=(ma // tm, nk),
        in_specs=[sa, sa, pl.BlockSpec((tk, nb), lambda i, kk: (kk, 0))], out_specs=[so, so],
        scratch_shapes=[pltpu.VMEM((tm, nb), F32)] * 2,
        compiler_params=_params(("arbitrary", "arbitrary"), V7X_VMEM_LIMIT_BYTES),
    )(a1, a2, b)


FFN_CHUNK = 256


def _ffn_down_bwd(name, dx, wd, gq, uq):
    t, d_model = dx.shape
    f = wd.shape[0]
    tm = _pick(t, (256, 128))
    ck = _pick(f, (FFN_CHUNK, 128))
    nt = (((1,), (1,)), ((), ()))
    tn = (((0,), (0,)), ((), ()))
    last = t // tm - 1

    def body(dx_ref, w_ref, g_ref, u_ref, dg_ref, du_ref, dw_ref, acc_ref):
        i = pl.program_id(0)

        @pl.when(i == 0)
        def _():
            acc_ref[...] = jnp.zeros_like(acc_ref)

        dxb = dx_ref[...].astype(BF16)
        for c0 in range(0, f, ck):
            cs = slice(c0, c0 + ck)
            dact = lax.dot_general(dxb, w_ref[cs, :], nt, preferred_element_type=F32)
            gf, uf = g_ref[:, cs].astype(F32), u_ref[:, cs].astype(F32)
            sg = _sigmoid(gf)
            dg_ref[:, cs] = (dact * uf * (sg * (1.0 + gf * (1.0 - sg)))).astype(BF16)
            du_ref[:, cs] = (dact * gf * sg).astype(BF16)
            act = (gf * sg * uf).astype(BF16)
            acc_ref[cs, :] += lax.dot_general(act, dxb, tn, preferred_element_type=F32)

        @pl.when(i == last)
        def _():
            dw_ref[...] = acc_ref[...].astype(BF16)

    wide = pl.BlockSpec((tm, f), lambda i: (i, 0))
    whole = pl.BlockSpec((f, d_model), lambda i: (0, 0))
    return pl.pallas_call(
        body, name=name,
        out_shape=[jax.ShapeDtypeStruct((t, f), BF16), jax.ShapeDtypeStruct((t, f), BF16),
                   jax.ShapeDtypeStruct((f, d_model), BF16)],
        grid=(t // tm,), in_specs=[pl.BlockSpec((tm, d_model), lambda i: (i, 0)), whole, wide, wide],
        out_specs=[wide, wide, whole], scratch_shapes=[pltpu.VMEM((f, d_model), F32)],
        compiler_params=_params(("arbitrary",), V7X_VMEM_LIMIT_BYTES),
    )(dx, wd, gq, uq)


def _rmsnorm_fwd(name, x, g):
    t, d = x.shape
    tr = _pick(t, (512, 256, 128))

    def body(x_ref, g_ref, h_ref):
        xv = x_ref[...]
        r = lax.rsqrt(jnp.mean(xv * xv, axis=-1, keepdims=True) + EPS)
        h_ref[...] = (xv * r * g_ref[...]).astype(BF16)

    return pl.pallas_call(
        body, name=name, out_shape=jax.ShapeDtypeStruct((t, d), BF16), grid=(t // tr,),
        in_specs=[pl.BlockSpec((tr, d), lambda i: (i, 0)), pl.BlockSpec((1, d), lambda i: (0, 0))],
        out_specs=pl.BlockSpec((tr, d), lambda i: (i, 0)), compiler_params=_params(("parallel",)),
    )(x, g)


def _loss_head(name, x, g, target):
    t, d = x.shape
    tr = _pick(t, (512, 256, 128))

    def body(x_ref, g_ref, t_ref, dx_ref, dg_ref, l_ref):
        xv = x_ref[...]
        r = lax.rsqrt(jnp.mean(xv * xv, axis=-1, keepdims=True) + EPS)
        yv = xv * r
        diff = yv * g_ref[...] - t_ref[...]
        dout = diff * (1.0 / d)
        dy = dout * g_ref[...]
        dx_ref[...] = r * (dy - yv * jnp.mean(dy * yv, axis=-1, keepdims=True))

        @pl.when(pl.program_id(0) == 0)
        def _():
            dg_ref[...] = jnp.zeros_like(dg_ref)
            l_ref[...] = jnp.zeros_like(l_ref)

        dg_ref[...] += jnp.sum(dout * yv, axis=0, keepdims=True)
        l_ref[...] += (0.5 / d) * jnp.sum(diff * diff, axis=0, keepdims=True)

    big = pl.BlockSpec((tr, d), lambda i: (i, 0))
    row = pl.BlockSpec((1, d), lambda i: (0, 0))
    return pl.pallas_call(
        body, name=name,
        out_shape=[jax.ShapeDtypeStruct((t, d), F32), jax.ShapeDtypeStruct((1, d), F32),
                   jax.ShapeDtypeStruct((1, d), F32)],
        grid=(t // tr,), in_specs=[big, row, big], out_specs=[big, row, row],
        compiler_params=_params(("arbitrary",)),
    )(x, g, target)


CONV_ROWS = 64
SUBLANES = 8


def _fill_window(win_ref, sh_ref, parts):
    rows = sh_ref.shape[2]
    for cb in range(win_ref.shape[0]):
        for r0, val in parts:
            win_ref[cb, r0:r0 + val.shape[0], :] = val[:, 128 * cb:128 * (cb + 1)]
        win_ref[cb, rows:rows + SUBLANES, :] = jnp.zeros((SUBLANES, 128), F32)
        for b in range(1, SUBLANES):
            sh_ref[b - 1, cb] = win_ref[cb, b:b + rows, :]


def _window_rows(win_ref, sh_ref, o, cb):
    b = o % SUBLANES
    if b == 0:
        return win_ref[cb, o:o + CONV_ROWS, :]
    return sh_ref[b - 1, cb, o - b:o - b + CONV_ROWS, :]


def _window_scratch(rows, c):
    return [pltpu.VMEM((c // 128, rows + SUBLANES, 128), F32), pltpu.VMEM((SUBLANES - 1, c // 128, rows, 128), F32)]


def _dwconv_fwd(name, glu, w_dw, b_dw, ln_g, ln_b):
    t, c = glu.shape
    tt = _pick(t, (256, 128))
    hb = tt // CONV_HALO

    def body(cur_ref, halo_ref, w_ref, b_ref, g_ref, be_ref, dw_ref, s_ref, win_ref, sh_ref):
        i = pl.program_id(0)
        halo = jnp.where(i > 0, halo_ref[...].astype(F32), 0.0)
        _fill_window(win_ref, sh_ref, [(0, halo), (CONV_HALO, cur_ref[...].astype(F32))])
        for r0 in range(0, tt, CONV_ROWS):
            for cb in range(c // 128):
                c0 = 128 * cb
                acc = jnp.zeros((CONV_ROWS, 128), F32) + b_ref[:, c0:c0 + 128]
                for k in range(CONV_WIDTH):
                    o = r0 + k + CONV_HALO - (CONV_WIDTH - 1)
                    acc = acc + w_ref[k:k + 1, c0:c0 + 128] * _window_rows(win_ref, sh_ref, o, cb)
                dw_ref[r0:r0 + CONV_ROWS, c0:c0 + 128] = acc
        u = dw_ref[...]
        mu = jnp.mean(u, axis=-1, keepdims=True)
        uc = u - mu
        rstd = lax.rsqrt(jnp.mean(uc * uc, axis=-1, keepdims=True) + EPS)
        z = uc * rstd * g_ref[...] + be_ref[...]
        s_ref[...] = (z * _sigmoid(z)).astype(BF16)

    big = pl.BlockSpec((tt, c), lambda i: (i, 0))
    row = pl.BlockSpec((1, c), lambda i: (0, 0))
    return pl.pallas_call(
        body, name=name, out_shape=[jax.ShapeDtypeStruct((t, c), F32), jax.ShapeDtypeStruct((t, c), BF16)],
        grid=(t // tt,),
        in_specs=[big, pl.BlockSpec((CONV_HALO, c), lambda i: (jnp.maximum(i * hb - 1, 0), 0)),
                  pl.BlockSpec((CONV_HALO, c), lambda i: (0, 0)), row, row, row],
        out_specs=[big, big], scratch_shapes=_window_scratch(tt + CONV_HALO, c),
        compiler_params=_params(("parallel",), V7X_VMEM_LIMIT_BYTES),
    )(glu, glu, w_dw, b_dw, ln_g, ln_b)


def _ln_silu_bwd(name, dx, w2, dw, ln_g, ln_b):
    t, c = dw.shape
    d_model = dx.shape[1]
    tr = _pick(t, (256, 128))

    def body(dx_ref, w_ref, dw_ref, g_ref, be_ref, o_ref, acc_ref):
        ds = lax.dot_general(dx_ref[...].astype(BF16), w_ref[...], (((1,), (1,)), ((), ())),
                             preferred_element_type=F32)
        u = dw_ref[...]
        mu = jnp.mean(u, axis=-1, keepdims=True)
        uc = u - mu
        rstd = lax.rsqrt(jnp.mean(uc * uc, axis=-1, keepdims=True) + EPS)
        xh = uc * rstd
        z = xh * g_ref[...] + be_ref[...]
        sg = _sigmoid(z)
        dz = ds * (sg * (1.0 + z * (1.0 - sg)))
        dxh = dz * g_ref[...]
        du = rstd * (dxh - jnp.mean(dxh, axis=-1, keepdims=True) - xh * jnp.mean(dxh * xh, axis=-1, keepdims=True))
        o_ref[...] = du

        @pl.when(pl.program_id(0) == 0)
        def _():
            acc_ref[...] = jnp.zeros_like(acc_ref)

        acc_ref[0:1, :] += jnp.sum(dz * xh, axis=0, keepdims=True)
        acc_ref[1:2, :] += jnp.sum(dz, axis=0, keepdims=True)
        acc_ref[2:3, :] += jnp.sum(du, axis=0, keepdims=True)

    big = pl.BlockSpec((tr, c), lambda i: (i, 0))
    row = pl.BlockSpec((1, c), lambda i: (0, 0))
    return pl.pallas_call(
        body, name=name, out_shape=[jax.ShapeDtypeStruct((t, c), F32), jax.ShapeDtypeStruct((8, c), F32)],
        grid=(t // tr,),
        in_specs=[pl.BlockSpec((tr, d_model), lambda i: (i, 0)), pl.BlockSpec((c, d_model), lambda i: (0, 0)),
                  big, row, row],
        out_specs=[big, pl.BlockSpec((8, c), lambda i: (0, 0))],
        compiler_params=_params(("arbitrary",), V7X_VMEM_LIMIT_BYTES),
    )(dx, w2, dw, ln_g, ln_b)


def _dwconv_bwd(name, ddw, a, gt, w_dw):
    t, c = ddw.shape
    tt = _pick(t, (256, 128))
    hb = tt // CONV_HALO
    last = t // tt - 1
    back = CONV_WIDTH - 1

    def body(d_ref, dn_ref, a_ref, ap_ref, g_ref, gp_ref, w_ref, du_ref, dwk_ref, db_ref,
             wd_ref, shd_ref, wg_ref, shg_ref, dg_ref, dwk8_ref):
        i = pl.program_id(0)
        _fill_window(wd_ref, shd_ref, [(0, d_ref[...]), (tt, jnp.where(i < last, dn_ref[...], 0.0))])
        glu_prev = ap_ref[...].astype(F32) * _sigmoid(gp_ref[...].astype(F32))
        av = a_ref[...].astype(F32)
        sg = _sigmoid(g_ref[...].astype(F32))
        _fill_window(wg_ref, shg_ref, [(0, jnp.where(i > 0, glu_prev, 0.0)), (CONV_HALO, av * sg)])

        @pl.when(i == 0)
        def _():
            dwk8_ref[...] = jnp.zeros_like(dwk8_ref)
            db_ref[...] = jnp.zeros_like(db_ref)

        for r0 in range(0, tt, CONV_ROWS):
            for cb in range(c // 128):
                c0 = 128 * cb
                dcur = wd_ref[cb, r0:r0 + CONV_ROWS, :]
                acc = jnp.zeros((CONV_ROWS, 128), F32)
                for k in range(CONV_WIDTH):
                    acc = acc + w_ref[k:k + 1, c0:c0 + 128] * _window_rows(wd_ref, shd_ref, r0 + back - k, cb)
                    p = dcur * _window_rows(wg_ref, shg_ref, r0 + k + CONV_HALO - back, cb)
                    s8 = p[0:SUBLANES]
                    for q in range(SUBLANES, CONV_ROWS, SUBLANES):
                        s8 = s8 + p[q:q + SUBLANES]
                    dwk8_ref[SUBLANES * k:SUBLANES * (k + 1), c0:c0 + 128] += s8
                dg_ref[r0:r0 + CONV_ROWS, c0:c0 + 128] = acc

        @pl.when(i == last)
        def _():
            for k in range(CONV_WIDTH):
                dwk_ref[k:k + 1, :] = jnp.sum(dwk8_ref[SUBLANES * k:SUBLANES * (k + 1), :], axis=0, keepdims=True)
            dwk_ref[CONV_WIDTH:, :] = jnp.zeros((CONV_HALO - CONV_WIDTH, c), F32)
        dglu = dg_ref[...]
        da = dglu * sg
        dgate = dglu * av * sg * (1.0 - sg)
        du_ref[:, 0:c] = da.astype(BF16)
        du_ref[:, c:] = dgate.astype(BF16)
        db_ref[:, 0:c] += jnp.sum(da, axis=0, keepdims=True)
        db_ref[:, c:] += jnp.sum(dgate, axis=0, keepdims=True)

    big = pl.BlockSpec((tt, c), lambda i: (i, 0))
    prev = pl.BlockSpec((CONV_HALO, c), lambda i: (jnp.maximum(i * hb - 1, 0), 0))
    nxt = pl.BlockSpec((CONV_HALO, c), lambda i: (jnp.minimum((i + 1) * hb, t // CONV_HALO - 1), 0))
    return pl.pallas_call(
        body, name=name,
        out_shape=[jax.ShapeDtypeStruct((t, 2 * c), BF16), jax.ShapeDtypeStruct((CONV_HALO, c), F32),
                   jax.ShapeDtypeStruct((1, 2 * c), F32)],
        grid=(t // tt,),
        in_specs=[big, nxt, big, prev, big, prev, pl.BlockSpec((CONV_HALO, c), lambda i: (0, 0))],
        out_specs=[pl.BlockSpec((tt, 2 * c), lambda i: (i, 0)), pl.BlockSpec((CONV_HALO, c), lambda i: (0, 0)),
                   pl.BlockSpec((1, 2 * c), lambda i: (0, 0))],
        scratch_shapes=_window_scratch(tt + CONV_HALO, c) + _window_scratch(tt + CONV_HALO, c)
        + [pltpu.VMEM((tt, c), F32), pltpu.VMEM((SUBLANES * CONV_HALO, c), F32)],
        compiler_params=_params(("arbitrary",), V7X_VMEM_LIMIT_BYTES),
    )(ddw, ddw, a, a, gt, gt, w_dw)


def _bucket_tables():
    i = np.arange(N_BACK)[:, None]
    j = np.arange(2 * N_BACK)[None, :]
    dist = i + N_BACK - j
    valid = (dist >= 0) & (dist <= N_BACK)
    max_exact = N_BUCKETS // 2
    out = []
    for d in DILATIONS:
        n = np.maximum(dist * d, 0)
        nf = np.maximum(n, 1).astype(np.float32)
        large = max_exact + (np.log(nf / np.float32(max_exact)) / np.float32(math.log(REL_MAX_DISTANCE / max_exact))
                             * np.float32(N_BUCKETS - max_exact)).astype(np.int32)
        large = np.minimum(large, N_BUCKETS - 1)
        out.append(np.where(valid, np.where(n < max_exact, n, large), -1))
    return np.stack(out).astype(np.int32)


def _bias_build(name, rel_bias, buckets):
    def body(tbl_ref, bk_ref, o_ref):
        g = pl.program_id(0)
        bk = bk_ref[0]
        for h in range(HEADS_PER_GROUP):
            acc = jnp.zeros(bk.shape, F32)
            for b in range(N_BUCKETS):
                acc = jnp.where(bk == b, tbl_ref[b, g * HEADS_PER_GROUP + h], acc)
            o_ref[h] = jnp.where(bk < 0, NEG_INF, acc)

    return pl.pallas_call(
        body, name=name, out_shape=jax.ShapeDtypeStruct((N_HEADS, N_BACK, 2 * N_BACK), F32), grid=(N_GROUPS,),
        in_specs=[pl.BlockSpec(memory_space=pltpu.SMEM), pl.BlockSpec((1, N_BACK, 2 * N_BACK), lambda g: (g, 0, 0))],
        out_specs=pl.BlockSpec((HEADS_PER_GROUP, N_BACK, 2 * N_BACK), lambda g: (g, 0, 0)),
        compiler_params=_params(("arbitrary",)),
    )(rel_bias, buckets)


def _bias_grad(name, dbs, buckets):
    nd = len(dbs)

    def body(*refs):
        bk = refs[nd][0]
        o_ref = refs[nd + 1]
        lane = lax.broadcasted_iota(jnp.int32, (1, 128), 1)
        db = [sum(r[h] for r in refs[:nd]) for h in range(HEADS_PER_GROUP)]
        for b in range(N_BUCKETS):
            row = jnp.zeros((1, 128), F32)
            for h in range(HEADS_PER_GROUP):
                s = jnp.sum(jnp.where(bk == b, db[h], 0.0), axis=0, keepdims=True)
                s = jnp.sum(s, axis=1, keepdims=True)
                row = jnp.where(lane // 32 == h, s, row)
            o_ref[0, b:b + 1, :] = row

    spec = pl.BlockSpec((HEADS_PER_GROUP, N_BACK, 2 * N_BACK), lambda g: (g, 0, 0))
    return pl.pallas_call(
        body, name=name, out_shape=jax.ShapeDtypeStruct((N_GROUPS, N_BUCKETS, 128), F32), grid=(N_GROUPS,),
        in_specs=[spec] * nd + [pl.BlockSpec((1, N_BACK, 2 * N_BACK), lambda g: (g, 0, 0))],
        out_specs=pl.BlockSpec((1, N_BUCKETS, 128), lambda g: (g, 0, 0)), compiler_params=_params(("arbitrary",)),
    )(*dbs, buckets)


def _head_cols(h):
    return slice(h * HEAD_DIM, (h + 1) * HEAD_DIM)


def _store_dilated(o_ref, tile_ref, val, d):
    nlb, rows, _ = tile_ref.shape
    width = 128 * nlb
    for lb in range(nlb):
        tile_ref[lb] = val[:, 128 * lb:128 * (lb + 1)]
    for r in range(d):
        for lb in range(nlb):
            c0 = r * width + 128 * lb
            o_ref[:, c0:c0 + 128] = tile_ref[lb, pl.ds(r, rows // d, stride=d), :].astype(o_ref.dtype)


def _attn_qkv_fwd(name, h, wq):
    t, k = h.shape
    tm = _pick(t, (512, 256))
    width = wq.shape[1]

    def body(h_ref, w0, w1, w2, o0, o1, o2, tile_ref):
        hv = h_ref[...]
        for g, (w_ref, o_ref) in enumerate(zip((w0, w1, w2), (o0, o1, o2))):
            acc = lax.dot_general(hv, w_ref[...], (((1,), (1,)), ((), ())), preferred_element_type=F32)
            if DILATIONS[g] == 1:
                o_ref[...] = acc.astype(BF16)
            else:
                _store_dilated(o_ref, tile_ref, acc, DILATIONS[g])

    return pl.pallas_call(
        body, name=name,
        out_shape=[jax.ShapeDtypeStruct((t // d, d * width), BF16) for d in DILATIONS], grid=(t // tm,),
        in_specs=[pl.BlockSpec((tm, k), lambda i: (i, 0))] + [pl.BlockSpec((width, k), lambda i: (0, 0))] * 3,
        out_specs=[pl.BlockSpec((tm // d, d * width), lambda i: (i, 0)) for d in DILATIONS],
        scratch_shapes=[pltpu.VMEM((width // 128, tm, 128), F32)],
        compiler_params=_params(("parallel",), V7X_VMEM_LIMIT_BYTES),
    )(h, wq[0], wq[1], wq[2])


def _attn_fwd(name, qkv, bias, g):
    d = DILATIONS[g]
    tq = qkv.shape[0]
    t = tq * d
    nblk = 3
    scale = HEAD_DIM ** -0.5

    def body(q_ref, kp_ref, kc_ref, vp_ref, vc_ref, b_ref, o_ref, l_ref):
        m2 = pl.program_id(1)
        col = lax.broadcasted_iota(jnp.int32, (N_BACK, 2 * N_BACK), 1)
        lane = lax.broadcasted_iota(jnp.int32, (N_BACK, 128), 1)
        for sub in range(2):
            rows = slice(N_BACK * sub, N_BACK * (sub + 1))
            lse_tile = jnp.zeros((N_BACK, 128), F32)
            outs = []
            for h in range(HEADS_PER_GROUP):
                hc = _head_cols(h)
                if sub == 0:
                    kk = jnp.concatenate([kp_ref[:, hc], kc_ref[0:N_BACK, hc]], axis=0)
                    vv = jnp.concatenate([vp_ref[:, hc], vc_ref[0:N_BACK, hc]], axis=0)
                else:
                    kk, vv = kc_ref[:, hc], vc_ref[:, hc]
                s = lax.dot_general(q_ref[rows, hc], kk, (((1,), (1,)), ((), ())), preferred_element_type=F32)
                s = s * scale + b_ref[h]
                if sub == 0:
                    s = jnp.where((col >= N_BACK) | (m2 > 0), s, NEG_INF)
                m = jnp.max(s, axis=-1, keepdims=True)
                p = jnp.exp(s - m)
                den = jnp.sum(p, axis=-1, keepdims=True)
                outs.append(jnp.dot(p.astype(BF16), vv, preferred_element_type=F32) / den)
                lse_tile = jnp.where(lane // 32 == h, m + jnp.log(den), lse_tile)
            o_ref[rows, :] = jnp.concatenate(outs, axis=1)
            l_ref[rows, :] = lse_tile

    def blk(part, prev):
        if prev:
            return pl.BlockSpec((N_BACK, GROUP_COLS), lambda r, n: (jnp.maximum(2 * n - 1, 0), r * nblk + part))
        return pl.BlockSpec((2 * N_BACK, GROUP_COLS), lambda r, n: (n, r * nblk + part))

    o, l = pl.pallas_call(
        body, name=name,
        out_shape=[jax.ShapeDtypeStruct((tq, d * GROUP_COLS), F32), jax.ShapeDtypeStruct((tq, d * 128), F32)],
        grid=(d, tq // (2 * N_BACK)),
        in_specs=[blk(0, False), blk(1, True), blk(1, False), blk(2, True), blk(2, False),
                  pl.BlockSpec((HEADS_PER_GROUP, N_BACK, 2 * N_BACK), lambda r, n: (g, 0, 0))],
        out_specs=[pl.BlockSpec((2 * N_BACK, GROUP_COLS), lambda r, n: (n, r)),
                   pl.BlockSpec((2 * N_BACK, 128), lambda r, n: (n, r))],
        compiler_params=_params(("parallel", "parallel")),
    )(qkv, qkv, qkv, qkv, qkv, bias)
    return o, l


def _group_weights(l_refs, h):
    ls = [l_ref[:, 32 * h:32 * h + 1] for l_ref in l_refs]
    m = jnp.maximum(jnp.maximum(ls[0], ls[1]), ls[2])
    es = [jnp.exp(l - m) for l in ls]
    tot = es[0] + es[1] + es[2]
    return [e / tot for e in es]


def _load_dilated(v_ref, n_ref, tile_ref, d):
    nlb, rows, _ = tile_ref.shape
    width = 128 * nlb
    for r in range(d):
        for lb in range(nlb):
            c0 = r * width + 128 * lb
            tile_ref[lb, pl.ds(r, rows // d, stride=d), :] = v_ref[:, c0:c0 + 128]
    n_ref[...] = jnp.concatenate([tile_ref[lb] for lb in range(nlb)], axis=1)


def _attn_merge_out(name, os_, ls, wot, x, gain):
    t, d_model = x.shape
    tr = _pick(t, (512, 256))

    def body(o0, o1v, o2v, l0, l1v, l2v, w_ref, x_ref, g_ref, om_ref, out_ref, h_ref, o1, o2, l1, l2,
             otile_ref, ltile_ref):
        _load_dilated(o1v, o1, otile_ref, DILATIONS[1])
        _load_dilated(o2v, o2, otile_ref, DILATIONS[2])
        _load_dilated(l1v, l1, ltile_ref, DILATIONS[1])
        _load_dilated(l2v, l2, ltile_ref, DILATIONS[2])
        o_refs = (o0, o1, o2)
        pieces = [[None] * HEADS_PER_GROUP for _ in range(N_GROUPS)]
        for h in range(HEADS_PER_GROUP):
            al = _group_weights((l0, l1, l2), h)
            for g in range(N_GROUPS):
                pieces[g][h] = o_refs[g][:, _head_cols(h)] * al[g]
        om = jnp.concatenate([p for row in pieces for p in row], axis=1).astype(BF16)
        om_ref[...] = om
        out = x_ref[...] + lax.dot_general(om, w_ref[...], (((1,), (1,)), ((), ())), preferred_element_type=F32)
        out_ref[...] = out
        h_ref[...] = _rms_rows(out, g_ref[...])

    so = pl.BlockSpec((tr, GROUP_COLS), lambda i: (i, 0))
    sl = pl.BlockSpec((tr, 128), lambda i: (i, 0))
    sx = pl.BlockSpec((tr, d_model), lambda i: (i, 0))
    views = [pl.BlockSpec((tr // d, d * GROUP_COLS), lambda i: (i, 0)) for d in DILATIONS]
    views += [pl.BlockSpec((tr // d, d * 128), lambda i: (i, 0)) for d in DILATIONS]
    res = pl.pallas_call(
        body, name=name,
        out_shape=[jax.ShapeDtypeStruct((t, D_ATTN), BF16), jax.ShapeDtypeStruct((t, d_model), F32),
                   jax.ShapeDtypeStruct((t, d_model), BF16)]
        + [jax.ShapeDtypeStruct((t, GROUP_COLS), F32)] * 2 + [jax.ShapeDtypeStruct((t, 128), F32)] * 2,
        grid=(t // tr,),
        in_specs=views + [pl.BlockSpec((d_model, D_ATTN), lambda i: (0, 0)), sx,
                          pl.BlockSpec((1, d_model), lambda i: (0, 0))],
        out_specs=[pl.BlockSpec((tr, D_ATTN), lambda i: (i, 0)), sx, sx, so, so, sl, sl],
        scratch_shapes=[pltpu.VMEM((GROUP_COLS // 128, tr, 128), F32), pltpu.VMEM((1, tr, 128), F32)],
        compiler_params=_params(("parallel",), V7X_VMEM_LIMIT_BYTES),
    )(*os_, *ls, wot, x, gain)
    om, x_new, h_next, o1, o2, l1, l2 = res
    return om, x_new, h_next, [os_[0], o1, o2], [ls[0], l1, l2]


def _attn_bwd_prep(name, dx, wot, os_, ls):
    t, d_model = dx.shape
    tr = _pick(t, (512, 256))

    def body(dx_ref, w_ref, o0, o1, o2, l0, l1, l2, d0, d1, d2, c0, c1, c2, dtile_ref, ctile_ref):
        o_refs, d_refs, c_refs = (o0, o1, o2), (d0, d1, d2), (c0, c1, c2)
        d_out = jnp.dot(dx_ref[...].astype(BF16), w_ref[...], preferred_element_type=F32)
        lane = lax.broadcasted_iota(jnp.int32, (tr, 128), 1)
        dos = [[None] * HEADS_PER_GROUP for _ in range(N_GROUPS)]
        cs = [jnp.zeros((tr, 128), F32) for _ in range(N_GROUPS)]
        for h in range(HEADS_PER_GROUP):
            al = _group_weights((l0, l1, l2), h)
            tot = jnp.zeros((tr, 1), F32)
            for g in range(N_GROUPS):
                dv = d_out[:, g * GROUP_COLS + h * HEAD_DIM:g * GROUP_COLS + (h + 1) * HEAD_DIM]
                tot = tot + al[g] * jnp.sum(dv * o_refs[g][:, _head_cols(h)], axis=-1, keepdims=True)
                dos[g][h] = dv * al[g]
            for g in range(N_GROUPS):
                cs[g] = jnp.where(lane // 32 == h, -al[g] * tot, cs[g])
        for g in range(N_GROUPS):
            do_g = jnp.concatenate(dos[g], axis=1)
            if DILATIONS[g] == 1:
                d_refs[g][...] = do_g.astype(BF16)
                c_refs[g][...] = cs[g]
            else:
                _store_dilated(d_refs[g], dtile_ref, do_g, DILATIONS[g])
                _store_dilated(c_refs[g], ctile_ref, cs[g], DILATIONS[g])

    so = pl.BlockSpec((tr, GROUP_COLS), lambda i: (i, 0))
    sl = pl.BlockSpec((tr, 128), lambda i: (i, 0))
    res = pl.pallas_call(
        body, name=name,
        out_shape=[jax.ShapeDtypeStruct((t // d, d * GROUP_COLS), BF16) for d in DILATIONS]
        + [jax.ShapeDtypeStruct((t // d, d * 128), F32) for d in DILATIONS],
        grid=(t // tr,),
        in_specs=[pl.BlockSpec((tr, d_model), lambda i: (i, 0)), pl.BlockSpec((d_model, D_ATTN), lambda i: (0, 0))]
        + [so] * 3 + [sl] * 3,
        out_specs=[pl.BlockSpec((tr // d, d * GROUP_COLS), lambda i: (i, 0)) for d in DILATIONS]
        + [pl.BlockSpec((tr // d, d * 128), lambda i: (i, 0)) for d in DILATIONS],
        scratch_shapes=[pltpu.VMEM((GROUP_COLS // 128, tr, 128), F32), pltpu.VMEM((1, tr, 128), F32)],
        compiler_params=_params(("parallel",), V7X_VMEM_LIMIT_BYTES),
    )(dx, wot, *os_, *ls)
    return res[:3], res[3:]


def _attn_bwd(name, qkv, do, lse, cterm, bias, g):
    d = DILATIONS[g]
    tq = qkv.shape[0]
    t = tq * d
    nb = tq // N_BACK
    nblk = 3
    scale = HEAD_DIM ** -0.5
    nt = (((1,), (1,)), ((), ()))
    tn = (((0,), (0,)), ((), ()))

    def body(q2, qx, kp, k2, vp, v2, do2, dox, l2, lx, c2, cx, b_ref, dqkv_ref, db_ref):
        m2 = pl.program_id(1)

        @pl.when((m2 == 0) & (pl.program_id(0) == 0))
        def _():
            db_ref[...] = jnp.zeros_like(db_ref)

        row2 = lax.broadcasted_iota(jnp.int32, (2 * N_BACK, N_BACK), 0)
        lo, hi = slice(0, N_BACK), slice(N_BACK, 2 * N_BACK)
        for sub in range(2):
            rows = hi if sub else lo
            has_prev = True if sub else m2 > 0
            has_next = 2 * m2 + 2 < nb if sub else True
            on_ac = (row2 < N_BACK) | has_next
            dqs, dks, dvs = [], [], []
            for h in range(HEADS_PER_GROUP):
                hc = _head_cols(h)
                st = slice(32 * h, 32 * h + 1)
                b_prev, b_same = b_ref[h, :, 0:N_BACK], b_ref[h, :, N_BACK:]
                q0, k1, v1, d0 = q2[rows, hc], k2[rows, hc], v2[rows, hc], do2[rows, hc]
                l0, c0 = l2[rows, st], c2[rows, st]
                if sub:
                    k0, v0 = k2[lo, hc], v2[lo, hc]
                    q_ac = jnp.concatenate([q0, qx[:, hc]], axis=0)
                    d_ac = jnp.concatenate([d0, dox[:, hc]], axis=0)
                    l_ac = jnp.concatenate([l0, lx[:, st]], axis=0)
                    c_ac = jnp.concatenate([c0, cx[:, st]], axis=0)
                else:
                    k0, v0 = kp[:, hc], vp[:, hc]
                    q_ac, d_ac, l_ac, c_ac = q2[:, hc], do2[:, hc], l2[:, st], c2[:, st]
                s_ac = (lax.dot_general(q_ac, k1, nt, preferred_element_type=F32) * scale
                        + jnp.concatenate([b_same, b_prev], axis=0))
                p_ac = jnp.where(on_ac, jnp.exp(s_ac - l_ac), 0.0)
                ds_ac = p_ac * (lax.dot_general(d_ac, v1, nt, preferred_element_type=F32) + c_ac)
                s_b = lax.dot_general(q0, k0, nt, preferred_element_type=F32) * scale + b_prev
                p_b = jnp.where(has_prev, jnp.exp(s_b - l0), 0.0)
                ds_b = p_b * (lax.dot_general(d0, v0, nt, preferred_element_type=F32) + c0)
                ds_a = ds_ac[0:N_BACK]
                dqs.append(scale * jnp.dot(jnp.concatenate([ds_b, ds_a], axis=1).astype(BF16),
                                           jnp.concatenate([k0, k1], axis=0), preferred_element_type=F32))
                dks.append(scale * lax.dot_general(ds_ac.astype(BF16), q_ac, tn, preferred_element_type=F32))
                dvs.append(lax.dot_general(p_ac.astype(BF16), d_ac, tn, preferred_element_type=F32))
                db_ref[h, :, 0:N_BACK] += ds_b
                db_ref[h, :, N_BACK:] += ds_a
            dqkv_ref[rows, :] = jnp.concatenate(dqs + dks + dvs, axis=1).astype(BF16)

    def blk(width, which, col):
        if which == "prev":
            return pl.BlockSpec((N_BACK, width), lambda r, n: (jnp.maximum(2 * n - 1, 0), col(r)))
        if which == "next":
            return pl.BlockSpec((N_BACK, width), lambda r, n: (jnp.minimum(2 * n + 2, nb - 1), col(r)))
        return pl.BlockSpec((2 * N_BACK, width), lambda r, n: (n, col(r)))

    def qkv_blk(part, which):
        return blk(GROUP_COLS, which, lambda r: r * nblk + part)

    def grp_blk(width, which):
        return blk(width, which, lambda r: r)

    qv, dov, lv, cv = qkv, do, lse, cterm
    dqkv_g, db = pl.pallas_call(
        body, name=name,
        out_shape=[jax.ShapeDtypeStruct((tq, d * 3 * GROUP_COLS), BF16),
                   jax.ShapeDtypeStruct((HEADS_PER_GROUP, N_BACK, 2 * N_BACK), F32)],
        grid=(d, nb // 2),
        in_specs=[qkv_blk(0, "same"), qkv_blk(0, "next"), qkv_blk(1, "prev"), qkv_blk(1, "same"),
                  qkv_blk(2, "prev"), qkv_blk(2, "same"), grp_blk(GROUP_COLS, "same"), grp_blk(GROUP_COLS, "next"),
                  grp_blk(128, "same"), grp_blk(128, "next"), grp_blk(128, "same"), grp_blk(128, "next"),
                  pl.BlockSpec((HEADS_PER_GROUP, N_BACK, 2 * N_BACK), lambda r, n: (g, 0, 0))],
        out_specs=[pl.BlockSpec((2 * N_BACK, 3 * GROUP_COLS), lambda r, n: (n, r)),
                   pl.BlockSpec((HEADS_PER_GROUP, N_BACK, 2 * N_BACK), lambda r, n: (0, 0, 0))],
        compiler_params=_params(("arbitrary", "arbitrary")),
    )(qv, qv, qv, qv, qv, qv, dov, dov, lv, lv, cv, cv, bias)
    return dqkv_g.reshape(t, 3 * GROUP_COLS), db


def _row(v):
    return v.reshape(1, -1)


def _glu_epi(accs, extras, rows):
    a = (accs[0] + rows[0]).astype(BF16)
    gt = (accs[1] + rows[1]).astype(BF16)
    return a, gt, a.astype(F32) * _sigmoid(gt.astype(F32))


def _swiglu_epi(accs, extras, rows):
    gq, uq = accs[0].astype(BF16), accs[1].astype(BF16)
    gf = gq.astype(F32)
    return gq, uq, gf * _sigmoid(gf) * uq.astype(F32)


def _group_rows(w):
    parts = [w[p * D_ATTN:(p + 1) * D_ATTN].reshape(N_GROUPS, GROUP_COLS, -1) for p in range(3)]
    return jnp.concatenate(parts, axis=1)


def _ungroup_rows(wg):
    return jnp.concatenate([wg[g][p * GROUP_COLS:(p + 1) * GROUP_COLS] for p in range(3) for g in range(N_GROUPS)],
                           axis=0)


def _local_step(x, target, sm, depth, fetch, emit):
    d_model = x.shape[1]
    buckets = jnp.asarray(_bucket_tables())
    bias = _bias_build("bias_build", sm["rel_bias"], buckets)
    saved = []
    h = _rmsnorm_fwd("rms_mix_fwd0", x, _row(sm["norm_mix"][0]))
    for i in range(depth):
        j = i // 2
        rec = {"x_mix": x}
        wm = fetch(2 * i, x)
        rec.update(h_mix=h, wm=wm)
        ffn_gain = _row(sm["norm_ffn"][i])
        if i % 2 == 0:
            c = wm["w1t"].shape[0] // 2
            tn = _pick(c, (1024, 512, 256, 128))
            b1 = sm["conv_b_pw1"][j]
            a, gt, glu = _mm_nt(f"conv_pw1_fwd{j}", h, [wm["w1t"]] * 2, [0, c // tn], c, _glu_epi, (BF16,) * 3,
                                rows=[_row(b1[:c]), _row(b1[c:])], tn=tn)
            wm.update(fetch(2 * i, a, 1))
            dw, s = _dwconv_fwd(f"dwconv_fwd{j}", glu, wm["wdw"], _row(sm["conv_b_dw"][j]),
                                _row(sm["conv_ln_g"][j]), _row(sm["conv_ln_b"][j]))
            x, h2 = _project_residual(f"conv_pw2_fwd{j}", s, wm["w2"], x, bias=_row(sm["conv_b_pw2"][j]),
                                      gain=ffn_gain)
            rec.update(a=a, gt=gt, dw=dw, s=s)
        else:
            wq = _group_rows(wm["wqkvt"])
            qkv = _attn_qkv_fwd(f"attn_qkv_fwd{j}", h, wq)
            og = [_attn_fwd(f"attn_fwd{j}_{g}", qkv[g], bias, g) for g in range(N_GROUPS)]
            om, x, h2, os_, ls = _attn_merge_out(f"attn_out_fwd{j}", [o for o, _ in og], [l for _, l in og],
                                                 wm["wot"], x, ffn_gain)
            rec.update(qkv=qkv, os=os_, ls=ls, lse_views=[l for _, l in og], om=om, wq=wq)
        rec["x_ffn"] = x
        wf = fetch(2 * i + 1, x)
        f = wf["wd"].shape[0]
        gq, uq, act = _mm_nt(f"ffn_up_fwd{i}", h2, [wf["wgt"], wf["wut"]], [0, 0], f, _swiglu_epi, (BF16,) * 3)
        x, h = _project_residual(f"ffn_down_fwd{i}", act, wf["wd"], x,
                                 gain=_row(sm["norm_mix"][i + 1]) if i + 1 < depth else None)
        rec.update(h_ffn=h2, gq=gq, uq=uq, wf=wf)
        saved.append(rec)

    dx, g_final, loss_cols = _loss_head("loss_head", x, _row(sm["final_norm"]), target)

    g_mix, g_ffn = [None] * depth, [None] * depth
    nconv = (depth + 1) // 2
    g_b1, g_bdw, g_lng, g_lnb, g_b2 = ([None] * nconv for _ in range(5))
    dbias = []
    for i in reversed(range(depth)):
        j = i // 2
        rec = saved[i]
        wm, wf = rec["wm"], rec["wf"]
        dgate, dup, dwd = _ffn_down_bwd(f"ffn_down_bwd{i}", dx, wf["wd"], rec["gq"], rec["uq"])
        gf = {"wd": dwd}
        gf["wgt"], gf["wut"] = _mm_tn_pair(f"ffn_gate_up_dw{i}", dgate, dup, rec["h_ffn"])
        dx, g_ffn[i] = _mm_nn_rms_bwd(f"ffn_up_bwd{i}", [dgate, dup], [wf["wgt"], wf["wut"]], rec["x_ffn"],
                                      _row(sm["norm_ffn"][i]), dx, tm=256)
        dx = emit(2 * i + 1, gf, dx)
        if i % 2 == 0:
            c = wm["w2"].shape[0]
            gm = {}
            gm["w2"], g_b2[j] = _mm_tn(f"conv_pw2_dw{j}", rec["s"], dx, colsum_b=True)
            ddw, sums = _ln_silu_bwd(f"conv_pw2_bwd{j}", dx, wm["w2"], rec["dw"], _row(sm["conv_ln_g"][j]),
                                     _row(sm["conv_ln_b"][j]))
            g_lng[j], g_lnb[j], g_bdw[j] = sums[0], sums[1], sums[2]
            du, dwk, db1 = _dwconv_bwd(f"dwconv_bwd{j}", ddw, rec["a"], rec["gt"], wm["wdw"])
            gm["wdw"] = dwk[:CONV_WIDTH]
            g_b1[j] = db1[0]
            gm["w1t"] = _mm_tn(f"conv_pw1_dw{j}", du, rec["h_mix"])
            dh_terms = ([du], [wm["w1t"]])
        else:
            gm = {"wot": _mm_tn(f"attn_out_dw{j}", dx, rec["om"])}
            dos, cs = _attn_bwd_prep(f"attn_out_bwd{j}", dx, wm["wot"], rec["os"], rec["ls"])
            back = [_attn_bwd(f"attn_bwd{j}_{g}", rec["qkv"][g], dos[g], rec["lse_views"][g], cs[g], bias, g)
                    for g in range(N_GROUPS)]
            dqkv = [b[0] for b in back]
            dbias.append(jnp.concatenate([b[1] for b in back], axis=0))
            dh_terms = (dqkv, [rec["wq"][g] for g in range(N_GROUPS)])
            gm["wqkvt"] = _ungroup_rows([_mm_tn(f"attn_qkv_dw{j}_{g}", dqkv[g], rec["h_mix"])
                                         for g in range(N_GROUPS)])
        dx, g_mix[i] = _mm_nn_rms_bwd(f"mix_in_bwd{i}", dh_terms[0], dh_terms[1], rec["x_mix"],
                                      _row(sm["norm_mix"][i]), dx)
        dx = emit(2 * i, gm, dx)

    gb = _bias_grad("bias_grad", dbias, buckets)
    g_rel = jnp.transpose(gb[:, :, ::32], (1, 0, 2)).reshape(N_BUCKETS, N_HEADS)
    gsm = {
        "norm_mix": jnp.concatenate(g_mix, axis=0), "norm_ffn": jnp.concatenate(g_ffn, axis=0),
        "final_norm": g_final[0], "conv_b_pw1": jnp.stack(g_b1), "conv_b_dw": jnp.stack(g_bdw),
        "conv_ln_g": jnp.stack(g_lng), "conv_ln_b": jnp.stack(g_lnb),
        "conv_b_pw2": jnp.concatenate(g_b2, axis=0), "rel_bias": g_rel,
    }
    return loss_cols, dx, gsm


SMALL = ("norm_mix", "norm_ffn", "final_norm", "conv_b_pw1", "conv_b_dw", "conv_ln_g", "conv_ln_b", "conv_b_pw2",
         "rel_bias")
SHARDED = (("conv_w_pw1", "w1t", True), ("conv_w_pw2", "w2", False), ("attn_w_qkv", "wqkvt", True),
           ("attn_w_o", "wot", True), ("ffn_w_gate", "wgt", True), ("ffn_w_up", "wut", True),
           ("ffn_w_down", "wd", False))
ORDER = ("norm_mix", "norm_ffn", "final_norm", "conv_w_pw1", "conv_b_pw1", "conv_w_dw", "conv_b_dw", "conv_ln_g",
         "conv_ln_b", "conv_w_pw2", "conv_b_pw2", "attn_w_qkv", "attn_w_o", "rel_bias", "ffn_w_gate", "ffn_w_up",
         "ffn_w_down")
PACK_LANES = 128
PACK_ROW_TILE = 8


def _pack_small(vals):
    flat = jnp.concatenate([vals[n].reshape(-1) for n in SMALL])
    per_tile = PACK_LANES * PACK_ROW_TILE
    return jnp.pad(flat, (0, -flat.shape[0] % per_tile)).reshape(-1, PACK_LANES)


def _unpack_small(pack, like):
    flat, out, pos = pack.reshape(-1), {}, 0
    for n in SMALL:
        out[n] = flat[pos:pos + like[n].size].reshape(like[n].shape)
        pos += like[n].size
    return out


def _dw_blocks(w):
    l, k, c = w.shape
    blk = jnp.transpose(w.reshape(l, k, N_DEV, c // N_DEV), (2, 0, 1, 3)).reshape(N_DEV, l * k, c // N_DEV)
    return jnp.pad(blk, ((0, 0), (0, -(l * k) % 8), (0, 0)))


def kernel(x, norm_mix, norm_ffn, final_norm, conv_w_pw1, conv_b_pw1, conv_w_dw, conv_b_dw, conv_ln_g, conv_ln_b, conv_w_pw2, conv_b_pw2, attn_w_qkv, attn_w_o, rel_bias, ffn_w_gate, ffn_w_up, ffn_w_down, loss_target, m_norm_mix, m_norm_ffn, m_final_norm, m_conv_w_pw1, m_conv_b_pw1, m_conv_w_dw, m_conv_b_dw, m_conv_ln_g, m_conv_ln_b, m_conv_w_pw2, m_conv_b_pw2, m_attn_w_qkv, m_attn_w_o, m_rel_bias, m_ffn_w_gate, m_ffn_w_up, m_ffn_w_down, v_norm_mix, v_norm_ffn, v_final_norm, v_conv_w_pw1, v_conv_b_pw1, v_conv_w_dw, v_conv_b_dw, v_conv_ln_g, v_conv_ln_b, v_conv_w_pw2, v_conv_b_pw2, v_attn_w_qkv, v_attn_w_o, v_rel_bias, v_ffn_w_gate, v_ffn_w_up, v_ffn_w_down):
    w = dict(norm_mix=norm_mix, norm_ffn=norm_ffn, final_norm=final_norm, conv_w_pw1=conv_w_pw1,
             conv_b_pw1=conv_b_pw1, conv_w_dw=conv_w_dw, conv_b_dw=conv_b_dw, conv_ln_g=conv_ln_g,
             conv_ln_b=conv_ln_b, conv_w_pw2=conv_w_pw2, conv_b_pw2=conv_b_pw2, attn_w_qkv=attn_w_qkv,
             attn_w_o=attn_w_o, rel_bias=rel_bias, ffn_w_gate=ffn_w_gate, ffn_w_up=ffn_w_up, ffn_w_down=ffn_w_down)
    m = dict(norm_mix=m_norm_mix, norm_ffn=m_norm_ffn, final_norm=m_final_norm, conv_w_pw1=m_conv_w_pw1,
             conv_b_pw1=m_conv_b_pw1, conv_w_dw=m_conv_w_dw, conv_b_dw=m_conv_b_dw, conv_ln_g=m_conv_ln_g,
             conv_ln_b=m_conv_ln_b, conv_w_pw2=m_conv_w_pw2, conv_b_pw2=m_conv_b_pw2, attn_w_qkv=m_attn_w_qkv,
             attn_w_o=m_attn_w_o, rel_bias=m_rel_bias, ffn_w_gate=m_ffn_w_gate, ffn_w_up=m_ffn_w_up,
             ffn_w_down=m_ffn_w_down)
    v = dict(norm_mix=v_norm_mix, norm_ffn=v_norm_ffn, final_norm=v_final_norm, conv_w_pw1=v_conv_w_pw1,
             conv_b_pw1=v_conv_b_pw1, conv_w_dw=v_conv_w_dw, conv_b_dw=v_conv_b_dw, conv_ln_g=v_conv_ln_g,
             conv_ln_b=v_conv_ln_b, conv_w_pw2=v_conv_w_pw2, conv_b_pw2=v_conv_b_pw2, attn_w_qkv=v_attn_w_qkv,
             attn_w_o=v_attn_w_o, rel_bias=v_rel_bias, ffn_w_gate=v_ffn_w_gate, ffn_w_up=v_ffn_w_up,
             ffn_w_down=v_ffn_w_down)

    me = 4 * lax.axis_index("x") + 2 * lax.axis_index("y") + lax.axis_index("c")
    depth = ffn_w_gate.shape[0]
    n_conv, _, cb = conv_w_dw.shape

    def sublayer(key, layer):
        if key in ("wgt", "wut", "wd"):
            return 2 * layer + 1
        return 4 * layer if key in ("w1t", "w2") else 4 * layer + 2

    def landing(block, own):
        land = lax.empty((N_DEV,) + block.shape, block.dtype)
        return lax.dynamic_update_slice(land, own[None], (me,) + (0,) * block.ndim)

    by_sub = {s: [] for s in range(2 * depth)}
    for name, key, cols in SHARDED:
        sw = (jnp.swapaxes(w[name], 1, 2) if cols else w[name]).astype(BF16)
        for layer in range(sw.shape[0]):
            by_sub[sublayer(key, layer)].append((key, layer, sw[layer]))
    likes = {s: jnp.zeros((sum(sh.size for _, _, sh in by_sub[s]) // 1024, 1024), BF16) for s in by_sub}
    dw_shard = jnp.pad(conv_w_dw.reshape(-1, cb), ((0, -(n_conv * CONV_WIDTH) % 8), (0, 0)))
    like_dw = jnp.zeros(dw_shard.shape, F32)
    stages = {}
    for s in range(2 * depth):
        for entry in by_sub[s]:
            stages.setdefault((s, 0 if entry[0] != "w2" else 1), []).append(entry)
    stage_order = sorted(stages)
    groups = [([sh for _, _, sh in stages[st]], [landing(sh, sh) for _, _, sh in stages[st]]) for st in stage_order]
    groups.insert(1, ([dw_shard], [landing(dw_shard, dw_shard)]))
    gather, _ = _exchange_start("gather_start", groups, scatter=False, carry=jnp.zeros((8, 128), F32))
    handle = dict(zip(stage_order, gather[:1] + gather[2:]))
    dw_filters = []

    def fetch(s, after, part=0):
        st = (s, part)
        like = jnp.zeros((sum(sh.size for _, _, sh in stages[st]) // 1024, 1024), BF16)
        pieces = [(handle[st], like)]
        if st == (0, 1):
            pieces.append((gather[1], like_dw))
        landed = _exchange_wait(f"gather_wait{s}_{part}", pieces, after)
        out = {key: g.reshape(g.shape[0] * g.shape[1], g.shape[2]) for (key, _, _), g in zip(stages[st], landed[0])}
        if st == (0, 1):
            dw_all = landed[1][0]
            full = jnp.transpose(dw_all[:, :n_conv * CONV_WIDTH].reshape(N_DEV, n_conv, CONV_WIDTH, cb), (1, 2, 0, 3))
            full = jnp.pad(full.reshape(n_conv, CONV_WIDTH, N_DEV * cb), ((0, 0), (0, CONV_HALO - CONV_WIDTH), (0, 0)))
            dw_filters.extend(full[layer] for layer in range(n_conv))
        if "w2" in out:
            out["wdw"] = dw_filters[s // 4]
        return out

    scatter, dw_grads, started = {}, {}, {}

    def emit(s, gd, carry):
        parts = [gd[key].reshape(N_DEV, -1, gd[key].shape[1]) for key, _, _ in by_sub[s]]
        if "wdw" in gd:
            dw_grads[s // 4] = gd["wdw"]
        if s == 0:
            parts.append(_dw_blocks(jnp.stack([dw_grads[layer] for layer in range(n_conv)])))
        lands = [landing(p[0], lax.dynamic_index_in_dim(p, me, 0, keepdims=False)) for p in parts]
        groups = [(parts[:len(by_sub[s])], lands[:len(by_sub[s])])]
        if s == 0:
            groups.append((parts[-1:], lands[-1:]))
        if s == 0:
            scatter[s], started[0] = _exchange_start(f"scatter_start{s}", groups, scatter=True,
                                                     carry=jnp.zeros((8, 128), F32))
            return carry
        scatter[s], carry = _exchange_start(f"scatter_start{s}", groups, scatter=True, carry=carry)
        return carry

    sm = {n: w[n] for n in SMALL}
    loss_cols, dx, gsm = _local_step(x[0], loss_target[0], sm, depth, fetch, emit)
    loss = lax.psum(jnp.sum(loss_cols), ("x", "y", "c"))

    grads, summed, delta, new_m, new_v = {}, {}, {}, {}, {}

    def reduce_pieces(subs, landed):
        for s, recv in zip(subs, landed):
            for (key, layer, _), r in zip(by_sub[s], recv):
                summed[key, layer] = _sum8(f"sum_{key}{layer}", r)

    def update(names):
        for name in names:
            shape = w[name].shape
            res = _adamw(f"adamw_{name}",
                         *[t.reshape(-1, shape[-1]) for t in (grads[name], w[name], m[name], v[name])])
            delta[name], new_m[name], new_v[name] = (t.reshape(shape) for t in res)

    def stacked(name, key, cols):
        g = jnp.stack([summed[key, layer] for layer in range(w[name].shape[0])])
        return jnp.swapaxes(g, 1, 2) if cols else g

    early = sorted((s for s in scatter if s != 0), reverse=True)
    reduce_pieces(early, _exchange_wait("scatter_wait_early", [(scatter[s][0], likes[s]) for s in early],
                                        started[0]))
    late_names = [name for name, key, _ in SHARDED if any(k == key for k, _, _ in by_sub[0])]
    for name, key, cols in SHARDED:
        if name not in late_names:
            grads[name] = stacked(name, key, cols)
    early_names = [name for name, _, _ in SHARDED if name not in late_names]
    update(early_names)
    pack = _pack_small(gsm)
    ((pack_all,),) = _exchange("gather_small_grads", [([pack], pack)], scatter=False)
    pack_sum = _sum8("sum_small", pack_all)
    grads.update(_unpack_small(pack_sum, sm))

    landed = _exchange_wait("scatter_wait_last", [(scatter[0][0], likes[0]), (scatter[0][1], like_dw)],
                            new_v[early_names[-1]])
    reduce_pieces([0], landed[:1])
    for name, key, cols in SHARDED:
        if name in late_names:
            grads[name] = stacked(name, key, cols)
    grads["conv_w_dw"] = _sum8("sum_wdw", landed[1][0])[:n_conv * CONV_WIDTH].reshape(conv_w_dw.shape)
    update(late_names + ["conv_w_dw"])
    res = _adamw("adamw_small", pack_sum, _pack_small(sm), _pack_small({n: m[n] for n in SMALL}),
                 _pack_small({n: v[n] for n in SMALL}))
    for dst, t in zip((delta, new_m, new_v), res):
        dst.update(_unpack_small(t, sm))

    outs = [loss, dx[None]]
    for d in (grads, delta, new_m, new_v):
        outs += [d[n] for n in ORDER]
    return tuple(outs)
```

```python
import functools
import math

import numpy as np
import jax
import jax.numpy as jnp
from jax import lax
from jax.experimental import pallas as pl
from jax.experimental.pallas import tpu as pltpu

F32 = jnp.float32
BF16 = jnp.bfloat16

N_DEV = 8
HEAD_DIM = 64
HEADS_PER_GROUP = 4
GROUP_COLS = HEADS_PER_GROUP * HEAD_DIM
DILATIONS = (1, 4, 16)
N_BACK = 128
N_GROUPS = 3
N_HEADS = 12
D_ATTN = 768
N_BUCKETS = 32
REL_MAX_DISTANCE = 2048
CONV_WIDTH = 31
CONV_HALO = 32
EPS = 1e-6
NEG_INF = -1e30
ADAM_LR, ADAM_B1, ADAM_B2, ADAM_EPS, ADAM_WD, ADAM_STEP = 0.001, 0.9, 0.999, 1e-08, 0.01, 10
V7X_VMEM_LIMIT_BYTES = 56 * 1024 * 1024
MESH = pl.DeviceIdType.MESH
ANY = pl.BlockSpec(memory_space=pl.ANY)


def _pick(n, prefs):
    for p in prefs:
        if n % p == 0:
            return p
    return n


def _params(sem, vmem=None):
    return pltpu.CompilerParams(dimension_semantics=sem, vmem_limit_bytes=vmem)


def _sigmoid(x):
    return 1.0 / (1.0 + jnp.exp(-x))


def _exchange(name, groups, scatter):
    n_arr = [len(arrs) for arrs, _ in groups]
    n_in = sum(n_arr) + len(groups)
    ng = len(groups)

    def body(*refs):
        ins, outs, (send_sems, recv_sems, local_sems) = refs[:n_in], refs[n_in:-3], refs[-3:]
        x, y, c = lax.axis_index("x"), lax.axis_index("y"), lax.axis_index("c")
        me = 4 * x + 2 * y + c
        pos_in = pos_out = 0
        plans = []
        for gi in range(ng):
            srcs = ins[pos_in:pos_in + n_arr[gi]]
            like = ins[pos_in + n_arr[gi]]
            dsts = outs[pos_out:pos_out + n_arr[gi]]
            pos_in += n_arr[gi] + 1
            pos_out += n_arr[gi]
            plans.append((gi, srcs, like, dsts))
        local = []
        for gi, srcs, like, dsts in plans:
            for s, d in zip(srcs, dsts):
                cp = pltpu.make_async_copy(s.at[me] if scatter else s, d.at[me], local_sems.at[gi])
                cp.start()
                local.append(cp)
        for delta in range(1, N_DEV):
            dx, dy, dc = (delta >> 2) & 1, (delta >> 1) & 1, delta & 1
            px, py, pc = (1 - x if dx else x), (1 - y if dy else y), (1 - c if dc else c)
            peer = 4 * px + 2 * py + pc
            for gi, srcs, like, dsts in plans:
                for s, d in zip(srcs, dsts):
                    pltpu.make_async_remote_copy(
                        src_ref=s.at[peer] if scatter else s, dst_ref=d.at[me],
                        send_sem=send_sems.at[gi, delta - 1], recv_sem=recv_sems.at[gi, delta - 1],
                        device_id=(px, py, pc), device_id_type=MESH).start()
        for delta in range(1, N_DEV):
            for gi, srcs, like, dsts in plans:
                pltpu.make_async_remote_copy(
                    src_ref=like, dst_ref=like, send_sem=send_sems.at[gi, delta - 1],
                    recv_sem=recv_sems.at[gi, delta - 1], device_id=(x, y, c), device_id_type=MESH).wait()
        for cp in local:
            cp.wait()

    operands, out_shape = [], []
    for arrs, like in groups:
        operands += list(arrs) + [like]
        for a in arrs:
            blk = a.shape[1:] if scatter else a.shape
            out_shape.append(jax.ShapeDtypeStruct((N_DEV,) + tuple(blk), a.dtype))
    outs = pl.pallas_call(
        body, name=name, out_shape=out_shape, in_specs=[ANY] * len(operands), out_specs=[ANY] * len(out_shape),
        scratch_shapes=[pltpu.SemaphoreType.DMA((ng, N_DEV - 1)), pltpu.SemaphoreType.DMA((ng, N_DEV - 1)),
                        pltpu.SemaphoreType.DMA((ng,))],
        compiler_params=pltpu.CompilerParams(has_side_effects=True),
    )(*operands)
    res, pos = [], 0
    for n in n_arr:
        res.append(list(outs[pos:pos + n]))
        pos += n
    return res


HBM = pl.BlockSpec(memory_space=pltpu.HBM)
SEM = pl.BlockSpec(memory_space=pltpu.SEMAPHORE)
EFFECT = pltpu.SideEffectType.DATAFLOW_SIDE_EFFECTING


def _in_hbm(a):
    return pltpu.with_memory_space_constraint(a, pltpu.HBM)


def _exchange_start(name, groups, scatter, carry):
    ns = [len(s) for s, _ in groups]
    n_in = 2 * sum(ns)

    def body(*refs):
        ins, outs = refs[:n_in], refs[n_in + 1:]
        x, y, c = lax.axis_index("x"), lax.axis_index("y"), lax.axis_index("c")
        me = 4 * x + 2 * y + c
        pi = po = 0
        for n in ns:
            srcs, lands = ins[pi:pi + n], ins[pi + n:pi + 2 * n]
            send_sems, recv_sems = outs[po], outs[po + 1]
            pi += 2 * n
            po += 2 + 2 * n
            for delta in range(1, N_DEV):
                dx, dy, dc = (delta >> 2) & 1, (delta >> 1) & 1, delta & 1
                px, py, pc = (1 - x if dx else x), (1 - y if dy else y), (1 - c if dc else c)
                peer = 4 * px + 2 * py + pc
                for s, d in zip(srcs, lands):
                    pltpu.make_async_remote_copy(
                        src_ref=s.at[peer] if scatter else s, dst_ref=d.at[me], send_sem=send_sems.at[delta - 1],
                        recv_sem=recv_sems.at[delta - 1], device_id=(px, py, pc), device_id_type=MESH).start()

    operands, out_shape, out_specs, aliases = [], [], [], {}
    for srcs, lands in groups:
        out_shape += [pltpu.SemaphoreType.DMA((N_DEV - 1,))] * 2
        out_specs += [SEM, SEM]
        for a in list(srcs) + list(lands):
            aliases[len(operands)] = len(out_shape)
            operands.append(_in_hbm(a))
            out_shape.append(pltpu.HBM(a.shape, a.dtype))
            out_specs.append(HBM)
    aliases[len(operands)] = len(out_shape)
    operands.append(_in_hbm(carry))
    out_shape.append(pltpu.HBM(carry.shape, carry.dtype))
    out_specs.append(HBM)
    outs = pl.pallas_call(
        body, name=name, out_shape=out_shape, in_specs=[HBM] * len(operands), out_specs=out_specs,
        input_output_aliases=aliases, compiler_params=pltpu.CompilerParams(has_side_effects=EFFECT),
    )(*operands)
    handles, po = [], 0
    for n in ns:
        handles.append((outs[po], outs[po + 1], list(outs[po + 2:po + 2 + n]), list(outs[po + 2 + n:po + 2 + 2 * n])))
        po += 2 + 2 * n
    return handles, outs[-1]


def _exchange_wait(name, pieces, after):
    ns = [len(h[2]) for h, _ in pieces]

    def body(*refs):
        x, y, c = lax.axis_index("x"), lax.axis_index("y"), lax.axis_index("c")
        pi = 0
        for n in ns:
            send_sems, recv_sems, like = refs[pi + 2 * n], refs[pi + 2 * n + 1], refs[pi + 2 * n + 2]
            pi += 2 * n + 3
            for delta in range(1, N_DEV):
                cp = pltpu.make_async_remote_copy(
                    src_ref=like, dst_ref=like, send_sem=send_sems.at[delta - 1], recv_sem=recv_sems.at[delta - 1],
                    device_id=(x, y, c), device_id_type=MESH)
                cp.wait_send()
                cp.wait_recv()

    operands, in_specs, out_shape, aliases = [], [], [], {}
    for (send_sems, recv_sems, srcs, lands), like in pieces:
        for a in srcs + lands:
            aliases[len(operands)] = len(out_shape)
            operands.append(a)
            in_specs.append(HBM)
            out_shape.append(pltpu.HBM(a.shape, a.dtype))
        operands += [send_sems, recv_sems, like]
        in_specs += [SEM, SEM, ANY]
    operands.append(after)
    in_specs.append(ANY)
    outs = pl.pallas_call(
        body, name=name, out_shape=out_shape, in_specs=in_specs, out_specs=[HBM] * len(out_shape),
        input_output_aliases=aliases, compiler_params=pltpu.CompilerParams(has_side_effects=EFFECT),
    )(*operands)
    res, po = [], 0
    for n in ns:
        res.append(list(outs[po + n:po + 2 * n]))
        po += 2 * n
    return res


def _sum8(name, r):
    _, rows, cols = r.shape
    tr = _pick(rows, (256, 128, 64, 32, 16, 8))

    def body(r_ref, o_ref):
        acc = r_ref[0].astype(F32)
        for p in range(1, N_DEV):
            acc = acc + r_ref[p].astype(F32)
        o_ref[...] = acc

    return pl.pallas_call(
        body, name=name, out_shape=jax.ShapeDtypeStruct((rows, cols), F32), grid=(rows // tr,),
        in_specs=[pl.BlockSpec((N_DEV, tr, cols), lambda i: (0, i, 0))],
        out_specs=pl.BlockSpec((tr, cols), lambda i: (i, 0)), compiler_params=_params(("parallel",)),
    )(r)


def _adamw(name, g, w, m, v):
    rows, cols = g.shape
    tr = _pick(rows, (512, 256, 128, 64, 32, 16, 8))
    c1 = 1.0 - ADAM_B1 ** ADAM_STEP
    c2 = 1.0 - ADAM_B2 ** ADAM_STEP

    def body(g_ref, w_ref, m_ref, v_ref, d_ref, nm_ref, nv_ref):
        gv = g_ref[...]
        nm = ADAM_B1 * m_ref[...] + (1.0 - ADAM_B1) * gv
        nv = ADAM_B2 * v_ref[...] + (1.0 - ADAM_B2) * (gv * gv)
        d_ref[...] = -ADAM_LR * ((nm / c1) / (jnp.sqrt(nv / c2) + ADAM_EPS) + ADAM_WD * w_ref[...])
        nm_ref[...] = nm
        nv_ref[...] = nv

    spec = pl.BlockSpec((tr, cols), lambda i: (i, 0))
    return pl.pallas_call(
        body, name=name, out_shape=[jax.ShapeDtypeStruct((rows, cols), F32)] * 3, grid=(rows // tr,),
        in_specs=[spec] * 4, out_specs=[spec] * 3, compiler_params=_params(("parallel",)),
    )(g, w, m, v)


def _mm_nt(name, a, ws, w_offs, n, epi, out_dtypes, extras=(), rows=(), tm=None, tn=None):
    m, k = a.shape
    tm = tm or _pick(m, (512, 256, 128))
    tn = tn or _pick(n, (1408, 1152, 1024, 512, 256, 128))
    nw, ne, nr = len(ws), len(extras), len(rows)

    def body(*refs):
        a_ref, w_refs = refs[0], refs[1:1 + nw]
        e_refs, r_refs = refs[1 + nw:1 + nw + ne], refs[1 + nw + ne:1 + nw + ne + nr]
        o_refs = refs[1 + nw + ne + nr:]
        av = a_ref[...].astype(BF16)
        accs = [lax.dot_general(av, w[...], (((1,), (1,)), ((), ())), preferred_element_type=F32) for w in w_refs]
        outs = epi(accs, [e[...] for e in e_refs], [r[...] for r in r_refs])
        for o_ref, o in zip(o_refs, outs):
            o_ref[...] = o.astype(o_ref.dtype)

    in_specs = [pl.BlockSpec((tm, k), lambda j, i: (i, 0))]
    in_specs += [pl.BlockSpec((tn, k), functools.partial(lambda j, i, off: (j + off, 0), off=off)) for off in w_offs]
    in_specs += [pl.BlockSpec((tm, tn), lambda j, i: (i, j))] * ne
    in_specs += [pl.BlockSpec((1, tn), lambda j, i: (0, j))] * nr
    return pl.pallas_call(
        body, name=name, out_shape=[jax.ShapeDtypeStruct((m, n), dt) for dt in out_dtypes],
        grid=(n // tn, m // tm), in_specs=in_specs,
        out_specs=[pl.BlockSpec((tm, tn), lambda j, i: (i, j))] * len(out_dtypes),
        compiler_params=_params(("parallel", "parallel"), V7X_VMEM_LIMIT_BYTES),
    )(a, *ws, *extras, *rows)


def _rms_rows(x, gain):
    r = lax.rsqrt(jnp.mean(x * x, axis=-1, keepdims=True) + EPS)
    return (x * r * gain).astype(BF16)


def _project_residual(name, a, b, x, bias=None, gain=None):
    m, k = a.shape
    n = b.shape[1]
    tm = _pick(m, (512, 256, 128))
    rows = [r for r in (bias, gain) if r is not None]

    def body(*refs):
        a_ref, b_ref, x_ref = refs[:3]
        r_refs = list(refs[3:3 + len(rows)])
        out = x_ref[...] + jnp.dot(a_ref[...], b_ref[...], preferred_element_type=F32)
        if bias is not None:
            out = out + r_refs.pop(0)[...]
        refs[3 + len(rows)][...] = out
        if gain is not None:
            refs[4 + len(rows)][...] = _rms_rows(out, r_refs.pop(0)[...])

    big = pl.BlockSpec((tm, n), lambda i: (i, 0))
    out_shape = [jax.ShapeDtypeStruct((m, n), F32)] + ([jax.ShapeDtypeStruct((m, n), BF16)] if gain is not None else [])
    res = pl.pallas_call(
        body, name=name, out_shape=out_shape, grid=(m // tm,),
        in_specs=[pl.BlockSpec((tm, k), lambda i: (i, 0)), pl.BlockSpec((k, n), lambda i: (0, 0)), big]
        + [pl.BlockSpec((1, n), lambda i: (0, 0))] * len(rows),
        out_specs=[big] * len(out_shape), compiler_params=_params(("parallel",), V7X_VMEM_LIMIT_BYTES),
    )(a, b, x, *rows)
    return res if gain is not None else (res[0], None)


def _mm_nn_rms_bwd(name, as_, bs, x, g, dx_out, tm=None):
    m, k = as_[0].shape
    n = bs[0].shape[1]
    tm = tm or _pick(m, (512, 256, 128))
    npair = len(as_)

    def body(*refs):
        a_refs, b_refs = refs[:npair], refs[npair:2 * npair]
        x_ref, g_ref, dxo_ref, dx_ref, dg_ref = refs[2 * npair:]
        dh = None
        for a_ref, b_ref in zip(a_refs, b_refs):
            p = jnp.dot(a_ref[...].astype(BF16), b_ref[...], preferred_element_type=F32)
            dh = p if dh is None else dh + p
        xv = x_ref[...]
        r = lax.rsqrt(jnp.mean(xv * xv, axis=-1, keepdims=True) + EPS)
        yv = xv * r
        dy = dh * g_ref[...]
        dx_ref[...] = dxo_ref[...] + r * (dy - yv * jnp.mean(dy * yv, axis=-1, keepdims=True))

        @pl.when(pl.program_id(0) == 0)
        def _():
            dg_ref[...] = jnp.zeros_like(dg_ref)

        dg_ref[...] += jnp.sum(dh * yv, axis=0, keepdims=True)

    big = pl.BlockSpec((tm, n), lambda i: (i, 0))
    row = pl.BlockSpec((1, n), lambda i: (0, 0))
    in_specs = [pl.BlockSpec((tm, k), lambda i: (i, 0))] * npair + [pl.BlockSpec((k, n), lambda i: (0, 0))] * npair
    return pl.pallas_call(
        body, name=name, out_shape=[jax.ShapeDtypeStruct((m, n), F32), jax.ShapeDtypeStruct((1, n), F32)],
        grid=(m // tm,), in_specs=in_specs + [big, row, big], out_specs=[big, row],
        compiler_params=_params(("arbitrary",), V7X_VMEM_LIMIT_BYTES),
    )(*as_, *bs, x, g, dx_out)


def _mm_tn(name, a, b, colsum_b=False, tm=None, tk=2048):
    t, ma = a.shape
    nb = b.shape[1]
    tm = tm or _pick(ma, (1408, 1152, 1024, 768, 512, 256, 128))
    tk = _pick(t, (tk, 256, 128))
    nk = t // tk

    def body(*refs):
        a_ref, b_ref, o_ref = refs[0], refs[1], refs[2]
        acc_ref = refs[-1]
        kk = pl.program_id(1)
        bv = b_ref[...]

        @pl.when(kk == 0)
        def _():
            acc_ref[...] = jnp.zeros_like(acc_ref)

        acc_ref[...] += lax.dot_general(a_ref[...].astype(BF16), bv.astype(BF16), (((0,), (0,)), ((), ())),
                                        preferred_element_type=F32)
        if colsum_b:
            s_ref = refs[3]

            @pl.when((kk == 0) & (pl.program_id(0) == 0))
            def _():
                s_ref[...] = jnp.zeros_like(s_ref)

            @pl.when(pl.program_id(0) == 0)
            def _():
                s_ref[...] += jnp.sum(bv.astype(F32), axis=0, keepdims=True)

        @pl.when(kk == nk - 1)
        def _():
            o_ref[...] = acc_ref[...].astype(o_ref.dtype)

    out_shape = [jax.ShapeDtypeStruct((ma, nb), BF16)]
    out_specs = [pl.BlockSpec((tm, nb), lambda i, kk: (i, 0))]
    if colsum_b:
        out_shape.append(jax.ShapeDtypeStruct((1, nb), F32))
        out_specs.append(pl.BlockSpec((1, nb), lambda i, kk: (0, 0)))
    res = pl.pallas_call(
        body, name=name, out_shape=out_shape, grid=(ma // tm, nk),
        in_specs=[pl.BlockSpec((tk, tm), lambda i, kk: (kk, i)), pl.BlockSpec((tk, nb), lambda i, kk: (kk, 0))],
        out_specs=out_specs, scratch_shapes=[pltpu.VMEM((tm, nb), F32)],
        compiler_params=_params(("arbitrary", "arbitrary"), V7X_VMEM_LIMIT_BYTES),
    )(a, b)
    return res if colsum_b else res[0]


def _mm_tn_pair(name, a1, a2, b, tk=1024):
    t, ma = a1.shape
    nb = b.shape[1]
    tm = _pick(ma, (1408, 1152, 1024, 768, 512, 256, 128))
    tk = _pick(t, (tk, 256, 128))
    nk = t // tk
    dims = (((0,), (0,)), ((), ()))

    def body(a1_ref, a2_ref, b_ref, o1_ref, o2_ref, acc1_ref, acc2_ref):
        kk = pl.program_id(1)
        bv = b_ref[...]

        @pl.when(kk == 0)
        def _():
            acc1_ref[...] = jnp.zeros_like(acc1_ref)
            acc2_ref[...] = jnp.zeros_like(acc2_ref)

        acc1_ref[...] += lax.dot_general(a1_ref[...], bv, dims, preferred_element_type=F32)
        acc2_ref[...] += lax.dot_general(a2_ref[...], bv, dims, preferred_element_type=F32)

        @pl.when(kk == nk - 1)
        def _():
            o1_ref[...] = acc1_ref[...].astype(BF16)
            o2_ref[...] = acc2_ref[...].astype(BF16)

    sa = pl.BlockSpec((tk, tm), lambda i, kk: (kk, i))
    so = pl.BlockSpec((tm, nb), lambda i, kk: (i, 0))
    return pl.pallas_call(
        body, name=name, out_shape=[jax.ShapeDtypeStruct((ma, nb), BF16)] * 2, grid=(ma // tm, nk),
        in_specs=[sa, sa, pl.BlockSpec((tk, nb), lambda i, kk: (kk, 0))], out_specs=[so, so],
        scratch_shapes=[pltpu.VMEM((tm, nb), F32)] * 2,
        compiler_params=_params(("arbitrary", "arbitrary"), V7X_VMEM_LIMIT_BYTES),
    )(a1, a2, b)


FFN_CHUNK = 256


def _ffn_down_bwd(name, dx, wd, gq, uq):
    t, d_model = dx.shape
    f = wd.shape[0]
    tm = _pick(t, (256, 128))
    ck = _pick(f, (FFN_CHUNK, 128))
    nt = (((1,), (1,)), ((), ()))
    tn = (((0,), (0,)), ((), ()))
    last = t // tm - 1

    def body(dx_ref, w_ref, g_ref, u_ref, dg_ref, du_ref, dw_ref, acc_ref):
        i = pl.program_id(0)

        @pl.when(i == 0)
        def _():
            acc_ref[...] = jnp.zeros_like(acc_ref)

        dxb = dx_ref[...].astype(BF16)
        for c0 in range(0, f, ck):
            cs = slice(c0, c0 + ck)
            dact = lax.dot_general(dxb, w_ref[cs, :], nt, preferred_element_type=F32)
            gf, uf = g_ref[:, cs].astype(F32), u_ref[:, cs].astype(F32)
            sg = _sigmoid(gf)
            dg_ref[:, cs] = (dact * uf * (sg * (1.0 + gf * (1.0 - sg)))).astype(BF16)
            du_ref[:, cs] = (dact * gf * sg).astype(BF16)
            act = (gf * sg * uf).astype(BF16)
            acc_ref[cs, :] += lax.dot_general(act, dxb, tn, preferred_element_type=F32)

        @pl.when(i == last)
        def _():
            dw_ref[...] = acc_ref[...].astype(BF16)

    wide = pl.BlockSpec((tm, f), lambda i: (i, 0))
    whole = pl.BlockSpec((f, d_model), lambda i: (0, 0))
    return pl.pallas_call(
        body, name=name,
        out_shape=[jax.ShapeDtypeStruct((t, f), BF16), jax.ShapeDtypeStruct((t, f), BF16),
                   jax.ShapeDtypeStruct((f, d_model), BF16)],
        grid=(t // tm,), in_specs=[pl.BlockSpec((tm, d_model), lambda i: (i, 0)), whole, wide, wide],
        out_specs=[wide, wide, whole], scratch_shapes=[pltpu.VMEM((f, d_model), F32)],
        compiler_params=_params(("arbitrary",), V7X_VMEM_LIMIT_BYTES),
    )(dx, wd, gq, uq)


def _rmsnorm_fwd(name, x, g):
    t, d = x.shape
    tr = _pick(t, (512, 256, 128))

    def body(x_ref, g_ref, h_ref):
        xv = x_ref[...]
        r = lax.rsqrt(jnp.mean(xv * xv, axis=-1, keepdims=True) + EPS)
        h_ref[...] = (xv * r * g_ref[...]).astype(BF16)

    return pl.pallas_call(
        body, name=name, out_shape=jax.ShapeDtypeStruct((t, d), BF16), grid=(t // tr,),
        in_specs=[pl.BlockSpec((tr, d), lambda i: (i, 0)), pl.BlockSpec((1, d), lambda i: (0, 0))],
        out_specs=pl.BlockSpec((tr, d), lambda i: (i, 0)), compiler_params=_params(("parallel",)),
    )(x, g)


def _loss_head(name, x, g, target):
    t, d = x.shape
    tr = _pick(t, (512, 256, 128))

    def body(x_ref, g_ref, t_ref, dx_ref, dg_ref, l_ref):
        xv = x_ref[...]
        r = lax.rsqrt(jnp.mean(xv * xv, axis=-1, keepdims=True) + EPS)
        yv = xv * r
        diff = yv * g_ref[...] - t_ref[...]
        dout = diff * (1.0 / d)
        dy = dout * g_ref[...]
        dx_ref[...] = r * (dy - yv * jnp.mean(dy * yv, axis=-1, keepdims=True))

        @pl.when(pl.program_id(0) == 0)
        def _():
            dg_ref[...] = jnp.zeros_like(dg_ref)
            l_ref[...] = jnp.zeros_like(l_ref)

        dg_ref[...] += jnp.sum(dout * yv, axis=0, keepdims=True)
        l_ref[...] += (0.5 / d) * jnp.sum(diff * diff, axis=0, keepdims=True)

    big = pl.BlockSpec((tr, d), lambda i: (i, 0))
    row = pl.BlockSpec((1, d), lambda i: (0, 0))
    return pl.pallas_call(
        body, name=name,
        out_shape=[jax.ShapeDtypeStruct((t, d), F32), jax.ShapeDtypeStruct((1, d), F32),
                   jax.ShapeDtypeStruct((1, d), F32)],
        grid=(t // tr,), in_specs=[big, row, big], out_specs=[big, row, row],
        compiler_params=_params(("arbitrary",)),
    )(x, g, target)


CONV_ROWS = 64
SUBLANES = 8


def _fill_window(win_ref, sh_ref, parts):
    rows = sh_ref.shape[2]
    for cb in range(win_ref.shape[0]):
        for r0, val in parts:
            win_ref[cb, r0:r0 + val.shape[0], :] = val[:, 128 * cb:128 * (cb + 1)]
        win_ref[cb, rows:rows + SUBLANES, :] = jnp.zeros((SUBLANES, 128), F32)
        for b in range(1, SUBLANES):
            sh_ref[b - 1, cb] = win_ref[cb, b:b + rows, :]


def _window_rows(win_ref, sh_ref, o, cb):
    b = o % SUBLANES
    if b == 0:
        return win_ref[cb, o:o + CONV_ROWS, :]
    return sh_ref[b - 1, cb, o - b:o - b + CONV_ROWS, :]


def _window_scratch(rows, c):
    return [pltpu.VMEM((c // 128, rows + SUBLANES, 128), F32), pltpu.VMEM((SUBLANES - 1, c // 128, rows, 128), F32)]


def _dwconv_fwd(name, glu, w_dw, b_dw, ln_g, ln_b):
    t, c = glu.shape
    tt = _pick(t, (256, 128))
    hb = tt // CONV_HALO

    def body(cur_ref, halo_ref, w_ref, b_ref, g_ref, be_ref, dw_ref, s_ref, win_ref, sh_ref):
        i = pl.program_id(0)
        halo = jnp.where(i > 0, halo_ref[...].astype(F32), 0.0)
        _fill_window(win_ref, sh_ref, [(0, halo), (CONV_HALO, cur_ref[...].astype(F32))])
        for r0 in range(0, tt, CONV_ROWS):
            for cb in range(c // 128):
                c0 = 128 * cb
                acc = jnp.zeros((CONV_ROWS, 128), F32) + b_ref[:, c0:c0 + 128]
                for k in range(CONV_WIDTH):
                    o = r0 + k + CONV_HALO - (CONV_WIDTH - 1)
                    acc = acc + w_ref[k:k + 1, c0:c0 + 128] * _window_rows(win_ref, sh_ref, o, cb)
                dw_ref[r0:r0 + CONV_ROWS, c0:c0 + 128] = acc
        u = dw_ref[...]
        mu = jnp.mean(u, axis=-1, keepdims=True)
        uc = u - mu
        rstd = lax.rsqrt(jnp.mean(uc * uc, axis=-1, keepdims=True) + EPS)
        z = uc * rstd * g_ref[...] + be_ref[...]
        s_ref[...] = (z * _sigmoid(z)).astype(BF16)

    big = pl.BlockSpec((tt, c), lambda i: (i, 0))
    row = pl.BlockSpec((1, c), lambda i: (0, 0))
    return pl.pallas_call(
        body, name=name, out_shape=[jax.ShapeDtypeStruct((t, c), F32), jax.ShapeDtypeStruct((t, c), BF16)],
        grid=(t // tt,),
        in_specs=[big, pl.BlockSpec((CONV_HALO, c), lambda i: (jnp.maximum(i * hb - 1, 0), 0)),
                  pl.BlockSpec((CONV_HALO, c), lambda i: (0, 0)), row, row, row],
        out_specs=[big, big], scratch_shapes=_window_scratch(tt + CONV_HALO, c),
        compiler_params=_params(("parallel",), V7X_VMEM_LIMIT_BYTES),
    )(glu, glu, w_dw, b_dw, ln_g, ln_b)


def _ln_silu_bwd(name, dx, w2, dw, ln_g, ln_b):
    t, c = dw.shape
    d_model = dx.shape[1]
    tr = _pick(t, (256, 128))

    def body(dx_ref, w_ref, dw_ref, g_ref, be_ref, o_ref, acc_ref):
        ds = lax.dot_general(dx_ref[...].astype(BF16), w_ref[...], (((1,), (1,)), ((), ())),
                             preferred_element_type=F32)
        u = dw_ref[...]
        mu = jnp.mean(u, axis=-1, keepdims=True)
        uc = u - mu
        rstd = lax.rsqrt(jnp.mean(uc * uc, axis=-1, keepdims=True) + EPS)
        xh = uc * rstd
        z = xh * g_ref[...] + be_ref[...]
        sg = _sigmoid(z)
        dz = ds * (sg * (1.0 + z * (1.0 - sg)))
        dxh = dz * g_ref[...]
        du = rstd * (dxh - jnp.mean(dxh, axis=-1, keepdims=True) - xh * jnp.mean(dxh * xh, axis=-1, keepdims=True))
        o_ref[...] = du

        @pl.when(pl.program_id(0) == 0)
        def _():
            acc_ref[...] = jnp.zeros_like(acc_ref)

        acc_ref[0:1, :] += jnp.sum(dz * xh, axis=0, keepdims=True)
        acc_ref[1:2, :] += jnp.sum(dz, axis=0, keepdims=True)
        acc_ref[2:3, :] += jnp.sum(du, axis=0, keepdims=True)

    big = pl.BlockSpec((tr, c), lambda i: (i, 0))
    row = pl.BlockSpec((1, c), lambda i: (0, 0))
    return pl.pallas_call(
        body, name=name, out_shape=[jax.ShapeDtypeStruct((t, c), F32), jax.ShapeDtypeStruct((8, c), F32)],
        grid=(t // tr,),
        in_specs=[pl.BlockSpec((tr, d_model), lambda i: (i, 0)), pl.BlockSpec((c, d_model), lambda i: (0, 0)),
                  big, row, row],
        out_specs=[big, pl.BlockSpec((8, c), lambda i: (0, 0))],
        compiler_params=_params(("arbitrary",), V7X_VMEM_LIMIT_BYTES),
    )(dx, w2, dw, ln_g, ln_b)


def _dwconv_bwd(name, ddw, a, gt, w_dw):
    t, c = ddw.shape
    tt = _pick(t, (256, 128))
    hb = tt // CONV_HALO
    last = t // tt - 1
    back = CONV_WIDTH - 1

    def body(d_ref, dn_ref, a_ref, ap_ref, g_ref, gp_ref, w_ref, du_ref, dwk_ref, db_ref,
             wd_ref, shd_ref, wg_ref, shg_ref, dg_ref, dwk8_ref):
        i = pl.program_id(0)
        _fill_window(wd_ref, shd_ref, [(0, d_ref[...]), (tt, jnp.where(i < last, dn_ref[...], 0.0))])
        glu_prev = ap_ref[...].astype(F32) * _sigmoid(gp_ref[...].astype(F32))
        av = a_ref[...].astype(F32)
        sg = _sigmoid(g_ref[...].astype(F32))
        _fill_window(wg_ref, shg_ref, [(0, jnp.where(i > 0, glu_prev, 0.0)), (CONV_HALO, av * sg)])

        @pl.when(i == 0)
        def _():
            dwk8_ref[...] = jnp.zeros_like(dwk8_ref)
            db_ref[...] = jnp.zeros_like(db_ref)

        for r0 in range(0, tt, CONV_ROWS):
            for cb in range(c // 128):
                c0 = 128 * cb
                dcur = wd_ref[cb, r0:r0 + CONV_ROWS, :]
                acc = jnp.zeros((CONV_ROWS, 128), F32)
                for k in range(CONV_WIDTH):
                    acc = acc + w_ref[k:k + 1, c0:c0 + 128] * _window_rows(wd_ref, shd_ref, r0 + back - k, cb)
                    p = dcur * _window_rows(wg_ref, shg_ref, r0 + k + CONV_HALO - back, cb)
                    s8 = p[0:SUBLANES]
                    for q in range(SUBLANES, CONV_ROWS, SUBLANES):
                        s8 = s8 + p[q:q + SUBLANES]
                    dwk8_ref[SUBLANES * k:SUBLANES * (k + 1), c0:c0 + 128] += s8
                dg_ref[r0:r0 + CONV_ROWS, c0:c0 + 128] = acc

        @pl.when(i == last)
        def _():
            for k in range(CONV_WIDTH):
                dwk_ref[k:k + 1, :] = jnp.sum(dwk8_ref[SUBLANES * k:SUBLANES * (k + 1), :], axis=0, keepdims=True)
            dwk_ref[CONV_WIDTH:, :] = jnp.zeros((CONV_HALO - CONV_WIDTH, c), F32)
        dglu = dg_ref[...]
        da = dglu * sg
        dgate = dglu * av * sg * (1.0 - sg)
        du_ref[:, 0:c] = da.astype(BF16)
        du_ref[:, c:] = dgate.astype(BF16)
        db_ref[:, 0:c] += jnp.sum(da, axis=0, keepdims=True)
        db_ref[:, c:] += jnp.sum(dgate, axis=0, keepdims=True)

    big = pl.BlockSpec((tt, c), lambda i: (i, 0))
    prev = pl.BlockSpec((CONV_HALO, c), lambda i: (jnp.maximum(i * hb - 1, 0), 0))
    nxt = pl.BlockSpec((CONV_HALO, c), lambda i: (jnp.minimum((i + 1) * hb, t // CONV_HALO - 1), 0))
    return pl.pallas_call(
        body, name=name,
        out_shape=[jax.ShapeDtypeStruct((t, 2 * c), BF16), jax.ShapeDtypeStruct((CONV_HALO, c), F32),
                   jax.ShapeDtypeStruct((1, 2 * c), F32)],
        grid=(t // tt,),
        in_specs=[big, nxt, big, prev, big, prev, pl.BlockSpec((CONV_HALO, c), lambda i: (0, 0))],
        out_specs=[pl.BlockSpec((tt, 2 * c), lambda i: (i, 0)), pl.BlockSpec((CONV_HALO, c), lambda i: (0, 0)),
                   pl.BlockSpec((1, 2 * c), lambda i: (0, 0))],
        scratch_shapes=_window_scratch(tt + CONV_HALO, c) + _window_scratch(tt + CONV_HALO, c)
        + [pltpu.VMEM((tt, c), F32), pltpu.VMEM((SUBLANES * CONV_HALO, c), F32)],
        compiler_params=_params(("arbitrary",), V7X_VMEM_LIMIT_BYTES),
    )(ddw, ddw, a, a, gt, gt, w_dw)


def _bucket_tables():
    i = np.arange(N_BACK)[:, None]
    j = np.arange(2 * N_BACK)[None, :]
    dist = i + N_BACK - j
    valid = (dist >= 0) & (dist <= N_BACK)
    max_exact = N_BUCKETS // 2
    out = []
    for d in DILATIONS:
        n = np.maximum(dist * d, 0)
        nf = np.maximum(n, 1).astype(np.float32)
        large = max_exact + (np.log(nf / np.float32(max_exact)) / np.float32(math.log(REL_MAX_DISTANCE / max_exact))
                             * np.float32(N_BUCKETS - max_exact)).astype(np.int32)
        large = np.minimum(large, N_BUCKETS - 1)
        out.append(np.where(valid, np.where(n < max_exact, n, large), -1))
    return np.stack(out).astype(np.int32)


def _bias_build(name, rel_bias, buckets):
    def body(tbl_ref, bk_ref, o_ref):
        g = pl.program_id(0)
        bk = bk_ref[0]
        for h in range(HEADS_PER_GROUP):
            acc = jnp.zeros(bk.shape, F32)
            for b in range(N_BUCKETS):
                acc = jnp.where(bk == b, tbl_ref[b, g * HEADS_PER_GROUP + h], acc)
            o_ref[h] = jnp.where(bk < 0, NEG_INF, acc)

    return pl.pallas_call(
        body, name=name, out_shape=jax.ShapeDtypeStruct((N_HEADS, N_BACK, 2 * N_BACK), F32), grid=(N_GROUPS,),
        in_specs=[pl.BlockSpec(memory_space=pltpu.SMEM), pl.BlockSpec((1, N_BACK, 2 * N_BACK), lambda g: (g, 0, 0))],
        out_specs=pl.BlockSpec((HEADS_PER_GROUP, N_BACK, 2 * N_BACK), lambda g: (g, 0, 0)),
        compiler_params=_params(("arbitrary",)),
    )(rel_bias, buckets)


def _bias_grad(name, dbs, buckets):
    nd = len(dbs)

    def body(*refs):
        bk = refs[nd][0]
        o_ref = refs[nd + 1]
        lane = lax.broadcasted_iota(jnp.int32, (1, 128), 1)
        db = [sum(r[h] for r in refs[:nd]) for h in range(HEADS_PER_GROUP)]
        for b in range(N_BUCKETS):
            row = jnp.zeros((1, 128), F32)
            for h in range(HEADS_PER_GROUP):
                s = jnp.sum(jnp.where(bk == b, db[h], 0.0), axis=0, keepdims=True)
                s = jnp.sum(s, axis=1, keepdims=True)
                row = jnp.where(lane // 32 == h, s, row)
            o_ref[0, b:b + 1, :] = row

    spec = pl.BlockSpec((HEADS_PER_GROUP, N_BACK, 2 * N_BACK), lambda g: (g, 0, 0))
    return pl.pallas_call(
        body, name=name, out_shape=jax.ShapeDtypeStruct((N_GROUPS, N_BUCKETS, 128), F32), grid=(N_GROUPS,),
        in_specs=[spec] * nd + [pl.BlockSpec((1, N_BACK, 2 * N_BACK), lambda g: (g, 0, 0))],
        out_specs=pl.BlockSpec((1, N_BUCKETS, 128), lambda g: (g, 0, 0)), compiler_params=_params(("arbitrary",)),
    )(*dbs, buckets)


def _head_cols(h):
    return slice(h * HEAD_DIM, (h + 1) * HEAD_DIM)


def _store_dilated(o_ref, tile_ref, val, d):
    nlb, rows, _ = tile_ref.shape
    width = 128 * nlb
    for lb in range(nlb):
        tile_ref[lb] = val[:, 128 * lb:128 * (lb + 1)]
    for r in range(d):
        for lb in range(nlb):
            c0 = r * width + 128 * lb
            o_ref[:, c0:c0 + 128] = tile_ref[lb, pl.ds(r, rows // d, stride=d), :].astype(o_ref.dtype)


def _attn_qkv_fwd(name, h, wq):
    t, k = h.shape
    tm = _pick(t, (512, 256))
    width = wq.shape[1]

    def body(h_ref, w0, w1, w2, o0, o1, o2, tile_ref):
        hv = h_ref[...]
        for g, (w_ref, o_ref) in enumerate(zip((w0, w1, w2), (o0, o1, o2))):
            acc = lax.dot_general(hv, w_ref[...], (((1,), (1,)), ((), ())), preferred_element_type=F32)
            if DILATIONS[g] == 1:
                o_ref[...] = acc.astype(BF16)
            else:
                _store_dilated(o_ref, tile_ref, acc, DILATIONS[g])

    return pl.pallas_call(
        body, name=name,
        out_shape=[jax.ShapeDtypeStruct((t // d, d * width), BF16) for d in DILATIONS], grid=(t // tm,),
        in_specs=[pl.BlockSpec((tm, k), lambda i: (i, 0))] + [pl.BlockSpec((width, k), lambda i: (0, 0))] * 3,
        out_specs=[pl.BlockSpec((tm // d, d * width), lambda i: (i, 0)) for d in DILATIONS],
        scratch_shapes=[pltpu.VMEM((width // 128, tm, 128), F32)],
        compiler_params=_params(("parallel",), V7X_VMEM_LIMIT_BYTES),
    )(h, wq[0], wq[1], wq[2])


def _attn_fwd(name, qkv, bias, g):
    d = DILATIONS[g]
    tq = qkv.shape[0]
    t = tq * d
    nblk = 3
    nsub = _pick(tq // N_BACK, (4, 2))
    scale = HEAD_DIM ** -0.5

    def body(q_ref, kp_ref, kc_ref, vp_ref, vc_ref, b_ref, o_ref, l_ref):
        m2 = pl.program_id(1)
        col = lax.broadcasted_iota(jnp.int32, (N_BACK, 2 * N_BACK), 1)
        lane = lax.broadcasted_iota(jnp.int32, (N_BACK, 128), 1)
        for sub in range(nsub):
            rows = slice(N_BACK * sub, N_BACK * (sub + 1))
            both = slice(N_BACK * (sub - 1), N_BACK * (sub + 1))
            lse_tile = jnp.zeros((N_BACK, 128), F32)
            outs = []
            for h in range(HEADS_PER_GROUP):
                hc = _head_cols(h)
                if sub == 0:
                    kk = jnp.concatenate([kp_ref[:, hc], kc_ref[0:N_BACK, hc]], axis=0)
                    vv = jnp.concatenate([vp_ref[:, hc], vc_ref[0:N_BACK, hc]], axis=0)
                else:
                    kk, vv = kc_ref[both, hc], vc_ref[both, hc]
                s = lax.dot_general(q_ref[rows, hc], kk, (((1,), (1,)), ((), ())), preferred_element_type=F32)
                s = s * scale + b_ref[h]
                if sub == 0:
                    s = jnp.where((col >= N_BACK) | (m2 > 0), s, NEG_INF)
                m = jnp.max(s, axis=-1, keepdims=True)
                p = jnp.exp(s - m)
                den = jnp.sum(p, axis=-1, keepdims=True)
                outs.append(jnp.dot(p.astype(BF16), vv, preferred_element_type=F32) / den)
                lse_tile = jnp.where(lane // 32 == h, m + jnp.log(den), lse_tile)
            o_ref[rows, :] = jnp.concatenate(outs, axis=1)
            l_ref[rows, :] = lse_tile

    def blk(part, prev):
        if prev:
            return pl.BlockSpec((N_BACK, GROUP_COLS), lambda r, n: (jnp.maximum(nsub * n - 1, 0), r * nblk + part))
        return pl.BlockSpec((nsub * N_BACK, GROUP_COLS), lambda r, n: (n, r * nblk + part))

    o, l = pl.pallas_call(
        body, name=name,
        out_shape=[jax.ShapeDtypeStruct((tq, d * GROUP_COLS), F32), jax.ShapeDtypeStruct((tq, d * 128), F32)],
        grid=(d, tq // (nsub * N_BACK)),
        in_specs=[blk(0, False), blk(1, True), blk(1, False), blk(2, True), blk(2, False),
                  pl.BlockSpec((HEADS_PER_GROUP, N_BACK, 2 * N_BACK), lambda r, n: (g, 0, 0))],
        out_specs=[pl.BlockSpec((nsub * N_BACK, GROUP_COLS), lambda r, n: (n, r)),
                   pl.BlockSpec((nsub * N_BACK, 128), lambda r, n: (n, r))],
        compiler_params=_params(("parallel", "parallel")),
    )(qkv, qkv, qkv, qkv, qkv, bias)
    return o, l


def _group_weights(l_refs, h):
    ls = [l_ref[:, 32 * h:32 * h + 1] for l_ref in l_refs]
    m = jnp.maximum(jnp.maximum(ls[0], ls[1]), ls[2])
    es = [jnp.exp(l - m) for l in ls]
    tot = es[0] + es[1] + es[2]
    return [e / tot for e in es]


def _load_dilated(v_ref, n_ref, tile_ref, d):
    nlb, rows, _ = tile_ref.shape
    width = 128 * nlb
    for r in range(d):
        for lb in range(nlb):
            c0 = r * width + 128 * lb
            tile_ref[lb, pl.ds(r, rows // d, stride=d), :] = v_ref[:, c0:c0 + 128]
    n_ref[...] = jnp.concatenate([tile_ref[lb] for lb in range(nlb)], axis=1)


def _attn_merge_out(name, os_, ls, wot, x, gain):
    t, d_model = x.shape
    tr = _pick(t, (512, 256))

    def body(o0, o1v, o2v, l0, l1v, l2v, w_ref, x_ref, g_ref, om_ref, out_ref, h_ref, o1, o2, l1, l2,
             otile_ref, ltile_ref):
        _load_dilated(o1v, o1, otile_ref, DILATIONS[1])
        _load_dilated(o2v, o2, otile_ref, DILATIONS[2])
        _load_dilated(l1v, l1, ltile_ref, DILATIONS[1])
        _load_dilated(l2v, l2, ltile_ref, DILATIONS[2])
        o_refs = (o0, o1, o2)
        pieces = [[None] * HEADS_PER_GROUP for _ in range(N_GROUPS)]
        for h in range(HEADS_PER_GROUP):
            al = _group_weights((l0, l1, l2), h)
            for g in range(N_GROUPS):
                pieces[g][h] = o_refs[g][:, _head_cols(h)] * al[g]
        om = jnp.concatenate([p for row in pieces for p in row], axis=1).astype(BF16)
        om_ref[...] = om
        out = x_ref[...] + lax.dot_general(om, w_ref[...], (((1,), (1,)), ((), ())), preferred_element_type=F32)
        out_ref[...] = out
        h_ref[...] = _rms_rows(out, g_ref[...])

    so = pl.BlockSpec((tr, GROUP_COLS), lambda i: (i, 0))
    sl = pl.BlockSpec((tr, 128), lambda i: (i, 0))
    sx = pl.BlockSpec((tr, d_model), lambda i: (i, 0))
    views = [pl.BlockSpec((tr // d, d * GROUP_COLS), lambda i: (i, 0)) for d in DILATIONS]
    views += [pl.BlockSpec((tr // d, d * 128), lambda i: (i, 0)) for d in DILATIONS]
    res = pl.pallas_call(
        body, name=name,
        out_shape=[jax.ShapeDtypeStruct((t, D_ATTN), BF16), jax.ShapeDtypeStruct((t, d_model), F32),
                   jax.ShapeDtypeStruct((t, d_model), BF16)]
        + [jax.ShapeDtypeStruct((t, GROUP_COLS), F32)] * 2 + [jax.ShapeDtypeStruct((t, 128), F32)] * 2,
        grid=(t // tr,),
        in_specs=views + [pl.BlockSpec((d_model, D_ATTN), lambda i: (0, 0)), sx,
                          pl.BlockSpec((1, d_model), lambda i: (0, 0))],
        out_specs=[pl.BlockSpec((tr, D_ATTN), lambda i: (i, 0)), sx, sx, so, so, sl, sl],
        scratch_shapes=[pltpu.VMEM((GROUP_COLS // 128, tr, 128), F32), pltpu.VMEM((1, tr, 128), F32)],
        compiler_params=_params(("parallel",), V7X_VMEM_LIMIT_BYTES),
    )(*os_, *ls, wot, x, gain)
    om, x_new, h_next, o1, o2, l1, l2 = res
    return om, x_new, h_next, [os_[0], o1, o2], [ls[0], l1, l2]


def _attn_bwd_prep(name, dx, wot, os_, ls):
    t, d_model = dx.shape
    tr = _pick(t, (512, 256))

    def body(dx_ref, w_ref, o0, o1, o2, l0, l1, l2, d0, d1, d2, c0, c1, c2, dtile_ref, ctile_ref):
        o_refs, d_refs, c_refs = (o0, o1, o2), (d0, d1, d2), (c0, c1, c2)
        d_out = jnp.dot(dx_ref[...].astype(BF16), w_ref[...], preferred_element_type=F32)
        lane = lax.broadcasted_iota(jnp.int32, (tr, 128), 1)
        dos = [[None] * HEADS_PER_GROUP for _ in range(N_GROUPS)]
        cs = [jnp.zeros((tr, 128), F32) for _ in range(N_GROUPS)]
        for h in range(HEADS_PER_GROUP):
            al = _group_weights((l0, l1, l2), h)
            tot = jnp.zeros((tr, 1), F32)
            for g in range(N_GROUPS):
                dv = d_out[:, g * GROUP_COLS + h * HEAD_DIM:g * GROUP_COLS + (h + 1) * HEAD_DIM]
                tot = tot + al[g] * jnp.sum(dv * o_refs[g][:, _head_cols(h)], axis=-1, keepdims=True)
                dos[g][h] = dv * al[g]
            for g in range(N_GROUPS):
                cs[g] = jnp.where(lane // 32 == h, -al[g] * tot, cs[g])
        for g in range(N_GROUPS):
            do_g = jnp.concatenate(dos[g], axis=1)
            if DILATIONS[g] == 1:
                d_refs[g][...] = do_g.astype(BF16)
                c_refs[g][...] = cs[g]
            else:
                _store_dilated(d_refs[g], dtile_ref, do_g, DILATIONS[g])
                _store_dilated(c_refs[g], ctile_ref, cs[g], DILATIONS[g])

    so = pl.BlockSpec((tr, GROUP_COLS), lambda i: (i, 0))
    sl = pl.BlockSpec((tr, 128), lambda i: (i, 0))
    res = pl.pallas_call(
        body, name=name,
        out_shape=[jax.ShapeDtypeStruct((t // d, d * GROUP_COLS), BF16) for d in DILATIONS]
        + [jax.ShapeDtypeStruct((t // d, d * 128), F32) for d in DILATIONS],
        grid=(t // tr,),
        in_specs=[pl.BlockSpec((tr, d_model), lambda i: (i, 0)), pl.BlockSpec((d_model, D_ATTN), lambda i: (0, 0))]
        + [so] * 3 + [sl] * 3,
        out_specs=[pl.BlockSpec((tr // d, d * GROUP_COLS), lambda i: (i, 0)) for d in DILATIONS]
        + [pl.BlockSpec((tr // d, d * 128), lambda i: (i, 0)) for d in DILATIONS],
        scratch_shapes=[pltpu.VMEM((GROUP_COLS // 128, tr, 128), F32), pltpu.VMEM((1, tr, 128), F32)],
        compiler_params=_params(("parallel",), V7X_VMEM_LIMIT_BYTES),
    )(dx, wot, *os_, *ls)
    return res[:3], res[3:]


def _attn_bwd(name, qkv, do, lse, cterm, bias, g):
    d = DILATIONS[g]
    tq = qkv.shape[0]
    t = tq * d
    nb = tq // N_BACK
    nblk = 3
    scale = HEAD_DIM ** -0.5
    nt = (((1,), (1,)), ((), ()))
    tn = (((0,), (0,)), ((), ()))

    def body(q2, qx, kp, k2, vp, v2, do2, dox, l2, lx, c2, cx, b_ref, dqkv_ref, db_ref):
        m2 = pl.program_id(1)

        @pl.when((m2 == 0) & (pl.program_id(0) == 0))
        def _():
            db_ref[...] = jnp.zeros_like(db_ref)

        row2 = lax.broadcasted_iota(jnp.int32, (2 * N_BACK, N_BACK), 0)
        lo, hi = slice(0, N_BACK), slice(N_BACK, 2 * N_BACK)
        for sub in range(2):
            rows = hi if sub else lo
            has_prev = True if sub else m2 > 0
            has_next = 2 * m2 + 2 < nb if sub else True
            on_ac = (row2 < N_BACK) | has_next
            dqs, dks, dvs = [], [], []
            for h in range(HEADS_PER_GROUP):
                hc = _head_cols(h)
                st = slice(32 * h, 32 * h + 1)
                b_prev, b_same = b_ref[h, :, 0:N_BACK], b_ref[h, :, N_BACK:]
                q0, k1, v1, d0 = q2[rows, hc], k2[rows, hc], v2[rows, hc], do2[rows, hc]
                l0, c0 = l2[rows, st], c2[rows, st]
                if sub:
                    k0, v0 = k2[lo, hc], v2[lo, hc]
                    q_ac = jnp.concatenate([q0, qx[:, hc]], axis=0)
                    d_ac = jnp.concatenate([d0, dox[:, hc]], axis=0)
                    l_ac = jnp.concatenate([l0, lx[:, st]], axis=0)
                    c_ac = jnp.concatenate([c0, cx[:, st]], axis=0)
                else:
                    k0, v0 = kp[:, hc], vp[:, hc]
                    q_ac, d_ac, l_ac, c_ac = q2[:, hc], do2[:, hc], l2[:, st], c2[:, st]
                s_ac = (lax.dot_general(q_ac, k1, nt, preferred_element_type=F32) * scale
                        + jnp.concatenate([b_same, b_prev], axis=0))
                p_ac = jnp.where(on_ac, jnp.exp(s_ac - l_ac), 0.0)
                ds_ac = p_ac * (lax.dot_general(d_ac, v1, nt, preferred_element_type=F32) + c_ac)
                s_b = lax.dot_general(q0, k0, nt, preferred_element_type=F32) * scale + b_prev
                p_b = jnp.where(has_prev, jnp.exp(s_b - l0), 0.0)
                ds_b = p_b * (lax.dot_general(d0, v0, nt, preferred_element_type=F32) + c0)
                ds_a = ds_ac[0:N_BACK]
                dqs.append(scale * jnp.dot(jnp.concatenate([ds_b, ds_a], axis=1).astype(BF16),
                                           jnp.concatenate([k0, k1], axis=0), preferred_element_type=F32))
                dks.append(scale * lax.dot_general(ds_ac.astype(BF16), q_ac, tn, preferred_element_type=F32))
                dvs.append(lax.dot_general(p_ac.astype(BF16), d_ac, tn, preferred_element_type=F32))
                db_ref[h, :, 0:N_BACK] += ds_b
                db_ref[h, :, N_BACK:] += ds_a
            dqkv_ref[rows, :] = jnp.concatenate(dqs + dks + dvs, axis=1).astype(BF16)

    def blk(width, which, col):
        if which == "prev":
            return pl.BlockSpec((N_BACK, width), lambda r, n: (jnp.maximum(2 * n - 1, 0), col(r)))
        if which == "next":
            return pl.BlockSpec((N_BACK, width), lambda r, n: (jnp.minimum(2 * n + 2, nb - 1), col(r)))
        return pl.BlockSpec((2 * N_BACK, width), lambda r, n: (n, col(r)))

    def qkv_blk(part, which):
        return blk(GROUP_COLS, which, lambda r: r * nblk + part)

    def grp_blk(width, which):
        return blk(width, which, lambda r: r)

    qv, dov, lv, cv = qkv, do, lse, cterm
    dqkv_g, db = pl.pallas_call(
        body, name=name,
        out_shape=[jax.ShapeDtypeStruct((tq, d * 3 * GROUP_COLS), BF16),
                   jax.ShapeDtypeStruct((HEADS_PER_GROUP, N_BACK, 2 * N_BACK), F32)],
        grid=(d, nb // 2),
        in_specs=[qkv_blk(0, "same"), qkv_blk(0, "next"), qkv_blk(1, "prev"), qkv_blk(1, "same"),
                  qkv_blk(2, "prev"), qkv_blk(2, "same"), grp_blk(GROUP_COLS, "same"), grp_blk(GROUP_COLS, "next"),
                  grp_blk(128, "same"), grp_blk(128, "next"), grp_blk(128, "same"), grp_blk(128, "next"),
                  pl.BlockSpec((HEADS_PER_GROUP, N_BACK, 2 * N_BACK), lambda r, n: (g, 0, 0))],
        out_specs=[pl.BlockSpec((2 * N_BACK, 3 * GROUP_COLS), lambda r, n: (n, r)),
                   pl.BlockSpec((HEADS_PER_GROUP, N_BACK, 2 * N_BACK), lambda r, n: (0, 0, 0))],
        compiler_params=_params(("arbitrary", "arbitrary")),
    )(qv, qv, qv, qv, qv, qv, dov, dov, lv, lv, cv, cv, bias)
    return dqkv_g.reshape(t, 3 * GROUP_COLS), db


def _row(v):
    return v.reshape(1, -1)


def _glu_epi(accs, extras, rows):
    a = (accs[0] + rows[0]).astype(BF16)
    gt = (accs[1] + rows[1]).astype(BF16)
    return a, gt, a.astype(F32) * _sigmoid(gt.astype(F32))


def _swiglu_epi(accs, extras, rows):
    gq, uq = accs[0].astype(BF16), accs[1].astype(BF16)
    gf = gq.astype(F32)
    return gq, uq, gf * _sigmoid(gf) * uq.astype(F32)


def _group_rows(w):
    parts = [w[p * D_ATTN:(p + 1) * D_ATTN].reshape(N_GROUPS, GROUP_COLS, -1) for p in range(3)]
    return jnp.concatenate(parts, axis=1)


def _ungroup_rows(wg):
    return jnp.concatenate([wg[g][p * GROUP_COLS:(p + 1) * GROUP_COLS] for p in range(3) for g in range(N_GROUPS)],
                           axis=0)


def _local_step(x, target, sm, depth, fetch, emit):
    d_model = x.shape[1]
    buckets = jnp.asarray(_bucket_tables())
    bias = _bias_build("bias_build", sm["rel_bias"], buckets)
    saved = []
    h = _rmsnorm_fwd("rms_mix_fwd0", x, _row(sm["norm_mix"][0]))
    for i in range(depth):
        j = i // 2
        rec = {"x_mix": x}
        wm = fetch(2 * i, x)
        rec.update(h_mix=h, wm=wm)
        ffn_gain = _row(sm["norm_ffn"][i])
        if i % 2 == 0:
            c = wm["w1t"].shape[0] // 2
            tn = _pick(c, (1024, 512, 256, 128))
            b1 = sm["conv_b_pw1"][j]
            a, gt, glu = _mm_nt(f"conv_pw1_fwd{j}", h, [wm["w1t"]] * 2, [0, c // tn], c, _glu_epi, (BF16,) * 3,
                                rows=[_row(b1[:c]), _row(b1[c:])], tn=tn)
            wm.update(fetch(2 * i, a, 1))
            dw, s = _dwconv_fwd(f"dwconv_fwd{j}", glu, wm["wdw"], _row(sm["conv_b_dw"][j]),
                                _row(sm["conv_ln_g"][j]), _row(sm["conv_ln_b"][j]))
            x, h2 = _project_residual(f"conv_pw2_fwd{j}", s, wm["w2"], x, bias=_row(sm["conv_b_pw2"][j]),
                                      gain=ffn_gain)
            rec.update(a=a, gt=gt, dw=dw, s=s)
        else:
            wq = _group_rows(wm["wqkvt"])
            qkv = _attn_qkv_fwd(f"attn_qkv_fwd{j}", h, wq)
            og = [_attn_fwd(f"attn_fwd{j}_{g}", qkv[g], bias, g) for g in range(N_GROUPS)]
            om, x, h2, os_, ls = _attn_merge_out(f"attn_out_fwd{j}", [o for o, _ in og], [l for _, l in og],
                                                 wm["wot"], x, ffn_gain)
            rec.update(qkv=qkv, os=os_, ls=ls, lse_views=[l for _, l in og], om=om, wq=wq)
        rec["x_ffn"] = x
        wf = fetch(2 * i + 1, x)
        f = wf["wd"].shape[0]
        gq, uq, act = _mm_nt(f"ffn_up_fwd{i}", h2, [wf["wgt"], wf["wut"]], [0, 0], f, _swiglu_epi, (BF16,) * 3)
        x, h = _project_residual(f"ffn_down_fwd{i}", act, wf["wd"], x,
                                 gain=_row(sm["norm_mix"][i + 1]) if i + 1 < depth else None)
        rec.update(h_ffn=h2, gq=gq, uq=uq, wf=wf)
        saved.append(rec)

    dx, g_final, loss_cols = _loss_head("loss_head", x, _row(sm["final_norm"]), target)

    g_mix, g_ffn = [None] * depth, [None] * depth
    nconv = (depth + 1) // 2
    g_b1, g_bdw, g_lng, g_lnb, g_b2 = ([None] * nconv for _ in range(5))
    dbias = []
    for i in reversed(range(depth)):
        j = i // 2
        rec = saved[i]
        wm, wf = rec["wm"], rec["wf"]
        dgate, dup, dwd = _ffn_down_bwd(f"ffn_down_bwd{i}", dx, wf["wd"], rec["gq"], rec["uq"])
        gf = {"wd": dwd}
        gf["wgt"], gf["wut"] = _mm_tn_pair(f"ffn_gate_up_dw{i}", dgate, dup, rec["h_ffn"])
        dx, g_ffn[i] = _mm_nn_rms_bwd(f"ffn_up_bwd{i}", [dgate, dup], [wf["wgt"], wf["wut"]], rec["x_ffn"],
                                      _row(sm["norm_ffn"][i]), dx, tm=256)
        dx = emit(2 * i + 1, gf, dx)
        if i % 2 == 0:
            c = wm["w2"].shape[0]
            gm = {}
            gm["w2"], g_b2[j] = _mm_tn(f"conv_pw2_dw{j}", rec["s"], dx, colsum_b=True)
            ddw, sums = _ln_silu_bwd(f"conv_pw2_bwd{j}", dx, wm["w2"], rec["dw"], _row(sm["conv_ln_g"][j]),
                                     _row(sm["conv_ln_b"][j]))
            g_lng[j], g_lnb[j], g_bdw[j] = sums[0], sums[1], sums[2]
            du, dwk, db1 = _dwconv_bwd(f"dwconv_bwd{j}", ddw, rec["a"], rec["gt"], wm["wdw"])
            gm["wdw"] = dwk[:CONV_WIDTH]
            g_b1[j] = db1[0]
            gm["w1t"] = _mm_tn(f"conv_pw1_dw{j}", du, rec["h_mix"])
            dh_terms = ([du], [wm["w1t"]])
        else:
            gm = {"wot": _mm_tn(f"attn_out_dw{j}", dx, rec["om"])}
            dos, cs = _attn_bwd_prep(f"attn_out_bwd{j}", dx, wm["wot"], rec["os"], rec["ls"])
            back = [_attn_bwd(f"attn_bwd{j}_{g}", rec["qkv"][g], dos[g], rec["lse_views"][g], cs[g], bias, g)
                    for g in range(N_GROUPS)]
            dqkv = [b[0] for b in back]
            dbias.append(jnp.concatenate([b[1] for b in back], axis=0))
            dh_terms = (dqkv, [rec["wq"][g] for g in range(N_GROUPS)])
            gm["wqkvt"] = _ungroup_rows([_mm_tn(f"attn_qkv_dw{j}_{g}", dqkv[g], rec["h_mix"])
                                         for g in range(N_GROUPS)])
        dx, g_mix[i] = _mm_nn_rms_bwd(f"mix_in_bwd{i}", dh_terms[0], dh_terms[1], rec["x_mix"],
                                      _row(sm["norm_mix"][i]), dx)
        dx = emit(2 * i, gm, dx)

    gb = _bias_grad("bias_grad", dbias, buckets)
    g_rel = jnp.transpose(gb[:, :, ::32], (1, 0, 2)).reshape(N_BUCKETS, N_HEADS)
    gsm = {
        "norm_mix": jnp.concatenate(g_mix, axis=0), "norm_ffn": jnp.concatenate(g_ffn, axis=0),
        "final_norm": g_final[0], "conv_b_pw1": jnp.stack(g_b1), "conv_b_dw": jnp.stack(g_bdw),
        "conv_ln_g": jnp.stack(g_lng), "conv_ln_b": jnp.stack(g_lnb),
        "conv_b_pw2": jnp.concatenate(g_b2, axis=0), "rel_bias": g_rel,
    }
    return loss_cols, dx, gsm


SMALL = ("norm_mix", "norm_ffn", "final_norm", "conv_b_pw1", "conv_b_dw", "conv_ln_g", "conv_ln_b", "conv_b_pw2",
         "rel_bias")
SHARDED = (("conv_w_pw1", "w1t", True), ("conv_w_pw2", "w2", False), ("attn_w_qkv", "wqkvt", True),
           ("attn_w_o", "wot", True), ("ffn_w_gate", "wgt", True), ("ffn_w_up", "wut", True),
           ("ffn_w_down", "wd", False))
ORDER = ("norm_mix", "norm_ffn", "final_norm", "conv_w_pw1", "conv_b_pw1", "conv_w_dw", "conv_b_dw", "conv_ln_g",
         "conv_ln_b", "conv_w_pw2", "conv_b_pw2", "attn_w_qkv", "attn_w_o", "rel_bias", "ffn_w_gate", "ffn_w_up",
         "ffn_w_down")
PACK_LANES = 128
PACK_ROW_TILE = 8


def _pack_small(vals):
    flat = jnp.concatenate([vals[n].reshape(-1) for n in SMALL])
    per_tile = PACK_LANES * PACK_ROW_TILE
    return jnp.pad(flat, (0, -flat.shape[0] % per_tile)).reshape(-1, PACK_LANES)


def _unpack_small(pack, like):
    flat, out, pos = pack.reshape(-1), {}, 0
    for n in SMALL:
        out[n] = flat[pos:pos + like[n].size].reshape(like[n].shape)
        pos += like[n].size
    return out


def _dw_blocks(w):
    l, k, c = w.shape
    blk = jnp.transpose(w.reshape(l, k, N_DEV, c // N_DEV), (2, 0, 1, 3)).reshape(N_DEV, l * k, c // N_DEV)
    return jnp.pad(blk, ((0, 0), (0, -(l * k) % 8), (0, 0)))


def kernel(x, norm_mix, norm_ffn, final_norm, conv_w_pw1, conv_b_pw1, conv_w_dw, conv_b_dw, conv_ln_g, conv_ln_b, conv_w_pw2, conv_b_pw2, attn_w_qkv, attn_w_o, rel_bias, ffn_w_gate, ffn_w_up, ffn_w_down, loss_target, m_norm_mix, m_norm_ffn, m_final_norm, m_conv_w_pw1, m_conv_b_pw1, m_conv_w_dw, m_conv_b_dw, m_conv_ln_g, m_conv_ln_b, m_conv_w_pw2, m_conv_b_pw2, m_attn_w_qkv, m_attn_w_o, m_rel_bias, m_ffn_w_gate, m_ffn_w_up, m_ffn_w_down, v_norm_mix, v_norm_ffn, v_final_norm, v_conv_w_pw1, v_conv_b_pw1, v_conv_w_dw, v_conv_b_dw, v_conv_ln_g, v_conv_ln_b, v_conv_w_pw2, v_conv_b_pw2, v_attn_w_qkv, v_attn_w_o, v_rel_bias, v_ffn_w_gate, v_ffn_w_up, v_ffn_w_down):
    w = dict(norm_mix=norm_mix, norm_ffn=norm_ffn, final_norm=final_norm, conv_w_pw1=conv_w_pw1,
             conv_b_pw1=conv_b_pw1, conv_w_dw=conv_w_dw, conv_b_dw=conv_b_dw, conv_ln_g=conv_ln_g,
             conv_ln_b=conv_ln_b, conv_w_pw2=conv_w_pw2, conv_b_pw2=conv_b_pw2, attn_w_qkv=attn_w_qkv,
             attn_w_o=attn_w_o, rel_bias=rel_bias, ffn_w_gate=ffn_w_gate, ffn_w_up=ffn_w_up, ffn_w_down=ffn_w_down)
    m = dict(norm_mix=m_norm_mix, norm_ffn=m_norm_ffn, final_norm=m_final_norm, conv_w_pw1=m_conv_w_pw1,
             conv_b_pw1=m_conv_b_pw1, conv_w_dw=m_conv_w_dw, conv_b_dw=m_conv_b_dw, conv_ln_g=m_conv_ln_g,
             conv_ln_b=m_conv_ln_b, conv_w_pw2=m_conv_w_pw2, conv_b_pw2=m_conv_b_pw2, attn_w_qkv=m_attn_w_qkv,
             attn_w_o=m_attn_w_o, rel_bias=m_rel_bias, ffn_w_gate=m_ffn_w_gate, ffn_w_up=m_ffn_w_up,
             ffn_w_down=m_ffn_w_down)
    v = dict(norm_mix=v_norm_mix, norm_ffn=v_norm_ffn, final_norm=v_final_norm, conv_w_pw1=v_conv_w_pw1,
             conv_b_pw1=v_conv_b_pw1, conv_w_dw=v_conv_w_dw, conv_b_dw=v_conv_b_dw, conv_ln_g=v_conv_ln_g,
             conv_ln_b=v_conv_ln_b, conv_w_pw2=v_conv_w_pw2, conv_b_pw2=v_conv_b_pw2, attn_w_qkv=v_attn_w_qkv,
             attn_w_o=v_attn_w_o, rel_bias=v_rel_bias, ffn_w_gate=v_ffn_w_gate, ffn_w_up=v_ffn_w_up,
             ffn_w_down=v_ffn_w_down)

    me = 4 * lax.axis_index("x") + 2 * lax.axis_index("y") + lax.axis_index("c")
    depth = ffn_w_gate.shape[0]
    n_conv, _, cb = conv_w_dw.shape

    def sublayer(key, layer):
        if key in ("wgt", "wut", "wd"):
            return 2 * layer + 1
        return 4 * layer if key in ("w1t", "w2") else 4 * layer + 2

    def landing(block, own):
        land = lax.empty((N_DEV,) + block.shape, block.dtype)
        return lax.dynamic_update_slice(land, own[None], (me,) + (0,) * block.ndim)

    by_sub = {s: [] for s in range(2 * depth)}
    for name, key, cols in SHARDED:
        sw = (jnp.swapaxes(w[name], 1, 2) if cols else w[name]).astype(BF16)
        for layer in range(sw.shape[0]):
            by_sub[sublayer(key, layer)].append((key, layer, sw[layer]))
    likes = {s: jnp.zeros((sum(sh.size for _, _, sh in by_sub[s]) // 1024, 1024), BF16) for s in by_sub}
    dw_shard = jnp.pad(conv_w_dw.reshape(-1, cb), ((0, -(n_conv * CONV_WIDTH) % 8), (0, 0)))
    like_dw = jnp.zeros(dw_shard.shape, F32)
    stages = {}
    for s in range(2 * depth):
        for entry in by_sub[s]:
            stages.setdefault((s, 0 if entry[0] != "w2" else 1), []).append(entry)
    stage_order = sorted(stages)
    groups = [([sh for _, _, sh in stages[st]], [landing(sh, sh) for _, _, sh in stages[st]]) for st in stage_order]
    groups.insert(1, ([dw_shard], [landing(dw_shard, dw_shard)]))
    gather, _ = _exchange_start("gather_start", groups, scatter=False, carry=jnp.zeros((8, 128), F32))
    handle = dict(zip(stage_order, gather[:1] + gather[2:]))
    dw_filters = []

    def fetch(s, after, part=0):
        st = (s, part)
        like = jnp.zeros((sum(sh.size for _, _, sh in stages[st]) // 1024, 1024), BF16)
        pieces = [(handle[st], like)]
        if st == (0, 1):
            pieces.append((gather[1], like_dw))
        landed = _exchange_wait(f"gather_wait{s}_{part}", pieces, after)
        out = {key: g.reshape(g.shape[0] * g.shape[1], g.shape[2]) for (key, _, _), g in zip(stages[st], landed[0])}
        if st == (0, 1):
            dw_all = landed[1][0]
            full = jnp.transpose(dw_all[:, :n_conv * CONV_WIDTH].reshape(N_DEV, n_conv, CONV_WIDTH, cb), (1, 2, 0, 3))
            full = jnp.pad(full.reshape(n_conv, CONV_WIDTH, N_DEV * cb), ((0, 0), (0, CONV_HALO - CONV_WIDTH), (0, 0)))
            dw_filters.extend(full[layer] for layer in range(n_conv))
        if "w2" in out:
            out["wdw"] = dw_filters[s // 4]
        return out

    scatter, dw_grads, started = {}, {}, {}

    def emit(s, gd, carry):
        parts = [gd[key].reshape(N_DEV, -1, gd[key].shape[1]) for key, _, _ in by_sub[s]]
        if "wdw" in gd:
            dw_grads[s // 4] = gd["wdw"]
        if s == 0:
            parts.append(_dw_blocks(jnp.stack([dw_grads[layer] for layer in range(n_conv)])))
        lands = [landing(p[0], lax.dynamic_index_in_dim(p, me, 0, keepdims=False)) for p in parts]
        groups = [(parts[:len(by_sub[s])], lands[:len(by_sub[s])])]
        if s == 0:
            groups.append((parts[-1:], lands[-1:]))
        if s == 0:
            scatter[s], started[0] = _exchange_start(f"scatter_start{s}", groups, scatter=True,
                                                     carry=jnp.zeros((8, 128), F32))
            return carry
        scatter[s], carry = _exchange_start(f"scatter_start{s}", groups, scatter=True, carry=carry)
        return carry

    sm = {n: w[n] for n in SMALL}
    loss_cols, dx, gsm = _local_step(x[0], loss_target[0], sm, depth, fetch, emit)
    loss = lax.psum(jnp.sum(loss_cols), ("x", "y", "c"))

    grads, summed, delta, new_m, new_v = {}, {}, {}, {}, {}

    def reduce_pieces(subs, landed):
        for s, recv in zip(subs, landed):
            for (key, layer, _), r in zip(by_sub[s], recv):
                summed[key, layer] = _sum8(f"sum_{key}{layer}", r)

    def update(names):
        for name in names:
            shape = w[name].shape
            res = _adamw(f"adamw_{name}",
                         *[t.reshape(-1, shape[-1]) for t in (grads[name], w[name], m[name], v[name])])
            delta[name], new_m[name], new_v[name] = (t.reshape(shape) for t in res)

    def stacked(name, key, cols):
        g = jnp.stack([summed[key, layer] for layer in range(w[name].shape[0])])
        return jnp.swapaxes(g, 1, 2) if cols else g

    early = sorted((s for s in scatter if s != 0), reverse=True)
    reduce_pieces(early, _exchange_wait("scatter_wait_early", [(scatter[s][0], likes[s]) for s in early],
                                        started[0]))
    late_names = [name for name, key, _ in SHARDED if any(k == key for k, _, _ in by_sub[0])]
    for name, key, cols in SHARDED:
        if name not in late_names:
            grads[name] = stacked(name, key, cols)
    early_names = [name for name, _, _ in SHARDED if name not in late_names]
    update(early_names)
    pack = _pack_small(gsm)
    ((pack_all,),) = _exchange("gather_small_grads", [([pack], pack)], scatter=False)
    pack_sum = _sum8("sum_small", pack_all)
    grads.update(_unpack_small(pack_sum, sm))

    landed = _exchange_wait("scatter_wait_last", [(scatter[0][0], likes[0]), (scatter[0][1], like_dw)],
                            new_v[early_names[-1]])
    reduce_pieces([0], landed[:1])
    for name, key, cols in SHARDED:
        if name in late_names:
            grads[name] = stacked(name, key, cols)
    grads["conv_w_dw"] = _sum8("sum_wdw", landed[1][0])[:n_conv * CONV_WIDTH].reshape(conv_w_dw.shape)
    update(late_names + ["conv_w_dw"])
    res = _adamw("adamw_small", pack_sum, _pack_small(sm), _pack_small({n: m[n] for n in SMALL}),
                 _pack_small({n: v[n] for n in SMALL}))
    for dst, t in zip((delta, new_m, new_v), res):
        dst.update(_unpack_small(t, sm))

    outs = [loss, dx[None]]
    for d in (grads, delta, new_m, new_v):
        outs += [d[n] for n in ORDER]
    return tuple(outs)
```

```python
import functools
import math

import numpy as np
import jax
import jax.numpy as jnp
from jax import lax
from jax.experimental import pallas as pl
from jax.experimental.pallas import tpu as pltpu

F32 = jnp.float32
BF16 = jnp.bfloat16

N_DEV = 8
HEAD_DIM = 64
HEADS_PER_GROUP = 4
GROUP_COLS = HEADS_PER_GROUP * HEAD_DIM
DILATIONS = (1, 4, 16)
N_BACK = 128
N_GROUPS = 3
N_HEADS = 12
D_ATTN = 768
N_BUCKETS = 32
REL_MAX_DISTANCE = 2048
CONV_WIDTH = 31
CONV_HALO = 32
EPS = 1e-6
NEG_INF = -1e30
ADAM_LR, ADAM_B1, ADAM_B2, ADAM_EPS, ADAM_WD, ADAM_STEP = 0.001, 0.9, 0.999, 1e-08, 0.01, 10
V7X_VMEM_LIMIT_BYTES = 56 * 1024 * 1024
MESH = pl.DeviceIdType.MESH
ANY = pl.BlockSpec(memory_space=pl.ANY)


def _pick(n, prefs):
    for p in prefs:
        if n % p == 0:
            return p
    return n


def _params(sem, vmem=None):
    return pltpu.CompilerParams(dimension_semantics=sem, vmem_limit_bytes=vmem)


def _sigmoid(x):
    return 1.0 / (1.0 + jnp.exp(-x))


def _exchange(name, groups, scatter):
    n_arr = [len(arrs) for arrs, _ in groups]
    n_in = sum(n_arr) + len(groups)
    ng = len(groups)

    def body(*refs):
        ins, outs, (send_sems, recv_sems, local_sems) = refs[:n_in], refs[n_in:-3], refs[-3:]
        x, y, c = lax.axis_index("x"), lax.axis_index("y"), lax.axis_index("c")
        me = 4 * x + 2 * y + c
        pos_in = pos_out = 0
        plans = []
        for gi in range(ng):
            srcs = ins[pos_in:pos_in + n_arr[gi]]
            like = ins[pos_in + n_arr[gi]]
            dsts = outs[pos_out:pos_out + n_arr[gi]]
            pos_in += n_arr[gi] + 1
            pos_out += n_arr[gi]
            plans.append((gi, srcs, like, dsts))
        local = []
        for gi, srcs, like, dsts in plans:
            for s, d in zip(srcs, dsts):
                cp = pltpu.make_async_copy(s.at[me] if scatter else s, d.at[me], local_sems.at[gi])
                cp.start()
                local.append(cp)
        for delta in range(1, N_DEV):
            dx, dy, dc = (delta >> 2) & 1, (delta >> 1) & 1, delta & 1
            px, py, pc = (1 - x if dx else x), (1 - y if dy else y), (1 - c if dc else c)
            peer = 4 * px + 2 * py + pc
            for gi, srcs, like, dsts in plans:
                for s, d in zip(srcs, dsts):
                    pltpu.make_async_remote_copy(
                        src_ref=s.at[peer] if scatter else s, dst_ref=d.at[me],
                        send_sem=send_sems.at[gi, delta - 1], recv_sem=recv_sems.at[gi, delta - 1],
                        device_id=(px, py, pc), device_id_type=MESH).start()
        for delta in range(1, N_DEV):
            for gi, srcs, like, dsts in plans:
                pltpu.make_async_remote_copy(
                    src_ref=like, dst_ref=like, send_sem=send_sems.at[gi, delta - 1],
                    recv_sem=recv_sems.at[gi, delta - 1], device_id=(x, y, c), device_id_type=MESH).wait()
        for cp in local:
            cp.wait()

    operands, out_shape = [], []
    for arrs, like in groups:
        operands += list(arrs) + [like]
        for a in arrs:
            blk = a.shape[1:] if scatter else a.shape
            out_shape.append(jax.ShapeDtypeStruct((N_DEV,) + tuple(blk), a.dtype))
    outs = pl.pallas_call(
        body, name=name, out_shape=out_shape, in_specs=[ANY] * len(operands), out_specs=[ANY] * len(out_shape),
        scratch_shapes=[pltpu.SemaphoreType.DMA((ng, N_DEV - 1)), pltpu.SemaphoreType.DMA((ng, N_DEV - 1)),
                        pltpu.SemaphoreType.DMA((ng,))],
        compiler_params=pltpu.CompilerParams(has_side_effects=True),
    )(*operands)
    res, pos = [], 0
    for n in n_arr:
        res.append(list(outs[pos:pos + n]))
        pos += n
    return res


HBM = pl.BlockSpec(memory_space=pltpu.HBM)
SEM = pl.BlockSpec(memory_space=pltpu.SEMAPHORE)
EFFECT = pltpu.SideEffectType.DATAFLOW_SIDE_EFFECTING


def _in_hbm(a):
    return pltpu.with_memory_space_constraint(a, pltpu.HBM)


def _exchange_start(name, groups, scatter, carry):
    ns = [len(s) for s, _ in groups]
    n_in = 2 * sum(ns)

    def body(*refs):
        ins, outs = refs[:n_in], refs[n_in + 1:]
        x, y, c = lax.axis_index("x"), lax.axis_index("y"), lax.axis_index("c")
        me = 4 * x + 2 * y + c
        pi = po = 0
        for n in ns:
            srcs, lands = ins[pi:pi + n], ins[pi + n:pi + 2 * n]
            send_sems, recv_sems = outs[po], outs[po + 1]
            pi += 2 * n
            po += 2 + 2 * n
            for delta in range(1, N_DEV):
                dx, dy, dc = (delta >> 2) & 1, (delta >> 1) & 1, delta & 1
                px, py, pc = (1 - x if dx else x), (1 - y if dy else y), (1 - c if dc else c)
                peer = 4 * px + 2 * py + pc
                for s, d in zip(srcs, lands):
                    pltpu.make_async_remote_copy(
                        src_ref=s.at[peer] if scatter else s, dst_ref=d.at[me], send_sem=send_sems.at[delta - 1],
                        recv_sem=recv_sems.at[delta - 1], device_id=(px, py, pc), device_id_type=MESH).start()

    operands, out_shape, out_specs, aliases = [], [], [], {}
    for srcs, lands in groups:
        out_shape += [pltpu.SemaphoreType.DMA((N_DEV - 1,))] * 2
        out_specs += [SEM, SEM]
        for a in list(srcs) + list(lands):
            aliases[len(operands)] = len(out_shape)
            operands.append(_in_hbm(a))
            out_shape.append(pltpu.HBM(a.shape, a.dtype))
            out_specs.append(HBM)
    aliases[len(operands)] = len(out_shape)
    operands.append(_in_hbm(carry))
    out_shape.append(pltpu.HBM(carry.shape, carry.dtype))
    out_specs.append(HBM)
    outs = pl.pallas_call(
        body, name=name, out_shape=out_shape, in_specs=[HBM] * len(operands), out_specs=out_specs,
        input_output_aliases=aliases, compiler_params=pltpu.CompilerParams(has_side_effects=EFFECT),
    )(*operands)
    handles, po = [], 0
    for n in ns:
        handles.append((outs[po], outs[po + 1], list(outs[po + 2:po + 2 + n]), list(outs[po + 2 + n:po + 2 + 2 * n])))
        po += 2 + 2 * n
    return handles, outs[-1]


def _exchange_wait(name, pieces, after):
    ns = [len(h[2]) for h, _ in pieces]

    def body(*refs):
        x, y, c = lax.axis_index("x"), lax.axis_index("y"), lax.axis_index("c")
        pi = 0
        for n in ns:
            send_sems, recv_sems, like = refs[pi + 2 * n], refs[pi + 2 * n + 1], refs[pi + 2 * n + 2]
            pi += 2 * n + 3
            for delta in range(1, N_DEV):
                cp = pltpu.make_async_remote_copy(
                    src_ref=like, dst_ref=like, send_sem=send_sems.at[delta - 1], recv_sem=recv_sems.at[delta - 1],
                    device_id=(x, y, c), device_id_type=MESH)
                cp.wait_send()
                cp.wait_recv()

    operands, in_specs, out_shape, aliases = [], [], [], {}
    for (send_sems, recv_sems, srcs, lands), like in pieces:
        for a in srcs + lands:
            aliases[len(operands)] = len(out_shape)
            operands.append(a)
            in_specs.append(HBM)
            out_shape.append(pltpu.HBM(a.shape, a.dtype))
        operands += [send_sems, recv_sems, like]
        in_specs += [SEM, SEM, ANY]
    operands.append(after)
    in_specs.append(ANY)
    outs = pl.pallas_call(
        body, name=name, out_shape=out_shape, in_specs=in_specs, out_specs=[HBM] * len(out_shape),
        input_output_aliases=aliases, compiler_params=pltpu.CompilerParams(has_side_effects=EFFECT),
    )(*operands)
    res, po = [], 0
    for n in ns:
        res.append(list(outs[po + n:po + 2 * n]))
        po += 2 * n
    return res


def _sum8(name, r):
    _, rows, cols = r.shape
    tr = _pick(rows, (256, 128, 64, 32, 16, 8))

    def body(r_ref, o_ref):
        acc = r_ref[0].astype(F32)
        for p in range(1, N_DEV):
            acc = acc + r_ref[p].astype(F32)
        o_ref[...] = acc

    return pl.pallas_call(
        body, name=name, out_shape=jax.ShapeDtypeStruct((rows, cols), F32), grid=(rows // tr,),
        in_specs=[pl.BlockSpec((N_DEV, tr, cols), lambda i: (0, i, 0))],
        out_specs=pl.BlockSpec((tr, cols), lambda i: (i, 0)), compiler_params=_params(("parallel",)),
    )(r)


def _adamw(name, g, w, m, v):
    rows, cols = g.shape
    tr = _pick(rows, (512, 256, 128, 64, 32, 16, 8))
    c1 = 1.0 - ADAM_B1 ** ADAM_STEP
    c2 = 1.0 - ADAM_B2 ** ADAM_STEP

    def body(g_ref, w_ref, m_ref, v_ref, d_ref, nm_ref, nv_ref):
        gv = g_ref[...]
        nm = ADAM_B1 * m_ref[...] + (1.0 - ADAM_B1) * gv
        nv = ADAM_B2 * v_ref[...] + (1.0 - ADAM_B2) * (gv * gv)
        d_ref[...] = -ADAM_LR * ((nm / c1) / (jnp.sqrt(nv / c2) + ADAM_EPS) + ADAM_WD * w_ref[...])
        nm_ref[...] = nm
        nv_ref[...] = nv

    spec = pl.BlockSpec((tr, cols), lambda i: (i, 0))
    return pl.pallas_call(
        body, name=name, out_shape=[jax.ShapeDtypeStruct((rows, cols), F32)] * 3, grid=(rows // tr,),
        in_specs=[spec] * 4, out_specs=[spec] * 3, compiler_params=_params(("parallel",)),
    )(g, w, m, v)


def _mm_nt(name, a, ws, w_offs, n, epi, out_dtypes, extras=(), rows=(), tm=None, tn=None):
    m, k = a.shape
    tm = tm or _pick(m, (512, 256, 128))
    tn = tn or _pick(n, (1408, 1152, 1024, 512, 256, 128))
    nw, ne, nr = len(ws), len(extras), len(rows)

    def body(*refs):
        a_ref, w_refs = refs[0], refs[1:1 + nw]
        e_refs, r_refs = refs[1 + nw:1 + nw + ne], refs[1 + nw + ne:1 + nw + ne + nr]
        o_refs = refs[1 + nw + ne + nr:]
        av = a_ref[...].astype(BF16)
        accs = [lax.dot_general(av, w[...], (((1,), (1,)), ((), ())), preferred_element_type=F32) for w in w_refs]
        outs = epi(accs, [e[...] for e in e_refs], [r[...] for r in r_refs])
        for o_ref, o in zip(o_refs, outs):
            o_ref[...] = o.astype(o_ref.dtype)

    in_specs = [pl.BlockSpec((tm, k), lambda j, i: (i, 0))]
    in_specs += [pl.BlockSpec((tn, k), functools.partial(lambda j, i, off: (j + off, 0), off=off)) for off in w_offs]
    in_specs += [pl.BlockSpec((tm, tn), lambda j, i: (i, j))] * ne
    in_specs += [pl.BlockSpec((1, tn), lambda j, i: (0, j))] * nr
    return pl.pallas_call(
        body, name=name, out_shape=[jax.ShapeDtypeStruct((m, n), dt) for dt in out_dtypes],
        grid=(n // tn, m // tm), in_specs=in_specs,
        out_specs=[pl.BlockSpec((tm, tn), lambda j, i: (i, j))] * len(out_dtypes),
        compiler_params=_params(("parallel", "parallel"), V7X_VMEM_LIMIT_BYTES),
    )(a, *ws, *extras, *rows)


def _rms_rows(x, gain):
    r = lax.rsqrt(jnp.mean(x * x, axis=-1, keepdims=True) + EPS)
    return (x * r * gain).astype(BF16)


def _project_residual(name, a, b, x, bias=None, gain=None):
    m, k = a.shape
    n = b.shape[1]
    tm = _pick(m, (512, 256, 128))
    rows = [r for r in (bias, gain) if r is not None]

    def body(*refs):
        a_ref, b_ref, x_ref = refs[:3]
        r_refs = list(refs[3:3 + len(rows)])
        out = x_ref[...] + jnp.dot(a_ref[...], b_ref[...], preferred_element_type=F32)
        if bias is not None:
            out = out + r_refs.pop(0)[...]
        refs[3 + len(rows)][...] = out
        if gain is not None:
            refs[4 + len(rows)][...] = _rms_rows(out, r_refs.pop(0)[...])

    big = pl.BlockSpec((tm, n), lambda i: (i, 0))
    out_shape = [jax.ShapeDtypeStruct((m, n), F32)] + ([jax.ShapeDtypeStruct((m, n), BF16)] if gain is not None else [])
    res = pl.pallas_call(
        body, name=name, out_shape=out_shape, grid=(m // tm,),
        in_specs=[pl.BlockSpec((tm, k), lambda i: (i, 0)), pl.BlockSpec((k, n), lambda i: (0, 0)), big]
        + [pl.BlockSpec((1, n), lambda i: (0, 0))] * len(rows),
        out_specs=[big] * len(out_shape), compiler_params=_params(("parallel",), V7X_VMEM_LIMIT_BYTES),
    )(a, b, x, *rows)
    return res if gain is not None else (res[0], None)


def _mm_nn_rms_bwd(name, as_, bs, x, g, dx_out, tm=None):
    m, k = as_[0].shape
    n = bs[0].shape[1]
    tm = tm or _pick(m, (512, 256, 128))
    npair = len(as_)

    def body(*refs):
        a_refs, b_refs = refs[:npair], refs[npair:2 * npair]
        x_ref, g_ref, dxo_ref, dx_ref, dg_ref = refs[2 * npair:]
        dh = None
        for a_ref, b_ref in zip(a_refs, b_refs):
            p = jnp.dot(a_ref[...].astype(BF16), b_ref[...], preferred_element_type=F32)
            dh = p if dh is None else dh + p
        xv = x_ref[...]
        r = lax.rsqrt(jnp.mean(xv * xv, axis=-1, keepdims=True) + EPS)
        yv = xv * r
        dy = dh * g_ref[...]
        dx_ref[...] = dxo_ref[...] + r * (dy - yv * jnp.mean(dy * yv, axis=-1, keepdims=True))

        @pl.when(pl.program_id(0) == 0)
        def _():
            dg_ref[...] = jnp.zeros_like(dg_ref)

        dg_ref[...] += jnp.sum(dh * yv, axis=0, keepdims=True)

    big = pl.BlockSpec((tm, n), lambda i: (i, 0))
    row = pl.BlockSpec((1, n), lambda i: (0, 0))
    in_specs = [pl.BlockSpec((tm, k), lambda i: (i, 0))] * npair + [pl.BlockSpec((k, n), lambda i: (0, 0))] * npair
    return pl.pallas_call(
        body, name=name, out_shape=[jax.ShapeDtypeStruct((m, n), F32), jax.ShapeDtypeStruct((1, n), F32)],
        grid=(m // tm,), in_specs=in_specs + [big, row, big], out_specs=[big, row],
        compiler_params=_params(("arbitrary",), V7X_VMEM_LIMIT_BYTES),
    )(*as_, *bs, x, g, dx_out)


def _mm_tn(name, a, b, colsum_b=False, tm=None, tk=2048):
    t, ma = a.shape
    nb = b.shape[1]
    tm = tm or _pick(ma, (1408, 1152, 1024, 768, 512, 256, 128))
    tk = _pick(t, (tk, 256, 128))
    nk = t // tk

    def body(*refs):
        a_ref, b_ref, o_ref = refs[0], refs[1], refs[2]
        acc_ref = refs[-1]
        kk = pl.program_id(1)
        bv = b_ref[...]

        @pl.when(kk == 0)
        def _():
            acc_ref[...] = jnp.zeros_like(acc_ref)

        acc_ref[...] += lax.dot_general(a_ref[...].astype(BF16), bv.astype(BF16), (((0,), (0,)), ((), ())),
                                        preferred_element_type=F32)
        if colsum_b:
            s_ref = refs[3]

            @pl.when((kk == 0) & (pl.program_id(0) == 0))
            def _():
                s_ref[...] = jnp.zeros_like(s_ref)

            @pl.when(pl.program_id(0) == 0)
            def _():
                s_ref[...] += jnp.sum(bv.astype(F32), axis=0, keepdims=True)

        @pl.when(kk == nk - 1)
        def _():
            o_ref[...] = acc_ref[...].astype(o_ref.dtype)

    out_shape = [jax.ShapeDtypeStruct((ma, nb), BF16)]
    out_specs = [pl.BlockSpec((tm, nb), lambda i, kk: (i, 0))]
    if colsum_b:
        out_shape.append(jax.ShapeDtypeStruct((1, nb), F32))
        out_specs.append(pl.BlockSpec((1, nb), lambda i, kk: (0, 0)))
    res = pl.pallas_call(
        body, name=name, out_shape=out_shape, grid=(ma // tm, nk),
        in_specs=[pl.BlockSpec((tk, tm), lambda i, kk: (kk, i)), pl.BlockSpec((tk, nb), lambda i, kk: (kk, 0))],
        out_specs=out_specs, scratch_shapes=[pltpu.VMEM((tm, nb), F32)],
        compiler_params=_params(("arbitrary", "arbitrary"), V7X_VMEM_LIMIT_BYTES),
    )(a, b)
    return res if colsum_b else res[0]


def _mm_tn_pair(name, a1, a2, b, tk=1024):
    t, ma = a1.shape
    nb = b.shape[1]
    tm = _pick(ma, (1408, 1152, 1024, 768, 512, 256, 128))
    tk = _pick(t, (tk, 256, 128))
    nk = t // tk
    dims = (((0,), (0,)), ((), ()))

    def body(a1_ref, a2_ref, b_ref, o1_ref, o2_ref, acc1_ref, acc2_ref):
        kk = pl.program_id(1)
        bv = b_ref[...]

        @pl.when(kk == 0)
        def _():
            acc1_ref[...] = jnp.zeros_like(acc1_ref)
            acc2_ref[...] = jnp.zeros_like(acc2_ref)

        acc1_ref[...] += lax.dot_general(a1_ref[...], bv, dims, preferred_element_type=F32)
        acc2_ref[...] += lax.dot_general(a2_ref[...], bv, dims, preferred_element_type=F32)

        @pl.when(kk == nk - 1)
        def _():
            o1_ref[...] = acc1_ref[...].astype(BF16)
            o2_ref[...] = acc2_ref[...].astype(BF16)

    sa = pl.BlockSpec((tk, tm), lambda i, kk: (kk, i))
    so = pl.BlockSpec((tm, nb), lambda i, kk: (i, 0))
    return pl.pallas_call(
        body, name=name, out_shape=[jax.ShapeDtypeStruct((ma, nb), BF16)] * 2, grid=(ma // tm, nk),
        in_specs=[sa, sa, pl.BlockSpec((tk, nb), lambda i, kk: (kk, 0))], out_specs=[so, so],
        scratch_shapes=[pltpu.VMEM((tm, nb), F32)] * 2,
        compiler_params=_params(("arbitrary", "arbitrary"), V7X_VMEM_LIMIT_BYTES),
    )(a1, a2, b)


FFN_CHUNK = 256


def _ffn_down_bwd(name, dx, wd, gq, uq):
    t, d_model = dx.shape
    f = wd.shape[0]
    tm = _pick(t, (256, 128))
    ck = _pick(f, (FFN_CHUNK, 128))
    nt = (((1,), (1,)), ((), ()))
    tn = (((0,), (0,)), ((), ()))
    last = t // tm - 1

    def body(dx_ref, w_ref, g_ref, u_ref, dg_ref, du_ref, dw_ref, acc_ref):
        i = pl.program_id(0)

        @pl.when(i == 0)
        def _():
            acc_ref[...] = jnp.zeros_like(acc_ref)

        dxb = dx_ref[...].astype(BF16)
        for c0 in range(0, f, ck):
            cs = slice(c0, c0 + ck)
            dact = lax.dot_general(dxb, w_ref[cs, :], nt, preferred_element_type=F32)
            gf, uf = g_ref[:, cs].astype(F32), u_ref[:, cs].astype(F32)
            sg = _sigmoid(gf)
            dg_ref[:, cs] = (dact * uf * (sg * (1.0 + gf * (1.0 - sg)))).astype(BF16)
            du_ref[:, cs] = (dact * gf * sg).astype(BF16)
            act = (gf * sg * uf).astype(BF16)
            acc_ref[cs, :] += lax.dot_general(act, dxb, tn, preferred_element_type=F32)

        @pl.when(i == last)
        def _():
            dw_ref[...] = acc_ref[...].astype(BF16)

    wide = pl.BlockSpec((tm, f), lambda i: (i, 0))
    whole = pl.BlockSpec((f, d_model), lambda i: (0, 0))
    return pl.pallas_call(
        body, name=name,
        out_shape=[jax.ShapeDtypeStruct((t, f), BF16), jax.ShapeDtypeStruct((t, f), BF16),
                   jax.ShapeDtypeStruct((f, d_model), BF16)],
        grid=(t // tm,), in_specs=[pl.BlockSpec((tm, d_model), lambda i: (i, 0)), whole, wide, wide],
        out_specs=[wide, wide, whole], scratch_shapes=[pltpu.VMEM((f, d_model), F32)],
        compiler_params=_params(("arbitrary",), V7X_VMEM_LIMIT_BYTES),
    )(dx, wd, gq, uq)


def _rmsnorm_fwd(name, x, g):
    t, d = x.shape
    tr = _pick(t, (512, 256, 128))

    def body(x_ref, g_ref, h_ref):
        xv = x_ref[...]
        r = lax.rsqrt(jnp.mean(xv * xv, axis=-1, keepdims=True) + EPS)
        h_ref[...] = (xv * r * g_ref[...]).astype(BF16)

    return pl.pallas_call(
        body, name=name, out_shape=jax.ShapeDtypeStruct((t, d), BF16), grid=(t // tr,),
        in_specs=[pl.BlockSpec((tr, d), lambda i: (i, 0)), pl.BlockSpec((1, d), lambda i: (0, 0))],
        out_specs=pl.BlockSpec((tr, d), lambda i: (i, 0)), compiler_params=_params(("parallel",)),
    )(x, g)


def _loss_head(name, x, g, target):
    t, d = x.shape
    tr = _pick(t, (512, 256, 128))

    def body(x_ref, g_ref, t_ref, dx_ref, dg_ref, l_ref):
        xv = x_ref[...]
        r = lax.rsqrt(jnp.mean(xv * xv, axis=-1, keepdims=True) + EPS)
        yv = xv * r
        diff = yv * g_ref[...] - t_ref[...]
        dout = diff * (1.0 / d)
        dy = dout * g_ref[...]
        dx_ref[...] = r * (dy - yv * jnp.mean(dy * yv, axis=-1, keepdims=True))

        @pl.when(pl.program_id(0) == 0)
        def _():
            dg_ref[...] = jnp.zeros_like(dg_ref)
            l_ref[...] = jnp.zeros_like(l_ref)

        dg_ref[...] += jnp.sum(dout * yv, axis=0, keepdims=True)
        l_ref[...] += (0.5 / d) * jnp.sum(diff * diff, axis=0, keepdims=True)

    big = pl.BlockSpec((tr, d), lambda i: (i, 0))
    row = pl.BlockSpec((1, d), lambda i: (0, 0))
    return pl.pallas_call(
        body, name=name,
        out_shape=[jax.ShapeDtypeStruct((t, d), F32), jax.ShapeDtypeStruct((1, d), F32),
                   jax.ShapeDtypeStruct((1, d), F32)],
        grid=(t // tr,), in_specs=[big, row, big], out_specs=[big, row, row],
        compiler_params=_params(("arbitrary",)),
    )(x, g, target)


CONV_ROWS = 64
SUBLANES = 8


def _fill_window(win_ref, sh_ref, parts):
    rows = sh_ref.shape[2]
    for cb in range(win_ref.shape[0]):
        for r0, val in parts:
            win_ref[cb, r0:r0 + val.shape[0], :] = val[:, 128 * cb:128 * (cb + 1)]
        win_ref[cb, rows:rows + SUBLANES, :] = jnp.zeros((SUBLANES, 128), F32)
        for b in range(1, SUBLANES):
            sh_ref[b - 1, cb] = win_ref[cb, b:b + rows, :]


def _window_rows(win_ref, sh_ref, o, cb):
    b = o % SUBLANES
    if b == 0:
        return win_ref[cb, o:o + CONV_ROWS, :]
    return sh_ref[b - 1, cb, o - b:o - b + CONV_ROWS, :]


def _window_scratch(rows, c):
    return [pltpu.VMEM((c // 128, rows + SUBLANES, 128), F32), pltpu.VMEM((SUBLANES - 1, c // 128, rows, 128), F32)]


def _dwconv_fwd(name, glu, w_dw, b_dw, ln_g, ln_b):
    t, c = glu.shape
    tt = _pick(t, (256, 128))
    hb = tt // CONV_HALO

    def body(cur_ref, halo_ref, w_ref, b_ref, g_ref, be_ref, dw_ref, s_ref, win_ref, sh_ref):
        i = pl.program_id(0)
        halo = jnp.where(i > 0, halo_ref[...].astype(F32), 0.0)
        _fill_window(win_ref, sh_ref, [(0, halo), (CONV_HALO, cur_ref[...].astype(F32))])
        for r0 in range(0, tt, CONV_ROWS):
            for cb in range(c // 128):
                c0 = 128 * cb
                acc = jnp.zeros((CONV_ROWS, 128), F32) + b_ref[:, c0:c0 + 128]
                for k in range(CONV_WIDTH):
                    o = r0 + k + CONV_HALO - (CONV_WIDTH - 1)
                    acc = acc + w_ref[k:k + 1, c0:c0 + 128] * _window_rows(win_ref, sh_ref, o, cb)
                dw_ref[r0:r0 + CONV_ROWS, c0:c0 + 128] = acc
        u = dw_ref[...]
        mu = jnp.mean(u, axis=-1, keepdims=True)
        uc = u - mu
        rstd = lax.rsqrt(jnp.mean(uc * uc, axis=-1, keepdims=True) + EPS)
        z = uc * rstd * g_ref[...] + be_ref[...]
        s_ref[...] = (z * _sigmoid(z)).astype(BF16)

    big = pl.BlockSpec((tt, c), lambda i: (i, 0))
    row = pl.BlockSpec((1, c), lambda i: (0, 0))
    return pl.pallas_call(
        body, name=name, out_shape=[jax.ShapeDtypeStruct((t, c), F32), jax.ShapeDtypeStruct((t, c), BF16)],
        grid=(t // tt,),
        in_specs=[big, pl.BlockSpec((CONV_HALO, c), lambda i: (jnp.maximum(i * hb - 1, 0), 0)),
                  pl.BlockSpec((CONV_HALO, c), lambda i: (0, 0)), row, row, row],
        out_specs=[big, big], scratch_shapes=_window_scratch(tt + CONV_HALO, c),
        compiler_params=_params(("parallel",), V7X_VMEM_LIMIT_BYTES),
    )(glu, glu, w_dw, b_dw, ln_g, ln_b)


def _ln_silu_bwd(name, dx, w2, dw, ln_g, ln_b):
    t, c = dw.shape
    d_model = dx.shape[1]
    tr = _pick(t, (256, 128))

    def body(dx_ref, w_ref, dw_ref, g_ref, be_ref, o_ref, acc_ref):
        ds = lax.dot_general(dx_ref[...].astype(BF16), w_ref[...], (((1,), (1,)), ((), ())),
                             preferred_element_type=F32)
        u = dw_ref[...]
        mu = jnp.mean(u, axis=-1, keepdims=True)
        uc = u - mu
        rstd = lax.rsqrt(jnp.mean(uc * uc, axis=-1, keepdims=True) + EPS)
        xh = uc * rstd
        z = xh * g_ref[...] + be_ref[...]
        sg = _sigmoid(z)
        dz = ds * (sg * (1.0 + z * (1.0 - sg)))
        dxh = dz * g_ref[...]
        du = rstd * (dxh - jnp.mean(dxh, axis=-1, keepdims=True) - xh * jnp.mean(dxh * xh, axis=-1, keepdims=True))
        o_ref[...] = du

        @pl.when(pl.program_id(0) == 0)
        def _():
            acc_ref[...] = jnp.zeros_like(acc_ref)

        acc_ref[0:1, :] += jnp.sum(dz * xh, axis=0, keepdims=True)
        acc_ref[1:2, :] += jnp.sum(dz, axis=0, keepdims=True)
        acc_ref[2:3, :] += jnp.sum(du, axis=0, keepdims=True)

    big = pl.BlockSpec((tr, c), lambda i: (i, 0))
    row = pl.BlockSpec((1, c), lambda i: (0, 0))
    return pl.pallas_call(
        body, name=name, out_shape=[jax.ShapeDtypeStruct((t, c), F32), jax.ShapeDtypeStruct((8, c), F32)],
        grid=(t // tr,),
        in_specs=[pl.BlockSpec((tr, d_model), lambda i: (i, 0)), pl.BlockSpec((c, d_model), lambda i: (0, 0)),
                  big, row, row],
        out_specs=[big, pl.BlockSpec((8, c), lambda i: (0, 0))],
        compiler_params=_params(("arbitrary",), V7X_VMEM_LIMIT_BYTES),
    )(dx, w2, dw, ln_g, ln_b)


def _dwconv_bwd(name, ddw, a, gt, w_dw):
    t, c = ddw.shape
    tt = _pick(t, (256, 128))
    hb = tt // CONV_HALO
    last = t // tt - 1
    back = CONV_WIDTH - 1

    def body(d_ref, dn_ref, a_ref, ap_ref, g_ref, gp_ref, w_ref, du_ref, dwk_ref, db_ref,
             wd_ref, shd_ref, wg_ref, shg_ref, dg_ref, dwk8_ref):
        i = pl.program_id(0)
        _fill_window(wd_ref, shd_ref, [(0, d_ref[...]), (tt, jnp.where(i < last, dn_ref[...], 0.0))])
        glu_prev = ap_ref[...].astype(F32) * _sigmoid(gp_ref[...].astype(F32))
        av = a_ref[...].astype(F32)
        sg = _sigmoid(g_ref[...].astype(F32))
        _fill_window(wg_ref, shg_ref, [(0, jnp.where(i > 0, glu_prev, 0.0)), (CONV_HALO, av * sg)])

        @pl.when(i == 0)
        def _():
            dwk8_ref[...] = jnp.zeros_like(dwk8_ref)
            db_ref[...] = jnp.zeros_like(db_ref)

        for r0 in range(0, tt, CONV_ROWS):
            for cb in range(c // 128):
                c0 = 128 * cb
                dcur = wd_ref[cb, r0:r0 + CONV_ROWS, :]
                acc = jnp.zeros((CONV_ROWS, 128), F32)
                for k in range(CONV_WIDTH):
                    acc = acc + w_ref[k:k + 1, c0:c0 + 128] * _window_rows(wd_ref, shd_ref, r0 + back - k, cb)
                    p = dcur * _window_rows(wg_ref, shg_ref, r0 + k + CONV_HALO - back, cb)
                    s8 = p[0:SUBLANES]
                    for q in range(SUBLANES, CONV_ROWS, SUBLANES):
                        s8 = s8 + p[q:q + SUBLANES]
                    dwk8_ref[SUBLANES * k:SUBLANES * (k + 1), c0:c0 + 128] += s8
                dg_ref[r0:r0 + CONV_ROWS, c0:c0 + 128] = acc

        @pl.when(i == last)
        def _():
            for k in range(CONV_WIDTH):
                dwk_ref[k:k + 1, :] = jnp.sum(dwk8_ref[SUBLANES * k:SUBLANES * (k + 1), :], axis=0, keepdims=True)
            dwk_ref[CONV_WIDTH:, :] = jnp.zeros((CONV_HALO - CONV_WIDTH, c), F32)
        dglu = dg_ref[...]
        da = dglu * sg
        dgate = dglu * av * sg * (1.0 - sg)
        du_ref[:, 0:c] = da.astype(BF16)
        du_ref[:, c:] = dgate.astype(BF16)
        db_ref[:, 0:c] += jnp.sum(da, axis=0, keepdims=True)
        db_ref[:, c:] += jnp.sum(dgate, axis=0, keepdims=True)

    big = pl.BlockSpec((tt, c), lambda i: (i, 0))
    prev = pl.BlockSpec((CONV_HALO, c), lambda i: (jnp.maximum(i * hb - 1, 0), 0))
    nxt = pl.BlockSpec((CONV_HALO, c), lambda i: (jnp.minimum((i + 1) * hb, t // CONV_HALO - 1), 0))
    return pl.pallas_call(
        body, name=name,
        out_shape=[jax.ShapeDtypeStruct((t, 2 * c), BF16), jax.ShapeDtypeStruct((CONV_HALO, c), F32),
                   jax.ShapeDtypeStruct((1, 2 * c), F32)],
        grid=(t // tt,),
        in_specs=[big, nxt, big, prev, big, prev, pl.BlockSpec((CONV_HALO, c), lambda i: (0, 0))],
        out_specs=[pl.BlockSpec((tt, 2 * c), lambda i: (i, 0)), pl.BlockSpec((CONV_HALO, c), lambda i: (0, 0)),
                   pl.BlockSpec((1, 2 * c), lambda i: (0, 0))],
        scratch_shapes=_window_scratch(tt + CONV_HALO, c) + _window_scratch(tt + CONV_HALO, c)
        + [pltpu.VMEM((tt, c), F32), pltpu.VMEM((SUBLANES * CONV_HALO, c), F32)],
        compiler_params=_params(("arbitrary",), V7X_VMEM_LIMIT_BYTES),
    )(ddw, ddw, a, a, gt, gt, w_dw)


def _bucket_tables():
    i = np.arange(N_BACK)[:, None]
    j = np.arange(2 * N_BACK)[None, :]
    dist = i + N_BACK - j
    valid = (dist >= 0) & (dist <= N_BACK)
    max_exact = N_BUCKETS // 2
    out = []
    for d in DILATIONS:
        n = np.maximum(dist * d, 0)
        nf = np.maximum(n, 1).astype(np.float32)
        large = max_exact + (np.log(nf / np.float32(max_exact)) / np.float32(math.log(REL_MAX_DISTANCE / max_exact))
                             * np.float32(N_BUCKETS - max_exact)).astype(np.int32)
        large = np.minimum(large, N_BUCKETS - 1)
        out.append(np.where(valid, np.where(n < max_exact, n, large), -1))
    return np.stack(out).astype(np.int32)


def _bias_build(name, rel_bias, buckets):
    def body(tbl_ref, bk_ref, o_ref):
        g = pl.program_id(0)
        bk = bk_ref[0]
        for h in range(HEADS_PER_GROUP):
            acc = jnp.zeros(bk.shape, F32)
            for b in range(N_BUCKETS):
                acc = jnp.where(bk == b, tbl_ref[b, g * HEADS_PER_GROUP + h], acc)
            o_ref[h] = jnp.where(bk < 0, NEG_INF, acc)

    return pl.pallas_call(
        body, name=name, out_shape=jax.ShapeDtypeStruct((N_HEADS, N_BACK, 2 * N_BACK), F32), grid=(N_GROUPS,),
        in_specs=[pl.BlockSpec(memory_space=pltpu.SMEM), pl.BlockSpec((1, N_BACK, 2 * N_BACK), lambda g: (g, 0, 0))],
        out_specs=pl.BlockSpec((HEADS_PER_GROUP, N_BACK, 2 * N_BACK), lambda g: (g, 0, 0)),
        compiler_params=_params(("arbitrary",)),
    )(rel_bias, buckets)


def _bias_grad(name, dbs, buckets):
    nd = len(dbs)

    def body(*refs):
        bk = refs[nd][0]
        o_ref = refs[nd + 1]
        lane = lax.broadcasted_iota(jnp.int32, (1, 128), 1)
        db = [sum(r[h] for r in refs[:nd]) for h in range(HEADS_PER_GROUP)]
        for b in range(N_BUCKETS):
            row = jnp.zeros((1, 128), F32)
            for h in range(HEADS_PER_GROUP):
                s = jnp.sum(jnp.where(bk == b, db[h], 0.0), axis=0, keepdims=True)
                s = jnp.sum(s, axis=1, keepdims=True)
                row = jnp.where(lane // 32 == h, s, row)
            o_ref[0, b:b + 1, :] = row

    spec = pl.BlockSpec((HEADS_PER_GROUP, N_BACK, 2 * N_BACK), lambda g: (g, 0, 0))
    return pl.pallas_call(
        body, name=name, out_shape=jax.ShapeDtypeStruct((N_GROUPS, N_BUCKETS, 128), F32), grid=(N_GROUPS,),
        in_specs=[spec] * nd + [pl.BlockSpec((1, N_BACK, 2 * N_BACK), lambda g: (g, 0, 0))],
        out_specs=pl.BlockSpec((1, N_BUCKETS, 128), lambda g: (g, 0, 0)), compiler_params=_params(("arbitrary",)),
    )(*dbs, buckets)


def _head_cols(h):
    return slice(h * HEAD_DIM, (h + 1) * HEAD_DIM)


def _store_dilated(o_ref, tile_ref, val, d):
    nlb, rows, _ = tile_ref.shape
    width = 128 * nlb
    for lb in range(nlb):
        tile_ref[lb] = val[:, 128 * lb:128 * (lb + 1)]
    for r in range(d):
        for lb in range(nlb):
            c0 = r * width + 128 * lb
            o_ref[:, c0:c0 + 128] = tile_ref[lb, pl.ds(r, rows // d, stride=d), :].astype(o_ref.dtype)


def _attn_qkv_fwd(name, h, wq):
    t, k = h.shape
    tm = _pick(t, (512, 256))
    width = wq.shape[1]

    def body(h_ref, w0, w1, w2, o0, o1, o2, tile_ref):
        hv = h_ref[...]
        for g, (w_ref, o_ref) in enumerate(zip((w0, w1, w2), (o0, o1, o2))):
            acc = lax.dot_general(hv, w_ref[...], (((1,), (1,)), ((), ())), preferred_element_type=F32)
            if DILATIONS[g] == 1:
                o_ref[...] = acc.astype(BF16)
            else:
                _store_dilated(o_ref, tile_ref, acc, DILATIONS[g])

    return pl.pallas_call(
        body, name=name,
        out_shape=[jax.ShapeDtypeStruct((t // d, d * width), BF16) for d in DILATIONS], grid=(t // tm,),
        in_specs=[pl.BlockSpec((tm, k), lambda i: (i, 0))] + [pl.BlockSpec((width, k), lambda i: (0, 0))] * 3,
        out_specs=[pl.BlockSpec((tm // d, d * width), lambda i: (i, 0)) for d in DILATIONS],
        scratch_shapes=[pltpu.VMEM((width // 128, tm, 128), F32)],
        compiler_params=_params(("parallel",), V7X_VMEM_LIMIT_BYTES),
    )(h, wq[0], wq[1], wq[2])


def _attn_fwd(name, qkv, bias, g):
    d = DILATIONS[g]
    tq = qkv.shape[0]
    t = tq * d
    nblk = 3
    nsub = _pick(tq // N_BACK, (4, 2))
    scale = HEAD_DIM ** -0.5

    def body(q_ref, kp_ref, kc_ref, vp_ref, vc_ref, b_ref, o_ref, l_ref):
        m2 = pl.program_id(1)
        col = lax.broadcasted_iota(jnp.int32, (N_BACK, 2 * N_BACK), 1)
        lane = lax.broadcasted_iota(jnp.int32, (N_BACK, 128), 1)
        for sub in range(nsub):
            rows = slice(N_BACK * sub, N_BACK * (sub + 1))
            both = slice(N_BACK * (sub - 1), N_BACK * (sub + 1))
            lse_tile = jnp.zeros((N_BACK, 128), F32)
            outs = []
            for h in range(HEADS_PER_GROUP):
                hc = _head_cols(h)
                if sub == 0:
                    kk = jnp.concatenate([kp_ref[:, hc], kc_ref[0:N_BACK, hc]], axis=0)
                    vv = jnp.concatenate([vp_ref[:, hc], vc_ref[0:N_BACK, hc]], axis=0)
                else:
                    kk, vv = kc_ref[both, hc], vc_ref[both, hc]
                s = lax.dot_general(q_ref[rows, hc], kk, (((1,), (1,)), ((), ())), preferred_element_type=F32)
                s = s * scale + b_ref[h]
                if sub == 0:
                    s = jnp.where((col >= N_BACK) | (m2 > 0), s, NEG_INF)
                m = jnp.max(s, axis=-1, keepdims=True)
                p = jnp.exp(s - m)
                den = jnp.sum(p, axis=-1, keepdims=True)
                outs.append(jnp.dot(p.astype(BF16), vv, preferred_element_type=F32) / den)
                lse_tile = jnp.where(lane // 32 == h, m + jnp.log(den), lse_tile)
            o_ref[rows, :] = jnp.concatenate(outs, axis=1)
            l_ref[rows, :] = lse_tile

    def blk(part, prev):
        if prev:
            return pl.BlockSpec((N_BACK, GROUP_COLS), lambda r, n: (jnp.maximum(nsub * n - 1, 0), r * nblk + part))
        return pl.BlockSpec((nsub * N_BACK, GROUP_COLS), lambda r, n: (n, r * nblk + part))

    o, l = pl.pallas_call(
        body, name=name,
        out_shape=[jax.ShapeDtypeStruct((tq, d * GROUP_COLS), F32), jax.ShapeDtypeStruct((tq, d * 128), F32)],
        grid=(d, tq // (nsub * N_BACK)),
        in_specs=[blk(0, False), blk(1, True), blk(1, False), blk(2, True), blk(2, False),
                  pl.BlockSpec((HEADS_PER_GROUP, N_BACK, 2 * N_BACK), lambda r, n: (g, 0, 0))],
        out_specs=[pl.BlockSpec((nsub * N_BACK, GROUP_COLS), lambda r, n: (n, r)),
                   pl.BlockSpec((nsub * N_BACK, 128), lambda r, n: (n, r))],
        compiler_params=_params(("parallel", "parallel")),
    )(qkv, qkv, qkv, qkv, qkv, bias)
    return o, l


def _group_weights(l_refs, h):
    ls = [l_ref[:, 32 * h:32 * h + 1] for l_ref in l_refs]
    m = jnp.maximum(jnp.maximum(ls[0], ls[1]), ls[2])
    es = [jnp.exp(l - m) for l in ls]
    tot = es[0] + es[1] + es[2]
    return [e / tot for e in es]


def _load_dilated(v_ref, n_ref, tile_ref, d):
    nlb, rows, _ = tile_ref.shape
    width = 128 * nlb
    for r in range(d):
        for lb in range(nlb):
            c0 = r * width + 128 * lb
            tile_ref[lb, pl.ds(r, rows // d, stride=d), :] = v_ref[:, c0:c0 + 128]
    n_ref[...] = jnp.concatenate([tile_ref[lb] for lb in range(nlb)], axis=1)


def _attn_merge_out(name, os_, ls, wot, x, gain):
    t, d_model = x.shape
    tr = _pick(t, (512, 256))

    def body(o0, o1v, o2v, l0, l1v, l2v, w_ref, x_ref, g_ref, om_ref, out_ref, h_ref, o1, o2, l1, l2,
             otile_ref, ltile_ref):
        _load_dilated(o1v, o1, otile_ref, DILATIONS[1])
        _load_dilated(o2v, o2, otile_ref, DILATIONS[2])
        _load_dilated(l1v, l1, ltile_ref, DILATIONS[1])
        _load_dilated(l2v, l2, ltile_ref, DILATIONS[2])
        o_refs = (o0, o1, o2)
        pieces = [[None] * HEADS_PER_GROUP for _ in range(N_GROUPS)]
        for h in range(HEADS_PER_GROUP):
            al = _group_weights((l0, l1, l2), h)
            for g in range(N_GROUPS):
                pieces[g][h] = o_refs[g][:, _head_cols(h)] * al[g]
        om = jnp.concatenate([p for row in pieces for p in row], axis=1).astype(BF16)
        om_ref[...] = om
        out = x_ref[...] + lax.dot_general(om, w_ref[...], (((1,), (1,)), ((), ())), preferred_element_type=F32)
        out_ref[...] = out
        h_ref[...] = _rms_rows(out, g_ref[...])

    so = pl.BlockSpec((tr, GROUP_COLS), lambda i: (i, 0))
    sl = pl.BlockSpec((tr, 128), lambda i: (i, 0))
    sx = pl.BlockSpec((tr, d_model), lambda i: (i, 0))
    views = [pl.BlockSpec((tr // d, d * GROUP_COLS), lambda i: (i, 0)) for d in DILATIONS]
    views += [pl.BlockSpec((tr // d, d * 128), lambda i: (i, 0)) for d in DILATIONS]
    res = pl.pallas_call(
        body, name=name,
        out_shape=[jax.ShapeDtypeStruct((t, D_ATTN), BF16), jax.ShapeDtypeStruct((t, d_model), F32),
                   jax.ShapeDtypeStruct((t, d_model), BF16)]
        + [jax.ShapeDtypeStruct((t, GROUP_COLS), F32)] * 2 + [jax.ShapeDtypeStruct((t, 128), F32)] * 2,
        grid=(t // tr,),
        in_specs=views + [pl.BlockSpec((d_model, D_ATTN), lambda i: (0, 0)), sx,
                          pl.BlockSpec((1, d_model), lambda i: (0, 0))],
        out_specs=[pl.BlockSpec((tr, D_ATTN), lambda i: (i, 0)), sx, sx, so, so, sl, sl],
        scratch_shapes=[pltpu.VMEM((GROUP_COLS // 128, tr, 128), F32), pltpu.VMEM((1, tr, 128), F32)],
        compiler_params=_params(("parallel",), V7X_VMEM_LIMIT_BYTES),
    )(*os_, *ls, wot, x, gain)
    om, x_new, h_next, o1, o2, l1, l2 = res
    return om, x_new, h_next, [os_[0], o1, o2], [ls[0], l1, l2]


def _attn_bwd_prep(name, dx, wot, os_, ls):
    t, d_model = dx.shape
    tr = _pick(t, (512, 256))

    def body(dx_ref, w_ref, o0, o1, o2, l0, l1, l2, d0, d1, d2, c0, c1, c2, dtile_ref, ctile_ref):
        o_refs, d_refs, c_refs = (o0, o1, o2), (d0, d1, d2), (c0, c1, c2)
        d_out = jnp.dot(dx_ref[...].astype(BF16), w_ref[...], preferred_element_type=F32)
        lane = lax.broadcasted_iota(jnp.int32, (tr, 128), 1)
        dos = [[None] * HEADS_PER_GROUP for _ in range(N_GROUPS)]
        cs = [jnp.zeros((tr, 128), F32) for _ in range(N_GROUPS)]
        for h in range(HEADS_PER_GROUP):
            al = _group_weights((l0, l1, l2), h)
            tot = jnp.zeros((tr, 1), F32)
            for g in range(N_GROUPS):
                dv = d_out[:, g * GROUP_COLS + h * HEAD_DIM:g * GROUP_COLS + (h + 1) * HEAD_DIM]
                tot = tot + al[g] * jnp.sum(dv * o_refs[g][:, _head_cols(h)], axis=-1, keepdims=True)
                dos[g][h] = dv * al[g]
            for g in range(N_GROUPS):
                cs[g] = jnp.where(lane // 32 == h, -al[g] * tot, cs[g])
        for g in range(N_GROUPS):
            do_g = jnp.concatenate(dos[g], axis=1)
            if DILATIONS[g] == 1:
                d_refs[g][...] = do_g.astype(BF16)
                c_refs[g][...] = cs[g]
            else:
                _store_dilated(d_refs[g], dtile_ref, do_g, DILATIONS[g])
                _store_dilated(c_refs[g], ctile_ref, cs[g], DILATIONS[g])

    so = pl.BlockSpec((tr, GROUP_COLS), lambda i: (i, 0))
    sl = pl.BlockSpec((tr, 128), lambda i: (i, 0))
    res = pl.pallas_call(
        body, name=name,
        out_shape=[jax.ShapeDtypeStruct((t // d, d * GROUP_COLS), BF16) for d in DILATIONS]
        + [jax.ShapeDtypeStruct((t // d, d * 128), F32) for d in DILATIONS],
        grid=(t // tr,),
        in_specs=[pl.BlockSpec((tr, d_model), lambda i: (i, 0)), pl.BlockSpec((d_model, D_ATTN), lambda i: (0, 0))]
        + [so] * 3 + [sl] * 3,
        out_specs=[pl.BlockSpec((tr // d, d * GROUP_COLS), lambda i: (i, 0)) for d in DILATIONS]
        + [pl.BlockSpec((tr // d, d * 128), lambda i: (i, 0)) for d in DILATIONS],
        scratch_shapes=[pltpu.VMEM((GROUP_COLS // 128, tr, 128), F32), pltpu.VMEM((1, tr, 128), F32)],
        compiler_params=_params(("parallel",), V7X_VMEM_LIMIT_BYTES),
    )(dx, wot, *os_, *ls)
    return res[:3], res[3:]


def _attn_bwd(name, qkv, do, lse, cterm, bias, g):
    d = DILATIONS[g]
    tq = qkv.shape[0]
    t = tq * d
    nb = tq // N_BACK
    nblk = 3
    scale = HEAD_DIM ** -0.5
    nt = (((1,), (1,)), ((), ()))
    tn = (((0,), (0,)), ((), ()))

    def body(q2, qx, kp, k2, vp, v2, do2, dox, l2, lx, c2, cx, b_ref, dqkv_ref, db_ref):
        m2 = pl.program_id(1)

        @pl.when((m2 == 0) & (pl.program_id(0) == 0))
        def _():
            db_ref[...] = jnp.zeros_like(db_ref)

        row2 = lax.broadcasted_iota(jnp.int32, (2 * N_BACK, N_BACK), 0)
        lo, hi = slice(0, N_BACK), slice(N_BACK, 2 * N_BACK)
        for sub in range(2):
            rows = hi if sub else lo
            has_prev = True if sub else m2 > 0
            has_next = 2 * m2 + 2 < nb if sub else True
            on_ac = (row2 < N_BACK) | has_next
            dqs, dks, dvs = [], [], []
            for h in range(HEADS_PER_GROUP):
                hc = _head_cols(h)
                st = slice(32 * h, 32 * h + 1)
                b_prev, b_same = b_ref[h, :, 0:N_BACK], b_ref[h, :, N_BACK:]
                q0, k1, v1, d0 = q2[rows, hc], k2[rows, hc], v2[rows, hc], do2[rows, hc]
                l0, c0 = l2[rows, st], c2[rows, st]
                if sub:
                    k0, v0 = k2[lo, hc], v2[lo, hc]
                    q_ac = jnp.concatenate([q0, qx[:, hc]], axis=0)
                    d_ac = jnp.concatenate([d0, dox[:, hc]], axis=0)
                    l_ac = jnp.concatenate([l0, lx[:, st]], axis=0)
                    c_ac = jnp.concatenate([c0, cx[:, st]], axis=0)
                else:
                    k0, v0 = kp[:, hc], vp[:, hc]
                    q_ac, d_ac, l_ac, c_ac = q2[:, hc], do2[:, hc], l2[:, st], c2[:, st]
                s_ac = (lax.dot_general(q_ac, k1, nt, preferred_element_type=F32) * scale
                        + jnp.concatenate([b_same, b_prev], axis=0))
                p_ac = jnp.where(on_ac, jnp.exp(s_ac - l_ac), 0.0)
                ds_ac = p_ac * (lax.dot_general(d_ac, v1, nt, preferred_element_type=F32) + c_ac)
                s_b = lax.dot_general(q0, k0, nt, preferred_element_type=F32) * scale + b_prev
                p_b = jnp.where(has_prev, jnp.exp(s_b - l0), 0.0)
                ds_b = p_b * (lax.dot_general(d0, v0, nt, preferred_element_type=F32) + c0)
                ds_a = ds_ac[0:N_BACK]
                dqs.append(scale * jnp.dot(jnp.concatenate([ds_b, ds_a], axis=1).astype(BF16),
                                           jnp.concatenate([k0, k1], axis=0), preferred_element_type=F32))
                dks.append(scale * lax.dot_general(ds_ac.astype(BF16), q_ac, tn, preferred_element_type=F32))
                dvs.append(lax.dot_general(p_ac.astype(BF16), d_ac, tn, preferred_element_type=F32))
                db_ref[h, :, 0:N_BACK] += ds_b
                db_ref[h, :, N_BACK:] += ds_a
            dqkv_ref[rows, :] = jnp.concatenate(dqs + dks + dvs, axis=1).astype(BF16)

    def blk(width, which, col):
        if which == "prev":
            return pl.BlockSpec((N_BACK, width), lambda r, n: (jnp.maximum(2 * n - 1, 0), col(r)))
        if which == "next":
            return pl.BlockSpec((N_BACK, width), lambda r, n: (jnp.minimum(2 * n + 2, nb - 1), col(r)))
        return pl.BlockSpec((2 * N_BACK, width), lambda r, n: (n, col(r)))

    def qkv_blk(part, which):
        return blk(GROUP_COLS, which, lambda r: r * nblk + part)

    def grp_blk(width, which):
        return blk(width, which, lambda r: r)

    qv, dov, lv, cv = qkv, do, lse, cterm
    dqkv_g, db = pl.pallas_call(
        body, name=name,
        out_shape=[jax.ShapeDtypeStruct((tq, d * 3 * GROUP_COLS), BF16),
                   jax.ShapeDtypeStruct((HEADS_PER_GROUP, N_BACK, 2 * N_BACK), F32)],
        grid=(d, nb // 2),
        in_specs=[qkv_blk(0, "same"), qkv_blk(0, "next"), qkv_blk(1, "prev"), qkv_blk(1, "same"),
                  qkv_blk(2, "prev"), qkv_blk(2, "same"), grp_blk(GROUP_COLS, "same"), grp_blk(GROUP_COLS, "next"),
                  grp_blk(128, "same"), grp_blk(128, "next"), grp_blk(128, "same"), grp_blk(128, "next"),
                  pl.BlockSpec((HEADS_PER_GROUP, N_BACK, 2 * N_BACK), lambda r, n: (g, 0, 0))],
        out_specs=[pl.BlockSpec((2 * N_BACK, 3 * GROUP_COLS), lambda r, n: (n, r)),
                   pl.BlockSpec((HEADS_PER_GROUP, N_BACK, 2 * N_BACK), lambda r, n: (0, 0, 0))],
        compiler_params=_params(("arbitrary", "arbitrary")),
    )(qv, qv, qv, qv, qv, qv, dov, dov, lv, lv, cv, cv, bias)
    return dqkv_g.reshape(t, 3 * GROUP_COLS), db


def _row(v):
    return v.reshape(1, -1)


def _glu_epi(accs, extras, rows):
    a = (accs[0] + rows[0]).astype(BF16)
    gt = (accs[1] + rows[1]).astype(BF16)
    return a, gt, a.astype(F32) * _sigmoid(gt.astype(F32))


def _swiglu_epi(accs, extras, rows):
    gq, uq = accs[0].astype(BF16), accs[1].astype(BF16)
    gf = gq.astype(F32)
    return gq, uq, gf * _sigmoid(gf) * uq.astype(F32)


def _group_rows(w):
    parts = [w[p * D_ATTN:(p + 1) * D_ATTN].reshape(N_GROUPS, GROUP_COLS, -1) for p in range(3)]
    return jnp.concatenate(parts, axis=1)


def _ungroup_rows(wg):
    return jnp.concatenate([wg[g][p * GROUP_COLS:(p + 1) * GROUP_COLS] for p in range(3) for g in range(N_GROUPS)],
                           axis=0)


def _local_step(x, target, sm, depth, fetch, emit):
    d_model = x.shape[1]
    buckets = jnp.asarray(_bucket_tables())
    bias = _bias_build("bias_build", sm["rel_bias"], buckets)
    saved = []
    h = _rmsnorm_fwd("rms_mix_fwd0", x, _row(sm["norm_mix"][0]))
    for i in range(depth):
        j = i // 2
        rec = {"x_mix": x}
        wm = fetch(2 * i, x)
        rec.update(h_mix=h, wm=wm)
        ffn_gain = _row(sm["norm_ffn"][i])
        if i % 2 == 0:
            c = wm["w1t"].shape[0] // 2
            tn = _pick(c, (1024, 512, 256, 128))
            b1 = sm["conv_b_pw1"][j]
            a, gt, glu = _mm_nt(f"conv_pw1_fwd{j}", h, [wm["w1t"]] * 2, [0, c // tn], c, _glu_epi, (BF16,) * 3,
                                rows=[_row(b1[:c]), _row(b1[c:])], tn=tn)
            wm.update(fetch(2 * i, a, 1))
            dw, s = _dwconv_fwd(f"dwconv_fwd{j}", glu, wm["wdw"], _row(sm["conv_b_dw"][j]),
                                _row(sm["conv_ln_g"][j]), _row(sm["conv_ln_b"][j]))
            x, h2 = _project_residual(f"conv_pw2_fwd{j}", s, wm["w2"], x, bias=_row(sm["conv_b_pw2"][j]),
                                      gain=ffn_gain)
            rec.update(a=a, gt=gt, dw=dw, s=s)
        else:
            wq = _group_rows(wm["wqkvt"])
            qkv = _attn_qkv_fwd(f"attn_qkv_fwd{j}", h, wq)
            og = [_attn_fwd(f"attn_fwd{j}_{g}", qkv[g], bias, g) for g in range(N_GROUPS)]
            om, x, h2, os_, ls = _attn_merge_out(f"attn_out_fwd{j}", [o for o, _ in og], [l for _, l in og],
                                                 wm["wot"], x, ffn_gain)
            rec.update(qkv=qkv, os=os_, ls=ls, lse_views=[l for _, l in og], om=om, wq=wq)
        rec["x_ffn"] = x
        wf = fetch(2 * i + 1, x)
        f = wf["wd"].shape[0]
        gq, uq, act = _mm_nt(f"ffn_up_fwd{i}", h2, [wf["wgt"], wf["wut"]], [0, 0], f, _swiglu_epi, (BF16,) * 3)
        x, h = _project_residual(f"ffn_down_fwd{i}", act, wf["wd"], x,
                                 gain=_row(sm["norm_mix"][i + 1]) if i + 1 < depth else None)
        rec.update(h_ffn=h2, gq=gq, uq=uq, wf=wf)
        saved.append(rec)

    dx, g_final, loss_cols = _loss_head("loss_head", x, _row(sm["final_norm"]), target)

    g_mix, g_ffn = [None] * depth, [None] * depth
    nconv = (depth + 1) // 2
    g_b1, g_bdw, g_lng, g_lnb, g_b2 = ([None] * nconv for _ in range(5))
    dbias = []
    for i in reversed(range(depth)):
        j = i // 2
        rec = saved[i]
        wm, wf = rec["wm"], rec["wf"]
        dgate, dup, dwd = _ffn_down_bwd(f"ffn_down_bwd{i}", dx, wf["wd"], rec["gq"], rec["uq"])
        gf = {"wd": dwd}
        gf["wgt"], gf["wut"] = _mm_tn_pair(f"ffn_gate_up_dw{i}", dgate, dup, rec["h_ffn"])
        dx, g_ffn[i] = _mm_nn_rms_bwd(f"ffn_up_bwd{i}", [dgate, dup], [wf["wgt"], wf["wut"]], rec["x_ffn"],
                                      _row(sm["norm_ffn"][i]), dx)
        dx = emit(2 * i + 1, gf, dx)
        if i % 2 == 0:
            c = wm["w2"].shape[0]
            gm = {}
            gm["w2"], g_b2[j] = _mm_tn(f"conv_pw2_dw{j}", rec["s"], dx, colsum_b=True)
            ddw, sums = _ln_silu_bwd(f"conv_pw2_bwd{j}", dx, wm["w2"], rec["dw"], _row(sm["conv_ln_g"][j]),
                                     _row(sm["conv_ln_b"][j]))
            g_lng[j], g_lnb[j], g_bdw[j] = sums[0], sums[1], sums[2]
            du, dwk, db1 = _dwconv_bwd(f"dwconv_bwd{j}", ddw, rec["a"], rec["gt"], wm["wdw"])
            gm["wdw"] = dwk[:CONV_WIDTH]
            g_b1[j] = db1[0]
            gm["w1t"] = _mm_tn(f"conv_pw1_dw{j}", du, rec["h_mix"])
            dh_terms = ([du], [wm["w1t"]])
        else:
            gm = {"wot": _mm_tn(f"attn_out_dw{j}", dx, rec["om"])}
            dos, cs = _attn_bwd_prep(f"attn_out_bwd{j}", dx, wm["wot"], rec["os"], rec["ls"])
            back = [_attn_bwd(f"attn_bwd{j}_{g}", rec["qkv"][g], dos[g], rec["lse_views"][g], cs[g], bias, g)
                    for g in range(N_GROUPS)]
            dqkv = [b[0] for b in back]
            dbias.append(jnp.concatenate([b[1] for b in back], axis=0))
            dh_terms = (dqkv, [rec["wq"][g] for g in range(N_GROUPS)])
            gm["wqkvt"] = _ungroup_rows([_mm_tn(f"attn_qkv_dw{j}_{g}", dqkv[g], rec["h_mix"])
                                         for g in range(N_GROUPS)])
        dx, g_mix[i] = _mm_nn_rms_bwd(f"mix_in_bwd{i}", dh_terms[0], dh_terms[1], rec["x_mix"],
                                      _row(sm["norm_mix"][i]), dx)
        dx = emit(2 * i, gm, dx)

    gb = _bias_grad("bias_grad", dbias, buckets)
    g_rel = jnp.transpose(gb[:, :, ::32], (1, 0, 2)).reshape(N_BUCKETS, N_HEADS)
    gsm = {
        "norm_mix": jnp.concatenate(g_mix, axis=0), "norm_ffn": jnp.concatenate(g_ffn, axis=0),
        "final_norm": g_final[0], "conv_b_pw1": jnp.stack(g_b1), "conv_b_dw": jnp.stack(g_bdw),
        "conv_ln_g": jnp.stack(g_lng), "conv_ln_b": jnp.stack(g_lnb),
        "conv_b_pw2": jnp.concatenate(g_b2, axis=0), "rel_bias": g_rel,
    }
    return loss_cols, dx, gsm


SMALL = ("norm_mix", "norm_ffn", "final_norm", "conv_b_pw1", "conv_b_dw", "conv_ln_g", "conv_ln_b", "conv_b_pw2",
         "rel_bias")
SHARDED = (("conv_w_pw1", "w1t", True), ("conv_w_pw2", "w2", False), ("attn_w_qkv", "wqkvt", True),
           ("attn_w_o", "wot", True), ("ffn_w_gate", "wgt", True), ("ffn_w_up", "wut", True),
           ("ffn_w_down", "wd", False))
ORDER = ("norm_mix", "norm_ffn", "final_norm", "conv_w_pw1", "conv_b_pw1", "conv_w_dw", "conv_b_dw", "conv_ln_g",
         "conv_ln_b", "conv_w_pw2", "conv_b_pw2", "attn_w_qkv", "attn_w_o", "rel_bias", "ffn_w_gate", "ffn_w_up",
         "ffn_w_down")
PACK_LANES = 128
PACK_ROW_TILE = 8


def _pack_small(vals):
    flat = jnp.concatenate([vals[n].reshape(-1) for n in SMALL])
    per_tile = PACK_LANES * PACK_ROW_TILE
    return jnp.pad(flat, (0, -flat.shape[0] % per_tile)).reshape(-1, PACK_LANES)


def _unpack_small(pack, like):
    flat, out, pos = pack.reshape(-1), {}, 0
    for n in SMALL:
        out[n] = flat[pos:pos + like[n].size].reshape(like[n].shape)
        pos += like[n].size
    return out


def _dw_blocks(w):
    l, k, c = w.shape
    blk = jnp.transpose(w.reshape(l, k, N_DEV, c // N_DEV), (2, 0, 1, 3)).reshape(N_DEV, l * k, c // N_DEV)
    return jnp.pad(blk, ((0, 0), (0, -(l * k) % 8), (0, 0)))


def kernel(x, norm_mix, norm_ffn, final_norm, conv_w_pw1, conv_b_pw1, conv_w_dw, conv_b_dw, conv_ln_g, conv_ln_b, conv_w_pw2, conv_b_pw2, attn_w_qkv, attn_w_o, rel_bias, ffn_w_gate, ffn_w_up, ffn_w_down, loss_target, m_norm_mix, m_norm_ffn, m_final_norm, m_conv_w_pw1, m_conv_b_pw1, m_conv_w_dw, m_conv_b_dw, m_conv_ln_g, m_conv_ln_b, m_conv_w_pw2, m_conv_b_pw2, m_attn_w_qkv, m_attn_w_o, m_rel_bias, m_ffn_w_gate, m_ffn_w_up, m_ffn_w_down, v_norm_mix, v_norm_ffn, v_final_norm, v_conv_w_pw1, v_conv_b_pw1, v_conv_w_dw, v_conv_b_dw, v_conv_ln_g, v_conv_ln_b, v_conv_w_pw2, v_conv_b_pw2, v_attn_w_qkv, v_attn_w_o, v_rel_bias, v_ffn_w_gate, v_ffn_w_up, v_ffn_w_down):
    w = dict(norm_mix=norm_mix, norm_ffn=norm_ffn, final_norm=final_norm, conv_w_pw1=conv_w_pw1,
             conv_b_pw1=conv_b_pw1, conv_w_dw=conv_w_dw, conv_b_dw=conv_b_dw, conv_ln_g=conv_ln_g,
             conv_ln_b=conv_ln_b, conv_w_pw2=conv_w_pw2, conv_b_pw2=conv_b_pw2, attn_w_qkv=attn_w_qkv,
             attn_w_o=attn_w_o, rel_bias=rel_bias, ffn_w_gate=ffn_w_gate, ffn_w_up=ffn_w_up, ffn_w_down=ffn_w_down)
    m = dict(norm_mix=m_norm_mix, norm_ffn=m_norm_ffn, final_norm=m_final_norm, conv_w_pw1=m_conv_w_pw1,
             conv_b_pw1=m_conv_b_pw1, conv_w_dw=m_conv_w_dw, conv_b_dw=m_conv_b_dw, conv_ln_g=m_conv_ln_g,
             conv_ln_b=m_conv_ln_b, conv_w_pw2=m_conv_w_pw2, conv_b_pw2=m_conv_b_pw2, attn_w_qkv=m_attn_w_qkv,
             attn_w_o=m_attn_w_o, rel_bias=m_rel_bias, ffn_w_gate=m_ffn_w_gate, ffn_w_up=m_ffn_w_up,
             ffn_w_down=m_ffn_w_down)
    v = dict(norm_mix=v_norm_mix, norm_ffn=v_norm_ffn, final_norm=v_final_norm, conv_w_pw1=v_conv_w_pw1,
             conv_b_pw1=v_conv_b_pw1, conv_w_dw=v_conv_w_dw, conv_b_dw=v_conv_b_dw, conv_ln_g=v_conv_ln_g,
             conv_ln_b=v_conv_ln_b, conv_w_pw2=v_conv_w_pw2, conv_b_pw2=v_conv_b_pw2, attn_w_qkv=v_attn_w_qkv,
             attn_w_o=v_attn_w_o, rel_bias=v_rel_bias, ffn_w_gate=v_ffn_w_gate, ffn_w_up=v_ffn_w_up,
             ffn_w_down=v_ffn_w_down)

    me = 4 * lax.axis_index("x") + 2 * lax.axis_index("y") + lax.axis_index("c")
    depth = ffn_w_gate.shape[0]
    n_conv, _, cb = conv_w_dw.shape

    def sublayer(key, layer):
        if key in ("wgt", "wut", "wd"):
            return 2 * layer + 1
        return 4 * layer if key in ("w1t", "w2") else 4 * layer + 2

    def landing(block, own):
        land = lax.empty((N_DEV,) + block.shape, block.dtype)
        return lax.dynamic_update_slice(land, own[None], (me,) + (0,) * block.ndim)

    by_sub = {s: [] for s in range(2 * depth)}
    for name, key, cols in SHARDED:
        sw = (jnp.swapaxes(w[name], 1, 2) if cols else w[name]).astype(BF16)
        for layer in range(sw.shape[0]):
            by_sub[sublayer(key, layer)].append((key, layer, sw[layer]))
    likes = {s: jnp.zeros((sum(sh.size for _, _, sh in by_sub[s]) // 1024, 1024), BF16) for s in by_sub}
    dw_shard = jnp.pad(conv_w_dw.reshape(-1, cb), ((0, -(n_conv * CONV_WIDTH) % 8), (0, 0)))
    like_dw = jnp.zeros(dw_shard.shape, F32)
    stages = {}
    for s in range(2 * depth):
        for entry in by_sub[s]:
            stages.setdefault((s, 0 if entry[0] != "w2" else 1), []).append(entry)
    stage_order = sorted(stages)
    groups = [([sh for _, _, sh in stages[st]], [landing(sh, sh) for _, _, sh in stages[st]]) for st in stage_order]
    groups.insert(1, ([dw_shard], [landing(dw_shard, dw_shard)]))
    gather, _ = _exchange_start("gather_start", groups, scatter=False, carry=jnp.zeros((8, 128), F32))
    handle = dict(zip(stage_order, gather[:1] + gather[2:]))
    dw_filters = []

    def fetch(s, after, part=0):
        st = (s, part)
        like = jnp.zeros((sum(sh.size for _, _, sh in stages[st]) // 1024, 1024), BF16)
        pieces = [(handle[st], like)]
        if st == (0, 1):
            pieces.append((gather[1], like_dw))
        landed = _exchange_wait(f"gather_wait{s}_{part}", pieces, after)
        out = {key: g.reshape(g.shape[0] * g.shape[1], g.shape[2]) for (key, _, _), g in zip(stages[st], landed[0])}
        if st == (0, 1):
            dw_all = landed[1][0]
            full = jnp.transpose(dw_all[:, :n_conv * CONV_WIDTH].reshape(N_DEV, n_conv, CONV_WIDTH, cb), (1, 2, 0, 3))
            full = jnp.pad(full.reshape(n_conv, CONV_WIDTH, N_DEV * cb), ((0, 0), (0, CONV_HALO - CONV_WIDTH), (0, 0)))
            dw_filters.extend(full[layer] for layer in range(n_conv))
        if "w2" in out:
            out["wdw"] = dw_filters[s // 4]
        return out

    scatter, dw_grads, started = {}, {}, {}

    def emit(s, gd, carry):
        parts = [gd[key].reshape(N_DEV, -1, gd[key].shape[1]) for key, _, _ in by_sub[s]]
        if "wdw" in gd:
            dw_grads[s // 4] = gd["wdw"]
        if s == 0:
            parts.append(_dw_blocks(jnp.stack([dw_grads[layer] for layer in range(n_conv)])))
        lands = [landing(p[0], lax.dynamic_index_in_dim(p, me, 0, keepdims=False)) for p in parts]
        groups = [(parts[:len(by_sub[s])], lands[:len(by_sub[s])])]
        if s == 0:
            groups.append((parts[-1:], lands[-1:]))
        if s == 0:
            scatter[s], started[0] = _exchange_start(f"scatter_start{s}", groups, scatter=True,
                                                     carry=jnp.zeros((8, 128), F32))
            return carry
        scatter[s], carry = _exchange_start(f"scatter_start{s}", groups, scatter=True, carry=carry)
        return carry

    sm = {n: w[n] for n in SMALL}
    loss_cols, dx, gsm = _local_step(x[0], loss_target[0], sm, depth, fetch, emit)
    loss = lax.psum(jnp.sum(loss_cols), ("x", "y", "c"))

    grads, summed, delta, new_m, new_v = {}, {}, {}, {}, {}

    def reduce_pieces(subs, landed):
        for s, recv in zip(subs, landed):
            for (key, layer, _), r in zip(by_sub[s], recv):
                summed[key, layer] = _sum8(f"sum_{key}{layer}", r)

    def update(names):
        for name in names:
            shape = w[name].shape
            res = _adamw(f"adamw_{name}",
                         *[t.reshape(-1, shape[-1]) for t in (grads[name], w[name], m[name], v[name])])
            delta[name], new_m[name], new_v[name] = (t.reshape(shape) for t in res)

    def stacked(name, key, cols):
        g = jnp.stack([summed[key, layer] for layer in range(w[name].shape[0])])
        return jnp.swapaxes(g, 1, 2) if cols else g

    early = sorted((s for s in scatter if s != 0), reverse=True)
    reduce_pieces(early, _exchange_wait("scatter_wait_early", [(scatter[s][0], likes[s]) for s in early],
                                        started[0]))
    late_names = [name for name, key, _ in SHARDED if any(k == key for k, _, _ in by_sub[0])]
    for name, key, cols in SHARDED:
        if name not in late_names:
            grads[name] = stacked(name, key, cols)
    early_names = [name for name, _, _ in SHARDED if name not in late_names]
    update(early_names)
    pack = _pack_small(gsm)
    ((pack_all,),) = _exchange("gather_small_grads", [([pack], pack)], scatter=False)
    pack_sum = _sum8("sum_small", pack_all)
    grads.update(_unpack_small(pack_sum, sm))

    landed = _exchange_wait("scatter_wait_last", [(scatter[0][0], likes[0]), (scatter[0][1], like_dw)],
                            new_v[early_names[-1]])
    reduce_pieces([0], landed[:1])
    for name, key, cols in SHARDED:
        if name in late_names:
            grads[name] = stacked(name, key, cols)
    grads["conv_w_dw"] = _sum8("sum_wdw", landed[1][0])[:n_conv * CONV_WIDTH].reshape(conv_w_dw.shape)
    update(late_names + ["conv_w_dw"])
    res = _adamw("adamw_small", pack_sum, _pack_small(sm), _pack_small({n: m[n] for n in SMALL}),
                 _pack_small({n: v[n] for n in SMALL}))
    for dst, t in zip((delta, new_m, new_v), res):
        dst.update(_unpack_small(t, sm))

    outs = [loss, dx[None]]
    for d in (grads, delta, new_m, new_v):
        outs += [d[n] for n in ORDER]
    return tuple(outs)
```

```python
import functools
import math

import numpy as np
import jax
import jax.numpy as jnp
from jax import lax
from jax.experimental import pallas as pl
from jax.experimental.pallas import tpu as pltpu

F32 = jnp.float32
BF16 = jnp.bfloat16

N_DEV = 8
HEAD_DIM = 64
HEADS_PER_GROUP = 4
GROUP_COLS = HEADS_PER_GROUP * HEAD_DIM
DILATIONS = (1, 4, 16)
N_BACK = 128
N_GROUPS = 3
N_HEADS = 12
D_ATTN = 768
N_BUCKETS = 32
REL_MAX_DISTANCE = 2048
CONV_WIDTH = 31
CONV_HALO = 32
EPS = 1e-6
NEG_INF = -1e30
ADAM_LR, ADAM_B1, ADAM_B2, ADAM_EPS, ADAM_WD, ADAM_STEP = 0.001, 0.9, 0.999, 1e-08, 0.01, 10
V7X_VMEM_LIMIT_BYTES = 56 * 1024 * 1024
MESH = pl.DeviceIdType.MESH
ANY = pl.BlockSpec(memory_space=pl.ANY)


def _pick(n, prefs):
    for p in prefs:
        if n % p == 0:
            return p
    return n


def _params(sem, vmem=None):
    return pltpu.CompilerParams(dimension_semantics=sem, vmem_limit_bytes=vmem)


def _sigmoid(x):
    return 1.0 / (1.0 + jnp.exp(-x))


def _exchange(name, groups, scatter):
    n_arr = [len(arrs) for arrs, _ in groups]
    n_in = sum(n_arr) + len(groups)
    ng = len(groups)

    def body(*refs):
        ins, outs, (send_sems, recv_sems, local_sems) = refs[:n_in], refs[n_in:-3], refs[-3:]
        x, y, c = lax.axis_index("x"), lax.axis_index("y"), lax.axis_index("c")
        me = 4 * x + 2 * y + c
        pos_in = pos_out = 0
        plans = []
        for gi in range(ng):
            srcs = ins[pos_in:pos_in + n_arr[gi]]
            like = ins[pos_in + n_arr[gi]]
            dsts = outs[pos_out:pos_out + n_arr[gi]]
            pos_in += n_arr[gi] + 1
            pos_out += n_arr[gi]
            plans.append((gi, srcs, like, dsts))
        local = []
        for gi, srcs, like, dsts in plans:
            for s, d in zip(srcs, dsts):
                cp = pltpu.make_async_copy(s.at[me] if scatter else s, d.at[me], local_sems.at[gi])
                cp.start()
                local.append(cp)
        for delta in range(1, N_DEV):
            dx, dy, dc = (delta >> 2) & 1, (delta >> 1) & 1, delta & 1
            px, py, pc = (1 - x if dx else x), (1 - y if dy else y), (1 - c if dc else c)
            peer = 4 * px + 2 * py + pc
            for gi, srcs, like, dsts in plans:
                for s, d in zip(srcs, dsts):
                    pltpu.make_async_remote_copy(
                        src_ref=s.at[peer] if scatter else s, dst_ref=d.at[me],
                        send_sem=send_sems.at[gi, delta - 1], recv_sem=recv_sems.at[gi, delta - 1],
                        device_id=(px, py, pc), device_id_type=MESH).start()
        for delta in range(1, N_DEV):
            for gi, srcs, like, dsts in plans:
                pltpu.make_async_remote_copy(
                    src_ref=like, dst_ref=like, send_sem=send_sems.at[gi, delta - 1],
                    recv_sem=recv_sems.at[gi, delta - 1], device_id=(x, y, c), device_id_type=MESH).wait()
        for cp in local:
            cp.wait()

    operands, out_shape = [], []
    for arrs, like in groups:
        operands += list(arrs) + [like]
        for a in arrs:
            blk = a.shape[1:] if scatter else a.shape
            out_shape.append(jax.ShapeDtypeStruct((N_DEV,) + tuple(blk), a.dtype))
    outs = pl.pallas_call(
        body, name=name, out_shape=out_shape, in_specs=[ANY] * len(operands), out_specs=[ANY] * len(out_shape),
        scratch_shapes=[pltpu.SemaphoreType.DMA((ng, N_DEV - 1)), pltpu.SemaphoreType.DMA((ng, N_DEV - 1)),
                        pltpu.SemaphoreType.DMA((ng,))],
        compiler_params=pltpu.CompilerParams(has_side_effects=True),
    )(*operands)
    res, pos = [], 0
    for n in n_arr:
        res.append(list(outs[pos:pos + n]))
        pos += n
    return res


HBM = pl.BlockSpec(memory_space=pltpu.HBM)
SEM = pl.BlockSpec(memory_space=pltpu.SEMAPHORE)
EFFECT = pltpu.SideEffectType.DATAFLOW_SIDE_EFFECTING


def _in_hbm(a):
    return pltpu.with_memory_space_constraint(a, pltpu.HBM)


def _exchange_start(name, groups, scatter, carry):
    ns = [len(s) for s, _ in groups]
    n_in = 2 * sum(ns)

    def body(*refs):
        ins, outs = refs[:n_in], refs[n_in + 1:]
        x, y, c = lax.axis_index("x"), lax.axis_index("y"), lax.axis_index("c")
        me = 4 * x + 2 * y + c
        pi = po = 0
        for n in ns:
            srcs, lands = ins[pi:pi + n], ins[pi + n:pi + 2 * n]
            send_sems, recv_sems = outs[po], outs[po + 1]
            pi += 2 * n
            po += 2 + 2 * n
            for delta in range(1, N_DEV):
                dx, dy, dc = (delta >> 2) & 1, (delta >> 1) & 1, delta & 1
                px, py, pc = (1 - x if dx else x), (1 - y if dy else y), (1 - c if dc else c)
                peer = 4 * px + 2 * py + pc
                for s, d in zip(srcs, lands):
                    pltpu.make_async_remote_copy(
                        src_ref=s.at[peer] if scatter else s, dst_ref=d.at[me], send_sem=send_sems.at[delta - 1],
                        recv_sem=recv_sems.at[delta - 1], device_id=(px, py, pc), device_id_type=MESH).start()

    operands, out_shape, out_specs, aliases = [], [], [], {}
    for srcs, lands in groups:
        out_shape += [pltpu.SemaphoreType.DMA((N_DEV - 1,))] * 2
        out_specs += [SEM, SEM]
        for a in list(srcs) + list(lands):
            aliases[len(operands)] = len(out_shape)
            operands.append(_in_hbm(a))
            out_shape.append(pltpu.HBM(a.shape, a.dtype))
            out_specs.append(HBM)
    aliases[len(operands)] = len(out_shape)
    operands.append(_in_hbm(carry))
    out_shape.append(pltpu.HBM(carry.shape, carry.dtype))
    out_specs.append(HBM)
    outs = pl.pallas_call(
        body, name=name, out_shape=out_shape, in_specs=[HBM] * len(operands), out_specs=out_specs,
        input_output_aliases=aliases, compiler_params=pltpu.CompilerParams(has_side_effects=EFFECT),
    )(*operands)
    handles, po = [], 0
    for n in ns:
        handles.append((outs[po], outs[po + 1], list(outs[po + 2:po + 2 + n]), list(outs[po + 2 + n:po + 2 + 2 * n])))
        po += 2 + 2 * n
    return handles, outs[-1]


def _exchange_wait(name, pieces, after):
    ns = [len(h[2]) for h, _ in pieces]

    def body(*refs):
        x, y, c = lax.axis_index("x"), lax.axis_index("y"), lax.axis_index("c")
        pi = 0
        for n in ns:
            send_sems, recv_sems, like = refs[pi + 2 * n], refs[pi + 2 * n + 1], refs[pi + 2 * n + 2]
            pi += 2 * n + 3
            for delta in range(1, N_DEV):
                cp = pltpu.make_async_remote_copy(
                    src_ref=like, dst_ref=like, send_sem=send_sems.at[delta - 1], recv_sem=recv_sems.at[delta - 1],
                    device_id=(x, y, c), device_id_type=MESH)
                cp.wait_send()
                cp.wait_recv()

    operands, in_specs, out_shape, aliases = [], [], [], {}
    for (send_sems, recv_sems, srcs, lands), like in pieces:
        for a in srcs + lands:
            aliases[len(operands)] = len(out_shape)
            operands.append(a)
            in_specs.append(HBM)
            out_shape.append(pltpu.HBM(a.shape, a.dtype))
        operands += [send_sems, recv_sems, like]
        in_specs += [SEM, SEM, ANY]
    operands.append(after)
    in_specs.append(ANY)
    outs = pl.pallas_call(
        body, name=name, out_shape=out_shape, in_specs=in_specs, out_specs=[HBM] * len(out_shape),
        input_output_aliases=aliases, compiler_params=pltpu.CompilerParams(has_side_effects=EFFECT),
    )(*operands)
    res, po = [], 0
    for n in ns:
        res.append(list(outs[po + n:po + 2 * n]))
        po += 2 * n
    return res


def _sum8(name, r):
    _, rows, cols = r.shape
    tr = _pick(rows, (256, 128, 64, 32, 16, 8))

    def body(r_ref, o_ref):
        acc = r_ref[0].astype(F32)
        for p in range(1, N_DEV):
            acc = acc + r_ref[p].astype(F32)
        o_ref[...] = acc

    return pl.pallas_call(
        body, name=name, out_shape=jax.ShapeDtypeStruct((rows, cols), F32), grid=(rows // tr,),
        in_specs=[pl.BlockSpec((N_DEV, tr, cols), lambda i: (0, i, 0))],
        out_specs=pl.BlockSpec((tr, cols), lambda i: (i, 0)), compiler_params=_params(("parallel",)),
    )(r)


def _adamw(name, g, w, m, v):
    rows, cols = g.shape
    tr = _pick(rows, (512, 256, 128, 64, 32, 16, 8))
    c1 = 1.0 - ADAM_B1 ** ADAM_STEP
    c2 = 1.0 - ADAM_B2 ** ADAM_STEP

    def body(g_ref, w_ref, m_ref, v_ref, d_ref, nm_ref, nv_ref):
        gv = g_ref[...]
        nm = ADAM_B1 * m_ref[...] + (1.0 - ADAM_B1) * gv
        nv = ADAM_B2 * v_ref[...] + (1.0 - ADAM_B2) * (gv * gv)
        d_ref[...] = -ADAM_LR * ((nm / c1) / (jnp.sqrt(nv / c2) + ADAM_EPS) + ADAM_WD * w_ref[...])
        nm_ref[...] = nm
        nv_ref[...] = nv

    spec = pl.BlockSpec((tr, cols), lambda i: (i, 0))
    return pl.pallas_call(
        body, name=name, out_shape=[jax.ShapeDtypeStruct((rows, cols), F32)] * 3, grid=(rows // tr,),
        in_specs=[spec] * 4, out_specs=[spec] * 3, compiler_params=_params(("parallel",)),
    )(g, w, m, v)


def _mm_nt(name, a, ws, w_offs, n, epi, out_dtypes, extras=(), rows=(), tm=None, tn=None):
    m, k = a.shape
    tm = tm or _pick(m, (512, 256, 128))
    tn = tn or _pick(n, (1408, 1152, 1024, 512, 256, 128))
    nw, ne, nr = len(ws), len(extras), len(rows)

    def body(*refs):
        a_ref, w_refs = refs[0], refs[1:1 + nw]
        e_refs, r_refs = refs[1 + nw:1 + nw + ne], refs[1 + nw + ne:1 + nw + ne + nr]
        o_refs = refs[1 + nw + ne + nr:]
        av = a_ref[...].astype(BF16)
        accs = [lax.dot_general(av, w[...], (((1,), (1,)), ((), ())), preferred_element_type=F32) for w in w_refs]
        outs = epi(accs, [e[...] for e in e_refs], [r[...] for r in r_refs])
        for o_ref, o in zip(o_refs, outs):
            o_ref[...] = o.astype(o_ref.dtype)

    in_specs = [pl.BlockSpec((tm, k), lambda j, i: (i, 0))]
    in_specs += [pl.BlockSpec((tn, k), functools.partial(lambda j, i, off: (j + off, 0), off=off)) for off in w_offs]
    in_specs += [pl.BlockSpec((tm, tn), lambda j, i: (i, j))] * ne
    in_specs += [pl.BlockSpec((1, tn), lambda j, i: (0, j))] * nr
    return pl.pallas_call(
        body, name=name, out_shape=[jax.ShapeDtypeStruct((m, n), dt) for dt in out_dtypes],
        grid=(n // tn, m // tm), in_specs=in_specs,
        out_specs=[pl.BlockSpec((tm, tn), lambda j, i: (i, j))] * len(out_dtypes),
        compiler_params=_params(("parallel", "parallel"), V7X_VMEM_LIMIT_BYTES),
    )(a, *ws, *extras, *rows)


def _rms_rows(x, gain):
    r = lax.rsqrt(jnp.mean(x * x, axis=-1, keepdims=True) + EPS)
    return (x * r * gain).astype(BF16)


def _project_residual(name, a, b, x, bias=None, gain=None):
    m, k = a.shape
    n = b.shape[1]
    tm = _pick(m, (512, 256, 128))
    rows = [r for r in (bias, gain) if r is not None]

    def body(*refs):
        a_ref, b_ref, x_ref = refs[:3]
        r_refs = list(refs[3:3 + len(rows)])
        out = x_ref[...] + jnp.dot(a_ref[...], b_ref[...], preferred_element_type=F32)
        if bias is not None:
            out = out + r_refs.pop(0)[...]
        refs[3 + len(rows)][...] = out
        if gain is not None:
            refs[4 + len(rows)][...] = _rms_rows(out, r_refs.pop(0)[...])

    big = pl.BlockSpec((tm, n), lambda i: (i, 0))
    out_shape = [jax.ShapeDtypeStruct((m, n), F32)] + ([jax.ShapeDtypeStruct((m, n), BF16)] if gain is not None else [])
    res = pl.pallas_call(
        body, name=name, out_shape=out_shape, grid=(m // tm,),
        in_specs=[pl.BlockSpec((tm, k), lambda i: (i, 0)), pl.BlockSpec((k, n), lambda i: (0, 0)), big]
        + [pl.BlockSpec((1, n), lambda i: (0, 0))] * len(rows),
        out_specs=[big] * len(out_shape), compiler_params=_params(("parallel",), V7X_VMEM_LIMIT_BYTES),
    )(a, b, x, *rows)
    return res if gain is not None else (res[0], None)


def _mm_nn_rms_bwd(name, as_, bs, x, g, dx_out, tm=None):
    m, k = as_[0].shape
    n = bs[0].shape[1]
    tm = tm or _pick(m, (512, 256, 128))
    npair = len(as_)

    def body(*refs):
        a_refs, b_refs = refs[:npair], refs[npair:2 * npair]
        x_ref, g_ref, dxo_ref, dx_ref, dg_ref = refs[2 * npair:]
        dh = None
        for a_ref, b_ref in zip(a_refs, b_refs):
            p = jnp.dot(a_ref[...].astype(BF16), b_ref[...], preferred_element_type=F32)
            dh = p if dh is None else dh + p
        xv = x_ref[...]
        r = lax.rsqrt(jnp.mean(xv * xv, axis=-1, keepdims=True) + EPS)
        yv = xv * r
        dy = dh * g_ref[...]
        dx_ref[...] = dxo_ref[...] + r * (dy - yv * jnp.mean(dy * yv, axis=-1, keepdims=True))

        @pl.when(pl.program_id(0) == 0)
        def _():
            dg_ref[...] = jnp.zeros_like(dg_ref)

        dg_ref[...] += jnp.sum(dh * yv, axis=0, keepdims=True)

    big = pl.BlockSpec((tm, n), lambda i: (i, 0))
    row = pl.BlockSpec((1, n), lambda i: (0, 0))
    in_specs = [pl.BlockSpec((tm, k), lambda i: (i, 0))] * npair + [pl.BlockSpec((k, n), lambda i: (0, 0))] * npair
    return pl.pallas_call(
        body, name=name, out_shape=[jax.ShapeDtypeStruct((m, n), F32), jax.ShapeDtypeStruct((1, n), F32)],
        grid=(m // tm,), in_specs=in_specs + [big, row, big], out_specs=[big, row],
        compiler_params=_params(("arbitrary",), V7X_VMEM_LIMIT_BYTES),
    )(*as_, *bs, x, g, dx_out)


def _mm_tn(name, a, b, colsum_b=False, tm=None, tk=2048):
    t, ma = a.shape
    nb = b.shape[1]
    tm = tm or _pick(ma, (1408, 1152, 1024, 768, 512, 256, 128))
    tk = _pick(t, (tk, 256, 128))
    nk = t // tk

    def body(*refs):
        a_ref, b_ref, o_ref = refs[0], refs[1], refs[2]
        acc_ref = refs[-1]
        kk = pl.program_id(1)
        bv = b_ref[...]

        @pl.when(kk == 0)
        def _():
            acc_ref[...] = jnp.zeros_like(acc_ref)

        acc_ref[...] += lax.dot_general(a_ref[...].astype(BF16), bv.astype(BF16), (((0,), (0,)), ((), ())),
                                        preferred_element_type=F32)
        if colsum_b:
            s_ref = refs[3]

            @pl.when((kk == 0) & (pl.program_id(0) == 0))
            def _():
                s_ref[...] = jnp.zeros_like(s_ref)

            @pl.when(pl.program_id(0) == 0)
            def _():
                s_ref[...] += jnp.sum(bv.astype(F32), axis=0, keepdims=True)

        @pl.when(kk == nk - 1)
        def _():
            o_ref[...] = acc_ref[...].astype(o_ref.dtype)

    out_shape = [jax.ShapeDtypeStruct((ma, nb), BF16)]
    out_specs = [pl.BlockSpec((tm, nb), lambda i, kk: (i, 0))]
    if colsum_b:
        out_shape.append(jax.ShapeDtypeStruct((1, nb), F32))
        out_specs.append(pl.BlockSpec((1, nb), lambda i, kk: (0, 0)))
    res = pl.pallas_call(
        body, name=name, out_shape=out_shape, grid=(ma // tm, nk),
        in_specs=[pl.BlockSpec((tk, tm), lambda i, kk: (kk, i)), pl.BlockSpec((tk, nb), lambda i, kk: (kk, 0))],
        out_specs=out_specs, scratch_shapes=[pltpu.VMEM((tm, nb), F32)],
        compiler_params=_params(("arbitrary", "arbitrary"), V7X_VMEM_LIMIT_BYTES),
    )(a, b)
    return res if colsum_b else res[0]


def _mm_tn_pair(name, a1, a2, b, tk=1024):
    t, ma = a1.shape
    nb = b.shape[1]
    tm = _pick(ma, (1408, 1152, 1024, 768, 512, 256, 128))
    tk = _pick(t, (tk, 256, 128))
    nk = t // tk
    dims = (((0,), (0,)), ((), ()))

    def body(a1_ref, a2_ref, b_ref, o1_ref, o2_ref, acc1_ref, acc2_ref):
        kk = pl.program_id(1)
        bv = b_ref[...]

        @pl.when(kk == 0)
        def _():
            acc1_ref[...] = jnp.zeros_like(acc1_ref)
            acc2_ref[...] = jnp.zeros_like(acc2_ref)

        acc1_ref[...] += lax.dot_general(a1_ref[...], bv, dims, preferred_element_type=F32)
        acc2_ref[...] += lax.dot_general(a2_ref[...], bv, dims, preferred_element_type=F32)

        @pl.when(kk == nk - 1)
        def _():
            o1_ref[...] = acc1_ref[...].astype(BF16)
            o2_ref[...] = acc2_ref[...].astype(BF16)

    sa = pl.BlockSpec((tk, tm), lambda i, kk: (kk, i))
    so = pl.BlockSpec((tm, nb), lambda i, kk: (i, 0))
    return pl.pallas_call(
        body, name=name, out_shape=[jax.ShapeDtypeStruct((ma, nb), BF16)] * 2, grid=(ma // tm, nk),
        in_specs=[sa, sa, pl.BlockSpec((tk, nb), lambda i, kk: (kk, 0))], out_specs=[so, so],
        scratch_shapes=[pltpu.VMEM((tm, nb), F32)] * 2,
        compiler_params=_params(("arbitrary", "arbitrary"), V7X_VMEM_LIMIT_BYTES),
    )(a1, a2, b)


FFN_CHUNK = 256


def _ffn_down_bwd(name, dx, wd, gq, uq):
    t, d_model = dx.shape
    f = wd.shape[0]
    tm = _pick(t, (256, 128))
    ck = _pick(f, (FFN_CHUNK, 128))
    nt = (((1,), (1,)), ((), ()))
    tn = (((0,), (0,)), ((), ()))
    last = t // tm - 1

    def body(dx_ref, w_ref, g_ref, u_ref, dg_ref, du_ref, dw_ref, acc_ref):
        i = pl.program_id(0)

        @pl.when(i == 0)
        def _():
            acc_ref[...] = jnp.zeros_like(acc_ref)

        dxb = dx_ref[...].astype(BF16)
        for c0 in range(0, f, ck):
            cs = slice(c0, c0 + ck)
            dact = lax.dot_general(dxb, w_ref[cs, :], nt, preferred_element_type=F32)
            gf, uf = g_ref[:, cs].astype(F32), u_ref[:, cs].astype(F32)
            sg = _sigmoid(gf)
            dg_ref[:, cs] = (dact * uf * (sg * (1.0 + gf * (1.0 - sg)))).astype(BF16)
            du_ref[:, cs] = (dact * gf * sg).astype(BF16)
            act = (gf * sg * uf).astype(BF16)
            acc_ref[cs, :] += lax.dot_general(act, dxb, tn, preferred_element_type=F32)

        @pl.when(i == last)
        def _():
            dw_ref[...] = acc_ref[...].astype(BF16)

    wide = pl.BlockSpec((tm, f), lambda i: (i, 0))
    whole = pl.BlockSpec((f, d_model), lambda i: (0, 0))
    return pl.pallas_call(
        body, name=name,
        out_shape=[jax.ShapeDtypeStruct((t, f), BF16), jax.ShapeDtypeStruct((t, f), BF16),
                   jax.ShapeDtypeStruct((f, d_model), BF16)],
        grid=(t // tm,), in_specs=[pl.BlockSpec((tm, d_model), lambda i: (i, 0)), whole, wide, wide],
        out_specs=[wide, wide, whole], scratch_shapes=[pltpu.VMEM((f, d_model), F32)],
        compiler_params=_params(("arbitrary",), V7X_VMEM_LIMIT_BYTES),
    )(dx, wd, gq, uq)


def _rmsnorm_fwd(name, x, g):
    t, d = x.shape
    tr = _pick(t, (512, 256, 128))

    def body(x_ref, g_ref, h_ref):
        xv = x_ref[...]
        r = lax.rsqrt(jnp.mean(xv * xv, axis=-1, keepdims=True) + EPS)
        h_ref[...] = (xv * r * g_ref[...]).astype(BF16)

    return pl.pallas_call(
        body, name=name, out_shape=jax.ShapeDtypeStruct((t, d), BF16), grid=(t // tr,),
        in_specs=[pl.BlockSpec((tr, d), lambda i: (i, 0)), pl.BlockSpec((1, d), lambda i: (0, 0))],
        out_specs=pl.BlockSpec((tr, d), lambda i: (i, 0)), compiler_params=_params(("parallel",)),
    )(x, g)


def _loss_head(name, x, g, target):
    t, d = x.shape
    tr = _pick(t, (512, 256, 128))

    def body(x_ref, g_ref, t_ref, dx_ref, dg_ref, l_ref):
        xv = x_ref[...]
        r = lax.rsqrt(jnp.mean(xv * xv, axis=-1, keepdims=True) + EPS)
        yv = xv * r
        diff = yv * g_ref[...] - t_ref[...]
        dout = diff * (1.0 / d)
        dy = dout * g_ref[...]
        dx_ref[...] = r * (dy - yv * jnp.mean(dy * yv, axis=-1, keepdims=True))

        @pl.when(pl.program_id(0) == 0)
        def _():
            dg_ref[...] = jnp.zeros_like(dg_ref)
            l_ref[...] = jnp.zeros_like(l_ref)

        dg_ref[...] += jnp.sum(dout * yv, axis=0, keepdims=True)
        l_ref[...] += (0.5 / d) * jnp.sum(diff * diff, axis=0, keepdims=True)

    big = pl.BlockSpec((tr, d), lambda i: (i, 0))
    row = pl.BlockSpec((1, d), lambda i: (0, 0))
    return pl.pallas_call(
        body, name=name,
        out_shape=[jax.ShapeDtypeStruct((t, d), F32), jax.ShapeDtypeStruct((1, d), F32),
                   jax.ShapeDtypeStruct((1, d), F32)],
        grid=(t // tr,), in_specs=[big, row, big], out_specs=[big, row, row],
        compiler_params=_params(("arbitrary",)),
    )(x, g, target)


CONV_ROWS = 64
SUBLANES = 8


def _fill_window(win_ref, sh_ref, parts):
    rows = sh_ref.shape[2]
    for cb in range(win_ref.shape[0]):
        for r0, val in parts:
            win_ref[cb, r0:r0 + val.shape[0], :] = val[:, 128 * cb:128 * (cb + 1)]
        win_ref[cb, rows:rows + SUBLANES, :] = jnp.zeros((SUBLANES, 128), F32)
        for b in range(1, SUBLANES):
            sh_ref[b - 1, cb] = win_ref[cb, b:b + rows, :]


def _window_rows(win_ref, sh_ref, o, cb):
    b = o % SUBLANES
    if b == 0:
        return win_ref[cb, o:o + CONV_ROWS, :]
    return sh_ref[b - 1, cb, o - b:o - b + CONV_ROWS, :]


def _window_scratch(rows, c):
    return [pltpu.VMEM((c // 128, rows + SUBLANES, 128), F32), pltpu.VMEM((SUBLANES - 1, c // 128, rows, 128), F32)]


def _dwconv_fwd(name, glu, w_dw, b_dw, ln_g, ln_b):
    t, c = glu.shape
    tt = _pick(t, (512, 256, 128))
    hb = tt // CONV_HALO

    def body(cur_ref, halo_ref, w_ref, b_ref, g_ref, be_ref, dw_ref, s_ref, win_ref, sh_ref):
        i = pl.program_id(0)
        halo = jnp.where(i > 0, halo_ref[...].astype(F32), 0.0)
        _fill_window(win_ref, sh_ref, [(0, halo), (CONV_HALO, cur_ref[...].astype(F32))])
        for r0 in range(0, tt, CONV_ROWS):
            for cb in range(c // 128):
                c0 = 128 * cb
                acc = jnp.zeros((CONV_ROWS, 128), F32) + b_ref[:, c0:c0 + 128]
                for k in range(CONV_WIDTH):
                    o = r0 + k + CONV_HALO - (CONV_WIDTH - 1)
                    acc = acc + w_ref[k:k + 1, c0:c0 + 128] * _window_rows(win_ref, sh_ref, o, cb)
                dw_ref[r0:r0 + CONV_ROWS, c0:c0 + 128] = acc
        u = dw_ref[...]
        mu = jnp.mean(u, axis=-1, keepdims=True)
        uc = u - mu
        rstd = lax.rsqrt(jnp.mean(uc * uc, axis=-1, keepdims=True) + EPS)
        z = uc * rstd * g_ref[...] + be_ref[...]
        s_ref[...] = (z * _sigmoid(z)).astype(BF16)

    big = pl.BlockSpec((tt, c), lambda i: (i, 0))
    row = pl.BlockSpec((1, c), lambda i: (0, 0))
    return pl.pallas_call(
        body, name=name, out_shape=[jax.ShapeDtypeStruct((t, c), F32), jax.ShapeDtypeStruct((t, c), BF16)],
        grid=(t // tt,),
        in_specs=[big, pl.BlockSpec((CONV_HALO, c), lambda i: (jnp.maximum(i * hb - 1, 0), 0)),
                  pl.BlockSpec((CONV_HALO, c), lambda i: (0, 0)), row, row, row],
        out_specs=[big, big], scratch_shapes=_window_scratch(tt + CONV_HALO, c),
        compiler_params=_params(("parallel",), V7X_VMEM_LIMIT_BYTES),
    )(glu, glu, w_dw, b_dw, ln_g, ln_b)


def _ln_silu_bwd(name, dx, w2, dw, ln_g, ln_b):
    t, c = dw.shape
    d_model = dx.shape[1]
    tr = _pick(t, (256, 128))

    def body(dx_ref, w_ref, dw_ref, g_ref, be_ref, o_ref, acc_ref):
        ds = lax.dot_general(dx_ref[...].astype(BF16), w_ref[...], (((1,), (1,)), ((), ())),
                             preferred_element_type=F32)
        u = dw_ref[...]
        mu = jnp.mean(u, axis=-1, keepdims=True)
        uc = u - mu
        rstd = lax.rsqrt(jnp.mean(uc * uc, axis=-1, keepdims=True) + EPS)
        xh = uc * rstd
        z = xh * g_ref[...] + be_ref[...]
        sg = _sigmoid(z)
        dz = ds * (sg * (1.0 + z * (1.0 - sg)))
        dxh = dz * g_ref[...]
        du = rstd * (dxh - jnp.mean(dxh, axis=-1, keepdims=True) - xh * jnp.mean(dxh * xh, axis=-1, keepdims=True))
        o_ref[...] = du

        @pl.when(pl.program_id(0) == 0)
        def _():
            acc_ref[...] = jnp.zeros_like(acc_ref)

        acc_ref[0:1, :] += jnp.sum(dz * xh, axis=0, keepdims=True)
        acc_ref[1:2, :] += jnp.sum(dz, axis=0, keepdims=True)
        acc_ref[2:3, :] += jnp.sum(du, axis=0, keepdims=True)

    big = pl.BlockSpec((tr, c), lambda i: (i, 0))
    row = pl.BlockSpec((1, c), lambda i: (0, 0))
    return pl.pallas_call(
        body, name=name, out_shape=[jax.ShapeDtypeStruct((t, c), F32), jax.ShapeDtypeStruct((8, c), F32)],
        grid=(t // tr,),
        in_specs=[pl.BlockSpec((tr, d_model), lambda i: (i, 0)), pl.BlockSpec((c, d_model), lambda i: (0, 0)),
                  big, row, row],
        out_specs=[big, pl.BlockSpec((8, c), lambda i: (0, 0))],
        compiler_params=_params(("arbitrary",), V7X_VMEM_LIMIT_BYTES),
    )(dx, w2, dw, ln_g, ln_b)


def _dwconv_bwd(name, ddw, a, gt, w_dw):
    t, c = ddw.shape
    tt = _pick(t, (256, 128))
    hb = tt // CONV_HALO
    last = t // tt - 1
    back = CONV_WIDTH - 1

    def body(d_ref, dn_ref, a_ref, ap_ref, g_ref, gp_ref, w_ref, du_ref, dwk_ref, db_ref,
             wd_ref, shd_ref, wg_ref, shg_ref, dg_ref, dwk8_ref):
        i = pl.program_id(0)
        _fill_window(wd_ref, shd_ref, [(0, d_ref[...]), (tt, jnp.where(i < last, dn_ref[...], 0.0))])
        glu_prev = ap_ref[...].astype(F32) * _sigmoid(gp_ref[...].astype(F32))
        av = a_ref[...].astype(F32)
        sg = _sigmoid(g_ref[...].astype(F32))
        _fill_window(wg_ref, shg_ref, [(0, jnp.where(i > 0, glu_prev, 0.0)), (CONV_HALO, av * sg)])

        @pl.when(i == 0)
        def _():
            dwk8_ref[...] = jnp.zeros_like(dwk8_ref)
            db_ref[...] = jnp.zeros_like(db_ref)

        for r0 in range(0, tt, CONV_ROWS):
            for cb in range(c // 128):
                c0 = 128 * cb
                dcur = wd_ref[cb, r0:r0 + CONV_ROWS, :]
                acc = jnp.zeros((CONV_ROWS, 128), F32)
                for k in range(CONV_WIDTH):
                    acc = acc + w_ref[k:k + 1, c0:c0 + 128] * _window_rows(wd_ref, shd_ref, r0 + back - k, cb)
                    p = dcur * _window_rows(wg_ref, shg_ref, r0 + k + CONV_HALO - back, cb)
                    s8 = p[0:SUBLANES]
                    for q in range(SUBLANES, CONV_ROWS, SUBLANES):
                        s8 = s8 + p[q:q + SUBLANES]
                    dwk8_ref[SUBLANES * k:SUBLANES * (k + 1), c0:c0 + 128] += s8
                dg_ref[r0:r0 + CONV_ROWS, c0:c0 + 128] = acc

        @pl.when(i == last)
        def _():
            for k in range(CONV_WIDTH):
                dwk_ref[k:k + 1, :] = jnp.sum(dwk8_ref[SUBLANES * k:SUBLANES * (k + 1), :], axis=0, keepdims=True)
            dwk_ref[CONV_WIDTH:, :] = jnp.zeros((CONV_HALO - CONV_WIDTH, c), F32)
        dglu = dg_ref[...]
        da = dglu * sg
        dgate = dglu * av * sg * (1.0 - sg)
        du_ref[:, 0:c] = da.astype(BF16)
        du_ref[:, c:] = dgate.astype(BF16)
        db_ref[:, 0:c] += jnp.sum(da, axis=0, keepdims=True)
        db_ref[:, c:] += jnp.sum(dgate, axis=0, keepdims=True)

    big = pl.BlockSpec((tt, c), lambda i: (i, 0))
    prev = pl.BlockSpec((CONV_HALO, c), lambda i: (jnp.maximum(i * hb - 1, 0), 0))
    nxt = pl.BlockSpec((CONV_HALO, c), lambda i: (jnp.minimum((i + 1) * hb, t // CONV_HALO - 1), 0))
    return pl.pallas_call(
        body, name=name,
        out_shape=[jax.ShapeDtypeStruct((t, 2 * c), BF16), jax.ShapeDtypeStruct((CONV_HALO, c), F32),
                   jax.ShapeDtypeStruct((1, 2 * c), F32)],
        grid=(t // tt,),
        in_specs=[big, nxt, big, prev, big, prev, pl.BlockSpec((CONV_HALO, c), lambda i: (0, 0))],
        out_specs=[pl.BlockSpec((tt, 2 * c), lambda i: (i, 0)), pl.BlockSpec((CONV_HALO, c), lambda i: (0, 0)),
                   pl.BlockSpec((1, 2 * c), lambda i: (0, 0))],
        scratch_shapes=_window_scratch(tt + CONV_HALO, c) + _window_scratch(tt + CONV_HALO, c)
        + [pltpu.VMEM((tt, c), F32), pltpu.VMEM((SUBLANES * CONV_HALO, c), F32)],
        compiler_params=_params(("arbitrary",), V7X_VMEM_LIMIT_BYTES),
    )(ddw, ddw, a, a, gt, gt, w_dw)


def _bucket_tables():
    i = np.arange(N_BACK)[:, None]
    j = np.arange(2 * N_BACK)[None, :]
    dist = i + N_BACK - j
    valid = (dist >= 0) & (dist <= N_BACK)
    max_exact = N_BUCKETS // 2
    out = []
    for d in DILATIONS:
        n = np.maximum(dist * d, 0)
        nf = np.maximum(n, 1).astype(np.float32)
        large = max_exact + (np.log(nf / np.float32(max_exact)) / np.float32(math.log(REL_MAX_DISTANCE / max_exact))
                             * np.float32(N_BUCKETS - max_exact)).astype(np.int32)
        large = np.minimum(large, N_BUCKETS - 1)
        out.append(np.where(valid, np.where(n < max_exact, n, large), -1))
    return np.stack(out).astype(np.int32)


def _bias_build(name, rel_bias, buckets):
    def body(tbl_ref, bk_ref, o_ref):
        g = pl.program_id(0)
        bk = bk_ref[0]
        for h in range(HEADS_PER_GROUP):
            acc = jnp.zeros(bk.shape, F32)
            for b in range(N_BUCKETS):
                acc = jnp.where(bk == b, tbl_ref[b, g * HEADS_PER_GROUP + h], acc)
            o_ref[h] = jnp.where(bk < 0, NEG_INF, acc)

    return pl.pallas_call(
        body, name=name, out_shape=jax.ShapeDtypeStruct((N_HEADS, N_BACK, 2 * N_BACK), F32), grid=(N_GROUPS,),
        in_specs=[pl.BlockSpec(memory_space=pltpu.SMEM), pl.BlockSpec((1, N_BACK, 2 * N_BACK), lambda g: (g, 0, 0))],
        out_specs=pl.BlockSpec((HEADS_PER_GROUP, N_BACK, 2 * N_BACK), lambda g: (g, 0, 0)),
        compiler_params=_params(("arbitrary",)),
    )(rel_bias, buckets)


def _bias_grad(name, dbs, buckets):
    nd = len(dbs)

    def body(*refs):
        bk = refs[nd][0]
        o_ref = refs[nd + 1]
        lane = lax.broadcasted_iota(jnp.int32, (1, 128), 1)
        db = [sum(r[h] for r in refs[:nd]) for h in range(HEADS_PER_GROUP)]
        for b in range(N_BUCKETS):
            row = jnp.zeros((1, 128), F32)
            for h in range(HEADS_PER_GROUP):
                s = jnp.sum(jnp.where(bk == b, db[h], 0.0), axis=0, keepdims=True)
                s = jnp.sum(s, axis=1, keepdims=True)
                row = jnp.where(lane // 32 == h, s, row)
            o_ref[0, b:b + 1, :] = row

    spec = pl.BlockSpec((HEADS_PER_GROUP, N_BACK, 2 * N_BACK), lambda g: (g, 0, 0))
    return pl.pallas_call(
        body, name=name, out_shape=jax.ShapeDtypeStruct((N_GROUPS, N_BUCKETS, 128), F32), grid=(N_GROUPS,),
        in_specs=[spec] * nd + [pl.BlockSpec((1, N_BACK, 2 * N_BACK), lambda g: (g, 0, 0))],
        out_specs=pl.BlockSpec((1, N_BUCKETS, 128), lambda g: (g, 0, 0)), compiler_params=_params(("arbitrary",)),
    )(*dbs, buckets)


def _head_cols(h):
    return slice(h * HEAD_DIM, (h + 1) * HEAD_DIM)


def _store_dilated(o_ref, tile_ref, val, d):
    nlb, rows, _ = tile_ref.shape
    width = 128 * nlb
    for lb in range(nlb):
        tile_ref[lb] = val[:, 128 * lb:128 * (lb + 1)]
    for r in range(d):
        for lb in range(nlb):
            c0 = r * width + 128 * lb
            o_ref[:, c0:c0 + 128] = tile_ref[lb, pl.ds(r, rows // d, stride=d), :].astype(o_ref.dtype)


def _attn_qkv_fwd(name, h, wq):
    t, k = h.shape
    tm = _pick(t, (512, 256))
    width = wq.shape[1]

    def body(h_ref, w0, w1, w2, o0, o1, o2, tile_ref):
        hv = h_ref[...]
        for g, (w_ref, o_ref) in enumerate(zip((w0, w1, w2), (o0, o1, o2))):
            acc = lax.dot_general(hv, w_ref[...], (((1,), (1,)), ((), ())), preferred_element_type=F32)
            if DILATIONS[g] == 1:
                o_ref[...] = acc.astype(BF16)
            else:
                _store_dilated(o_ref, tile_ref, acc, DILATIONS[g])

    return pl.pallas_call(
        body, name=name,
        out_shape=[jax.ShapeDtypeStruct((t // d, d * width), BF16) for d in DILATIONS], grid=(t // tm,),
        in_specs=[pl.BlockSpec((tm, k), lambda i: (i, 0))] + [pl.BlockSpec((width, k), lambda i: (0, 0))] * 3,
        out_specs=[pl.BlockSpec((tm // d, d * width), lambda i: (i, 0)) for d in DILATIONS],
        scratch_shapes=[pltpu.VMEM((width // 128, tm, 128), F32)],
        compiler_params=_params(("parallel",), V7X_VMEM_LIMIT_BYTES),
    )(h, wq[0], wq[1], wq[2])


def _attn_fwd(name, qkv, bias, g):
    d = DILATIONS[g]
    tq = qkv.shape[0]
    t = tq * d
    nblk = 3
    nsub = _pick(tq // N_BACK, (4, 2))
    scale = HEAD_DIM ** -0.5

    def body(q_ref, kp_ref, kc_ref, vp_ref, vc_ref, b_ref, o_ref, l_ref):
        m2 = pl.program_id(1)
        col = lax.broadcasted_iota(jnp.int32, (N_BACK, 2 * N_BACK), 1)
        lane = lax.broadcasted_iota(jnp.int32, (N_BACK, 128), 1)
        for sub in range(nsub):
            rows = slice(N_BACK * sub, N_BACK * (sub + 1))
            both = slice(N_BACK * (sub - 1), N_BACK * (sub + 1))
            lse_tile = jnp.zeros((N_BACK, 128), F32)
            outs = []
            for h in range(HEADS_PER_GROUP):
                hc = _head_cols(h)
                if sub == 0:
                    kk = jnp.concatenate([kp_ref[:, hc], kc_ref[0:N_BACK, hc]], axis=0)
                    vv = jnp.concatenate([vp_ref[:, hc], vc_ref[0:N_BACK, hc]], axis=0)
                else:
                    kk, vv = kc_ref[both, hc], vc_ref[both, hc]
                s = lax.dot_general(q_ref[rows, hc], kk, (((1,), (1,)), ((), ())), preferred_element_type=F32)
                s = s * scale + b_ref[h]
                if sub == 0:
                    s = jnp.where((col >= N_BACK) | (m2 > 0), s, NEG_INF)
                m = jnp.max(s, axis=-1, keepdims=True)
                p = jnp.exp(s - m)
                den = jnp.sum(p, axis=-1, keepdims=True)
                outs.append(jnp.dot(p.astype(BF16), vv, preferred_element_type=F32) / den)
                lse_tile = jnp.where(lane // 32 == h, m + jnp.log(den), lse_tile)
            o_ref[rows, :] = jnp.concatenate(outs, axis=1)
            l_ref[rows, :] = lse_tile

    def blk(part, prev):
        if prev:
            return pl.BlockSpec((N_BACK, GROUP_COLS), lambda r, n: (jnp.maximum(nsub * n - 1, 0), r * nblk + part))
        return pl.BlockSpec((nsub * N_BACK, GROUP_COLS), lambda r, n: (n, r * nblk + part))

    o, l = pl.pallas_call(
        body, name=name,
        out_shape=[jax.ShapeDtypeStruct((tq, d * GROUP_COLS), F32), jax.ShapeDtypeStruct((tq, d * 128), F32)],
        grid=(d, tq // (nsub * N_BACK)),
        in_specs=[blk(0, False), blk(1, True), blk(1, False), blk(2, True), blk(2, False),
                  pl.BlockSpec((HEADS_PER_GROUP, N_BACK, 2 * N_BACK), lambda r, n: (g, 0, 0))],
        out_specs=[pl.BlockSpec((nsub * N_BACK, GROUP_COLS), lambda r, n: (n, r)),
                   pl.BlockSpec((nsub * N_BACK, 128), lambda r, n: (n, r))],
        compiler_params=_params(("parallel", "parallel")),
    )(qkv, qkv, qkv, qkv, qkv, bias)
    return o, l


def _group_weights(l_refs, h):
    ls = [l_ref[:, 32 * h:32 * h + 1] for l_ref in l_refs]
    m = jnp.maximum(jnp.maximum(ls[0], ls[1]), ls[2])
    es = [jnp.exp(l - m) for l in ls]
    tot = es[0] + es[1] + es[2]
    return [e / tot for e in es]


def _load_dilated(v_ref, n_ref, tile_ref, d):
    nlb, rows, _ = tile_ref.shape
    width = 128 * nlb
    for r in range(d):
        for lb in range(nlb):
            c0 = r * width + 128 * lb
            tile_ref[lb, pl.ds(r, rows // d, stride=d), :] = v_ref[:, c0:c0 + 128]
    n_ref[...] = jnp.concatenate([tile_ref[lb] for lb in range(nlb)], axis=1)


def _attn_merge_out(name, os_, ls, wot, x, gain):
    t, d_model = x.shape
    tr = _pick(t, (512, 256))

    def body(o0, o1v, o2v, l0, l1v, l2v, w_ref, x_ref, g_ref, om_ref, out_ref, h_ref, o1, o2, l1, l2,
             otile_ref, ltile_ref):
        _load_dilated(o1v, o1, otile_ref, DILATIONS[1])
        _load_dilated(o2v, o2, otile_ref, DILATIONS[2])
        _load_dilated(l1v, l1, ltile_ref, DILATIONS[1])
        _load_dilated(l2v, l2, ltile_ref, DILATIONS[2])
        o_refs = (o0, o1, o2)
        pieces = [[None] * HEADS_PER_GROUP for _ in range(N_GROUPS)]
        for h in range(HEADS_PER_GROUP):
            al = _group_weights((l0, l1, l2), h)
            for g in range(N_GROUPS):
                pieces[g][h] = o_refs[g][:, _head_cols(h)] * al[g]
        om = jnp.concatenate([p for row in pieces for p in row], axis=1).astype(BF16)
        om_ref[...] = om
        out = x_ref[...] + lax.dot_general(om, w_ref[...], (((1,), (1,)), ((), ())), preferred_element_type=F32)
        out_ref[...] = out
        h_ref[...] = _rms_rows(out, g_ref[...])

    so = pl.BlockSpec((tr, GROUP_COLS), lambda i: (i, 0))
    sl = pl.BlockSpec((tr, 128), lambda i: (i, 0))
    sx = pl.BlockSpec((tr, d_model), lambda i: (i, 0))
    views = [pl.BlockSpec((tr // d, d * GROUP_COLS), lambda i: (i, 0)) for d in DILATIONS]
    views += [pl.BlockSpec((tr // d, d * 128), lambda i: (i, 0)) for d in DILATIONS]
    res = pl.pallas_call(
        body, name=name,
        out_shape=[jax.ShapeDtypeStruct((t, D_ATTN), BF16), jax.ShapeDtypeStruct((t, d_model), F32),
                   jax.ShapeDtypeStruct((t, d_model), BF16)]
        + [jax.ShapeDtypeStruct((t, GROUP_COLS), F32)] * 2 + [jax.ShapeDtypeStruct((t, 128), F32)] * 2,
        grid=(t // tr,),
        in_specs=views + [pl.BlockSpec((d_model, D_ATTN), lambda i: (0, 0)), sx,
                          pl.BlockSpec((1, d_model), lambda i: (0, 0))],
        out_specs=[pl.BlockSpec((tr, D_ATTN), lambda i: (i, 0)), sx, sx, so, so, sl, sl],
        scratch_shapes=[pltpu.VMEM((GROUP_COLS // 128, tr, 128), F32), pltpu.VMEM((1, tr, 128), F32)],
        compiler_params=_params(("parallel",), V7X_VMEM_LIMIT_BYTES),
    )(*os_, *ls, wot, x, gain)
    om, x_new, h_next, o1, o2, l1, l2 = res
    return om, x_new, h_next, [os_[0], o1, o2], [ls[0], l1, l2]


def _attn_bwd_prep(name, dx, wot, os_, ls):
    t, d_model = dx.shape
    tr = _pick(t, (512, 256))

    def body(dx_ref, w_ref, o0, o1, o2, l0, l1, l2, d0, d1, d2, c0, c1, c2, dtile_ref, ctile_ref):
        o_refs, d_refs, c_refs = (o0, o1, o2), (d0, d1, d2), (c0, c1, c2)
        d_out = jnp.dot(dx_ref[...].astype(BF16), w_ref[...], preferred_element_type=F32)
        lane = lax.broadcasted_iota(jnp.int32, (tr, 128), 1)
        dos = [[None] * HEADS_PER_GROUP for _ in range(N_GROUPS)]
        cs = [jnp.zeros((tr, 128), F32) for _ in range(N_GROUPS)]
        for h in range(HEADS_PER_GROUP):
            al = _group_weights((l0, l1, l2), h)
            tot = jnp.zeros((tr, 1), F32)
            for g in range(N_GROUPS):
                dv = d_out[:, g * GROUP_COLS + h * HEAD_DIM:g * GROUP_COLS + (h + 1) * HEAD_DIM]
                tot = tot + al[g] * jnp.sum(dv * o_refs[g][:, _head_cols(h)], axis=-1, keepdims=True)
                dos[g][h] = dv * al[g]
            for g in range(N_GROUPS):
                cs[g] = jnp.where(lane // 32 == h, -al[g] * tot, cs[g])
        for g in range(N_GROUPS):
            do_g = jnp.concatenate(dos[g], axis=1)
            if DILATIONS[g] == 1:
                d_refs[g][...] = do_g.astype(BF16)
                c_refs[g][...] = cs[g]
            else:
                _store_dilated(d_refs[g], dtile_ref, do_g, DILATIONS[g])
                _store_dilated(c_refs[g], ctile_ref, cs[g], DILATIONS[g])

    so = pl.BlockSpec((tr, GROUP_COLS), lambda i: (i, 0))
    sl = pl.BlockSpec((tr, 128), lambda i: (i, 0))
    res = pl.pallas_call(
        body, name=name,
        out_shape=[jax.ShapeDtypeStruct((t // d, d * GROUP_COLS), BF16) for d in DILATIONS]
        + [jax.ShapeDtypeStruct((t // d, d * 128), F32) for d in DILATIONS],
        grid=(t // tr,),
        in_specs=[pl.BlockSpec((tr, d_model), lambda i: (i, 0)), pl.BlockSpec((d_model, D_ATTN), lambda i: (0, 0))]
        + [so] * 3 + [sl] * 3,
        out_specs=[pl.BlockSpec((tr // d, d * GROUP_COLS), lambda i: (i, 0)) for d in DILATIONS]
        + [pl.BlockSpec((tr // d, d * 128), lambda i: (i, 0)) for d in DILATIONS],
        scratch_shapes=[pltpu.VMEM((GROUP_COLS // 128, tr, 128), F32), pltpu.VMEM((1, tr, 128), F32)],
        compiler_params=_params(("parallel",), V7X_VMEM_LIMIT_BYTES),
    )(dx, wot, *os_, *ls)
    return res[:3], res[3:]


def _attn_bwd(name, qkv, do, lse, cterm, bias, g):
    d = DILATIONS[g]
    tq = qkv.shape[0]
    t = tq * d
    nb = tq // N_BACK
    nblk = 3
    scale = HEAD_DIM ** -0.5
    nt = (((1,), (1,)), ((), ()))
    tn = (((0,), (0,)), ((), ()))

    def body(q2, qx, kp, k2, vp, v2, do2, dox, l2, lx, c2, cx, b_ref, dqkv_ref, db_ref):
        m2 = pl.program_id(1)

        @pl.when((m2 == 0) & (pl.program_id(0) == 0))
        def _():
            db_ref[...] = jnp.zeros_like(db_ref)

        row2 = lax.broadcasted_iota(jnp.int32, (2 * N_BACK, N_BACK), 0)
        lo, hi = slice(0, N_BACK), slice(N_BACK, 2 * N_BACK)
        for sub in range(2):
            rows = hi if sub else lo
            has_prev = True if sub else m2 > 0
            has_next = 2 * m2 + 2 < nb if sub else True
            on_ac = (row2 < N_BACK) | has_next
            dqs, dks, dvs = [], [], []
            for h in range(HEADS_PER_GROUP):
                hc = _head_cols(h)
                st = slice(32 * h, 32 * h + 1)
                b_prev, b_same = b_ref[h, :, 0:N_BACK], b_ref[h, :, N_BACK:]
                q0, k1, v1, d0 = q2[rows, hc], k2[rows, hc], v2[rows, hc], do2[rows, hc]
                l0, c0 = l2[rows, st], c2[rows, st]
                if sub:
                    k0, v0 = k2[lo, hc], v2[lo, hc]
                    q_ac = jnp.concatenate([q0, qx[:, hc]], axis=0)
                    d_ac = jnp.concatenate([d0, dox[:, hc]], axis=0)
                    l_ac = jnp.concatenate([l0, lx[:, st]], axis=0)
                    c_ac = jnp.concatenate([c0, cx[:, st]], axis=0)
                else:
                    k0, v0 = kp[:, hc], vp[:, hc]
                    q_ac, d_ac, l_ac, c_ac = q2[:, hc], do2[:, hc], l2[:, st], c2[:, st]
                s_ac = (lax.dot_general(q_ac, k1, nt, preferred_element_type=F32) * scale
                        + jnp.concatenate([b_same, b_prev], axis=0))
                p_ac = jnp.where(on_ac, jnp.exp(s_ac - l_ac), 0.0)
                ds_ac = p_ac * (lax.dot_general(d_ac, v1, nt, preferred_element_type=F32) + c_ac)
                s_b = lax.dot_general(q0, k0, nt, preferred_element_type=F32) * scale + b_prev
                p_b = jnp.where(has_prev, jnp.exp(s_b - l0), 0.0)
                ds_b = p_b * (lax.dot_general(d0, v0, nt, preferred_element_type=F32) + c0)
                ds_a = ds_ac[0:N_BACK]
                dqs.append(scale * jnp.dot(jnp.concatenate([ds_b, ds_a], axis=1).astype(BF16),
                                           jnp.concatenate([k0, k1], axis=0), preferred_element_type=F32))
                dks.append(scale * lax.dot_general(ds_ac.astype(BF16), q_ac, tn, preferred_element_type=F32))
                dvs.append(lax.dot_general(p_ac.astype(BF16), d_ac, tn, preferred_element_type=F32))
                db_ref[h, :, 0:N_BACK] += ds_b
                db_ref[h, :, N_BACK:] += ds_a
            dqkv_ref[rows, :] = jnp.concatenate(dqs + dks + dvs, axis=1).astype(BF16)

    def blk(width, which, col):
        if which == "prev":
            return pl.BlockSpec((N_BACK, width), lambda r, n: (jnp.maximum(2 * n - 1, 0), col(r)))
        if which == "next":
            return pl.BlockSpec((N_BACK, width), lambda r, n: (jnp.minimum(2 * n + 2, nb - 1), col(r)))
        return pl.BlockSpec((2 * N_BACK, width), lambda r, n: (n, col(r)))

    def qkv_blk(part, which):
        return blk(GROUP_COLS, which, lambda r: r * nblk + part)

    def grp_blk(width, which):
        return blk(width, which, lambda r: r)

    qv, dov, lv, cv = qkv, do, lse, cterm
    dqkv_g, db = pl.pallas_call(
        body, name=name,
        out_shape=[jax.ShapeDtypeStruct((tq, d * 3 * GROUP_COLS), BF16),
                   jax.ShapeDtypeStruct((HEADS_PER_GROUP, N_BACK, 2 * N_BACK), F32)],
        grid=(d, nb // 2),
        in_specs=[qkv_blk(0, "same"), qkv_blk(0, "next"), qkv_blk(1, "prev"), qkv_blk(1, "same"),
                  qkv_blk(2, "prev"), qkv_blk(2, "same"), grp_blk(GROUP_COLS, "same"), grp_blk(GROUP_COLS, "next"),
                  grp_blk(128, "same"), grp_blk(128, "next"), grp_blk(128, "same"), grp_blk(128, "next"),
                  pl.BlockSpec((HEADS_PER_GROUP, N_BACK, 2 * N_BACK), lambda r, n: (g, 0, 0))],
        out_specs=[pl.BlockSpec((2 * N_BACK, 3 * GROUP_COLS), lambda r, n: (n, r)),
                   pl.BlockSpec((HEADS_PER_GROUP, N_BACK, 2 * N_BACK), lambda r, n: (0, 0, 0))],
        compiler_params=_params(("arbitrary", "arbitrary")),
    )(qv, qv, qv, qv, qv, qv, dov, dov, lv, lv, cv, cv, bias)
    return dqkv_g.reshape(t, 3 * GROUP_COLS), db


def _row(v):
    return v.reshape(1, -1)


def _glu_epi(accs, extras, rows):
    a = (accs[0] + rows[0]).astype(BF16)
    gt = (accs[1] + rows[1]).astype(BF16)
    return a, gt, a.astype(F32) * _sigmoid(gt.astype(F32))


def _swiglu_epi(accs, extras, rows):
    gq, uq = accs[0].astype(BF16), accs[1].astype(BF16)
    gf = gq.astype(F32)
    return gq, uq, gf * _sigmoid(gf) * uq.astype(F32)


def _group_rows(w):
    parts = [w[p * D_ATTN:(p + 1) * D_ATTN].reshape(N_GROUPS, GROUP_COLS, -1) for p in range(3)]
    return jnp.concatenate(parts, axis=1)


def _ungroup_rows(wg):
    return jnp.concatenate([wg[g][p * GROUP_COLS:(p + 1) * GROUP_COLS] for p in range(3) for g in range(N_GROUPS)],
                           axis=0)


def _local_step(x, target, sm, depth, fetch, emit):
    d_model = x.shape[1]
    buckets = jnp.asarray(_bucket_tables())
    bias = _bias_build("bias_build", sm["rel_bias"], buckets)
    saved = []
    h = _rmsnorm_fwd("rms_mix_fwd0", x, _row(sm["norm_mix"][0]))
    for i in range(depth):
        j = i // 2
        rec = {"x_mix": x}
        wm = fetch(2 * i, x)
        rec.update(h_mix=h, wm=wm)
        ffn_gain = _row(sm["norm_ffn"][i])
        if i % 2 == 0:
            c = wm["w1t"].shape[0] // 2
            tn = _pick(c, (1024, 512, 256, 128))
            b1 = sm["conv_b_pw1"][j]
            a, gt, glu = _mm_nt(f"conv_pw1_fwd{j}", h, [wm["w1t"]] * 2, [0, c // tn], c, _glu_epi, (BF16,) * 3,
                                rows=[_row(b1[:c]), _row(b1[c:])], tn=tn)
            wm.update(fetch(2 * i, a, 1))
            dw, s = _dwconv_fwd(f"dwconv_fwd{j}", glu, wm["wdw"], _row(sm["conv_b_dw"][j]),
                                _row(sm["conv_ln_g"][j]), _row(sm["conv_ln_b"][j]))
            x, h2 = _project_residual(f"conv_pw2_fwd{j}", s, wm["w2"], x, bias=_row(sm["conv_b_pw2"][j]),
                                      gain=ffn_gain)
            rec.update(a=a, gt=gt, dw=dw, s=s)
        else:
            wq = _group_rows(wm["wqkvt"])
            qkv = _attn_qkv_fwd(f"attn_qkv_fwd{j}", h, wq)
            og = [_attn_fwd(f"attn_fwd{j}_{g}", qkv[g], bias, g) for g in range(N_GROUPS)]
            om, x, h2, os_, ls = _attn_merge_out(f"attn_out_fwd{j}", [o for o, _ in og], [l for _, l in og],
                                                 wm["wot"], x, ffn_gain)
            rec.update(qkv=qkv, os=os_, ls=ls, lse_views=[l for _, l in og], om=om, wq=wq)
        rec["x_ffn"] = x
        wf = fetch(2 * i + 1, x)
        f = wf["wd"].shape[0]
        gq, uq, act = _mm_nt(f"ffn_up_fwd{i}", h2, [wf["wgt"], wf["wut"]], [0, 0], f, _swiglu_epi, (BF16,) * 3)
        x, h = _project_residual(f"ffn_down_fwd{i}", act, wf["wd"], x,
                                 gain=_row(sm["norm_mix"][i + 1]) if i + 1 < depth else None)
        rec.update(h_ffn=h2, gq=gq, uq=uq, wf=wf)
        saved.append(rec)

    dx, g_final, loss_cols = _loss_head("loss_head", x, _row(sm["final_norm"]), target)

    g_mix, g_ffn = [None] * depth, [None] * depth
    nconv = (depth + 1) // 2
    g_b1, g_bdw, g_lng, g_lnb, g_b2 = ([None] * nconv for _ in range(5))
    dbias = []
    for i in reversed(range(depth)):
        j = i // 2
        rec = saved[i]
        wm, wf = rec["wm"], rec["wf"]
        dgate, dup, dwd = _ffn_down_bwd(f"ffn_down_bwd{i}", dx, wf["wd"], rec["gq"], rec["uq"])
        gf = {"wd": dwd}
        gf["wgt"], gf["wut"] = _mm_tn_pair(f"ffn_gate_up_dw{i}", dgate, dup, rec["h_ffn"])
        dx, g_ffn[i] = _mm_nn_rms_bwd(f"ffn_up_bwd{i}", [dgate, dup], [wf["wgt"], wf["wut"]], rec["x_ffn"],
                                      _row(sm["norm_ffn"][i]), dx)
        dx = emit(2 * i + 1, gf, dx)
        if i % 2 == 0:
            c = wm["w2"].shape[0]
            gm = {}
            gm["w2"], g_b2[j] = _mm_tn(f"conv_pw2_dw{j}", rec["s"], dx, colsum_b=True)
            ddw, sums = _ln_silu_bwd(f"conv_pw2_bwd{j}", dx, wm["w2"], rec["dw"], _row(sm["conv_ln_g"][j]),
                                     _row(sm["conv_ln_b"][j]))
            g_lng[j], g_lnb[j], g_bdw[j] = sums[0], sums[1], sums[2]
            du, dwk, db1 = _dwconv_bwd(f"dwconv_bwd{j}", ddw, rec["a"], rec["gt"], wm["wdw"])
            gm["wdw"] = dwk[:CONV_WIDTH]
            g_b1[j] = db1[0]
            gm["w1t"] = _mm_tn(f"conv_pw1_dw{j}", du, rec["h_mix"])
            dh_terms = ([du], [wm["w1t"]])
        else:
            gm = {"wot": _mm_tn(f"attn_out_dw{j}", dx, rec["om"])}
            dos, cs = _attn_bwd_prep(f"attn_out_bwd{j}", dx, wm["wot"], rec["os"], rec["ls"])
            back = [_attn_bwd(f"attn_bwd{j}_{g}", rec["qkv"][g], dos[g], rec["lse_views"][g], cs[g], bias, g)
                    for g in range(N_GROUPS)]
            dqkv = [b[0] for b in back]
            dbias.append(jnp.concatenate([b[1] for b in back], axis=0))
            dh_terms = (dqkv, [rec["wq"][g] for g in range(N_GROUPS)])
            gm["wqkvt"] = _ungroup_rows([_mm_tn(f"attn_qkv_dw{j}_{g}", dqkv[g], rec["h_mix"])
                                         for g in range(N_GROUPS)])
        dx, g_mix[i] = _mm_nn_rms_bwd(f"mix_in_bwd{i}", dh_terms[0], dh_terms[1], rec["x_mix"],
                                      _row(sm["norm_mix"][i]), dx)
        dx = emit(2 * i, gm, dx)

    gb = _bias_grad("bias_grad", dbias, buckets)
    g_rel = jnp.transpose(gb[:, :, ::32], (1, 0, 2)).reshape(N_BUCKETS, N_HEADS)
    gsm = {
        "norm_mix": jnp.concatenate(g_mix, axis=0), "norm_ffn": jnp.concatenate(g_ffn, axis=0),
        "final_norm": g_final[0], "conv_b_pw1": jnp.stack(g_b1), "conv_b_dw": jnp.stack(g_bdw),
        "conv_ln_g": jnp.stack(g_lng), "conv_ln_b": jnp.stack(g_lnb),
        "conv_b_pw2": jnp.concatenate(g_b2, axis=0), "rel_bias": g_rel,
    }
    return loss_cols, dx, gsm


SMALL = ("norm_mix", "norm_ffn", "final_norm", "conv_b_pw1", "conv_b_dw", "conv_ln_g", "conv_ln_b", "conv_b_pw2",
         "rel_bias")
SHARDED = (("conv_w_pw1", "w1t", True), ("conv_w_pw2", "w2", False), ("attn_w_qkv", "wqkvt", True),
           ("attn_w_o", "wot", True), ("ffn_w_gate", "wgt", True), ("ffn_w_up", "wut", True),
           ("ffn_w_down", "wd", False))
ORDER = ("norm_mix", "norm_ffn", "final_norm", "conv_w_pw1", "conv_b_pw1", "conv_w_dw", "conv_b_dw", "conv_ln_g",
         "conv_ln_b", "conv_w_pw2", "conv_b_pw2", "attn_w_qkv", "attn_w_o", "rel_bias", "ffn_w_gate", "ffn_w_up",
         "ffn_w_down")
PACK_LANES = 128
PACK_ROW_TILE = 8


def _pack_small(vals):
    flat = jnp.concatenate([vals[n].reshape(-1) for n in SMALL])
    per_tile = PACK_LANES * PACK_ROW_TILE
    return jnp.pad(flat, (0, -flat.shape[0] % per_tile)).reshape(-1, PACK_LANES)


def _unpack_small(pack, like):
    flat, out, pos = pack.reshape(-1), {}, 0
    for n in SMALL:
        out[n] = flat[pos:pos + like[n].size].reshape(like[n].shape)
        pos += like[n].size
    return out


def _dw_blocks(w):
    l, k, c = w.shape
    blk = jnp.transpose(w.reshape(l, k, N_DEV, c // N_DEV), (2, 0, 1, 3)).reshape(N_DEV, l * k, c // N_DEV)
    return jnp.pad(blk, ((0, 0), (0, -(l * k) % 8), (0, 0)))


def kernel(x, norm_mix, norm_ffn, final_norm, conv_w_pw1, conv_b_pw1, conv_w_dw, conv_b_dw, conv_ln_g, conv_ln_b, conv_w_pw2, conv_b_pw2, attn_w_qkv, attn_w_o, rel_bias, ffn_w_gate, ffn_w_up, ffn_w_down, loss_target, m_norm_mix, m_norm_ffn, m_final_norm, m_conv_w_pw1, m_conv_b_pw1, m_conv_w_dw, m_conv_b_dw, m_conv_ln_g, m_conv_ln_b, m_conv_w_pw2, m_conv_b_pw2, m_attn_w_qkv, m_attn_w_o, m_rel_bias, m_ffn_w_gate, m_ffn_w_up, m_ffn_w_down, v_norm_mix, v_norm_ffn, v_final_norm, v_conv_w_pw1, v_conv_b_pw1, v_conv_w_dw, v_conv_b_dw, v_conv_ln_g, v_conv_ln_b, v_conv_w_pw2, v_conv_b_pw2, v_attn_w_qkv, v_attn_w_o, v_rel_bias, v_ffn_w_gate, v_ffn_w_up, v_ffn_w_down):
    w = dict(norm_mix=norm_mix, norm_ffn=norm_ffn, final_norm=final_norm, conv_w_pw1=conv_w_pw1,
             conv_b_pw1=conv_b_pw1, conv_w_dw=conv_w_dw, conv_b_dw=conv_b_dw, conv_ln_g=conv_ln_g,
             conv_ln_b=conv_ln_b, conv_w_pw2=conv_w_pw2, conv_b_pw2=conv_b_pw2, attn_w_qkv=attn_w_qkv,
             attn_w_o=attn_w_o, rel_bias=rel_bias, ffn_w_gate=ffn_w_gate, ffn_w_up=ffn_w_up, ffn_w_down=ffn_w_down)
    m = dict(norm_mix=m_norm_mix, norm_ffn=m_norm_ffn, final_norm=m_final_norm, conv_w_pw1=m_conv_w_pw1,
             conv_b_pw1=m_conv_b_pw1, conv_w_dw=m_conv_w_dw, conv_b_dw=m_conv_b_dw, conv_ln_g=m_conv_ln_g,
             conv_ln_b=m_conv_ln_b, conv_w_pw2=m_conv_w_pw2, conv_b_pw2=m_conv_b_pw2, attn_w_qkv=m_attn_w_qkv,
             attn_w_o=m_attn_w_o, rel_bias=m_rel_bias, ffn_w_gate=m_ffn_w_gate, ffn_w_up=m_ffn_w_up,
             ffn_w_down=m_ffn_w_down)
    v = dict(norm_mix=v_norm_mix, norm_ffn=v_norm_ffn, final_norm=v_final_norm, conv_w_pw1=v_conv_w_pw1,
             conv_b_pw1=v_conv_b_pw1, conv_w_dw=v_conv_w_dw, conv_b_dw=v_conv_b_dw, conv_ln_g=v_conv_ln_g,
             conv_ln_b=v_conv_ln_b, conv_w_pw2=v_conv_w_pw2, conv_b_pw2=v_conv_b_pw2, attn_w_qkv=v_attn_w_qkv,
             attn_w_o=v_attn_w_o, rel_bias=v_rel_bias, ffn_w_gate=v_ffn_w_gate, ffn_w_up=v_ffn_w_up,
             ffn_w_down=v_ffn_w_down)

    me = 4 * lax.axis_index("x") + 2 * lax.axis_index("y") + lax.axis_index("c")
    depth = ffn_w_gate.shape[0]
    n_conv, _, cb = conv_w_dw.shape

    def sublayer(key, layer):
        if key in ("wgt", "wut", "wd"):
            return 2 * layer + 1
        return 4 * layer if key in ("w1t", "w2") else 4 * layer + 2

    def landing(block, own):
        land = lax.empty((N_DEV,) + block.shape, block.dtype)
        return lax.dynamic_update_slice(land, own[None], (me,) + (0,) * block.ndim)

    by_sub = {s: [] for s in range(2 * depth)}
    for name, key, cols in SHARDED:
        sw = (jnp.swapaxes(w[name], 1, 2) if cols else w[name]).astype(BF16)
        for layer in range(sw.shape[0]):
            by_sub[sublayer(key, layer)].append((key, layer, sw[layer]))
    likes = {s: jnp.zeros((sum(sh.size for _, _, sh in by_sub[s]) // 1024, 1024), BF16) for s in by_sub}
    dw_shard = jnp.pad(conv_w_dw.reshape(-1, cb), ((0, -(n_conv * CONV_WIDTH) % 8), (0, 0)))
    like_dw = jnp.zeros(dw_shard.shape, F32)
    stages = {}
    for s in range(2 * depth):
        for entry in by_sub[s]:
            stages.setdefault((s, 0 if entry[0] != "w2" else 1), []).append(entry)
    stage_order = sorted(stages)
    groups = [([sh for _, _, sh in stages[st]], [landing(sh, sh) for _, _, sh in stages[st]]) for st in stage_order]
    groups.insert(1, ([dw_shard], [landing(dw_shard, dw_shard)]))
    gather, _ = _exchange_start("gather_start", groups, scatter=False, carry=jnp.zeros((8, 128), F32))
    handle = dict(zip(stage_order, gather[:1] + gather[2:]))
    dw_filters = []

    def fetch(s, after, part=0):
        st = (s, part)
        like = jnp.zeros((sum(sh.size for _, _, sh in stages[st]) // 1024, 1024), BF16)
        pieces = [(handle[st], like)]
        if st == (0, 1):
            pieces.append((gather[1], like_dw))
        landed = _exchange_wait(f"gather_wait{s}_{part}", pieces, after)
        out = {key: g.reshape(g.shape[0] * g.shape[1], g.shape[2]) for (key, _, _), g in zip(stages[st], landed[0])}
        if st == (0, 1):
            dw_all = landed[1][0]
            full = jnp.transpose(dw_all[:, :n_conv * CONV_WIDTH].reshape(N_DEV, n_conv, CONV_WIDTH, cb), (1, 2, 0, 3))
            full = jnp.pad(full.reshape(n_conv, CONV_WIDTH, N_DEV * cb), ((0, 0), (0, CONV_HALO - CONV_WIDTH), (0, 0)))
            dw_filters.extend(full[layer] for layer in range(n_conv))
        if "w2" in out:
            out["wdw"] = dw_filters[s // 4]
        return out

    scatter, dw_grads, started = {}, {}, {}

    def emit(s, gd, carry):
        parts = [gd[key].reshape(N_DEV, -1, gd[key].shape[1]) for key, _, _ in by_sub[s]]
        if "wdw" in gd:
            dw_grads[s // 4] = gd["wdw"]
        if s == 0:
            parts.append(_dw_blocks(jnp.stack([dw_grads[layer] for layer in range(n_conv)])))
        lands = [landing(p[0], lax.dynamic_index_in_dim(p, me, 0, keepdims=False)) for p in parts]
        groups = [(parts[:len(by_sub[s])], lands[:len(by_sub[s])])]
        if s == 0:
            groups.append((parts[-1:], lands[-1:]))
        if s == 0:
            scatter[s], started[0] = _exchange_start(f"scatter_start{s}", groups, scatter=True,
                                                     carry=jnp.zeros((8, 128), F32))
            return carry
        scatter[s], carry = _exchange_start(f"scatter_start{s}", groups, scatter=True, carry=carry)
        return carry

    sm = {n: w[n] for n in SMALL}
    loss_cols, dx, gsm = _local_step(x[0], loss_target[0], sm, depth, fetch, emit)
    loss = lax.psum(jnp.sum(loss_cols), ("x", "y", "c"))

    grads, summed, delta, new_m, new_v = {}, {}, {}, {}, {}

    def reduce_pieces(subs, landed):
        for s, recv in zip(subs, landed):
            for (key, layer, _), r in zip(by_sub[s], recv):
                summed[key, layer] = _sum8(f"sum_{key}{layer}", r)

    def update(names):
        for name in names:
            shape = w[name].shape
            res = _adamw(f"adamw_{name}",
                         *[t.reshape(-1, shape[-1]) for t in (grads[name], w[name], m[name], v[name])])
            delta[name], new_m[name], new_v[name] = (t.reshape(shape) for t in res)

    def stacked(name, key, cols):
        g = jnp.stack([summed[key, layer] for layer in range(w[name].shape[0])])
        return jnp.swapaxes(g, 1, 2) if cols else g

    early = sorted((s for s in scatter if s != 0), reverse=True)
    reduce_pieces(early, _exchange_wait("scatter_wait_early", [(scatter[s][0], likes[s]) for s in early],
                                        started[0]))
    late_names = [name for name, key, _ in SHARDED if any(k == key for k, _, _ in by_sub[0])]
    for name, key, cols in SHARDED:
        if name not in late_names:
            grads[name] = stacked(name, key, cols)
    early_names = [name for name, _, _ in SHARDED if name not in late_names]
    update(early_names)
    pack = _pack_small(gsm)
    ((pack_all,),) = _exchange("gather_small_grads", [([pack], pack)], scatter=False)
    pack_sum = _sum8("sum_small", pack_all)
    grads.update(_unpack_small(pack_sum, sm))

    landed = _exchange_wait("scatter_wait_last", [(scatter[0][0], likes[0]), (scatter[0][1], like_dw)],
                            new_v[early_names[-1]])
    reduce_pieces([0], landed[:1])
    for name, key, cols in SHARDED:
        if name in late_names:
            grads[name] = stacked(name, key, cols)
    grads["conv_w_dw"] = _sum8("sum_wdw", landed[1][0])[:n_conv * CONV_WIDTH].reshape(conv_w_dw.shape)
    update(late_names + ["conv_w_dw"])
    res = _adamw("adamw_small", pack_sum, _pack_small(sm), _pack_small({n: m[n] for n in SMALL}),
                 _pack_small({n: v[n] for n in SMALL}))
    for dst, t in zip((delta, new_m, new_v), res):
        dst.update(_unpack_small(t, sm))

    outs = [loss, dx[None]]
    for d in (grads, delta, new_m, new_v):
        outs += [d[n] for n in ORDER]
    return tuple(outs)
```

```python
import functools
import math

import numpy as np
import jax
import jax.numpy as jnp
from jax import lax
from jax.experimental import pallas as pl
from jax.experimental.pallas import tpu as pltpu

F32 = jnp.float32
BF16 = jnp.bfloat16

N_DEV = 8
HEAD_DIM = 64
HEADS_PER_GROUP = 4
GROUP_COLS = HEADS_PER_GROUP * HEAD_DIM
DILATIONS = (1, 4, 16)
N_BACK = 128
N_GROUPS = 3
N_HEADS = 12
D_ATTN = 768
N_BUCKETS = 32
REL_MAX_DISTANCE = 2048
CONV_WIDTH = 31
CONV_HALO = 32
EPS = 1e-6
NEG_INF = -1e30
ADAM_LR, ADAM_B1, ADAM_B2, ADAM_EPS, ADAM_WD, ADAM_STEP = 0.001, 0.9, 0.999, 1e-08, 0.01, 10
V7X_VMEM_LIMIT_BYTES = 56 * 1024 * 1024
MESH = pl.DeviceIdType.MESH
ANY = pl.BlockSpec(memory_space=pl.ANY)


def _pick(n, prefs):
    for p in prefs:
        if n % p == 0:
            return p
    return n


def _params(sem, vmem=None):
    return pltpu.CompilerParams(dimension_semantics=sem, vmem_limit_bytes=vmem)


def _sigmoid(x):
    return 1.0 / (1.0 + jnp.exp(-x))


def _exchange(name, groups, scatter):
    n_arr = [len(arrs) for arrs, _ in groups]
    n_in = sum(n_arr) + len(groups)
    ng = len(groups)

    def body(*refs):
        ins, outs, (send_sems, recv_sems, local_sems) = refs[:n_in], refs[n_in:-3], refs[-3:]
        x, y, c = lax.axis_index("x"), lax.axis_index("y"), lax.axis_index("c")
        me = 4 * x + 2 * y + c
        pos_in = pos_out = 0
        plans = []
        for gi in range(ng):
            srcs = ins[pos_in:pos_in + n_arr[gi]]
            like = ins[pos_in + n_arr[gi]]
            dsts = outs[pos_out:pos_out + n_arr[gi]]
            pos_in += n_arr[gi] + 1
            pos_out += n_arr[gi]
            plans.append((gi, srcs, like, dsts))
        local = []
        for gi, srcs, like, dsts in plans:
            for s, d in zip(srcs, dsts):
                cp = pltpu.make_async_copy(s.at[me] if scatter else s, d.at[me], local_sems.at[gi])
                cp.start()
                local.append(cp)
        for delta in range(1, N_DEV):
            dx, dy, dc = (delta >> 2) & 1, (delta >> 1) & 1, delta & 1
            px, py, pc = (1 - x if dx else x), (1 - y if dy else y), (1 - c if dc else c)
            peer = 4 * px + 2 * py + pc
            for gi, srcs, like, dsts in plans:
                for s, d in zip(srcs, dsts):
                    pltpu.make_async_remote_copy(
                        src_ref=s.at[peer] if scatter else s, dst_ref=d.at[me],
                        send_sem=send_sems.at[gi, delta - 1], recv_sem=recv_sems.at[gi, delta - 1],
                        device_id=(px, py, pc), device_id_type=MESH).start()
        for delta in range(1, N_DEV):
            for gi, srcs, like, dsts in plans:
                pltpu.make_async_remote_copy(
                    src_ref=like, dst_ref=like, send_sem=send_sems.at[gi, delta - 1],
                    recv_sem=recv_sems.at[gi, delta - 1], device_id=(x, y, c), device_id_type=MESH).wait()
        for cp in local:
            cp.wait()

    operands, out_shape = [], []
    for arrs, like in groups:
        operands += list(arrs) + [like]
        for a in arrs:
            blk = a.shape[1:] if scatter else a.shape
            out_shape.append(jax.ShapeDtypeStruct((N_DEV,) + tuple(blk), a.dtype))
    outs = pl.pallas_call(
        body, name=name, out_shape=out_shape, in_specs=[ANY] * len(operands), out_specs=[ANY] * len(out_shape),
        scratch_shapes=[pltpu.SemaphoreType.DMA((ng, N_DEV - 1)), pltpu.SemaphoreType.DMA((ng, N_DEV - 1)),
                        pltpu.SemaphoreType.DMA((ng,))],
        compiler_params=pltpu.CompilerParams(has_side_effects=True),
    )(*operands)
    res, pos = [], 0
    for n in n_arr:
        res.append(list(outs[pos:pos + n]))
        pos += n
    return res


HBM = pl.BlockSpec(memory_space=pltpu.HBM)
SEM = pl.BlockSpec(memory_space=pltpu.SEMAPHORE)
EFFECT = pltpu.SideEffectType.DATAFLOW_SIDE_EFFECTING


def _in_hbm(a):
    return pltpu.with_memory_space_constraint(a, pltpu.HBM)


def _exchange_start(name, groups, scatter, carry):
    ns = [len(s) for s, _ in groups]
    n_in = 2 * sum(ns)

    def body(*refs):
        ins, outs = refs[:n_in], refs[n_in + 1:]
        x, y, c = lax.axis_index("x"), lax.axis_index("y"), lax.axis_index("c")
        me = 4 * x + 2 * y + c
        pi = po = 0
        for n in ns:
            srcs, lands = ins[pi:pi + n], ins[pi + n:pi + 2 * n]
            send_sems, recv_sems = outs[po], outs[po + 1]
            pi += 2 * n
            po += 2 + 2 * n
            for delta in range(1, N_DEV):
                dx, dy, dc = (delta >> 2) & 1, (delta >> 1) & 1, delta & 1
                px, py, pc = (1 - x if dx else x), (1 - y if dy else y), (1 - c if dc else c)
                peer = 4 * px + 2 * py + pc
                for s, d in zip(srcs, lands):
                    pltpu.make_async_remote_copy(
                        src_ref=s.at[peer] if scatter else s, dst_ref=d.at[me], send_sem=send_sems.at[delta - 1],
                        recv_sem=recv_sems.at[delta - 1], device_id=(px, py, pc), device_id_type=MESH).start()

    operands, out_shape, out_specs, aliases = [], [], [], {}
    for srcs, lands in groups:
        out_shape += [pltpu.SemaphoreType.DMA((N_DEV - 1,))] * 2
        out_specs += [SEM, SEM]
        for a in list(srcs) + list(lands):
            aliases[len(operands)] = len(out_shape)
            operands.append(_in_hbm(a))
            out_shape.append(pltpu.HBM(a.shape, a.dtype))
            out_specs.append(HBM)
    aliases[len(operands)] = len(out_shape)
    operands.append(_in_hbm(carry))
    out_shape.append(pltpu.HBM(carry.shape, carry.dtype))
    out_specs.append(HBM)
    outs = pl.pallas_call(
        body, name=name, out_shape=out_shape, in_specs=[HBM] * len(operands), out_specs=out_specs,
        input_output_aliases=aliases, compiler_params=pltpu.CompilerParams(has_side_effects=EFFECT),
    )(*operands)
    handles, po = [], 0
    for n in ns:
        handles.append((outs[po], outs[po + 1], list(outs[po + 2:po + 2 + n]), list(outs[po + 2 + n:po + 2 + 2 * n])))
        po += 2 + 2 * n
    return handles, outs[-1]


def _exchange_wait(name, pieces, after):
    ns = [len(h[2]) for h, _ in pieces]

    def body(*refs):
        x, y, c = lax.axis_index("x"), lax.axis_index("y"), lax.axis_index("c")
        pi = 0
        for n in ns:
            send_sems, recv_sems, like = refs[pi + 2 * n], refs[pi + 2 * n + 1], refs[pi + 2 * n + 2]
            pi += 2 * n + 3
            for delta in range(1, N_DEV):
                cp = pltpu.make_async_remote_copy(
                    src_ref=like, dst_ref=like, send_sem=send_sems.at[delta - 1], recv_sem=recv_sems.at[delta - 1],
                    device_id=(x, y, c), device_id_type=MESH)
                cp.wait_send()
                cp.wait_recv()

    operands, in_specs, out_shape, aliases = [], [], [], {}
    for (send_sems, recv_sems, srcs, lands), like in pieces:
        for a in srcs + lands:
            aliases[len(operands)] = len(out_shape)
            operands.append(a)
            in_specs.append(HBM)
            out_shape.append(pltpu.HBM(a.shape, a.dtype))
        operands += [send_sems, recv_sems, like]
        in_specs += [SEM, SEM, ANY]
    operands.append(after)
    in_specs.append(ANY)
    outs = pl.pallas_call(
        body, name=name, out_shape=out_shape, in_specs=in_specs, out_specs=[HBM] * len(out_shape),
        input_output_aliases=aliases, compiler_params=pltpu.CompilerParams(has_side_effects=EFFECT),
    )(*operands)
    res, po = [], 0
    for n in ns:
        res.append(list(outs[po + n:po + 2 * n]))
        po += 2 * n
    return res


def _sum8(name, r):
    _, rows, cols = r.shape
    tr = _pick(rows, (256, 128, 64, 32, 16, 8))

    def body(r_ref, o_ref):
        acc = r_ref[0].astype(F32)
        for p in range(1, N_DEV):
            acc = acc + r_ref[p].astype(F32)
        o_ref[...] = acc

    return pl.pallas_call(
        body, name=name, out_shape=jax.ShapeDtypeStruct((rows, cols), F32), grid=(rows // tr,),
        in_specs=[pl.BlockSpec((N_DEV, tr, cols), lambda i: (0, i, 0))],
        out_specs=pl.BlockSpec((tr, cols), lambda i: (i, 0)), compiler_params=_params(("parallel",)),
    )(r)


def _adamw(name, g, w, m, v):
    rows, cols = g.shape
    tr = _pick(rows, (512, 256, 128, 64, 32, 16, 8))
    c1 = 1.0 - ADAM_B1 ** ADAM_STEP
    c2 = 1.0 - ADAM_B2 ** ADAM_STEP

    def body(g_ref, w_ref, m_ref, v_ref, d_ref, nm_ref, nv_ref):
        gv = g_ref[...]
        nm = ADAM_B1 * m_ref[...] + (1.0 - ADAM_B1) * gv
        nv = ADAM_B2 * v_ref[...] + (1.0 - ADAM_B2) * (gv * gv)
        d_ref[...] = -ADAM_LR * ((nm / c1) / (jnp.sqrt(nv / c2) + ADAM_EPS) + ADAM_WD * w_ref[...])
        nm_ref[...] = nm
        nv_ref[...] = nv

    spec = pl.BlockSpec((tr, cols), lambda i: (i, 0))
    return pl.pallas_call(
        body, name=name, out_shape=[jax.ShapeDtypeStruct((rows, cols), F32)] * 3, grid=(rows // tr,),
        in_specs=[spec] * 4, out_specs=[spec] * 3, compiler_params=_params(("parallel",)),
    )(g, w, m, v)


def _adamw_summed(name, parts, w, m, v):
    nl = len(parts)
    _, rows, cols = parts[0].shape
    tr = _pick(rows, (32, 16, 8))
    steps = rows // tr
    c1 = 1.0 - ADAM_B1 ** ADAM_STEP
    c2 = 1.0 - ADAM_B2 ** ADAM_STEP

    def body(*refs):
        p_refs = refs[:nl]
        w_ref, m_ref, v_ref, g_ref, d_ref, nm_ref, nv_ref = refs[nl:]
        layer = pl.program_id(0)
        for l in range(nl):
            @pl.when(layer == l)
            def _(p_ref=p_refs[l]):
                gv = p_ref[0].astype(F32)
                for p in range(1, N_DEV):
                    gv = gv + p_ref[p].astype(F32)
                nm = ADAM_B1 * m_ref[...] + (1.0 - ADAM_B1) * gv
                nv = ADAM_B2 * v_ref[...] + (1.0 - ADAM_B2) * (gv * gv)
                g_ref[...] = gv
                d_ref[...] = -ADAM_LR * ((nm / c1) / (jnp.sqrt(nv / c2) + ADAM_EPS) + ADAM_WD * w_ref[...])
                nm_ref[...] = nm
                nv_ref[...] = nv

    spec = pl.BlockSpec((tr, cols), lambda l, i: (l * steps + i, 0))
    part_specs = [pl.BlockSpec((N_DEV, tr, cols), functools.partial(lambda l, i, own: (0, jnp.where(l == own, i, 0), 0),
                                                                    own=own)) for own in range(nl)]
    return pl.pallas_call(
        body, name=name, out_shape=[jax.ShapeDtypeStruct((nl * rows, cols), F32)] * 4, grid=(nl, steps),
        in_specs=part_specs + [spec] * 3, out_specs=[spec] * 4, compiler_params=_params(("arbitrary", "arbitrary")),
    )(*parts, w, m, v)


def _mm_nt(name, a, ws, w_offs, n, epi, out_dtypes, extras=(), rows=(), tm=None, tn=None):
    m, k = a.shape
    tm = tm or _pick(m, (512, 256, 128))
    tn = tn or _pick(n, (1408, 1152, 1024, 512, 256, 128))
    nw, ne, nr = len(ws), len(extras), len(rows)

    def body(*refs):
        a_ref, w_refs = refs[0], refs[1:1 + nw]
        e_refs, r_refs = refs[1 + nw:1 + nw + ne], refs[1 + nw + ne:1 + nw + ne + nr]
        o_refs = refs[1 + nw + ne + nr:]
        av = a_ref[...].astype(BF16)
        accs = [lax.dot_general(av, w[...], (((1,), (1,)), ((), ())), preferred_element_type=F32) for w in w_refs]
        outs = epi(accs, [e[...] for e in e_refs], [r[...] for r in r_refs])
        for o_ref, o in zip(o_refs, outs):
            o_ref[...] = o.astype(o_ref.dtype)

    in_specs = [pl.BlockSpec((tm, k), lambda j, i: (i, 0))]
    in_specs += [pl.BlockSpec((tn, k), functools.partial(lambda j, i, off: (j + off, 0), off=off)) for off in w_offs]
    in_specs += [pl.BlockSpec((tm, tn), lambda j, i: (i, j))] * ne
    in_specs += [pl.BlockSpec((1, tn), lambda j, i: (0, j))] * nr
    return pl.pallas_call(
        body, name=name, out_shape=[jax.ShapeDtypeStruct((m, n), dt) for dt in out_dtypes],
        grid=(n // tn, m // tm), in_specs=in_specs,
        out_specs=[pl.BlockSpec((tm, tn), lambda j, i: (i, j))] * len(out_dtypes),
        compiler_params=_params(("parallel", "parallel"), V7X_VMEM_LIMIT_BYTES),
    )(a, *ws, *extras, *rows)


def _rms_rows(x, gain):
    r = lax.rsqrt(jnp.mean(x * x, axis=-1, keepdims=True) + EPS)
    return (x * r * gain).astype(BF16)


def _project_residual(name, a, b, x, bias=None, gain=None):
    m, k = a.shape
    n = b.shape[1]
    tm = _pick(m, (512, 256, 128))
    rows = [r for r in (bias, gain) if r is not None]

    def body(*refs):
        a_ref, b_ref, x_ref = refs[:3]
        r_refs = list(refs[3:3 + len(rows)])
        out = x_ref[...] + jnp.dot(a_ref[...], b_ref[...], preferred_element_type=F32)
        if bias is not None:
            out = out + r_refs.pop(0)[...]
        refs[3 + len(rows)][...] = out
        if gain is not None:
            refs[4 + len(rows)][...] = _rms_rows(out, r_refs.pop(0)[...])

    big = pl.BlockSpec((tm, n), lambda i: (i, 0))
    out_shape = [jax.ShapeDtypeStruct((m, n), F32)] + ([jax.ShapeDtypeStruct((m, n), BF16)] if gain is not None else [])
    res = pl.pallas_call(
        body, name=name, out_shape=out_shape, grid=(m // tm,),
        in_specs=[pl.BlockSpec((tm, k), lambda i: (i, 0)), pl.BlockSpec((k, n), lambda i: (0, 0)), big]
        + [pl.BlockSpec((1, n), lambda i: (0, 0))] * len(rows),
        out_specs=[big] * len(out_shape), compiler_params=_params(("parallel",), V7X_VMEM_LIMIT_BYTES),
    )(a, b, x, *rows)
    return res if gain is not None else (res[0], None)


def _mm_nn_rms_bwd(name, as_, bs, x, g, dx_out, tm=None):
    m, k = as_[0].shape
    n = bs[0].shape[1]
    tm = tm or _pick(m, (512, 256, 128))
    npair = len(as_)

    def body(*refs):
        a_refs, b_refs = refs[:npair], refs[npair:2 * npair]
        x_ref, g_ref, dxo_ref, dx_ref, dg_ref = refs[2 * npair:]
        dh = None
        for a_ref, b_ref in zip(a_refs, b_refs):
            p = jnp.dot(a_ref[...].astype(BF16), b_ref[...], preferred_element_type=F32)
            dh = p if dh is None else dh + p
        xv = x_ref[...]
        r = lax.rsqrt(jnp.mean(xv * xv, axis=-1, keepdims=True) + EPS)
        yv = xv * r
        dy = dh * g_ref[...]
        dx_ref[...] = dxo_ref[...] + r * (dy - yv * jnp.mean(dy * yv, axis=-1, keepdims=True))

        @pl.when(pl.program_id(0) == 0)
        def _():
            dg_ref[...] = jnp.zeros_like(dg_ref)

        dg_ref[...] += jnp.sum(dh * yv, axis=0, keepdims=True)

    big = pl.BlockSpec((tm, n), lambda i: (i, 0))
    row = pl.BlockSpec((1, n), lambda i: (0, 0))
    in_specs = [pl.BlockSpec((tm, k), lambda i: (i, 0))] * npair + [pl.BlockSpec((k, n), lambda i: (0, 0))] * npair
    return pl.pallas_call(
        body, name=name, out_shape=[jax.ShapeDtypeStruct((m, n), F32), jax.ShapeDtypeStruct((1, n), F32)],
        grid=(m // tm,), in_specs=in_specs + [big, row, big], out_specs=[big, row],
        compiler_params=_params(("arbitrary",), V7X_VMEM_LIMIT_BYTES),
    )(*as_, *bs, x, g, dx_out)


def _mm_tn(name, a, b, colsum_b=False, tm=None, tk=2048):
    t, ma = a.shape
    nb = b.shape[1]
    tm = tm or _pick(ma, (1408, 1152, 1024, 768, 512, 256, 128))
    tk = _pick(t, (tk, 256, 128))
    nk = t // tk

    def body(*refs):
        a_ref, b_ref, o_ref = refs[0], refs[1], refs[2]
        acc_ref = refs[-1]
        kk = pl.program_id(1)
        bv = b_ref[...]

        @pl.when(kk == 0)
        def _():
            acc_ref[...] = jnp.zeros_like(acc_ref)

        acc_ref[...] += lax.dot_general(a_ref[...].astype(BF16), bv.astype(BF16), (((0,), (0,)), ((), ())),
                                        preferred_element_type=F32)
        if colsum_b:
            s_ref = refs[3]

            @pl.when((kk == 0) & (pl.program_id(0) == 0))
            def _():
                s_ref[...] = jnp.zeros_like(s_ref)

            @pl.when(pl.program_id(0) == 0)
            def _():
                s_ref[...] += jnp.sum(bv.astype(F32), axis=0, keepdims=True)

        @pl.when(kk == nk - 1)
        def _():
            o_ref[...] = acc_ref[...].astype(o_ref.dtype)

    out_shape = [jax.ShapeDtypeStruct((ma, nb), BF16)]
    out_specs = [pl.BlockSpec((tm, nb), lambda i, kk: (i, 0))]
    if colsum_b:
        out_shape.append(jax.ShapeDtypeStruct((1, nb), F32))
        out_specs.append(pl.BlockSpec((1, nb), lambda i, kk: (0, 0)))
    res = pl.pallas_call(
        body, name=name, out_shape=out_shape, grid=(ma // tm, nk),
        in_specs=[pl.BlockSpec((tk, tm), lambda i, kk: (kk, i)), pl.BlockSpec((tk, nb), lambda i, kk: (kk, 0))],
        out_specs=out_specs, scratch_shapes=[pltpu.VMEM((tm, nb), F32)],
        compiler_params=_params(("arbitrary", "arbitrary"), V7X_VMEM_LIMIT_BYTES),
    )(a, b)
    return res if colsum_b else res[0]


def _mm_tn_pair(name, a1, a2, b, tk=1024):
    t, ma = a1.shape
    nb = b.shape[1]
    tm = _pick(ma, (1408, 1152, 1024, 768, 512, 256, 128))
    tk = _pick(t, (tk, 256, 128))
    nk = t // tk
    dims = (((0,), (0,)), ((), ()))

    def body(a1_ref, a2_ref, b_ref, o1_ref, o2_ref, acc1_ref, acc2_ref):
        kk = pl.program_id(1)
        bv = b_ref[...]

        @pl.when(kk == 0)
        def _():
            acc1_ref[...] = jnp.zeros_like(acc1_ref)
            acc2_ref[...] = jnp.zeros_like(acc2_ref)

        acc1_ref[...] += lax.dot_general(a1_ref[...], bv, dims, preferred_element_type=F32)
        acc2_ref[...] += lax.dot_general(a2_ref[...], bv, dims, preferred_element_type=F32)

        @pl.when(kk == nk - 1)
        def _():
            o1_ref[...] = acc1_ref[...].astype(BF16)
            o2_ref[...] = acc2_ref[...].astype(BF16)

    sa = pl.BlockSpec((tk, tm), lambda i, kk: (kk, i))
    so = pl.BlockSpec((tm, nb), lambda i, kk: (i, 0))
    return pl.pallas_call(
        body, name=name, out_shape=[jax.ShapeDtypeStruct((ma, nb), BF16)] * 2, grid=(ma // tm, nk),
        in_specs=[sa, sa, pl.BlockSpec((tk, nb), lambda i, kk: (kk, 0))], out_specs=[so, so],
        scratch_shapes=[pltpu.VMEM((tm, nb), F32)] * 2,
        compiler_params=_params(("arbitrary", "arbitrary"), V7X_VMEM_LIMIT_BYTES),
    )(a1, a2, b)


FFN_CHUNK = 256


def _ffn_down_bwd(name, dx, wd, gq, uq):
    t, d_model = dx.shape
    f = wd.shape[0]
    tm = _pick(t, (256, 128))
    ck = _pick(f, (FFN_CHUNK, 128))
    nt = (((1,), (1,)), ((), ()))
    tn = (((0,), (0,)), ((), ()))
    last = t // tm - 1

    def body(dx_ref, w_ref, g_ref, u_ref, dg_ref, du_ref, dw_ref, acc_ref):
        i = pl.program_id(0)

        @pl.when(i == 0)
        def _():
            acc_ref[...] = jnp.zeros_like(acc_ref)

        dxb = dx_ref[...].astype(BF16)
        for c0 in range(0, f, ck):
            cs = slice(c0, c0 + ck)
            dact = lax.dot_general(dxb, w_ref[cs, :], nt, preferred_element_type=F32)
            gf, uf = g_ref[:, cs].astype(F32), u_ref[:, cs].astype(F32)
            sg = _sigmoid(gf)
            dg_ref[:, cs] = (dact * uf * (sg * (1.0 + gf * (1.0 - sg)))).astype(BF16)
            du_ref[:, cs] = (dact * gf * sg).astype(BF16)
            act = (gf * sg * uf).astype(BF16)
            acc_ref[cs, :] += lax.dot_general(act, dxb, tn, preferred_element_type=F32)

        @pl.when(i == last)
        def _():
            dw_ref[...] = acc_ref[...].astype(BF16)

    wide = pl.BlockSpec((tm, f), lambda i: (i, 0))
    whole = pl.BlockSpec((f, d_model), lambda i: (0, 0))
    return pl.pallas_call(
        body, name=name,
        out_shape=[jax.ShapeDtypeStruct((t, f), BF16), jax.ShapeDtypeStruct((t, f), BF16),
                   jax.ShapeDtypeStruct((f, d_model), BF16)],
        grid=(t // tm,), in_specs=[pl.BlockSpec((tm, d_model), lambda i: (i, 0)), whole, wide, wide],
        out_specs=[wide, wide, whole], scratch_shapes=[pltpu.VMEM((f, d_model), F32)],
        compiler_params=_params(("arbitrary",), V7X_VMEM_LIMIT_BYTES),
    )(dx, wd, gq, uq)


def _rmsnorm_fwd(name, x, g):
    t, d = x.shape
    tr = _pick(t, (512, 256, 128))

    def body(x_ref, g_ref, h_ref):
        xv = x_ref[...]
        r = lax.rsqrt(jnp.mean(xv * xv, axis=-1, keepdims=True) + EPS)
        h_ref[...] = (xv * r * g_ref[...]).astype(BF16)

    return pl.pallas_call(
        body, name=name, out_shape=jax.ShapeDtypeStruct((t, d), BF16), grid=(t // tr,),
        in_specs=[pl.BlockSpec((tr, d), lambda i: (i, 0)), pl.BlockSpec((1, d), lambda i: (0, 0))],
        out_specs=pl.BlockSpec((tr, d), lambda i: (i, 0)), compiler_params=_params(("parallel",)),
    )(x, g)


def _loss_head(name, x, g, target):
    t, d = x.shape
    tr = _pick(t, (512, 256, 128))

    def body(x_ref, g_ref, t_ref, dx_ref, dg_ref, l_ref):
        xv = x_ref[...]
        r = lax.rsqrt(jnp.mean(xv * xv, axis=-1, keepdims=True) + EPS)
        yv = xv * r
        diff = yv * g_ref[...] - t_ref[...]
        dout = diff * (1.0 / d)
        dy = dout * g_ref[...]
        dx_ref[...] = r * (dy - yv * jnp.mean(dy * yv, axis=-1, keepdims=True))

        @pl.when(pl.program_id(0) == 0)
        def _():
            dg_ref[...] = jnp.zeros_like(dg_ref)
            l_ref[...] = jnp.zeros_like(l_ref)

        dg_ref[...] += jnp.sum(dout * yv, axis=0, keepdims=True)
        l_ref[...] += (0.5 / d) * jnp.sum(diff * diff, axis=0, keepdims=True)

    big = pl.BlockSpec((tr, d), lambda i: (i, 0))
    row = pl.BlockSpec((1, d), lambda i: (0, 0))
    return pl.pallas_call(
        body, name=name,
        out_shape=[jax.ShapeDtypeStruct((t, d), F32), jax.ShapeDtypeStruct((1, d), F32),
                   jax.ShapeDtypeStruct((1, d), F32)],
        grid=(t // tr,), in_specs=[big, row, big], out_specs=[big, row, row],
        compiler_params=_params(("arbitrary",)),
    )(x, g, target)


CONV_ROWS = 64
SUBLANES = 8


def _fill_window(win_ref, sh_ref, parts):
    rows = sh_ref.shape[2]
    for cb in range(win_ref.shape[0]):
        for r0, val in parts:
            win_ref[cb, r0:r0 + val.shape[0], :] = val[:, 128 * cb:128 * (cb + 1)]
        win_ref[cb, rows:rows + SUBLANES, :] = jnp.zeros((SUBLANES, 128), F32)
        for b in range(1, SUBLANES):
            sh_ref[b - 1, cb] = win_ref[cb, b:b + rows, :]


def _window_rows(win_ref, sh_ref, o, cb):
    b = o % SUBLANES
    if b == 0:
        return win_ref[cb, o:o + CONV_ROWS, :]
    return sh_ref[b - 1, cb, o - b:o - b + CONV_ROWS, :]


def _window_scratch(rows, c):
    return [pltpu.VMEM((c // 128, rows + SUBLANES, 128), F32), pltpu.VMEM((SUBLANES - 1, c // 128, rows, 128), F32)]


def _dwconv_fwd(name, glu, w_dw, b_dw, ln_g, ln_b):
    t, c = glu.shape
    tt = _pick(t, (256, 128))
    hb = tt // CONV_HALO

    def body(cur_ref, halo_ref, w_ref, b_ref, g_ref, be_ref, dw_ref, s_ref, win_ref, sh_ref):
        i = pl.program_id(0)
        halo = jnp.where(i > 0, halo_ref[...].astype(F32), 0.0)
        _fill_window(win_ref, sh_ref, [(0, halo), (CONV_HALO, cur_ref[...].astype(F32))])
        for r0 in range(0, tt, CONV_ROWS):
            for cb in range(c // 128):
                c0 = 128 * cb
                acc = jnp.zeros((CONV_ROWS, 128), F32) + b_ref[:, c0:c0 + 128]
                for k in range(CONV_WIDTH):
                    o = r0 + k + CONV_HALO - (CONV_WIDTH - 1)
                    acc = acc + w_ref[k:k + 1, c0:c0 + 128] * _window_rows(win_ref, sh_ref, o, cb)
                dw_ref[r0:r0 + CONV_ROWS, c0:c0 + 128] = acc
        u = dw_ref[...]
        mu = jnp.mean(u, axis=-1, keepdims=True)
        uc = u - mu
        rstd = lax.rsqrt(jnp.mean(uc * uc, axis=-1, keepdims=True) + EPS)
        z = uc * rstd * g_ref[...] + be_ref[...]
        s_ref[...] = (z * _sigmoid(z)).astype(BF16)

    big = pl.BlockSpec((tt, c), lambda i: (i, 0))
    row = pl.BlockSpec((1, c), lambda i: (0, 0))
    return pl.pallas_call(
        body, name=name, out_shape=[jax.ShapeDtypeStruct((t, c), F32), jax.ShapeDtypeStruct((t, c), BF16)],
        grid=(t // tt,),
        in_specs=[big, pl.BlockSpec((CONV_HALO, c), lambda i: (jnp.maximum(i * hb - 1, 0), 0)),
                  pl.BlockSpec((CONV_HALO, c), lambda i: (0, 0)), row, row, row],
        out_specs=[big, big], scratch_shapes=_window_scratch(tt + CONV_HALO, c),
        compiler_params=_params(("parallel",), V7X_VMEM_LIMIT_BYTES),
    )(glu, glu, w_dw, b_dw, ln_g, ln_b)


def _ln_silu_bwd(name, dx, w2, dw, ln_g, ln_b):
    t, c = dw.shape
    d_model = dx.shape[1]
    tr = _pick(t, (256, 128))

    def body(dx_ref, w_ref, dw_ref, g_ref, be_ref, o_ref, acc_ref):
        ds = lax.dot_general(dx_ref[...].astype(BF16), w_ref[...], (((1,), (1,)), ((), ())),
                             preferred_element_type=F32)
        u = dw_ref[...]
        mu = jnp.mean(u, axis=-1, keepdims=True)
        uc = u - mu
        rstd = lax.rsqrt(jnp.mean(uc * uc, axis=-1, keepdims=True) + EPS)
        xh = uc * rstd
        z = xh * g_ref[...] + be_ref[...]
        sg = _sigmoid(z)
        dz = ds * (sg * (1.0 + z * (1.0 - sg)))
        dxh = dz * g_ref[...]
        du = rstd * (dxh - jnp.mean(dxh, axis=-1, keepdims=True) - xh * jnp.mean(dxh * xh, axis=-1, keepdims=True))
        o_ref[...] = du

        @pl.when(pl.program_id(0) == 0)
        def _():
            acc_ref[...] = jnp.zeros_like(acc_ref)

        acc_ref[0:1, :] += jnp.sum(dz * xh, axis=0, keepdims=True)
        acc_ref[1:2, :] += jnp.sum(dz, axis=0, keepdims=True)
        acc_ref[2:3, :] += jnp.sum(du, axis=0, keepdims=True)

    big = pl.BlockSpec((tr, c), lambda i: (i, 0))
    row = pl.BlockSpec((1, c), lambda i: (0, 0))
    return pl.pallas_call(
        body, name=name, out_shape=[jax.ShapeDtypeStruct((t, c), F32), jax.ShapeDtypeStruct((8, c), F32)],
        grid=(t // tr,),
        in_specs=[pl.BlockSpec((tr, d_model), lambda i: (i, 0)), pl.BlockSpec((c, d_model), lambda i: (0, 0)),
                  big, row, row],
        out_specs=[big, pl.BlockSpec((8, c), lambda i: (0, 0))],
        compiler_params=_params(("arbitrary",), V7X_VMEM_LIMIT_BYTES),
    )(dx, w2, dw, ln_g, ln_b)


def _dwconv_bwd(name, ddw, a, gt, w_dw):
    t, c = ddw.shape
    tt = _pick(t, (256, 128))
    hb = tt // CONV_HALO
    last = t // tt - 1
    back = CONV_WIDTH - 1

    def body(d_ref, dn_ref, a_ref, ap_ref, g_ref, gp_ref, w_ref, du_ref, dwk_ref, db_ref,
             wd_ref, shd_ref, wg_ref, shg_ref, dg_ref, dwk8_ref):
        i = pl.program_id(0)
        _fill_window(wd_ref, shd_ref, [(0, d_ref[...]), (tt, jnp.where(i < last, dn_ref[...], 0.0))])
        glu_prev = ap_ref[...].astype(F32) * _sigmoid(gp_ref[...].astype(F32))
        av = a_ref[...].astype(F32)
        sg = _sigmoid(g_ref[...].astype(F32))
        _fill_window(wg_ref, shg_ref, [(0, jnp.where(i > 0, glu_prev, 0.0)), (CONV_HALO, av * sg)])

        @pl.when(i == 0)
        def _():
            dwk8_ref[...] = jnp.zeros_like(dwk8_ref)
            db_ref[...] = jnp.zeros_like(db_ref)

        for r0 in range(0, tt, CONV_ROWS):
            for cb in range(c // 128):
                c0 = 128 * cb
                dcur = wd_ref[cb, r0:r0 + CONV_ROWS, :]
                acc = jnp.zeros((CONV_ROWS, 128), F32)
                for k in range(CONV_WIDTH):
                    acc = acc + w_ref[k:k + 1, c0:c0 + 128] * _window_rows(wd_ref, shd_ref, r0 + back - k, cb)
                    p = dcur * _window_rows(wg_ref, shg_ref, r0 + k + CONV_HALO - back, cb)
                    s8 = p[0:SUBLANES]
                    for q in range(SUBLANES, CONV_ROWS, SUBLANES):
                        s8 = s8 + p[q:q + SUBLANES]
                    dwk8_ref[SUBLANES * k:SUBLANES * (k + 1), c0:c0 + 128] += s8
                dg_ref[r0:r0 + CONV_ROWS, c0:c0 + 128] = acc

        @pl.when(i == last)
        def _():
            for k in range(CONV_WIDTH):
                dwk_ref[k:k + 1, :] = jnp.sum(dwk8_ref[SUBLANES * k:SUBLANES * (k + 1), :], axis=0, keepdims=True)
            dwk_ref[CONV_WIDTH:, :] = jnp.zeros((CONV_HALO - CONV_WIDTH, c), F32)
        dglu = dg_ref[...]
        da = dglu * sg
        dgate = dglu * av * sg * (1.0 - sg)
        du_ref[:, 0:c] = da.astype(BF16)
        du_ref[:, c:] = dgate.astype(BF16)
        db_ref[:, 0:c] += jnp.sum(da, axis=0, keepdims=True)
        db_ref[:, c:] += jnp.sum(dgate, axis=0, keepdims=True)

    big = pl.BlockSpec((tt, c), lambda i: (i, 0))
    prev = pl.BlockSpec((CONV_HALO, c), lambda i: (jnp.maximum(i * hb - 1, 0), 0))
    nxt = pl.BlockSpec((CONV_HALO, c), lambda i: (jnp.minimum((i + 1) * hb, t // CONV_HALO - 1), 0))
    return pl.pallas_call(
        body, name=name,
        out_shape=[jax.ShapeDtypeStruct((t, 2 * c), BF16), jax.ShapeDtypeStruct((CONV_HALO, c), F32),
                   jax.ShapeDtypeStruct((1, 2 * c), F32)],
        grid=(t // tt,),
        in_specs=[big, nxt, big, prev, big, prev, pl.BlockSpec((CONV_HALO, c), lambda i: (0, 0))],
        out_specs=[pl.BlockSpec((tt, 2 * c), lambda i: (i, 0)), pl.BlockSpec((CONV_HALO, c), lambda i: (0, 0)),
                   pl.BlockSpec((1, 2 * c), lambda i: (0, 0))],
        scratch_shapes=_window_scratch(tt + CONV_HALO, c) + _window_scratch(tt + CONV_HALO, c)
        + [pltpu.VMEM((tt, c), F32), pltpu.VMEM((SUBLANES * CONV_HALO, c), F32)],
        compiler_params=_params(("arbitrary",), V7X_VMEM_LIMIT_BYTES),
    )(ddw, ddw, a, a, gt, gt, w_dw)


def _bucket_tables():
    i = np.arange(N_BACK)[:, None]
    j = np.arange(2 * N_BACK)[None, :]
    dist = i + N_BACK - j
    valid = (dist >= 0) & (dist <= N_BACK)
    max_exact = N_BUCKETS // 2
    out = []
    for d in DILATIONS:
        n = np.maximum(dist * d, 0)
        nf = np.maximum(n, 1).astype(np.float32)
        large = max_exact + (np.log(nf / np.float32(max_exact)) / np.float32(math.log(REL_MAX_DISTANCE / max_exact))
                             * np.float32(N_BUCKETS - max_exact)).astype(np.int32)
        large = np.minimum(large, N_BUCKETS - 1)
        out.append(np.where(valid, np.where(n < max_exact, n, large), -1))
    return np.stack(out).astype(np.int32)


def _bias_build(name, rel_bias, buckets):
    def body(tbl_ref, bk_ref, o_ref):
        g = pl.program_id(0)
        bk = bk_ref[0]
        for h in range(HEADS_PER_GROUP):
            acc = jnp.zeros(bk.shape, F32)
            for b in range(N_BUCKETS):
                acc = jnp.where(bk == b, tbl_ref[b, g * HEADS_PER_GROUP + h], acc)
            o_ref[h] = jnp.where(bk < 0, NEG_INF, acc)

    return pl.pallas_call(
        body, name=name, out_shape=jax.ShapeDtypeStruct((N_HEADS, N_BACK, 2 * N_BACK), F32), grid=(N_GROUPS,),
        in_specs=[pl.BlockSpec(memory_space=pltpu.SMEM), pl.BlockSpec((1, N_BACK, 2 * N_BACK), lambda g: (g, 0, 0))],
        out_specs=pl.BlockSpec((HEADS_PER_GROUP, N_BACK, 2 * N_BACK), lambda g: (g, 0, 0)),
        compiler_params=_params(("arbitrary",)),
    )(rel_bias, buckets)


def _bias_grad(name, dbs, buckets):
    nd = len(dbs)

    def body(*refs):
        bk = refs[nd][0]
        o_ref = refs[nd + 1]
        lane = lax.broadcasted_iota(jnp.int32, (1, 128), 1)
        db = [sum(r[h] for r in refs[:nd]) for h in range(HEADS_PER_GROUP)]
        for b in range(N_BUCKETS):
            row = jnp.zeros((1, 128), F32)
            for h in range(HEADS_PER_GROUP):
                s = jnp.sum(jnp.where(bk == b, db[h], 0.0), axis=0, keepdims=True)
                s = jnp.sum(s, axis=1, keepdims=True)
                row = jnp.where(lane // 32 == h, s, row)
            o_ref[0, b:b + 1, :] = row

    spec = pl.BlockSpec((HEADS_PER_GROUP, N_BACK, 2 * N_BACK), lambda g: (g, 0, 0))
    return pl.pallas_call(
        body, name=name, out_shape=jax.ShapeDtypeStruct((N_GROUPS, N_BUCKETS, 128), F32), grid=(N_GROUPS,),
        in_specs=[spec] * nd + [pl.BlockSpec((1, N_BACK, 2 * N_BACK), lambda g: (g, 0, 0))],
        out_specs=pl.BlockSpec((1, N_BUCKETS, 128), lambda g: (g, 0, 0)), compiler_params=_params(("arbitrary",)),
    )(*dbs, buckets)


def _head_cols(h):
    return slice(h * HEAD_DIM, (h + 1) * HEAD_DIM)


def _store_dilated(o_ref, tile_ref, val, d):
    nlb, rows, _ = tile_ref.shape
    width = 128 * nlb
    for lb in range(nlb):
        tile_ref[lb] = val[:, 128 * lb:128 * (lb + 1)]
    for r in range(d):
        for lb in range(nlb):
            c0 = r * width + 128 * lb
            o_ref[:, c0:c0 + 128] = tile_ref[lb, pl.ds(r, rows // d, stride=d), :].astype(o_ref.dtype)


def _attn_qkv_fwd(name, h, wq):
    t, k = h.shape
    tm = _pick(t, (512, 256))
    width = wq.shape[1]

    def body(h_ref, w0, w1, w2, o0, o1, o2, tile_ref):
        hv = h_ref[...]
        for g, (w_ref, o_ref) in enumerate(zip((w0, w1, w2), (o0, o1, o2))):
            acc = lax.dot_general(hv, w_ref[...], (((1,), (1,)), ((), ())), preferred_element_type=F32)
            if DILATIONS[g] == 1:
                o_ref[...] = acc.astype(BF16)
            else:
                _store_dilated(o_ref, tile_ref, acc, DILATIONS[g])

    return pl.pallas_call(
        body, name=name,
        out_shape=[jax.ShapeDtypeStruct((t // d, d * width), BF16) for d in DILATIONS], grid=(t // tm,),
        in_specs=[pl.BlockSpec((tm, k), lambda i: (i, 0))] + [pl.BlockSpec((width, k), lambda i: (0, 0))] * 3,
        out_specs=[pl.BlockSpec((tm // d, d * width), lambda i: (i, 0)) for d in DILATIONS],
        scratch_shapes=[pltpu.VMEM((width // 128, tm, 128), F32)],
        compiler_params=_params(("parallel",), V7X_VMEM_LIMIT_BYTES),
    )(h, wq[0], wq[1], wq[2])


def _attn_fwd(name, qkv, bias, g):
    d = DILATIONS[g]
    tq = qkv.shape[0]
    t = tq * d
    nblk = 3
    nsub = _pick(tq // N_BACK, (4, 2))
    scale = HEAD_DIM ** -0.5

    def body(q_ref, kp_ref, kc_ref, vp_ref, vc_ref, b_ref, o_ref, l_ref):
        m2 = pl.program_id(1)
        col = lax.broadcasted_iota(jnp.int32, (N_BACK, 2 * N_BACK), 1)
        lane = lax.broadcasted_iota(jnp.int32, (N_BACK, 128), 1)
        for sub in range(nsub):
            rows = slice(N_BACK * sub, N_BACK * (sub + 1))
            both = slice(N_BACK * (sub - 1), N_BACK * (sub + 1))
            lse_tile = jnp.zeros((N_BACK, 128), F32)
            outs = []
            for h in range(HEADS_PER_GROUP):
                hc = _head_cols(h)
                if sub == 0:
                    kk = jnp.concatenate([kp_ref[:, hc], kc_ref[0:N_BACK, hc]], axis=0)
                    vv = jnp.concatenate([vp_ref[:, hc], vc_ref[0:N_BACK, hc]], axis=0)
                else:
                    kk, vv = kc_ref[both, hc], vc_ref[both, hc]
                s = lax.dot_general(q_ref[rows, hc], kk, (((1,), (1,)), ((), ())), preferred_element_type=F32)
                s = s * scale + b_ref[h]
                if sub == 0:
                    s = jnp.where((col >= N_BACK) | (m2 > 0), s, NEG_INF)
                m = jnp.max(s, axis=-1, keepdims=True)
                p = jnp.exp(s - m)
                den = jnp.sum(p, axis=-1, keepdims=True)
                outs.append(jnp.dot(p.astype(BF16), vv, preferred_element_type=F32) / den)
                lse_tile = jnp.where(lane // 32 == h, m + jnp.log(den), lse_tile)
            o_ref[rows, :] = jnp.concatenate(outs, axis=1)
            l_ref[rows, :] = lse_tile

    def blk(part, prev):
        if prev:
            return pl.BlockSpec((N_BACK, GROUP_COLS), lambda r, n: (jnp.maximum(nsub * n - 1, 0), r * nblk + part))
        return pl.BlockSpec((nsub * N_BACK, GROUP_COLS), lambda r, n: (n, r * nblk + part))

    o, l = pl.pallas_call(
        body, name=name,
        out_shape=[jax.ShapeDtypeStruct((tq, d * GROUP_COLS), F32), jax.ShapeDtypeStruct((tq, d * 128), F32)],
        grid=(d, tq // (nsub * N_BACK)),
        in_specs=[blk(0, False), blk(1, True), blk(1, False), blk(2, True), blk(2, False),
                  pl.BlockSpec((HEADS_PER_GROUP, N_BACK, 2 * N_BACK), lambda r, n: (g, 0, 0))],
        out_specs=[pl.BlockSpec((nsub * N_BACK, GROUP_COLS), lambda r, n: (n, r)),
                   pl.BlockSpec((nsub * N_BACK, 128), lambda r, n: (n, r))],
        compiler_params=_params(("parallel", "parallel")),
    )(qkv, qkv, qkv, qkv, qkv, bias)
    return o, l


def _group_weights(l_refs, h):
    ls = [l_ref[:, 32 * h:32 * h + 1] for l_ref in l_refs]
    m = jnp.maximum(jnp.maximum(ls[0], ls[1]), ls[2])
    es = [jnp.exp(l - m) for l in ls]
    tot = es[0] + es[1] + es[2]
    return [e / tot for e in es]


def _load_dilated(v_ref, n_ref, tile_ref, d):
    nlb, rows, _ = tile_ref.shape
    width = 128 * nlb
    for r in range(d):
        for lb in range(nlb):
            c0 = r * width + 128 * lb
            tile_ref[lb, pl.ds(r, rows // d, stride=d), :] = v_ref[:, c0:c0 + 128]
    n_ref[...] = jnp.concatenate([tile_ref[lb] for lb in range(nlb)], axis=1)


def _attn_merge_out(name, os_, ls, wot, x, gain):
    t, d_model = x.shape
    tr = _pick(t, (512, 256))

    def body(o0, o1v, o2v, l0, l1v, l2v, w_ref, x_ref, g_ref, om_ref, out_ref, h_ref, o1, o2, l1, l2,
             otile_ref, ltile_ref):
        _load_dilated(o1v, o1, otile_ref, DILATIONS[1])
        _load_dilated(o2v, o2, otile_ref, DILATIONS[2])
        _load_dilated(l1v, l1, ltile_ref, DILATIONS[1])
        _load_dilated(l2v, l2, ltile_ref, DILATIONS[2])
        o_refs = (o0, o1, o2)
        pieces = [[None] * HEADS_PER_GROUP for _ in range(N_GROUPS)]
        for h in range(HEADS_PER_GROUP):
            al = _group_weights((l0, l1, l2), h)
            for g in range(N_GROUPS):
                pieces[g][h] = o_refs[g][:, _head_cols(h)] * al[g]
        om = jnp.concatenate([p for row in pieces for p in row], axis=1).astype(BF16)
        om_ref[...] = om
        out = x_ref[...] + lax.dot_general(om, w_ref[...], (((1,), (1,)), ((), ())), preferred_element_type=F32)
        out_ref[...] = out
        h_ref[...] = _rms_rows(out, g_ref[...])

    so = pl.BlockSpec((tr, GROUP_COLS), lambda i: (i, 0))
    sl = pl.BlockSpec((tr, 128), lambda i: (i, 0))
    sx = pl.BlockSpec((tr, d_model), lambda i: (i, 0))
    views = [pl.BlockSpec((tr // d, d * GROUP_COLS), lambda i: (i, 0)) for d in DILATIONS]
    views += [pl.BlockSpec((tr // d, d * 128), lambda i: (i, 0)) for d in DILATIONS]
    res = pl.pallas_call(
        body, name=name,
        out_shape=[jax.ShapeDtypeStruct((t, D_ATTN), BF16), jax.ShapeDtypeStruct((t, d_model), F32),
                   jax.ShapeDtypeStruct((t, d_model), BF16)]
        + [jax.ShapeDtypeStruct((t, GROUP_COLS), F32)] * 2 + [jax.ShapeDtypeStruct((t, 128), F32)] * 2,
        grid=(t // tr,),
        in_specs=views + [pl.BlockSpec((d_model, D_ATTN), lambda i: (0, 0)), sx,
                          pl.BlockSpec((1, d_model), lambda i: (0, 0))],
        out_specs=[pl.BlockSpec((tr, D_ATTN), lambda i: (i, 0)), sx, sx, so, so, sl, sl],
        scratch_shapes=[pltpu.VMEM((GROUP_COLS // 128, tr, 128), F32), pltpu.VMEM((1, tr, 128), F32)],
        compiler_params=_params(("parallel",), V7X_VMEM_LIMIT_BYTES),
    )(*os_, *ls, wot, x, gain)
    om, x_new, h_next, o1, o2, l1, l2 = res
    return om, x_new, h_next, [os_[0], o1, o2], [ls[0], l1, l2]


def _attn_bwd_prep(name, dx, wot, os_, ls):
    t, d_model = dx.shape
    tr = _pick(t, (512, 256))

    def body(dx_ref, w_ref, o0, o1, o2, l0, l1, l2, d0, d1, d2, c0, c1, c2, dtile_ref, ctile_ref):
        o_refs, d_refs, c_refs = (o0, o1, o2), (d0, d1, d2), (c0, c1, c2)
        d_out = jnp.dot(dx_ref[...].astype(BF16), w_ref[...], preferred_element_type=F32)
        lane = lax.broadcasted_iota(jnp.int32, (tr, 128), 1)
        dos = [[None] * HEADS_PER_GROUP for _ in range(N_GROUPS)]
        cs = [jnp.zeros((tr, 128), F32) for _ in range(N_GROUPS)]
        for h in range(HEADS_PER_GROUP):
            al = _group_weights((l0, l1, l2), h)
            tot = jnp.zeros((tr, 1), F32)
            for g in range(N_GROUPS):
                dv = d_out[:, g * GROUP_COLS + h * HEAD_DIM:g * GROUP_COLS + (h + 1) * HEAD_DIM]
                tot = tot + al[g] * jnp.sum(dv * o_refs[g][:, _head_cols(h)], axis=-1, keepdims=True)
                dos[g][h] = dv * al[g]
            for g in range(N_GROUPS):
                cs[g] = jnp.where(lane // 32 == h, -al[g] * tot, cs[g])
        for g in range(N_GROUPS):
            do_g = jnp.concatenate(dos[g], axis=1)
            if DILATIONS[g] == 1:
                d_refs[g][...] = do_g.astype(BF16)
                c_refs[g][...] = cs[g]
            else:
                _store_dilated(d_refs[g], dtile_ref, do_g, DILATIONS[g])
                _store_dilated(c_refs[g], ctile_ref, cs[g], DILATIONS[g])

    so = pl.BlockSpec((tr, GROUP_COLS), lambda i: (i, 0))
    sl = pl.BlockSpec((tr, 128), lambda i: (i, 0))
    res = pl.pallas_call(
        body, name=name,
        out_shape=[jax.ShapeDtypeStruct((t // d, d * GROUP_COLS), BF16) for d in DILATIONS]
        + [jax.ShapeDtypeStruct((t // d, d * 128), F32) for d in DILATIONS],
        grid=(t // tr,),
        in_specs=[pl.BlockSpec((tr, d_model), lambda i: (i, 0)), pl.BlockSpec((d_model, D_ATTN), lambda i: (0, 0))]
        + [so] * 3 + [sl] * 3,
        out_specs=[pl.BlockSpec((tr // d, d * GROUP_COLS), lambda i: (i, 0)) for d in DILATIONS]
        + [pl.BlockSpec((tr // d, d * 128), lambda i: (i, 0)) for d in DILATIONS],
        scratch_shapes=[pltpu.VMEM((GROUP_COLS // 128, tr, 128), F32), pltpu.VMEM((1, tr, 128), F32)],
        compiler_params=_params(("parallel",), V7X_VMEM_LIMIT_BYTES),
    )(dx, wot, *os_, *ls)
    return res[:3], res[3:]


def _attn_bwd(name, qkv, do, lse, cterm, bias, g):
    d = DILATIONS[g]
    tq = qkv.shape[0]
    t = tq * d
    nb = tq // N_BACK
    nblk = 3
    scale = HEAD_DIM ** -0.5
    nt = (((1,), (1,)), ((), ()))
    tn = (((0,), (0,)), ((), ()))

    def body(q2, qx, kp, k2, vp, v2, do2, dox, l2, lx, c2, cx, b_ref, dqkv_ref, db_ref):
        m2 = pl.program_id(1)

        @pl.when((m2 == 0) & (pl.program_id(0) == 0))
        def _():
            db_ref[...] = jnp.zeros_like(db_ref)

        row2 = lax.broadcasted_iota(jnp.int32, (2 * N_BACK, N_BACK), 0)
        lo, hi = slice(0, N_BACK), slice(N_BACK, 2 * N_BACK)
        for sub in range(2):
            rows = hi if sub else lo
            has_prev = True if sub else m2 > 0
            has_next = 2 * m2 + 2 < nb if sub else True
            on_ac = (row2 < N_BACK) | has_next
            dqs, dks, dvs = [], [], []
            for h in range(HEADS_PER_GROUP):
                hc = _head_cols(h)
                st = slice(32 * h, 32 * h + 1)
                b_prev, b_same = b_ref[h, :, 0:N_BACK], b_ref[h, :, N_BACK:]
                q0, k1, v1, d0 = q2[rows, hc], k2[rows, hc], v2[rows, hc], do2[rows, hc]
                l0, c0 = l2[rows, st], c2[rows, st]
                if sub:
                    k0, v0 = k2[lo, hc], v2[lo, hc]
                    q_ac = jnp.concatenate([q0, qx[:, hc]], axis=0)
                    d_ac = jnp.concatenate([d0, dox[:, hc]], axis=0)
                    l_ac = jnp.concatenate([l0, lx[:, st]], axis=0)
                    c_ac = jnp.concatenate([c0, cx[:, st]], axis=0)
                else:
                    k0, v0 = kp[:, hc], vp[:, hc]
                    q_ac, d_ac, l_ac, c_ac = q2[:, hc], do2[:, hc], l2[:, st], c2[:, st]
                s_ac = (lax.dot_general(q_ac, k1, nt, preferred_element_type=F32) * scale
                        + jnp.concatenate([b_same, b_prev], axis=0))
                p_ac = jnp.where(on_ac, jnp.exp(s_ac - l_ac), 0.0)
                ds_ac = p_ac * (lax.dot_general(d_ac, v1, nt, preferred_element_type=F32) + c_ac)
                s_b = lax.dot_general(q0, k0, nt, preferred_element_type=F32) * scale + b_prev
                p_b = jnp.where(has_prev, jnp.exp(s_b - l0), 0.0)
                ds_b = p_b * (lax.dot_general(d0, v0, nt, preferred_element_type=F32) + c0)
                ds_a = ds_ac[0:N_BACK]
                dqs.append(scale * jnp.dot(jnp.concatenate([ds_b, ds_a], axis=1).astype(BF16),
                                           jnp.concatenate([k0, k1], axis=0), preferred_element_type=F32))
                dks.append(scale * lax.dot_general(ds_ac.astype(BF16), q_ac, tn, preferred_element_type=F32))
                dvs.append(lax.dot_general(p_ac.astype(BF16), d_ac, tn, preferred_element_type=F32))
                db_ref[h, :, 0:N_BACK] += ds_b
                db_ref[h, :, N_BACK:] += ds_a
            dqkv_ref[rows, :] = jnp.concatenate(dqs + dks + dvs, axis=1).astype(BF16)

    def blk(width, which, col):
        if which == "prev":
            return pl.BlockSpec((N_BACK, width), lambda r, n: (jnp.maximum(2 * n - 1, 0), col(r)))
        if which == "next":
            return pl.BlockSpec((N_BACK, width), lambda r, n: (jnp.minimum(2 * n + 2, nb - 1), col(r)))
        return pl.BlockSpec((2 * N_BACK, width), lambda r, n: (n, col(r)))

    def qkv_blk(part, which):
        return blk(GROUP_COLS, which, lambda r: r * nblk + part)

    def grp_blk(width, which):
        return blk(width, which, lambda r: r)

    qv, dov, lv, cv = qkv, do, lse, cterm
    dqkv_g, db = pl.pallas_call(
        body, name=name,
        out_shape=[jax.ShapeDtypeStruct((tq, d * 3 * GROUP_COLS), BF16),
                   jax.ShapeDtypeStruct((HEADS_PER_GROUP, N_BACK, 2 * N_BACK), F32)],
        grid=(d, nb // 2),
        in_specs=[qkv_blk(0, "same"), qkv_blk(0, "next"), qkv_blk(1, "prev"), qkv_blk(1, "same"),
                  qkv_blk(2, "prev"), qkv_blk(2, "same"), grp_blk(GROUP_COLS, "same"), grp_blk(GROUP_COLS, "next"),
                  grp_blk(128, "same"), grp_blk(128, "next"), grp_blk(128, "same"), grp_blk(128, "next"),
                  pl.BlockSpec((HEADS_PER_GROUP, N_BACK, 2 * N_BACK), lambda r, n: (g, 0, 0))],
        out_specs=[pl.BlockSpec((2 * N_BACK, 3 * GROUP_COLS), lambda r, n: (n, r)),
                   pl.BlockSpec((HEADS_PER_GROUP, N_BACK, 2 * N_BACK), lambda r, n: (0, 0, 0))],
        compiler_params=_params(("arbitrary", "arbitrary")),
    )(qv, qv, qv, qv, qv, qv, dov, dov, lv, lv, cv, cv, bias)
    return dqkv_g.reshape(t, 3 * GROUP_COLS), db


def _row(v):
    return v.reshape(1, -1)


def _glu_epi(accs, extras, rows):
    a = (accs[0] + rows[0]).astype(BF16)
    gt = (accs[1] + rows[1]).astype(BF16)
    return a, gt, a.astype(F32) * _sigmoid(gt.astype(F32))


def _swiglu_epi(accs, extras, rows):
    gq, uq = accs[0].astype(BF16), accs[1].astype(BF16)
    gf = gq.astype(F32)
    return gq, uq, gf * _sigmoid(gf) * uq.astype(F32)


def _group_rows(w):
    parts = [w[p * D_ATTN:(p + 1) * D_ATTN].reshape(N_GROUPS, GROUP_COLS, -1) for p in range(3)]
    return jnp.concatenate(parts, axis=1)


def _ungroup_rows(wg):
    return jnp.concatenate([wg[g][p * GROUP_COLS:(p + 1) * GROUP_COLS] for p in range(3) for g in range(N_GROUPS)],
                           axis=0)


def _local_step(x, target, sm, depth, fetch, emit):
    d_model = x.shape[1]
    buckets = jnp.asarray(_bucket_tables())
    bias = _bias_build("bias_build", sm["rel_bias"], buckets)
    saved = []
    h = _rmsnorm_fwd("rms_mix_fwd0", x, _row(sm["norm_mix"][0]))
    for i in range(depth):
        j = i // 2
        rec = {"x_mix": x}
        wm = fetch(2 * i, x)
        rec.update(h_mix=h, wm=wm)
        ffn_gain = _row(sm["norm_ffn"][i])
        if i % 2 == 0:
            c = wm["w1t"].shape[0] // 2
            tn = _pick(c, (1024, 512, 256, 128))
            b1 = sm["conv_b_pw1"][j]
            a, gt, glu = _mm_nt(f"conv_pw1_fwd{j}", h, [wm["w1t"]] * 2, [0, c // tn], c, _glu_epi, (BF16,) * 3,
                                rows=[_row(b1[:c]), _row(b1[c:])], tn=tn)
            wm.update(fetch(2 * i, a, 1))
            dw, s = _dwconv_fwd(f"dwconv_fwd{j}", glu, wm["wdw"], _row(sm["conv_b_dw"][j]),
                                _row(sm["conv_ln_g"][j]), _row(sm["conv_ln_b"][j]))
            x, h2 = _project_residual(f"conv_pw2_fwd{j}", s, wm["w2"], x, bias=_row(sm["conv_b_pw2"][j]),
                                      gain=ffn_gain)
            rec.update(a=a, gt=gt, dw=dw, s=s)
        else:
            wq = _group_rows(wm["wqkvt"])
            qkv = _attn_qkv_fwd(f"attn_qkv_fwd{j}", h, wq)
            og = [_attn_fwd(f"attn_fwd{j}_{g}", qkv[g], bias, g) for g in range(N_GROUPS)]
            om, x, h2, os_, ls = _attn_merge_out(f"attn_out_fwd{j}", [o for o, _ in og], [l for _, l in og],
                                                 wm["wot"], x, ffn_gain)
            rec.update(qkv=qkv, os=os_, ls=ls, lse_views=[l for _, l in og], om=om, wq=wq)
        rec["x_ffn"] = x
        wf = fetch(2 * i + 1, x)
        f = wf["wd"].shape[0]
        gq, uq, act = _mm_nt(f"ffn_up_fwd{i}", h2, [wf["wgt"], wf["wut"]], [0, 0], f, _swiglu_epi, (BF16,) * 3)
        x, h = _project_residual(f"ffn_down_fwd{i}", act, wf["wd"], x,
                                 gain=_row(sm["norm_mix"][i + 1]) if i + 1 < depth else None)
        rec.update(h_ffn=h2, gq=gq, uq=uq, wf=wf)
        saved.append(rec)

    dx, g_final, loss_cols = _loss_head("loss_head", x, _row(sm["final_norm"]), target)

    g_mix, g_ffn = [None] * depth, [None] * depth
    nconv = (depth + 1) // 2
    g_b1, g_bdw, g_lng, g_lnb, g_b2 = ([None] * nconv for _ in range(5))
    dbias = []
    for i in reversed(range(depth)):
        j = i // 2
        rec = saved[i]
        wm, wf = rec["wm"], rec["wf"]
        dgate, dup, dwd = _ffn_down_bwd(f"ffn_down_bwd{i}", dx, wf["wd"], rec["gq"], rec["uq"])
        gf = {"wd": dwd}
        gf["wgt"], gf["wut"] = _mm_tn_pair(f"ffn_gate_up_dw{i}", dgate, dup, rec["h_ffn"])
        dx, g_ffn[i] = _mm_nn_rms_bwd(f"ffn_up_bwd{i}", [dgate, dup], [wf["wgt"], wf["wut"]], rec["x_ffn"],
                                      _row(sm["norm_ffn"][i]), dx)
        dx = emit(2 * i + 1, gf, dx)
        if i % 2 == 0:
            c = wm["w2"].shape[0]
            gm = {}
            gm["w2"], g_b2[j] = _mm_tn(f"conv_pw2_dw{j}", rec["s"], dx, colsum_b=True)
            ddw, sums = _ln_silu_bwd(f"conv_pw2_bwd{j}", dx, wm["w2"], rec["dw"], _row(sm["conv_ln_g"][j]),
                                     _row(sm["conv_ln_b"][j]))
            g_lng[j], g_lnb[j], g_bdw[j] = sums[0], sums[1], sums[2]
            du, dwk, db1 = _dwconv_bwd(f"dwconv_bwd{j}", ddw, rec["a"], rec["gt"], wm["wdw"])
            gm["wdw"] = dwk[:CONV_WIDTH]
            g_b1[j] = db1[0]
            gm["w1t"] = _mm_tn(f"conv_pw1_dw{j}", du, rec["h_mix"])
            dh_terms = ([du], [wm["w1t"]])
        else:
            gm = {"wot": _mm_tn(f"attn_out_dw{j}", dx, rec["om"])}
            dos, cs = _attn_bwd_prep(f"attn_out_bwd{j}", dx, wm["wot"], rec["os"], rec["ls"])
            back = [_attn_bwd(f"attn_bwd{j}_{g}", rec["qkv"][g], dos[g], rec["lse_views"][g], cs[g], bias, g)
                    for g in range(N_GROUPS)]
            dqkv = [b[0] for b in back]
            dbias.append(jnp.concatenate([b[1] for b in back], axis=0))
            dh_terms = (dqkv, [rec["wq"][g] for g in range(N_GROUPS)])
            gm["wqkvt"] = _ungroup_rows([_mm_tn(f"attn_qkv_dw{j}_{g}", dqkv[g], rec["h_mix"])
                                         for g in range(N_GROUPS)])
        dx, g_mix[i] = _mm_nn_rms_bwd(f"mix_in_bwd{i}", dh_terms[0], dh_terms[1], rec["x_mix"],
                                      _row(sm["norm_mix"][i]), dx)
        dx = emit(2 * i, gm, dx)

    gb = _bias_grad("bias_grad", dbias, buckets)
    g_rel = jnp.transpose(gb[:, :, ::32], (1, 0, 2)).reshape(N_BUCKETS, N_HEADS)
    gsm = {
        "norm_mix": jnp.concatenate(g_mix, axis=0), "norm_ffn": jnp.concatenate(g_ffn, axis=0),
        "final_norm": g_final[0], "conv_b_pw1": jnp.stack(g_b1), "conv_b_dw": jnp.stack(g_bdw),
        "conv_ln_g": jnp.stack(g_lng), "conv_ln_b": jnp.stack(g_lnb),
        "conv_b_pw2": jnp.concatenate(g_b2, axis=0), "rel_bias": g_rel,
    }
    return loss_cols, dx, gsm


SMALL = ("norm_mix", "norm_ffn", "final_norm", "conv_b_pw1", "conv_b_dw", "conv_ln_g", "conv_ln_b", "conv_b_pw2",
         "rel_bias")
SHARDED = (("conv_w_pw1", "w1t", True), ("conv_w_pw2", "w2", False), ("attn_w_qkv", "wqkvt", True),
           ("attn_w_o", "wot", True), ("ffn_w_gate", "wgt", True), ("ffn_w_up", "wut", True),
           ("ffn_w_down", "wd", False))
ORDER = ("norm_mix", "norm_ffn", "final_norm", "conv_w_pw1", "conv_b_pw1", "conv_w_dw", "conv_b_dw", "conv_ln_g",
         "conv_ln_b", "conv_w_pw2", "conv_b_pw2", "attn_w_qkv", "attn_w_o", "rel_bias", "ffn_w_gate", "ffn_w_up",
         "ffn_w_down")
PACK_LANES = 128
PACK_ROW_TILE = 8


def _pack_small(vals):
    flat = jnp.concatenate([vals[n].reshape(-1) for n in SMALL])
    per_tile = PACK_LANES * PACK_ROW_TILE
    return jnp.pad(flat, (0, -flat.shape[0] % per_tile)).reshape(-1, PACK_LANES)


def _unpack_small(pack, like):
    flat, out, pos = pack.reshape(-1), {}, 0
    for n in SMALL:
        out[n] = flat[pos:pos + like[n].size].reshape(like[n].shape)
        pos += like[n].size
    return out


def _dw_blocks(w):
    l, k, c = w.shape
    blk = jnp.transpose(w.reshape(l, k, N_DEV, c // N_DEV), (2, 0, 1, 3)).reshape(N_DEV, l * k, c // N_DEV)
    return jnp.pad(blk, ((0, 0), (0, -(l * k) % 8), (0, 0)))


def kernel(x, norm_mix, norm_ffn, final_norm, conv_w_pw1, conv_b_pw1, conv_w_dw, conv_b_dw, conv_ln_g, conv_ln_b, conv_w_pw2, conv_b_pw2, attn_w_qkv, attn_w_o, rel_bias, ffn_w_gate, ffn_w_up, ffn_w_down, loss_target, m_norm_mix, m_norm_ffn, m_final_norm, m_conv_w_pw1, m_conv_b_pw1, m_conv_w_dw, m_conv_b_dw, m_conv_ln_g, m_conv_ln_b, m_conv_w_pw2, m_conv_b_pw2, m_attn_w_qkv, m_attn_w_o, m_rel_bias, m_ffn_w_gate, m_ffn_w_up, m_ffn_w_down, v_norm_mix, v_norm_ffn, v_final_norm, v_conv_w_pw1, v_conv_b_pw1, v_conv_w_dw, v_conv_b_dw, v_conv_ln_g, v_conv_ln_b, v_conv_w_pw2, v_conv_b_pw2, v_attn_w_qkv, v_attn_w_o, v_rel_bias, v_ffn_w_gate, v_ffn_w_up, v_ffn_w_down):
    w = dict(norm_mix=norm_mix, norm_ffn=norm_ffn, final_norm=final_norm, conv_w_pw1=conv_w_pw1,
             conv_b_pw1=conv_b_pw1, conv_w_dw=conv_w_dw, conv_b_dw=conv_b_dw, conv_ln_g=conv_ln_g,
             conv_ln_b=conv_ln_b, conv_w_pw2=conv_w_pw2, conv_b_pw2=conv_b_pw2, attn_w_qkv=attn_w_qkv,
             attn_w_o=attn_w_o, rel_bias=rel_bias, ffn_w_gate=ffn_w_gate, ffn_w_up=ffn_w_up, ffn_w_down=ffn_w_down)
    m = dict(norm_mix=m_norm_mix, norm_ffn=m_norm_ffn, final_norm=m_final_norm, conv_w_pw1=m_conv_w_pw1,
             conv_b_pw1=m_conv_b_pw1, conv_w_dw=m_conv_w_dw, conv_b_dw=m_conv_b_dw, conv_ln_g=m_conv_ln_g,
             conv_ln_b=m_conv_ln_b, conv_w_pw2=m_conv_w_pw2, conv_b_pw2=m_conv_b_pw2, attn_w_qkv=m_attn_w_qkv,
             attn_w_o=m_attn_w_o, rel_bias=m_rel_bias, ffn_w_gate=m_ffn_w_gate, ffn_w_up=m_ffn_w_up,
             ffn_w_down=m_ffn_w_down)
    v = dict(norm_mix=v_norm_mix, norm_ffn=v_norm_ffn, final_norm=v_final_norm, conv_w_pw1=v_conv_w_pw1,
             conv_b_pw1=v_conv_b_pw1, conv_w_dw=v_conv_w_dw, conv_b_dw=v_conv_b_dw, conv_ln_g=v_conv_ln_g,
             conv_ln_b=v_conv_ln_b, conv_w_pw2=v_conv_w_pw2, conv_b_pw2=v_conv_b_pw2, attn_w_qkv=v_attn_w_qkv,
             attn_w_o=v_attn_w_o, rel_bias=v_rel_bias, ffn_w_gate=v_ffn_w_gate, ffn_w_up=v_ffn_w_up,
             ffn_w_down=v_ffn_w_down)

    me = 4 * lax.axis_index("x") + 2 * lax.axis_index("y") + lax.axis_index("c")
    depth = ffn_w_gate.shape[0]
    n_conv, _, cb = conv_w_dw.shape

    def sublayer(key, layer):
        if key in ("wgt", "wut", "wd"):
            return 2 * layer + 1
        return 4 * layer if key in ("w1t", "w2") else 4 * layer + 2

    def landing(block, own):
        land = lax.empty((N_DEV,) + block.shape, block.dtype)
        return lax.dynamic_update_slice(land, own[None], (me,) + (0,) * block.ndim)

    by_sub = {s: [] for s in range(2 * depth)}
    for name, key, cols in SHARDED:
        sw = (jnp.swapaxes(w[name], 1, 2) if cols else w[name]).astype(BF16)
        for layer in range(sw.shape[0]):
            by_sub[sublayer(key, layer)].append((key, layer, sw[layer]))
    likes = {s: jnp.zeros((sum(sh.size for _, _, sh in by_sub[s]) // 1024, 1024), BF16) for s in by_sub}
    dw_shard = jnp.pad(conv_w_dw.reshape(-1, cb), ((0, -(n_conv * CONV_WIDTH) % 8), (0, 0)))
    like_dw = jnp.zeros(dw_shard.shape, F32)
    stages = {}
    for s in range(2 * depth):
        for entry in by_sub[s]:
            stages.setdefault((s, 0 if entry[0] != "w2" else 1), []).append(entry)
    stage_order = sorted(stages)
    groups = [([sh for _, _, sh in stages[st]], [landing(sh, sh) for _, _, sh in stages[st]]) for st in stage_order]
    groups.insert(1, ([dw_shard], [landing(dw_shard, dw_shard)]))
    gather, _ = _exchange_start("gather_start", groups, scatter=False, carry=jnp.zeros((8, 128), F32))
    handle = dict(zip(stage_order, gather[:1] + gather[2:]))
    dw_filters = []

    def fetch(s, after, part=0):
        st = (s, part)
        like = jnp.zeros((sum(sh.size for _, _, sh in stages[st]) // 1024, 1024), BF16)
        pieces = [(handle[st], like)]
        if st == (0, 1):
            pieces.append((gather[1], like_dw))
        landed = _exchange_wait(f"gather_wait{s}_{part}", pieces, after)
        out = {key: g.reshape(g.shape[0] * g.shape[1], g.shape[2]) for (key, _, _), g in zip(stages[st], landed[0])}
        if st == (0, 1):
            dw_all = landed[1][0]
            full = jnp.transpose(dw_all[:, :n_conv * CONV_WIDTH].reshape(N_DEV, n_conv, CONV_WIDTH, cb), (1, 2, 0, 3))
            full = jnp.pad(full.reshape(n_conv, CONV_WIDTH, N_DEV * cb), ((0, 0), (0, CONV_HALO - CONV_WIDTH), (0, 0)))
            dw_filters.extend(full[layer] for layer in range(n_conv))
        if "w2" in out:
            out["wdw"] = dw_filters[s // 4]
        return out

    scatter, dw_grads, started = {}, {}, {}

    def emit(s, gd, carry):
        parts = [gd[key].reshape(N_DEV, -1, gd[key].shape[1]) for key, _, _ in by_sub[s]]
        if "wdw" in gd:
            dw_grads[s // 4] = gd["wdw"]
        if s == 0:
            parts.append(_dw_blocks(jnp.stack([dw_grads[layer] for layer in range(n_conv)])))
        lands = [landing(p[0], lax.dynamic_index_in_dim(p, me, 0, keepdims=False)) for p in parts]
        groups = [(parts[:len(by_sub[s])], lands[:len(by_sub[s])])]
        if s == 0:
            groups.append((parts[-1:], lands[-1:]))
        if s == 0:
            scatter[s], started[0] = _exchange_start(f"scatter_start{s}", groups, scatter=True,
                                                     carry=jnp.zeros((8, 128), F32))
            return carry
        scatter[s], carry = _exchange_start(f"scatter_start{s}", groups, scatter=True, carry=carry)
        return carry

    sm = {n: w[n] for n in SMALL}
    loss_cols, dx, gsm = _local_step(x[0], loss_target[0], sm, depth, fetch, emit)
    loss = lax.psum(jnp.sum(loss_cols), ("x", "y", "c"))

    grads, summed, landed_wd, delta, new_m, new_v = {}, {}, {}, {}, {}, {}

    def reduce_pieces(subs, landed):
        for s, recv in zip(subs, landed):
            for (key, layer, _), r in zip(by_sub[s], recv):
                if key == "wd":
                    landed_wd[layer] = r
                else:
                    summed[key, layer] = _sum8(f"sum_{key}{layer}", r)

    def update(names):
        for name in names:
            shape = w[name].shape
            if name == "ffn_w_down":
                res = _adamw_summed(f"adamw_{name}", [landed_wd[layer] for layer in range(shape[0])],
                                    *[t.reshape(-1, shape[-1]) for t in (w[name], m[name], v[name])])
                grads[name], delta[name], new_m[name], new_v[name] = (t.reshape(shape) for t in res)
                continue
            res = _adamw(f"adamw_{name}",
                         *[t.reshape(-1, shape[-1]) for t in (grads[name], w[name], m[name], v[name])])
            delta[name], new_m[name], new_v[name] = (t.reshape(shape) for t in res)

    def stacked(name, key, cols):
        g = jnp.stack([summed[key, layer] for layer in range(w[name].shape[0])])
        return jnp.swapaxes(g, 1, 2) if cols else g

    early = sorted((s for s in scatter if s != 0), reverse=True)
    reduce_pieces(early, _exchange_wait("scatter_wait_early", [(scatter[s][0], likes[s]) for s in early],
                                        started[0]))
    late_names = [name for name, key, _ in SHARDED if any(k == key for k, _, _ in by_sub[0])]
    for name, key, cols in SHARDED:
        if name not in late_names and key != "wd":
            grads[name] = stacked(name, key, cols)
    early_names = [name for name, _, _ in SHARDED if name not in late_names]
    update(early_names)
    pack = _pack_small(gsm)
    ((pack_all,),) = _exchange("gather_small_grads", [([pack], pack)], scatter=False)
    pack_sum = _sum8("sum_small", pack_all)
    grads.update(_unpack_small(pack_sum, sm))

    landed = _exchange_wait("scatter_wait_last", [(scatter[0][0], likes[0]), (scatter[0][1], like_dw)],
                            new_v[early_names[-1]])
    reduce_pieces([0], landed[:1])
    for name, key, cols in SHARDED:
        if name in late_names:
            grads[name] = stacked(name, key, cols)
    grads["conv_w_dw"] = _sum8("sum_wdw", landed[1][0])[:n_conv * CONV_WIDTH].reshape(conv_w_dw.shape)
    update(late_names + ["conv_w_dw"])
    res = _adamw("adamw_small", pack_sum, _pack_small(sm), _pack_small({n: m[n] for n in SMALL}),
                 _pack_small({n: v[n] for n in SMALL}))
    for dst, t in zip((delta, new_m, new_v), res):
        dst.update(_unpack_small(t, sm))

    outs = [loss, dx[None]]
    for d in (grads, delta, new_m, new_v):
        outs += [d[n] for n in ORDER]
    return tuple(outs)
```
